```python
import math
import jax, jax.numpy as jnp
from jax import lax
import numpy as np

D_MODEL = 1024
BATCH = 16
SEQ = 256
DEPTH = 1
DEC_BATCH = 2
DEC_SEQ = 2048
PAST_LEN = 256

GRID_W = 64
N_HEADS_A = 4
DH_A = 64
DV_A = 2 * DH_A
QK_A = N_HEADS_A * 2 * DH_A
W_A = N_HEADS_A * DV_A
ROPE_BASE = 10000.0
Q_BLOCK = 128
N_HEADS_H = 4
DK_H = 128
DV_H = 128
QK_H = N_HEADS_H * DK_H
W_H = N_HEADS_H * DV_H
CHUNK = 16
D_FF = 2816
MIX_IN = 2 * QK_A + W_A + 3 * QK_H + 2 * W_H + 2 * D_MODEL
ALPHA = (2 * DEPTH) ** 0.25
INIT_BETA = (8 * DEPTH) ** -0.25
LN_EPS = 1e-5
RMS_EPS = 1e-6

kernel_name = 'hybrid_diffattn_hgrn2_dit_step'


def _layer_norm(x, g, b):
    xf = x.astype(jnp.float32)
    mu = jnp.mean(xf, axis=-1, keepdims=True)
    var = jnp.mean(jnp.square(xf - mu), axis=-1, keepdims=True)
    return ((xf - mu) * lax.rsqrt(var + LN_EPS)).astype(x.dtype) * g + b


def _rms_norm(x, g):
    xf = x.astype(jnp.float32)
    return (xf * lax.rsqrt(jnp.mean(xf * xf, axis=-1, keepdims=True) + RMS_EPS)).astype(x.dtype) * g


def _swiglu(h, w_in, w_out):
    gate, up = jnp.split(h @ w_in, 2, axis=-1)
    return (jax.nn.silu(gate) * up) @ w_out


def _split_mix(proj):
    sizes = (QK_A, QK_A, W_A, QK_H, QK_H, QK_H, W_H, W_H, D_MODEL, D_MODEL)
    points, acc = [], 0
    for s in sizes[:-1]:
        acc += s
        points.append(acc)
    return jnp.split(proj, points, axis=-1)


def _axial_rope(n_tok, dtype):
    rows = n_tok // GRID_W
    row = jnp.repeat(jnp.arange(rows, dtype=jnp.float32), GRID_W)
    col = jnp.tile(jnp.arange(GRID_W, dtype=jnp.float32), rows)
    half = DH_A // 2
    inv = ROPE_BASE ** (-jnp.arange(0, half, 2, dtype=jnp.float32) / half)
    ar = row[:, None] * inv
    ac = col[:, None] * inv
    ang = jnp.concatenate([ar, ar, ac, ac], axis=-1)
    return (jnp.cos(ang).astype(dtype)[None, :, None, None, :],
            jnp.sin(ang).astype(dtype)[None, :, None, None, :])


def _rotate_half_axial(x):
    xr, xc = jnp.split(x, 2, axis=-1)
    def rh(p):
        a, b = jnp.split(p, 2, axis=-1)
        return jnp.concatenate([-b, a], axis=-1)
    return jnp.concatenate([rh(xr), rh(xc)], axis=-1)


def _apply_rope(x, cos, sin):
    return x * cos + _rotate_half_axial(x) * sin


def _diff_attention(q, k, v, lam):
    bsz, t = q.shape[:2]
    nb = t // Q_BLOCK
    qb = q.reshape(bsz, nb, Q_BLOCK, N_HEADS_A, 2, DH_A).swapaxes(0, 1)
    scale = DH_A ** -0.5
    def one_block(qi):
        s = jnp.einsum('bqhcd,bkhcd->bhcqk', qi, k).astype(jnp.float32) * scale
        p = jax.nn.softmax(s, axis=-1)
        a = p[:, :, 0] - lam * p[:, :, 1]
        return jnp.einsum('bhqk,bkhv->bqhv', a.astype(v.dtype), v)
    o = lax.map(one_block, qb)
    return o.swapaxes(0, 1).reshape(bsz, t, N_HEADS_A, DV_A)


def _hgrn2_chunk_scan(q, k, log_f, v, s0):
    bsz, t = q.shape[:2]
    nc = t // CHUNK
    def to_chunks(a):
        return a.reshape((bsz, nc, CHUNK) + a.shape[2:]).swapaxes(0, 1)
    causal = jnp.tril(jnp.ones((CHUNK, CHUNK), dtype=bool))[None, :, :, None, None]
    def step(s, inp):
        qc, kc, gc, vc = inp
        b = jnp.cumsum(gc, axis=1)
        rel = jnp.where(causal, b[:, :, None] - b[:, None, :], -jnp.inf)
        a = jnp.einsum('bthk,bshk,btshk->bhts', qc, kc, jnp.exp(rel))
        o = (jnp.einsum('bhts,bshv->bthv', a, vc)
             + jnp.einsum('bthk,bhkv->bthv', qc * jnp.exp(b), s))
        b_end = b[:, -1]
        s_new = (jnp.exp(b_end)[..., None] * s
                 + jnp.einsum('bshk,bshv->bhkv', kc * jnp.exp(b_end[:, None] - b), vc))
        return s_new, o
    s_fin, o = lax.scan(step, s0, (to_chunks(q), to_chunks(k), to_chunks(log_f), to_chunks(v)))
    return o.swapaxes(0, 1).reshape(bsz, t, N_HEADS_H, DV_H), s_fin


def _hgrn2(hq, hf_f, hf_b, hi, hg, lb, norm_g, s0):
    bsz, t, _ = hq.shape
    shp_k = (bsz, t, N_HEADS_H, DK_H)
    q = jax.nn.silu(hq).reshape(shp_k).astype(jnp.float32)
    v = hi.reshape(bsz, t, N_HEADS_H, DV_H).astype(jnp.float32)
    outs, states = [], []
    for d, (zf, reverse) in enumerate(((hf_f, False), (hf_b, True))):
        z = zf.reshape(shp_k).astype(jnp.float32)
        lbd = lb[d].reshape(N_HEADS_H, DK_H)
        log_f = jnp.logaddexp(jnp.log(lbd), jnp.log1p(-lbd) + jax.nn.log_sigmoid(z))
        kd = (1.0 - lbd) * jax.nn.sigmoid(-z)
        qd, vd = q, v
        if reverse:
            qd, kd, log_f, vd = (jnp.flip(a, axis=1) for a in (qd, kd, log_f, vd))
        o, s = _hgrn2_chunk_scan(qd, kd, log_f, vd, s0[:, d].astype(jnp.float32))
        if reverse:
            o = jnp.flip(o, axis=1)
        outs.append(o)
        states.append(s)
    o = _rms_norm(outs[0] + outs[1], norm_g.astype(jnp.float32))
    o = (o * jax.nn.silu(hg.reshape(bsz, t, N_HEADS_H, DV_H).astype(jnp.float32))).astype(hq.dtype)
    return o.reshape(bsz, t, W_H), jnp.stack(states, axis=1).astype(hq.dtype)


def _token_mixer(h, lw, lam_init, ctx):
    bsz, t, _ = h.shape
    aq, ak, av, hq, hf_f, hf_b, hi, hg, ga, gh = _split_mix(h @ lw['w_mix_in'])
    q = aq.reshape(bsz, t, N_HEADS_A, 2, DH_A)
    k = ak.reshape(bsz, t, N_HEADS_A, 2, DH_A)
    v = av.reshape(bsz, t, N_HEADS_A, DV_A)
    if ctx is None:
        k_all, v_all = k, v
        s0 = jnp.zeros((bsz, 2, N_HEADS_H, DK_H, DV_H), h.dtype)
    else:
        k_ctx, v_ctx, s0 = ctx
        cos, sin = _axial_rope(t, h.dtype)
        q = _apply_rope(q, cos, sin)
        k_lat = _apply_rope(k, cos, sin)
        k_all = jnp.concatenate([k_ctx.astype(h.dtype), k_lat], axis=1)
        v_all = jnp.concatenate([v_ctx.astype(h.dtype), v], axis=1)
    lam = (jnp.exp(jnp.sum(lw['lambda_q1'].astype(jnp.float32) * lw['lambda_k1'].astype(jnp.float32)))
           - jnp.exp(jnp.sum(lw['lambda_q2'].astype(jnp.float32) * lw['lambda_k2'].astype(jnp.float32)))
           + lam_init)
    o_a = _diff_attention(q, k_all, v_all, lam)
    o_a = (_rms_norm(o_a, lw['attn_subln_g']) * (1.0 - lam_init)).reshape(bsz, t, W_A)
    o_h, s_fin = _hgrn2(hq, hf_f, hf_b, hi, hg, lw['lb'], lw['hgrn_norm_g'], s0)
    merged = (jax.nn.sigmoid(ga) * (o_a @ lw['w_branch_a'])
              + jax.nn.sigmoid(gh) * (o_h @ lw['w_branch_h']))
    out = merged @ lw['w_mix_out']
    ctx_tensors = (k, v, s_fin) if ctx is None else None
    return out, ctx_tensors


def _layer(x, cond, lw, lam_init, ctx):
    nb = cond.shape[0]
    mod = (jax.nn.silu(cond) @ lw['w_ada'] + lw['b_ada']).reshape(nb, 3, 3, 1, D_MODEL)
    def modulate(hh, s):
        return hh * (1.0 + mod[:, s, 1]) + mod[:, s, 0]
    f1 = _swiglu(modulate(x, 0), lw['ffn1_w_in'], lw['ffn1_w_out'])
    x = _layer_norm(ALPHA * x + 0.5 * mod[:, 0, 2] * f1, lw['ln_g'][0], lw['ln_b'][0])
    mix, ctx_tensors = _token_mixer(modulate(x, 1), lw, lam_init, ctx)
    x = _layer_norm(ALPHA * x + mod[:, 1, 2] * mix, lw['ln_g'][1], lw['ln_b'][1])
    f2 = _swiglu(modulate(x, 2), lw['ffn2_w_in'], lw['ffn2_w_out'])
    x = _layer_norm(ALPHA * x + 0.5 * mod[:, 2, 2] * f2, lw['ln_g'][2], lw['ln_b'][2])
    return x, ctx_tensors


def setup_inputs(seed: int = 0) -> dict:
    key = jax.random.key(seed)
    ks = jax.random.split(key, 32)
    nrm = jax.random.normal
    f32 = jnp.float32
    return {
        'x_prompt': nrm(ks[0], (BATCH, SEQ, D_MODEL), f32),
        'x_sample': nrm(ks[1], (DEC_BATCH, DEC_SEQ, D_MODEL), f32),
        'cache_k': nrm(ks[2], (DEC_BATCH, DEPTH, PAST_LEN, N_HEADS_A, 2, DH_A), f32),
        'cache_v': nrm(ks[3], (DEC_BATCH, DEPTH, PAST_LEN, N_HEADS_A, DV_A), f32),
        'state_hgrn': nrm(ks[4], (DEC_BATCH, DEPTH, 2, N_HEADS_H, DK_H, DV_H), f32),
        'c': nrm(ks[5], (DEC_BATCH, D_MODEL), f32),
        'c_ctx': nrm(ks[6], (D_MODEL,), f32),
        'w_ada': nrm(ks[7], (DEPTH, D_MODEL, 9 * D_MODEL), f32) * D_MODEL ** -0.5,
        'b_ada': 0.02 * nrm(ks[8], (DEPTH, 9 * D_MODEL), f32),
        'ffn1_w_in': nrm(ks[9], (DEPTH, D_MODEL, 2 * D_FF), f32) * D_MODEL ** -0.5,
        'ffn1_w_out': nrm(ks[10], (DEPTH, D_FF, D_MODEL), f32) * (D_FF ** -0.5 * INIT_BETA),
        'w_mix_in': nrm(ks[11], (DEPTH, D_MODEL, MIX_IN), f32) * D_MODEL ** -0.5,
        'lambda_q1': 0.1 * nrm(ks[12], (DEPTH, DH_A), f32),
        'lambda_k1': 0.1 * nrm(ks[13], (DEPTH, DH_A), f32),
        'lambda_q2': 0.1 * nrm(ks[14], (DEPTH, DH_A), f32),
        'lambda_k2': 0.1 * nrm(ks[15], (DEPTH, DH_A), f32),
        'attn_subln_g': 1.0 + 0.02 * nrm(ks[16], (DEPTH, DV_A), f32),
        'hgrn_lb_logits': 0.5 * nrm(ks[17], (2, DEPTH + 1, QK_H), f32),
        'hgrn_norm_g': 1.0 + 0.02 * nrm(ks[18], (DEPTH, DV_H), f32),
        'w_branch_a': nrm(ks[19], (DEPTH, W_A, D_MODEL), f32) * W_A ** -0.5,
        'w_branch_h': nrm(ks[20], (DEPTH, W_H, D_MODEL), f32) * W_H ** -0.5,
        'w_mix_out': nrm(ks[21], (DEPTH, D_MODEL, D_MODEL), f32) * (D_MODEL ** -0.5 * INIT_BETA),
        'ffn2_w_in': nrm(ks[22], (DEPTH, D_MODEL, 2 * D_FF), f32) * D_MODEL ** -0.5,
        'ffn2_w_out': nrm(ks[23], (DEPTH, D_FF, D_MODEL), f32) * (D_FF ** -0.5 * INIT_BETA),
        'ln_g': 1.0 + 0.02 * nrm(ks[24], (DEPTH, 3, D_MODEL), f32),
        'ln_b': 0.02 * nrm(ks[25], (DEPTH, 3, D_MODEL), f32),
    }


def reference(x_prompt, x_sample, cache_k, cache_v, state_hgrn, c, c_ctx,
              w_ada, b_ada, ffn1_w_in, ffn1_w_out, w_mix_in,
              lambda_q1, lambda_k1, lambda_q2, lambda_k2, attn_subln_g,
              hgrn_lb_logits, hgrn_norm_g, w_branch_a, w_branch_h, w_mix_out,
              ffn2_w_in, ffn2_w_out, ln_g, ln_b):
    lb_all = jnp.cumsum(jax.nn.softmax(hgrn_lb_logits.astype(jnp.float32), axis=1), axis=1)
    layers = []
    for l in range(DEPTH):
        layers.append({
            'w_ada': w_ada[l], 'b_ada': b_ada[l],
            'ffn1_w_in': ffn1_w_in[l], 'ffn1_w_out': ffn1_w_out[l],
            'w_mix_in': w_mix_in[l],
            'lambda_q1': lambda_q1[l], 'lambda_k1': lambda_k1[l],
            'lambda_q2': lambda_q2[l], 'lambda_k2': lambda_k2[l],
            'attn_subln_g': attn_subln_g[l],
            'lb': lb_all[:, l], 'hgrn_norm_g': hgrn_norm_g[l],
            'w_branch_a': w_branch_a[l], 'w_branch_h': w_branch_h[l], 'w_mix_out': w_mix_out[l],
            'ffn2_w_in': ffn2_w_in[l], 'ffn2_w_out': ffn2_w_out[l],
            'ln_g': ln_g[l], 'ln_b': ln_b[l],
        })
    lam_inits = [0.8 - 0.6 * math.exp(-0.3 * l) for l in range(DEPTH)]

    xp = x_prompt
    cond_ctx = c_ctx[None, :]
    ks_list, vs_list, ss_list = [], [], []
    for l in range(DEPTH):
        xp, (k_l, v_l, s_l) = _layer(xp, cond_ctx, layers[l], lam_inits[l], None)
        ks_list.append(k_l)
        vs_list.append(v_l)
        ss_list.append(s_l)
    y_prompt = xp
    new_cache_k = jnp.stack(ks_list, axis=1)
    new_cache_v = jnp.stack(vs_list, axis=1)
    new_state_hgrn = jnp.stack(ss_list, axis=1)

    xs = x_sample
    for l in range(DEPTH):
        xs, _ = _layer(xs, c, layers[l], lam_inits[l],
                       (cache_k[:, l], cache_v[:, l], state_hgrn[:, l]))
    y_sample = xs
    return (y_prompt, y_sample, new_cache_k, new_cache_v, new_state_hgrn)
```

```python
import functools
import math

import jax
import jax.numpy as jnp
from jax import lax
from jax.experimental import pallas as pl
from jax.experimental.pallas import tpu as pltpu

D_MODEL = 1024
DEPTH = 1
GRID_W = 64
N_HEADS_A = 4
DH_A = 64
DV_A = 2 * DH_A
QK_A = N_HEADS_A * 2 * DH_A
W_A = N_HEADS_A * DV_A
ROPE_BASE = 10000.0
N_HEADS_H = 4
DK_H = 128
DV_H = 128
QK_H = N_HEADS_H * DK_H
W_H = N_HEADS_H * DV_H
D_FF = 2816
MIX_IN = 2 * QK_A + W_A + 3 * QK_H + 2 * W_H + 2 * D_MODEL
ALPHA = (2 * DEPTH) ** 0.25
LN_EPS = 1e-5
RMS_EPS = 1e-6

F32 = jnp.float32
BF16 = jnp.bfloat16

LANES = 128
VMEM_LIMIT = 56 * 1024 * 1024

FFN_TM = 1024
FFN_TF = 256
MIX_TM = 512
MIX_TN = 512
MERGE_TM = 512
ATT_TQ = 256
HGRN_C = 64
HGRN_SAFE_DECAY = 60.0
MOD_ROWS = 8


def _params(sem):
    return pltpu.CompilerParams(dimension_semantics=sem, vmem_limit_bytes=VMEM_LIMIT)


def _nt_dot(a, b):
    return lax.dot_general(a, b, (((1,), (1,)), ((), ())), preferred_element_type=F32)


def _tn_dot(a, b):
    return lax.dot_general(a, b, (((0,), (0,)), ((), ())), preferred_element_type=F32)


def _layer_norm(y, g, b):
    mu = jnp.mean(y, axis=-1, keepdims=True)
    yc = y - mu
    var = jnp.mean(yc * yc, axis=-1, keepdims=True)
    return yc * lax.rsqrt(var + LN_EPS) * g + b


def _mod_kernel(c_ref, w_ref, b_ref, o_ref):
    c = c_ref[...]
    a = (c * jax.nn.sigmoid(c)).astype(BF16)
    o_ref[...] = jnp.dot(a, w_ref[...].astype(BF16), preferred_element_type=F32) + b_ref[...]


def _modulation(cond, w_ada, b_ada):
    n_out = w_ada.shape[1]
    tn = D_MODEL
    out = pl.pallas_call(
        _mod_kernel,
        grid=(n_out // tn,),
        in_specs=[
            pl.BlockSpec((MOD_ROWS, D_MODEL), lambda n: (0, 0)),
            pl.BlockSpec((D_MODEL, tn), lambda n: (0, n)),
            pl.BlockSpec((1, tn), lambda n: (0, n)),
        ],
        out_specs=pl.BlockSpec((MOD_ROWS, tn), lambda n: (0, n)),
        out_shape=jax.ShapeDtypeStruct((MOD_ROWS, n_out), F32),
        compiler_params=_params(("arbitrary",)),
    )(cond, w_ada, b_ada.reshape(1, n_out))
    return out.reshape(MOD_ROWS, 9, D_MODEL)


def _ffn_kernel(x_ref, mod_ref, wg_ref, wu_ref, wo_ref, lng_ref, lnb_ref, *rest, sub, sub_next):
    if sub_next is None:
        xo_ref, h_scr, acc_scr = rest
        ho_ref = None
    else:
        xo_ref, ho_ref, h_scr, acc_scr = rest
    j = pl.program_id(1)

    @pl.when(j == 0)
    def _():
        shift = mod_ref[0, 3 * sub:3 * sub + 1, :]
        scale = mod_ref[0, 3 * sub + 1:3 * sub + 2, :]
        h_scr[...] = (x_ref[...] * (1.0 + scale) + shift).astype(BF16)
        acc_scr[...] = jnp.zeros_like(acc_scr)

    h = h_scr[...]
    gate = jnp.dot(h, wg_ref[...].astype(BF16), preferred_element_type=F32)
    up = jnp.dot(h, wu_ref[...].astype(BF16), preferred_element_type=F32)
    act = (gate * jax.nn.sigmoid(gate) * up).astype(BF16)
    acc_scr[...] += jnp.dot(act, wo_ref[...].astype(BF16), preferred_element_type=F32)

    @pl.when(j == pl.num_programs(1) - 1)
    def _():
        g = mod_ref[0, 3 * sub + 2:3 * sub + 3, :]
        y = ALPHA * x_ref[...] + 0.5 * g * acc_scr[...]
        xn = _layer_norm(y, lng_ref[sub:sub + 1, :], lnb_ref[sub:sub + 1, :])
        xo_ref[...] = xn
        if ho_ref is not None:
            shift = mod_ref[0, 3 * sub_next:3 * sub_next + 1, :]
            scale = mod_ref[0, 3 * sub_next + 1:3 * sub_next + 2, :]
            ho_ref[...] = (xn * (1.0 + scale) + shift).astype(BF16)


def _ffn(x, mod, cond_of_tile, w_in, w_out, ln_g, ln_b, sub, sub_next):
    m = x.shape[0]
    nj = D_FF // FFN_TF
    row = lambda i, j: (i, 0)
    out_shape = [jax.ShapeDtypeStruct((m, D_MODEL), F32)]
    out_specs = [pl.BlockSpec((FFN_TM, D_MODEL), row)]
    if sub_next is not None:
        out_shape.append(jax.ShapeDtypeStruct((m, D_MODEL), BF16))
        out_specs.append(pl.BlockSpec((FFN_TM, D_MODEL), row))
    return pl.pallas_call(
        functools.partial(_ffn_kernel, sub=sub, sub_next=sub_next),
        grid=(m // FFN_TM, nj),
        in_specs=[
            pl.BlockSpec((FFN_TM, D_MODEL), row),
            pl.BlockSpec((1, 9, D_MODEL), lambda i, j: (cond_of_tile(i, FFN_TM), 0, 0)),
            pl.BlockSpec((D_MODEL, FFN_TF), lambda i, j: (0, j)),
            pl.BlockSpec((D_MODEL, FFN_TF), lambda i, j: (0, nj + j)),
            pl.BlockSpec((FFN_TF, D_MODEL), lambda i, j: (j, 0)),
            pl.BlockSpec((3, D_MODEL), lambda i, j: (0, 0)),
            pl.BlockSpec((3, D_MODEL), lambda i, j: (0, 0)),
        ],
        out_specs=out_specs,
        out_shape=out_shape,
        scratch_shapes=[pltpu.VMEM((FFN_TM, D_MODEL), BF16), pltpu.VMEM((FFN_TM, D_MODEL), F32)],
        compiler_params=_params(("parallel", "arbitrary")),
    )(x, mod, w_in, w_in, w_out, ln_g, ln_b)


def _rope(x, cos, sin):
    lane = lax.broadcasted_iota(jnp.int32, (1, LANES), 1)
    first_half = (lane % 32) < 16
    outs = []
    for hb in range(QK_A // LANES):
        xb = x[:, hb * LANES:(hb + 1) * LANES]
        ahead = pltpu.roll(xb, LANES - 16, 1)
        behind = pltpu.roll(xb, 16, 1)
        rot = jnp.where(first_half, -ahead, behind)
        outs.append(xb * cos + rot * sin)
    return jnp.concatenate(outs, axis=-1)


def _mixin_kernel(*refs, rope):
    if rope:
        h_ref, w_ref, cos_ref, sin_ref = refs[:4]
        outs = refs[4:]
    else:
        h_ref, w_ref = refs[:2]
        outs = refs[2:]
    q_ref, k_ref, v_ref, hq_ref, hff_ref, hfb_ref, hi_ref, hg_ref, sa_ref, sh_ref = outs
    n = pl.program_id(1)
    p = jnp.dot(h_ref[...], w_ref[...].astype(BF16), preferred_element_type=F32)

    def rot(x):
        return _rope(x, cos_ref[...], sin_ref[...]) if rope else x

    @pl.when(n == 0)
    def _():
        q_ref[...] = (rot(p) * (DH_A ** -0.5)).astype(q_ref.dtype)

    @pl.when(n == 1)
    def _():
        k_ref[...] = rot(p).astype(k_ref.dtype)

    plain = {2: v_ref, 3: hq_ref, 4: hff_ref, 5: hfb_ref, 6: hi_ref, 7: hg_ref}
    for idx, ref in plain.items():
        @pl.when(n == idx)
        def _(ref=ref):
            ref[...] = p.astype(ref.dtype)

    gates = {8: (sa_ref, 0), 9: (sa_ref, 1), 10: (sh_ref, 0), 11: (sh_ref, 1)}
    for idx, (ref, half) in gates.items():
        @pl.when(n == idx)
        def _(ref=ref, half=half):
            ref[:, half * MIX_TN:(half + 1) * MIX_TN] = jax.nn.sigmoid(p).astype(ref.dtype)


def _mix_in(h, w, rope_tables, kv_dtype):
    m = h.shape[0]
    row = lambda i, n: (i, 0)
    rope = rope_tables is not None
    in_specs = [
        pl.BlockSpec((MIX_TM, D_MODEL), row),
        pl.BlockSpec((D_MODEL, MIX_TN), lambda i, n: (0, n)),
    ]
    args = [h, w]
    if rope:
        n_rope_tiles = rope_tables[0].shape[0] // MIX_TM
        for t in rope_tables:
            in_specs.append(pl.BlockSpec((MIX_TM, DV_A), lambda i, n: (i % n_rope_tiles, 0)))
            args.append(t)
    widths = [(QK_A, BF16), (QK_A, kv_dtype), (W_A, kv_dtype), (QK_H, F32), (QK_H, F32), (QK_H, F32),
              (W_H, F32), (W_H, F32), (D_MODEL, BF16), (D_MODEL, BF16)]
    return pl.pallas_call(
        functools.partial(_mixin_kernel, rope=rope),
        grid=(m // MIX_TM, MIX_IN // MIX_TN),
        in_specs=in_specs,
        out_specs=[pl.BlockSpec((MIX_TM, wd), row) for wd, _ in widths],
        out_shape=[jax.ShapeDtypeStruct((m, wd), dt) for wd, dt in widths],
        compiler_params=_params(("parallel", "arbitrary")),
    )(*args)


def _attn_kernel(*refs, lam_init, has_ctx):
    if has_ctx:
        lam_ref, g_ref, q_ref, k_ref, v_ref, kc_ref, vc_ref, o_ref = refs
    else:
        lam_ref, g_ref, q_ref, k_ref, v_ref, o_ref = refs
    lp = lam_ref[...]
    lam = (jnp.exp(jnp.sum(lp[0:1] * lp[1:2], axis=-1, keepdims=True))
           - jnp.exp(jnp.sum(lp[2:3] * lp[3:4], axis=-1, keepdims=True)) + lam_init)

    q = q_ref[0]
    lane = lax.broadcasted_iota(jnp.int32, (1, DV_A), 1)
    comp0 = lane < DH_A
    zero = jnp.zeros_like(q)
    qs = (jnp.where(comp0, q, zero), jnp.where(comp0, zero, q))
    keys = [k_ref[0].astype(BF16)]
    vals = [v_ref[0].astype(BF16)]
    if has_ctx:
        keys.append(kc_ref[0].astype(BF16))
        vals.append(vc_ref[0].astype(BF16))

    probs = []
    for qc in qs:
        ss = [_nt_dot(qc, kk) for kk in keys]
        mx = functools.reduce(jnp.maximum, [jnp.max(s, axis=-1, keepdims=True) for s in ss])
        es = [jnp.exp(s - mx) for s in ss]
        den = functools.reduce(jnp.add, [jnp.sum(e, axis=-1, keepdims=True) for e in es])
        probs.append((es, 1.0 / den))
    (e0, r0), (e1, r1) = probs
    r1 = lam * r1
    o = None
    for a, b, vv in zip(e0, e1, vals):
        part = jnp.dot((a * r0 - b * r1).astype(BF16), vv, preferred_element_type=F32)
        o = part if o is None else o + part
    o = o * lax.rsqrt(jnp.mean(o * o, axis=-1, keepdims=True) + RMS_EPS) * g_ref[...]
    o_ref[0] = (o * (1.0 - lam_init)).astype(o_ref.dtype)


def _attention(q, k, v, ctx_kv, lam_params, subln_g, lam_init):
    bsz, t, _ = q.shape
    tq = min(ATT_TQ, t)
    head_q = lambda b, h, i: (b, i, h)
    head_all = lambda b, h, i: (b, 0, h)
    const = lambda b, h, i: (0, 0)
    in_specs = [
        pl.BlockSpec((4, DH_A), const),
        pl.BlockSpec((1, DV_A), const),
        pl.BlockSpec((1, tq, DV_A), head_q),
        pl.BlockSpec((1, t, DV_A), head_all),
        pl.BlockSpec((1, t, DV_A), head_all),
    ]
    args = [lam_params, subln_g, q, k, v]
    if ctx_kv is not None:
        p = ctx_kv[0].shape[1]
        in_specs += [pl.BlockSpec((1, p, DV_A), head_all), pl.BlockSpec((1, p, DV_A), head_all)]
        args += list(ctx_kv)
    return pl.pallas_call(
        functools.partial(_attn_kernel, lam_init=lam_init, has_ctx=ctx_kv is not None),
        grid=(bsz, N_HEADS_A, t // tq),
        in_specs=in_specs,
        out_specs=pl.BlockSpec((1, tq, DV_A), head_q),
        out_shape=jax.ShapeDtypeStruct((bsz, t, W_A), BF16),
        compiler_params=_params(("parallel", "parallel", "arbitrary")),
    )(*args)


def _chunk_scan(g, reverse):
    c = g.shape[0]
    row = lax.broadcasted_iota(jnp.int32, (c, 1), 0)
    b = g
    sh = 1
    while sh < c:
        if reverse:
            b = b + jnp.where(row < c - sh, pltpu.roll(b, c - sh, 0), 0.0)
        else:
            b = b + jnp.where(row >= sh, pltpu.roll(b, sh, 0), 0.0)
        sh *= 2
    return b


def _hgrn_kernel(*refs, seq_len, has_s0, want_state):
    refs = list(refs)
    hq_ref, hff_ref, hfb_ref, hi_ref, hg_ref, lbl_ref, ng_ref = refs[:7]
    refs = refs[7:]
    s0_ref = refs.pop(0) if has_s0 else None
    o_ref = refs.pop(0)
    sfin_ref = refs.pop(0) if want_state else None
    od_scr, st_scr, b_scr, q_scr, kd_scr, v_scr, oi_scr = refs
    c = HGRN_C
    nc = seq_len // c

    for d in range(2):
        if has_s0:
            st_scr[d] = s0_ref[0, d, 0].T
        else:
            st_scr[d] = jnp.zeros((DV_H, DK_H), F32)

    row = lax.broadcasted_iota(jnp.int32, (c, 1), 0)
    col = lax.broadcasted_iota(jnp.int32, (1, c), 1)

    def lower_bound(d):
        logits = lbl_ref[d]
        mx = jnp.max(logits, axis=0, keepdims=True)
        e = jnp.exp(logits - mx)
        return e[0:1] / jnp.sum(e, axis=0, keepdims=True)

    lbs = [lower_bound(0), lower_bound(1)]

    def chunk(d, r0):
        reverse = d == 1
        rows = pl.ds(r0, c)
        z = (hfb_ref if reverse else hff_ref)[0, rows, :]
        hq = hq_ref[0, rows, :]
        v = hi_ref[0, rows, :]
        lb = lbs[d]
        e = jnp.exp(-jnp.abs(z))
        r = 1.0 / (1.0 + e)
        er = e * r
        pos = z >= 0.0
        sig = jnp.where(pos, r, er)
        nsig = jnp.where(pos, er, r)
        g = jnp.log(lb + (1.0 - lb) * sig)
        kd = (1.0 - lb) * nsig
        q = hq * jax.nn.sigmoid(hq)
        b = _chunk_scan(g, reverse)
        b_end = b[0:1] if reverse else b[c - 1:c]
        safe = jnp.max(-b_end) <= HGRN_SAFE_DECAY
        vb = v.astype(BF16)
        qd = (q * jnp.exp(b)).astype(BF16)
        causal = (row <= col) if reverse else (row >= col)

        @pl.when(safe)
        def _():
            ku = (kd * jnp.exp(-b)).astype(BF16)
            a = jnp.where(causal, _nt_dot(qd, ku), 0.0)
            oi_scr[...] = jnp.dot(a.astype(BF16), vb, preferred_element_type=F32)

        @pl.when(jnp.logical_not(safe))
        def _():
            b_scr[...] = b
            q_scr[...] = q
            kd_scr[...] = kd
            v_scr[...] = v

            def one_row(t, carry):
                bt = b_scr[pl.ds(t, 1), :]
                qt = q_scr[pl.ds(t, 1), :]
                seen = (row >= t) if reverse else (row <= t)
                w = jnp.exp(jnp.where(seen, bt - b_scr[...], -jnp.inf))
                p = jnp.sum(qt * kd_scr[...] * w, axis=-1, keepdims=True)
                oi_scr[pl.ds(t, 1), :] = jnp.sum(p * v_scr[...], axis=0, keepdims=True)
                return carry

            lax.fori_loop(0, c, one_row, 0)

        st = st_scr[d]
        od_scr[d, rows, :] = oi_scr[...] + _nt_dot(qd, st.astype(BF16))
        k_end = (kd * jnp.exp(b_end - b)).astype(BF16)
        st_scr[d] = jnp.exp(b_end) * st + _tn_dot(vb, k_end)

    def step(ci, carry):
        chunk(0, pl.multiple_of(ci * c, c))
        chunk(1, pl.multiple_of((nc - 1 - ci) * c, c))
        return carry

    lax.fori_loop(0, nc, step, 0)

    o = od_scr[0] + od_scr[1]
    o = o * lax.rsqrt(jnp.mean(o * o, axis=-1, keepdims=True) + RMS_EPS) * ng_ref[...]
    hg = hg_ref[0]
    o_ref[0] = (o * (hg * jax.nn.sigmoid(hg))).astype(o_ref.dtype)
    if want_state:
        for d in range(2):
            sfin_ref[0, d, 0] = st_scr[d].T


def _hgrn(hq, hff, hfb, hi, hg, lb_logits, norm_g, s0, want_state):
    bsz, t, _ = hq.shape
    head = lambda b, h: (b, 0, h)
    in_specs = [pl.BlockSpec((1, t, DK_H), head) for _ in range(5)]
    in_specs += [
        pl.BlockSpec((2, DEPTH + 1, DK_H), lambda b, h: (0, 0, h)),
        pl.BlockSpec((1, DV_H), lambda b, h: (0, 0)),
    ]
    args = [hq, hff, hfb, hi, hg, lb_logits, norm_g]
    state_spec = pl.BlockSpec((1, 2, 1, DK_H, DV_H), lambda b, h: (b, 0, h, 0, 0))
    if s0 is not None:
        in_specs.append(state_spec)
        args.append(s0)
    out_specs = [pl.BlockSpec((1, t, DV_H), head)]
    out_shape = [jax.ShapeDtypeStruct((bsz, t, W_H), BF16)]
    if want_state:
        out_specs.append(state_spec)
        out_shape.append(jax.ShapeDtypeStruct((bsz, 2, N_HEADS_H, DK_H, DV_H), F32))
    chunk_buf = pltpu.VMEM((HGRN_C, DK_H), F32)
    outs = pl.pallas_call(
        functools.partial(_hgrn_kernel, seq_len=t, has_s0=s0 is not None, want_state=want_state),
        grid=(bsz, N_HEADS_H),
        in_specs=in_specs,
        out_specs=out_specs,
        out_shape=out_shape,
        scratch_shapes=[
            pltpu.VMEM((2, t, DV_H), F32),
            pltpu.VMEM((2, DV_H, DK_H), F32),
            chunk_buf, chunk_buf, chunk_buf, chunk_buf, chunk_buf,
        ],
        compiler_params=_params(("parallel", "parallel")),
    )(*args)
    return outs if want_state else (outs[0], None)


def _merge_kernel(oa_ref, oh_ref, sa_ref, sh_ref, x_ref, mod_ref, wa_ref, wh_ref, wo_ref, lng_ref, lnb_ref,
                  xo_ref, ho_ref, wa_scr, wh_scr, wo_scr):
    @pl.when(pl.program_id(0) == 0)
    def _():
        wa_scr[...] = wa_ref[...].astype(BF16)
        wh_scr[...] = wh_ref[...].astype(BF16)
        wo_scr[...] = wo_ref[...].astype(BF16)

    ba = jnp.dot(oa_ref[...], wa_scr[...], preferred_element_type=F32)
    bh = jnp.dot(oh_ref[...], wh_scr[...], preferred_element_type=F32)
    merged = sa_ref[...].astype(F32) * ba + sh_ref[...].astype(F32) * bh
    mix = jnp.dot(merged.astype(BF16), wo_scr[...], preferred_element_type=F32)
    y = ALPHA * x_ref[...] + mod_ref[0, 5:6, :] * mix
    xn = _layer_norm(y, lng_ref[1:2, :], lnb_ref[1:2, :])
    xo_ref[...] = xn
    ho_ref[...] = (xn * (1.0 + mod_ref[0, 7:8, :]) + mod_ref[0, 6:7, :]).astype(BF16)


def _merge(o_a, o_h, sig_a, sig_h, x, mod, cond_of_tile, w_a, w_h, w_o, ln_g, ln_b):
    m = x.shape[0]
    row = lambda i: (i, 0)
    const = lambda i: (0, 0)
    return pl.pallas_call(
        _merge_kernel,
        grid=(m // MERGE_TM,),
        in_specs=[
            pl.BlockSpec((MERGE_TM, W_A), row),
            pl.BlockSpec((MERGE_TM, W_H), row),
            pl.BlockSpec((MERGE_TM, D_MODEL), row),
            pl.BlockSpec((MERGE_TM, D_MODEL), row),
            pl.BlockSpec((MERGE_TM, D_MODEL), row),
            pl.BlockSpec((1, 9, D_MODEL), lambda i: (cond_of_tile(i, MERGE_TM), 0, 0)),
            pl.BlockSpec((W_A, D_MODEL), const),
            pl.BlockSpec((W_H, D_MODEL), const),
            pl.BlockSpec((D_MODEL, D_MODEL), const),
            pl.BlockSpec((3, D_MODEL), const),
            pl.BlockSpec((3, D_MODEL), const),
        ],
        out_specs=[pl.BlockSpec((MERGE_TM, D_MODEL), row), pl.BlockSpec((MERGE_TM, D_MODEL), row)],
        out_shape=[jax.ShapeDtypeStruct((m, D_MODEL), F32), jax.ShapeDtypeStruct((m, D_MODEL), BF16)],
        scratch_shapes=[pltpu.VMEM((W_A, D_MODEL), BF16), pltpu.VMEM((W_H, D_MODEL), BF16),
                        pltpu.VMEM((D_MODEL, D_MODEL), BF16)],
        compiler_params=_params(("arbitrary",)),
    )(o_a, o_h, sig_a, sig_h, x, mod, w_a, w_h, w_o, ln_g, ln_b)


def _rope_tables(n_tok):
    rows = n_tok // GRID_W
    row = jnp.repeat(jnp.arange(rows, dtype=F32), GRID_W)
    col = jnp.tile(jnp.arange(GRID_W, dtype=F32), rows)
    half = DH_A // 2
    inv = ROPE_BASE ** (-jnp.arange(0, half, 2, dtype=F32) / half)
    ar = row[:, None] * inv
    ac = col[:, None] * inv
    ang = jnp.concatenate([ar, ar, ac, ac] * 2, axis=-1)
    return jnp.cos(ang), jnp.sin(ang)


def _layer_pass(x, mod, cond_of_tile, wts, lam_init, bsz, ctx):
    m = x.shape[0]
    t = m // bsz
    latent = ctx is not None
    x1, h1 = _ffn(x, mod, cond_of_tile, wts['ffn1_w_in'], wts['ffn1_w_out'], wts['ln_g'], wts['ln_b'], 0, 1)
    q, k, v, hq, hff, hfb, hi, hg, sig_a, sig_h = _mix_in(
        h1, wts['w_mix_in'], _rope_tables(t) if latent else None, BF16 if latent else F32)
    b3 = lambda a: a.reshape(bsz, t, a.shape[-1])
    ctx_kv = (ctx[0], ctx[1]) if latent else None
    o_a = _attention(b3(q), b3(k), b3(v), ctx_kv, wts['lam_params'], wts['attn_subln_g'], lam_init)
    o_h, s_fin = _hgrn(b3(hq), b3(hff), b3(hfb), b3(hi), b3(hg), wts['lb_logits'], wts['hgrn_norm_g'],
                       ctx[2] if latent else None, not latent)
    x2, h2 = _merge(o_a.reshape(m, W_A), o_h.reshape(m, W_H), sig_a, sig_h, x1, mod, cond_of_tile,
                    wts['w_branch_a'], wts['w_branch_h'], wts['w_mix_out'], wts['ln_g'], wts['ln_b'])
    (y,) = _ffn(x2, mod, cond_of_tile, wts['ffn2_w_in'], wts['ffn2_w_out'], wts['ln_g'], wts['ln_b'], 2, None)
    return y, (k, v, s_fin)


def kernel(x_prompt, x_sample, cache_k, cache_v, state_hgrn, c, c_ctx, w_ada, b_ada, ffn1_w_in, ffn1_w_out,
           w_mix_in, lambda_q1, lambda_k1, lambda_q2, lambda_k2, attn_subln_g, hgrn_lb_logits, hgrn_norm_g,
           w_branch_a, w_branch_h, w_mix_out, ffn2_w_in, ffn2_w_out, ln_g, ln_b):
    batch, seq, _ = x_prompt.shape
    dec_batch, dec_seq, _ = x_sample.shape
    past = cache_k.shape[2]
    assert DEPTH == 1 and dec_batch + 1 <= MOD_ROWS
    lam_init = 0.8 - 0.6 * math.exp(-0.3 * 0)

    cond = jnp.concatenate([c_ctx[None, :], c, jnp.zeros((MOD_ROWS - 1 - dec_batch, D_MODEL), F32)], axis=0)
    mod = _modulation(cond, w_ada[0], b_ada[0])

    wts = {
        'ffn1_w_in': ffn1_w_in[0], 'ffn1_w_out': ffn1_w_out[0], 'w_mix_in': w_mix_in[0],
        'lam_params': jnp.concatenate([lambda_q1, lambda_k1, lambda_q2, lambda_k2], axis=0),
        'attn_subln_g': attn_subln_g, 'lb_logits': hgrn_lb_logits, 'hgrn_norm_g': hgrn_norm_g,
        'w_branch_a': w_branch_a[0], 'w_branch_h': w_branch_h[0], 'w_mix_out': w_mix_out[0],
        'ffn2_w_in': ffn2_w_in[0], 'ffn2_w_out': ffn2_w_out[0], 'ln_g': ln_g[0], 'ln_b': ln_b[0],
    }

    y_p, (k_c, v_c, s_c) = _layer_pass(x_prompt.reshape(batch * seq, D_MODEL), mod, lambda i, tm: 0,
                                       wts, lam_init, batch, None)
    ctx = (cache_k[:, 0].reshape(dec_batch, past, QK_A), cache_v[:, 0].reshape(dec_batch, past, W_A),
           state_hgrn[:, 0])
    y_s, _ = _layer_pass(x_sample.reshape(dec_batch * dec_seq, D_MODEL), mod,
                         lambda i, tm: 1 + (i * tm) // dec_seq, wts, lam_init, dec_batch, ctx)

    return (y_p.reshape(batch, seq, D_MODEL),
            y_s.reshape(dec_batch, dec_seq, D_MODEL),
            k_c.reshape(batch, DEPTH, seq, N_HEADS_A, 2, DH_A),
            v_c.reshape(batch, DEPTH, seq, N_HEADS_A, DV_A),
            s_c.reshape(batch, DEPTH, 2, N_HEADS_H, DK_H, DV_H))
```

```python
import functools
import math

import jax
import jax.numpy as jnp
from jax import lax
from jax.experimental import pallas as pl
from jax.experimental.pallas import tpu as pltpu

D_MODEL = 1024
DEPTH = 1
GRID_W = 64
N_HEADS_A = 4
DH_A = 64
DV_A = 2 * DH_A
QK_A = N_HEADS_A * 2 * DH_A
W_A = N_HEADS_A * DV_A
ROPE_BASE = 10000.0
N_HEADS_H = 4
DK_H = 128
DV_H = 128
QK_H = N_HEADS_H * DK_H
W_H = N_HEADS_H * DV_H
D_FF = 2816
MIX_IN = 2 * QK_A + W_A + 3 * QK_H + 2 * W_H + 2 * D_MODEL
ALPHA = (2 * DEPTH) ** 0.25
LN_EPS = 1e-5
RMS_EPS = 1e-6

F32 = jnp.float32
BF16 = jnp.bfloat16

LANES = 128
VMEM_LIMIT = 56 * 1024 * 1024

FFN_TM = 1024
FFN_TF = 256
MIX_TM = 1024
MIX_TN = 512
MERGE_TM = 512
ATT_TQ = 256
HGRN_C = 64
HGRN_ROWS = 256
HGRN_SAFE_DECAY = 60.0
MOD_ROWS = 8

W_Q, W_K, W_V, W_HQ, W_HFF, W_HFB, W_HI, W_HG, W_GA0, W_GA1, W_GH0, W_GH1 = range(12)
P32_BLOCKS = (W_HQ, W_HFF, W_HFB, W_HI, W_HG)
P16_CTX_BLOCKS = (W_GA0, W_GA1, W_GH0, W_GH1, W_Q)
P16_LAT_BLOCKS = (W_GA0, W_GA1, W_GH0, W_GH1, W_Q, W_K, W_V)


def _params(sem):
    return pltpu.CompilerParams(dimension_semantics=sem, vmem_limit_bytes=VMEM_LIMIT)


def _nt_dot(a, b):
    return lax.dot_general(a, b, (((1,), (1,)), ((), ())), preferred_element_type=F32)


def _tn_dot(a, b):
    return lax.dot_general(a, b, (((0,), (0,)), ((), ())), preferred_element_type=F32)


def _layer_norm(y, g, b):
    mu = jnp.mean(y, axis=-1, keepdims=True)
    yc = y - mu
    var = jnp.mean(yc * yc, axis=-1, keepdims=True)
    return yc * lax.rsqrt(var + LN_EPS) * g + b


def _mod_kernel(c_ref, w_ref, b_ref, o_ref):
    c = c_ref[...]
    a = (c * jax.nn.sigmoid(c)).astype(BF16)
    o_ref[...] = jnp.dot(a, w_ref[...].astype(BF16), preferred_element_type=F32) + b_ref[...]


def _modulation(cond, w_ada, b_ada):
    n_out = w_ada.shape[1]
    tn = D_MODEL
    out = pl.pallas_call(
        _mod_kernel,
        grid=(n_out // tn,),
        in_specs=[
            pl.BlockSpec((MOD_ROWS, D_MODEL), lambda n: (0, 0)),
            pl.BlockSpec((D_MODEL, tn), lambda n: (0, n)),
            pl.BlockSpec((1, tn), lambda n: (0, n)),
        ],
        out_specs=pl.BlockSpec((MOD_ROWS, tn), lambda n: (0, n)),
        out_shape=jax.ShapeDtypeStruct((MOD_ROWS, n_out), F32),
        compiler_params=_params(("arbitrary",)),
        name="modulation",
    )(cond, w_ada, b_ada.reshape(1, n_out))
    return out.reshape(MOD_ROWS, 9, D_MODEL)


def _ffn_kernel(x_ref, mod_ref, wg_ref, wu_ref, wo_ref, lng_ref, lnb_ref, *rest, sub, sub_next):
    if sub_next is None:
        xo_ref, h_scr, acc_scr = rest
        ho_ref = None
    else:
        xo_ref, ho_ref, h_scr, acc_scr = rest
    j = pl.program_id(1)

    @pl.when(j == 0)
    def _():
        shift = mod_ref[0, 3 * sub:3 * sub + 1, :]
        scale = mod_ref[0, 3 * sub + 1:3 * sub + 2, :]
        h_scr[...] = (x_ref[...] * (1.0 + scale) + shift).astype(BF16)
        acc_scr[...] = jnp.zeros_like(acc_scr)

    h = h_scr[...]
    gate = jnp.dot(h, wg_ref[...].astype(BF16), preferred_element_type=F32)
    up = jnp.dot(h, wu_ref[...].astype(BF16), preferred_element_type=F32)
    act = (gate * jax.nn.sigmoid(gate) * up).astype(BF16)
    acc_scr[...] += jnp.dot(act, wo_ref[...].astype(BF16), preferred_element_type=F32)

    @pl.when(j == pl.num_programs(1) - 1)
    def _():
        g = mod_ref[0, 3 * sub + 2:3 * sub + 3, :]
        y = ALPHA * x_ref[...] + 0.5 * g * acc_scr[...]
        xn = _layer_norm(y, lng_ref[sub:sub + 1, :], lnb_ref[sub:sub + 1, :])
        xo_ref[...] = xn
        if ho_ref is not None:
            shift = mod_ref[0, 3 * sub_next:3 * sub_next + 1, :]
            scale = mod_ref[0, 3 * sub_next + 1:3 * sub_next + 2, :]
            ho_ref[...] = (xn * (1.0 + scale) + shift).astype(BF16)


def _ffn(x, mod, cond_of_tile, w_in, w_out, ln_g, ln_b, sub, sub_next):
    m = x.shape[0]
    nj = D_FF // FFN_TF
    row = lambda i, j: (i, 0)
    out_shape = [jax.ShapeDtypeStruct((m, D_MODEL), F32)]
    out_specs = [pl.BlockSpec((FFN_TM, D_MODEL), row)]
    if sub_next is not None:
        out_shape.append(jax.ShapeDtypeStruct((m, D_MODEL), BF16))
        out_specs.append(pl.BlockSpec((FFN_TM, D_MODEL), row))
    return pl.pallas_call(
        functools.partial(_ffn_kernel, sub=sub, sub_next=sub_next),
        grid=(m // FFN_TM, nj),
        in_specs=[
            pl.BlockSpec((FFN_TM, D_MODEL), row),
            pl.BlockSpec((1, 9, D_MODEL), lambda i, j: (cond_of_tile(i, FFN_TM), 0, 0)),
            pl.BlockSpec((D_MODEL, FFN_TF), lambda i, j: (0, j)),
            pl.BlockSpec((D_MODEL, FFN_TF), lambda i, j: (0, nj + j)),
            pl.BlockSpec((FFN_TF, D_MODEL), lambda i, j: (j, 0)),
            pl.BlockSpec((3, D_MODEL), lambda i, j: (0, 0)),
            pl.BlockSpec((3, D_MODEL), lambda i, j: (0, 0)),
        ],
        out_specs=out_specs,
        out_shape=out_shape,
        scratch_shapes=[pltpu.VMEM((FFN_TM, D_MODEL), BF16), pltpu.VMEM((FFN_TM, D_MODEL), F32)],
        compiler_params=_params(("parallel", "arbitrary")),
        name=f"ffn{sub}",
    )(x, mod, w_in, w_in, w_out, ln_g, ln_b)


def _rope(x, cos, sin):
    lane = lax.broadcasted_iota(jnp.int32, (1, LANES), 1)
    first_half = (lane % 32) < 16
    outs = []
    for hb in range(QK_A // LANES):
        xb = x[:, hb * LANES:(hb + 1) * LANES]
        ahead = pltpu.roll(xb, LANES - 16, 1)
        behind = pltpu.roll(xb, 16, 1)
        rot = jnp.where(first_half, -ahead, behind)
        outs.append(xb * cos + rot * sin)
    return jnp.concatenate(outs, axis=-1)


def _mixin_kernel(*refs, rope, steps):
    if rope:
        h_ref, w_ref, cos_ref, sin_ref = refs[:4]
        refs = refs[4:]
    else:
        h_ref, w_ref = refs[:2]
        refs = refs[2:]
    out_refs, w_scr = refs[:-1], refs[-1]
    n = pl.program_id(0)

    @pl.when(pl.program_id(1) == 0)
    def _():
        w_scr[...] = w_ref[...].astype(BF16)

    p = jnp.dot(h_ref[...], w_scr[...], preferred_element_type=F32)
    for idx, (out_pos, col, wblk) in enumerate(steps):
        @pl.when(n == idx)
        def _(out_pos=out_pos, col=col, wblk=wblk):
            y = p
            if rope and wblk in (W_Q, W_K):
                y = _rope(y, cos_ref[...], sin_ref[...])
            if wblk == W_Q:
                y = y * (DH_A ** -0.5)
            if wblk in (W_GA0, W_GA1, W_GH0, W_GH1):
                y = jax.nn.sigmoid(y)
            out_refs[out_pos][...] = y.astype(out_refs[out_pos].dtype)


def _mix_in(h, w, rope_tables, groups):
    m = h.shape[0]
    n_i = m // MIX_TM
    rope = rope_tables is not None
    steps, bounds = [], []
    for pos, (_, blocks) in enumerate(groups):
        bounds.append((len(steps), len(steps) + len(blocks)))
        steps += [(pos, col, wblk) for col, wblk in enumerate(blocks)]
    wblk_table = [s[2] for s in steps]

    def w_index(n, i):
        blk = 0
        for idx, wblk in enumerate(wblk_table):
            blk = jnp.where(n == idx, wblk, blk)
        return (0, blk)

    def out_index(lo, hi):
        def index(n, i):
            before, after = n < lo, n >= hi
            return (jnp.where(before, 0, jnp.where(after, n_i - 1, i)), jnp.clip(n - lo, 0, hi - lo - 1))
        return index

    in_specs = [pl.BlockSpec((MIX_TM, D_MODEL), lambda n, i: (i, 0)),
                pl.BlockSpec((D_MODEL, MIX_TN), w_index)]
    args = [h, w]
    if rope:
        n_rope_tiles = rope_tables[0].shape[0] // MIX_TM
        for t in rope_tables:
            in_specs.append(pl.BlockSpec((MIX_TM, DV_A), lambda n, i: (i % n_rope_tiles, 0)))
            args.append(t)
    return pl.pallas_call(
        functools.partial(_mixin_kernel, rope=rope, steps=tuple(steps)),
        grid=(len(steps), n_i),
        in_specs=in_specs,
        out_specs=[pl.BlockSpec((MIX_TM, MIX_TN), out_index(lo, hi)) for lo, hi in bounds],
        out_shape=[jax.ShapeDtypeStruct((m, len(blocks) * MIX_TN), dt) for dt, blocks in groups],
        scratch_shapes=[pltpu.VMEM((D_MODEL, MIX_TN), BF16)],
        compiler_params=_params(("arbitrary", "arbitrary")),
        name="mix_in",
    )(*args)


def _attn_kernel(*refs, lam_init, has_ctx):
    if has_ctx:
        lam_ref, g_ref, q_ref, k_ref, v_ref, kc_ref, vc_ref, o_ref = refs
    else:
        lam_ref, g_ref, q_ref, k_ref, v_ref, o_ref = refs
    lp = lam_ref[...]
    lam = (jnp.exp(jnp.sum(lp[0:1] * lp[1:2], axis=-1, keepdims=True))
           - jnp.exp(jnp.sum(lp[2:3] * lp[3:4], axis=-1, keepdims=True)) + lam_init)

    q = q_ref[0]
    lane = lax.broadcasted_iota(jnp.int32, (1, DV_A), 1)
    comp0 = lane < DH_A
    zero = jnp.zeros_like(q)
    qs = (jnp.where(comp0, q, zero), jnp.where(comp0, zero, q))
    keys = [k_ref[0].astype(BF16)]
    vals = [v_ref[0].astype(BF16)]
    if has_ctx:
        keys.append(kc_ref[0].astype(BF16))
        vals.append(vc_ref[0].astype(BF16))

    probs = []
    for qc in qs:
        ss = [_nt_dot(qc, kk) for kk in keys]
        mx = functools.reduce(jnp.maximum, [jnp.max(s, axis=-1, keepdims=True) for s in ss])
        es = [jnp.exp(s - mx) for s in ss]
        den = functools.reduce(jnp.add, [jnp.sum(e, axis=-1, keepdims=True) for e in es])
        probs.append((es, 1.0 / den))
    (e0, r0), (e1, r1) = probs
    r1 = lam * r1
    o = None
    for a, b, vv in zip(e0, e1, vals):
        part = jnp.dot((a * r0 - b * r1).astype(BF16), vv, preferred_element_type=F32)
        o = part if o is None else o + part
    o = o * lax.rsqrt(jnp.mean(o * o, axis=-1, keepdims=True) + RMS_EPS) * g_ref[...]
    o_ref[0] = (o * (1.0 - lam_init)).astype(o_ref.dtype)


def _attention(q, k, v, ctx_kv, lam_params, subln_g, lam_init):
    (q_arr, q_blk), (k_arr, k_blk), (v_arr, v_blk) = q, k, v
    bsz, t, _ = q_arr.shape
    tq = min(ATT_TQ, t)
    const = lambda b, h, i: (0, 0)
    in_specs = [
        pl.BlockSpec((4, DH_A), const),
        pl.BlockSpec((1, DV_A), const),
        pl.BlockSpec((1, tq, DV_A), lambda b, h, i: (b, i, q_blk + h)),
        pl.BlockSpec((1, t, DV_A), lambda b, h, i: (b, 0, k_blk + h)),
        pl.BlockSpec((1, t, DV_A), lambda b, h, i: (b, 0, v_blk + h)),
    ]
    args = [lam_params, subln_g, q_arr, k_arr, v_arr]
    if ctx_kv is not None:
        p = ctx_kv[0].shape[1]
        head_all = lambda b, h, i: (b, 0, h)
        in_specs += [pl.BlockSpec((1, p, DV_A), head_all), pl.BlockSpec((1, p, DV_A), head_all)]
        args += list(ctx_kv)
    return pl.pallas_call(
        functools.partial(_attn_kernel, lam_init=lam_init, has_ctx=ctx_kv is not None),
        grid=(bsz, N_HEADS_A, t // tq),
        in_specs=in_specs,
        out_specs=pl.BlockSpec((1, tq, DV_A), lambda b, h, i: (b, i, h)),
        out_shape=jax.ShapeDtypeStruct((bsz, t, W_A), BF16),
        compiler_params=_params(("parallel", "parallel", "arbitrary")),
        name="attention",
    )(*args)


def _chunk_scan(g, reverse):
    r = g.shape[0]
    pos = lax.broadcasted_iota(jnp.int32, (r, 1), 0) % HGRN_C
    b = g
    sh = 1
    while sh < HGRN_C:
        if reverse:
            b = b + jnp.where(pos < HGRN_C - sh, pltpu.roll(b, r - sh, 0), 0.0)
        else:
            b = b + jnp.where(pos >= sh, pltpu.roll(b, sh, 0), 0.0)
        sh *= 2
    return b


def _hgrn_kernel(*refs, seq_len, has_s0, want_state):
    refs = list(refs)
    hq_ref, hff_ref, hfb_ref, hi_ref, hg_ref, lbl_ref, ng_ref = refs[:7]
    refs = refs[7:]
    s0_ref = refs.pop(0) if has_s0 else None
    o_ref = refs.pop(0)
    sfin_ref = refs.pop(0) if want_state else None
    g_scr, kd_scr, q_scr, od_scr, st_scr, b_scr = refs
    c = HGRN_C
    rb = min(HGRN_ROWS, seq_len)
    nb = seq_len // rb
    cpb = rb // c
    nc = seq_len // c
    z_refs = (hff_ref, hfb_ref)

    def lower_bound(d):
        logits = lbl_ref[d]
        mx = jnp.max(logits, axis=0, keepdims=True)
        e = jnp.exp(logits - mx)
        return e[0:1] / jnp.sum(e, axis=0, keepdims=True)

    lbs = [lower_bound(0), lower_bound(1)]

    def gates(i, worst):
        rows = pl.ds(pl.multiple_of(i * rb, rb), rb)
        hq = hq_ref[0, rows, :]
        q_scr[rows, :] = hq * jax.nn.sigmoid(hq)
        for d in range(2):
            z = z_refs[d][0, rows, :]
            lb = lbs[d]
            e = jnp.exp(-jnp.abs(z))
            r = 1.0 / (1.0 + e)
            er = e * r
            pos = z >= 0.0
            g = jnp.log(lb + (1.0 - lb) * jnp.where(pos, r, er))
            g_scr[d, rows, :] = g
            kd_scr[d, rows, :] = (1.0 - lb) * jnp.where(pos, er, r)
            for j in range(cpb):
                tot = jnp.sum(g[j * c:(j + 1) * c], axis=0, keepdims=True)
                worst = jnp.maximum(worst, -tot)
        return worst

    worst = lax.fori_loop(0, nb, gates, jnp.zeros((1, DK_H), F32))
    safe = jnp.max(worst) <= HGRN_SAFE_DECAY

    for d in range(2):
        if has_s0:
            st_scr[d] = s0_ref[0, d, 0].T
        else:
            st_scr[d] = jnp.zeros((DV_H, DK_H), F32)

    row = lax.broadcasted_iota(jnp.int32, (c, 1), 0)
    col = lax.broadcasted_iota(jnp.int32, (1, c), 1)

    @pl.when(safe)
    def _():
        def block(i, carry):
            for d in range(2):
                reverse = d == 1
                r0 = pl.multiple_of(((nb - 1 - i) if reverse else i) * rb, rb)
                rows = pl.ds(r0, rb)
                kd = kd_scr[d, rows, :]
                vb = hi_ref[0, rows, :].astype(BF16)
                b = _chunk_scan(g_scr[d, rows, :], reverse)
                qd = (q_scr[rows, :] * jnp.exp(b)).astype(BF16)
                ku = kd * jnp.exp(-b)
                kub = ku.astype(BF16)
                causal = (row <= col) if reverse else (row >= col)
                st = st_scr[d]
                for j in (range(cpb - 1, -1, -1) if reverse else range(cpb)):
                    sl = slice(j * c, (j + 1) * c)
                    b_end = b[j * c:j * c + 1] if reverse else b[(j + 1) * c - 1:(j + 1) * c]
                    e_end = jnp.exp(b_end)
                    a = jnp.where(causal, _nt_dot(qd[sl], kub[sl]), 0.0).astype(BF16)
                    o = jnp.dot(a, vb[sl], preferred_element_type=F32) + _nt_dot(qd[sl], st.astype(BF16))
                    od_scr[d, pl.ds(r0 + j * c, c), :] = o
                    st = e_end * st + _tn_dot(vb[sl], (ku[sl] * e_end).astype(BF16))
                st_scr[d] = st
            return carry

        lax.fori_loop(0, nb, block, 0)

    @pl.when(jnp.logical_not(safe))
    def _():
        def chunk(ci, carry):
            for d in range(2):
                reverse = d == 1
                r0 = pl.multiple_of(((nc - 1 - ci) if reverse else ci) * c, c)
                rows = pl.ds(r0, c)
                kd = kd_scr[d, rows, :]
                v = hi_ref[0, rows, :]
                b = _chunk_scan(g_scr[d, rows, :], reverse)
                b_scr[...] = b

                def one_row(t, carry):
                    bt = b_scr[pl.ds(t, 1), :]
                    qt = q_scr[pl.ds(r0 + t, 1), :]
                    seen = (row >= t) if reverse else (row <= t)
                    w = jnp.exp(jnp.where(seen, bt - b_scr[...], -jnp.inf))
                    p = jnp.sum(qt * kd_scr[d, rows, :] * w, axis=-1, keepdims=True)
                    od_scr[d, pl.ds(r0 + t, 1), :] = jnp.sum(p * hi_ref[0, rows, :], axis=0, keepdims=True)
                    return carry

                lax.fori_loop(0, c, one_row, 0)
                b_end = b[0:1] if reverse else b[c - 1:c]
                st = st_scr[d]
                qd = (q_scr[rows, :] * jnp.exp(b)).astype(BF16)
                od_scr[d, rows, :] = od_scr[d, rows, :] + _nt_dot(qd, st.astype(BF16))
                k_end = (kd * jnp.exp(b_end - b)).astype(BF16)
                st_scr[d] = jnp.exp(b_end) * st + _tn_dot(v.astype(BF16), k_end)
            return carry

        lax.fori_loop(0, nc, chunk, 0)

    def finish(i, carry):
        rows = pl.ds(pl.multiple_of(i * rb, rb), rb)
        o = od_scr[0, rows, :] + od_scr[1, rows, :]
        o = o * lax.rsqrt(jnp.mean(o * o, axis=-1, keepdims=True) + RMS_EPS) * ng_ref[...]
        hg = hg_ref[0, rows, :]
        o_ref[0, rows, :] = (o * (hg * jax.nn.sigmoid(hg))).astype(o_ref.dtype)
        return carry

    lax.fori_loop(0, nb, finish, 0)
    if want_state:
        for d in range(2):
            sfin_ref[0, d, 0] = st_scr[d].T


def _hgrn(p32, lb_logits, norm_g, s0, want_state):
    bsz, t, _ = p32.shape
    stream = lambda blk: pl.BlockSpec((1, t, DK_H), lambda b, h: (b, 0, blk * N_HEADS_H + h))
    in_specs = [stream(P32_BLOCKS.index(w)) for w in (W_HQ, W_HFF, W_HFB, W_HI, W_HG)]
    in_specs += [
        pl.BlockSpec((2, DEPTH + 1, DK_H), lambda b, h: (0, 0, h)),
        pl.BlockSpec((1, DV_H), lambda b, h: (0, 0)),
    ]
    args = [p32] * 5 + [lb_logits, norm_g]
    state_spec = pl.BlockSpec((1, 2, 1, DK_H, DV_H), lambda b, h: (b, 0, h, 0, 0))
    if s0 is not None:
        in_specs.append(state_spec)
        args.append(s0)
    out_specs = [pl.BlockSpec((1, t, DV_H), lambda b, h: (b, 0, h))]
    out_shape = [jax.ShapeDtypeStruct((bsz, t, W_H), BF16)]
    if want_state:
        out_specs.append(state_spec)
        out_shape.append(jax.ShapeDtypeStruct((bsz, 2, N_HEADS_H, DK_H, DV_H), F32))
    outs = pl.pallas_call(
        functools.partial(_hgrn_kernel, seq_len=t, has_s0=s0 is not None, want_state=want_state),
        grid=(bsz, N_HEADS_H),
        in_specs=in_specs,
        out_specs=out_specs,
        out_shape=out_shape,
        scratch_shapes=[
            pltpu.VMEM((2, t, DK_H), F32),
            pltpu.VMEM((2, t, DK_H), F32),
            pltpu.VMEM((t, DK_H), F32),
            pltpu.VMEM((2, t, DV_H), F32),
            pltpu.VMEM((2, DV_H, DK_H), F32),
            pltpu.VMEM((HGRN_C, DK_H), F32),
        ],
        compiler_params=_params(("parallel", "parallel")),
        name="hgrn",
    )(*args)
    return outs if want_state else (outs[0], None)


def _merge_kernel(oa_ref, oh_ref, sa_ref, sh_ref, x_ref, mod_ref, wa_ref, wh_ref, wo_ref, lng_ref, lnb_ref,
                  xo_ref, ho_ref, wa_scr, wh_scr, wo_scr):
    @pl.when(pl.program_id(0) == 0)
    def _():
        wa_scr[...] = wa_ref[...].astype(BF16)
        wh_scr[...] = wh_ref[...].astype(BF16)
        wo_scr[...] = wo_ref[...].astype(BF16)

    ba = jnp.dot(oa_ref[...], wa_scr[...], preferred_element_type=F32)
    bh = jnp.dot(oh_ref[...], wh_scr[...], preferred_element_type=F32)
    merged = sa_ref[...].astype(F32) * ba + sh_ref[...].astype(F32) * bh
    mix = jnp.dot(merged.astype(BF16), wo_scr[...], preferred_element_type=F32)
    y = ALPHA * x_ref[...] + mod_ref[0, 5:6, :] * mix
    xn = _layer_norm(y, lng_ref[1:2, :], lnb_ref[1:2, :])
    xo_ref[...] = xn
    ho_ref[...] = (xn * (1.0 + mod_ref[0, 7:8, :]) + mod_ref[0, 6:7, :]).astype(BF16)


def _merge(o_a, o_h, p16, x, mod, cond_of_tile, w_a, w_h, w_o, ln_g, ln_b):
    m = x.shape[0]
    row = lambda i: (i, 0)
    const = lambda i: (0, 0)
    return pl.pallas_call(
        _merge_kernel,
        grid=(m // MERGE_TM,),
        in_specs=[
            pl.BlockSpec((MERGE_TM, W_A), row),
            pl.BlockSpec((MERGE_TM, W_H), row),
            pl.BlockSpec((MERGE_TM, D_MODEL), lambda i: (i, 0)),
            pl.BlockSpec((MERGE_TM, D_MODEL), lambda i: (i, 1)),
            pl.BlockSpec((MERGE_TM, D_MODEL), row),
            pl.BlockSpec((1, 9, D_MODEL), lambda i: (cond_of_tile(i, MERGE_TM), 0, 0)),
            pl.BlockSpec((W_A, D_MODEL), const),
            pl.BlockSpec((W_H, D_MODEL), const),
            pl.BlockSpec((D_MODEL, D_MODEL), const),
            pl.BlockSpec((3, D_MODEL), const),
            pl.BlockSpec((3, D_MODEL), const),
        ],
        out_specs=[pl.BlockSpec((MERGE_TM, D_MODEL), row), pl.BlockSpec((MERGE_TM, D_MODEL), row)],
        out_shape=[jax.ShapeDtypeStruct((m, D_MODEL), F32), jax.ShapeDtypeStruct((m, D_MODEL), BF16)],
        scratch_shapes=[pltpu.VMEM((W_A, D_MODEL), BF16), pltpu.VMEM((W_H, D_MODEL), BF16),
                        pltpu.VMEM((D_MODEL, D_MODEL), BF16)],
        compiler_params=_params(("arbitrary",)),
        name="merge",
    )(o_a, o_h, p16, p16, x, mod, w_a, w_h, w_o, ln_g, ln_b)


def _rope_tables(n_tok):
    rows = n_tok // GRID_W
    row = jnp.repeat(jnp.arange(rows, dtype=F32), GRID_W)
    col = jnp.tile(jnp.arange(GRID_W, dtype=F32), rows)
    half = DH_A // 2
    inv = ROPE_BASE ** (-jnp.arange(0, half, 2, dtype=F32) / half)
    ar = row[:, None] * inv
    ac = col[:, None] * inv
    ang = jnp.concatenate([ar, ar, ac, ac] * 2, axis=-1)
    return jnp.cos(ang), jnp.sin(ang)


def _layer_pass(x, mod, cond_of_tile, wts, lam_init, bsz, ctx):
    m = x.shape[0]
    t = m // bsz
    latent = ctx is not None
    b3 = lambda a: a.reshape(bsz, t, a.shape[-1])
    heads = N_HEADS_A
    x1, h1 = _ffn(x, mod, cond_of_tile, wts['ffn1_w_in'], wts['ffn1_w_out'], wts['ln_g'], wts['ln_b'], 0, 1)
    if latent:
        p32, p16 = _mix_in(h1, wts['w_mix_in'], _rope_tables(t), [(F32, P32_BLOCKS), (BF16, P16_LAT_BLOCKS)])
        k = v = None
        p16_3 = b3(p16)
        qkv = [(p16_3, P16_LAT_BLOCKS.index(w) * heads) for w in (W_Q, W_K, W_V)]
        ctx_kv = (ctx[0], ctx[1])
    else:
        k, v, p32, p16 = _mix_in(h1, wts['w_mix_in'], None,
                                 [(F32, (W_K,)), (F32, (W_V,)), (F32, P32_BLOCKS), (BF16, P16_CTX_BLOCKS)])
        qkv = [(b3(p16), P16_CTX_BLOCKS.index(W_Q) * heads), (b3(k), 0), (b3(v), 0)]
        ctx_kv = None
    o_a = _attention(*qkv, ctx_kv, wts['lam_params'], wts['attn_subln_g'], lam_init)
    o_h, s_fin = _hgrn(b3(p32), wts['lb_logits'], wts['hgrn_norm_g'], ctx[2] if latent else None, not latent)
    x2, h2 = _merge(o_a.reshape(m, W_A), o_h.reshape(m, W_H), p16, x1, mod, cond_of_tile,
                    wts['w_branch_a'], wts['w_branch_h'], wts['w_mix_out'], wts['ln_g'], wts['ln_b'])
    (y,) = _ffn(x2, mod, cond_of_tile, wts['ffn2_w_in'], wts['ffn2_w_out'], wts['ln_g'], wts['ln_b'], 2, None)
    return y, (k, v, s_fin)


def kernel(x_prompt, x_sample, cache_k, cache_v, state_hgrn, c, c_ctx, w_ada, b_ada, ffn1_w_in, ffn1_w_out,
           w_mix_in, lambda_q1, lambda_k1, lambda_q2, lambda_k2, attn_subln_g, hgrn_lb_logits, hgrn_norm_g,
           w_branch_a, w_branch_h, w_mix_out, ffn2_w_in, ffn2_w_out, ln_g, ln_b):
    batch, seq, _ = x_prompt.shape
    dec_batch, dec_seq, _ = x_sample.shape
    past = cache_k.shape[2]
    assert DEPTH == 1 and dec_batch + 1 <= MOD_ROWS
    lam_init = 0.8 - 0.6 * math.exp(-0.3 * 0)

    cond = jnp.concatenate([c_ctx[None, :], c, jnp.zeros((MOD_ROWS - 1 - dec_batch, D_MODEL), F32)], axis=0)
    mod = _modulation(cond, w_ada[0], b_ada[0])

    wts = {
        'ffn1_w_in': ffn1_w_in[0], 'ffn1_w_out': ffn1_w_out[0], 'w_mix_in': w_mix_in[0],
        'lam_params': jnp.concatenate([lambda_q1, lambda_k1, lambda_q2, lambda_k2], axis=0),
        'attn_subln_g': attn_subln_g, 'lb_logits': hgrn_lb_logits, 'hgrn_norm_g': hgrn_norm_g,
        'w_branch_a': w_branch_a[0], 'w_branch_h': w_branch_h[0], 'w_mix_out': w_mix_out[0],
        'ffn2_w_in': ffn2_w_in[0], 'ffn2_w_out': ffn2_w_out[0], 'ln_g': ln_g[0], 'ln_b': ln_b[0],
    }

    y_p, (k_c, v_c, s_c) = _layer_pass(x_prompt.reshape(batch * seq, D_MODEL), mod, lambda i, tm: 0,
                                       wts, lam_init, batch, None)
    ctx = (cache_k[:, 0].reshape(dec_batch, past, QK_A), cache_v[:, 0].reshape(dec_batch, past, W_A),
           state_hgrn[:, 0])
    y_s, _ = _layer_pass(x_sample.reshape(dec_batch * dec_seq, D_MODEL), mod,
                         lambda i, tm: 1 + (i * tm) // dec_seq, wts, lam_init, dec_batch, ctx)

    return (y_p.reshape(batch, seq, D_MODEL),
            y_s.reshape(dec_batch, dec_seq, D_MODEL),
            k_c.reshape(batch, DEPTH, seq, N_HEADS_A, 2, DH_A),
            v_c.reshape(batch, DEPTH, seq, N_HEADS_A, DV_A),
            s_c.reshape(batch, DEPTH, 2, N_HEADS_H, DK_H, DV_H))
```

```python
import functools
import math

import jax
import jax.numpy as jnp
from jax import lax
from jax.experimental import pallas as pl
from jax.experimental.pallas import tpu as pltpu

D_MODEL = 1024
DEPTH = 1
GRID_W = 64
N_HEADS_A = 4
DH_A = 64
DV_A = 2 * DH_A
QK_A = N_HEADS_A * 2 * DH_A
W_A = N_HEADS_A * DV_A
ROPE_BASE = 10000.0
N_HEADS_H = 4
DK_H = 128
DV_H = 128
QK_H = N_HEADS_H * DK_H
W_H = N_HEADS_H * DV_H
D_FF = 2816
MIX_IN = 2 * QK_A + W_A + 3 * QK_H + 2 * W_H + 2 * D_MODEL
ALPHA = (2 * DEPTH) ** 0.25
LN_EPS = 1e-5
RMS_EPS = 1e-6

F32 = jnp.float32
BF16 = jnp.bfloat16

LANES = 128
VMEM_LIMIT = 56 * 1024 * 1024

FFN_TM = 1024
FFN_TF = 256
MIX_TM = 1024
MIX_TN = 512
MERGE_TM = 512
ATT_TQ = 256
HGRN_C = 64
HGRN_ROWS = 256
HGRN_SAFE_DECAY = 75.0
MOD_ROWS = 8

W_Q, W_K, W_V, W_HQ, W_HFF, W_HFB, W_HI, W_HG, W_GA0, W_GA1, W_GH0, W_GH1 = range(12)
HGRN_BLOCKS = (W_HQ, W_HFF, W_HFB, W_HI, W_HG)
GATE_BLOCKS = (W_GA0, W_GA1, W_GH0, W_GH1)


def _params(sem):
    return pltpu.CompilerParams(dimension_semantics=sem, vmem_limit_bytes=VMEM_LIMIT)


def _nt_dot(a, b):
    return lax.dot_general(a, b, (((1,), (1,)), ((), ())), preferred_element_type=F32)


def _tn_dot(a, b):
    return lax.dot_general(a, b, (((0,), (0,)), ((), ())), preferred_element_type=F32)


def _layer_norm(y, g, b):
    mu = jnp.mean(y, axis=-1, keepdims=True)
    yc = y - mu
    var = jnp.mean(yc * yc, axis=-1, keepdims=True)
    return yc * lax.rsqrt(var + LN_EPS) * g + b


def _mod_kernel(c_ref, w_ref, b_ref, o_ref):
    c = c_ref[...]
    a = (c * jax.nn.sigmoid(c)).astype(BF16)
    o_ref[...] = jnp.dot(a, w_ref[...].astype(BF16), preferred_element_type=F32) + b_ref[...]


def _modulation(cond, w_ada, b_ada):
    n_out = w_ada.shape[1]
    tn = D_MODEL
    out = pl.pallas_call(
        _mod_kernel,
        grid=(n_out // tn,),
        in_specs=[
            pl.BlockSpec((MOD_ROWS, D_MODEL), lambda n: (0, 0)),
            pl.BlockSpec((D_MODEL, tn), lambda n: (0, n)),
            pl.BlockSpec((1, tn), lambda n: (0, n)),
        ],
        out_specs=pl.BlockSpec((MOD_ROWS, tn), lambda n: (0, n)),
        out_shape=jax.ShapeDtypeStruct((MOD_ROWS, n_out), F32),
        compiler_params=_params(("arbitrary",)),
        name="modulation",
    )(cond, w_ada, b_ada.reshape(1, n_out))
    return out.reshape(MOD_ROWS, 9, D_MODEL)


def _ffn_kernel(x_ref, mod_ref, wg_ref, wu_ref, wo_ref, lng_ref, lnb_ref, *rest, sub, sub_next):
    if sub_next is None:
        xo_ref, h_scr, acc_scr = rest
        ho_ref = None
    else:
        xo_ref, ho_ref, h_scr, acc_scr = rest
    j = pl.program_id(1)

    @pl.when(j == 0)
    def _():
        shift = mod_ref[0, 3 * sub:3 * sub + 1, :]
        scale = mod_ref[0, 3 * sub + 1:3 * sub + 2, :]
        h_scr[...] = (x_ref[...] * (1.0 + scale) + shift).astype(BF16)
        acc_scr[...] = jnp.zeros_like(acc_scr)

    h = h_scr[...]
    gate = jnp.dot(h, wg_ref[...].astype(BF16), preferred_element_type=F32)
    up = jnp.dot(h, wu_ref[...].astype(BF16), preferred_element_type=F32)
    act = (gate * jax.nn.sigmoid(gate) * up).astype(BF16)
    acc_scr[...] += jnp.dot(act, wo_ref[...].astype(BF16), preferred_element_type=F32)

    @pl.when(j == pl.num_programs(1) - 1)
    def _():
        g = mod_ref[0, 3 * sub + 2:3 * sub + 3, :]
        y = ALPHA * x_ref[...] + 0.5 * g * acc_scr[...]
        xn = _layer_norm(y, lng_ref[sub:sub + 1, :], lnb_ref[sub:sub + 1, :])
        xo_ref[...] = xn
        if ho_ref is not None:
            shift = mod_ref[0, 3 * sub_next:3 * sub_next + 1, :]
            scale = mod_ref[0, 3 * sub_next + 1:3 * sub_next + 2, :]
            ho_ref[...] = (xn * (1.0 + scale) + shift).astype(BF16)


def _ffn(x, mod, cond_of_tile, w_in, w_out, ln_g, ln_b, sub, sub_next):
    m = x.shape[0]
    nj = D_FF // FFN_TF
    row = lambda i, j: (i, 0)
    out_shape = [jax.ShapeDtypeStruct((m, D_MODEL), F32)]
    out_specs = [pl.BlockSpec((FFN_TM, D_MODEL), row)]
    if sub_next is not None:
        out_shape.append(jax.ShapeDtypeStruct((m, D_MODEL), BF16))
        out_specs.append(pl.BlockSpec((FFN_TM, D_MODEL), row))
    return pl.pallas_call(
        functools.partial(_ffn_kernel, sub=sub, sub_next=sub_next),
        grid=(m // FFN_TM, nj),
        in_specs=[
            pl.BlockSpec((FFN_TM, D_MODEL), row),
            pl.BlockSpec((1, 9, D_MODEL), lambda i, j: (cond_of_tile(i, FFN_TM), 0, 0)),
            pl.BlockSpec((D_MODEL, FFN_TF), lambda i, j: (0, j)),
            pl.BlockSpec((D_MODEL, FFN_TF), lambda i, j: (0, nj + j)),
            pl.BlockSpec((FFN_TF, D_MODEL), lambda i, j: (j, 0)),
            pl.BlockSpec((3, D_MODEL), lambda i, j: (0, 0)),
            pl.BlockSpec((3, D_MODEL), lambda i, j: (0, 0)),
        ],
        out_specs=out_specs,
        out_shape=out_shape,
        scratch_shapes=[pltpu.VMEM((FFN_TM, D_MODEL), BF16), pltpu.VMEM((FFN_TM, D_MODEL), F32)],
        compiler_params=_params(("parallel", "arbitrary")),
        name=f"ffn{sub}",
    )(x, mod, w_in, w_in, w_out, ln_g, ln_b)


def _rope(x, cos, sin):
    lane = lax.broadcasted_iota(jnp.int32, (1, LANES), 1)
    first_half = (lane % 32) < 16
    outs = []
    for hb in range(QK_A // LANES):
        xb = x[:, hb * LANES:(hb + 1) * LANES]
        ahead = pltpu.roll(xb, LANES - 16, 1)
        behind = pltpu.roll(xb, 16, 1)
        rot = jnp.where(first_half, -ahead, behind)
        outs.append(xb * cos + rot * sin)
    return jnp.concatenate(outs, axis=-1)


def _mixin_kernel(*refs, rope, steps, writers):
    if rope:
        h_ref, w_ref, cos_ref, sin_ref = refs[:4]
        refs = refs[4:]
    else:
        h_ref, w_ref = refs[:2]
        refs = refs[2:]
    out_refs, w_scr = refs[:-1], refs[-1]
    n = pl.program_id(0)

    @pl.when(pl.program_id(1) == 0)
    def _():
        w_scr[...] = w_ref[...].astype(BF16)

    p = jnp.dot(h_ref[...], w_scr[...], preferred_element_type=F32)
    for idx, wblk in enumerate(steps):
        @pl.when(n == idx)
        def _(idx=idx, wblk=wblk):
            y = p
            if rope and wblk in (W_Q, W_K):
                y = _rope(y, cos_ref[...], sin_ref[...])
            if wblk == W_Q:
                y = y * (DH_A ** -0.5)
            if wblk in (W_GA0, W_GA1, W_GH0, W_GH1):
                y = jax.nn.sigmoid(y)
            for out_pos, per_head in writers[idx]:
                ref = out_refs[out_pos]
                if per_head:
                    for hh in range(MIX_TN // LANES):
                        ref[hh] = y[:, hh * LANES:(hh + 1) * LANES].astype(ref.dtype)
                else:
                    ref[...] = y.astype(ref.dtype)


def _mix_in(h, w, rope_tables, groups):
    m = h.shape[0]
    n_i = m // MIX_TM
    rope = rope_tables is not None
    steps = []
    for _, _, blocks in groups:
        steps += [b for b in blocks if b not in steps]
    bounds = []
    writers = [[] for _ in steps]
    for pos, (_, per_head, blocks) in enumerate(groups):
        lo = steps.index(blocks[0])
        hi = lo + len(blocks)
        assert tuple(steps[lo:hi]) == tuple(blocks)
        bounds.append((lo, hi))
        for idx in range(lo, hi):
            writers[idx].append((pos, per_head))

    def w_index(n, i):
        blk = 0
        for idx, wblk in enumerate(steps):
            blk = jnp.where(n == idx, wblk, blk)
        return (0, blk)

    def out_spec(per_head, lo, hi):
        def position(n, i):
            row = jnp.where(n < lo, 0, jnp.where(n >= hi, n_i - 1, i))
            return row, jnp.clip(n - lo, 0, hi - lo - 1)
        if per_head:
            def index(n, i):
                row, col = position(n, i)
                return (col, row, 0)
            return pl.BlockSpec((MIX_TN // LANES, MIX_TM, LANES), index)
        return pl.BlockSpec((MIX_TM, MIX_TN), position)

    def out_struct(dt, per_head, blocks):
        if per_head:
            return jax.ShapeDtypeStruct((len(blocks) * (MIX_TN // LANES), m, LANES), dt)
        return jax.ShapeDtypeStruct((m, len(blocks) * MIX_TN), dt)

    in_specs = [pl.BlockSpec((MIX_TM, D_MODEL), lambda n, i: (i, 0)),
                pl.BlockSpec((D_MODEL, MIX_TN), w_index)]
    args = [h, w]
    if rope:
        n_rope_tiles = rope_tables[0].shape[0] // MIX_TM
        for t in rope_tables:
            in_specs.append(pl.BlockSpec((MIX_TM, DV_A), lambda n, i: (i % n_rope_tiles, 0)))
            args.append(t)
    return pl.pallas_call(
        functools.partial(_mixin_kernel, rope=rope, steps=tuple(steps),
                          writers=tuple(tuple(wr) for wr in writers)),
        grid=(len(steps), n_i),
        in_specs=in_specs,
        out_specs=[out_spec(per_head, lo, hi) for (_, per_head, _), (lo, hi) in zip(groups, bounds)],
        out_shape=[out_struct(*grp) for grp in groups],
        scratch_shapes=[pltpu.VMEM((D_MODEL, MIX_TN), BF16)],
        compiler_params=_params(("arbitrary", "arbitrary")),
        name="mix_in",
    )(*args)


def _attn_kernel(*refs, lam_init, has_ctx):
    if has_ctx:
        lam_ref, g_ref, q_ref, k_ref, v_ref, kc_ref, vc_ref, o_ref = refs
    else:
        lam_ref, g_ref, q_ref, k_ref, v_ref, o_ref = refs
    lp = lam_ref[...]
    lam = (jnp.exp(jnp.sum(lp[0:1] * lp[1:2], axis=-1, keepdims=True))
           - jnp.exp(jnp.sum(lp[2:3] * lp[3:4], axis=-1, keepdims=True)) + lam_init)

    q = q_ref[0, 0]
    lane = lax.broadcasted_iota(jnp.int32, (1, DV_A), 1)
    comp0 = lane < DH_A
    zero = jnp.zeros_like(q)
    qs = (jnp.where(comp0, q, zero), jnp.where(comp0, zero, q))
    keys = [k_ref[0, 0]]
    vals = [v_ref[0, 0]]
    if has_ctx:
        keys.append(kc_ref[0].astype(BF16))
        vals.append(vc_ref[0].astype(BF16))

    probs = []
    for qc in qs:
        ss = [_nt_dot(qc, kk) for kk in keys]
        mx = functools.reduce(jnp.maximum, [jnp.max(s, axis=-1, keepdims=True) for s in ss])
        es = [jnp.exp(s - mx) for s in ss]
        den = functools.reduce(jnp.add, [jnp.sum(e, axis=-1, keepdims=True) for e in es])
        probs.append((es, 1.0 / den))
    (e0, r0), (e1, r1) = probs
    r1 = lam * r1
    o = None
    for a, b, vv in zip(e0, e1, vals):
        part = jnp.dot((a * r0 - b * r1).astype(BF16), vv, preferred_element_type=F32)
        o = part if o is None else o + part
    o = o * lax.rsqrt(jnp.mean(o * o, axis=-1, keepdims=True) + RMS_EPS) * g_ref[...]
    o_ref[0, 0] = (o * (1.0 - lam_init)).astype(o_ref.dtype)


def _attention(q, k, v, ctx_kv, lam_params, subln_g, lam_init):
    (q_arr, q_row), (k_arr, k_row), (v_arr, v_row) = q, k, v
    _, bsz, t, _ = q_arr.shape
    tq = min(ATT_TQ, t)
    const = lambda b, h, i: (0, 0)
    in_specs = [
        pl.BlockSpec((4, DH_A), const),
        pl.BlockSpec((1, DV_A), const),
        pl.BlockSpec((1, 1, tq, DV_A), lambda b, h, i: (q_row + h, b, i, 0)),
        pl.BlockSpec((1, 1, t, DV_A), lambda b, h, i: (k_row + h, b, 0, 0)),
        pl.BlockSpec((1, 1, t, DV_A), lambda b, h, i: (v_row + h, b, 0, 0)),
    ]
    args = [lam_params, subln_g, q_arr, k_arr, v_arr]
    if ctx_kv is not None:
        p = ctx_kv[0].shape[1]
        head_all = lambda b, h, i: (b, 0, h)
        in_specs += [pl.BlockSpec((1, p, DV_A), head_all), pl.BlockSpec((1, p, DV_A), head_all)]
        args += list(ctx_kv)
    return pl.pallas_call(
        functools.partial(_attn_kernel, lam_init=lam_init, has_ctx=ctx_kv is not None),
        grid=(bsz, N_HEADS_A, t // tq),
        in_specs=in_specs,
        out_specs=pl.BlockSpec((1, 1, tq, DV_A), lambda b, h, i: (h, b, i, 0)),
        out_shape=jax.ShapeDtypeStruct((N_HEADS_A, bsz, t, DV_A), BF16),
        compiler_params=_params(("parallel", "parallel", "arbitrary")),
        name="attention",
    )(*args)


def _chunk_scan(g, reverse):
    r = g.shape[0]
    pos = lax.broadcasted_iota(jnp.int32, (r, 1), 0) % HGRN_C
    b = g
    sh = 1
    while sh < HGRN_C:
        if reverse:
            b = b + jnp.where(pos < HGRN_C - sh, pltpu.roll(b, r - sh, 0), 0.0)
        else:
            b = b + jnp.where(pos >= sh, pltpu.roll(b, sh, 0), 0.0)
        sh *= 2
    return b


def _hgrn_kernel(*refs, seq_len, has_s0, want_state):
    refs = list(refs)
    hq_ref, hff_ref, hfb_ref, hi_ref, hg_ref, lbl_ref, ng_ref = refs[:7]
    refs = refs[7:]
    s0_ref = refs.pop(0) if has_s0 else None
    o_ref = refs.pop(0)
    sfin_ref = refs.pop(0) if want_state else None
    g_scr, kd_scr, q_scr, od_scr, st_scr, b_scr, ok_scr = refs
    c = HGRN_C
    rb = min(HGRN_ROWS, seq_len)
    nb = seq_len // rb
    cpb = rb // c
    z_refs = (hff_ref, hfb_ref)

    def lower_bound(d):
        logits = lbl_ref[d]
        mx = jnp.max(logits, axis=0, keepdims=True)
        e = jnp.exp(logits - mx)
        return e[0:1] / jnp.sum(e, axis=0, keepdims=True)

    lbs = [lower_bound(0), lower_bound(1)]

    def gates(i, carry):
        rows = pl.ds(pl.multiple_of(i * rb, rb), rb)
        hq = hq_ref[0, 0, rows, :]
        q_scr[rows, :] = hq * jax.nn.sigmoid(hq)
        for d in range(2):
            z = z_refs[d][0, 0, rows, :]
            lb = lbs[d]
            e = jnp.exp(-jnp.abs(z))
            r = 1.0 / (1.0 + e)
            er = e * r
            pos = z >= 0.0
            g = jnp.log(lb + (1.0 - lb) * jnp.where(pos, r, er))
            g_scr[d, rows, :] = g
            kd_scr[d, rows, :] = (1.0 - lb) * jnp.where(pos, er, r)
            worst = jnp.zeros((1, DK_H), F32)
            for j in range(cpb):
                worst = jnp.maximum(worst, -jnp.sum(g[j * c:(j + 1) * c], axis=0, keepdims=True))
            ok_scr[d, i] = (jnp.max(worst) <= HGRN_SAFE_DECAY).astype(jnp.int32)
        return carry

    lax.fori_loop(0, nb, gates, 0)

    for d in range(2):
        if has_s0:
            st_scr[d] = s0_ref[0, d, 0].T
        else:
            st_scr[d] = jnp.zeros((DV_H, DK_H), F32)

    row = lax.broadcasted_iota(jnp.int32, (c, 1), 0)
    col = lax.broadcasted_iota(jnp.int32, (1, c), 1)

    def fast_block(d, r0):
        reverse = d == 1
        rows = pl.ds(r0, rb)
        kd = kd_scr[d, rows, :]
        vb = hi_ref[0, 0, rows, :].astype(BF16)
        b = _chunk_scan(g_scr[d, rows, :], reverse)
        qd = (q_scr[rows, :] * jnp.exp(b)).astype(BF16)
        ku = kd * jnp.exp(-b)
        kub = ku.astype(BF16)
        causal = (row <= col) if reverse else (row >= col)
        st = st_scr[d]
        for j in (range(cpb - 1, -1, -1) if reverse else range(cpb)):
            sl = slice(j * c, (j + 1) * c)
            b_end = b[j * c:j * c + 1] if reverse else b[(j + 1) * c - 1:(j + 1) * c]
            e_end = jnp.exp(b_end)
            a = jnp.where(causal, _nt_dot(qd[sl], kub[sl]), 0.0).astype(BF16)
            o = jnp.dot(a, vb[sl], preferred_element_type=F32) + _nt_dot(qd[sl], st.astype(BF16))
            od_scr[d, pl.ds(r0 + j * c, c), :] = o
            st = e_end * st + _tn_dot(vb[sl], (ku[sl] * e_end).astype(BF16))
        st_scr[d] = st

    def exact_chunk(d, r0):
        reverse = d == 1
        rows = pl.ds(r0, c)
        b = _chunk_scan(g_scr[d, rows, :], reverse)
        b_scr[...] = b

        def one_row(t, carry):
            bt = b_scr[pl.ds(t, 1), :]
            qt = q_scr[pl.ds(r0 + t, 1), :]
            seen = (row >= t) if reverse else (row <= t)
            w = jnp.exp(jnp.where(seen, bt - b_scr[...], -jnp.inf))
            p = jnp.sum(qt * kd_scr[d, rows, :] * w, axis=-1, keepdims=True)
            od_scr[d, pl.ds(r0 + t, 1), :] = jnp.sum(p * hi_ref[0, 0, rows, :], axis=0, keepdims=True)
            return carry

        lax.fori_loop(0, c, one_row, 0)
        b_end = b[0:1] if reverse else b[c - 1:c]
        st = st_scr[d]
        qd = (q_scr[rows, :] * jnp.exp(b)).astype(BF16)
        od_scr[d, rows, :] = od_scr[d, rows, :] + _nt_dot(qd, st.astype(BF16))
        k_end = (kd_scr[d, rows, :] * jnp.exp(b_end - b)).astype(BF16)
        st_scr[d] = jnp.exp(b_end) * st + _tn_dot(hi_ref[0, 0, rows, :].astype(BF16), k_end)

    def block(i, carry):
        starts = (pl.multiple_of(i * rb, rb), pl.multiple_of((nb - 1 - i) * rb, rb))
        mild = (ok_scr[0, i] + ok_scr[1, nb - 1 - i]) == 2

        @pl.when(mild)
        def _():
            for d in range(2):
                fast_block(d, starts[d])

        @pl.when(jnp.logical_not(mild))
        def _():
            for d in range(2):
                for j in (range(cpb - 1, -1, -1) if d == 1 else range(cpb)):
                    exact_chunk(d, pl.multiple_of(starts[d] + j * c, c))

        return carry

    lax.fori_loop(0, nb, block, 0)

    def finish(i, carry):
        rows = pl.ds(pl.multiple_of(i * rb, rb), rb)
        o = od_scr[0, rows, :] + od_scr[1, rows, :]
        o = o * lax.rsqrt(jnp.mean(o * o, axis=-1, keepdims=True) + RMS_EPS) * ng_ref[...]
        hg = hg_ref[0, 0, rows, :]
        o_ref[0, 0, rows, :] = (o * (hg * jax.nn.sigmoid(hg))).astype(o_ref.dtype)
        return carry

    lax.fori_loop(0, nb, finish, 0)
    if want_state:
        for d in range(2):
            sfin_ref[0, d, 0] = st_scr[d].T


def _hgrn(streams, lb_logits, norm_g, s0, want_state):
    _, bsz, t, _ = streams.shape
    stream = lambda s: pl.BlockSpec((1, 1, t, DK_H), lambda b, h: (s * N_HEADS_H + h, b, 0, 0))
    in_specs = [stream(HGRN_BLOCKS.index(w)) for w in (W_HQ, W_HFF, W_HFB, W_HI, W_HG)]
    in_specs += [
        pl.BlockSpec((2, DEPTH + 1, DK_H), lambda b, h: (0, 0, h)),
        pl.BlockSpec((1, DV_H), lambda b, h: (0, 0)),
    ]
    args = [streams] * 5 + [lb_logits, norm_g]
    state_spec = pl.BlockSpec((1, 2, 1, DK_H, DV_H), lambda b, h: (b, 0, h, 0, 0))
    if s0 is not None:
        in_specs.append(state_spec)
        args.append(s0)
    out_specs = [pl.BlockSpec((1, 1, t, DV_H), lambda b, h: (h, b, 0, 0))]
    out_shape = [jax.ShapeDtypeStruct((N_HEADS_H, bsz, t, DV_H), BF16)]
    if want_state:
        out_specs.append(state_spec)
        out_shape.append(jax.ShapeDtypeStruct((bsz, 2, N_HEADS_H, DK_H, DV_H), F32))
    n_blocks = t // min(HGRN_ROWS, t)
    outs = pl.pallas_call(
        functools.partial(_hgrn_kernel, seq_len=t, has_s0=s0 is not None, want_state=want_state),
        grid=(bsz, N_HEADS_H),
        in_specs=in_specs,
        out_specs=out_specs,
        out_shape=out_shape,
        scratch_shapes=[
            pltpu.VMEM((2, t, DK_H), F32),
            pltpu.VMEM((2, t, DK_H), F32),
            pltpu.VMEM((t, DK_H), F32),
            pltpu.VMEM((2, t, DV_H), F32),
            pltpu.VMEM((2, DV_H, DK_H), F32),
            pltpu.VMEM((HGRN_C, DK_H), F32),
            pltpu.SMEM((2, n_blocks), jnp.int32),
        ],
        compiler_params=_params(("parallel", "parallel")),
        name="hgrn",
    )(*args)
    return outs if want_state else (outs[0], None)


def _merge_kernel(oa_ref, oh_ref, sa_ref, sh_ref, x_ref, mod_ref, wa_ref, wh_ref, wo_ref, lng_ref, lnb_ref,
                  xo_ref, ho_ref, wa_scr, wh_scr, wo_scr):
    @pl.when(pl.program_id(0) == 0)
    def _():
        wa_scr[...] = wa_ref[...].astype(BF16)
        wh_scr[...] = wh_ref[...].astype(BF16)
        wo_scr[...] = wo_ref[...].astype(BF16)

    heads = lambda ref: jnp.concatenate([ref[hh] for hh in range(ref.shape[0])], axis=-1)
    ba = jnp.dot(heads(oa_ref), wa_scr[...], preferred_element_type=F32)
    bh = jnp.dot(heads(oh_ref), wh_scr[...], preferred_element_type=F32)
    merged = sa_ref[...].astype(F32) * ba + sh_ref[...].astype(F32) * bh
    mix = jnp.dot(merged.astype(BF16), wo_scr[...], preferred_element_type=F32)
    y = ALPHA * x_ref[...] + mod_ref[0, 5:6, :] * mix
    xn = _layer_norm(y, lng_ref[1:2, :], lnb_ref[1:2, :])
    xo_ref[...] = xn
    ho_ref[...] = (xn * (1.0 + mod_ref[0, 7:8, :]) + mod_ref[0, 6:7, :]).astype(BF16)


def _merge(o_a, o_h, gates, x, mod, cond_of_tile, w_a, w_h, w_o, ln_g, ln_b):
    m = x.shape[0]
    row = lambda i: (i, 0)
    const = lambda i: (0, 0)
    per_head = lambda i: (0, i, 0)
    return pl.pallas_call(
        _merge_kernel,
        grid=(m // MERGE_TM,),
        in_specs=[
            pl.BlockSpec((N_HEADS_A, MERGE_TM, DV_A), per_head),
            pl.BlockSpec((N_HEADS_H, MERGE_TM, DV_H), per_head),
            pl.BlockSpec((MERGE_TM, D_MODEL), lambda i: (i, 0)),
            pl.BlockSpec((MERGE_TM, D_MODEL), lambda i: (i, 1)),
            pl.BlockSpec((MERGE_TM, D_MODEL), row),
            pl.BlockSpec((1, 9, D_MODEL), lambda i: (cond_of_tile(i, MERGE_TM), 0, 0)),
            pl.BlockSpec((W_A, D_MODEL), const),
            pl.BlockSpec((W_H, D_MODEL), const),
            pl.BlockSpec((D_MODEL, D_MODEL), const),
            pl.BlockSpec((3, D_MODEL), const),
            pl.BlockSpec((3, D_MODEL), const),
        ],
        out_specs=[pl.BlockSpec((MERGE_TM, D_MODEL), row), pl.BlockSpec((MERGE_TM, D_MODEL), row)],
        out_shape=[jax.ShapeDtypeStruct((m, D_MODEL), F32), jax.ShapeDtypeStruct((m, D_MODEL), BF16)],
        scratch_shapes=[pltpu.VMEM((W_A, D_MODEL), BF16), pltpu.VMEM((W_H, D_MODEL), BF16),
                        pltpu.VMEM((D_MODEL, D_MODEL), BF16)],
        compiler_params=_params(("arbitrary",)),
        name="merge",
    )(o_a, o_h, gates, gates, x, mod, w_a, w_h, w_o, ln_g, ln_b)


def _rope_tables(n_tok):
    rows = n_tok // GRID_W
    row = jnp.repeat(jnp.arange(rows, dtype=F32), GRID_W)
    col = jnp.tile(jnp.arange(GRID_W, dtype=F32), rows)
    half = DH_A // 2
    inv = ROPE_BASE ** (-jnp.arange(0, half, 2, dtype=F32) / half)
    ar = row[:, None] * inv
    ac = col[:, None] * inv
    ang = jnp.concatenate([ar, ar, ac, ac] * 2, axis=-1)
    return jnp.cos(ang), jnp.sin(ang)


def _layer_pass(x, mod, cond_of_tile, wts, lam_init, bsz, ctx):
    m = x.shape[0]
    t = m // bsz
    latent = ctx is not None
    per_head = lambda a: a.reshape(a.shape[0], bsz, t, a.shape[-1])
    x1, h1 = _ffn(x, mod, cond_of_tile, wts['ffn1_w_in'], wts['ffn1_w_out'], wts['ln_g'], wts['ln_b'], 0, 1)
    if latent:
        streams, gates, qkv = _mix_in(
            h1, wts['w_mix_in'], _rope_tables(t),
            [(F32, True, HGRN_BLOCKS), (BF16, False, GATE_BLOCKS), (BF16, True, (W_Q, W_K, W_V))])
        k = v = None
        qkv = per_head(qkv)
        q_kv = [(qkv, 0), (qkv, N_HEADS_A), (qkv, 2 * N_HEADS_A)]
        ctx_kv = (ctx[0], ctx[1])
    else:
        k, kh, v, vh, streams, gates, q = _mix_in(
            h1, wts['w_mix_in'], None,
            [(F32, False, (W_K,)), (BF16, True, (W_K,)), (F32, False, (W_V,)), (BF16, True, (W_V,)),
             (F32, True, HGRN_BLOCKS), (BF16, False, GATE_BLOCKS), (BF16, True, (W_Q,))])
        q_kv = [(per_head(q), 0), (per_head(kh), 0), (per_head(vh), 0)]
        ctx_kv = None
    o_a = _attention(*q_kv, ctx_kv, wts['lam_params'], wts['attn_subln_g'], lam_init)
    o_h, s_fin = _hgrn(per_head(streams), wts['lb_logits'], wts['hgrn_norm_g'],
                       ctx[2] if latent else None, not latent)
    x2, h2 = _merge(o_a.reshape(N_HEADS_A, m, DV_A), o_h.reshape(N_HEADS_H, m, DV_H), gates, x1, mod,
                    cond_of_tile, wts['w_branch_a'], wts['w_branch_h'], wts['w_mix_out'], wts['ln_g'], wts['ln_b'])
    (y,) = _ffn(x2, mod, cond_of_tile, wts['ffn2_w_in'], wts['ffn2_w_out'], wts['ln_g'], wts['ln_b'], 2, None)
    return y, (k, v, s_fin)


def kernel(x_prompt, x_sample, cache_k, cache_v, state_hgrn, c, c_ctx, w_ada, b_ada, ffn1_w_in, ffn1_w_out,
           w_mix_in, lambda_q1, lambda_k1, lambda_q2, lambda_k2, attn_subln_g, hgrn_lb_logits, hgrn_norm_g,
           w_branch_a, w_branch_h, w_mix_out, ffn2_w_in, ffn2_w_out, ln_g, ln_b):
    batch, seq, _ = x_prompt.shape
    dec_batch, dec_seq, _ = x_sample.shape
    past = cache_k.shape[2]
    assert DEPTH == 1 and dec_batch + 1 <= MOD_ROWS
    lam_init = 0.8 - 0.6 * math.exp(-0.3 * 0)

    cond = jnp.concatenate([c_ctx[None, :], c, jnp.zeros((MOD_ROWS - 1 - dec_batch, D_MODEL), F32)], axis=0)
    mod = _modulation(cond, w_ada[0], b_ada[0])

    wts = {
        'ffn1_w_in': ffn1_w_in[0], 'ffn1_w_out': ffn1_w_out[0], 'w_mix_in': w_mix_in[0],
        'lam_params': jnp.concatenate([lambda_q1, lambda_k1, lambda_q2, lambda_k2], axis=0),
        'attn_subln_g': attn_subln_g, 'lb_logits': hgrn_lb_logits, 'hgrn_norm_g': hgrn_norm_g,
        'w_branch_a': w_branch_a[0], 'w_branch_h': w_branch_h[0], 'w_mix_out': w_mix_out[0],
        'ffn2_w_in': ffn2_w_in[0], 'ffn2_w_out': ffn2_w_out[0], 'ln_g': ln_g[0], 'ln_b': ln_b[0],
    }

    y_p, (k_c, v_c, s_c) = _layer_pass(x_prompt.reshape(batch * seq, D_MODEL), mod, lambda i, tm: 0,
                                       wts, lam_init, batch, None)
    ctx = (cache_k[:, 0].reshape(dec_batch, past, QK_A), cache_v[:, 0].reshape(dec_batch, past, W_A),
           state_hgrn[:, 0])
    y_s, _ = _layer_pass(x_sample.reshape(dec_batch * dec_seq, D_MODEL), mod,
                         lambda i, tm: 1 + (i * tm) // dec_seq, wts, lam_init, dec_batch, ctx)

    return (y_p.reshape(batch, seq, D_MODEL),
            y_s.reshape(dec_batch, dec_seq, D_MODEL),
            k_c.reshape(batch, DEPTH, seq, N_HEADS_A, 2, DH_A),
            v_c.reshape(batch, DEPTH, seq, N_HEADS_A, DV_A),
            s_c.reshape(batch, DEPTH, 2, N_HEADS_H, DK_H, DV_H))
```

```python
import functools
import math

import numpy as np
import jax
import jax.numpy as jnp
from jax import lax
from jax.experimental import pallas as pl
from jax.experimental.pallas import tpu as pltpu

D_MODEL = 1024
DEPTH = 1
GRID_W = 64
N_HEADS_A = 4
DH_A = 64
DV_A = 2 * DH_A
QK_A = N_HEADS_A * 2 * DH_A
W_A = N_HEADS_A * DV_A
ROPE_BASE = 10000.0
N_HEADS_H = 4
DK_H = 128
DV_H = 128
QK_H = N_HEADS_H * DK_H
W_H = N_HEADS_H * DV_H
D_FF = 2816
MIX_IN = 2 * QK_A + W_A + 3 * QK_H + 2 * W_H + 2 * D_MODEL
ALPHA = (2 * DEPTH) ** 0.25
LN_EPS = 1e-5
RMS_EPS = 1e-6

F32 = jnp.float32
BF16 = jnp.bfloat16

LANES = 128
VMEM_LIMIT = 56 * 1024 * 1024

FFN_TM = 1024
FFN_TF = 256
MIX_TM = 2048
MIX_TN = 512
MERGE_TM = 512
ATT_TQ = 256
HGRN_C = 64
HGRN_ROWS = 256
HGRN_SAFE_DECAY = 75.0
MOD_ROWS = 8

W_Q, W_K, W_V, W_HQ, W_HFF, W_HFB, W_HI, W_HG, W_GA0, W_GA1, W_GH0, W_GH1 = range(12)
HGRN_BLOCKS = (W_HQ, W_HFF, W_HFB, W_HI, W_HG)
GATE_BLOCKS = (W_GA0, W_GA1, W_GH0, W_GH1)


def _params(sem):
    return pltpu.CompilerParams(dimension_semantics=sem, vmem_limit_bytes=VMEM_LIMIT)


def _nt_dot(a, b):
    return lax.dot_general(a, b, (((1,), (1,)), ((), ())), preferred_element_type=F32)


def _tn_dot(a, b):
    return lax.dot_general(a, b, (((0,), (0,)), ((), ())), preferred_element_type=F32)


def _layer_norm(y, g, b):
    mu = jnp.mean(y, axis=-1, keepdims=True)
    yc = y - mu
    var = jnp.mean(yc * yc, axis=-1, keepdims=True)
    return yc * lax.rsqrt(var + LN_EPS) * g + b


def _mod_kernel(c_ref, w_ref, b_ref, o_ref):
    c = c_ref[...]
    a = (c * jax.nn.sigmoid(c)).astype(BF16)
    o_ref[...] = jnp.dot(a, w_ref[...].astype(BF16), preferred_element_type=F32) + b_ref[...]


def _modulation(cond, w_ada, b_ada):
    n_out = w_ada.shape[1]
    tn = D_MODEL
    out = pl.pallas_call(
        _mod_kernel,
        grid=(n_out // tn,),
        in_specs=[
            pl.BlockSpec((MOD_ROWS, D_MODEL), lambda n: (0, 0)),
            pl.BlockSpec((D_MODEL, tn), lambda n: (0, n)),
            pl.BlockSpec((1, tn), lambda n: (0, n)),
        ],
        out_specs=pl.BlockSpec((MOD_ROWS, tn), lambda n: (0, n)),
        out_shape=jax.ShapeDtypeStruct((MOD_ROWS, n_out), F32),
        compiler_params=_params(("arbitrary",)),
        name="modulation",
    )(cond, w_ada, b_ada.reshape(1, n_out))
    return out.reshape(MOD_ROWS, 9, D_MODEL)


def _ffn_kernel(x_ref, mod_ref, wg_ref, wu_ref, wo_ref, lng_ref, lnb_ref, *rest, sub, sub_next):
    if sub_next is None:
        xo_ref, h_scr, acc_scr = rest
        ho_ref = None
    else:
        xo_ref, ho_ref, h_scr, acc_scr = rest
    j = pl.program_id(1)

    @pl.when(j == 0)
    def _():
        shift = mod_ref[0, 3 * sub:3 * sub + 1, :]
        scale = mod_ref[0, 3 * sub + 1:3 * sub + 2, :]
        h_scr[...] = (x_ref[...] * (1.0 + scale) + shift).astype(BF16)
        acc_scr[...] = jnp.zeros_like(acc_scr)

    h = h_scr[...]
    gate = jnp.dot(h, wg_ref[...].astype(BF16), preferred_element_type=F32)
    up = jnp.dot(h, wu_ref[...].astype(BF16), preferred_element_type=F32)
    act = (gate * jax.nn.sigmoid(gate) * up).astype(BF16)
    acc_scr[...] += jnp.dot(act, wo_ref[...].astype(BF16), preferred_element_type=F32)

    @pl.when(j == pl.num_programs(1) - 1)
    def _():
        g = mod_ref[0, 3 * sub + 2:3 * sub + 3, :]
        y = ALPHA * x_ref[...] + 0.5 * g * acc_scr[...]
        xn = _layer_norm(y, lng_ref[sub:sub + 1, :], lnb_ref[sub:sub + 1, :])
        xo_ref[...] = xn
        if ho_ref is not None:
            shift = mod_ref[0, 3 * sub_next:3 * sub_next + 1, :]
            scale = mod_ref[0, 3 * sub_next + 1:3 * sub_next + 2, :]
            ho_ref[...] = (xn * (1.0 + scale) + shift).astype(BF16)


def _ffn(x, mod, cond_of_tile, w_in, w_out, ln_g, ln_b, sub, sub_next):
    m = x.shape[0]
    nj = D_FF // FFN_TF
    row = lambda i, j: (i, 0)
    out_shape = [jax.ShapeDtypeStruct((m, D_MODEL), F32)]
    out_specs = [pl.BlockSpec((FFN_TM, D_MODEL), row)]
    if sub_next is not None:
        out_shape.append(jax.ShapeDtypeStruct((m, D_MODEL), BF16))
        out_specs.append(pl.BlockSpec((FFN_TM, D_MODEL), row))
    return pl.pallas_call(
        functools.partial(_ffn_kernel, sub=sub, sub_next=sub_next),
        grid=(m // FFN_TM, nj),
        in_specs=[
            pl.BlockSpec((FFN_TM, D_MODEL), row),
            pl.BlockSpec((1, 9, D_MODEL), lambda i, j: (cond_of_tile(i, FFN_TM), 0, 0)),
            pl.BlockSpec((D_MODEL, FFN_TF), lambda i, j: (0, j)),
            pl.BlockSpec((D_MODEL, FFN_TF), lambda i, j: (0, nj + j)),
            pl.BlockSpec((FFN_TF, D_MODEL), lambda i, j: (j, 0)),
            pl.BlockSpec((3, D_MODEL), lambda i, j: (0, 0)),
            pl.BlockSpec((3, D_MODEL), lambda i, j: (0, 0)),
        ],
        out_specs=out_specs,
        out_shape=out_shape,
        scratch_shapes=[pltpu.VMEM((FFN_TM, D_MODEL), BF16), pltpu.VMEM((FFN_TM, D_MODEL), F32)],
        compiler_params=_params(("parallel", "arbitrary")),
        name=f"ffn{sub}",
    )(x, mod, w_in, w_in, w_out, ln_g, ln_b)


def _rope(x, cos, sin):
    lane = lax.broadcasted_iota(jnp.int32, (1, LANES), 1)
    first_half = (lane % 32) < 16
    outs = []
    for hb in range(QK_A // LANES):
        xb = x[:, hb * LANES:(hb + 1) * LANES]
        ahead = pltpu.roll(xb, LANES - 16, 1)
        behind = pltpu.roll(xb, 16, 1)
        rot = jnp.where(first_half, -ahead, behind)
        outs.append(xb * cos + rot * sin)
    return jnp.concatenate(outs, axis=-1)


def _proj_kernel(*refs, rope, sigmoid, scale_first, outs):
    if rope:
        h_ref, w_ref, cos_ref, sin_ref = refs[:4]
        refs = refs[4:]
    else:
        h_ref, w_ref = refs[:2]
        refs = refs[2:]
    out_refs, w_scr = refs[:-1], refs[-1]

    @pl.when(pl.program_id(1) == 0)
    def _():
        w_scr[...] = w_ref[...].astype(BF16)

    y = jnp.dot(h_ref[...], w_scr[...], preferred_element_type=F32)
    if rope:
        y = _rope(y, cos_ref[0], sin_ref[0])
    if scale_first:
        y = y * jnp.where(pl.program_id(0) == 0, DH_A ** -0.5, 1.0)
    if sigmoid:
        y = jax.nn.sigmoid(y)
    for ref, kind in zip(out_refs, outs):
        if kind == 'head':
            for hh in range(MIX_TN // LANES):
                ref[hh] = y[:, hh * LANES:(hh + 1) * LANES].astype(ref.dtype)
        elif kind == 'stack':
            ref[0] = y.astype(ref.dtype)
        else:
            ref[...] = y.astype(ref.dtype)


def _project(h, w, first_block, n_blocks, outs, rope_tables=None, sigmoid=False, scale_first=False, name="proj"):
    m = h.shape[0]
    rope = rope_tables is not None
    heads = MIX_TN // LANES
    in_specs = [pl.BlockSpec((MIX_TM, D_MODEL), lambda n, i: (i, 0)),
                pl.BlockSpec((D_MODEL, MIX_TN), lambda n, i: (0, first_block + n))]
    args = [h, w]
    if rope:
        n_rope_tiles = rope_tables[0].shape[1] // MIX_TM
        for t in rope_tables:
            in_specs.append(pl.BlockSpec((1, MIX_TM, DV_A), lambda n, i: (n, i % n_rope_tiles, 0)))
            args.append(t)
    specs = {
        'head': (pl.BlockSpec((heads, MIX_TM, LANES), lambda n, i: (n, i, 0)), (n_blocks * heads, m, LANES)),
        'tok': (pl.BlockSpec((MIX_TM, MIX_TN), lambda n, i: (i, n)), (m, n_blocks * MIX_TN)),
        'stack': (pl.BlockSpec((1, MIX_TM, MIX_TN), lambda n, i: (n, i, 0)), (n_blocks, m, MIX_TN)),
    }
    return pl.pallas_call(
        functools.partial(_proj_kernel, rope=rope, sigmoid=sigmoid, scale_first=scale_first,
                          outs=tuple(kind for kind, _ in outs)),
        grid=(n_blocks, m // MIX_TM),
        in_specs=in_specs,
        out_specs=[specs[kind][0] for kind, _ in outs],
        out_shape=[jax.ShapeDtypeStruct(specs[kind][1], dt) for kind, dt in outs],
        scratch_shapes=[pltpu.VMEM((D_MODEL, MIX_TN), BF16)],
        compiler_params=_params(("arbitrary", "arbitrary")),
        name=name,
    )(*args)


def _attn_kernel(*refs, lam_init, has_ctx):
    if has_ctx:
        lam_ref, g_ref, q_ref, k_ref, v_ref, kc_ref, vc_ref, o_ref = refs
    else:
        lam_ref, g_ref, q_ref, k_ref, v_ref, o_ref = refs
    lp = lam_ref[...]
    lam = (jnp.exp(jnp.sum(lp[0:1] * lp[1:2], axis=-1, keepdims=True))
           - jnp.exp(jnp.sum(lp[2:3] * lp[3:4], axis=-1, keepdims=True)) + lam_init)

    q = q_ref[0, 0]
    lane = lax.broadcasted_iota(jnp.int32, (1, DV_A), 1)
    comp0 = lane < DH_A
    zero = jnp.zeros_like(q)
    qs = (jnp.where(comp0, q, zero), jnp.where(comp0, zero, q))
    keys = [k_ref[0, 0]]
    vals = [v_ref[0, 0]]
    if has_ctx:
        keys.append(kc_ref[0].astype(BF16))
        vals.append(vc_ref[0].astype(BF16))

    probs = []
    for qc in qs:
        ss = [_nt_dot(qc, kk) for kk in keys]
        mx = functools.reduce(jnp.maximum, [jnp.max(s, axis=-1, keepdims=True) for s in ss])
        es = [jnp.exp(s - mx) for s in ss]
        den = functools.reduce(jnp.add, [jnp.sum(e, axis=-1, keepdims=True) for e in es])
        probs.append((es, 1.0 / den))
    (e0, r0), (e1, r1) = probs
    r1 = lam * r1
    o = None
    for a, b, vv in zip(e0, e1, vals):
        part = jnp.dot((a * r0 - b * r1).astype(BF16), vv, preferred_element_type=F32)
        o = part if o is None else o + part
    o = o * lax.rsqrt(jnp.mean(o * o, axis=-1, keepdims=True) + RMS_EPS) * g_ref[...]
    o_ref[0, 0] = (o * (1.0 - lam_init)).astype(o_ref.dtype)


def _attention(q, k, v, ctx_kv, lam_params, subln_g, lam_init):
    (q_arr, q_row), (k_arr, k_row), (v_arr, v_row) = q, k, v
    _, bsz, t, _ = q_arr.shape
    tq = min(ATT_TQ, t)
    const = lambda b, h, i: (0, 0)
    in_specs = [
        pl.BlockSpec((4, DH_A), const),
        pl.BlockSpec((1, DV_A), const),
        pl.BlockSpec((1, 1, tq, DV_A), lambda b, h, i: (q_row + h, b, i, 0)),
        pl.BlockSpec((1, 1, t, DV_A), lambda b, h, i: (k_row + h, b, 0, 0)),
        pl.BlockSpec((1, 1, t, DV_A), lambda b, h, i: (v_row + h, b, 0, 0)),
    ]
    args = [lam_params, subln_g, q_arr, k_arr, v_arr]
    if ctx_kv is not None:
        p = ctx_kv[0].shape[1]
        head_all = lambda b, h, i: (b, 0, h)
        in_specs += [pl.BlockSpec((1, p, DV_A), head_all), pl.BlockSpec((1, p, DV_A), head_all)]
        args += list(ctx_kv)
    return pl.pallas_call(
        functools.partial(_attn_kernel, lam_init=lam_init, has_ctx=ctx_kv is not None),
        grid=(bsz, N_HEADS_A, t // tq),
        in_specs=in_specs,
        out_specs=pl.BlockSpec((1, 1, tq, DV_A), lambda b, h, i: (h, b, i, 0)),
        out_shape=jax.ShapeDtypeStruct((N_HEADS_A, bsz, t, DV_A), BF16),
        compiler_params=_params(("parallel", "parallel", "arbitrary")),
        name="attention",
    )(*args)


def _chunk_scan(g, reverse):
    r = g.shape[0]
    pos = lax.broadcasted_iota(jnp.int32, (r, 1), 0) % HGRN_C
    b = g
    sh = 1
    while sh < HGRN_C:
        if reverse:
            b = b + jnp.where(pos < HGRN_C - sh, pltpu.roll(b, r - sh, 0), 0.0)
        else:
            b = b + jnp.where(pos >= sh, pltpu.roll(b, sh, 0), 0.0)
        sh *= 2
    return b


def _hgrn_kernel(*refs, seq_len, has_s0, want_state):
    refs = list(refs)
    hq_ref, hff_ref, hfb_ref, hi_ref, hg_ref, lbl_ref, ng_ref = refs[:7]
    refs = refs[7:]
    s0_ref = refs.pop(0) if has_s0 else None
    o_ref = refs.pop(0)
    sfin_ref = refs.pop(0) if want_state else None
    g_scr, kd_scr, q_scr, od_scr, st_scr, b_scr, ok_scr = refs
    c = HGRN_C
    rb = min(HGRN_ROWS, seq_len)
    nb = seq_len // rb
    cpb = rb // c
    z_refs = (hff_ref, hfb_ref)

    def lower_bound(d):
        logits = lbl_ref[d]
        mx = jnp.max(logits, axis=0, keepdims=True)
        e = jnp.exp(logits - mx)
        return e[0:1] / jnp.sum(e, axis=0, keepdims=True)

    lbs = [lower_bound(0), lower_bound(1)]

    def gates(i, carry):
        rows = pl.ds(pl.multiple_of(i * rb, rb), rb)
        hq = hq_ref[0, 0, rows, :]
        q_scr[rows, :] = hq * jax.nn.sigmoid(hq)
        for d in range(2):
            z = z_refs[d][0, 0, rows, :]
            lb = lbs[d]
            e = jnp.exp(-jnp.abs(z))
            r = 1.0 / (1.0 + e)
            er = e * r
            pos = z >= 0.0
            g = jnp.log(lb + (1.0 - lb) * jnp.where(pos, r, er))
            g_scr[d, rows, :] = g
            kd_scr[d, rows, :] = (1.0 - lb) * jnp.where(pos, er, r)
            worst = jnp.zeros((1, DK_H), F32)
            for j in range(cpb):
                worst = jnp.maximum(worst, -jnp.sum(g[j * c:(j + 1) * c], axis=0, keepdims=True))
            ok_scr[d, i] = (jnp.max(worst) <= HGRN_SAFE_DECAY).astype(jnp.int32)
        return carry

    lax.fori_loop(0, nb, gates, 0)

    for d in range(2):
        if has_s0:
            st_scr[d] = s0_ref[0, d, 0].T
        else:
            st_scr[d] = jnp.zeros((DV_H, DK_H), F32)

    row = lax.broadcasted_iota(jnp.int32, (c, 1), 0)
    col = lax.broadcasted_iota(jnp.int32, (1, c), 1)

    def fast_block(d, r0):
        reverse = d == 1
        rows = pl.ds(r0, rb)
        kd = kd_scr[d, rows, :]
        vb = hi_ref[0, 0, rows, :].astype(BF16)
        b = _chunk_scan(g_scr[d, rows, :], reverse)
        qd = (q_scr[rows, :] * jnp.exp(b)).astype(BF16)
        ku = kd * jnp.exp(-b)
        kub = ku.astype(BF16)
        causal = (row <= col) if reverse else (row >= col)
        st = st_scr[d]
        for j in (range(cpb - 1, -1, -1) if reverse else range(cpb)):
            sl = slice(j * c, (j + 1) * c)
            b_end = b[j * c:j * c + 1] if reverse else b[(j + 1) * c - 1:(j + 1) * c]
            e_end = jnp.exp(b_end)
            a = jnp.where(causal, _nt_dot(qd[sl], kub[sl]), 0.0).astype(BF16)
            o = jnp.dot(a, vb[sl], preferred_element_type=F32) + _nt_dot(qd[sl], st.astype(BF16))
            od_scr[d, pl.ds(r0 + j * c, c), :] = o
            st = e_end * st + _tn_dot(vb[sl], (ku[sl] * e_end).astype(BF16))
        st_scr[d] = st

    def exact_chunk(d, r0):
        reverse = d == 1
        rows = pl.ds(r0, c)
        b = _chunk_scan(g_scr[d, rows, :], reverse)
        b_scr[...] = b

        def one_row(t, carry):
            bt = b_scr[pl.ds(t, 1), :]
            qt = q_scr[pl.ds(r0 + t, 1), :]
            seen = (row >= t) if reverse else (row <= t)
            w = jnp.exp(jnp.where(seen, bt - b_scr[...], -jnp.inf))
            p = jnp.sum(qt * kd_scr[d, rows, :] * w, axis=-1, keepdims=True)
            od_scr[d, pl.ds(r0 + t, 1), :] = jnp.sum(p * hi_ref[0, 0, rows, :], axis=0, keepdims=True)
            return carry

        lax.fori_loop(0, c, one_row, 0)
        b_end = b[0:1] if reverse else b[c - 1:c]
        st = st_scr[d]
        qd = (q_scr[rows, :] * jnp.exp(b)).astype(BF16)
        od_scr[d, rows, :] = od_scr[d, rows, :] + _nt_dot(qd, st.astype(BF16))
        k_end = (kd_scr[d, rows, :] * jnp.exp(b_end - b)).astype(BF16)
        st_scr[d] = jnp.exp(b_end) * st + _tn_dot(hi_ref[0, 0, rows, :].astype(BF16), k_end)

    def block(i, carry):
        starts = (pl.multiple_of(i * rb, rb), pl.multiple_of((nb - 1 - i) * rb, rb))
        mild = (ok_scr[0, i] + ok_scr[1, nb - 1 - i]) == 2

        @pl.when(mild)
        def _():
            for d in range(2):
                fast_block(d, starts[d])

        @pl.when(jnp.logical_not(mild))
        def _():
            for d in range(2):
                for j in (range(cpb - 1, -1, -1) if d == 1 else range(cpb)):
                    exact_chunk(d, pl.multiple_of(starts[d] + j * c, c))

        return carry

    lax.fori_loop(0, nb, block, 0)

    def finish(i, carry):
        rows = pl.ds(pl.multiple_of(i * rb, rb), rb)
        o = od_scr[0, rows, :] + od_scr[1, rows, :]
        o = o * lax.rsqrt(jnp.mean(o * o, axis=-1, keepdims=True) + RMS_EPS) * ng_ref[...]
        hg = hg_ref[0, 0, rows, :]
        o_ref[0, 0, rows, :] = (o * (hg * jax.nn.sigmoid(hg))).astype(o_ref.dtype)
        return carry

    lax.fori_loop(0, nb, finish, 0)
    if want_state:
        for d in range(2):
            sfin_ref[0, d, 0] = st_scr[d].T


def _hgrn(streams, lb_logits, norm_g, s0, want_state):
    _, bsz, t, _ = streams.shape
    stream = lambda s: pl.BlockSpec((1, 1, t, DK_H), lambda b, h: (s * N_HEADS_H + h, b, 0, 0))
    in_specs = [stream(HGRN_BLOCKS.index(w)) for w in (W_HQ, W_HFF, W_HFB, W_HI, W_HG)]
    in_specs += [
        pl.BlockSpec((2, DEPTH + 1, DK_H), lambda b, h: (0, 0, h)),
        pl.BlockSpec((1, DV_H), lambda b, h: (0, 0)),
    ]
    args = [streams] * 5 + [lb_logits, norm_g]
    state_spec = pl.BlockSpec((1, 2, 1, DK_H, DV_H), lambda b, h: (b, 0, h, 0, 0))
    if s0 is not None:
        in_specs.append(state_spec)
        args.append(s0)
    out_specs = [pl.BlockSpec((1, 1, t, DV_H), lambda b, h: (h, b, 0, 0))]
    out_shape = [jax.ShapeDtypeStruct((N_HEADS_H, bsz, t, DV_H), BF16)]
    if want_state:
        out_specs.append(state_spec)
        out_shape.append(jax.ShapeDtypeStruct((bsz, 2, N_HEADS_H, DK_H, DV_H), F32))
    n_blocks = t // min(HGRN_ROWS, t)
    outs = pl.pallas_call(
        functools.partial(_hgrn_kernel, seq_len=t, has_s0=s0 is not None, want_state=want_state),
        grid=(bsz, N_HEADS_H),
        in_specs=in_specs,
        out_specs=out_specs,
        out_shape=out_shape,
        scratch_shapes=[
            pltpu.VMEM((2, t, DK_H), F32),
            pltpu.VMEM((2, t, DK_H), F32),
            pltpu.VMEM((t, DK_H), F32),
            pltpu.VMEM((2, t, DV_H), F32),
            pltpu.VMEM((2, DV_H, DK_H), F32),
            pltpu.VMEM((HGRN_C, DK_H), F32),
            pltpu.SMEM((2, n_blocks), jnp.int32),
        ],
        compiler_params=_params(("parallel", "parallel")),
        name="hgrn",
    )(*args)
    return outs if want_state else (outs[0], None)


def _merge_kernel(oa_ref, oh_ref, sa_ref, sh_ref, x_ref, mod_ref, wa_ref, wh_ref, wo_ref, lng_ref, lnb_ref,
                  xo_ref, ho_ref, wa_scr, wh_scr, wo_scr):
    @pl.when(pl.program_id(0) == 0)
    def _():
        wa_scr[...] = wa_ref[...].astype(BF16)
        wh_scr[...] = wh_ref[...].astype(BF16)
        wo_scr[...] = wo_ref[...].astype(BF16)

    heads = lambda ref: jnp.concatenate([ref[hh] for hh in range(ref.shape[0])], axis=-1)
    ba = jnp.dot(heads(oa_ref), wa_scr[...], preferred_element_type=F32)
    bh = jnp.dot(heads(oh_ref), wh_scr[...], preferred_element_type=F32)
    merged = sa_ref[...].astype(F32) * ba + sh_ref[...].astype(F32) * bh
    mix = jnp.dot(merged.astype(BF16), wo_scr[...], preferred_element_type=F32)
    y = ALPHA * x_ref[...] + mod_ref[0, 5:6, :] * mix
    xn = _layer_norm(y, lng_ref[1:2, :], lnb_ref[1:2, :])
    xo_ref[...] = xn
    ho_ref[...] = (xn * (1.0 + mod_ref[0, 7:8, :]) + mod_ref[0, 6:7, :]).astype(BF16)


def _merge(o_a, o_h, gates, x, mod, cond_of_tile, w_a, w_h, w_o, ln_g, ln_b):
    m = x.shape[0]
    row = lambda i: (i, 0)
    const = lambda i: (0, 0)
    per_head = lambda i: (0, i, 0)
    return pl.pallas_call(
        _merge_kernel,
        grid=(m // MERGE_TM,),
        in_specs=[
            pl.BlockSpec((N_HEADS_A, MERGE_TM, DV_A), per_head),
            pl.BlockSpec((N_HEADS_H, MERGE_TM, DV_H), per_head),
            pl.BlockSpec((MERGE_TM, D_MODEL), lambda i: (i, 0)),
            pl.BlockSpec((MERGE_TM, D_MODEL), lambda i: (i, 1)),
            pl.BlockSpec((MERGE_TM, D_MODEL), row),
            pl.BlockSpec((1, 9, D_MODEL), lambda i: (cond_of_tile(i, MERGE_TM), 0, 0)),
            pl.BlockSpec((W_A, D_MODEL), const),
            pl.BlockSpec((W_H, D_MODEL), const),
            pl.BlockSpec((D_MODEL, D_MODEL), const),
            pl.BlockSpec((3, D_MODEL), const),
            pl.BlockSpec((3, D_MODEL), const),
        ],
        out_specs=[pl.BlockSpec((MERGE_TM, D_MODEL), row), pl.BlockSpec((MERGE_TM, D_MODEL), row)],
        out_shape=[jax.ShapeDtypeStruct((m, D_MODEL), F32), jax.ShapeDtypeStruct((m, D_MODEL), BF16)],
        scratch_shapes=[pltpu.VMEM((W_A, D_MODEL), BF16), pltpu.VMEM((W_H, D_MODEL), BF16),
                        pltpu.VMEM((D_MODEL, D_MODEL), BF16)],
        compiler_params=_params(("arbitrary",)),
        name="merge",
    )(o_a, o_h, gates, gates, x, mod, w_a, w_h, w_o, ln_g, ln_b)


def _rope_tables(n_tok):
    rows = n_tok // GRID_W
    row = np.repeat(np.arange(rows, dtype=np.float32), GRID_W)
    col = np.tile(np.arange(GRID_W, dtype=np.float32), rows)
    half = DH_A // 2
    inv = (ROPE_BASE ** (-np.arange(0, half, 2, dtype=np.float32) / half)).astype(np.float32)
    ar = row[:, None] * inv
    ac = col[:, None] * inv
    ang = np.concatenate([ar, ar, ac, ac] * 2, axis=-1)
    cos, sin = np.cos(ang).astype(np.float32), np.sin(ang).astype(np.float32)
    scale = np.float32(DH_A ** -0.5)
    return (jnp.asarray(np.stack([cos * scale, cos, np.ones_like(cos)])),
            jnp.asarray(np.stack([sin * scale, sin, np.zeros_like(sin)])))


def _layer_pass(x, mod, cond_of_tile, wts, lam_init, bsz, ctx):
    m = x.shape[0]
    t = m // bsz
    latent = ctx is not None
    per_head = lambda a: a.reshape(a.shape[0], bsz, t, a.shape[-1])
    x1, h1 = _ffn(x, mod, cond_of_tile, wts['ffn1_w_in'], wts['ffn1_w_out'], wts['ln_g'], wts['ln_b'], 0, 1)
    w_mix = wts['w_mix_in']
    if latent:
        (qkv,) = _project(h1, w_mix, W_Q, 3, [('head', BF16)], rope_tables=_rope_tables(t), name="proj_qkv")
        k = v = None
        ctx_kv = (ctx[0], ctx[1])
    else:
        qkv, qkv32 = _project(h1, w_mix, W_Q, 3, [('head', BF16), ('stack', F32)], scale_first=True,
                              name="proj_qkv")
        k, v = qkv32[W_K], qkv32[W_V]
        ctx_kv = None
    (streams,) = _project(h1, w_mix, W_HQ, len(HGRN_BLOCKS), [('head', F32)], name="proj_hgrn")
    (gates,) = _project(h1, w_mix, W_GA0, len(GATE_BLOCKS), [('tok', BF16)], sigmoid=True, name="proj_gates")
    qkv = per_head(qkv)
    q_kv = [(qkv, W_Q * N_HEADS_A), (qkv, W_K * N_HEADS_A), (qkv, W_V * N_HEADS_A)]
    o_a = _attention(*q_kv, ctx_kv, wts['lam_params'], wts['attn_subln_g'], lam_init)
    o_h, s_fin = _hgrn(per_head(streams), wts['lb_logits'], wts['hgrn_norm_g'],
                       ctx[2] if latent else None, not latent)
    x2, h2 = _merge(o_a.reshape(N_HEADS_A, m, DV_A), o_h.reshape(N_HEADS_H, m, DV_H), gates, x1, mod,
                    cond_of_tile, wts['w_branch_a'], wts['w_branch_h'], wts['w_mix_out'], wts['ln_g'], wts['ln_b'])
    (y,) = _ffn(x2, mod, cond_of_tile, wts['ffn2_w_in'], wts['ffn2_w_out'], wts['ln_g'], wts['ln_b'], 2, None)
    return y, (k, v, s_fin)


def kernel(x_prompt, x_sample, cache_k, cache_v, state_hgrn, c, c_ctx, w_ada, b_ada, ffn1_w_in, ffn1_w_out,
           w_mix_in, lambda_q1, lambda_k1, lambda_q2, lambda_k2, attn_subln_g, hgrn_lb_logits, hgrn_norm_g,
           w_branch_a, w_branch_h, w_mix_out, ffn2_w_in, ffn2_w_out, ln_g, ln_b):
    batch, seq, _ = x_prompt.shape
    dec_batch, dec_seq, _ = x_sample.shape
    past = cache_k.shape[2]
    assert DEPTH == 1 and dec_batch + 1 <= MOD_ROWS
    lam_init = 0.8 - 0.6 * math.exp(-0.3 * 0)

    cond = jnp.concatenate([c_ctx[None, :], c, jnp.zeros((MOD_ROWS - 1 - dec_batch, D_MODEL), F32)], axis=0)
    mod = _modulation(cond, w_ada[0], b_ada[0])

    wts = {
        'ffn1_w_in': ffn1_w_in[0], 'ffn1_w_out': ffn1_w_out[0], 'w_mix_in': w_mix_in[0],
        'lam_params': jnp.concatenate([lambda_q1, lambda_k1, lambda_q2, lambda_k2], axis=0),
        'attn_subln_g': attn_subln_g, 'lb_logits': hgrn_lb_logits, 'hgrn_norm_g': hgrn_norm_g,
        'w_branch_a': w_branch_a[0], 'w_branch_h': w_branch_h[0], 'w_mix_out': w_mix_out[0],
        'ffn2_w_in': ffn2_w_in[0], 'ffn2_w_out': ffn2_w_out[0], 'ln_g': ln_g[0], 'ln_b': ln_b[0],
    }

    y_p, (k_c, v_c, s_c) = _layer_pass(x_prompt.reshape(batch * seq, D_MODEL), mod, lambda i, tm: 0,
                                       wts, lam_init, batch, None)
    ctx = (cache_k[:, 0].reshape(dec_batch, past, QK_A), cache_v[:, 0].reshape(dec_batch, past, W_A),
           state_hgrn[:, 0])
    y_s, _ = _layer_pass(x_sample.reshape(dec_batch * dec_seq, D_MODEL), mod,
                         lambda i, tm: 1 + (i * tm) // dec_seq, wts, lam_init, dec_batch, ctx)

    return (y_p.reshape(batch, seq, D_MODEL),
            y_s.reshape(dec_batch, dec_seq, D_MODEL),
            k_c.reshape(batch, DEPTH, seq, N_HEADS_A, 2, DH_A),
            v_c.reshape(batch, DEPTH, seq, N_HEADS_A, DV_A),
            s_c.reshape(batch, DEPTH, 2, N_HEADS_H, DK_H, DV_H))
```

```python
import functools
import math

import numpy as np
import jax
import jax.numpy as jnp
from jax import lax
from jax.experimental import pallas as pl
from jax.experimental.pallas import tpu as pltpu

D_MODEL = 1024
DEPTH = 1
GRID_W = 64
N_HEADS_A = 4
DH_A = 64
DV_A = 2 * DH_A
QK_A = N_HEADS_A * 2 * DH_A
W_A = N_HEADS_A * DV_A
ROPE_BASE = 10000.0
N_HEADS_H = 4
DK_H = 128
DV_H = 128
QK_H = N_HEADS_H * DK_H
W_H = N_HEADS_H * DV_H
D_FF = 2816
MIX_IN = 2 * QK_A + W_A + 3 * QK_H + 2 * W_H + 2 * D_MODEL
ALPHA = (2 * DEPTH) ** 0.25
LN_EPS = 1e-5
RMS_EPS = 1e-6

F32 = jnp.float32
BF16 = jnp.bfloat16

LANES = 128
VMEM_LIMIT = 56 * 1024 * 1024

FFN_TM = 512
FFN_TF = 256
MIX_TM = 2048
MIX_TN = 512
MERGE_TM = 512
ATT_TQ = 256
HGRN_C = 64
HGRN_ROWS = 256
HGRN_SAFE_DECAY = 75.0
MOD_ROWS = 8

W_Q, W_K, W_V, W_HQ, W_HFF, W_HFB, W_HI, W_HG, W_GA0, W_GA1, W_GH0, W_GH1 = range(12)
HGRN_BLOCKS = (W_HQ, W_HFF, W_HFB, W_HI, W_HG)
GATE_BLOCKS = (W_GA0, W_GA1, W_GH0, W_GH1)


def _params(sem):
    return pltpu.CompilerParams(dimension_semantics=sem, vmem_limit_bytes=VMEM_LIMIT)


def _nt_dot(a, b):
    return lax.dot_general(a, b, (((1,), (1,)), ((), ())), preferred_element_type=F32)


def _tn_dot(a, b):
    return lax.dot_general(a, b, (((0,), (0,)), ((), ())), preferred_element_type=F32)


def _layer_norm(y, g, b):
    mu = jnp.mean(y, axis=-1, keepdims=True)
    yc = y - mu
    var = jnp.mean(yc * yc, axis=-1, keepdims=True)
    return yc * lax.rsqrt(var + LN_EPS) * g + b


def _mod_kernel(c_ref, w_ref, b_ref, o_ref):
    c = c_ref[...]
    a = (c * jax.nn.sigmoid(c)).astype(BF16)
    o_ref[...] = jnp.dot(a, w_ref[...].astype(BF16), preferred_element_type=F32) + b_ref[...]


def _modulation(cond, w_ada, b_ada):
    n_out = w_ada.shape[1]
    tn = D_MODEL
    out = pl.pallas_call(
        _mod_kernel,
        grid=(n_out // tn,),
        in_specs=[
            pl.BlockSpec((MOD_ROWS, D_MODEL), lambda n: (0, 0)),
            pl.BlockSpec((D_MODEL, tn), lambda n: (0, n)),
            pl.BlockSpec((1, tn), lambda n: (0, n)),
        ],
        out_specs=pl.BlockSpec((MOD_ROWS, tn), lambda n: (0, n)),
        out_shape=jax.ShapeDtypeStruct((MOD_ROWS, n_out), F32),
        compiler_params=_params(("arbitrary",)),
        name="modulation",
    )(cond, w_ada, b_ada.reshape(1, n_out))
    return out.reshape(MOD_ROWS, 9, D_MODEL)


def _cast_kernel(w_ref, o_ref):
    o_ref[...] = w_ref[...].astype(o_ref.dtype)


def _to_bf16(w, row_tiles=8):
    rows, cols = w.shape
    tr = rows // row_tiles
    return pl.pallas_call(
        _cast_kernel,
        grid=(row_tiles,),
        in_specs=[pl.BlockSpec((tr, cols), lambda i: (i, 0))],
        out_specs=pl.BlockSpec((tr, cols), lambda i: (i, 0)),
        out_shape=jax.ShapeDtypeStruct((rows, cols), BF16),
        compiler_params=_params(("parallel",)),
        name="to_bf16",
    )(w)


def _ffn_kernel(x_ref, mod_ref, win_ref, wout_ref, lng_ref, lnb_ref, *rest, sub, sub_next):
    if sub_next is None:
        xo_ref, act_scr = rest
        ho_ref = None
    else:
        xo_ref, ho_ref, act_scr = rest
    x = x_ref[...]
    shift = mod_ref[0, 3 * sub:3 * sub + 1, :]
    scale = mod_ref[0, 3 * sub + 1:3 * sub + 2, :]
    h = (x * (1.0 + scale) + shift).astype(BF16)
    for c in range(D_FF // FFN_TF):
        cols = slice(c * FFN_TF, (c + 1) * FFN_TF)
        up_cols = slice(D_FF + c * FFN_TF, D_FF + (c + 1) * FFN_TF)
        gate = jnp.dot(h, win_ref[:, cols], preferred_element_type=F32)
        up = jnp.dot(h, win_ref[:, up_cols], preferred_element_type=F32)
        act_scr[:, cols] = (gate * jax.nn.sigmoid(gate) * up).astype(BF16)
    f = jnp.dot(act_scr[...], wout_ref[...], preferred_element_type=F32)
    g = mod_ref[0, 3 * sub + 2:3 * sub + 3, :]
    xn = _layer_norm(ALPHA * x + 0.5 * g * f, lng_ref[sub:sub + 1, :], lnb_ref[sub:sub + 1, :])
    xo_ref[...] = xn
    if ho_ref is not None:
        shift = mod_ref[0, 3 * sub_next:3 * sub_next + 1, :]
        scale = mod_ref[0, 3 * sub_next + 1:3 * sub_next + 2, :]
        ho_ref[...] = (xn * (1.0 + scale) + shift).astype(BF16)


def _ffn(x, mod, cond_of_tile, w_in, w_out, ln_g, ln_b, sub, sub_next):
    m = x.shape[0]
    row = lambda i: (i, 0)
    const = lambda i: (0, 0)
    resident = pl.Buffered(1)
    out_shape = [jax.ShapeDtypeStruct((m, D_MODEL), F32)]
    out_specs = [pl.BlockSpec((FFN_TM, D_MODEL), row)]
    if sub_next is not None:
        out_shape.append(jax.ShapeDtypeStruct((m, D_MODEL), BF16))
        out_specs.append(pl.BlockSpec((FFN_TM, D_MODEL), row))
    return pl.pallas_call(
        functools.partial(_ffn_kernel, sub=sub, sub_next=sub_next),
        grid=(m // FFN_TM,),
        in_specs=[
            pl.BlockSpec((FFN_TM, D_MODEL), row),
            pl.BlockSpec((1, 9, D_MODEL), lambda i: (cond_of_tile(i, FFN_TM), 0, 0)),
            pl.BlockSpec((D_MODEL, 2 * D_FF), const, pipeline_mode=resident),
            pl.BlockSpec((D_FF, D_MODEL), const, pipeline_mode=resident),
            pl.BlockSpec((3, D_MODEL), const),
            pl.BlockSpec((3, D_MODEL), const),
        ],
        out_specs=out_specs,
        out_shape=out_shape,
        scratch_shapes=[pltpu.VMEM((FFN_TM, D_FF), BF16)],
        compiler_params=_params(("parallel",)),
        name=f"ffn{sub}",
    )(x, mod, w_in, w_out, ln_g, ln_b)


def _rope(x, cos, sin):
    lane = lax.broadcasted_iota(jnp.int32, (1, LANES), 1)
    first_half = (lane % 32) < 16
    outs = []
    for hb in range(QK_A // LANES):
        xb = x[:, hb * LANES:(hb + 1) * LANES]
        ahead = pltpu.roll(xb, LANES - 16, 1)
        behind = pltpu.roll(xb, 16, 1)
        rot = jnp.where(first_half, -ahead, behind)
        outs.append(xb * cos + rot * sin)
    return jnp.concatenate(outs, axis=-1)


def _proj_kernel(*refs, rope, sigmoid, scale_first, outs):
    if rope:
        h_ref, w_ref, cos_ref, sin_ref = refs[:4]
        refs = refs[4:]
    else:
        h_ref, w_ref = refs[:2]
        refs = refs[2:]
    out_refs, w_scr = refs[:-1], refs[-1]

    @pl.when(pl.program_id(1) == 0)
    def _():
        w_scr[...] = w_ref[...].astype(BF16)

    y = jnp.dot(h_ref[...], w_scr[...], preferred_element_type=F32)
    if rope:
        y = _rope(y, cos_ref[0], sin_ref[0])
    if scale_first:
        y = y * jnp.where(pl.program_id(0) == 0, DH_A ** -0.5, 1.0)
    if sigmoid:
        y = jax.nn.sigmoid(y)
    for ref, kind in zip(out_refs, outs):
        if kind == 'head':
            for hh in range(MIX_TN // LANES):
                ref[hh] = y[:, hh * LANES:(hh + 1) * LANES].astype(ref.dtype)
        elif kind == 'stack':
            ref[0] = y.astype(ref.dtype)
        else:
            ref[...] = y.astype(ref.dtype)


def _project(h, w, first_block, n_blocks, outs, rope_tables=None, sigmoid=False, scale_first=False, name="proj"):
    m = h.shape[0]
    rope = rope_tables is not None
    heads = MIX_TN // LANES
    in_specs = [pl.BlockSpec((MIX_TM, D_MODEL), lambda n, i: (i, 0)),
                pl.BlockSpec((D_MODEL, MIX_TN), lambda n, i: (0, first_block + n))]
    args = [h, w]
    if rope:
        n_rope_tiles = rope_tables[0].shape[1] // MIX_TM
        for t in rope_tables:
            in_specs.append(pl.BlockSpec((1, MIX_TM, DV_A), lambda n, i: (n, i % n_rope_tiles, 0)))
            args.append(t)
    specs = {
        'head': (pl.BlockSpec((heads, MIX_TM, LANES), lambda n, i: (n, i, 0)), (n_blocks * heads, m, LANES)),
        'tok': (pl.BlockSpec((MIX_TM, MIX_TN), lambda n, i: (i, n)), (m, n_blocks * MIX_TN)),
        'stack': (pl.BlockSpec((1, MIX_TM, MIX_TN), lambda n, i: (n, i, 0)), (n_blocks, m, MIX_TN)),
    }
    return pl.pallas_call(
        functools.partial(_proj_kernel, rope=rope, sigmoid=sigmoid, scale_first=scale_first,
                          outs=tuple(kind for kind, _ in outs)),
        grid=(n_blocks, m // MIX_TM),
        in_specs=in_specs,
        out_specs=[specs[kind][0] for kind, _ in outs],
        out_shape=[jax.ShapeDtypeStruct(specs[kind][1], dt) for kind, dt in outs],
        scratch_shapes=[pltpu.VMEM((D_MODEL, MIX_TN), BF16)],
        compiler_params=_params(("arbitrary", "arbitrary")),
        name=name,
    )(*args)


def _attn_kernel(*refs, lam_init, has_ctx):
    if has_ctx:
        lam_ref, g_ref, q_ref, k_ref, v_ref, kc_ref, vc_ref, o_ref = refs
    else:
        lam_ref, g_ref, q_ref, k_ref, v_ref, o_ref = refs
    lp = lam_ref[...]
    lam = (jnp.exp(jnp.sum(lp[0:1] * lp[1:2], axis=-1, keepdims=True))
           - jnp.exp(jnp.sum(lp[2:3] * lp[3:4], axis=-1, keepdims=True)) + lam_init)

    q = q_ref[0, 0]
    lane = lax.broadcasted_iota(jnp.int32, (1, DV_A), 1)
    comp0 = lane < DH_A
    zero = jnp.zeros_like(q)
    qs = (jnp.where(comp0, q, zero), jnp.where(comp0, zero, q))
    keys = [k_ref[0, 0]]
    vals = [v_ref[0, 0]]
    if has_ctx:
        keys.append(kc_ref[0].astype(BF16))
        vals.append(vc_ref[0].astype(BF16))

    probs = []
    for qc in qs:
        ss = [_nt_dot(qc, kk) for kk in keys]
        mx = functools.reduce(jnp.maximum, [jnp.max(s, axis=-1, keepdims=True) for s in ss])
        es = [jnp.exp(s - mx) for s in ss]
        den = functools.reduce(jnp.add, [jnp.sum(e, axis=-1, keepdims=True) for e in es])
        probs.append((es, 1.0 / den))
    (e0, r0), (e1, r1) = probs
    r1 = lam * r1
    o = None
    for a, b, vv in zip(e0, e1, vals):
        part = jnp.dot((a * r0 - b * r1).astype(BF16), vv, preferred_element_type=F32)
        o = part if o is None else o + part
    o = o * lax.rsqrt(jnp.mean(o * o, axis=-1, keepdims=True) + RMS_EPS) * g_ref[...]
    o_ref[0, 0] = (o * (1.0 - lam_init)).astype(o_ref.dtype)


def _attention(q, k, v, ctx_kv, lam_params, subln_g, lam_init):
    (q_arr, q_row), (k_arr, k_row), (v_arr, v_row) = q, k, v
    _, bsz, t, _ = q_arr.shape
    tq = min(ATT_TQ, t)
    const = lambda b, h, i: (0, 0)
    in_specs = [
        pl.BlockSpec((4, DH_A), const),
        pl.BlockSpec((1, DV_A), const),
        pl.BlockSpec((1, 1, tq, DV_A), lambda b, h, i: (q_row + h, b, i, 0)),
        pl.BlockSpec((1, 1, t, DV_A), lambda b, h, i: (k_row + h, b, 0, 0)),
        pl.BlockSpec((1, 1, t, DV_A), lambda b, h, i: (v_row + h, b, 0, 0)),
    ]
    args = [lam_params, subln_g, q_arr, k_arr, v_arr]
    if ctx_kv is not None:
        p = ctx_kv[0].shape[1]
        head_all = lambda b, h, i: (b, 0, h)
        in_specs += [pl.BlockSpec((1, p, DV_A), head_all), pl.BlockSpec((1, p, DV_A), head_all)]
        args += list(ctx_kv)
    return pl.pallas_call(
        functools.partial(_attn_kernel, lam_init=lam_init, has_ctx=ctx_kv is not None),
        grid=(bsz, N_HEADS_A, t // tq),
        in_specs=in_specs,
        out_specs=pl.BlockSpec((1, 1, tq, DV_A), lambda b, h, i: (h, b, i, 0)),
        out_shape=jax.ShapeDtypeStruct((N_HEADS_A, bsz, t, DV_A), BF16),
        compiler_params=_params(("parallel", "parallel", "arbitrary")),
        name="attention",
    )(*args)


def _chunk_scan(g, reverse):
    r = g.shape[0]
    pos = lax.broadcasted_iota(jnp.int32, (r, 1), 0) % HGRN_C
    b = g
    sh = 1
    while sh < HGRN_C:
        if reverse:
            b = b + jnp.where(pos < HGRN_C - sh, pltpu.roll(b, r - sh, 0), 0.0)
        else:
            b = b + jnp.where(pos >= sh, pltpu.roll(b, sh, 0), 0.0)
        sh *= 2
    return b


def _hgrn_kernel(*refs, seq_len, has_s0, want_state):
    refs = list(refs)
    hq_ref, hff_ref, hfb_ref, hi_ref, hg_ref, lbl_ref, ng_ref = refs[:7]
    refs = refs[7:]
    s0_ref = refs.pop(0) if has_s0 else None
    o_ref = refs.pop(0)
    sfin_ref = refs.pop(0) if want_state else None
    g_scr, kd_scr, q_scr, od_scr, st_scr, b_scr, ok_scr = refs
    c = HGRN_C
    rb = min(HGRN_ROWS, seq_len)
    nb = seq_len // rb
    cpb = rb // c
    z_refs = (hff_ref, hfb_ref)

    def lower_bound(d):
        logits = lbl_ref[d]
        mx = jnp.max(logits, axis=0, keepdims=True)
        e = jnp.exp(logits - mx)
        return e[0:1] / jnp.sum(e, axis=0, keepdims=True)

    lbs = [lower_bound(0), lower_bound(1)]

    def gates(i, carry):
        rows = pl.ds(pl.multiple_of(i * rb, rb), rb)
        hq = hq_ref[0, 0, rows, :]
        q_scr[rows, :] = hq * jax.nn.sigmoid(hq)
        for d in range(2):
            z = z_refs[d][0, 0, rows, :]
            lb = lbs[d]
            e = jnp.exp(-jnp.abs(z))
            r = 1.0 / (1.0 + e)
            er = e * r
            pos = z >= 0.0
            g = jnp.log(lb + (1.0 - lb) * jnp.where(pos, r, er))
            g_scr[d, rows, :] = g
            kd_scr[d, rows, :] = (1.0 - lb) * jnp.where(pos, er, r)
            worst = jnp.zeros((1, DK_H), F32)
            for j in range(cpb):
                worst = jnp.maximum(worst, -jnp.sum(g[j * c:(j + 1) * c], axis=0, keepdims=True))
            ok_scr[d, i] = (jnp.max(worst) <= HGRN_SAFE_DECAY).astype(jnp.int32)
        return carry

    lax.fori_loop(0, nb, gates, 0)

    for d in range(2):
        if has_s0:
            st_scr[d] = s0_ref[0, d, 0].T
        else:
            st_scr[d] = jnp.zeros((DV_H, DK_H), F32)

    row = lax.broadcasted_iota(jnp.int32, (c, 1), 0)
    col = lax.broadcasted_iota(jnp.int32, (1, c), 1)

    def fast_block(d, r0):
        reverse = d == 1
        rows = pl.ds(r0, rb)
        kd = kd_scr[d, rows, :]
        vb = hi_ref[0, 0, rows, :].astype(BF16)
        b = _chunk_scan(g_scr[d, rows, :], reverse)
        qd = (q_scr[rows, :] * jnp.exp(b)).astype(BF16)
        ku = kd * jnp.exp(-b)
        kub = ku.astype(BF16)
        causal = (row <= col) if reverse else (row >= col)
        st = st_scr[d]
        for j in (range(cpb - 1, -1, -1) if reverse else range(cpb)):
            sl = slice(j * c, (j + 1) * c)
            b_end = b[j * c:j * c + 1] if reverse else b[(j + 1) * c - 1:(j + 1) * c]
            e_end = jnp.exp(b_end)
            a = jnp.where(causal, _nt_dot(qd[sl], kub[sl]), 0.0).astype(BF16)
            o = jnp.dot(a, vb[sl], preferred_element_type=F32) + _nt_dot(qd[sl], st.astype(BF16))
            od_scr[d, pl.ds(r0 + j * c, c), :] = o
            st = e_end * st + _tn_dot(vb[sl], (ku[sl] * e_end).astype(BF16))
        st_scr[d] = st

    def exact_chunk(d, r0):
        reverse = d == 1
        rows = pl.ds(r0, c)
        b = _chunk_scan(g_scr[d, rows, :], reverse)
        b_scr[...] = b

        def one_row(t, carry):
            bt = b_scr[pl.ds(t, 1), :]
            qt = q_scr[pl.ds(r0 + t, 1), :]
            seen = (row >= t) if reverse else (row <= t)
            w = jnp.exp(jnp.where(seen, bt - b_scr[...], -jnp.inf))
            p = jnp.sum(qt * kd_scr[d, rows, :] * w, axis=-1, keepdims=True)
            od_scr[d, pl.ds(r0 + t, 1), :] = jnp.sum(p * hi_ref[0, 0, rows, :], axis=0, keepdims=True)
            return carry

        lax.fori_loop(0, c, one_row, 0)
        b_end = b[0:1] if reverse else b[c - 1:c]
        st = st_scr[d]
        qd = (q_scr[rows, :] * jnp.exp(b)).astype(BF16)
        od_scr[d, rows, :] = od_scr[d, rows, :] + _nt_dot(qd, st.astype(BF16))
        k_end = (kd_scr[d, rows, :] * jnp.exp(b_end - b)).astype(BF16)
        st_scr[d] = jnp.exp(b_end) * st + _tn_dot(hi_ref[0, 0, rows, :].astype(BF16), k_end)

    def block(i, carry):
        starts = (pl.multiple_of(i * rb, rb), pl.multiple_of((nb - 1 - i) * rb, rb))
        mild = (ok_scr[0, i] + ok_scr[1, nb - 1 - i]) == 2

        @pl.when(mild)
        def _():
            for d in range(2):
                fast_block(d, starts[d])

        @pl.when(jnp.logical_not(mild))
        def _():
            for d in range(2):
                for j in (range(cpb - 1, -1, -1) if d == 1 else range(cpb)):
                    exact_chunk(d, pl.multiple_of(starts[d] + j * c, c))

        return carry

    lax.fori_loop(0, nb, block, 0)

    def finish(i, carry):
        rows = pl.ds(pl.multiple_of(i * rb, rb), rb)
        o = od_scr[0, rows, :] + od_scr[1, rows, :]
        o = o * lax.rsqrt(jnp.mean(o * o, axis=-1, keepdims=True) + RMS_EPS) * ng_ref[...]
        hg = hg_ref[0, 0, rows, :]
        o_ref[0, 0, rows, :] = (o * (hg * jax.nn.sigmoid(hg))).astype(o_ref.dtype)
        return carry

    lax.fori_loop(0, nb, finish, 0)
    if want_state:
        for d in range(2):
            sfin_ref[0, d, 0] = st_scr[d].T


def _hgrn(streams, lb_logits, norm_g, s0, want_state):
    _, bsz, t, _ = streams.shape
    stream = lambda s: pl.BlockSpec((1, 1, t, DK_H), lambda b, h: (s * N_HEADS_H + h, b, 0, 0))
    in_specs = [stream(HGRN_BLOCKS.index(w)) for w in (W_HQ, W_HFF, W_HFB, W_HI, W_HG)]
    in_specs += [
        pl.BlockSpec((2, DEPTH + 1, DK_H), lambda b, h: (0, 0, h)),
        pl.BlockSpec((1, DV_H), lambda b, h: (0, 0)),
    ]
    args = [streams] * 5 + [lb_logits, norm_g]
    state_spec = pl.BlockSpec((1, 2, 1, DK_H, DV_H), lambda b, h: (b, 0, h, 0, 0))
    if s0 is not None:
        in_specs.append(state_spec)
        args.append(s0)
    out_specs = [pl.BlockSpec((1, 1, t, DV_H), lambda b, h: (h, b, 0, 0))]
    out_shape = [jax.ShapeDtypeStruct((N_HEADS_H, bsz, t, DV_H), BF16)]
    if want_state:
        out_specs.append(state_spec)
        out_shape.append(jax.ShapeDtypeStruct((bsz, 2, N_HEADS_H, DK_H, DV_H), F32))
    n_blocks = t // min(HGRN_ROWS, t)
    outs = pl.pallas_call(
        functools.partial(_hgrn_kernel, seq_len=t, has_s0=s0 is not None, want_state=want_state),
        grid=(bsz, N_HEADS_H),
        in_specs=in_specs,
        out_specs=out_specs,
        out_shape=out_shape,
        scratch_shapes=[
            pltpu.VMEM((2, t, DK_H), F32),
            pltpu.VMEM((2, t, DK_H), F32),
            pltpu.VMEM((t, DK_H), F32),
            pltpu.VMEM((2, t, DV_H), F32),
            pltpu.VMEM((2, DV_H, DK_H), F32),
            pltpu.VMEM((HGRN_C, DK_H), F32),
            pltpu.SMEM((2, n_blocks), jnp.int32),
        ],
        compiler_params=_params(("parallel", "parallel")),
        name="hgrn",
    )(*args)
    return outs if want_state else (outs[0], None)


def _merge_kernel(oa_ref, oh_ref, sa_ref, sh_ref, x_ref, mod_ref, wa_ref, wh_ref, wo_ref, lng_ref, lnb_ref,
                  xo_ref, ho_ref, wa_scr, wh_scr, wo_scr):
    @pl.when(pl.program_id(0) == 0)
    def _():
        wa_scr[...] = wa_ref[...].astype(BF16)
        wh_scr[...] = wh_ref[...].astype(BF16)
        wo_scr[...] = wo_ref[...].astype(BF16)

    heads = lambda ref: jnp.concatenate([ref[hh] for hh in range(ref.shape[0])], axis=-1)
    ba = jnp.dot(heads(oa_ref), wa_scr[...], preferred_element_type=F32)
    bh = jnp.dot(heads(oh_ref), wh_scr[...], preferred_element_type=F32)
    merged = sa_ref[...].astype(F32) * ba + sh_ref[...].astype(F32) * bh
    mix = jnp.dot(merged.astype(BF16), wo_scr[...], preferred_element_type=F32)
    y = ALPHA * x_ref[...] + mod_ref[0, 5:6, :] * mix
    xn = _layer_norm(y, lng_ref[1:2, :], lnb_ref[1:2, :])
    xo_ref[...] = xn
    ho_ref[...] = (xn * (1.0 + mod_ref[0, 7:8, :]) + mod_ref[0, 6:7, :]).astype(BF16)


def _merge(o_a, o_h, gates, x, mod, cond_of_tile, w_a, w_h, w_o, ln_g, ln_b):
    m = x.shape[0]
    row = lambda i: (i, 0)
    const = lambda i: (0, 0)
    per_head = lambda i: (0, i, 0)
    return pl.pallas_call(
        _merge_kernel,
        grid=(m // MERGE_TM,),
        in_specs=[
            pl.BlockSpec((N_HEADS_A, MERGE_TM, DV_A), per_head),
            pl.BlockSpec((N_HEADS_H, MERGE_TM, DV_H), per_head),
            pl.BlockSpec((MERGE_TM, D_MODEL), lambda i: (i, 0)),
            pl.BlockSpec((MERGE_TM, D_MODEL), lambda i: (i, 1)),
            pl.BlockSpec((MERGE_TM, D_MODEL), row),
            pl.BlockSpec((1, 9, D_MODEL), lambda i: (cond_of_tile(i, MERGE_TM), 0, 0)),
            pl.BlockSpec((W_A, D_MODEL), const),
            pl.BlockSpec((W_H, D_MODEL), const),
            pl.BlockSpec((D_MODEL, D_MODEL), const),
            pl.BlockSpec((3, D_MODEL), const),
            pl.BlockSpec((3, D_MODEL), const),
        ],
        out_specs=[pl.BlockSpec((MERGE_TM, D_MODEL), row), pl.BlockSpec((MERGE_TM, D_MODEL), row)],
        out_shape=[jax.ShapeDtypeStruct((m, D_MODEL), F32), jax.ShapeDtypeStruct((m, D_MODEL), BF16)],
        scratch_shapes=[pltpu.VMEM((W_A, D_MODEL), BF16), pltpu.VMEM((W_H, D_MODEL), BF16),
                        pltpu.VMEM((D_MODEL, D_MODEL), BF16)],
        compiler_params=_params(("arbitrary",)),
        name="merge",
    )(o_a, o_h, gates, gates, x, mod, w_a, w_h, w_o, ln_g, ln_b)


def _rope_tables(n_tok):
    rows = n_tok // GRID_W
    row = np.repeat(np.arange(rows, dtype=np.float32), GRID_W)
    col = np.tile(np.arange(GRID_W, dtype=np.float32), rows)
    half = DH_A // 2
    inv = (ROPE_BASE ** (-np.arange(0, half, 2, dtype=np.float32) / half)).astype(np.float32)
    ar = row[:, None] * inv
    ac = col[:, None] * inv
    ang = np.concatenate([ar, ar, ac, ac] * 2, axis=-1)
    cos, sin = np.cos(ang).astype(np.float32), np.sin(ang).astype(np.float32)
    scale = np.float32(DH_A ** -0.5)
    return (jnp.asarray(np.stack([cos * scale, cos, np.ones_like(cos)])),
            jnp.asarray(np.stack([sin * scale, sin, np.zeros_like(sin)])))


def _layer_pass(x, mod, cond_of_tile, wts, lam_init, bsz, ctx):
    m = x.shape[0]
    t = m // bsz
    latent = ctx is not None
    per_head = lambda a: a.reshape(a.shape[0], bsz, t, a.shape[-1])
    x1, h1 = _ffn(x, mod, cond_of_tile, wts['ffn1_w_in'], wts['ffn1_w_out'], wts['ln_g'], wts['ln_b'], 0, 1)
    w_mix = wts['w_mix_in']
    if latent:
        (qkv,) = _project(h1, w_mix, W_Q, 3, [('head', BF16)], rope_tables=_rope_tables(t), name="proj_qkv")
        k = v = None
        ctx_kv = (ctx[0], ctx[1])
    else:
        qkv, qkv32 = _project(h1, w_mix, W_Q, 3, [('head', BF16), ('stack', F32)], scale_first=True,
                              name="proj_qkv")
        k, v = qkv32[W_K], qkv32[W_V]
        ctx_kv = None
    (streams,) = _project(h1, w_mix, W_HQ, len(HGRN_BLOCKS), [('head', F32)], name="proj_hgrn")
    (gates,) = _project(h1, w_mix, W_GA0, len(GATE_BLOCKS), [('tok', BF16)], sigmoid=True, name="proj_gates")
    qkv = per_head(qkv)
    q_kv = [(qkv, W_Q * N_HEADS_A), (qkv, W_K * N_HEADS_A), (qkv, W_V * N_HEADS_A)]
    o_a = _attention(*q_kv, ctx_kv, wts['lam_params'], wts['attn_subln_g'], lam_init)
    o_h, s_fin = _hgrn(per_head(streams), wts['lb_logits'], wts['hgrn_norm_g'],
                       ctx[2] if latent else None, not latent)
    x2, h2 = _merge(o_a.reshape(N_HEADS_A, m, DV_A), o_h.reshape(N_HEADS_H, m, DV_H), gates, x1, mod,
                    cond_of_tile, wts['w_branch_a'], wts['w_branch_h'], wts['w_mix_out'], wts['ln_g'], wts['ln_b'])
    (y,) = _ffn(x2, mod, cond_of_tile, wts['ffn2_w_in'], wts['ffn2_w_out'], wts['ln_g'], wts['ln_b'], 2, None)
    return y, (k, v, s_fin)


def kernel(x_prompt, x_sample, cache_k, cache_v, state_hgrn, c, c_ctx, w_ada, b_ada, ffn1_w_in, ffn1_w_out,
           w_mix_in, lambda_q1, lambda_k1, lambda_q2, lambda_k2, attn_subln_g, hgrn_lb_logits, hgrn_norm_g,
           w_branch_a, w_branch_h, w_mix_out, ffn2_w_in, ffn2_w_out, ln_g, ln_b):
    batch, seq, _ = x_prompt.shape
    dec_batch, dec_seq, _ = x_sample.shape
    past = cache_k.shape[2]
    assert DEPTH == 1 and dec_batch + 1 <= MOD_ROWS
    lam_init = 0.8 - 0.6 * math.exp(-0.3 * 0)

    cond = jnp.concatenate([c_ctx[None, :], c, jnp.zeros((MOD_ROWS - 1 - dec_batch, D_MODEL), F32)], axis=0)
    mod = _modulation(cond, w_ada[0], b_ada[0])

    wts = {
        'ffn1_w_in': _to_bf16(ffn1_w_in[0]), 'ffn1_w_out': _to_bf16(ffn1_w_out[0]), 'w_mix_in': w_mix_in[0],
        'lam_params': jnp.concatenate([lambda_q1, lambda_k1, lambda_q2, lambda_k2], axis=0),
        'attn_subln_g': attn_subln_g, 'lb_logits': hgrn_lb_logits, 'hgrn_norm_g': hgrn_norm_g,
        'w_branch_a': w_branch_a[0], 'w_branch_h': w_branch_h[0], 'w_mix_out': w_mix_out[0],
        'ffn2_w_in': _to_bf16(ffn2_w_in[0]), 'ffn2_w_out': _to_bf16(ffn2_w_out[0]),
        'ln_g': ln_g[0], 'ln_b': ln_b[0],
    }

    y_p, (k_c, v_c, s_c) = _layer_pass(x_prompt.reshape(batch * seq, D_MODEL), mod, lambda i, tm: 0,
                                       wts, lam_init, batch, None)
    ctx = (cache_k[:, 0].reshape(dec_batch, past, QK_A), cache_v[:, 0].reshape(dec_batch, past, W_A),
           state_hgrn[:, 0])
    y_s, _ = _layer_pass(x_sample.reshape(dec_batch * dec_seq, D_MODEL), mod,
                         lambda i, tm: 1 + (i * tm) // dec_seq, wts, lam_init, dec_batch, ctx)

    return (y_p.reshape(batch, seq, D_MODEL),
            y_s.reshape(dec_batch, dec_seq, D_MODEL),
            k_c.reshape(batch, DEPTH, seq, N_HEADS_A, 2, DH_A),
            v_c.reshape(batch, DEPTH, seq, N_HEADS_A, DV_A),
            s_c.reshape(batch, DEPTH, 2, N_HEADS_H, DK_H, DV_H))
```

```python
import functools
import math

import numpy as np
import jax
import jax.numpy as jnp
from jax import lax
from jax.experimental import pallas as pl
from jax.experimental.pallas import tpu as pltpu

D_MODEL = 1024
DEPTH = 1
GRID_W = 64
N_HEADS_A = 4
DH_A = 64
DV_A = 2 * DH_A
QK_A = N_HEADS_A * 2 * DH_A
W_A = N_HEADS_A * DV_A
ROPE_BASE = 10000.0
N_HEADS_H = 4
DK_H = 128
DV_H = 128
QK_H = N_HEADS_H * DK_H
W_H = N_HEADS_H * DV_H
D_FF = 2816
MIX_IN = 2 * QK_A + W_A + 3 * QK_H + 2 * W_H + 2 * D_MODEL
ALPHA = (2 * DEPTH) ** 0.25
LN_EPS = 1e-5
RMS_EPS = 1e-6
Q_SCALE = DH_A ** -0.5 * math.log2(math.e)

F32 = jnp.float32
BF16 = jnp.bfloat16

LANES = 128
VMEM_LIMIT = 56 * 1024 * 1024

FFN_TM = 512
FFN_TF = 256
MIX_TM = 2048
MIX_TN = 512
MERGE_TM = 512
ATT_TQ = 512
ATT_SHORT_SEQ = 256
HGRN_C = 64
HGRN_ROWS = 256
HGRN_SAFE_DECAY = 75.0
MOD_ROWS = 8

W_Q, W_K, W_V, W_HQ, W_HFF, W_HFB, W_HI, W_HG, W_GA0, W_GA1, W_GH0, W_GH1 = range(12)
HGRN_BLOCKS = (W_HQ, W_HFF, W_HFB, W_HI, W_HG)
GATE_BLOCKS = (W_GA0, W_GA1, W_GH0, W_GH1)


def _params(sem):
    return pltpu.CompilerParams(dimension_semantics=sem, vmem_limit_bytes=VMEM_LIMIT)


def _nt_dot(a, b):
    return lax.dot_general(a, b, (((1,), (1,)), ((), ())), preferred_element_type=F32)


def _tn_dot(a, b):
    return lax.dot_general(a, b, (((0,), (0,)), ((), ())), preferred_element_type=F32)


def _layer_norm(y, g, b):
    mu = jnp.mean(y, axis=-1, keepdims=True)
    yc = y - mu
    var = jnp.mean(yc * yc, axis=-1, keepdims=True)
    return yc * lax.rsqrt(var + LN_EPS) * g + b


def _mod_kernel(c_ref, w_ref, b_ref, o_ref):
    c = c_ref[...]
    a = (c * jax.nn.sigmoid(c)).astype(BF16)
    o_ref[...] = jnp.dot(a, w_ref[...].astype(BF16), preferred_element_type=F32) + b_ref[...]


def _modulation(cond, w_ada, b_ada):
    n_out = w_ada.shape[1]
    tn = D_MODEL
    out = pl.pallas_call(
        _mod_kernel,
        grid=(n_out // tn,),
        in_specs=[
            pl.BlockSpec((MOD_ROWS, D_MODEL), lambda n: (0, 0)),
            pl.BlockSpec((D_MODEL, tn), lambda n: (0, n)),
            pl.BlockSpec((1, tn), lambda n: (0, n)),
        ],
        out_specs=pl.BlockSpec((MOD_ROWS, tn), lambda n: (0, n)),
        out_shape=jax.ShapeDtypeStruct((MOD_ROWS, n_out), F32),
        compiler_params=_params(("arbitrary",)),
        name="modulation",
    )(cond, w_ada, b_ada.reshape(1, n_out))
    return out.reshape(MOD_ROWS, 9, D_MODEL)


def _cast_kernel(w_ref, o_ref):
    o_ref[...] = w_ref[...].astype(o_ref.dtype)


def _to_bf16(w, row_tiles=8):
    rows, cols = w.shape
    tr = rows // row_tiles
    return pl.pallas_call(
        _cast_kernel,
        grid=(row_tiles,),
        in_specs=[pl.BlockSpec((tr, cols), lambda i: (i, 0))],
        out_specs=pl.BlockSpec((tr, cols), lambda i: (i, 0)),
        out_shape=jax.ShapeDtypeStruct((rows, cols), BF16),
        compiler_params=_params(("parallel",)),
        name="to_bf16",
    )(w)


def _ffn_kernel(x_ref, mod_ref, win_ref, wout_ref, lng_ref, lnb_ref, *rest, sub, sub_next):
    if sub_next is None:
        xo_ref, act_scr = rest
        ho_ref = None
    else:
        xo_ref, ho_ref, act_scr = rest
    x = x_ref[...]
    shift = mod_ref[0, 3 * sub:3 * sub + 1, :]
    scale = mod_ref[0, 3 * sub + 1:3 * sub + 2, :]
    h = (x * (1.0 + scale) + shift).astype(BF16)
    for c in range(D_FF // FFN_TF):
        cols = slice(c * FFN_TF, (c + 1) * FFN_TF)
        up_cols = slice(D_FF + c * FFN_TF, D_FF + (c + 1) * FFN_TF)
        gate = jnp.dot(h, win_ref[:, cols], preferred_element_type=F32)
        up = jnp.dot(h, win_ref[:, up_cols], preferred_element_type=F32)
        act_scr[:, cols] = (gate * jax.nn.sigmoid(gate) * up).astype(BF16)
    f = jnp.dot(act_scr[...], wout_ref[...], preferred_element_type=F32)
    g = mod_ref[0, 3 * sub + 2:3 * sub + 3, :]
    xn = _layer_norm(ALPHA * x + 0.5 * g * f, lng_ref[sub:sub + 1, :], lnb_ref[sub:sub + 1, :])
    xo_ref[...] = xn
    if ho_ref is not None:
        shift = mod_ref[0, 3 * sub_next:3 * sub_next + 1, :]
        scale = mod_ref[0, 3 * sub_next + 1:3 * sub_next + 2, :]
        ho_ref[...] = (xn * (1.0 + scale) + shift).astype(BF16)


def _ffn(x, mod, cond_of_tile, w_in, w_out, ln_g, ln_b, sub, sub_next):
    m = x.shape[0]
    row = lambda i: (i, 0)
    const = lambda i: (0, 0)
    resident = pl.Buffered(1)
    out_shape = [jax.ShapeDtypeStruct((m, D_MODEL), F32)]
    out_specs = [pl.BlockSpec((FFN_TM, D_MODEL), row)]
    if sub_next is not None:
        out_shape.append(jax.ShapeDtypeStruct((m, D_MODEL), BF16))
        out_specs.append(pl.BlockSpec((FFN_TM, D_MODEL), row))
    return pl.pallas_call(
        functools.partial(_ffn_kernel, sub=sub, sub_next=sub_next),
        grid=(m // FFN_TM,),
        in_specs=[
            pl.BlockSpec((FFN_TM, D_MODEL), row),
            pl.BlockSpec((1, 9, D_MODEL), lambda i: (cond_of_tile(i, FFN_TM), 0, 0)),
            pl.BlockSpec((D_MODEL, 2 * D_FF), const, pipeline_mode=resident),
            pl.BlockSpec((D_FF, D_MODEL), const, pipeline_mode=resident),
            pl.BlockSpec((3, D_MODEL), const),
            pl.BlockSpec((3, D_MODEL), const),
        ],
        out_specs=out_specs,
        out_shape=out_shape,
        scratch_shapes=[pltpu.VMEM((FFN_TM, D_FF), BF16)],
        compiler_params=_params(("parallel",)),
        name=f"ffn{sub}",
    )(x, mod, w_in, w_out, ln_g, ln_b)


def _rope(x, cos, sin):
    lane = lax.broadcasted_iota(jnp.int32, (1, LANES), 1)
    first_half = (lane % 32) < 16
    outs = []
    for hb in range(QK_A // LANES):
        xb = x[:, hb * LANES:(hb + 1) * LANES]
        ahead = pltpu.roll(xb, LANES - 16, 1)
        behind = pltpu.roll(xb, 16, 1)
        rot = jnp.where(first_half, -ahead, behind)
        outs.append(xb * cos + rot * sin)
    return jnp.concatenate(outs, axis=-1)


def _proj_kernel(*refs, rope, sigmoid, scale_first, outs):
    if rope:
        h_ref, w_ref, cos_ref, sin_ref = refs[:4]
        refs = refs[4:]
    else:
        h_ref, w_ref = refs[:2]
        refs = refs[2:]
    out_refs, w_scr = refs[:-1], refs[-1]

    @pl.when(pl.program_id(1) == 0)
    def _():
        w_scr[...] = w_ref[...].astype(BF16)

    y = jnp.dot(h_ref[...], w_scr[...], preferred_element_type=F32)
    if rope:
        y = _rope(y, cos_ref[0], sin_ref[0])
    if scale_first:
        y = y * jnp.where(pl.program_id(0) == 0, Q_SCALE, 1.0)
    if sigmoid:
        y = jax.nn.sigmoid(y)
    for ref, kind in zip(out_refs, outs):
        if kind == 'head':
            for hh in range(MIX_TN // LANES):
                ref[hh] = y[:, hh * LANES:(hh + 1) * LANES].astype(ref.dtype)
        elif kind == 'stack':
            ref[0] = y.astype(ref.dtype)
        else:
            ref[...] = y.astype(ref.dtype)


def _project(h, w, first_block, n_blocks, outs, rope_tables=None, sigmoid=False, scale_first=False, name="proj"):
    m = h.shape[0]
    rope = rope_tables is not None
    heads = MIX_TN // LANES
    in_specs = [pl.BlockSpec((MIX_TM, D_MODEL), lambda n, i: (i, 0)),
                pl.BlockSpec((D_MODEL, MIX_TN), lambda n, i: (0, first_block + n))]
    args = [h, w]
    if rope:
        n_rope_tiles = rope_tables[0].shape[1] // MIX_TM
        for t in rope_tables:
            in_specs.append(pl.BlockSpec((1, MIX_TM, DV_A), lambda n, i: (n, i % n_rope_tiles, 0)))
            args.append(t)
    specs = {
        'head': (pl.BlockSpec((heads, MIX_TM, LANES), lambda n, i: (n, i, 0)), (n_blocks * heads, m, LANES)),
        'tok': (pl.BlockSpec((MIX_TM, MIX_TN), lambda n, i: (i, n)), (m, n_blocks * MIX_TN)),
        'stack': (pl.BlockSpec((1, MIX_TM, MIX_TN), lambda n, i: (n, i, 0)), (n_blocks, m, MIX_TN)),
    }
    return pl.pallas_call(
        functools.partial(_proj_kernel, rope=rope, sigmoid=sigmoid, scale_first=scale_first,
                          outs=tuple(kind for kind, _ in outs)),
        grid=(n_blocks, m // MIX_TM),
        in_specs=in_specs,
        out_specs=[specs[kind][0] for kind, _ in outs],
        out_shape=[jax.ShapeDtypeStruct(specs[kind][1], dt) for kind, dt in outs],
        scratch_shapes=[pltpu.VMEM((D_MODEL, MIX_TN), BF16)],
        compiler_params=_params(("arbitrary", "arbitrary")),
        name=name,
    )(*args)


def _attn_kernel(*refs, lam_init, has_ctx):
    if has_ctx:
        lam_ref, g_ref, q_ref, k_ref, v_ref, kc_ref, vc_ref, o_ref = refs
    else:
        lam_ref, g_ref, q_ref, k_ref, v_ref, o_ref = refs
    lp = lam_ref[...]
    lam = (jnp.exp(jnp.sum(lp[0:1] * lp[1:2], axis=-1, keepdims=True))
           - jnp.exp(jnp.sum(lp[2:3] * lp[3:4], axis=-1, keepdims=True)) + lam_init)

    lane = lax.broadcasted_iota(jnp.int32, (1, DV_A), 1)
    comp0 = lane < DH_A
    for hh in range(q_ref.shape[0]):
        q = q_ref[hh, 0]
        zero = jnp.zeros_like(q)
        qs = (jnp.where(comp0, q, zero), jnp.where(comp0, zero, q))
        keys = [k_ref[hh, 0]]
        vals = [v_ref[hh, 0]]
        if has_ctx:
            keys.append(kc_ref[0, :, hh * DV_A:(hh + 1) * DV_A].astype(BF16))
            vals.append(vc_ref[0, :, hh * DV_A:(hh + 1) * DV_A].astype(BF16))
        maps = []
        for qc in qs:
            ss = [_nt_dot(qc, kk) for kk in keys]
            mx = functools.reduce(jnp.maximum, [jnp.max(s, axis=-1, keepdims=True) for s in ss])
            es = [jnp.exp2(s - mx) for s in ss]
            den = functools.reduce(jnp.add, [jnp.sum(e, axis=-1, keepdims=True) for e in es])
            ov = functools.reduce(jnp.add, [jnp.dot(e.astype(BF16), vv, preferred_element_type=F32)
                                            for e, vv in zip(es, vals)])
            maps.append(ov * (1.0 / den))
        o = maps[0] - lam * maps[1]
        o = o * lax.rsqrt(jnp.mean(o * o, axis=-1, keepdims=True) + RMS_EPS) * g_ref[...]
        o_ref[hh, 0] = (o * (1.0 - lam_init)).astype(o_ref.dtype)


def _attention(q, k, v, ctx_kv, lam_params, subln_g, lam_init):
    (q_arr, q_row), (k_arr, k_row), (v_arr, v_row) = q, k, v
    _, bsz, t, _ = q_arr.shape
    tq = min(ATT_TQ, t)
    hp = N_HEADS_A if t <= ATT_SHORT_SEQ else 1
    const = lambda b, h, i: (0, 0)
    rows = lambda first: (lambda b, h, i: (first // hp + h, b, 0, 0))
    in_specs = [
        pl.BlockSpec((4, DH_A), const),
        pl.BlockSpec((1, DV_A), const),
        pl.BlockSpec((hp, 1, tq, DV_A), lambda b, h, i: (q_row // hp + h, b, i, 0)),
        pl.BlockSpec((hp, 1, t, DV_A), rows(k_row)),
        pl.BlockSpec((hp, 1, t, DV_A), rows(v_row)),
    ]
    args = [lam_params, subln_g, q_arr, k_arr, v_arr]
    if ctx_kv is not None:
        p = ctx_kv[0].shape[1]
        head_all = lambda b, h, i: (b, 0, h)
        in_specs += [pl.BlockSpec((1, p, hp * DV_A), head_all), pl.BlockSpec((1, p, hp * DV_A), head_all)]
        args += list(ctx_kv)
    return pl.pallas_call(
        functools.partial(_attn_kernel, lam_init=lam_init, has_ctx=ctx_kv is not None),
        grid=(bsz, N_HEADS_A // hp, t // tq),
        in_specs=in_specs,
        out_specs=pl.BlockSpec((hp, 1, tq, DV_A), lambda b, h, i: (h, b, i, 0)),
        out_shape=jax.ShapeDtypeStruct((N_HEADS_A, bsz, t, DV_A), BF16),
        compiler_params=_params(("parallel", "parallel", "arbitrary")),
        name="attention",
    )(*args)


def _chunk_scan(g, reverse):
    r = g.shape[0]
    pos = lax.broadcasted_iota(jnp.int32, (r, 1), 0) % HGRN_C
    b = g
    sh = 1
    while sh < HGRN_C:
        if reverse:
            b = b + jnp.where(pos < HGRN_C - sh, pltpu.roll(b, r - sh, 0), 0.0)
        else:
            b = b + jnp.where(pos >= sh, pltpu.roll(b, sh, 0), 0.0)
        sh *= 2
    return b


def _hgrn_kernel(*refs, seq_len, has_s0, want_state):
    refs = list(refs)
    hq_ref, hff_ref, hfb_ref, hi_ref, hg_ref, lbl_ref, ng_ref = refs[:7]
    refs = refs[7:]
    s0_ref = refs.pop(0) if has_s0 else None
    o_ref = refs.pop(0)
    sfin_ref = refs.pop(0) if want_state else None
    g_scr, kd_scr, q_scr, od_scr, st_scr, b_scr, ok_scr = refs
    c = HGRN_C
    rb = min(HGRN_ROWS, seq_len)
    nb = seq_len // rb
    cpb = rb // c
    z_refs = (hff_ref, hfb_ref)

    def lower_bound(d):
        logits = lbl_ref[d]
        mx = jnp.max(logits, axis=0, keepdims=True)
        e = jnp.exp(logits - mx)
        return e[0:1] / jnp.sum(e, axis=0, keepdims=True)

    lbs = [lower_bound(0), lower_bound(1)]

    def gates(i, carry):
        rows = pl.ds(pl.multiple_of(i * rb, rb), rb)
        hq = hq_ref[0, 0, rows, :]
        q_scr[rows, :] = hq * jax.nn.sigmoid(hq)
        for d in range(2):
            z = z_refs[d][0, 0, rows, :]
            lb = lbs[d]
            e = jnp.exp(-jnp.abs(z))
            r = 1.0 / (1.0 + e)
            er = e * r
            pos = z >= 0.0
            g = jnp.log(lb + (1.0 - lb) * jnp.where(pos, r, er))
            g_scr[d, rows, :] = g
            kd_scr[d, rows, :] = (1.0 - lb) * jnp.where(pos, er, r)
            worst = jnp.zeros((1, DK_H), F32)
            for j in range(cpb):
                worst = jnp.maximum(worst, -jnp.sum(g[j * c:(j + 1) * c], axis=0, keepdims=True))
            ok_scr[d, i] = (jnp.max(worst) <= HGRN_SAFE_DECAY).astype(jnp.int32)
        return carry

    lax.fori_loop(0, nb, gates, 0)

    for d in range(2):
        if has_s0:
            st_scr[d] = s0_ref[0, d, 0].T
        else:
            st_scr[d] = jnp.zeros((DV_H, DK_H), F32)

    row = lax.broadcasted_iota(jnp.int32, (c, 1), 0)
    col = lax.broadcasted_iota(jnp.int32, (1, c), 1)

    def fast_block(d, r0):
        reverse = d == 1
        rows = pl.ds(r0, rb)
        kd = kd_scr[d, rows, :]
        vb = hi_ref[0, 0, rows, :].astype(BF16)
        b = _chunk_scan(g_scr[d, rows, :], reverse)
        qd = (q_scr[rows, :] * jnp.exp(b)).astype(BF16)
        ku = kd * jnp.exp(-b)
        kub = ku.astype(BF16)
        causal = (row <= col) if reverse else (row >= col)
        st = st_scr[d]
        for j in (range(cpb - 1, -1, -1) if reverse else range(cpb)):
            sl = slice(j * c, (j + 1) * c)
            b_end = b[j * c:j * c + 1] if reverse else b[(j + 1) * c - 1:(j + 1) * c]
            e_end = jnp.exp(b_end)
            a = jnp.where(causal, _nt_dot(qd[sl], kub[sl]), 0.0).astype(BF16)
            o = jnp.dot(a, vb[sl], preferred_element_type=F32) + _nt_dot(qd[sl], st.astype(BF16))
            od_scr[d, pl.ds(r0 + j * c, c), :] = o
            st = e_end * st + _tn_dot(vb[sl], (ku[sl] * e_end).astype(BF16))
        st_scr[d] = st

    def exact_chunk(d, r0):
        reverse = d == 1
        rows = pl.ds(r0, c)
        b = _chunk_scan(g_scr[d, rows, :], reverse)
        b_scr[...] = b

        def one_row(t, carry):
            bt = b_scr[pl.ds(t, 1), :]
            qt = q_scr[pl.ds(r0 + t, 1), :]
            seen = (row >= t) if reverse else (row <= t)
            w = jnp.exp(jnp.where(seen, bt - b_scr[...], -jnp.inf))
            p = jnp.sum(qt * kd_scr[d, rows, :] * w, axis=-1, keepdims=True)
            od_scr[d, pl.ds(r0 + t, 1), :] = jnp.sum(p * hi_ref[0, 0, rows, :], axis=0, keepdims=True)
            return carry

        lax.fori_loop(0, c, one_row, 0)
        b_end = b[0:1] if reverse else b[c - 1:c]
        st = st_scr[d]
        qd = (q_scr[rows, :] * jnp.exp(b)).astype(BF16)
        od_scr[d, rows, :] = od_scr[d, rows, :] + _nt_dot(qd, st.astype(BF16))
        k_end = (kd_scr[d, rows, :] * jnp.exp(b_end - b)).astype(BF16)
        st_scr[d] = jnp.exp(b_end) * st + _tn_dot(hi_ref[0, 0, rows, :].astype(BF16), k_end)

    def block(i, carry):
        starts = (pl.multiple_of(i * rb, rb), pl.multiple_of((nb - 1 - i) * rb, rb))
        mild = (ok_scr[0, i] + ok_scr[1, nb - 1 - i]) == 2

        @pl.when(mild)
        def _():
            for d in range(2):
                fast_block(d, starts[d])

        @pl.when(jnp.logical_not(mild))
        def _():
            for d in range(2):
                for j in (range(cpb - 1, -1, -1) if d == 1 else range(cpb)):
                    exact_chunk(d, pl.multiple_of(starts[d] + j * c, c))

        return carry

    lax.fori_loop(0, nb, block, 0)

    def finish(i, carry):
        rows = pl.ds(pl.multiple_of(i * rb, rb), rb)
        o = od_scr[0, rows, :] + od_scr[1, rows, :]
        o = o * lax.rsqrt(jnp.mean(o * o, axis=-1, keepdims=True) + RMS_EPS) * ng_ref[...]
        hg = hg_ref[0, 0, rows, :]
        o_ref[0, 0, rows, :] = (o * (hg * jax.nn.sigmoid(hg))).astype(o_ref.dtype)
        return carry

    lax.fori_loop(0, nb, finish, 0)
    if want_state:
        for d in range(2):
            sfin_ref[0, d, 0] = st_scr[d].T


def _hgrn(streams, lb_logits, norm_g, s0, want_state):
    _, bsz, t, _ = streams.shape
    stream = lambda s: pl.BlockSpec((1, 1, t, DK_H), lambda b, h: (s * N_HEADS_H + h, b, 0, 0))
    in_specs = [stream(HGRN_BLOCKS.index(w)) for w in (W_HQ, W_HFF, W_HFB, W_HI, W_HG)]
    in_specs += [
        pl.BlockSpec((2, DEPTH + 1, DK_H), lambda b, h: (0, 0, h)),
        pl.BlockSpec((1, DV_H), lambda b, h: (0, 0)),
    ]
    args = [streams] * 5 + [lb_logits, norm_g]
    state_spec = pl.BlockSpec((1, 2, 1, DK_H, DV_H), lambda b, h: (b, 0, h, 0, 0))
    if s0 is not None:
        in_specs.append(state_spec)
        args.append(s0)
    out_specs = [pl.BlockSpec((1, 1, t, DV_H), lambda b, h: (h, b, 0, 0))]
    out_shape = [jax.ShapeDtypeStruct((N_HEADS_H, bsz, t, DV_H), BF16)]
    if want_state:
        out_specs.append(state_spec)
        out_shape.append(jax.ShapeDtypeStruct((bsz, 2, N_HEADS_H, DK_H, DV_H), F32))
    n_blocks = t // min(HGRN_ROWS, t)
    outs = pl.pallas_call(
        functools.partial(_hgrn_kernel, seq_len=t, has_s0=s0 is not None, want_state=want_state),
        grid=(bsz, N_HEADS_H),
        in_specs=in_specs,
        out_specs=out_specs,
        out_shape=out_shape,
        scratch_shapes=[
            pltpu.VMEM((2, t, DK_H), F32),
            pltpu.VMEM((2, t, DK_H), F32),
            pltpu.VMEM((t, DK_H), F32),
            pltpu.VMEM((2, t, DV_H), F32),
            pltpu.VMEM((2, DV_H, DK_H), F32),
            pltpu.VMEM((HGRN_C, DK_H), F32),
            pltpu.SMEM((2, n_blocks), jnp.int32),
        ],
        compiler_params=_params(("parallel", "parallel")),
        name="hgrn",
    )(*args)
    return outs if want_state else (outs[0], None)


def _merge_kernel(oa_ref, oh_ref, sa_ref, sh_ref, x_ref, mod_ref, wa_ref, wh_ref, wo_ref, lng_ref, lnb_ref,
                  xo_ref, ho_ref, wa_scr, wh_scr, wo_scr):
    @pl.when(pl.program_id(0) == 0)
    def _():
        wa_scr[...] = wa_ref[...].astype(BF16)
        wh_scr[...] = wh_ref[...].astype(BF16)
        wo_scr[...] = wo_ref[...].astype(BF16)

    heads = lambda ref: jnp.concatenate([ref[hh] for hh in range(ref.shape[0])], axis=-1)
    ba = jnp.dot(heads(oa_ref), wa_scr[...], preferred_element_type=F32)
    bh = jnp.dot(heads(oh_ref), wh_scr[...], preferred_element_type=F32)
    merged = sa_ref[...].astype(F32) * ba + sh_ref[...].astype(F32) * bh
    mix = jnp.dot(merged.astype(BF16), wo_scr[...], preferred_element_type=F32)
    y = ALPHA * x_ref[...] + mod_ref[0, 5:6, :] * mix
    xn = _layer_norm(y, lng_ref[1:2, :], lnb_ref[1:2, :])
    xo_ref[...] = xn
    ho_ref[...] = (xn * (1.0 + mod_ref[0, 7:8, :]) + mod_ref[0, 6:7, :]).astype(BF16)


def _merge(o_a, o_h, gates, x, mod, cond_of_tile, w_a, w_h, w_o, ln_g, ln_b):
    m = x.shape[0]
    row = lambda i: (i, 0)
    const = lambda i: (0, 0)
    per_head = lambda i: (0, i, 0)
    return pl.pallas_call(
        _merge_kernel,
        grid=(m // MERGE_TM,),
        in_specs=[
            pl.BlockSpec((N_HEADS_A, MERGE_TM, DV_A), per_head),
            pl.BlockSpec((N_HEADS_H, MERGE_TM, DV_H), per_head),
            pl.BlockSpec((MERGE_TM, D_MODEL), lambda i: (i, 0)),
            pl.BlockSpec((MERGE_TM, D_MODEL), lambda i: (i, 1)),
            pl.BlockSpec((MERGE_TM, D_MODEL), row),
            pl.BlockSpec((1, 9, D_MODEL), lambda i: (cond_of_tile(i, MERGE_TM), 0, 0)),
            pl.BlockSpec((W_A, D_MODEL), const),
            pl.BlockSpec((W_H, D_MODEL), const),
            pl.BlockSpec((D_MODEL, D_MODEL), const),
            pl.BlockSpec((3, D_MODEL), const),
            pl.BlockSpec((3, D_MODEL), const),
        ],
        out_specs=[pl.BlockSpec((MERGE_TM, D_MODEL), row), pl.BlockSpec((MERGE_TM, D_MODEL), row)],
        out_shape=[jax.ShapeDtypeStruct((m, D_MODEL), F32), jax.ShapeDtypeStruct((m, D_MODEL), BF16)],
        scratch_shapes=[pltpu.VMEM((W_A, D_MODEL), BF16), pltpu.VMEM((W_H, D_MODEL), BF16),
                        pltpu.VMEM((D_MODEL, D_MODEL), BF16)],
        compiler_params=_params(("arbitrary",)),
        name="merge",
    )(o_a, o_h, gates, gates, x, mod, w_a, w_h, w_o, ln_g, ln_b)


def _rope_tables(n_tok):
    rows = n_tok // GRID_W
    row = np.repeat(np.arange(rows, dtype=np.float32), GRID_W)
    col = np.tile(np.arange(GRID_W, dtype=np.float32), rows)
    half = DH_A // 2
    inv = (ROPE_BASE ** (-np.arange(0, half, 2, dtype=np.float32) / half)).astype(np.float32)
    ar = row[:, None] * inv
    ac = col[:, None] * inv
    ang = np.concatenate([ar, ar, ac, ac] * 2, axis=-1)
    cos, sin = np.cos(ang).astype(np.float32), np.sin(ang).astype(np.float32)
    scale = np.float32(Q_SCALE)
    return (jnp.asarray(np.stack([cos * scale, cos, np.ones_like(cos)])),
            jnp.asarray(np.stack([sin * scale, sin, np.zeros_like(sin)])))


def _layer_pass(x, mod, cond_of_tile, wts, lam_init, bsz, ctx):
    m = x.shape[0]
    t = m // bsz
    latent = ctx is not None
    per_head = lambda a: a.reshape(a.shape[0], bsz, t, a.shape[-1])
    x1, h1 = _ffn(x, mod, cond_of_tile, wts['ffn1_w_in'], wts['ffn1_w_out'], wts['ln_g'], wts['ln_b'], 0, 1)
    w_mix = wts['w_mix_in']
    if latent:
        (qkv,) = _project(h1, w_mix, W_Q, 3, [('head', BF16)], rope_tables=_rope_tables(t), name="proj_qkv")
        k = v = None
        ctx_kv = (ctx[0], ctx[1])
    else:
        qkv, qkv32 = _project(h1, w_mix, W_Q, 3, [('head', BF16), ('stack', F32)], scale_first=True,
                              name="proj_qkv")
        k, v = qkv32[W_K], qkv32[W_V]
        ctx_kv = None
    (streams,) = _project(h1, w_mix, W_HQ, len(HGRN_BLOCKS), [('head', F32)], name="proj_hgrn")
    (gates,) = _project(h1, w_mix, W_GA0, len(GATE_BLOCKS), [('tok', BF16)], sigmoid=True, name="proj_gates")
    qkv = per_head(qkv)
    q_kv = [(qkv, W_Q * N_HEADS_A), (qkv, W_K * N_HEADS_A), (qkv, W_V * N_HEADS_A)]
    o_a = _attention(*q_kv, ctx_kv, wts['lam_params'], wts['attn_subln_g'], lam_init)
    o_h, s_fin = _hgrn(per_head(streams), wts['lb_logits'], wts['hgrn_norm_g'],
                       ctx[2] if latent else None, not latent)
    x2, h2 = _merge(o_a.reshape(N_HEADS_A, m, DV_A), o_h.reshape(N_HEADS_H, m, DV_H), gates, x1, mod,
                    cond_of_tile, wts['w_branch_a'], wts['w_branch_h'], wts['w_mix_out'], wts['ln_g'], wts['ln_b'])
    (y,) = _ffn(x2, mod, cond_of_tile, wts['ffn2_w_in'], wts['ffn2_w_out'], wts['ln_g'], wts['ln_b'], 2, None)
    return y, (k, v, s_fin)


def kernel(x_prompt, x_sample, cache_k, cache_v, state_hgrn, c, c_ctx, w_ada, b_ada, ffn1_w_in, ffn1_w_out,
           w_mix_in, lambda_q1, lambda_k1, lambda_q2, lambda_k2, attn_subln_g, hgrn_lb_logits, hgrn_norm_g,
           w_branch_a, w_branch_h, w_mix_out, ffn2_w_in, ffn2_w_out, ln_g, ln_b):
    batch, seq, _ = x_prompt.shape
    dec_batch, dec_seq, _ = x_sample.shape
    past = cache_k.shape[2]
    assert DEPTH == 1 and dec_batch + 1 <= MOD_ROWS
    lam_init = 0.8 - 0.6 * math.exp(-0.3 * 0)

    cond = jnp.concatenate([c_ctx[None, :], c, jnp.zeros((MOD_ROWS - 1 - dec_batch, D_MODEL), F32)], axis=0)
    mod = _modulation(cond, w_ada[0], b_ada[0])

    wts = {
        'ffn1_w_in': _to_bf16(ffn1_w_in[0]), 'ffn1_w_out': _to_bf16(ffn1_w_out[0]), 'w_mix_in': w_mix_in[0],
        'lam_params': jnp.concatenate([lambda_q1, lambda_k1, lambda_q2, lambda_k2], axis=0),
        'attn_subln_g': attn_subln_g, 'lb_logits': hgrn_lb_logits, 'hgrn_norm_g': hgrn_norm_g,
        'w_branch_a': w_branch_a[0], 'w_branch_h': w_branch_h[0], 'w_mix_out': w_mix_out[0],
        'ffn2_w_in': _to_bf16(ffn2_w_in[0]), 'ffn2_w_out': _to_bf16(ffn2_w_out[0]),
        'ln_g': ln_g[0], 'ln_b': ln_b[0],
    }

    y_p, (k_c, v_c, s_c) = _layer_pass(x_prompt.reshape(batch * seq, D_MODEL), mod, lambda i, tm: 0,
                                       wts, lam_init, batch, None)
    ctx = (cache_k[:, 0].reshape(dec_batch, past, QK_A), cache_v[:, 0].reshape(dec_batch, past, W_A),
           state_hgrn[:, 0])
    y_s, _ = _layer_pass(x_sample.reshape(dec_batch * dec_seq, D_MODEL), mod,
                         lambda i, tm: 1 + (i * tm) // dec_seq, wts, lam_init, dec_batch, ctx)

    return (y_p.reshape(batch, seq, D_MODEL),
            y_s.reshape(dec_batch, dec_seq, D_MODEL),
            k_c.reshape(batch, DEPTH, seq, N_HEADS_A, 2, DH_A),
            v_c.reshape(batch, DEPTH, seq, N_HEADS_A, DV_A),
            s_c.reshape(batch, DEPTH, 2, N_HEADS_H, DK_H, DV_H))
```

```python
import functools
import math

import numpy as np
import jax
import jax.numpy as jnp
from jax import lax
from jax.experimental import pallas as pl
from jax.experimental.pallas import tpu as pltpu

D_MODEL = 1024
DEPTH = 1
GRID_W = 64
N_HEADS_A = 4
DH_A = 64
DV_A = 2 * DH_A
QK_A = N_HEADS_A * 2 * DH_A
W_A = N_HEADS_A * DV_A
ROPE_BASE = 10000.0
N_HEADS_H = 4
DK_H = 128
DV_H = 128
QK_H = N_HEADS_H * DK_H
W_H = N_HEADS_H * DV_H
D_FF = 2816
MIX_IN = 2 * QK_A + W_A + 3 * QK_H + 2 * W_H + 2 * D_MODEL
ALPHA = (2 * DEPTH) ** 0.25
LN_EPS = 1e-5
RMS_EPS = 1e-6
Q_SCALE = DH_A ** -0.5 * math.log2(math.e)

F32 = jnp.float32
BF16 = jnp.bfloat16

LANES = 128
VMEM_LIMIT = 56 * 1024 * 1024

FFN_TM = 512
FFN_TF = 256
MIX_TM = 2048
MIX_TN = 512
MERGE_TM = 512
ATT_TQ = 512
ATT_SHORT_SEQ = 256
HGRN_C = 64
HGRN_ROWS = 512
HGRN_SHORT_HEADS = 4
HGRN_SAFE_DECAY = 75.0
MOD_ROWS = 8

W_Q, W_K, W_V, W_HQ, W_HFF, W_HFB, W_HI, W_HG, W_GA0, W_GA1, W_GH0, W_GH1 = range(12)
HGRN_BLOCKS = (W_HQ, W_HFF, W_HFB, W_HI, W_HG)
GATE_BLOCKS = (W_GA0, W_GA1, W_GH0, W_GH1)


def _params(sem):
    return pltpu.CompilerParams(dimension_semantics=sem, vmem_limit_bytes=VMEM_LIMIT)


def _nt_dot(a, b):
    return lax.dot_general(a, b, (((1,), (1,)), ((), ())), preferred_element_type=F32)


def _tn_dot(a, b):
    return lax.dot_general(a, b, (((0,), (0,)), ((), ())), preferred_element_type=F32)


def _layer_norm(y, g, b):
    mu = jnp.mean(y, axis=-1, keepdims=True)
    yc = y - mu
    var = jnp.mean(yc * yc, axis=-1, keepdims=True)
    return yc * lax.rsqrt(var + LN_EPS) * g + b


def _mod_kernel(c_ref, w_ref, b_ref, o_ref):
    c = c_ref[...]
    a = (c * jax.nn.sigmoid(c)).astype(BF16)
    o_ref[...] = jnp.dot(a, w_ref[...].astype(BF16), preferred_element_type=F32) + b_ref[...]


def _modulation(cond, w_ada, b_ada):
    n_out = w_ada.shape[1]
    tn = D_MODEL
    out = pl.pallas_call(
        _mod_kernel,
        grid=(n_out // tn,),
        in_specs=[
            pl.BlockSpec((MOD_ROWS, D_MODEL), lambda n: (0, 0)),
            pl.BlockSpec((D_MODEL, tn), lambda n: (0, n)),
            pl.BlockSpec((1, tn), lambda n: (0, n)),
        ],
        out_specs=pl.BlockSpec((MOD_ROWS, tn), lambda n: (0, n)),
        out_shape=jax.ShapeDtypeStruct((MOD_ROWS, n_out), F32),
        compiler_params=_params(("arbitrary",)),
        name="modulation",
    )(cond, w_ada, b_ada.reshape(1, n_out))
    return out.reshape(MOD_ROWS, 9, D_MODEL)


def _cast_kernel(w_ref, o_ref):
    o_ref[...] = w_ref[...].astype(o_ref.dtype)


def _to_bf16(w, row_tiles=8):
    rows, cols = w.shape
    tr = rows // row_tiles
    return pl.pallas_call(
        _cast_kernel,
        grid=(row_tiles,),
        in_specs=[pl.BlockSpec((tr, cols), lambda i: (i, 0))],
        out_specs=pl.BlockSpec((tr, cols), lambda i: (i, 0)),
        out_shape=jax.ShapeDtypeStruct((rows, cols), BF16),
        compiler_params=_params(("parallel",)),
        name="to_bf16",
    )(w)


def _ffn_kernel(x_ref, mod_ref, win_ref, wout_ref, lng_ref, lnb_ref, *rest, sub, sub_next):
    if sub_next is None:
        xo_ref, act_scr = rest
        ho_ref = None
    else:
        xo_ref, ho_ref, act_scr = rest
    x = x_ref[...]
    shift = mod_ref[0, 3 * sub:3 * sub + 1, :]
    scale = mod_ref[0, 3 * sub + 1:3 * sub + 2, :]
    h = (x * (1.0 + scale) + shift).astype(BF16)
    for c in range(D_FF // FFN_TF):
        cols = slice(c * FFN_TF, (c + 1) * FFN_TF)
        up_cols = slice(D_FF + c * FFN_TF, D_FF + (c + 1) * FFN_TF)
        gate = jnp.dot(h, win_ref[:, cols], preferred_element_type=F32)
        up = jnp.dot(h, win_ref[:, up_cols], preferred_element_type=F32)
        act_scr[:, cols] = (gate * jax.nn.sigmoid(gate) * up).astype(BF16)
    f = jnp.dot(act_scr[...], wout_ref[...], preferred_element_type=F32)
    g = mod_ref[0, 3 * sub + 2:3 * sub + 3, :]
    xn = _layer_norm(ALPHA * x + 0.5 * g * f, lng_ref[sub:sub + 1, :], lnb_ref[sub:sub + 1, :])
    xo_ref[...] = xn
    if ho_ref is not None:
        shift = mod_ref[0, 3 * sub_next:3 * sub_next + 1, :]
        scale = mod_ref[0, 3 * sub_next + 1:3 * sub_next + 2, :]
        ho_ref[...] = (xn * (1.0 + scale) + shift).astype(BF16)


def _ffn(x, mod, cond_of_tile, w_in, w_out, ln_g, ln_b, sub, sub_next):
    m = x.shape[0]
    row = lambda i: (i, 0)
    const = lambda i: (0, 0)
    resident = pl.Buffered(1)
    out_shape = [jax.ShapeDtypeStruct((m, D_MODEL), F32)]
    out_specs = [pl.BlockSpec((FFN_TM, D_MODEL), row)]
    if sub_next is not None:
        out_shape.append(jax.ShapeDtypeStruct((m, D_MODEL), BF16))
        out_specs.append(pl.BlockSpec((FFN_TM, D_MODEL), row))
    return pl.pallas_call(
        functools.partial(_ffn_kernel, sub=sub, sub_next=sub_next),
        grid=(m // FFN_TM,),
        in_specs=[
            pl.BlockSpec((FFN_TM, D_MODEL), row),
            pl.BlockSpec((1, 9, D_MODEL), lambda i: (cond_of_tile(i, FFN_TM), 0, 0)),
            pl.BlockSpec((D_MODEL, 2 * D_FF), const, pipeline_mode=resident),
            pl.BlockSpec((D_FF, D_MODEL), const, pipeline_mode=resident),
            pl.BlockSpec((3, D_MODEL), const),
            pl.BlockSpec((3, D_MODEL), const),
        ],
        out_specs=out_specs,
        out_shape=out_shape,
        scratch_shapes=[pltpu.VMEM((FFN_TM, D_FF), BF16)],
        compiler_params=_params(("parallel",)),
        name=f"ffn{sub}",
    )(x, mod, w_in, w_out, ln_g, ln_b)


def _rope(x, cos, sin):
    lane = lax.broadcasted_iota(jnp.int32, (1, LANES), 1)
    first_half = (lane % 32) < 16
    outs = []
    for hb in range(QK_A // LANES):
        xb = x[:, hb * LANES:(hb + 1) * LANES]
        ahead = pltpu.roll(xb, LANES - 16, 1)
        behind = pltpu.roll(xb, 16, 1)
        rot = jnp.where(first_half, -ahead, behind)
        outs.append(xb * cos + rot * sin)
    return jnp.concatenate(outs, axis=-1)


def _proj_kernel(*refs, rope, sigmoid, scale_first, outs):
    if rope:
        h_ref, w_ref, cos_ref, sin_ref = refs[:4]
        refs = refs[4:]
    else:
        h_ref, w_ref = refs[:2]
        refs = refs[2:]
    out_refs, w_scr = refs[:-1], refs[-1]

    @pl.when(pl.program_id(1) == 0)
    def _():
        w_scr[...] = w_ref[...].astype(BF16)

    y = jnp.dot(h_ref[...], w_scr[...], preferred_element_type=F32)
    if rope:
        y = _rope(y, cos_ref[0], sin_ref[0])
    if scale_first:
        y = y * jnp.where(pl.program_id(0) == 0, Q_SCALE, 1.0)
    if sigmoid:
        y = jax.nn.sigmoid(y)
    for ref, kind in zip(out_refs, outs):
        if kind == 'head':
            for hh in range(MIX_TN // LANES):
                ref[hh] = y[:, hh * LANES:(hh + 1) * LANES].astype(ref.dtype)
        elif kind == 'stack':
            ref[0] = y.astype(ref.dtype)
        else:
            ref[...] = y.astype(ref.dtype)


def _project(h, w, first_block, n_blocks, outs, rope_tables=None, sigmoid=False, scale_first=False, name="proj"):
    m = h.shape[0]
    rope = rope_tables is not None
    heads = MIX_TN // LANES
    in_specs = [pl.BlockSpec((MIX_TM, D_MODEL), lambda n, i: (i, 0)),
                pl.BlockSpec((D_MODEL, MIX_TN), lambda n, i: (0, first_block + n))]
    args = [h, w]
    if rope:
        n_rope_tiles = rope_tables[0].shape[1] // MIX_TM
        for t in rope_tables:
            in_specs.append(pl.BlockSpec((1, MIX_TM, DV_A), lambda n, i: (n, i % n_rope_tiles, 0)))
            args.append(t)
    specs = {
        'head': (pl.BlockSpec((heads, MIX_TM, LANES), lambda n, i: (n, i, 0)), (n_blocks * heads, m, LANES)),
        'tok': (pl.BlockSpec((MIX_TM, MIX_TN), lambda n, i: (i, n)), (m, n_blocks * MIX_TN)),
        'stack': (pl.BlockSpec((1, MIX_TM, MIX_TN), lambda n, i: (n, i, 0)), (n_blocks, m, MIX_TN)),
    }
    return pl.pallas_call(
        functools.partial(_proj_kernel, rope=rope, sigmoid=sigmoid, scale_first=scale_first,
                          outs=tuple(kind for kind, _ in outs)),
        grid=(n_blocks, m // MIX_TM),
        in_specs=in_specs,
        out_specs=[specs[kind][0] for kind, _ in outs],
        out_shape=[jax.ShapeDtypeStruct(specs[kind][1], dt) for kind, dt in outs],
        scratch_shapes=[pltpu.VMEM((D_MODEL, MIX_TN), BF16)],
        compiler_params=_params(("arbitrary", "arbitrary")),
        name=name,
    )(*args)


def _attn_kernel(*refs, lam_init, has_ctx):
    if has_ctx:
        lam_ref, g_ref, q_ref, k_ref, v_ref, kc_ref, vc_ref, o_ref = refs
    else:
        lam_ref, g_ref, q_ref, k_ref, v_ref, o_ref = refs
    lp = lam_ref[...]
    lam = (jnp.exp(jnp.sum(lp[0:1] * lp[1:2], axis=-1, keepdims=True))
           - jnp.exp(jnp.sum(lp[2:3] * lp[3:4], axis=-1, keepdims=True)) + lam_init)

    lane = lax.broadcasted_iota(jnp.int32, (1, DV_A), 1)
    comp0 = lane < DH_A
    for hh in range(q_ref.shape[0]):
        q = q_ref[hh, 0]
        zero = jnp.zeros_like(q)
        qs = (jnp.where(comp0, q, zero), jnp.where(comp0, zero, q))
        keys = [k_ref[hh, 0]]
        vals = [v_ref[hh, 0]]
        if has_ctx:
            keys.append(kc_ref[0, :, hh * DV_A:(hh + 1) * DV_A].astype(BF16))
            vals.append(vc_ref[0, :, hh * DV_A:(hh + 1) * DV_A].astype(BF16))
        maps = []
        for qc in qs:
            ss = [_nt_dot(qc, kk) for kk in keys]
            mx = functools.reduce(jnp.maximum, [jnp.max(s, axis=-1, keepdims=True) for s in ss])
            es = [jnp.exp2(s - mx) for s in ss]
            den = functools.reduce(jnp.add, [jnp.sum(e, axis=-1, keepdims=True) for e in es])
            ov = functools.reduce(jnp.add, [jnp.dot(e.astype(BF16), vv, preferred_element_type=F32)
                                            for e, vv in zip(es, vals)])
            maps.append(ov * (1.0 / den))
        o = maps[0] - lam * maps[1]
        o = o * lax.rsqrt(jnp.mean(o * o, axis=-1, keepdims=True) + RMS_EPS) * g_ref[...]
        o_ref[hh, 0] = (o * (1.0 - lam_init)).astype(o_ref.dtype)


def _attention(q, k, v, ctx_kv, lam_params, subln_g, lam_init):
    (q_arr, q_row), (k_arr, k_row), (v_arr, v_row) = q, k, v
    _, bsz, t, _ = q_arr.shape
    tq = min(ATT_TQ, t)
    hp = N_HEADS_A if t <= ATT_SHORT_SEQ else 1
    const = lambda b, h, i: (0, 0)
    rows = lambda first: (lambda b, h, i: (first // hp + h, b, 0, 0))
    in_specs = [
        pl.BlockSpec((4, DH_A), const),
        pl.BlockSpec((1, DV_A), const),
        pl.BlockSpec((hp, 1, tq, DV_A), lambda b, h, i: (q_row // hp + h, b, i, 0)),
        pl.BlockSpec((hp, 1, t, DV_A), rows(k_row)),
        pl.BlockSpec((hp, 1, t, DV_A), rows(v_row)),
    ]
    args = [lam_params, subln_g, q_arr, k_arr, v_arr]
    if ctx_kv is not None:
        p = ctx_kv[0].shape[1]
        head_all = lambda b, h, i: (b, 0, h)
        in_specs += [pl.BlockSpec((1, p, hp * DV_A), head_all), pl.BlockSpec((1, p, hp * DV_A), head_all)]
        args += list(ctx_kv)
    return pl.pallas_call(
        functools.partial(_attn_kernel, lam_init=lam_init, has_ctx=ctx_kv is not None),
        grid=(bsz, N_HEADS_A // hp, t // tq),
        in_specs=in_specs,
        out_specs=pl.BlockSpec((hp, 1, tq, DV_A), lambda b, h, i: (h, b, i, 0)),
        out_shape=jax.ShapeDtypeStruct((N_HEADS_A, bsz, t, DV_A), BF16),
        compiler_params=_params(("parallel", "parallel", "arbitrary")),
        name="attention",
    )(*args)


def _chunk_scan(g, reverse):
    r = g.shape[0]
    pos = lax.broadcasted_iota(jnp.int32, (r, 1), 0) % HGRN_C
    b = g
    sh = 1
    while sh < HGRN_C:
        if reverse:
            b = b + jnp.where(pos < HGRN_C - sh, pltpu.roll(b, r - sh, 0), 0.0)
        else:
            b = b + jnp.where(pos >= sh, pltpu.roll(b, sh, 0), 0.0)
        sh *= 2
    return b


def _hgrn_kernel(*refs, seq_len, has_s0, want_state):
    refs = list(refs)
    hq_ref, hff_ref, hfb_ref, hi_ref, hg_ref, lbl_ref, ng_ref = refs[:7]
    refs = refs[7:]
    s0_ref = refs.pop(0) if has_s0 else None
    o_ref = refs.pop(0)
    sfin_ref = refs.pop(0) if want_state else None
    g_scr, kd_scr, q_scr, od_scr, st_scr, b_scr, ok_scr = refs
    hp = hq_ref.shape[0]
    c = HGRN_C
    rb = min(HGRN_ROWS, seq_len)
    nb = seq_len // rb
    cpb = rb // c
    z_refs = (hff_ref, hfb_ref)

    def lower_bound(hh, d):
        logits = lbl_ref[d, :, hh * DK_H:(hh + 1) * DK_H]
        mx = jnp.max(logits, axis=0, keepdims=True)
        e = jnp.exp(logits - mx)
        return e[0:1] / jnp.sum(e, axis=0, keepdims=True)

    lbs = [[lower_bound(hh, d) for d in range(2)] for hh in range(hp)]

    def gates(i, carry):
        rows = pl.ds(pl.multiple_of(i * rb, rb), rb)
        for d in range(2):
            worst = jnp.zeros((1, DK_H), F32)
            for hh in range(hp):
                if d == 0:
                    hq = hq_ref[hh, 0, rows, :]
                    q_scr[hh, rows, :] = hq * jax.nn.sigmoid(hq)
                z = z_refs[d][hh, 0, rows, :]
                lb = lbs[hh][d]
                e = jnp.exp(-jnp.abs(z))
                r = 1.0 / (1.0 + e)
                er = e * r
                pos = z >= 0.0
                g = jnp.log(lb + (1.0 - lb) * jnp.where(pos, r, er))
                g_scr[hh, d, rows, :] = g
                kd_scr[hh, d, rows, :] = (1.0 - lb) * jnp.where(pos, er, r)
                for j in range(cpb):
                    worst = jnp.maximum(worst, -jnp.sum(g[j * c:(j + 1) * c], axis=0, keepdims=True))
            ok_scr[d, i] = (jnp.max(worst) <= HGRN_SAFE_DECAY).astype(jnp.int32)
        return carry

    lax.fori_loop(0, nb, gates, 0)

    for hh in range(hp):
        for d in range(2):
            if has_s0:
                st_scr[hh, d] = s0_ref[0, d, hh].T
            else:
                st_scr[hh, d] = jnp.zeros((DV_H, DK_H), F32)

    row = lax.broadcasted_iota(jnp.int32, (c, 1), 0)
    col = lax.broadcasted_iota(jnp.int32, (1, c), 1)

    def fast_block(d, r0):
        reverse = d == 1
        rows = pl.ds(r0, rb)
        n = hp * cpb
        flat = lambda a: a.reshape(hp * rb, a.shape[-1])
        chunks = lambda a: a.reshape(n, c, a.shape[-1])
        vb = chunks(hi_ref[:, 0, rows, :].astype(BF16))
        b = _chunk_scan(flat(g_scr[:, d, rows, :]), reverse)
        qd = chunks((flat(q_scr[:, rows, :]) * jnp.exp(b)).astype(BF16))
        ku = chunks(flat(kd_scr[:, d, rows, :]) * jnp.exp(-b))
        b = chunks(b)
        e_end = jnp.exp(b[:, 0:1, :] if reverse else b[:, c - 1:c, :])
        causal = ((row <= col) if reverse else (row >= col))[None]
        a = jnp.einsum('nck,nsk->ncs', qd, ku.astype(BF16), preferred_element_type=F32)
        o = jnp.einsum('ncs,nsv->ncv', jnp.where(causal, a, 0.0).astype(BF16), vb, preferred_element_type=F32)
        u = jnp.einsum('nsv,nsk->nvk', vb, (ku * e_end).astype(BF16), preferred_element_type=F32)
        entering = [None] * n
        for hh in range(hp):
            st = st_scr[hh, d]
            for j in (range(cpb - 1, -1, -1) if reverse else range(cpb)):
                entering[hh * cpb + j] = st.astype(BF16)
                st = e_end[hh * cpb + j] * st + u[hh * cpb + j]
            st_scr[hh, d] = st
        o = o + jnp.einsum('nck,nvk->ncv', qd, jnp.stack(entering), preferred_element_type=F32)
        od_scr[:, d, rows, :] = o.reshape(hp, rb, DV_H)

    def exact_chunk(hh, d, r0):
        reverse = d == 1
        rows = pl.ds(r0, c)
        b = _chunk_scan(g_scr[hh, d, rows, :], reverse)
        b_scr[...] = b

        def one_row(t, carry):
            bt = b_scr[pl.ds(t, 1), :]
            qt = q_scr[hh, pl.ds(r0 + t, 1), :]
            seen = (row >= t) if reverse else (row <= t)
            w = jnp.exp(jnp.where(seen, bt - b_scr[...], -jnp.inf))
            p = jnp.sum(qt * kd_scr[hh, d, rows, :] * w, axis=-1, keepdims=True)
            od_scr[hh, d, pl.ds(r0 + t, 1), :] = jnp.sum(p * hi_ref[hh, 0, rows, :], axis=0, keepdims=True)
            return carry

        lax.fori_loop(0, c, one_row, 0)
        b_end = b[0:1] if reverse else b[c - 1:c]
        st = st_scr[hh, d]
        qd = (q_scr[hh, rows, :] * jnp.exp(b)).astype(BF16)
        od_scr[hh, d, rows, :] = od_scr[hh, d, rows, :] + _nt_dot(qd, st.astype(BF16))
        k_end = (kd_scr[hh, d, rows, :] * jnp.exp(b_end - b)).astype(BF16)
        st_scr[hh, d] = jnp.exp(b_end) * st + _tn_dot(hi_ref[hh, 0, rows, :].astype(BF16), k_end)

    def block(i, carry):
        starts = (pl.multiple_of(i * rb, rb), pl.multiple_of((nb - 1 - i) * rb, rb))
        mild = (ok_scr[0, i] + ok_scr[1, nb - 1 - i]) == 2

        @pl.when(mild)
        def _():
            for d in range(2):
                fast_block(d, starts[d])

        @pl.when(jnp.logical_not(mild))
        def _():
            for hh in range(hp):
                for d in range(2):
                    for j in (range(cpb - 1, -1, -1) if d == 1 else range(cpb)):
                        exact_chunk(hh, d, pl.multiple_of(starts[d] + j * c, c))

        return carry

    lax.fori_loop(0, nb, block, 0)

    def finish(i, carry):
        rows = pl.ds(pl.multiple_of(i * rb, rb), rb)
        for hh in range(hp):
            o = od_scr[hh, 0, rows, :] + od_scr[hh, 1, rows, :]
            o = o * lax.rsqrt(jnp.mean(o * o, axis=-1, keepdims=True) + RMS_EPS) * ng_ref[...]
            hg = hg_ref[hh, 0, rows, :]
            o_ref[hh, 0, rows, :] = (o * (hg * jax.nn.sigmoid(hg))).astype(o_ref.dtype)
        return carry

    lax.fori_loop(0, nb, finish, 0)
    if want_state:
        for hh in range(hp):
            for d in range(2):
                sfin_ref[0, d, hh] = st_scr[hh, d].T


def _hgrn(streams, lb_logits, norm_g, s0, want_state):
    _, bsz, t, _ = streams.shape
    hp = HGRN_SHORT_HEADS if t <= HGRN_ROWS else 1
    stream = lambda s: pl.BlockSpec((hp, 1, t, DK_H), lambda b, h: (s * N_HEADS_H // hp + h, b, 0, 0))
    in_specs = [stream(HGRN_BLOCKS.index(w)) for w in (W_HQ, W_HFF, W_HFB, W_HI, W_HG)]
    in_specs += [
        pl.BlockSpec((2, DEPTH + 1, hp * DK_H), lambda b, h: (0, 0, h)),
        pl.BlockSpec((1, DV_H), lambda b, h: (0, 0)),
    ]
    args = [streams] * 5 + [lb_logits, norm_g]
    state_spec = pl.BlockSpec((1, 2, hp, DK_H, DV_H), lambda b, h: (b, 0, h, 0, 0))
    if s0 is not None:
        in_specs.append(state_spec)
        args.append(s0)
    out_specs = [pl.BlockSpec((hp, 1, t, DV_H), lambda b, h: (h, b, 0, 0))]
    out_shape = [jax.ShapeDtypeStruct((N_HEADS_H, bsz, t, DV_H), BF16)]
    if want_state:
        out_specs.append(state_spec)
        out_shape.append(jax.ShapeDtypeStruct((bsz, 2, N_HEADS_H, DK_H, DV_H), F32))
    n_blocks = t // min(HGRN_ROWS, t)
    outs = pl.pallas_call(
        functools.partial(_hgrn_kernel, seq_len=t, has_s0=s0 is not None, want_state=want_state),
        grid=(bsz, N_HEADS_H // hp),
        in_specs=in_specs,
        out_specs=out_specs,
        out_shape=out_shape,
        scratch_shapes=[
            pltpu.VMEM((hp, 2, t, DK_H), F32),
            pltpu.VMEM((hp, 2, t, DK_H), F32),
            pltpu.VMEM((hp, t, DK_H), F32),
            pltpu.VMEM((hp, 2, t, DV_H), F32),
            pltpu.VMEM((hp, 2, DV_H, DK_H), F32),
            pltpu.VMEM((HGRN_C, DK_H), F32),
            pltpu.SMEM((2, n_blocks), jnp.int32),
        ],
        compiler_params=_params(("parallel", "parallel")),
        name="hgrn",
    )(*args)
    return outs if want_state else (outs[0], None)


def _merge_kernel(oa_ref, oh_ref, sa_ref, sh_ref, x_ref, mod_ref, wa_ref, wh_ref, wo_ref, lng_ref, lnb_ref,
                  xo_ref, ho_ref, wa_scr, wh_scr, wo_scr):
    @pl.when(pl.program_id(0) == 0)
    def _():
        wa_scr[...] = wa_ref[...].astype(BF16)
        wh_scr[...] = wh_ref[...].astype(BF16)
        wo_scr[...] = wo_ref[...].astype(BF16)

    heads = lambda ref: jnp.concatenate([ref[hh] for hh in range(ref.shape[0])], axis=-1)
    ba = jnp.dot(heads(oa_ref), wa_scr[...], preferred_element_type=F32)
    bh = jnp.dot(heads(oh_ref), wh_scr[...], preferred_element_type=F32)
    merged = sa_ref[...].astype(F32) * ba + sh_ref[...].astype(F32) * bh
    mix = jnp.dot(merged.astype(BF16), wo_scr[...], preferred_element_type=F32)
    y = ALPHA * x_ref[...] + mod_ref[0, 5:6, :] * mix
    xn = _layer_norm(y, lng_ref[1:2, :], lnb_ref[1:2, :])
    xo_ref[...] = xn
    ho_ref[...] = (xn * (1.0 + mod_ref[0, 7:8, :]) + mod_ref[0, 6:7, :]).astype(BF16)


def _merge(o_a, o_h, gates, x, mod, cond_of_tile, w_a, w_h, w_o, ln_g, ln_b):
    m = x.shape[0]
    row = lambda i: (i, 0)
    const = lambda i: (0, 0)
    per_head = lambda i: (0, i, 0)
    return pl.pallas_call(
        _merge_kernel,
        grid=(m // MERGE_TM,),
        in_specs=[
            pl.BlockSpec((N_HEADS_A, MERGE_TM, DV_A), per_head),
            pl.BlockSpec((N_HEADS_H, MERGE_TM, DV_H), per_head),
            pl.BlockSpec((MERGE_TM, D_MODEL), lambda i: (i, 0)),
            pl.BlockSpec((MERGE_TM, D_MODEL), lambda i: (i, 1)),
            pl.BlockSpec((MERGE_TM, D_MODEL), row),
            pl.BlockSpec((1, 9, D_MODEL), lambda i: (cond_of_tile(i, MERGE_TM), 0, 0)),
            pl.BlockSpec((W_A, D_MODEL), const),
            pl.BlockSpec((W_H, D_MODEL), const),
            pl.BlockSpec((D_MODEL, D_MODEL), const),
            pl.BlockSpec((3, D_MODEL), const),
            pl.BlockSpec((3, D_MODEL), const),
        ],
        out_specs=[pl.BlockSpec((MERGE_TM, D_MODEL), row), pl.BlockSpec((MERGE_TM, D_MODEL), row)],
        out_shape=[jax.ShapeDtypeStruct((m, D_MODEL), F32), jax.ShapeDtypeStruct((m, D_MODEL), BF16)],
        scratch_shapes=[pltpu.VMEM((W_A, D_MODEL), BF16), pltpu.VMEM((W_H, D_MODEL), BF16),
                        pltpu.VMEM((D_MODEL, D_MODEL), BF16)],
        compiler_params=_params(("arbitrary",)),
        name="merge",
    )(o_a, o_h, gates, gates, x, mod, w_a, w_h, w_o, ln_g, ln_b)


def _rope_tables(n_tok):
    rows = n_tok // GRID_W
    row = np.repeat(np.arange(rows, dtype=np.float32), GRID_W)
    col = np.tile(np.arange(GRID_W, dtype=np.float32), rows)
    half = DH_A // 2
    inv = (ROPE_BASE ** (-np.arange(0, half, 2, dtype=np.float32) / half)).astype(np.float32)
    ar = row[:, None] * inv
    ac = col[:, None] * inv
    ang = np.concatenate([ar, ar, ac, ac] * 2, axis=-1)
    cos, sin = np.cos(ang).astype(np.float32), np.sin(ang).astype(np.float32)
    scale = np.float32(Q_SCALE)
    return (jnp.asarray(np.stack([cos * scale, cos, np.ones_like(cos)])),
            jnp.asarray(np.stack([sin * scale, sin, np.zeros_like(sin)])))


def _layer_pass(x, mod, cond_of_tile, wts, lam_init, bsz, ctx):
    m = x.shape[0]
    t = m // bsz
    latent = ctx is not None
    per_head = lambda a: a.reshape(a.shape[0], bsz, t, a.shape[-1])
    x1, h1 = _ffn(x, mod, cond_of_tile, wts['ffn1_w_in'], wts['ffn1_w_out'], wts['ln_g'], wts['ln_b'], 0, 1)
    w_mix = wts['w_mix_in']
    if latent:
        (qkv,) = _project(h1, w_mix, W_Q, 3, [('head', BF16)], rope_tables=_rope_tables(t), name="proj_qkv")
        k = v = None
        ctx_kv = (ctx[0], ctx[1])
    else:
        qkv, qkv32 = _project(h1, w_mix, W_Q, 3, [('head', BF16), ('stack', F32)], scale_first=True,
                              name="proj_qkv")
        k, v = qkv32[W_K], qkv32[W_V]
        ctx_kv = None
    (streams,) = _project(h1, w_mix, W_HQ, len(HGRN_BLOCKS), [('head', F32)], name="proj_hgrn")
    (gates,) = _project(h1, w_mix, W_GA0, len(GATE_BLOCKS), [('tok', BF16)], sigmoid=True, name="proj_gates")
    qkv = per_head(qkv)
    q_kv = [(qkv, W_Q * N_HEADS_A), (qkv, W_K * N_HEADS_A), (qkv, W_V * N_HEADS_A)]
    o_a = _attention(*q_kv, ctx_kv, wts['lam_params'], wts['attn_subln_g'], lam_init)
    o_h, s_fin = _hgrn(per_head(streams), wts['lb_logits'], wts['hgrn_norm_g'],
                       ctx[2] if latent else None, not latent)
    x2, h2 = _merge(o_a.reshape(N_HEADS_A, m, DV_A), o_h.reshape(N_HEADS_H, m, DV_H), gates, x1, mod,
                    cond_of_tile, wts['w_branch_a'], wts['w_branch_h'], wts['w_mix_out'], wts['ln_g'], wts['ln_b'])
    (y,) = _ffn(x2, mod, cond_of_tile, wts['ffn2_w_in'], wts['ffn2_w_out'], wts['ln_g'], wts['ln_b'], 2, None)
    return y, (k, v, s_fin)


def kernel(x_prompt, x_sample, cache_k, cache_v, state_hgrn, c, c_ctx, w_ada, b_ada, ffn1_w_in, ffn1_w_out,
           w_mix_in, lambda_q1, lambda_k1, lambda_q2, lambda_k2, attn_subln_g, hgrn_lb_logits, hgrn_norm_g,
           w_branch_a, w_branch_h, w_mix_out, ffn2_w_in, ffn2_w_out, ln_g, ln_b):
    batch, seq, _ = x_prompt.shape
    dec_batch, dec_seq, _ = x_sample.shape
    past = cache_k.shape[2]
    assert DEPTH == 1 and dec_batch + 1 <= MOD_ROWS
    lam_init = 0.8 - 0.6 * math.exp(-0.3 * 0)

    cond = jnp.concatenate([c_ctx[None, :], c, jnp.zeros((MOD_ROWS - 1 - dec_batch, D_MODEL), F32)], axis=0)
    mod = _modulation(cond, w_ada[0], b_ada[0])

    wts = {
        'ffn1_w_in': _to_bf16(ffn1_w_in[0]), 'ffn1_w_out': _to_bf16(ffn1_w_out[0]), 'w_mix_in': w_mix_in[0],
        'lam_params': jnp.concatenate([lambda_q1, lambda_k1, lambda_q2, lambda_k2], axis=0),
        'attn_subln_g': attn_subln_g, 'lb_logits': hgrn_lb_logits, 'hgrn_norm_g': hgrn_norm_g,
        'w_branch_a': w_branch_a[0], 'w_branch_h': w_branch_h[0], 'w_mix_out': w_mix_out[0],
        'ffn2_w_in': _to_bf16(ffn2_w_in[0]), 'ffn2_w_out': _to_bf16(ffn2_w_out[0]),
        'ln_g': ln_g[0], 'ln_b': ln_b[0],
    }

    y_p, (k_c, v_c, s_c) = _layer_pass(x_prompt.reshape(batch * seq, D_MODEL), mod, lambda i, tm: 0,
                                       wts, lam_init, batch, None)
    ctx = (cache_k[:, 0].reshape(dec_batch, past, QK_A), cache_v[:, 0].reshape(dec_batch, past, W_A),
           state_hgrn[:, 0])
    y_s, _ = _layer_pass(x_sample.reshape(dec_batch * dec_seq, D_MODEL), mod,
                         lambda i, tm: 1 + (i * tm) // dec_seq, wts, lam_init, dec_batch, ctx)

    return (y_p.reshape(batch, seq, D_MODEL),
            y_s.reshape(dec_batch, dec_seq, D_MODEL),
            k_c.reshape(batch, DEPTH, seq, N_HEADS_A, 2, DH_A),
            v_c.reshape(batch, DEPTH, seq, N_HEADS_A, DV_A),
            s_c.reshape(batch, DEPTH, 2, N_HEADS_H, DK_H, DV_H))
```

```python
import functools
import math

import numpy as np
import jax
import jax.numpy as jnp
from jax import lax
from jax.experimental import pallas as pl
from jax.experimental.pallas import tpu as pltpu

D_MODEL = 1024
DEPTH = 1
GRID_W = 64
N_HEADS_A = 4
DH_A = 64
DV_A = 2 * DH_A
QK_A = N_HEADS_A * 2 * DH_A
W_A = N_HEADS_A * DV_A
ROPE_BASE = 10000.0
N_HEADS_H = 4
DK_H = 128
DV_H = 128
QK_H = N_HEADS_H * DK_H
W_H = N_HEADS_H * DV_H
D_FF = 2816
MIX_IN = 2 * QK_A + W_A + 3 * QK_H + 2 * W_H + 2 * D_MODEL
ALPHA = (2 * DEPTH) ** 0.25
LN_EPS = 1e-5
RMS_EPS = 1e-6
Q_SCALE = DH_A ** -0.5 * math.log2(math.e)

F32 = jnp.float32
BF16 = jnp.bfloat16

LANES = 128
VMEM_LIMIT = 56 * 1024 * 1024

FFN_TM = 512
FFN_TF = 256
MIX_TM = 2048
MIX_TN = 512
MERGE_TM = 512
ATT_TQ = 1024
ATT_SUB = 512
ATT_SHORT_SEQ = 256
HGRN_C = 64
HGRN_ROWS = 512
HGRN_SHORT_HEADS = 4
HGRN_SAFE_DECAY = 75.0
MOD_ROWS = 8

W_Q, W_K, W_V, W_HQ, W_HFF, W_HFB, W_HI, W_HG, W_GA0, W_GA1, W_GH0, W_GH1 = range(12)
HGRN_BLOCKS = (W_HQ, W_HFF, W_HFB, W_HI, W_HG)
GATE_BLOCKS = (W_GA0, W_GA1, W_GH0, W_GH1)


def _params(sem):
    return pltpu.CompilerParams(dimension_semantics=sem, vmem_limit_bytes=VMEM_LIMIT)


def _nt_dot(a, b):
    return lax.dot_general(a, b, (((1,), (1,)), ((), ())), preferred_element_type=F32)


def _tn_dot(a, b):
    return lax.dot_general(a, b, (((0,), (0,)), ((), ())), preferred_element_type=F32)


def _layer_norm(y, g, b):
    mu = jnp.mean(y, axis=-1, keepdims=True)
    yc = y - mu
    var = jnp.mean(yc * yc, axis=-1, keepdims=True)
    return yc * lax.rsqrt(var + LN_EPS) * g + b


def _mod_kernel(c_ref, w_ref, b_ref, o_ref):
    c = c_ref[...]
    a = (c * jax.nn.sigmoid(c)).astype(BF16)
    o_ref[...] = jnp.dot(a, w_ref[...].astype(BF16), preferred_element_type=F32) + b_ref[...]


def _modulation(cond, w_ada, b_ada):
    n_out = w_ada.shape[1]
    tn = D_MODEL
    out = pl.pallas_call(
        _mod_kernel,
        grid=(n_out // tn,),
        in_specs=[
            pl.BlockSpec((MOD_ROWS, D_MODEL), lambda n: (0, 0)),
            pl.BlockSpec((D_MODEL, tn), lambda n: (0, n)),
            pl.BlockSpec((1, tn), lambda n: (0, n)),
        ],
        out_specs=pl.BlockSpec((MOD_ROWS, tn), lambda n: (0, n)),
        out_shape=jax.ShapeDtypeStruct((MOD_ROWS, n_out), F32),
        compiler_params=_params(("arbitrary",)),
        name="modulation",
    )(cond, w_ada, b_ada.reshape(1, n_out))
    return out.reshape(MOD_ROWS, 9, D_MODEL)


def _cast_kernel(w_ref, o_ref):
    o_ref[...] = w_ref[...].astype(o_ref.dtype)


def _to_bf16(w, row_tiles=8):
    rows, cols = w.shape
    tr = rows // row_tiles
    return pl.pallas_call(
        _cast_kernel,
        grid=(row_tiles,),
        in_specs=[pl.BlockSpec((tr, cols), lambda i: (i, 0))],
        out_specs=pl.BlockSpec((tr, cols), lambda i: (i, 0)),
        out_shape=jax.ShapeDtypeStruct((rows, cols), BF16),
        compiler_params=_params(("parallel",)),
        name="to_bf16",
    )(w)


def _ffn_kernel(x_ref, mod_ref, win_ref, wout_ref, lng_ref, lnb_ref, *rest, sub, sub_next):
    if sub_next is None:
        xo_ref, act_scr = rest
        ho_ref = None
    else:
        xo_ref, ho_ref, act_scr = rest
    x = x_ref[...]
    shift = mod_ref[0, 3 * sub:3 * sub + 1, :]
    scale = mod_ref[0, 3 * sub + 1:3 * sub + 2, :]
    h = (x * (1.0 + scale) + shift).astype(BF16)
    for c in range(D_FF // FFN_TF):
        cols = slice(c * FFN_TF, (c + 1) * FFN_TF)
        up_cols = slice(D_FF + c * FFN_TF, D_FF + (c + 1) * FFN_TF)
        gate = jnp.dot(h, win_ref[:, cols], preferred_element_type=F32)
        up = jnp.dot(h, win_ref[:, up_cols], preferred_element_type=F32)
        act_scr[:, cols] = (gate * jax.nn.sigmoid(gate) * up).astype(BF16)
    f = jnp.dot(act_scr[...], wout_ref[...], preferred_element_type=F32)
    g = mod_ref[0, 3 * sub + 2:3 * sub + 3, :]
    xn = _layer_norm(ALPHA * x + 0.5 * g * f, lng_ref[sub:sub + 1, :], lnb_ref[sub:sub + 1, :])
    xo_ref[...] = xn
    if ho_ref is not None:
        shift = mod_ref[0, 3 * sub_next:3 * sub_next + 1, :]
        scale = mod_ref[0, 3 * sub_next + 1:3 * sub_next + 2, :]
        ho_ref[...] = (xn * (1.0 + scale) + shift).astype(BF16)


def _ffn(x, mod, cond_of_tile, w_in, w_out, ln_g, ln_b, sub, sub_next):
    m = x.shape[0]
    row = lambda i: (i, 0)
    const = lambda i: (0, 0)
    resident = pl.Buffered(1)
    out_shape = [jax.ShapeDtypeStruct((m, D_MODEL), F32)]
    out_specs = [pl.BlockSpec((FFN_TM, D_MODEL), row)]
    if sub_next is not None:
        out_shape.append(jax.ShapeDtypeStruct((m, D_MODEL), BF16))
        out_specs.append(pl.BlockSpec((FFN_TM, D_MODEL), row))
    return pl.pallas_call(
        functools.partial(_ffn_kernel, sub=sub, sub_next=sub_next),
        grid=(m // FFN_TM,),
        in_specs=[
            pl.BlockSpec((FFN_TM, D_MODEL), row),
            pl.BlockSpec((1, 9, D_MODEL), lambda i: (cond_of_tile(i, FFN_TM), 0, 0)),
            pl.BlockSpec((D_MODEL, 2 * D_FF), const, pipeline_mode=resident),
            pl.BlockSpec((D_FF, D_MODEL), const, pipeline_mode=resident),
            pl.BlockSpec((3, D_MODEL), const),
            pl.BlockSpec((3, D_MODEL), const),
        ],
        out_specs=out_specs,
        out_shape=out_shape,
        scratch_shapes=[pltpu.VMEM((FFN_TM, D_FF), BF16)],
        compiler_params=_params(("parallel",)),
        name=f"ffn{sub}",
    )(x, mod, w_in, w_out, ln_g, ln_b)


def _rope(x, cos, sin):
    lane = lax.broadcasted_iota(jnp.int32, (1, LANES), 1)
    first_half = (lane % 32) < 16
    outs = []
    for hb in range(QK_A // LANES):
        xb = x[:, hb * LANES:(hb + 1) * LANES]
        ahead = pltpu.roll(xb, LANES - 16, 1)
        behind = pltpu.roll(xb, 16, 1)
        rot = jnp.where(first_half, -ahead, behind)
        outs.append(xb * cos + rot * sin)
    return jnp.concatenate(outs, axis=-1)


def _proj_kernel(*refs, rope, sigmoid, scale_first, outs):
    if rope:
        h_ref, w_ref, cos_ref, sin_ref = refs[:4]
        refs = refs[4:]
    else:
        h_ref, w_ref = refs[:2]
        refs = refs[2:]
    out_refs, w_scr = refs[:-1], refs[-1]

    @pl.when(pl.program_id(1) == 0)
    def _():
        w_scr[...] = w_ref[...].astype(BF16)

    y = jnp.dot(h_ref[...], w_scr[...], preferred_element_type=F32)
    if rope:
        y = _rope(y, cos_ref[0], sin_ref[0])
    if scale_first:
        y = y * jnp.where(pl.program_id(0) == 0, Q_SCALE, 1.0)
    if sigmoid:
        y = jax.nn.sigmoid(y)
    for ref, kind in zip(out_refs, outs):
        if kind == 'head':
            for hh in range(MIX_TN // LANES):
                ref[hh] = y[:, hh * LANES:(hh + 1) * LANES].astype(ref.dtype)
        elif kind == 'stack':
            ref[0] = y.astype(ref.dtype)
        else:
            ref[...] = y.astype(ref.dtype)


def _project(h, w, first_block, n_blocks, outs, rope_tables=None, sigmoid=False, scale_first=False, name="proj"):
    m = h.shape[0]
    rope = rope_tables is not None
    heads = MIX_TN // LANES
    in_specs = [pl.BlockSpec((MIX_TM, D_MODEL), lambda n, i: (i, 0)),
                pl.BlockSpec((D_MODEL, MIX_TN), lambda n, i: (0, first_block + n))]
    args = [h, w]
    if rope:
        n_rope_tiles = rope_tables[0].shape[1] // MIX_TM
        for t in rope_tables:
            in_specs.append(pl.BlockSpec((1, MIX_TM, DV_A), lambda n, i: (n, i % n_rope_tiles, 0)))
            args.append(t)
    specs = {
        'head': (pl.BlockSpec((heads, MIX_TM, LANES), lambda n, i: (n, i, 0)), (n_blocks * heads, m, LANES)),
        'tok': (pl.BlockSpec((MIX_TM, MIX_TN), lambda n, i: (i, n)), (m, n_blocks * MIX_TN)),
        'stack': (pl.BlockSpec((1, MIX_TM, MIX_TN), lambda n, i: (n, i, 0)), (n_blocks, m, MIX_TN)),
    }
    return pl.pallas_call(
        functools.partial(_proj_kernel, rope=rope, sigmoid=sigmoid, scale_first=scale_first,
                          outs=tuple(kind for kind, _ in outs)),
        grid=(n_blocks, m // MIX_TM),
        in_specs=in_specs,
        out_specs=[specs[kind][0] for kind, _ in outs],
        out_shape=[jax.ShapeDtypeStruct(specs[kind][1], dt) for kind, dt in outs],
        scratch_shapes=[pltpu.VMEM((D_MODEL, MIX_TN), BF16)],
        compiler_params=_params(("arbitrary", "arbitrary")),
        name=name,
    )(*args)


def _attn_kernel(*refs, lam_init, has_ctx):
    if has_ctx:
        lam_ref, g_ref, q_ref, k_ref, v_ref, kc_ref, vc_ref, o_ref = refs
    else:
        lam_ref, g_ref, q_ref, k_ref, v_ref, o_ref = refs
    lp = lam_ref[...]
    lam = (jnp.exp(jnp.sum(lp[0:1] * lp[1:2], axis=-1, keepdims=True))
           - jnp.exp(jnp.sum(lp[2:3] * lp[3:4], axis=-1, keepdims=True)) + lam_init)

    lane = lax.broadcasted_iota(jnp.int32, (1, DV_A), 1)
    comp0 = lane < DH_A
    tq = q_ref.shape[2]
    sub = min(ATT_SUB, tq)
    problems = []
    for hh in range(q_ref.shape[0]):
        keys = [k_ref[hh, 0]]
        vals = [v_ref[hh, 0]]
        if has_ctx:
            keys.append(kc_ref[0, :, hh * DV_A:(hh + 1) * DV_A].astype(BF16))
            vals.append(vc_ref[0, :, hh * DV_A:(hh + 1) * DV_A].astype(BF16))
        for r in range(tq // sub):
            q = q_ref[hh, 0, r * sub:(r + 1) * sub, :]
            zero = jnp.zeros_like(q)
            for qc in (jnp.where(comp0, q, zero), jnp.where(comp0, zero, q)):
                problems.append((qc, keys, vals))
    scores = [[_nt_dot(qc, kk) for kk in keys] for qc, keys, _ in problems]
    weights = []
    for ss in scores:
        mx = functools.reduce(jnp.maximum, [jnp.max(s, axis=-1, keepdims=True) for s in ss])
        es = [jnp.exp2(s - mx) for s in ss]
        den = functools.reduce(jnp.add, [jnp.sum(e, axis=-1, keepdims=True) for e in es])
        weights.append(([e.astype(BF16) for e in es], 1.0 / den))
    maps = [functools.reduce(jnp.add, [jnp.dot(e, vv, preferred_element_type=F32) for e, vv in zip(es, vals)]) * r
            for (es, r), (_, _, vals) in zip(weights, problems)]
    idx = 0
    for hh in range(q_ref.shape[0]):
        for r in range(tq // sub):
            o = maps[idx] - lam * maps[idx + 1]
            idx += 2
            o = o * lax.rsqrt(jnp.mean(o * o, axis=-1, keepdims=True) + RMS_EPS) * g_ref[...]
            o_ref[hh, 0, r * sub:(r + 1) * sub, :] = (o * (1.0 - lam_init)).astype(o_ref.dtype)


def _attention(q, k, v, ctx_kv, lam_params, subln_g, lam_init):
    (q_arr, q_row), (k_arr, k_row), (v_arr, v_row) = q, k, v
    _, bsz, t, _ = q_arr.shape
    tq = min(ATT_TQ, t)
    hp = N_HEADS_A if t <= ATT_SHORT_SEQ else 1
    const = lambda b, h, i: (0, 0)
    rows = lambda first: (lambda b, h, i: (first // hp + h, b, 0, 0))
    in_specs = [
        pl.BlockSpec((4, DH_A), const),
        pl.BlockSpec((1, DV_A), const),
        pl.BlockSpec((hp, 1, tq, DV_A), lambda b, h, i: (q_row // hp + h, b, i, 0)),
        pl.BlockSpec((hp, 1, t, DV_A), rows(k_row)),
        pl.BlockSpec((hp, 1, t, DV_A), rows(v_row)),
    ]
    args = [lam_params, subln_g, q_arr, k_arr, v_arr]
    if ctx_kv is not None:
        p = ctx_kv[0].shape[1]
        head_all = lambda b, h, i: (b, 0, h)
        in_specs += [pl.BlockSpec((1, p, hp * DV_A), head_all), pl.BlockSpec((1, p, hp * DV_A), head_all)]
        args += list(ctx_kv)
    return pl.pallas_call(
        functools.partial(_attn_kernel, lam_init=lam_init, has_ctx=ctx_kv is not None),
        grid=(bsz, N_HEADS_A // hp, t // tq),
        in_specs=in_specs,
        out_specs=pl.BlockSpec((hp, 1, tq, DV_A), lambda b, h, i: (h, b, i, 0)),
        out_shape=jax.ShapeDtypeStruct((N_HEADS_A, bsz, t, DV_A), BF16),
        compiler_params=_params(("parallel", "parallel", "arbitrary")),
        name="attention",
    )(*args)


def _chunk_scan(g, reverse):
    r = g.shape[0]
    pos = lax.broadcasted_iota(jnp.int32, (r, 1), 0) % HGRN_C
    b = g
    sh = 1
    while sh < HGRN_C:
        if reverse:
            b = b + jnp.where(pos < HGRN_C - sh, pltpu.roll(b, r - sh, 0), 0.0)
        else:
            b = b + jnp.where(pos >= sh, pltpu.roll(b, sh, 0), 0.0)
        sh *= 2
    return b


def _hgrn_kernel(*refs, seq_len, has_s0, want_state):
    refs = list(refs)
    hq_ref, hff_ref, hfb_ref, hi_ref, hg_ref, lbl_ref, ng_ref = refs[:7]
    refs = refs[7:]
    s0_ref = refs.pop(0) if has_s0 else None
    o_ref = refs.pop(0)
    sfin_ref = refs.pop(0) if want_state else None
    g_scr, kd_scr, q_scr, od_scr, st_scr, b_scr, ok_scr = refs
    hp = hq_ref.shape[0]
    c = HGRN_C
    rb = min(HGRN_ROWS, seq_len)
    nb = seq_len // rb
    cpb = rb // c
    z_refs = (hff_ref, hfb_ref)

    def lower_bound(hh, d):
        logits = lbl_ref[d, :, hh * DK_H:(hh + 1) * DK_H]
        mx = jnp.max(logits, axis=0, keepdims=True)
        e = jnp.exp(logits - mx)
        return e[0:1] / jnp.sum(e, axis=0, keepdims=True)

    lbs = [[lower_bound(hh, d) for d in range(2)] for hh in range(hp)]

    def gates(i, carry):
        rows = pl.ds(pl.multiple_of(i * rb, rb), rb)
        for d in range(2):
            worst = jnp.zeros((1, DK_H), F32)
            for hh in range(hp):
                if d == 0:
                    hq = hq_ref[hh, 0, rows, :]
                    q_scr[hh, rows, :] = hq * jax.nn.sigmoid(hq)
                z = z_refs[d][hh, 0, rows, :]
                lb = lbs[hh][d]
                e = jnp.exp(-jnp.abs(z))
                r = 1.0 / (1.0 + e)
                er = e * r
                pos = z >= 0.0
                g = jnp.log(lb + (1.0 - lb) * jnp.where(pos, r, er))
                g_scr[hh, d, rows, :] = g
                kd_scr[hh, d, rows, :] = (1.0 - lb) * jnp.where(pos, er, r)
                for j in range(cpb):
                    worst = jnp.maximum(worst, -jnp.sum(g[j * c:(j + 1) * c], axis=0, keepdims=True))
            ok_scr[d, i] = (jnp.max(worst) <= HGRN_SAFE_DECAY).astype(jnp.int32)
        return carry

    lax.fori_loop(0, nb, gates, 0)

    for hh in range(hp):
        for d in range(2):
            if has_s0:
                st_scr[hh, d] = s0_ref[0, d, hh].T
            else:
                st_scr[hh, d] = jnp.zeros((DV_H, DK_H), F32)

    row = lax.broadcasted_iota(jnp.int32, (c, 1), 0)
    col = lax.broadcasted_iota(jnp.int32, (1, c), 1)

    def fast_block(d, r0):
        reverse = d == 1
        rows = pl.ds(r0, rb)
        n = hp * cpb
        flat = lambda a: a.reshape(hp * rb, a.shape[-1])
        chunks = lambda a: a.reshape(n, c, a.shape[-1])
        vb = chunks(hi_ref[:, 0, rows, :].astype(BF16))
        b = _chunk_scan(flat(g_scr[:, d, rows, :]), reverse)
        qd = chunks((flat(q_scr[:, rows, :]) * jnp.exp(b)).astype(BF16))
        ku = chunks(flat(kd_scr[:, d, rows, :]) * jnp.exp(-b))
        b = chunks(b)
        e_end = jnp.exp(b[:, 0:1, :] if reverse else b[:, c - 1:c, :])
        causal = ((row <= col) if reverse else (row >= col))[None]
        a = jnp.einsum('nck,nsk->ncs', qd, ku.astype(BF16), preferred_element_type=F32)
        o = jnp.einsum('ncs,nsv->ncv', jnp.where(causal, a, 0.0).astype(BF16), vb, preferred_element_type=F32)
        u = jnp.einsum('nsv,nsk->nvk', vb, (ku * e_end).astype(BF16), preferred_element_type=F32)
        entering = [None] * n
        for hh in range(hp):
            st = st_scr[hh, d]
            for j in (range(cpb - 1, -1, -1) if reverse else range(cpb)):
                entering[hh * cpb + j] = st.astype(BF16)
                st = e_end[hh * cpb + j] * st + u[hh * cpb + j]
            st_scr[hh, d] = st
        o = o + jnp.einsum('nck,nvk->ncv', qd, jnp.stack(entering), preferred_element_type=F32)
        od_scr[:, d, rows, :] = o.reshape(hp, rb, DV_H)

    def exact_chunk(hh, d, r0):
        reverse = d == 1
        rows = pl.ds(r0, c)
        b = _chunk_scan(g_scr[hh, d, rows, :], reverse)
        b_scr[...] = b

        def one_row(t, carry):
            bt = b_scr[pl.ds(t, 1), :]
            qt = q_scr[hh, pl.ds(r0 + t, 1), :]
            seen = (row >= t) if reverse else (row <= t)
            w = jnp.exp(jnp.where(seen, bt - b_scr[...], -jnp.inf))
            p = jnp.sum(qt * kd_scr[hh, d, rows, :] * w, axis=-1, keepdims=True)
            od_scr[hh, d, pl.ds(r0 + t, 1), :] = jnp.sum(p * hi_ref[hh, 0, rows, :], axis=0, keepdims=True)
            return carry

        lax.fori_loop(0, c, one_row, 0)
        b_end = b[0:1] if reverse else b[c - 1:c]
        st = st_scr[hh, d]
        qd = (q_scr[hh, rows, :] * jnp.exp(b)).astype(BF16)
        od_scr[hh, d, rows, :] = od_scr[hh, d, rows, :] + _nt_dot(qd, st.astype(BF16))
        k_end = (kd_scr[hh, d, rows, :] * jnp.exp(b_end - b)).astype(BF16)
        st_scr[hh, d] = jnp.exp(b_end) * st + _tn_dot(hi_ref[hh, 0, rows, :].astype(BF16), k_end)

    def block(i, carry):
        starts = (pl.multiple_of(i * rb, rb), pl.multiple_of((nb - 1 - i) * rb, rb))
        mild = (ok_scr[0, i] + ok_scr[1, nb - 1 - i]) == 2

        @pl.when(mild)
        def _():
            for d in range(2):
                fast_block(d, starts[d])

        @pl.when(jnp.logical_not(mild))
        def _():
            for hh in range(hp):
                for d in range(2):
                    for j in (range(cpb - 1, -1, -1) if d == 1 else range(cpb)):
                        exact_chunk(hh, d, pl.multiple_of(starts[d] + j * c, c))

        return carry

    lax.fori_loop(0, nb, block, 0)

    def finish(i, carry):
        rows = pl.ds(pl.multiple_of(i * rb, rb), rb)
        for hh in range(hp):
            o = od_scr[hh, 0, rows, :] + od_scr[hh, 1, rows, :]
            o = o * lax.rsqrt(jnp.mean(o * o, axis=-1, keepdims=True) + RMS_EPS) * ng_ref[...]
            hg = hg_ref[hh, 0, rows, :]
            o_ref[hh, 0, rows, :] = (o * (hg * jax.nn.sigmoid(hg))).astype(o_ref.dtype)
        return carry

    lax.fori_loop(0, nb, finish, 0)
    if want_state:
        for hh in range(hp):
            for d in range(2):
                sfin_ref[0, d, hh] = st_scr[hh, d].T


def _hgrn(streams, lb_logits, norm_g, s0, want_state):
    _, bsz, t, _ = streams.shape
    hp = HGRN_SHORT_HEADS if t <= HGRN_ROWS else 1
    stream = lambda s: pl.BlockSpec((hp, 1, t, DK_H), lambda b, h: (s * N_HEADS_H // hp + h, b, 0, 0))
    in_specs = [stream(HGRN_BLOCKS.index(w)) for w in (W_HQ, W_HFF, W_HFB, W_HI, W_HG)]
    in_specs += [
        pl.BlockSpec((2, DEPTH + 1, hp * DK_H), lambda b, h: (0, 0, h)),
        pl.BlockSpec((1, DV_H), lambda b, h: (0, 0)),
    ]
    args = [streams] * 5 + [lb_logits, norm_g]
    state_spec = pl.BlockSpec((1, 2, hp, DK_H, DV_H), lambda b, h: (b, 0, h, 0, 0))
    if s0 is not None:
        in_specs.append(state_spec)
        args.append(s0)
    out_specs = [pl.BlockSpec((hp, 1, t, DV_H), lambda b, h: (h, b, 0, 0))]
    out_shape = [jax.ShapeDtypeStruct((N_HEADS_H, bsz, t, DV_H), BF16)]
    if want_state:
        out_specs.append(state_spec)
        out_shape.append(jax.ShapeDtypeStruct((bsz, 2, N_HEADS_H, DK_H, DV_H), F32))
    n_blocks = t // min(HGRN_ROWS, t)
    outs = pl.pallas_call(
        functools.partial(_hgrn_kernel, seq_len=t, has_s0=s0 is not None, want_state=want_state),
        grid=(bsz, N_HEADS_H // hp),
        in_specs=in_specs,
        out_specs=out_specs,
        out_shape=out_shape,
        scratch_shapes=[
            pltpu.VMEM((hp, 2, t, DK_H), F32),
            pltpu.VMEM((hp, 2, t, DK_H), F32),
            pltpu.VMEM((hp, t, DK_H), F32),
            pltpu.VMEM((hp, 2, t, DV_H), F32),
            pltpu.VMEM((hp, 2, DV_H, DK_H), F32),
            pltpu.VMEM((HGRN_C, DK_H), F32),
            pltpu.SMEM((2, n_blocks), jnp.int32),
        ],
        compiler_params=_params(("parallel", "parallel")),
        name="hgrn",
    )(*args)
    return outs if want_state else (outs[0], None)


def _merge_kernel(oa_ref, oh_ref, sa_ref, sh_ref, x_ref, mod_ref, wa_ref, wh_ref, wo_ref, lng_ref, lnb_ref,
                  xo_ref, ho_ref, wa_scr, wh_scr, wo_scr):
    @pl.when(pl.program_id(0) == 0)
    def _():
        wa_scr[...] = wa_ref[...].astype(BF16)
        wh_scr[...] = wh_ref[...].astype(BF16)
        wo_scr[...] = wo_ref[...].astype(BF16)

    heads = lambda ref: jnp.concatenate([ref[hh] for hh in range(ref.shape[0])], axis=-1)
    ba = jnp.dot(heads(oa_ref), wa_scr[...], preferred_element_type=F32)
    bh = jnp.dot(heads(oh_ref), wh_scr[...], preferred_element_type=F32)
    merged = sa_ref[...].astype(F32) * ba + sh_ref[...].astype(F32) * bh
    mix = jnp.dot(merged.astype(BF16), wo_scr[...], preferred_element_type=F32)
    y = ALPHA * x_ref[...] + mod_ref[0, 5:6, :] * mix
    xn = _layer_norm(y, lng_ref[1:2, :], lnb_ref[1:2, :])
    xo_ref[...] = xn
    ho_ref[...] = (xn * (1.0 + mod_ref[0, 7:8, :]) + mod_ref[0, 6:7, :]).astype(BF16)


def _merge(o_a, o_h, gates, x, mod, cond_of_tile, w_a, w_h, w_o, ln_g, ln_b):
    m = x.shape[0]
    row = lambda i: (i, 0)
    const = lambda i: (0, 0)
    per_head = lambda i: (0, i, 0)
    return pl.pallas_call(
        _merge_kernel,
        grid=(m // MERGE_TM,),
        in_specs=[
            pl.BlockSpec((N_HEADS_A, MERGE_TM, DV_A), per_head),
            pl.BlockSpec((N_HEADS_H, MERGE_TM, DV_H), per_head),
            pl.BlockSpec((MERGE_TM, D_MODEL), lambda i: (i, 0)),
            pl.BlockSpec((MERGE_TM, D_MODEL), lambda i: (i, 1)),
            pl.BlockSpec((MERGE_TM, D_MODEL), row),
            pl.BlockSpec((1, 9, D_MODEL), lambda i: (cond_of_tile(i, MERGE_TM), 0, 0)),
            pl.BlockSpec((W_A, D_MODEL), const),
            pl.BlockSpec((W_H, D_MODEL), const),
            pl.BlockSpec((D_MODEL, D_MODEL), const),
            pl.BlockSpec((3, D_MODEL), const),
            pl.BlockSpec((3, D_MODEL), const),
        ],
        out_specs=[pl.BlockSpec((MERGE_TM, D_MODEL), row), pl.BlockSpec((MERGE_TM, D_MODEL), row)],
        out_shape=[jax.ShapeDtypeStruct((m, D_MODEL), F32), jax.ShapeDtypeStruct((m, D_MODEL), BF16)],
        scratch_shapes=[pltpu.VMEM((W_A, D_MODEL), BF16), pltpu.VMEM((W_H, D_MODEL), BF16),
                        pltpu.VMEM((D_MODEL, D_MODEL), BF16)],
        compiler_params=_params(("arbitrary",)),
        name="merge",
    )(o_a, o_h, gates, gates, x, mod, w_a, w_h, w_o, ln_g, ln_b)


def _rope_tables(n_tok):
    rows = n_tok // GRID_W
    row = np.repeat(np.arange(rows, dtype=np.float32), GRID_W)
    col = np.tile(np.arange(GRID_W, dtype=np.float32), rows)
    half = DH_A // 2
    inv = (ROPE_BASE ** (-np.arange(0, half, 2, dtype=np.float32) / half)).astype(np.float32)
    ar = row[:, None] * inv
    ac = col[:, None] * inv
    ang = np.concatenate([ar, ar, ac, ac] * 2, axis=-1)
    cos, sin = np.cos(ang).astype(np.float32), np.sin(ang).astype(np.float32)
    scale = np.float32(Q_SCALE)
    return (jnp.asarray(np.stack([cos * scale, cos, np.ones_like(cos)])),
            jnp.asarray(np.stack([sin * scale, sin, np.zeros_like(sin)])))


def _layer_pass(x, mod, cond_of_tile, wts, lam_init, bsz, ctx):
    m = x.shape[0]
    t = m // bsz
    latent = ctx is not None
    per_head = lambda a: a.reshape(a.shape[0], bsz, t, a.shape[-1])
    x1, h1 = _ffn(x, mod, cond_of_tile, wts['ffn1_w_in'], wts['ffn1_w_out'], wts['ln_g'], wts['ln_b'], 0, 1)
    w_mix = wts['w_mix_in']
    if latent:
        (qkv,) = _project(h1, w_mix, W_Q, 3, [('head', BF16)], rope_tables=_rope_tables(t), name="proj_qkv")
        k = v = None
        ctx_kv = (ctx[0], ctx[1])
    else:
        qkv, qkv32 = _project(h1, w_mix, W_Q, 3, [('head', BF16), ('stack', F32)], scale_first=True,
                              name="proj_qkv")
        k, v = qkv32[W_K], qkv32[W_V]
        ctx_kv = None
    (streams,) = _project(h1, w_mix, W_HQ, len(HGRN_BLOCKS), [('head', F32)], name="proj_hgrn")
    (gates,) = _project(h1, w_mix, W_GA0, len(GATE_BLOCKS), [('tok', BF16)], sigmoid=True, name="proj_gates")
    qkv = per_head(qkv)
    q_kv = [(qkv, W_Q * N_HEADS_A), (qkv, W_K * N_HEADS_A), (qkv, W_V * N_HEADS_A)]
    o_a = _attention(*q_kv, ctx_kv, wts['lam_params'], wts['attn_subln_g'], lam_init)
    o_h, s_fin = _hgrn(per_head(streams), wts['lb_logits'], wts['hgrn_norm_g'],
                       ctx[2] if latent else None, not latent)
    x2, h2 = _merge(o_a.reshape(N_HEADS_A, m, DV_A), o_h.reshape(N_HEADS_H, m, DV_H), gates, x1, mod,
                    cond_of_tile, wts['w_branch_a'], wts['w_branch_h'], wts['w_mix_out'], wts['ln_g'], wts['ln_b'])
    (y,) = _ffn(x2, mod, cond_of_tile, wts['ffn2_w_in'], wts['ffn2_w_out'], wts['ln_g'], wts['ln_b'], 2, None)
    return y, (k, v, s_fin)


def kernel(x_prompt, x_sample, cache_k, cache_v, state_hgrn, c, c_ctx, w_ada, b_ada, ffn1_w_in, ffn1_w_out,
           w_mix_in, lambda_q1, lambda_k1, lambda_q2, lambda_k2, attn_subln_g, hgrn_lb_logits, hgrn_norm_g,
           w_branch_a, w_branch_h, w_mix_out, ffn2_w_in, ffn2_w_out, ln_g, ln_b):
    batch, seq, _ = x_prompt.shape
    dec_batch, dec_seq, _ = x_sample.shape
    past = cache_k.shape[2]
    assert DEPTH == 1 and dec_batch + 1 <= MOD_ROWS
    lam_init = 0.8 - 0.6 * math.exp(-0.3 * 0)

    cond = jnp.concatenate([c_ctx[None, :], c, jnp.zeros((MOD_ROWS - 1 - dec_batch, D_MODEL), F32)], axis=0)
    mod = _modulation(cond, w_ada[0], b_ada[0])

    wts = {
        'ffn1_w_in': _to_bf16(ffn1_w_in[0]), 'ffn1_w_out': _to_bf16(ffn1_w_out[0]), 'w_mix_in': w_mix_in[0],
        'lam_params': jnp.concatenate([lambda_q1, lambda_k1, lambda_q2, lambda_k2], axis=0),
        'attn_subln_g': attn_subln_g, 'lb_logits': hgrn_lb_logits, 'hgrn_norm_g': hgrn_norm_g,
        'w_branch_a': w_branch_a[0], 'w_branch_h': w_branch_h[0], 'w_mix_out': w_mix_out[0],
        'ffn2_w_in': _to_bf16(ffn2_w_in[0]), 'ffn2_w_out': _to_bf16(ffn2_w_out[0]),
        'ln_g': ln_g[0], 'ln_b': ln_b[0],
    }

    y_p, (k_c, v_c, s_c) = _layer_pass(x_prompt.reshape(batch * seq, D_MODEL), mod, lambda i, tm: 0,
                                       wts, lam_init, batch, None)
    ctx = (cache_k[:, 0].reshape(dec_batch, past, QK_A), cache_v[:, 0].reshape(dec_batch, past, W_A),
           state_hgrn[:, 0])
    y_s, _ = _layer_pass(x_sample.reshape(dec_batch * dec_seq, D_MODEL), mod,
                         lambda i, tm: 1 + (i * tm) // dec_seq, wts, lam_init, dec_batch, ctx)

    return (y_p.reshape(batch, seq, D_MODEL),
            y_s.reshape(dec_batch, dec_seq, D_MODEL),
            k_c.reshape(batch, DEPTH, seq, N_HEADS_A, 2, DH_A),
            v_c.reshape(batch, DEPTH, seq, N_HEADS_A, DV_A),
            s_c.reshape(batch, DEPTH, 2, N_HEADS_H, DK_H, DV_H))
```

```python
import functools
import math

import numpy as np
import jax
import jax.numpy as jnp
from jax import lax
from jax.experimental import pallas as pl
from jax.experimental.pallas import tpu as pltpu

D_MODEL = 1024
DEPTH = 1
GRID_W = 64
N_HEADS_A = 4
DH_A = 64
DV_A = 2 * DH_A
QK_A = N_HEADS_A * 2 * DH_A
W_A = N_HEADS_A * DV_A
ROPE_BASE = 10000.0
N_HEADS_H = 4
DK_H = 128
DV_H = 128
QK_H = N_HEADS_H * DK_H
W_H = N_HEADS_H * DV_H
D_FF = 2816
MIX_IN = 2 * QK_A + W_A + 3 * QK_H + 2 * W_H + 2 * D_MODEL
ALPHA = (2 * DEPTH) ** 0.25
LN_EPS = 1e-5
RMS_EPS = 1e-6
Q_SCALE = DH_A ** -0.5 * math.log2(math.e)

F32 = jnp.float32
BF16 = jnp.bfloat16

LANES = 128
VMEM_LIMIT = 56 * 1024 * 1024

FFN_TM = 512
FFN_TF = 256
MIX_TM = 4096
MIX_TN = 512
MERGE_TM = 512
ATT_TQ = 1024
ATT_SUB = 512
ATT_SHORT_SEQ = 256
HGRN_C = 64
HGRN_ROWS = 512
HGRN_SHORT_HEADS = 4
HGRN_SAFE_DECAY = 75.0
MOD_ROWS = 8

W_Q, W_K, W_V, W_HQ, W_HFF, W_HFB, W_HI, W_HG, W_GA0, W_GA1, W_GH0, W_GH1 = range(12)
HGRN_BLOCKS = (W_HQ, W_HFF, W_HFB, W_HI, W_HG)
GATE_BLOCKS = (W_GA0, W_GA1, W_GH0, W_GH1)


def _params(sem):
    return pltpu.CompilerParams(dimension_semantics=sem, vmem_limit_bytes=VMEM_LIMIT)


def _nt_dot(a, b):
    return lax.dot_general(a, b, (((1,), (1,)), ((), ())), preferred_element_type=F32)


def _tn_dot(a, b):
    return lax.dot_general(a, b, (((0,), (0,)), ((), ())), preferred_element_type=F32)


def _layer_norm(y, g, b):
    mu = jnp.mean(y, axis=-1, keepdims=True)
    yc = y - mu
    var = jnp.mean(yc * yc, axis=-1, keepdims=True)
    return yc * lax.rsqrt(var + LN_EPS) * g + b


def _mod_kernel(c_ref, w_ref, b_ref, o_ref):
    c = c_ref[...]
    a = (c * jax.nn.sigmoid(c)).astype(BF16)
    o_ref[...] = jnp.dot(a, w_ref[...].astype(BF16), preferred_element_type=F32) + b_ref[...]


def _modulation(cond, w_ada, b_ada):
    n_out = w_ada.shape[1]
    tn = D_MODEL
    out = pl.pallas_call(
        _mod_kernel,
        grid=(n_out // tn,),
        in_specs=[
            pl.BlockSpec((MOD_ROWS, D_MODEL), lambda n: (0, 0)),
            pl.BlockSpec((D_MODEL, tn), lambda n: (0, n)),
            pl.BlockSpec((1, tn), lambda n: (0, n)),
        ],
        out_specs=pl.BlockSpec((MOD_ROWS, tn), lambda n: (0, n)),
        out_shape=jax.ShapeDtypeStruct((MOD_ROWS, n_out), F32),
        compiler_params=_params(("arbitrary",)),
        name="modulation",
    )(cond, w_ada, b_ada.reshape(1, n_out))
    return out.reshape(MOD_ROWS, 9, D_MODEL)


def _cast_kernel(w_ref, o_ref):
    o_ref[...] = w_ref[...].astype(o_ref.dtype)


def _to_bf16(w, row_tiles=8):
    rows, cols = w.shape
    tr = rows // row_tiles
    return pl.pallas_call(
        _cast_kernel,
        grid=(row_tiles,),
        in_specs=[pl.BlockSpec((tr, cols), lambda i: (i, 0))],
        out_specs=pl.BlockSpec((tr, cols), lambda i: (i, 0)),
        out_shape=jax.ShapeDtypeStruct((rows, cols), BF16),
        compiler_params=_params(("parallel",)),
        name="to_bf16",
    )(w)


def _ffn_kernel(x_ref, mod_ref, win_ref, wout_ref, lng_ref, lnb_ref, *rest, sub, sub_next):
    if sub_next is None:
        xo_ref, act_scr = rest
        ho_ref = None
    else:
        xo_ref, ho_ref, act_scr = rest
    x = x_ref[...]
    shift = mod_ref[0, 3 * sub:3 * sub + 1, :]
    scale = mod_ref[0, 3 * sub + 1:3 * sub + 2, :]
    h = (x * (1.0 + scale) + shift).astype(BF16)
    for c in range(D_FF // FFN_TF):
        cols = slice(c * FFN_TF, (c + 1) * FFN_TF)
        up_cols = slice(D_FF + c * FFN_TF, D_FF + (c + 1) * FFN_TF)
        gate = jnp.dot(h, win_ref[:, cols], preferred_element_type=F32)
        up = jnp.dot(h, win_ref[:, up_cols], preferred_element_type=F32)
        act_scr[:, cols] = (gate * jax.nn.sigmoid(gate) * up).astype(BF16)
    f = jnp.dot(act_scr[...], wout_ref[...], preferred_element_type=F32)
    g = mod_ref[0, 3 * sub + 2:3 * sub + 3, :]
    xn = _layer_norm(ALPHA * x + 0.5 * g * f, lng_ref[sub:sub + 1, :], lnb_ref[sub:sub + 1, :])
    xo_ref[...] = xn
    if ho_ref is not None:
        shift = mod_ref[0, 3 * sub_next:3 * sub_next + 1, :]
        scale = mod_ref[0, 3 * sub_next + 1:3 * sub_next + 2, :]
        ho_ref[...] = (xn * (1.0 + scale) + shift).astype(BF16)


def _ffn(x, mod, cond_of_tile, w_in, w_out, ln_g, ln_b, sub, sub_next):
    m = x.shape[0]
    row = lambda i: (i, 0)
    const = lambda i: (0, 0)
    resident = pl.Buffered(1)
    out_shape = [jax.ShapeDtypeStruct((m, D_MODEL), F32)]
    out_specs = [pl.BlockSpec((FFN_TM, D_MODEL), row)]
    if sub_next is not None:
        out_shape.append(jax.ShapeDtypeStruct((m, D_MODEL), BF16))
        out_specs.append(pl.BlockSpec((FFN_TM, D_MODEL), row))
    return pl.pallas_call(
        functools.partial(_ffn_kernel, sub=sub, sub_next=sub_next),
        grid=(m // FFN_TM,),
        in_specs=[
            pl.BlockSpec((FFN_TM, D_MODEL), row),
            pl.BlockSpec((1, 9, D_MODEL), lambda i: (cond_of_tile(i, FFN_TM), 0, 0)),
            pl.BlockSpec((D_MODEL, 2 * D_FF), const, pipeline_mode=resident),
            pl.BlockSpec((D_FF, D_MODEL), const, pipeline_mode=resident),
            pl.BlockSpec((3, D_MODEL), const),
            pl.BlockSpec((3, D_MODEL), const),
        ],
        out_specs=out_specs,
        out_shape=out_shape,
        scratch_shapes=[pltpu.VMEM((FFN_TM, D_FF), BF16)],
        compiler_params=_params(("parallel",)),
        name=f"ffn{sub}",
    )(x, mod, w_in, w_out, ln_g, ln_b)


def _rope(x, cos, sin):
    lane = lax.broadcasted_iota(jnp.int32, (1, LANES), 1)
    first_half = (lane % 32) < 16
    outs = []
    for hb in range(QK_A // LANES):
        xb = x[:, hb * LANES:(hb + 1) * LANES]
        ahead = pltpu.roll(xb, LANES - 16, 1)
        behind = pltpu.roll(xb, 16, 1)
        rot = jnp.where(first_half, -ahead, behind)
        outs.append(xb * cos + rot * sin)
    return jnp.concatenate(outs, axis=-1)


def _proj_kernel(*refs, rope, sigmoid, scale_first, outs):
    if rope:
        h_ref, w_ref, cos_ref, sin_ref = refs[:4]
        refs = refs[4:]
    else:
        h_ref, w_ref = refs[:2]
        refs = refs[2:]
    out_refs, w_scr = refs[:-1], refs[-1]

    @pl.when(pl.program_id(1) == 0)
    def _():
        w_scr[...] = w_ref[...].astype(BF16)

    y = jnp.dot(h_ref[...], w_scr[...], preferred_element_type=F32)
    if rope:
        t_rope = cos_ref.shape[1]
        y = jnp.concatenate([_rope(y[r * t_rope:(r + 1) * t_rope], cos_ref[0], sin_ref[0])
                             for r in range(y.shape[0] // t_rope)], axis=0)
    if scale_first:
        y = y * jnp.where(pl.program_id(0) == 0, Q_SCALE, 1.0)
    if sigmoid:
        y = jax.nn.sigmoid(y)
    for ref, kind in zip(out_refs, outs):
        if kind == 'head':
            for hh in range(MIX_TN // LANES):
                ref[hh] = y[:, hh * LANES:(hh + 1) * LANES].astype(ref.dtype)
        elif kind == 'stack':
            ref[0] = y.astype(ref.dtype)
        else:
            ref[...] = y.astype(ref.dtype)


def _project(h, w, first_block, n_blocks, outs, rope_tables=None, sigmoid=False, scale_first=False, name="proj"):
    m = h.shape[0]
    rope = rope_tables is not None
    heads = MIX_TN // LANES
    h_mode = dict(pipeline_mode=pl.Buffered(1)) if m == MIX_TM else {}
    in_specs = [pl.BlockSpec((MIX_TM, D_MODEL), lambda n, i: (i, 0), **h_mode),
                pl.BlockSpec((D_MODEL, MIX_TN), lambda n, i: (0, first_block + n))]
    args = [h, w]
    if rope:
        t_rope = rope_tables[0].shape[1]
        rope_rows = min(t_rope, MIX_TM)
        n_rope_tiles = t_rope // rope_rows
        for t in rope_tables:
            in_specs.append(pl.BlockSpec((1, rope_rows, DV_A), lambda n, i: (n, i % n_rope_tiles, 0)))
            args.append(t)
    specs = {
        'head': (pl.BlockSpec((heads, MIX_TM, LANES), lambda n, i: (n, i, 0)), (n_blocks * heads, m, LANES)),
        'tok': (pl.BlockSpec((MIX_TM, MIX_TN), lambda n, i: (i, n)), (m, n_blocks * MIX_TN)),
        'stack': (pl.BlockSpec((1, MIX_TM, MIX_TN), lambda n, i: (n, i, 0)), (n_blocks, m, MIX_TN)),
    }
    return pl.pallas_call(
        functools.partial(_proj_kernel, rope=rope, sigmoid=sigmoid, scale_first=scale_first,
                          outs=tuple(kind for kind, _ in outs)),
        grid=(n_blocks, m // MIX_TM),
        in_specs=in_specs,
        out_specs=[specs[kind][0] for kind, _ in outs],
        out_shape=[jax.ShapeDtypeStruct(specs[kind][1], dt) for kind, dt in outs],
        scratch_shapes=[pltpu.VMEM((D_MODEL, MIX_TN), BF16)],
        compiler_params=_params(("arbitrary", "arbitrary")),
        name=name,
    )(*args)


def _attn_kernel(*refs, lam_init, has_ctx):
    if has_ctx:
        lam_ref, g_ref, q_ref, k_ref, v_ref, kc_ref, vc_ref, o_ref = refs
    else:
        lam_ref, g_ref, q_ref, k_ref, v_ref, o_ref = refs
    lp = lam_ref[...]
    lam = (jnp.exp(jnp.sum(lp[0:1] * lp[1:2], axis=-1, keepdims=True))
           - jnp.exp(jnp.sum(lp[2:3] * lp[3:4], axis=-1, keepdims=True)) + lam_init)

    lane = lax.broadcasted_iota(jnp.int32, (1, DV_A), 1)
    comp0 = lane < DH_A
    tq = q_ref.shape[2]
    sub = min(ATT_SUB, tq)
    problems = []
    for hh in range(q_ref.shape[0]):
        keys = [k_ref[hh, 0]]
        vals = [v_ref[hh, 0]]
        if has_ctx:
            keys.append(kc_ref[0, :, hh * DV_A:(hh + 1) * DV_A].astype(BF16))
            vals.append(vc_ref[0, :, hh * DV_A:(hh + 1) * DV_A].astype(BF16))
        for r in range(tq // sub):
            q = q_ref[hh, 0, r * sub:(r + 1) * sub, :]
            zero = jnp.zeros_like(q)
            for qc in (jnp.where(comp0, q, zero), jnp.where(comp0, zero, q)):
                problems.append((qc, keys, vals))
    scores = [[_nt_dot(qc, kk) for kk in keys] for qc, keys, _ in problems]
    weights = []
    for ss in scores:
        mx = functools.reduce(jnp.maximum, [jnp.max(s, axis=-1, keepdims=True) for s in ss])
        es = [jnp.exp2(s - mx) for s in ss]
        den = functools.reduce(jnp.add, [jnp.sum(e, axis=-1, keepdims=True) for e in es])
        weights.append(([e.astype(BF16) for e in es], 1.0 / den))
    maps = [functools.reduce(jnp.add, [jnp.dot(e, vv, preferred_element_type=F32) for e, vv in zip(es, vals)]) * r
            for (es, r), (_, _, vals) in zip(weights, problems)]
    idx = 0
    for hh in range(q_ref.shape[0]):
        for r in range(tq // sub):
            o = maps[idx] - lam * maps[idx + 1]
            idx += 2
            o = o * lax.rsqrt(jnp.mean(o * o, axis=-1, keepdims=True) + RMS_EPS) * g_ref[...]
            o_ref[hh, 0, r * sub:(r + 1) * sub, :] = (o * (1.0 - lam_init)).astype(o_ref.dtype)


def _attention(q, k, v, ctx_kv, lam_params, subln_g, lam_init):
    (q_arr, q_row), (k_arr, k_row), (v_arr, v_row) = q, k, v
    _, bsz, t, _ = q_arr.shape
    tq = min(ATT_TQ, t)
    hp = N_HEADS_A if t <= ATT_SHORT_SEQ else 1
    const = lambda b, h, i: (0, 0)
    rows = lambda first: (lambda b, h, i: (first // hp + h, b, 0, 0))
    in_specs = [
        pl.BlockSpec((4, DH_A), const),
        pl.BlockSpec((1, DV_A), const),
        pl.BlockSpec((hp, 1, tq, DV_A), lambda b, h, i: (q_row // hp + h, b, i, 0)),
        pl.BlockSpec((hp, 1, t, DV_A), rows(k_row)),
        pl.BlockSpec((hp, 1, t, DV_A), rows(v_row)),
    ]
    args = [lam_params, subln_g, q_arr, k_arr, v_arr]
    if ctx_kv is not None:
        p = ctx_kv[0].shape[1]
        head_all = lambda b, h, i: (b, 0, h)
        in_specs += [pl.BlockSpec((1, p, hp * DV_A), head_all), pl.BlockSpec((1, p, hp * DV_A), head_all)]
        args += list(ctx_kv)
    return pl.pallas_call(
        functools.partial(_attn_kernel, lam_init=lam_init, has_ctx=ctx_kv is not None),
        grid=(bsz, N_HEADS_A // hp, t // tq),
        in_specs=in_specs,
        out_specs=pl.BlockSpec((hp, 1, tq, DV_A), lambda b, h, i: (h, b, i, 0)),
        out_shape=jax.ShapeDtypeStruct((N_HEADS_A, bsz, t, DV_A), BF16),
        compiler_params=_params(("parallel", "parallel", "arbitrary")),
        name="attention",
    )(*args)


def _chunk_scan(g, reverse):
    r = g.shape[0]
    pos = lax.broadcasted_iota(jnp.int32, (r, 1), 0) % HGRN_C
    b = g
    sh = 1
    while sh < HGRN_C:
        if reverse:
            b = b + jnp.where(pos < HGRN_C - sh, pltpu.roll(b, r - sh, 0), 0.0)
        else:
            b = b + jnp.where(pos >= sh, pltpu.roll(b, sh, 0), 0.0)
        sh *= 2
    return b


def _hgrn_kernel(*refs, seq_len, has_s0, want_state):
    refs = list(refs)
    hq_ref, hff_ref, hfb_ref, hi_ref, hg_ref, lbl_ref, ng_ref = refs[:7]
    refs = refs[7:]
    s0_ref = refs.pop(0) if has_s0 else None
    o_ref = refs.pop(0)
    sfin_ref = refs.pop(0) if want_state else None
    g_scr, kd_scr, q_scr, od_scr, st_scr, b_scr, ok_scr = refs
    hp = hq_ref.shape[0]
    c = HGRN_C
    rb = min(HGRN_ROWS, seq_len)
    nb = seq_len // rb
    cpb = rb // c
    z_refs = (hff_ref, hfb_ref)

    def lower_bound(hh, d):
        logits = lbl_ref[d, :, hh * DK_H:(hh + 1) * DK_H]
        mx = jnp.max(logits, axis=0, keepdims=True)
        e = jnp.exp(logits - mx)
        return e[0:1] / jnp.sum(e, axis=0, keepdims=True)

    lbs = [[lower_bound(hh, d) for d in range(2)] for hh in range(hp)]

    def gates(i, carry):
        rows = pl.ds(pl.multiple_of(i * rb, rb), rb)
        for d in range(2):
            worst = jnp.zeros((1, DK_H), F32)
            for hh in range(hp):
                if d == 0:
                    hq = hq_ref[hh, 0, rows, :]
                    q_scr[hh, rows, :] = hq * jax.nn.sigmoid(hq)
                z = z_refs[d][hh, 0, rows, :]
                lb = lbs[hh][d]
                e = jnp.exp(-jnp.abs(z))
                r = 1.0 / (1.0 + e)
                er = e * r
                pos = z >= 0.0
                g = jnp.log(lb + (1.0 - lb) * jnp.where(pos, r, er))
                g_scr[hh, d, rows, :] = g
                kd_scr[hh, d, rows, :] = (1.0 - lb) * jnp.where(pos, er, r)
                for j in range(cpb):
                    worst = jnp.maximum(worst, -jnp.sum(g[j * c:(j + 1) * c], axis=0, keepdims=True))
            ok_scr[d, i] = (jnp.max(worst) <= HGRN_SAFE_DECAY).astype(jnp.int32)
        return carry

    lax.fori_loop(0, nb, gates, 0)

    for hh in range(hp):
        for d in range(2):
            if has_s0:
                st_scr[hh, d] = s0_ref[0, d, hh].T
            else:
                st_scr[hh, d] = jnp.zeros((DV_H, DK_H), F32)

    row = lax.broadcasted_iota(jnp.int32, (c, 1), 0)
    col = lax.broadcasted_iota(jnp.int32, (1, c), 1)

    def fast_block(d, r0):
        reverse = d == 1
        rows = pl.ds(r0, rb)
        n = hp * cpb
        flat = lambda a: a.reshape(hp * rb, a.shape[-1])
        chunks = lambda a: a.reshape(n, c, a.shape[-1])
        vb = chunks(hi_ref[:, 0, rows, :].astype(BF16))
        b = _chunk_scan(flat(g_scr[:, d, rows, :]), reverse)
        qd = chunks((flat(q_scr[:, rows, :]) * jnp.exp(b)).astype(BF16))
        ku = chunks(flat(kd_scr[:, d, rows, :]) * jnp.exp(-b))
        b = chunks(b)
        e_end = jnp.exp(b[:, 0:1, :] if reverse else b[:, c - 1:c, :])
        causal = ((row <= col) if reverse else (row >= col))[None]
        a = jnp.einsum('nck,nsk->ncs', qd, ku.astype(BF16), preferred_element_type=F32)
        o = jnp.einsum('ncs,nsv->ncv', jnp.where(causal, a, 0.0).astype(BF16), vb, preferred_element_type=F32)
        u = jnp.einsum('nsv,nsk->nvk', vb, (ku * e_end).astype(BF16), preferred_element_type=F32)
        entering = [None] * n
        for hh in range(hp):
            st = st_scr[hh, d]
            for j in (range(cpb - 1, -1, -1) if reverse else range(cpb)):
                entering[hh * cpb + j] = st.astype(BF16)
                st = e_end[hh * cpb + j] * st + u[hh * cpb + j]
            st_scr[hh, d] = st
        o = o + jnp.einsum('nck,nvk->ncv', qd, jnp.stack(entering), preferred_element_type=F32)
        od_scr[:, d, rows, :] = o.reshape(hp, rb, DV_H)

    def exact_chunk(hh, d, r0):
        reverse = d == 1
        rows = pl.ds(r0, c)
        b = _chunk_scan(g_scr[hh, d, rows, :], reverse)
        b_scr[...] = b

        def one_row(t, carry):
            bt = b_scr[pl.ds(t, 1), :]
            qt = q_scr[hh, pl.ds(r0 + t, 1), :]
            seen = (row >= t) if reverse else (row <= t)
            w = jnp.exp(jnp.where(seen, bt - b_scr[...], -jnp.inf))
            p = jnp.sum(qt * kd_scr[hh, d, rows, :] * w, axis=-1, keepdims=True)
            od_scr[hh, d, pl.ds(r0 + t, 1), :] = jnp.sum(p * hi_ref[hh, 0, rows, :], axis=0, keepdims=True)
            return carry

        lax.fori_loop(0, c, one_row, 0)
        b_end = b[0:1] if reverse else b[c - 1:c]
        st = st_scr[hh, d]
        qd = (q_scr[hh, rows, :] * jnp.exp(b)).astype(BF16)
        od_scr[hh, d, rows, :] = od_scr[hh, d, rows, :] + _nt_dot(qd, st.astype(BF16))
        k_end = (kd_scr[hh, d, rows, :] * jnp.exp(b_end - b)).astype(BF16)
        st_scr[hh, d] = jnp.exp(b_end) * st + _tn_dot(hi_ref[hh, 0, rows, :].astype(BF16), k_end)

    def block(i, carry):
        starts = (pl.multiple_of(i * rb, rb), pl.multiple_of((nb - 1 - i) * rb, rb))
        mild = (ok_scr[0, i] + ok_scr[1, nb - 1 - i]) == 2

        @pl.when(mild)
        def _():
            for d in range(2):
                fast_block(d, starts[d])

        @pl.when(jnp.logical_not(mild))
        def _():
            for hh in range(hp):
                for d in range(2):
                    for j in (range(cpb - 1, -1, -1) if d == 1 else range(cpb)):
                        exact_chunk(hh, d, pl.multiple_of(starts[d] + j * c, c))

        return carry

    lax.fori_loop(0, nb, block, 0)

    def finish(i, carry):
        rows = pl.ds(pl.multiple_of(i * rb, rb), rb)
        for hh in range(hp):
            o = od_scr[hh, 0, rows, :] + od_scr[hh, 1, rows, :]
            o = o * lax.rsqrt(jnp.mean(o * o, axis=-1, keepdims=True) + RMS_EPS) * ng_ref[...]
            hg = hg_ref[hh, 0, rows, :]
            o_ref[hh, 0, rows, :] = (o * (hg * jax.nn.sigmoid(hg))).astype(o_ref.dtype)
        return carry

    lax.fori_loop(0, nb, finish, 0)
    if want_state:
        for hh in range(hp):
            for d in range(2):
                sfin_ref[0, d, hh] = st_scr[hh, d].T


def _hgrn(streams, lb_logits, norm_g, s0, want_state):
    _, bsz, t, _ = streams.shape
    hp = HGRN_SHORT_HEADS if t <= HGRN_ROWS else 1
    stream = lambda s: pl.BlockSpec((hp, 1, t, DK_H), lambda b, h: (s * N_HEADS_H // hp + h, b, 0, 0))
    in_specs = [stream(HGRN_BLOCKS.index(w)) for w in (W_HQ, W_HFF, W_HFB, W_HI, W_HG)]
    in_specs += [
        pl.BlockSpec((2, DEPTH + 1, hp * DK_H), lambda b, h: (0, 0, h)),
        pl.BlockSpec((1, DV_H), lambda b, h: (0, 0)),
    ]
    args = [streams] * 5 + [lb_logits, norm_g]
    state_spec = pl.BlockSpec((1, 2, hp, DK_H, DV_H), lambda b, h: (b, 0, h, 0, 0))
    if s0 is not None:
        in_specs.append(state_spec)
        args.append(s0)
    out_specs = [pl.BlockSpec((hp, 1, t, DV_H), lambda b, h: (h, b, 0, 0))]
    out_shape = [jax.ShapeDtypeStruct((N_HEADS_H, bsz, t, DV_H), BF16)]
    if want_state:
        out_specs.append(state_spec)
        out_shape.append(jax.ShapeDtypeStruct((bsz, 2, N_HEADS_H, DK_H, DV_H), F32))
    n_blocks = t // min(HGRN_ROWS, t)
    outs = pl.pallas_call(
        functools.partial(_hgrn_kernel, seq_len=t, has_s0=s0 is not None, want_state=want_state),
        grid=(bsz, N_HEADS_H // hp),
        in_specs=in_specs,
        out_specs=out_specs,
        out_shape=out_shape,
        scratch_shapes=[
            pltpu.VMEM((hp, 2, t, DK_H), F32),
            pltpu.VMEM((hp, 2, t, DK_H), F32),
            pltpu.VMEM((hp, t, DK_H), F32),
            pltpu.VMEM((hp, 2, t, DV_H), F32),
            pltpu.VMEM((hp, 2, DV_H, DK_H), F32),
            pltpu.VMEM((HGRN_C, DK_H), F32),
            pltpu.SMEM((2, n_blocks), jnp.int32),
        ],
        compiler_params=_params(("parallel", "parallel")),
        name="hgrn",
    )(*args)
    return outs if want_state else (outs[0], None)


def _merge_kernel(oa_ref, oh_ref, sa_ref, sh_ref, x_ref, mod_ref, wa_ref, wh_ref, wo_ref, lng_ref, lnb_ref,
                  xo_ref, ho_ref, wa_scr, wh_scr, wo_scr):
    @pl.when(pl.program_id(0) == 0)
    def _():
        wa_scr[...] = wa_ref[...].astype(BF16)
        wh_scr[...] = wh_ref[...].astype(BF16)
        wo_scr[...] = wo_ref[...].astype(BF16)

    heads = lambda ref: jnp.concatenate([ref[hh] for hh in range(ref.shape[0])], axis=-1)
    ba = jnp.dot(heads(oa_ref), wa_scr[...], preferred_element_type=F32)
    bh = jnp.dot(heads(oh_ref), wh_scr[...], preferred_element_type=F32)
    merged = sa_ref[...].astype(F32) * ba + sh_ref[...].astype(F32) * bh
    mix = jnp.dot(merged.astype(BF16), wo_scr[...], preferred_element_type=F32)
    y = ALPHA * x_ref[...] + mod_ref[0, 5:6, :] * mix
    xn = _layer_norm(y, lng_ref[1:2, :], lnb_ref[1:2, :])
    xo_ref[...] = xn
    ho_ref[...] = (xn * (1.0 + mod_ref[0, 7:8, :]) + mod_ref[0, 6:7, :]).astype(BF16)


def _merge(o_a, o_h, gates, x, mod, cond_of_tile, w_a, w_h, w_o, ln_g, ln_b):
    m = x.shape[0]
    row = lambda i: (i, 0)
    const = lambda i: (0, 0)
    per_head = lambda i: (0, i, 0)
    return pl.pallas_call(
        _merge_kernel,
        grid=(m // MERGE_TM,),
        in_specs=[
            pl.BlockSpec((N_HEADS_A, MERGE_TM, DV_A), per_head),
            pl.BlockSpec((N_HEADS_H, MERGE_TM, DV_H), per_head),
            pl.BlockSpec((MERGE_TM, D_MODEL), lambda i: (i, 0)),
            pl.BlockSpec((MERGE_TM, D_MODEL), lambda i: (i, 1)),
            pl.BlockSpec((MERGE_TM, D_MODEL), row),
            pl.BlockSpec((1, 9, D_MODEL), lambda i: (cond_of_tile(i, MERGE_TM), 0, 0)),
            pl.BlockSpec((W_A, D_MODEL), const),
            pl.BlockSpec((W_H, D_MODEL), const),
            pl.BlockSpec((D_MODEL, D_MODEL), const),
            pl.BlockSpec((3, D_MODEL), const),
            pl.BlockSpec((3, D_MODEL), const),
        ],
        out_specs=[pl.BlockSpec((MERGE_TM, D_MODEL), row), pl.BlockSpec((MERGE_TM, D_MODEL), row)],
        out_shape=[jax.ShapeDtypeStruct((m, D_MODEL), F32), jax.ShapeDtypeStruct((m, D_MODEL), BF16)],
        scratch_shapes=[pltpu.VMEM((W_A, D_MODEL), BF16), pltpu.VMEM((W_H, D_MODEL), BF16),
                        pltpu.VMEM((D_MODEL, D_MODEL), BF16)],
        compiler_params=_params(("arbitrary",)),
        name="merge",
    )(o_a, o_h, gates, gates, x, mod, w_a, w_h, w_o, ln_g, ln_b)


def _rope_tables(n_tok):
    rows = n_tok // GRID_W
    row = np.repeat(np.arange(rows, dtype=np.float32), GRID_W)
    col = np.tile(np.arange(GRID_W, dtype=np.float32), rows)
    half = DH_A // 2
    inv = (ROPE_BASE ** (-np.arange(0, half, 2, dtype=np.float32) / half)).astype(np.float32)
    ar = row[:, None] * inv
    ac = col[:, None] * inv
    ang = np.concatenate([ar, ar, ac, ac] * 2, axis=-1)
    cos, sin = np.cos(ang).astype(np.float32), np.sin(ang).astype(np.float32)
    scale = np.float32(Q_SCALE)
    return (jnp.asarray(np.stack([cos * scale, cos, np.ones_like(cos)])),
            jnp.asarray(np.stack([sin * scale, sin, np.zeros_like(sin)])))


def _layer_pass(x, mod, cond_of_tile, wts, lam_init, bsz, ctx):
    m = x.shape[0]
    t = m // bsz
    latent = ctx is not None
    per_head = lambda a: a.reshape(a.shape[0], bsz, t, a.shape[-1])
    x1, h1 = _ffn(x, mod, cond_of_tile, wts['ffn1_w_in'], wts['ffn1_w_out'], wts['ln_g'], wts['ln_b'], 0, 1)
    w_mix = wts['w_mix_in']
    if latent:
        (qkv,) = _project(h1, w_mix, W_Q, 3, [('head', BF16)], rope_tables=_rope_tables(t), name="proj_qkv")
        k = v = None
        ctx_kv = (ctx[0], ctx[1])
    else:
        qkv, qkv32 = _project(h1, w_mix, W_Q, 3, [('head', BF16), ('stack', F32)], scale_first=True,
                              name="proj_qkv")
        k, v = qkv32[W_K], qkv32[W_V]
        ctx_kv = None
    (streams,) = _project(h1, w_mix, W_HQ, len(HGRN_BLOCKS), [('head', F32)], name="proj_hgrn")
    (gates,) = _project(h1, w_mix, W_GA0, len(GATE_BLOCKS), [('tok', BF16)], sigmoid=True, name="proj_gates")
    qkv = per_head(qkv)
    q_kv = [(qkv, W_Q * N_HEADS_A), (qkv, W_K * N_HEADS_A), (qkv, W_V * N_HEADS_A)]
    o_a = _attention(*q_kv, ctx_kv, wts['lam_params'], wts['attn_subln_g'], lam_init)
    o_h, s_fin = _hgrn(per_head(streams), wts['lb_logits'], wts['hgrn_norm_g'],
                       ctx[2] if latent else None, not latent)
    x2, h2 = _merge(o_a.reshape(N_HEADS_A, m, DV_A), o_h.reshape(N_HEADS_H, m, DV_H), gates, x1, mod,
                    cond_of_tile, wts['w_branch_a'], wts['w_branch_h'], wts['w_mix_out'], wts['ln_g'], wts['ln_b'])
    (y,) = _ffn(x2, mod, cond_of_tile, wts['ffn2_w_in'], wts['ffn2_w_out'], wts['ln_g'], wts['ln_b'], 2, None)
    return y, (k, v, s_fin)


def kernel(x_prompt, x_sample, cache_k, cache_v, state_hgrn, c, c_ctx, w_ada, b_ada, ffn1_w_in, ffn1_w_out,
           w_mix_in, lambda_q1, lambda_k1, lambda_q2, lambda_k2, attn_subln_g, hgrn_lb_logits, hgrn_norm_g,
           w_branch_a, w_branch_h, w_mix_out, ffn2_w_in, ffn2_w_out, ln_g, ln_b):
    batch, seq, _ = x_prompt.shape
    dec_batch, dec_seq, _ = x_sample.shape
    past = cache_k.shape[2]
    assert DEPTH == 1 and dec_batch + 1 <= MOD_ROWS
    lam_init = 0.8 - 0.6 * math.exp(-0.3 * 0)

    cond = jnp.concatenate([c_ctx[None, :], c, jnp.zeros((MOD_ROWS - 1 - dec_batch, D_MODEL), F32)], axis=0)
    mod = _modulation(cond, w_ada[0], b_ada[0])

    wts = {
        'ffn1_w_in': _to_bf16(ffn1_w_in[0]), 'ffn1_w_out': _to_bf16(ffn1_w_out[0]), 'w_mix_in': w_mix_in[0],
        'lam_params': jnp.concatenate([lambda_q1, lambda_k1, lambda_q2, lambda_k2], axis=0),
        'attn_subln_g': attn_subln_g, 'lb_logits': hgrn_lb_logits, 'hgrn_norm_g': hgrn_norm_g,
        'w_branch_a': w_branch_a[0], 'w_branch_h': w_branch_h[0], 'w_mix_out': w_mix_out[0],
        'ffn2_w_in': _to_bf16(ffn2_w_in[0]), 'ffn2_w_out': _to_bf16(ffn2_w_out[0]),
        'ln_g': ln_g[0], 'ln_b': ln_b[0],
    }

    y_p, (k_c, v_c, s_c) = _layer_pass(x_prompt.reshape(batch * seq, D_MODEL), mod, lambda i, tm: 0,
                                       wts, lam_init, batch, None)
    ctx = (cache_k[:, 0].reshape(dec_batch, past, QK_A), cache_v[:, 0].reshape(dec_batch, past, W_A),
           state_hgrn[:, 0])
    y_s, _ = _layer_pass(x_sample.reshape(dec_batch * dec_seq, D_MODEL), mod,
                         lambda i, tm: 1 + (i * tm) // dec_seq, wts, lam_init, dec_batch, ctx)

    return (y_p.reshape(batch, seq, D_MODEL),
            y_s.reshape(dec_batch, dec_seq, D_MODEL),
            k_c.reshape(batch, DEPTH, seq, N_HEADS_A, 2, DH_A),
            v_c.reshape(batch, DEPTH, seq, N_HEADS_A, DV_A),
            s_c.reshape(batch, DEPTH, 2, N_HEADS_H, DK_H, DV_H))
```

```python
import functools
import math

import numpy as np
import jax
import jax.numpy as jnp
from jax import lax
from jax.experimental import pallas as pl
from jax.experimental.pallas import tpu as pltpu

D_MODEL = 1024
DEPTH = 1
GRID_W = 64
N_HEADS_A = 4
DH_A = 64
DV_A = 2 * DH_A
QK_A = N_HEADS_A * 2 * DH_A
W_A = N_HEADS_A * DV_A
ROPE_BASE = 10000.0
N_HEADS_H = 4
DK_H = 128
DV_H = 128
QK_H = N_HEADS_H * DK_H
W_H = N_HEADS_H * DV_H
D_FF = 2816
MIX_IN = 2 * QK_A + W_A + 3 * QK_H + 2 * W_H + 2 * D_MODEL
ALPHA = (2 * DEPTH) ** 0.25
LN_EPS = 1e-5
RMS_EPS = 1e-6
Q_SCALE = DH_A ** -0.5 * math.log2(math.e)

F32 = jnp.float32
BF16 = jnp.bfloat16

LANES = 128
VMEM_LIMIT = 56 * 1024 * 1024

FFN_TM = 512
FFN_TF = 256
MIX_TM = 4096
MIX_TN = 512
ATT_TQ = 1024
ATT_SUB = 512
ATT_SHORT_SEQ = 256
HGRN_C = 64
HGRN_ROWS = 512
HGRN_SHORT_HEADS = 4
HGRN_SAFE_DECAY = 75.0
MOD_ROWS = 8

W_Q, W_K, W_V, W_HQ, W_HFF, W_HFB, W_HI, W_HG, W_GA0, W_GA1, W_GH0, W_GH1 = range(12)
HGRN_BLOCKS = (W_HQ, W_HFF, W_HFB, W_HI, W_HG)
GATE_BLOCKS = (W_GA0, W_GA1, W_GH0, W_GH1)


def _params(sem):
    return pltpu.CompilerParams(dimension_semantics=sem, vmem_limit_bytes=VMEM_LIMIT)


def _nt_dot(a, b):
    return lax.dot_general(a, b, (((1,), (1,)), ((), ())), preferred_element_type=F32)


def _tn_dot(a, b):
    return lax.dot_general(a, b, (((0,), (0,)), ((), ())), preferred_element_type=F32)


def _layer_norm(y, g, b):
    mu = jnp.mean(y, axis=-1, keepdims=True)
    yc = y - mu
    var = jnp.mean(yc * yc, axis=-1, keepdims=True)
    return yc * lax.rsqrt(var + LN_EPS) * g + b


def _mod_kernel(c_ref, w_ref, b_ref, o_ref):
    c = c_ref[...]
    a = (c * jax.nn.sigmoid(c)).astype(BF16)
    o_ref[...] = jnp.dot(a, w_ref[...].astype(BF16), preferred_element_type=F32) + b_ref[...]


def _modulation(cond, w_ada, b_ada):
    n_out = w_ada.shape[1]
    tn = D_MODEL
    out = pl.pallas_call(
        _mod_kernel,
        grid=(n_out // tn,),
        in_specs=[
            pl.BlockSpec((MOD_ROWS, D_MODEL), lambda n: (0, 0)),
            pl.BlockSpec((D_MODEL, tn), lambda n: (0, n)),
            pl.BlockSpec((1, tn), lambda n: (0, n)),
        ],
        out_specs=pl.BlockSpec((MOD_ROWS, tn), lambda n: (0, n)),
        out_shape=jax.ShapeDtypeStruct((MOD_ROWS, n_out), F32),
        compiler_params=_params(("arbitrary",)),
        name="modulation",
    )(cond, w_ada, b_ada.reshape(1, n_out))
    return out.reshape(MOD_ROWS, 9, D_MODEL)


def _cast_kernel(w_ref, o_ref):
    o_ref[...] = w_ref[...].astype(o_ref.dtype)


def _to_bf16(w, row_tiles=8):
    rows, cols = w.shape
    tr = rows // row_tiles
    return pl.pallas_call(
        _cast_kernel,
        grid=(row_tiles,),
        in_specs=[pl.BlockSpec((tr, cols), lambda i: (i, 0))],
        out_specs=pl.BlockSpec((tr, cols), lambda i: (i, 0)),
        out_shape=jax.ShapeDtypeStruct((rows, cols), BF16),
        compiler_params=_params(("parallel",)),
        name="to_bf16",
    )(w)


def _modulate(x, mod_ref, sub):
    return x * (1.0 + mod_ref[0, 3 * sub + 1:3 * sub + 2, :]) + mod_ref[0, 3 * sub:3 * sub + 1, :]


def _half_step(x, mod_ref, win_ref, wout_ref, lng_ref, lnb_ref, act_scr, sub):
    h = _modulate(x, mod_ref, sub).astype(BF16)
    for c in range(D_FF // FFN_TF):
        cols = slice(c * FFN_TF, (c + 1) * FFN_TF)
        up_cols = slice(D_FF + c * FFN_TF, D_FF + (c + 1) * FFN_TF)
        gate = jnp.dot(h, win_ref[:, cols], preferred_element_type=F32)
        up = jnp.dot(h, win_ref[:, up_cols], preferred_element_type=F32)
        act_scr[:, cols] = (gate * jax.nn.sigmoid(gate) * up).astype(BF16)
    f = jnp.dot(act_scr[...], wout_ref[...], preferred_element_type=F32)
    g = mod_ref[0, 3 * sub + 2:3 * sub + 3, :]
    return _layer_norm(ALPHA * x + 0.5 * g * f, lng_ref[sub:sub + 1, :], lnb_ref[sub:sub + 1, :])


def _ffn1_kernel(x_ref, mod_ref, win_ref, wout_ref, lng_ref, lnb_ref, xo_ref, ho_ref, act_scr):
    xn = _half_step(x_ref[...], mod_ref, win_ref, wout_ref, lng_ref, lnb_ref, act_scr, 0)
    xo_ref[...] = xn
    ho_ref[...] = _modulate(xn, mod_ref, 1).astype(BF16)


def _ffn1(x, mod, cond_of_tile, w_in, w_out, ln_g, ln_b):
    m = x.shape[0]
    row = lambda i: (i, 0)
    const = lambda i: (0, 0)
    resident = pl.Buffered(1)
    return pl.pallas_call(
        _ffn1_kernel,
        grid=(m // FFN_TM,),
        in_specs=[
            pl.BlockSpec((FFN_TM, D_MODEL), row),
            pl.BlockSpec((1, 9, D_MODEL), lambda i: (cond_of_tile(i, FFN_TM), 0, 0)),
            pl.BlockSpec((D_MODEL, 2 * D_FF), const, pipeline_mode=resident),
            pl.BlockSpec((D_FF, D_MODEL), const, pipeline_mode=resident),
            pl.BlockSpec((3, D_MODEL), const),
            pl.BlockSpec((3, D_MODEL), const),
        ],
        out_specs=[pl.BlockSpec((FFN_TM, D_MODEL), row), pl.BlockSpec((FFN_TM, D_MODEL), row)],
        out_shape=[jax.ShapeDtypeStruct((m, D_MODEL), F32), jax.ShapeDtypeStruct((m, D_MODEL), BF16)],
        scratch_shapes=[pltpu.VMEM((FFN_TM, D_FF), BF16)],
        compiler_params=_params(("parallel",)),
        name="ffn1",
    )(x, mod, w_in, w_out, ln_g, ln_b)


def _rope(x, cos, sin):
    lane = lax.broadcasted_iota(jnp.int32, (1, LANES), 1)
    first_half = (lane % 32) < 16
    outs = []
    for hb in range(QK_A // LANES):
        xb = x[:, hb * LANES:(hb + 1) * LANES]
        ahead = pltpu.roll(xb, LANES - 16, 1)
        behind = pltpu.roll(xb, 16, 1)
        rot = jnp.where(first_half, -ahead, behind)
        outs.append(xb * cos + rot * sin)
    return jnp.concatenate(outs, axis=-1)


def _proj_kernel(*refs, rope, sigmoid, scale_first, outs):
    if rope:
        h_ref, w_ref, cos_ref, sin_ref = refs[:4]
        refs = refs[4:]
    else:
        h_ref, w_ref = refs[:2]
        refs = refs[2:]
    out_refs, w_scr = refs[:-1], refs[-1]

    @pl.when(pl.program_id(1) == 0)
    def _():
        w_scr[...] = w_ref[...].astype(BF16)

    y = jnp.dot(h_ref[...], w_scr[...], preferred_element_type=F32)
    if rope:
        t_rope = cos_ref.shape[1]
        y = jnp.concatenate([_rope(y[r * t_rope:(r + 1) * t_rope], cos_ref[0], sin_ref[0])
                             for r in range(y.shape[0] // t_rope)], axis=0)
    if scale_first:
        y = y * jnp.where(pl.program_id(0) == 0, Q_SCALE, 1.0)
    if sigmoid:
        y = jax.nn.sigmoid(y)
    for ref, kind in zip(out_refs, outs):
        if kind == 'head':
            for hh in range(MIX_TN // LANES):
                ref[hh] = y[:, hh * LANES:(hh + 1) * LANES].astype(ref.dtype)
        elif kind == 'stack':
            ref[0] = y.astype(ref.dtype)
        else:
            ref[...] = y.astype(ref.dtype)


def _project(h, w, first_block, n_blocks, outs, rope_tables=None, sigmoid=False, scale_first=False, name="proj"):
    m = h.shape[0]
    rope = rope_tables is not None
    heads = MIX_TN // LANES
    h_mode = dict(pipeline_mode=pl.Buffered(1)) if m == MIX_TM else {}
    in_specs = [pl.BlockSpec((MIX_TM, D_MODEL), lambda n, i: (i, 0), **h_mode),
                pl.BlockSpec((D_MODEL, MIX_TN), lambda n, i: (0, first_block + n))]
    args = [h, w]
    if rope:
        t_rope = rope_tables[0].shape[1]
        rope_rows = min(t_rope, MIX_TM)
        n_rope_tiles = t_rope // rope_rows
        for t in rope_tables:
            in_specs.append(pl.BlockSpec((1, rope_rows, DV_A), lambda n, i: (n, i % n_rope_tiles, 0)))
            args.append(t)
    specs = {
        'head': (pl.BlockSpec((heads, MIX_TM, LANES), lambda n, i: (n, i, 0)), (n_blocks * heads, m, LANES)),
        'tok': (pl.BlockSpec((MIX_TM, MIX_TN), lambda n, i: (i, n)), (m, n_blocks * MIX_TN)),
        'stack': (pl.BlockSpec((1, MIX_TM, MIX_TN), lambda n, i: (n, i, 0)), (n_blocks, m, MIX_TN)),
    }
    return pl.pallas_call(
        functools.partial(_proj_kernel, rope=rope, sigmoid=sigmoid, scale_first=scale_first,
                          outs=tuple(kind for kind, _ in outs)),
        grid=(n_blocks, m // MIX_TM),
        in_specs=in_specs,
        out_specs=[specs[kind][0] for kind, _ in outs],
        out_shape=[jax.ShapeDtypeStruct(specs[kind][1], dt) for kind, dt in outs],
        scratch_shapes=[pltpu.VMEM((D_MODEL, MIX_TN), BF16)],
        compiler_params=_params(("arbitrary", "arbitrary")),
        name=name,
    )(*args)


def _attn_kernel(*refs, lam_init, has_ctx):
    if has_ctx:
        lam_ref, g_ref, q_ref, k_ref, v_ref, kc_ref, vc_ref, o_ref = refs
    else:
        lam_ref, g_ref, q_ref, k_ref, v_ref, o_ref = refs
    lp = lam_ref[...]
    lam = (jnp.exp(jnp.sum(lp[0:1] * lp[1:2], axis=-1, keepdims=True))
           - jnp.exp(jnp.sum(lp[2:3] * lp[3:4], axis=-1, keepdims=True)) + lam_init)

    lane = lax.broadcasted_iota(jnp.int32, (1, DV_A), 1)
    comp0 = lane < DH_A
    tq = q_ref.shape[2]
    sub = min(ATT_SUB, tq)
    problems = []
    for hh in range(q_ref.shape[0]):
        keys = [k_ref[hh, 0]]
        vals = [v_ref[hh, 0]]
        if has_ctx:
            keys.append(kc_ref[0, :, hh * DV_A:(hh + 1) * DV_A].astype(BF16))
            vals.append(vc_ref[0, :, hh * DV_A:(hh + 1) * DV_A].astype(BF16))
        for r in range(tq // sub):
            q = q_ref[hh, 0, r * sub:(r + 1) * sub, :]
            zero = jnp.zeros_like(q)
            for qc in (jnp.where(comp0, q, zero), jnp.where(comp0, zero, q)):
                problems.append((qc, keys, vals))
    scores = [[_nt_dot(qc, kk) for kk in keys] for qc, keys, _ in problems]
    weights = []
    for ss in scores:
        mx = functools.reduce(jnp.maximum, [jnp.max(s, axis=-1, keepdims=True) for s in ss])
        es = [jnp.exp2(s - mx) for s in ss]
        den = functools.reduce(jnp.add, [jnp.sum(e, axis=-1, keepdims=True) for e in es])
        weights.append(([e.astype(BF16) for e in es], 1.0 / den))
    maps = [functools.reduce(jnp.add, [jnp.dot(e, vv, preferred_element_type=F32) for e, vv in zip(es, vals)]) * r
            for (es, r), (_, _, vals) in zip(weights, problems)]
    idx = 0
    for hh in range(q_ref.shape[0]):
        for r in range(tq // sub):
            o = maps[idx] - lam * maps[idx + 1]
            idx += 2
            o = o * lax.rsqrt(jnp.mean(o * o, axis=-1, keepdims=True) + RMS_EPS) * g_ref[...]
            o_ref[hh, 0, r * sub:(r + 1) * sub, :] = (o * (1.0 - lam_init)).astype(o_ref.dtype)


def _attention(q, k, v, ctx_kv, lam_params, subln_g, lam_init):
    (q_arr, q_row), (k_arr, k_row), (v_arr, v_row) = q, k, v
    _, bsz, t, _ = q_arr.shape
    tq = min(ATT_TQ, t)
    hp = N_HEADS_A if t <= ATT_SHORT_SEQ else 1
    const = lambda b, h, i: (0, 0)
    rows = lambda first: (lambda b, h, i: (first // hp + h, b, 0, 0))
    in_specs = [
        pl.BlockSpec((4, DH_A), const),
        pl.BlockSpec((1, DV_A), const),
        pl.BlockSpec((hp, 1, tq, DV_A), lambda b, h, i: (q_row // hp + h, b, i, 0)),
        pl.BlockSpec((hp, 1, t, DV_A), rows(k_row)),
        pl.BlockSpec((hp, 1, t, DV_A), rows(v_row)),
    ]
    args = [lam_params, subln_g, q_arr, k_arr, v_arr]
    if ctx_kv is not None:
        p = ctx_kv[0].shape[1]
        head_all = lambda b, h, i: (b, 0, h)
        in_specs += [pl.BlockSpec((1, p, hp * DV_A), head_all), pl.BlockSpec((1, p, hp * DV_A), head_all)]
        args += list(ctx_kv)
    return pl.pallas_call(
        functools.partial(_attn_kernel, lam_init=lam_init, has_ctx=ctx_kv is not None),
        grid=(bsz, N_HEADS_A // hp, t // tq),
        in_specs=in_specs,
        out_specs=pl.BlockSpec((hp, 1, tq, DV_A), lambda b, h, i: (h, b, i, 0)),
        out_shape=jax.ShapeDtypeStruct((N_HEADS_A, bsz, t, DV_A), BF16),
        compiler_params=_params(("parallel", "parallel", "arbitrary")),
        name="attention",
    )(*args)


def _chunk_scan(g, reverse):
    r = g.shape[0]
    pos = lax.broadcasted_iota(jnp.int32, (r, 1), 0) % HGRN_C
    b = g
    sh = 1
    while sh < HGRN_C:
        if reverse:
            b = b + jnp.where(pos < HGRN_C - sh, pltpu.roll(b, r - sh, 0), 0.0)
        else:
            b = b + jnp.where(pos >= sh, pltpu.roll(b, sh, 0), 0.0)
        sh *= 2
    return b


def _hgrn_kernel(*refs, seq_len, has_s0, want_state):
    refs = list(refs)
    hq_ref, hff_ref, hfb_ref, hi_ref, hg_ref, lbl_ref, ng_ref = refs[:7]
    refs = refs[7:]
    s0_ref = refs.pop(0) if has_s0 else None
    o_ref = refs.pop(0)
    sfin_ref = refs.pop(0) if want_state else None
    g_scr, kd_scr, q_scr, od_scr, st_scr, b_scr, ok_scr = refs
    hp = hq_ref.shape[0]
    c = HGRN_C
    rb = min(HGRN_ROWS, seq_len)
    nb = seq_len // rb
    cpb = rb // c
    z_refs = (hff_ref, hfb_ref)

    def lower_bound(hh, d):
        logits = lbl_ref[d, :, hh * DK_H:(hh + 1) * DK_H]
        mx = jnp.max(logits, axis=0, keepdims=True)
        e = jnp.exp(logits - mx)
        return e[0:1] / jnp.sum(e, axis=0, keepdims=True)

    lbs = [[lower_bound(hh, d) for d in range(2)] for hh in range(hp)]

    def gates(i, carry):
        rows = pl.ds(pl.multiple_of(i * rb, rb), rb)
        for d in range(2):
            worst = jnp.zeros((1, DK_H), F32)
            for hh in range(hp):
                if d == 0:
                    hq = hq_ref[hh, 0, rows, :]
                    q_scr[hh, rows, :] = hq * jax.nn.sigmoid(hq)
                z = z_refs[d][hh, 0, rows, :]
                lb = lbs[hh][d]
                e = jnp.exp(-jnp.abs(z))
                r = 1.0 / (1.0 + e)
                er = e * r
                pos = z >= 0.0
                g = jnp.log(lb + (1.0 - lb) * jnp.where(pos, r, er))
                g_scr[hh, d, rows, :] = g
                kd_scr[hh, d, rows, :] = (1.0 - lb) * jnp.where(pos, er, r)
                for j in range(cpb):
                    worst = jnp.maximum(worst, -jnp.sum(g[j * c:(j + 1) * c], axis=0, keepdims=True))
            ok_scr[d, i] = (jnp.max(worst) <= HGRN_SAFE_DECAY).astype(jnp.int32)
        return carry

    lax.fori_loop(0, nb, gates, 0)

    for hh in range(hp):
        for d in range(2):
            if has_s0:
                st_scr[hh, d] = s0_ref[0, d, hh].T
            else:
                st_scr[hh, d] = jnp.zeros((DV_H, DK_H), F32)

    row = lax.broadcasted_iota(jnp.int32, (c, 1), 0)
    col = lax.broadcasted_iota(jnp.int32, (1, c), 1)

    def fast_block(d, r0):
        reverse = d == 1
        rows = pl.ds(r0, rb)
        n = hp * cpb
        flat = lambda a: a.reshape(hp * rb, a.shape[-1])
        chunks = lambda a: a.reshape(n, c, a.shape[-1])
        vb = chunks(hi_ref[:, 0, rows, :].astype(BF16))
        b = _chunk_scan(flat(g_scr[:, d, rows, :]), reverse)
        qd = chunks((flat(q_scr[:, rows, :]) * jnp.exp(b)).astype(BF16))
        ku = chunks(flat(kd_scr[:, d, rows, :]) * jnp.exp(-b))
        b = chunks(b)
        e_end = jnp.exp(b[:, 0:1, :] if reverse else b[:, c - 1:c, :])
        causal = ((row <= col) if reverse else (row >= col))[None]
        a = jnp.einsum('nck,nsk->ncs', qd, ku.astype(BF16), preferred_element_type=F32)
        o = jnp.einsum('ncs,nsv->ncv', jnp.where(causal, a, 0.0).astype(BF16), vb, preferred_element_type=F32)
        u = jnp.einsum('nsv,nsk->nvk', vb, (ku * e_end).astype(BF16), preferred_element_type=F32)
        entering = [None] * n
        for hh in range(hp):
            st = st_scr[hh, d]
            for j in (range(cpb - 1, -1, -1) if reverse else range(cpb)):
                entering[hh * cpb + j] = st.astype(BF16)
                st = e_end[hh * cpb + j] * st + u[hh * cpb + j]
            st_scr[hh, d] = st
        o = o + jnp.einsum('nck,nvk->ncv', qd, jnp.stack(entering), preferred_element_type=F32)
        od_scr[:, d, rows, :] = o.reshape(hp, rb, DV_H)

    def exact_chunk(hh, d, r0):
        reverse = d == 1
        rows = pl.ds(r0, c)
        b = _chunk_scan(g_scr[hh, d, rows, :], reverse)
        b_scr[...] = b

        def one_row(t, carry):
            bt = b_scr[pl.ds(t, 1), :]
            qt = q_scr[hh, pl.ds(r0 + t, 1), :]
            seen = (row >= t) if reverse else (row <= t)
            w = jnp.exp(jnp.where(seen, bt - b_scr[...], -jnp.inf))
            p = jnp.sum(qt * kd_scr[hh, d, rows, :] * w, axis=-1, keepdims=True)
            od_scr[hh, d, pl.ds(r0 + t, 1), :] = jnp.sum(p * hi_ref[hh, 0, rows, :], axis=0, keepdims=True)
            return carry

        lax.fori_loop(0, c, one_row, 0)
        b_end = b[0:1] if reverse else b[c - 1:c]
        st = st_scr[hh, d]
        qd = (q_scr[hh, rows, :] * jnp.exp(b)).astype(BF16)
        od_scr[hh, d, rows, :] = od_scr[hh, d, rows, :] + _nt_dot(qd, st.astype(BF16))
        k_end = (kd_scr[hh, d, rows, :] * jnp.exp(b_end - b)).astype(BF16)
        st_scr[hh, d] = jnp.exp(b_end) * st + _tn_dot(hi_ref[hh, 0, rows, :].astype(BF16), k_end)

    def block(i, carry):
        starts = (pl.multiple_of(i * rb, rb), pl.multiple_of((nb - 1 - i) * rb, rb))
        mild = (ok_scr[0, i] + ok_scr[1, nb - 1 - i]) == 2

        @pl.when(mild)
        def _():
            for d in range(2):
                fast_block(d, starts[d])

        @pl.when(jnp.logical_not(mild))
        def _():
            for hh in range(hp):
                for d in range(2):
                    for j in (range(cpb - 1, -1, -1) if d == 1 else range(cpb)):
                        exact_chunk(hh, d, pl.multiple_of(starts[d] + j * c, c))

        return carry

    lax.fori_loop(0, nb, block, 0)

    def finish(i, carry):
        rows = pl.ds(pl.multiple_of(i * rb, rb), rb)
        for hh in range(hp):
            o = od_scr[hh, 0, rows, :] + od_scr[hh, 1, rows, :]
            o = o * lax.rsqrt(jnp.mean(o * o, axis=-1, keepdims=True) + RMS_EPS) * ng_ref[...]
            hg = hg_ref[hh, 0, rows, :]
            o_ref[hh, 0, rows, :] = (o * (hg * jax.nn.sigmoid(hg))).astype(o_ref.dtype)
        return carry

    lax.fori_loop(0, nb, finish, 0)
    if want_state:
        for hh in range(hp):
            for d in range(2):
                sfin_ref[0, d, hh] = st_scr[hh, d].T


def _hgrn(streams, lb_logits, norm_g, s0, want_state):
    _, bsz, t, _ = streams.shape
    hp = HGRN_SHORT_HEADS if t <= HGRN_ROWS else 1
    stream = lambda s: pl.BlockSpec((hp, 1, t, DK_H), lambda b, h: (s * N_HEADS_H // hp + h, b, 0, 0))
    in_specs = [stream(HGRN_BLOCKS.index(w)) for w in (W_HQ, W_HFF, W_HFB, W_HI, W_HG)]
    in_specs += [
        pl.BlockSpec((2, DEPTH + 1, hp * DK_H), lambda b, h: (0, 0, h)),
        pl.BlockSpec((1, DV_H), lambda b, h: (0, 0)),
    ]
    args = [streams] * 5 + [lb_logits, norm_g]
    state_spec = pl.BlockSpec((1, 2, hp, DK_H, DV_H), lambda b, h: (b, 0, h, 0, 0))
    if s0 is not None:
        in_specs.append(state_spec)
        args.append(s0)
    out_specs = [pl.BlockSpec((hp, 1, t, DV_H), lambda b, h: (h, b, 0, 0))]
    out_shape = [jax.ShapeDtypeStruct((N_HEADS_H, bsz, t, DV_H), BF16)]
    if want_state:
        out_specs.append(state_spec)
        out_shape.append(jax.ShapeDtypeStruct((bsz, 2, N_HEADS_H, DK_H, DV_H), F32))
    n_blocks = t // min(HGRN_ROWS, t)
    outs = pl.pallas_call(
        functools.partial(_hgrn_kernel, seq_len=t, has_s0=s0 is not None, want_state=want_state),
        grid=(bsz, N_HEADS_H // hp),
        in_specs=in_specs,
        out_specs=out_specs,
        out_shape=out_shape,
        scratch_shapes=[
            pltpu.VMEM((hp, 2, t, DK_H), F32),
            pltpu.VMEM((hp, 2, t, DK_H), F32),
            pltpu.VMEM((hp, t, DK_H), F32),
            pltpu.VMEM((hp, 2, t, DV_H), F32),
            pltpu.VMEM((hp, 2, DV_H, DK_H), F32),
            pltpu.VMEM((HGRN_C, DK_H), F32),
            pltpu.SMEM((2, n_blocks), jnp.int32),
        ],
        compiler_params=_params(("parallel", "parallel")),
        name="hgrn",
    )(*args)
    return outs if want_state else (outs[0], None)


def _merge_ffn2_kernel(oa_ref, oh_ref, sa_ref, sh_ref, x_ref, mod_ref, wa_ref, wh_ref, wo_ref, win_ref, wout_ref,
                       lng_ref, lnb_ref, y_ref, wa_scr, wh_scr, wo_scr, act_scr):
    @pl.when(pl.program_id(0) == 0)
    def _():
        wa_scr[...] = wa_ref[...].astype(BF16)
        wh_scr[...] = wh_ref[...].astype(BF16)
        wo_scr[...] = wo_ref[...].astype(BF16)

    heads = lambda ref: jnp.concatenate([ref[hh] for hh in range(ref.shape[0])], axis=-1)
    ba = jnp.dot(heads(oa_ref), wa_scr[...], preferred_element_type=F32)
    bh = jnp.dot(heads(oh_ref), wh_scr[...], preferred_element_type=F32)
    merged = sa_ref[...].astype(F32) * ba + sh_ref[...].astype(F32) * bh
    mix = jnp.dot(merged.astype(BF16), wo_scr[...], preferred_element_type=F32)
    x2 = _layer_norm(ALPHA * x_ref[...] + mod_ref[0, 5:6, :] * mix, lng_ref[1:2, :], lnb_ref[1:2, :])
    y_ref[...] = _half_step(x2, mod_ref, win_ref, wout_ref, lng_ref, lnb_ref, act_scr, 2)


def _merge_ffn2(o_a, o_h, gates, x, mod, cond_of_tile, w_a, w_h, w_o, w_in, w_out, ln_g, ln_b):
    m = x.shape[0]
    row = lambda i: (i, 0)
    const = lambda i: (0, 0)
    per_head = lambda i: (0, i, 0)
    resident = pl.Buffered(1)
    return pl.pallas_call(
        _merge_ffn2_kernel,
        grid=(m // FFN_TM,),
        in_specs=[
            pl.BlockSpec((N_HEADS_A, FFN_TM, DV_A), per_head),
            pl.BlockSpec((N_HEADS_H, FFN_TM, DV_H), per_head),
            pl.BlockSpec((FFN_TM, D_MODEL), lambda i: (i, 0)),
            pl.BlockSpec((FFN_TM, D_MODEL), lambda i: (i, 1)),
            pl.BlockSpec((FFN_TM, D_MODEL), row),
            pl.BlockSpec((1, 9, D_MODEL), lambda i: (cond_of_tile(i, FFN_TM), 0, 0)),
            pl.BlockSpec((W_A, D_MODEL), const, pipeline_mode=resident),
            pl.BlockSpec((W_H, D_MODEL), const, pipeline_mode=resident),
            pl.BlockSpec((D_MODEL, D_MODEL), const, pipeline_mode=resident),
            pl.BlockSpec((D_MODEL, 2 * D_FF), const, pipeline_mode=resident),
            pl.BlockSpec((D_FF, D_MODEL), const, pipeline_mode=resident),
            pl.BlockSpec((3, D_MODEL), const),
            pl.BlockSpec((3, D_MODEL), const),
        ],
        out_specs=pl.BlockSpec((FFN_TM, D_MODEL), row),
        out_shape=jax.ShapeDtypeStruct((m, D_MODEL), F32),
        scratch_shapes=[pltpu.VMEM((W_A, D_MODEL), BF16), pltpu.VMEM((W_H, D_MODEL), BF16),
                        pltpu.VMEM((D_MODEL, D_MODEL), BF16), pltpu.VMEM((FFN_TM, D_FF), BF16)],
        compiler_params=_params(("arbitrary",)),
        name="merge_ffn2",
    )(o_a, o_h, gates, gates, x, mod, w_a, w_h, w_o, w_in, w_out, ln_g, ln_b)


def _rope_tables(n_tok):
    rows = n_tok // GRID_W
    row = np.repeat(np.arange(rows, dtype=np.float32), GRID_W)
    col = np.tile(np.arange(GRID_W, dtype=np.float32), rows)
    half = DH_A // 2
    inv = (ROPE_BASE ** (-np.arange(0, half, 2, dtype=np.float32) / half)).astype(np.float32)
    ar = row[:, None] * inv
    ac = col[:, None] * inv
    ang = np.concatenate([ar, ar, ac, ac] * 2, axis=-1)
    cos, sin = np.cos(ang).astype(np.float32), np.sin(ang).astype(np.float32)
    scale = np.float32(Q_SCALE)
    return (jnp.asarray(np.stack([cos * scale, cos, np.ones_like(cos)])),
            jnp.asarray(np.stack([sin * scale, sin, np.zeros_like(sin)])))


def _layer_pass(x, mod, cond_of_tile, wts, lam_init, bsz, ctx):
    m = x.shape[0]
    t = m // bsz
    latent = ctx is not None
    per_head = lambda a: a.reshape(a.shape[0], bsz, t, a.shape[-1])
    x1, h1 = _ffn1(x, mod, cond_of_tile, wts['ffn1_w_in'], wts['ffn1_w_out'], wts['ln_g'], wts['ln_b'])
    w_mix = wts['w_mix_in']
    if latent:
        (qkv,) = _project(h1, w_mix, W_Q, 3, [('head', BF16)], rope_tables=_rope_tables(t), name="proj_qkv")
        k = v = None
        ctx_kv = (ctx[0], ctx[1])
    else:
        qkv, qkv32 = _project(h1, w_mix, W_Q, 3, [('head', BF16), ('stack', F32)], scale_first=True,
                              name="proj_qkv")
        k, v = qkv32[W_K], qkv32[W_V]
        ctx_kv = None
    (streams,) = _project(h1, w_mix, W_HQ, len(HGRN_BLOCKS), [('head', F32)], name="proj_hgrn")
    (gates,) = _project(h1, w_mix, W_GA0, len(GATE_BLOCKS), [('tok', BF16)], sigmoid=True, name="proj_gates")
    qkv = per_head(qkv)
    q_kv = [(qkv, W_Q * N_HEADS_A), (qkv, W_K * N_HEADS_A), (qkv, W_V * N_HEADS_A)]
    o_a = _attention(*q_kv, ctx_kv, wts['lam_params'], wts['attn_subln_g'], lam_init)
    o_h, s_fin = _hgrn(per_head(streams), wts['lb_logits'], wts['hgrn_norm_g'],
                       ctx[2] if latent else None, not latent)
    y = _merge_ffn2(o_a.reshape(N_HEADS_A, m, DV_A), o_h.reshape(N_HEADS_H, m, DV_H), gates, x1, mod,
                    cond_of_tile, wts['w_branch_a'], wts['w_branch_h'], wts['w_mix_out'],
                    wts['ffn2_w_in'], wts['ffn2_w_out'], wts['ln_g'], wts['ln_b'])
    return y, (k, v, s_fin)


def kernel(x_prompt, x_sample, cache_k, cache_v, state_hgrn, c, c_ctx, w_ada, b_ada, ffn1_w_in, ffn1_w_out,
           w_mix_in, lambda_q1, lambda_k1, lambda_q2, lambda_k2, attn_subln_g, hgrn_lb_logits, hgrn_norm_g,
           w_branch_a, w_branch_h, w_mix_out, ffn2_w_in, ffn2_w_out, ln_g, ln_b):
    batch, seq, _ = x_prompt.shape
    dec_batch, dec_seq, _ = x_sample.shape
    past = cache_k.shape[2]
    assert DEPTH == 1 and dec_batch + 1 <= MOD_ROWS
    lam_init = 0.8 - 0.6 * math.exp(-0.3 * 0)

    cond = jnp.concatenate([c_ctx[None, :], c, jnp.zeros((MOD_ROWS - 1 - dec_batch, D_MODEL), F32)], axis=0)
    mod = _modulation(cond, w_ada[0], b_ada[0])

    wts = {
        'ffn1_w_in': _to_bf16(ffn1_w_in[0]), 'ffn1_w_out': _to_bf16(ffn1_w_out[0]), 'w_mix_in': w_mix_in[0],
        'lam_params': jnp.concatenate([lambda_q1, lambda_k1, lambda_q2, lambda_k2], axis=0),
        'attn_subln_g': attn_subln_g, 'lb_logits': hgrn_lb_logits, 'hgrn_norm_g': hgrn_norm_g,
        'w_branch_a': w_branch_a[0], 'w_branch_h': w_branch_h[0], 'w_mix_out': w_mix_out[0],
        'ffn2_w_in': _to_bf16(ffn2_w_in[0]), 'ffn2_w_out': _to_bf16(ffn2_w_out[0]),
        'ln_g': ln_g[0], 'ln_b': ln_b[0],
    }

    y_p, (k_c, v_c, s_c) = _layer_pass(x_prompt.reshape(batch * seq, D_MODEL), mod, lambda i, tm: 0,
                                       wts, lam_init, batch, None)
    ctx = (cache_k[:, 0].reshape(dec_batch, past, QK_A), cache_v[:, 0].reshape(dec_batch, past, W_A),
           state_hgrn[:, 0])
    y_s, _ = _layer_pass(x_sample.reshape(dec_batch * dec_seq, D_MODEL), mod,
                         lambda i, tm: 1 + (i * tm) // dec_seq, wts, lam_init, dec_batch, ctx)

    return (y_p.reshape(batch, seq, D_MODEL),
            y_s.reshape(dec_batch, dec_seq, D_MODEL),
            k_c.reshape(batch, DEPTH, seq, N_HEADS_A, 2, DH_A),
            v_c.reshape(batch, DEPTH, seq, N_HEADS_A, DV_A),
            s_c.reshape(batch, DEPTH, 2, N_HEADS_H, DK_H, DV_H))
```

```python
import functools
import math

import numpy as np
import jax
import jax.numpy as jnp
from jax import lax
from jax.experimental import pallas as pl
from jax.experimental.pallas import tpu as pltpu

D_MODEL = 1024
DEPTH = 1
GRID_W = 64
N_HEADS_A = 4
DH_A = 64
DV_A = 2 * DH_A
QK_A = N_HEADS_A * 2 * DH_A
W_A = N_HEADS_A * DV_A
ROPE_BASE = 10000.0
N_HEADS_H = 4
DK_H = 128
DV_H = 128
QK_H = N_HEADS_H * DK_H
W_H = N_HEADS_H * DV_H
D_FF = 2816
MIX_IN = 2 * QK_A + W_A + 3 * QK_H + 2 * W_H + 2 * D_MODEL
ALPHA = (2 * DEPTH) ** 0.25
LN_EPS = 1e-5
RMS_EPS = 1e-6
Q_SCALE = DH_A ** -0.5 * math.log2(math.e)

F32 = jnp.float32
BF16 = jnp.bfloat16

LANES = 128
VMEM_LIMIT = 56 * 1024 * 1024

FFN_TM = 512
FFN_TF = 256
WIDE_CHUNK_ROWS = 128
SQUARE_CHUNK_ROWS = 256
MIX_TM = 4096
MIX_TN = 512
ATT_TQ = 1024
ATT_SUB = 512
ATT_SHORT_SEQ = 256
HGRN_C = 64
HGRN_ROWS = 512
HGRN_SHORT_HEADS = 4
HGRN_SAFE_DECAY = 75.0
MOD_ROWS = 8

W_Q, W_K, W_V, W_HQ, W_HFF, W_HFB, W_HI, W_HG, W_GA0, W_GA1, W_GH0, W_GH1 = range(12)
HGRN_BLOCKS = (W_HQ, W_HFF, W_HFB, W_HI, W_HG)
GATE_BLOCKS = (W_GA0, W_GA1, W_GH0, W_GH1)


def _params(sem):
    return pltpu.CompilerParams(dimension_semantics=sem, vmem_limit_bytes=VMEM_LIMIT)


def _nt_dot(a, b):
    return lax.dot_general(a, b, (((1,), (1,)), ((), ())), preferred_element_type=F32)


def _tn_dot(a, b):
    return lax.dot_general(a, b, (((0,), (0,)), ((), ())), preferred_element_type=F32)


def _layer_norm(y, g, b):
    mu = jnp.mean(y, axis=-1, keepdims=True)
    yc = y - mu
    var = jnp.mean(yc * yc, axis=-1, keepdims=True)
    return yc * lax.rsqrt(var + LN_EPS) * g + b


def _mod_kernel(c_ref, w_ref, b_ref, o_ref):
    c = c_ref[...]
    a = (c * jax.nn.sigmoid(c)).astype(BF16)
    o_ref[...] = jnp.dot(a, w_ref[...].astype(BF16), preferred_element_type=F32) + b_ref[...]


def _modulation(cond, w_ada, b_ada):
    n_out = w_ada.shape[1]
    tn = D_MODEL
    out = pl.pallas_call(
        _mod_kernel,
        grid=(n_out // tn,),
        in_specs=[
            pl.BlockSpec((MOD_ROWS, D_MODEL), lambda n: (0, 0)),
            pl.BlockSpec((D_MODEL, tn), lambda n: (0, n)),
            pl.BlockSpec((1, tn), lambda n: (0, n)),
        ],
        out_specs=pl.BlockSpec((MOD_ROWS, tn), lambda n: (0, n)),
        out_shape=jax.ShapeDtypeStruct((MOD_ROWS, n_out), F32),
        compiler_params=_params(("arbitrary",)),
        name="modulation",
    )(cond, w_ada, b_ada.reshape(1, n_out))
    return out.reshape(MOD_ROWS, 9, D_MODEL)


def _load_weight(w_hbm, w_scr, stage, sems):
    chunk = stage.shape[1]
    n_chunks = w_hbm.shape[0] // chunk

    def copy(c):
        return pltpu.make_async_copy(w_hbm.at[pl.ds(c * chunk, chunk), :], stage.at[c % 2], sems.at[c % 2])

    copy(0).start()
    for c in range(n_chunks):
        if c + 1 < n_chunks:
            copy(c + 1).start()
        copy(c).wait()
        w_scr[c * chunk:(c + 1) * chunk, :] = stage[c % 2].astype(BF16)


def _modulate(x, mod_ref, sub):
    return x * (1.0 + mod_ref[0, 3 * sub + 1:3 * sub + 2, :]) + mod_ref[0, 3 * sub:3 * sub + 1, :]


def _half_step(x, mod_ref, win_ref, wout_ref, lng_ref, lnb_ref, act_scr, sub):
    h = _modulate(x, mod_ref, sub).astype(BF16)
    for c in range(D_FF // FFN_TF):
        cols = slice(c * FFN_TF, (c + 1) * FFN_TF)
        up_cols = slice(D_FF + c * FFN_TF, D_FF + (c + 1) * FFN_TF)
        gate = jnp.dot(h, win_ref[:, cols], preferred_element_type=F32)
        up = jnp.dot(h, win_ref[:, up_cols], preferred_element_type=F32)
        act_scr[:, cols] = (gate * jax.nn.sigmoid(gate) * up).astype(BF16)
    f = jnp.dot(act_scr[...], wout_ref[...], preferred_element_type=F32)
    g = mod_ref[0, 3 * sub + 2:3 * sub + 3, :]
    return _layer_norm(ALPHA * x + 0.5 * g * f, lng_ref[sub:sub + 1, :], lnb_ref[sub:sub + 1, :])


def _ffn_weight_scratch():
    return [
        pltpu.VMEM((D_MODEL, 2 * D_FF), BF16),
        pltpu.VMEM((D_FF, D_MODEL), BF16),
        pltpu.VMEM((2, WIDE_CHUNK_ROWS, 2 * D_FF), F32),
        pltpu.VMEM((2, SQUARE_CHUNK_ROWS, D_MODEL), F32),
        pltpu.SemaphoreType.DMA((2,)),
        pltpu.SemaphoreType.DMA((2,)),
    ]


def _ffn1_kernel(xa_ref, xb_ref, mod_ref, win_hbm, wout_hbm, lng_ref, lnb_ref, xo_ref, ho_ref,
                 win_scr, wout_scr, wide_stage, square_stage, wide_sems, square_sems, act_scr, *, tiles_a):
    i = pl.program_id(0)

    @pl.when(i == 0)
    def _():
        _load_weight(win_hbm, win_scr, wide_stage, wide_sems)
        _load_weight(wout_hbm, wout_scr, square_stage, square_sems)

    x = jnp.where(i < tiles_a, xa_ref[...], xb_ref[...])
    xn = _half_step(x, mod_ref, win_scr, wout_scr, lng_ref, lnb_ref, act_scr, 0)
    xo_ref[...] = xn
    ho_ref[...] = _modulate(xn, mod_ref, 1).astype(BF16)


def _ffn1(xa, xb, mod, cond_of_tile, w_in, w_out, ln_g, ln_b):
    tiles_a, tiles_b = xa.shape[0] // FFN_TM, xb.shape[0] // FFN_TM
    m = xa.shape[0] + xb.shape[0]
    row = lambda i: (i, 0)
    const = lambda i: (0, 0)
    hbm = pl.BlockSpec(memory_space=pl.ANY)
    return pl.pallas_call(
        functools.partial(_ffn1_kernel, tiles_a=tiles_a),
        grid=(tiles_a + tiles_b,),
        in_specs=[
            pl.BlockSpec((FFN_TM, D_MODEL), lambda i: (jnp.minimum(i, tiles_a - 1), 0)),
            pl.BlockSpec((FFN_TM, D_MODEL), lambda i: (jnp.maximum(i - tiles_a, 0), 0)),
            pl.BlockSpec((1, 9, D_MODEL), lambda i: (cond_of_tile(i, FFN_TM), 0, 0)),
            hbm, hbm,
            pl.BlockSpec((3, D_MODEL), const),
            pl.BlockSpec((3, D_MODEL), const),
        ],
        out_specs=[pl.BlockSpec((FFN_TM, D_MODEL), row), pl.BlockSpec((FFN_TM, D_MODEL), row)],
        out_shape=[jax.ShapeDtypeStruct((m, D_MODEL), F32), jax.ShapeDtypeStruct((m, D_MODEL), BF16)],
        scratch_shapes=_ffn_weight_scratch() + [pltpu.VMEM((FFN_TM, D_FF), BF16)],
        compiler_params=_params(("arbitrary",)),
        name="ffn1",
    )(xa, xb, mod, w_in, w_out, ln_g, ln_b)


def _rope(x, cos, sin):
    lane = lax.broadcasted_iota(jnp.int32, (1, LANES), 1)
    first_half = (lane % 32) < 16
    outs = []
    for hb in range(QK_A // LANES):
        xb = x[:, hb * LANES:(hb + 1) * LANES]
        ahead = pltpu.roll(xb, LANES - 16, 1)
        behind = pltpu.roll(xb, 16, 1)
        rot = jnp.where(first_half, -ahead, behind)
        outs.append(xb * cos + rot * sin)
    return jnp.concatenate(outs, axis=-1)


def _proj_kernel(*refs, rope, sigmoid, scale_first, outs):
    if rope:
        h_ref, w_ref, cos_ref, sin_ref = refs[:4]
        refs = refs[4:]
    else:
        h_ref, w_ref = refs[:2]
        refs = refs[2:]
    out_refs, w_scr = refs[:-1], refs[-1]

    @pl.when(pl.program_id(1) == 0)
    def _():
        w_scr[...] = w_ref[...].astype(BF16)

    y = jnp.dot(h_ref[...], w_scr[...], preferred_element_type=F32)
    if rope:
        t_rope = cos_ref.shape[1]
        y = jnp.concatenate([_rope(y[r * t_rope:(r + 1) * t_rope], cos_ref[0], sin_ref[0])
                             for r in range(y.shape[0] // t_rope)], axis=0)
    if scale_first:
        y = y * jnp.where(pl.program_id(0) == 0, Q_SCALE, 1.0)
    if sigmoid:
        y = jax.nn.sigmoid(y)
    for ref, kind in zip(out_refs, outs):
        if kind == 'head':
            for hh in range(MIX_TN // LANES):
                ref[hh] = y[:, hh * LANES:(hh + 1) * LANES].astype(ref.dtype)
        elif kind == 'stack':
            ref[0] = y.astype(ref.dtype)
        else:
            ref[...] = y.astype(ref.dtype)


def _project(h, group, w, first_block, n_blocks, outs, rope_tables=None, sigmoid=False, scale_first=False,
             name="proj"):
    m = MIX_TM
    rope = rope_tables is not None
    heads = MIX_TN // LANES
    in_specs = [pl.BlockSpec((MIX_TM, D_MODEL), lambda n, i: (group + i, 0), pipeline_mode=pl.Buffered(1)),
                pl.BlockSpec((D_MODEL, MIX_TN), lambda n, i: (0, first_block + n))]
    args = [h, w]
    if rope:
        t_rope = rope_tables[0].shape[1]
        rope_rows = min(t_rope, MIX_TM)
        n_rope_tiles = t_rope // rope_rows
        for t in rope_tables:
            in_specs.append(pl.BlockSpec((1, rope_rows, DV_A), lambda n, i: (n, i % n_rope_tiles, 0)))
            args.append(t)
    specs = {
        'head': (pl.BlockSpec((heads, MIX_TM, LANES), lambda n, i: (n, i, 0)), (n_blocks * heads, m, LANES)),
        'tok': (pl.BlockSpec((MIX_TM, MIX_TN), lambda n, i: (i, n)), (m, n_blocks * MIX_TN)),
        'stack': (pl.BlockSpec((1, MIX_TM, MIX_TN), lambda n, i: (n, i, 0)), (n_blocks, m, MIX_TN)),
    }
    return pl.pallas_call(
        functools.partial(_proj_kernel, rope=rope, sigmoid=sigmoid, scale_first=scale_first,
                          outs=tuple(kind for kind, _ in outs)),
        grid=(n_blocks, m // MIX_TM),
        in_specs=in_specs,
        out_specs=[specs[kind][0] for kind, _ in outs],
        out_shape=[jax.ShapeDtypeStruct(specs[kind][1], dt) for kind, dt in outs],
        scratch_shapes=[pltpu.VMEM((D_MODEL, MIX_TN), BF16)],
        compiler_params=_params(("arbitrary", "arbitrary")),
        name=name,
    )(*args)


def _attn_kernel(*refs, lam_init, has_ctx):
    if has_ctx:
        lam_ref, g_ref, q_ref, k_ref, v_ref, kc_ref, vc_ref, o_ref = refs
    else:
        lam_ref, g_ref, q_ref, k_ref, v_ref, o_ref = refs
    lp = lam_ref[...]
    lam = (jnp.exp(jnp.sum(lp[0:1] * lp[1:2], axis=-1, keepdims=True))
           - jnp.exp(jnp.sum(lp[2:3] * lp[3:4], axis=-1, keepdims=True)) + lam_init)

    lane = lax.broadcasted_iota(jnp.int32, (1, DV_A), 1)
    comp0 = lane < DH_A
    tq = q_ref.shape[2]
    sub = min(ATT_SUB, tq)
    problems = []
    for hh in range(q_ref.shape[0]):
        keys = [k_ref[hh, 0]]
        vals = [v_ref[hh, 0]]
        if has_ctx:
            keys.append(kc_ref[0, :, hh * DV_A:(hh + 1) * DV_A].astype(BF16))
            vals.append(vc_ref[0, :, hh * DV_A:(hh + 1) * DV_A].astype(BF16))
        for r in range(tq // sub):
            q = q_ref[hh, 0, r * sub:(r + 1) * sub, :]
            zero = jnp.zeros_like(q)
            for qc in (jnp.where(comp0, q, zero), jnp.where(comp0, zero, q)):
                problems.append((qc, keys, vals))
    scores = [[_nt_dot(qc, kk) for kk in keys] for qc, keys, _ in problems]
    weights = []
    for ss in scores:
        mx = functools.reduce(jnp.maximum, [jnp.max(s, axis=-1, keepdims=True) for s in ss])
        es = [jnp.exp2(s - mx) for s in ss]
        den = functools.reduce(jnp.add, [jnp.sum(e, axis=-1, keepdims=True) for e in es])
        weights.append(([e.astype(BF16) for e in es], 1.0 / den))
    maps = [functools.reduce(jnp.add, [jnp.dot(e, vv, preferred_element_type=F32) for e, vv in zip(es, vals)]) * r
            for (es, r), (_, _, vals) in zip(weights, problems)]
    idx = 0
    for hh in range(q_ref.shape[0]):
        for r in range(tq // sub):
            o = maps[idx] - lam * maps[idx + 1]
            idx += 2
            o = o * lax.rsqrt(jnp.mean(o * o, axis=-1, keepdims=True) + RMS_EPS) * g_ref[...]
            o_ref[hh, 0, r * sub:(r + 1) * sub, :] = (o * (1.0 - lam_init)).astype(o_ref.dtype)


def _attention(q, k, v, ctx_kv, lam_params, subln_g, lam_init):
    (q_arr, q_row), (k_arr, k_row), (v_arr, v_row) = q, k, v
    _, bsz, t, _ = q_arr.shape
    tq = min(ATT_TQ, t)
    hp = N_HEADS_A if t <= ATT_SHORT_SEQ else 1
    const = lambda b, h, i: (0, 0)
    rows = lambda first: (lambda b, h, i: (first // hp + h, b, 0, 0))
    in_specs = [
        pl.BlockSpec((4, DH_A), const),
        pl.BlockSpec((1, DV_A), const),
        pl.BlockSpec((hp, 1, tq, DV_A), lambda b, h, i: (q_row // hp + h, b, i, 0)),
        pl.BlockSpec((hp, 1, t, DV_A), rows(k_row)),
        pl.BlockSpec((hp, 1, t, DV_A), rows(v_row)),
    ]
    args = [lam_params, subln_g, q_arr, k_arr, v_arr]
    if ctx_kv is not None:
        p = ctx_kv[0].shape[1]
        head_all = lambda b, h, i: (b, 0, h)
        in_specs += [pl.BlockSpec((1, p, hp * DV_A), head_all), pl.BlockSpec((1, p, hp * DV_A), head_all)]
        args += list(ctx_kv)
    return pl.pallas_call(
        functools.partial(_attn_kernel, lam_init=lam_init, has_ctx=ctx_kv is not None),
        grid=(bsz, N_HEADS_A // hp, t // tq),
        in_specs=in_specs,
        out_specs=pl.BlockSpec((hp, 1, tq, DV_A), lambda b, h, i: (h, b, i, 0)),
        out_shape=jax.ShapeDtypeStruct((N_HEADS_A, bsz, t, DV_A), BF16),
        compiler_params=_params(("parallel", "parallel", "arbitrary")),
        name="attention",
    )(*args)


def _chunk_scan(g, reverse):
    r = g.shape[0]
    pos = lax.broadcasted_iota(jnp.int32, (r, 1), 0) % HGRN_C
    b = g
    sh = 1
    while sh < HGRN_C:
        if reverse:
            b = b + jnp.where(pos < HGRN_C - sh, pltpu.roll(b, r - sh, 0), 0.0)
        else:
            b = b + jnp.where(pos >= sh, pltpu.roll(b, sh, 0), 0.0)
        sh *= 2
    return b


def _hgrn_kernel(*refs, seq_len, has_s0, want_state):
    refs = list(refs)
    hq_ref, hff_ref, hfb_ref, hi_ref, hg_ref, lbl_ref, ng_ref = refs[:7]
    refs = refs[7:]
    s0_ref = refs.pop(0) if has_s0 else None
    o_ref = refs.pop(0)
    sfin_ref = refs.pop(0) if want_state else None
    g_scr, kd_scr, q_scr, od_scr, st_scr, b_scr, ok_scr = refs
    hp = hq_ref.shape[0]
    c = HGRN_C
    rb = min(HGRN_ROWS, seq_len)
    nb = seq_len // rb
    cpb = rb // c
    z_refs = (hff_ref, hfb_ref)

    def lower_bound(hh, d):
        logits = lbl_ref[d, :, hh * DK_H:(hh + 1) * DK_H]
        mx = jnp.max(logits, axis=0, keepdims=True)
        e = jnp.exp(logits - mx)
        return e[0:1] / jnp.sum(e, axis=0, keepdims=True)

    lbs = [[lower_bound(hh, d) for d in range(2)] for hh in range(hp)]

    def gates(i, carry):
        rows = pl.ds(pl.multiple_of(i * rb, rb), rb)
        for d in range(2):
            worst = jnp.zeros((1, DK_H), F32)
            for hh in range(hp):
                if d == 0:
                    hq = hq_ref[hh, 0, rows, :]
                    q_scr[hh, rows, :] = hq * jax.nn.sigmoid(hq)
                z = z_refs[d][hh, 0, rows, :]
                lb = lbs[hh][d]
                e = jnp.exp(-jnp.abs(z))
                r = 1.0 / (1.0 + e)
                er = e * r
                pos = z >= 0.0
                g = jnp.log(lb + (1.0 - lb) * jnp.where(pos, r, er))
                g_scr[hh, d, rows, :] = g
                kd_scr[hh, d, rows, :] = (1.0 - lb) * jnp.where(pos, er, r)
                for j in range(cpb):
                    worst = jnp.maximum(worst, -jnp.sum(g[j * c:(j + 1) * c], axis=0, keepdims=True))
            ok_scr[d, i] = (jnp.max(worst) <= HGRN_SAFE_DECAY).astype(jnp.int32)
        return carry

    lax.fori_loop(0, nb, gates, 0)

    for hh in range(hp):
        for d in range(2):
            if has_s0:
                st_scr[hh, d] = s0_ref[0, d, hh].T
            else:
                st_scr[hh, d] = jnp.zeros((DV_H, DK_H), F32)

    row = lax.broadcasted_iota(jnp.int32, (c, 1), 0)
    col = lax.broadcasted_iota(jnp.int32, (1, c), 1)

    def fast_block(d, r0):
        reverse = d == 1
        rows = pl.ds(r0, rb)
        n = hp * cpb
        flat = lambda a: a.reshape(hp * rb, a.shape[-1])
        chunks = lambda a: a.reshape(n, c, a.shape[-1])
        vb = chunks(hi_ref[:, 0, rows, :].astype(BF16))
        b = _chunk_scan(flat(g_scr[:, d, rows, :]), reverse)
        qd = chunks((flat(q_scr[:, rows, :]) * jnp.exp(b)).astype(BF16))
        ku = chunks(flat(kd_scr[:, d, rows, :]) * jnp.exp(-b))
        b = chunks(b)
        e_end = jnp.exp(b[:, 0:1, :] if reverse else b[:, c - 1:c, :])
        causal = ((row <= col) if reverse else (row >= col))[None]
        a = jnp.einsum('nck,nsk->ncs', qd, ku.astype(BF16), preferred_element_type=F32)
        o = jnp.einsum('ncs,nsv->ncv', jnp.where(causal, a, 0.0).astype(BF16), vb, preferred_element_type=F32)
        u = jnp.einsum('nsv,nsk->nvk', vb, (ku * e_end).astype(BF16), preferred_element_type=F32)
        entering = [None] * n
        for hh in range(hp):
            st = st_scr[hh, d]
            for j in (range(cpb - 1, -1, -1) if reverse else range(cpb)):
                entering[hh * cpb + j] = st.astype(BF16)
                st = e_end[hh * cpb + j] * st + u[hh * cpb + j]
            st_scr[hh, d] = st
        o = o + jnp.einsum('nck,nvk->ncv', qd, jnp.stack(entering), preferred_element_type=F32)
        od_scr[:, d, rows, :] = o.reshape(hp, rb, DV_H)

    def exact_chunk(hh, d, r0):
        reverse = d == 1
        rows = pl.ds(r0, c)
        b = _chunk_scan(g_scr[hh, d, rows, :], reverse)
        b_scr[...] = b

        def one_row(t, carry):
            bt = b_scr[pl.ds(t, 1), :]
            qt = q_scr[hh, pl.ds(r0 + t, 1), :]
            seen = (row >= t) if reverse else (row <= t)
            w = jnp.exp(jnp.where(seen, bt - b_scr[...], -jnp.inf))
            p = jnp.sum(qt * kd_scr[hh, d, rows, :] * w, axis=-1, keepdims=True)
            od_scr[hh, d, pl.ds(r0 + t, 1), :] = jnp.sum(p * hi_ref[hh, 0, rows, :], axis=0, keepdims=True)
            return carry

        lax.fori_loop(0, c, one_row, 0)
        b_end = b[0:1] if reverse else b[c - 1:c]
        st = st_scr[hh, d]
        qd = (q_scr[hh, rows, :] * jnp.exp(b)).astype(BF16)
        od_scr[hh, d, rows, :] = od_scr[hh, d, rows, :] + _nt_dot(qd, st.astype(BF16))
        k_end = (kd_scr[hh, d, rows, :] * jnp.exp(b_end - b)).astype(BF16)
        st_scr[hh, d] = jnp.exp(b_end) * st + _tn_dot(hi_ref[hh, 0, rows, :].astype(BF16), k_end)

    def block(i, carry):
        starts = (pl.multiple_of(i * rb, rb), pl.multiple_of((nb - 1 - i) * rb, rb))
        mild = (ok_scr[0, i] + ok_scr[1, nb - 1 - i]) == 2

        @pl.when(mild)
        def _():
            for d in range(2):
                fast_block(d, starts[d])

        @pl.when(jnp.logical_not(mild))
        def _():
            for hh in range(hp):
                for d in range(2):
                    for j in (range(cpb - 1, -1, -1) if d == 1 else range(cpb)):
                        exact_chunk(hh, d, pl.multiple_of(starts[d] + j * c, c))

        return carry

    lax.fori_loop(0, nb, block, 0)

    def finish(i, carry):
        rows = pl.ds(pl.multiple_of(i * rb, rb), rb)
        for hh in range(hp):
            o = od_scr[hh, 0, rows, :] + od_scr[hh, 1, rows, :]
            o = o * lax.rsqrt(jnp.mean(o * o, axis=-1, keepdims=True) + RMS_EPS) * ng_ref[...]
            hg = hg_ref[hh, 0, rows, :]
            o_ref[hh, 0, rows, :] = (o * (hg * jax.nn.sigmoid(hg))).astype(o_ref.dtype)
        return carry

    lax.fori_loop(0, nb, finish, 0)
    if want_state:
        for hh in range(hp):
            for d in range(2):
                sfin_ref[0, d, hh] = st_scr[hh, d].T


def _hgrn(streams, lb_logits, norm_g, s0, want_state):
    _, bsz, t, _ = streams.shape
    hp = HGRN_SHORT_HEADS if t <= HGRN_ROWS else 1
    stream = lambda s: pl.BlockSpec((hp, 1, t, DK_H), lambda b, h: (s * N_HEADS_H // hp + h, b, 0, 0))
    in_specs = [stream(HGRN_BLOCKS.index(w)) for w in (W_HQ, W_HFF, W_HFB, W_HI, W_HG)]
    in_specs += [
        pl.BlockSpec((2, DEPTH + 1, hp * DK_H), lambda b, h: (0, 0, h)),
        pl.BlockSpec((1, DV_H), lambda b, h: (0, 0)),
    ]
    args = [streams] * 5 + [lb_logits, norm_g]
    state_spec = pl.BlockSpec((1, 2, hp, DK_H, DV_H), lambda b, h: (b, 0, h, 0, 0))
    if s0 is not None:
        in_specs.append(state_spec)
        args.append(s0)
    out_specs = [pl.BlockSpec((hp, 1, t, DV_H), lambda b, h: (h, b, 0, 0))]
    out_shape = [jax.ShapeDtypeStruct((N_HEADS_H, bsz, t, DV_H), BF16)]
    if want_state:
        out_specs.append(state_spec)
        out_shape.append(jax.ShapeDtypeStruct((bsz, 2, N_HEADS_H, DK_H, DV_H), F32))
    n_blocks = t // min(HGRN_ROWS, t)
    outs = pl.pallas_call(
        functools.partial(_hgrn_kernel, seq_len=t, has_s0=s0 is not None, want_state=want_state),
        grid=(bsz, N_HEADS_H // hp),
        in_specs=in_specs,
        out_specs=out_specs,
        out_shape=out_shape,
        scratch_shapes=[
            pltpu.VMEM((hp, 2, t, DK_H), F32),
            pltpu.VMEM((hp, 2, t, DK_H), F32),
            pltpu.VMEM((hp, t, DK_H), F32),
            pltpu.VMEM((hp, 2, t, DV_H), F32),
            pltpu.VMEM((hp, 2, DV_H, DK_H), F32),
            pltpu.VMEM((HGRN_C, DK_H), F32),
            pltpu.SMEM((2, n_blocks), jnp.int32),
        ],
        compiler_params=_params(("parallel", "parallel")),
        name="hgrn",
    )(*args)
    return outs if want_state else (outs[0], None)


def _merge_ffn2_kernel(oa_ref, oh_ref, sa_ref, sh_ref, x_ref, mod_ref, wa_hbm, wh_hbm, wo_hbm, win_hbm, wout_hbm,
                       lng_ref, lnb_ref, y_ref, wa_scr, wh_scr, wo_scr,
                       win_scr, wout_scr, wide_stage, square_stage, wide_sems, square_sems, act_scr):
    @pl.when(pl.program_id(0) == 0)
    def _():
        _load_weight(win_hbm, win_scr, wide_stage, wide_sems)
        for w_hbm, w_scr in ((wa_hbm, wa_scr), (wh_hbm, wh_scr), (wo_hbm, wo_scr), (wout_hbm, wout_scr)):
            _load_weight(w_hbm, w_scr, square_stage, square_sems)

    heads = lambda ref: jnp.concatenate([ref[hh] for hh in range(ref.shape[0])], axis=-1)
    ba = jnp.dot(heads(oa_ref), wa_scr[...], preferred_element_type=F32)
    bh = jnp.dot(heads(oh_ref), wh_scr[...], preferred_element_type=F32)
    merged = sa_ref[...].astype(F32) * ba + sh_ref[...].astype(F32) * bh
    mix = jnp.dot(merged.astype(BF16), wo_scr[...], preferred_element_type=F32)
    x2 = _layer_norm(ALPHA * x_ref[...] + mod_ref[0, 5:6, :] * mix, lng_ref[1:2, :], lnb_ref[1:2, :])
    y_ref[...] = _half_step(x2, mod_ref, win_scr, wout_scr, lng_ref, lnb_ref, act_scr, 2)


def _merge_ffn2(o_a, o_h, gates, x, first_tile, mod, cond_of_tile, w_a, w_h, w_o, w_in, w_out, ln_g, ln_b):
    m = gates.shape[0]
    row = lambda i: (i, 0)
    const = lambda i: (0, 0)
    per_head = lambda i: (0, i, 0)
    hbm = pl.BlockSpec(memory_space=pl.ANY)
    return pl.pallas_call(
        _merge_ffn2_kernel,
        grid=(m // FFN_TM,),
        in_specs=[
            pl.BlockSpec((N_HEADS_A, FFN_TM, DV_A), per_head),
            pl.BlockSpec((N_HEADS_H, FFN_TM, DV_H), per_head),
            pl.BlockSpec((FFN_TM, D_MODEL), lambda i: (i, 0)),
            pl.BlockSpec((FFN_TM, D_MODEL), lambda i: (i, 1)),
            pl.BlockSpec((FFN_TM, D_MODEL), lambda i: (first_tile + i, 0)),
            pl.BlockSpec((1, 9, D_MODEL), lambda i: (cond_of_tile(first_tile + i, FFN_TM), 0, 0)),
            hbm, hbm, hbm, hbm, hbm,
            pl.BlockSpec((3, D_MODEL), const),
            pl.BlockSpec((3, D_MODEL), const),
        ],
        out_specs=pl.BlockSpec((FFN_TM, D_MODEL), row),
        out_shape=jax.ShapeDtypeStruct((m, D_MODEL), F32),
        scratch_shapes=[pltpu.VMEM((W_A, D_MODEL), BF16), pltpu.VMEM((W_H, D_MODEL), BF16),
                        pltpu.VMEM((D_MODEL, D_MODEL), BF16)] + _ffn_weight_scratch()
                       + [pltpu.VMEM((FFN_TM, D_FF), BF16)],
        compiler_params=_params(("arbitrary",)),
        name="merge_ffn2",
    )(o_a, o_h, gates, gates, x, mod, w_a, w_h, w_o, w_in, w_out, ln_g, ln_b)


def _rope_tables(n_tok):
    rows = n_tok // GRID_W
    row = np.repeat(np.arange(rows, dtype=np.float32), GRID_W)
    col = np.tile(np.arange(GRID_W, dtype=np.float32), rows)
    half = DH_A // 2
    inv = (ROPE_BASE ** (-np.arange(0, half, 2, dtype=np.float32) / half)).astype(np.float32)
    ar = row[:, None] * inv
    ac = col[:, None] * inv
    ang = np.concatenate([ar, ar, ac, ac] * 2, axis=-1)
    cos, sin = np.cos(ang).astype(np.float32), np.sin(ang).astype(np.float32)
    scale = np.float32(Q_SCALE)
    return (jnp.asarray(np.stack([cos * scale, cos, np.ones_like(cos)])),
            jnp.asarray(np.stack([sin * scale, sin, np.zeros_like(sin)])))


def _mixer_and_ffn2(x1, h1, group, mod, cond_of_tile, wts, lam_init, bsz, ctx):
    m = MIX_TM
    t = m // bsz
    latent = ctx is not None
    per_head = lambda a: a.reshape(a.shape[0], bsz, t, a.shape[-1])
    w_mix = wts['w_mix_in']
    if latent:
        (qkv,) = _project(h1, group, w_mix, W_Q, 3, [('head', BF16)], rope_tables=_rope_tables(t), name="proj_qkv")
        k = v = None
        ctx_kv = (ctx[0], ctx[1])
    else:
        qkv, qkv32 = _project(h1, group, w_mix, W_Q, 3, [('head', BF16), ('stack', F32)], scale_first=True,
                              name="proj_qkv")
        k, v = qkv32[W_K], qkv32[W_V]
        ctx_kv = None
    (streams,) = _project(h1, group, w_mix, W_HQ, len(HGRN_BLOCKS), [('head', F32)], name="proj_hgrn")
    (gates,) = _project(h1, group, w_mix, W_GA0, len(GATE_BLOCKS), [('tok', BF16)], sigmoid=True,
                        name="proj_gates")
    qkv = per_head(qkv)
    q_kv = [(qkv, W_Q * N_HEADS_A), (qkv, W_K * N_HEADS_A), (qkv, W_V * N_HEADS_A)]
    o_a = _attention(*q_kv, ctx_kv, wts['lam_params'], wts['attn_subln_g'], lam_init)
    o_h, s_fin = _hgrn(per_head(streams), wts['lb_logits'], wts['hgrn_norm_g'],
                       ctx[2] if latent else None, not latent)
    y = _merge_ffn2(o_a.reshape(N_HEADS_A, m, DV_A), o_h.reshape(N_HEADS_H, m, DV_H), gates,
                    x1, group * (m // FFN_TM), mod, cond_of_tile,
                    wts['w_branch_a'], wts['w_branch_h'], wts['w_mix_out'],
                    wts['ffn2_w_in'], wts['ffn2_w_out'], wts['ln_g'], wts['ln_b'])
    return y, (k, v, s_fin)


def kernel(x_prompt, x_sample, cache_k, cache_v, state_hgrn, c, c_ctx, w_ada, b_ada, ffn1_w_in, ffn1_w_out,
           w_mix_in, lambda_q1, lambda_k1, lambda_q2, lambda_k2, attn_subln_g, hgrn_lb_logits, hgrn_norm_g,
           w_branch_a, w_branch_h, w_mix_out, ffn2_w_in, ffn2_w_out, ln_g, ln_b):
    batch, seq, _ = x_prompt.shape
    dec_batch, dec_seq, _ = x_sample.shape
    past = cache_k.shape[2]
    assert DEPTH == 1 and dec_batch + 1 <= MOD_ROWS
    lam_init = 0.8 - 0.6 * math.exp(-0.3 * 0)

    cond = jnp.concatenate([c_ctx[None, :], c, jnp.zeros((MOD_ROWS - 1 - dec_batch, D_MODEL), F32)], axis=0)
    mod = _modulation(cond, w_ada[0], b_ada[0])

    wts = {
        'w_mix_in': w_mix_in[0],
        'lam_params': jnp.concatenate([lambda_q1, lambda_k1, lambda_q2, lambda_k2], axis=0),
        'attn_subln_g': attn_subln_g, 'lb_logits': hgrn_lb_logits, 'hgrn_norm_g': hgrn_norm_g,
        'w_branch_a': w_branch_a[0], 'w_branch_h': w_branch_h[0], 'w_mix_out': w_mix_out[0],
        'ffn2_w_in': ffn2_w_in[0], 'ffn2_w_out': ffn2_w_out[0],
        'ln_g': ln_g[0], 'ln_b': ln_b[0],
    }
    n_ctx = batch * seq
    assert n_ctx == MIX_TM and dec_batch * dec_seq == MIX_TM

    def cond_of_tile(i, tm):
        first = i * tm
        return jnp.where(first < n_ctx, 0, 1 + (first - n_ctx) // dec_seq)

    x1, h1 = _ffn1(x_prompt.reshape(n_ctx, D_MODEL), x_sample.reshape(dec_batch * dec_seq, D_MODEL), mod,
                   cond_of_tile, ffn1_w_in[0], ffn1_w_out[0], ln_g[0], ln_b[0])
    y_p, (k_c, v_c, s_c) = _mixer_and_ffn2(x1, h1, 0, mod, cond_of_tile, wts, lam_init, batch, None)
    ctx = (cache_k[:, 0].reshape(dec_batch, past, QK_A), cache_v[:, 0].reshape(dec_batch, past, W_A),
           state_hgrn[:, 0])
    y_s, _ = _mixer_and_ffn2(x1, h1, 1, mod, cond_of_tile, wts, lam_init, dec_batch, ctx)

    return (y_p.reshape(batch, seq, D_MODEL),
            y_s.reshape(dec_batch, dec_seq, D_MODEL),
            k_c.reshape(batch, DEPTH, seq, N_HEADS_A, 2, DH_A),
            v_c.reshape(batch, DEPTH, seq, N_HEADS_A, DV_A),
            s_c.reshape(batch, DEPTH, 2, N_HEADS_H, DK_H, DV_H))
```

```python
import functools
import math

import numpy as np
import jax
import jax.numpy as jnp
from jax import lax
from jax.experimental import pallas as pl
from jax.experimental.pallas import tpu as pltpu

D_MODEL = 1024
DEPTH = 1
GRID_W = 64
N_HEADS_A = 4
DH_A = 64
DV_A = 2 * DH_A
QK_A = N_HEADS_A * 2 * DH_A
W_A = N_HEADS_A * DV_A
ROPE_BASE = 10000.0
N_HEADS_H = 4
DK_H = 128
DV_H = 128
QK_H = N_HEADS_H * DK_H
W_H = N_HEADS_H * DV_H
D_FF = 2816
MIX_IN = 2 * QK_A + W_A + 3 * QK_H + 2 * W_H + 2 * D_MODEL
ALPHA = (2 * DEPTH) ** 0.25
LN_EPS = 1e-5
RMS_EPS = 1e-6
Q_SCALE = DH_A ** -0.5 * math.log2(math.e)

F32 = jnp.float32
BF16 = jnp.bfloat16

LANES = 128
VMEM_LIMIT = 56 * 1024 * 1024

FFN_TM = 512
FFN_TF = 256
WIDE_CHUNK_ROWS = 128
SQUARE_CHUNK_ROWS = 256
MIX_TM = 4096
PROJ_TM = 2048
MIX_TN = 512
ATT_TQ = 1024
ATT_SUB = 512
ATT_SHORT_SEQ = 256
HGRN_C = 64
HGRN_ROWS = 512
HGRN_SHORT_HEADS = 4
HGRN_SAFE_DECAY = 75.0
MOD_ROWS = 8

W_Q, W_K, W_V, W_HQ, W_HFF, W_HFB, W_HI, W_HG, W_GA0, W_GA1, W_GH0, W_GH1 = range(12)
HGRN_BLOCKS = (W_HQ, W_HFF, W_HFB, W_HI, W_HG)
GATE_BLOCKS = (W_GA0, W_GA1, W_GH0, W_GH1)


def _params(sem):
    return pltpu.CompilerParams(dimension_semantics=sem, vmem_limit_bytes=VMEM_LIMIT)


def _nt_dot(a, b):
    return lax.dot_general(a, b, (((1,), (1,)), ((), ())), preferred_element_type=F32)


def _tn_dot(a, b):
    return lax.dot_general(a, b, (((0,), (0,)), ((), ())), preferred_element_type=F32)


def _layer_norm(y, g, b):
    mu = jnp.mean(y, axis=-1, keepdims=True)
    yc = y - mu
    var = jnp.mean(yc * yc, axis=-1, keepdims=True)
    return yc * lax.rsqrt(var + LN_EPS) * g + b


def _mod_kernel(c_ref, w_ref, b_ref, o_ref):
    c = c_ref[...]
    a = (c * jax.nn.sigmoid(c)).astype(BF16)
    o_ref[...] = jnp.dot(a, w_ref[...].astype(BF16), preferred_element_type=F32) + b_ref[...]


def _modulation(cond, w_ada, b_ada):
    n_out = w_ada.shape[1]
    tn = D_MODEL
    out = pl.pallas_call(
        _mod_kernel,
        grid=(n_out // tn,),
        in_specs=[
            pl.BlockSpec((MOD_ROWS, D_MODEL), lambda n: (0, 0)),
            pl.BlockSpec((D_MODEL, tn), lambda n: (0, n)),
            pl.BlockSpec((1, tn), lambda n: (0, n)),
        ],
        out_specs=pl.BlockSpec((MOD_ROWS, tn), lambda n: (0, n)),
        out_shape=jax.ShapeDtypeStruct((MOD_ROWS, n_out), F32),
        compiler_params=_params(("arbitrary",)),
        name="modulation",
    )(cond, w_ada, b_ada.reshape(1, n_out))
    return out.reshape(MOD_ROWS, 9, D_MODEL)


def _load_weight(w_hbm, w_scr, stage, sems):
    chunk = stage.shape[1]
    n_chunks = w_hbm.shape[0] // chunk

    def copy(c):
        return pltpu.make_async_copy(w_hbm.at[pl.ds(c * chunk, chunk), :], stage.at[c % 2], sems.at[c % 2])

    copy(0).start()
    for c in range(n_chunks):
        if c + 1 < n_chunks:
            copy(c + 1).start()
        copy(c).wait()
        w_scr[c * chunk:(c + 1) * chunk, :] = stage[c % 2].astype(BF16)


def _modulate(x, mod_ref, sub):
    return x * (1.0 + mod_ref[0, 3 * sub + 1:3 * sub + 2, :]) + mod_ref[0, 3 * sub:3 * sub + 1, :]


def _half_step(x, mod_ref, win_ref, wout_ref, lng_ref, lnb_ref, act_scr, sub):
    h = _modulate(x, mod_ref, sub).astype(BF16)
    for c in range(D_FF // FFN_TF):
        cols = slice(c * FFN_TF, (c + 1) * FFN_TF)
        up_cols = slice(D_FF + c * FFN_TF, D_FF + (c + 1) * FFN_TF)
        gate = jnp.dot(h, win_ref[:, cols], preferred_element_type=F32)
        up = jnp.dot(h, win_ref[:, up_cols], preferred_element_type=F32)
        act_scr[:, cols] = (gate * jax.nn.sigmoid(gate) * up).astype(BF16)
    f = jnp.dot(act_scr[...], wout_ref[...], preferred_element_type=F32)
    g = mod_ref[0, 3 * sub + 2:3 * sub + 3, :]
    return _layer_norm(ALPHA * x + 0.5 * g * f, lng_ref[sub:sub + 1, :], lnb_ref[sub:sub + 1, :])


def _ffn_weight_scratch():
    return [
        pltpu.VMEM((D_MODEL, 2 * D_FF), BF16),
        pltpu.VMEM((D_FF, D_MODEL), BF16),
        pltpu.VMEM((2, WIDE_CHUNK_ROWS, 2 * D_FF), F32),
        pltpu.VMEM((2, SQUARE_CHUNK_ROWS, D_MODEL), F32),
        pltpu.SemaphoreType.DMA((2,)),
        pltpu.SemaphoreType.DMA((2,)),
    ]


def _ffn1_kernel(xa_ref, xb_ref, mod_ref, win_hbm, wout_hbm, lng_ref, lnb_ref, xo_ref, ho_ref,
                 win_scr, wout_scr, wide_stage, square_stage, wide_sems, square_sems, act_scr, *, tiles_a):
    i = pl.program_id(0)

    @pl.when(i == 0)
    def _():
        _load_weight(win_hbm, win_scr, wide_stage, wide_sems)
        _load_weight(wout_hbm, wout_scr, square_stage, square_sems)

    x = jnp.where(i < tiles_a, xa_ref[...], xb_ref[...])
    xn = _half_step(x, mod_ref, win_scr, wout_scr, lng_ref, lnb_ref, act_scr, 0)
    xo_ref[...] = xn
    ho_ref[...] = _modulate(xn, mod_ref, 1).astype(BF16)


def _ffn1(xa, xb, mod, cond_of_tile, w_in, w_out, ln_g, ln_b):
    tiles_a, tiles_b = xa.shape[0] // FFN_TM, xb.shape[0] // FFN_TM
    m = xa.shape[0] + xb.shape[0]
    row = lambda i: (i, 0)
    const = lambda i: (0, 0)
    hbm = pl.BlockSpec(memory_space=pl.ANY)
    return pl.pallas_call(
        functools.partial(_ffn1_kernel, tiles_a=tiles_a),
        grid=(tiles_a + tiles_b,),
        in_specs=[
            pl.BlockSpec((FFN_TM, D_MODEL), lambda i: (jnp.minimum(i, tiles_a - 1), 0)),
            pl.BlockSpec((FFN_TM, D_MODEL), lambda i: (jnp.maximum(i - tiles_a, 0), 0)),
            pl.BlockSpec((1, 9, D_MODEL), lambda i: (cond_of_tile(i, FFN_TM), 0, 0)),
            hbm, hbm,
            pl.BlockSpec((3, D_MODEL), const),
            pl.BlockSpec((3, D_MODEL), const),
        ],
        out_specs=[pl.BlockSpec((FFN_TM, D_MODEL), row), pl.BlockSpec((FFN_TM, D_MODEL), row)],
        out_shape=[jax.ShapeDtypeStruct((m, D_MODEL), F32), jax.ShapeDtypeStruct((m, D_MODEL), BF16)],
        scratch_shapes=_ffn_weight_scratch() + [pltpu.VMEM((FFN_TM, D_FF), BF16)],
        compiler_params=_params(("arbitrary",)),
        name="ffn1",
    )(xa, xb, mod, w_in, w_out, ln_g, ln_b)


def _rope(x, cos, sin):
    lane = lax.broadcasted_iota(jnp.int32, (1, LANES), 1)
    first_half = (lane % 32) < 16
    outs = []
    for hb in range(QK_A // LANES):
        xb = x[:, hb * LANES:(hb + 1) * LANES]
        ahead = pltpu.roll(xb, LANES - 16, 1)
        behind = pltpu.roll(xb, 16, 1)
        rot = jnp.where(first_half, -ahead, behind)
        outs.append(xb * cos + rot * sin)
    return jnp.concatenate(outs, axis=-1)


def _proj_kernel(*refs, rope, sigmoid, scale_first, outs):
    if rope:
        h_ref, w_ref, cos_ref, sin_ref = refs[:4]
        refs = refs[4:]
    else:
        h_ref, w_ref = refs[:2]
        refs = refs[2:]
    out_refs = refs
    y = jnp.dot(h_ref[...], w_ref[...].astype(BF16), preferred_element_type=F32)
    if rope:
        t_rope = cos_ref.shape[1]
        y = jnp.concatenate([_rope(y[r * t_rope:(r + 1) * t_rope], cos_ref[0], sin_ref[0])
                             for r in range(y.shape[0] // t_rope)], axis=0)
    if scale_first:
        y = y * jnp.where(pl.program_id(1) == 0, Q_SCALE, 1.0)
    if sigmoid:
        y = jax.nn.sigmoid(y)
    for ref, kind in zip(out_refs, outs):
        if kind == 'head':
            for hh in range(MIX_TN // LANES):
                ref[hh] = y[:, hh * LANES:(hh + 1) * LANES].astype(ref.dtype)
        elif kind == 'stack':
            ref[0] = y.astype(ref.dtype)
        else:
            ref[...] = y.astype(ref.dtype)


def _project(h, group, w, first_block, n_blocks, outs, rope_tables=None, sigmoid=False, scale_first=False,
             name="proj"):
    m = MIX_TM
    tm = PROJ_TM
    tiles = m // tm
    rope = rope_tables is not None
    heads = MIX_TN // LANES
    in_specs = [pl.BlockSpec((tm, D_MODEL), lambda i, n: (group * tiles + i, 0)),
                pl.BlockSpec((D_MODEL, MIX_TN), lambda i, n: (0, first_block + n))]
    args = [h, w]
    if rope:
        t_rope = rope_tables[0].shape[1]
        rope_rows = min(t_rope, tm)
        n_rope_tiles = t_rope // rope_rows
        for t in rope_tables:
            in_specs.append(pl.BlockSpec((1, rope_rows, DV_A), lambda i, n: (n, i % n_rope_tiles, 0)))
            args.append(t)
    specs = {
        'head': (pl.BlockSpec((heads, tm, LANES), lambda i, n: (n, i, 0)), (n_blocks * heads, m, LANES)),
        'tok': (pl.BlockSpec((tm, MIX_TN), lambda i, n: (i, n)), (m, n_blocks * MIX_TN)),
        'stack': (pl.BlockSpec((1, tm, MIX_TN), lambda i, n: (n, i, 0)), (n_blocks, m, MIX_TN)),
    }
    return pl.pallas_call(
        functools.partial(_proj_kernel, rope=rope, sigmoid=sigmoid, scale_first=scale_first,
                          outs=tuple(kind for kind, _ in outs)),
        grid=(tiles, n_blocks),
        in_specs=in_specs,
        out_specs=[specs[kind][0] for kind, _ in outs],
        out_shape=[jax.ShapeDtypeStruct(specs[kind][1], dt) for kind, dt in outs],
        compiler_params=_params(("arbitrary", "arbitrary")),
        name=name,
    )(*args)


def _attn_kernel(*refs, lam_init, has_ctx):
    if has_ctx:
        lam_ref, g_ref, q_ref, k_ref, v_ref, kc_ref, vc_ref, o_ref = refs
    else:
        lam_ref, g_ref, q_ref, k_ref, v_ref, o_ref = refs
    lp = lam_ref[...]
    lam = (jnp.exp(jnp.sum(lp[0:1] * lp[1:2], axis=-1, keepdims=True))
           - jnp.exp(jnp.sum(lp[2:3] * lp[3:4], axis=-1, keepdims=True)) + lam_init)

    lane = lax.broadcasted_iota(jnp.int32, (1, DV_A), 1)
    comp0 = lane < DH_A
    tq = q_ref.shape[2]
    sub = min(ATT_SUB, tq)
    problems = []
    for hh in range(q_ref.shape[0]):
        keys = [k_ref[hh, 0]]
        vals = [v_ref[hh, 0]]
        if has_ctx:
            keys.append(kc_ref[0, :, hh * DV_A:(hh + 1) * DV_A].astype(BF16))
            vals.append(vc_ref[0, :, hh * DV_A:(hh + 1) * DV_A].astype(BF16))
        for r in range(tq // sub):
            q = q_ref[hh, 0, r * sub:(r + 1) * sub, :]
            zero = jnp.zeros_like(q)
            for qc in (jnp.where(comp0, q, zero), jnp.where(comp0, zero, q)):
                problems.append((qc, keys, vals))
    scores = [[_nt_dot(qc, kk) for kk in keys] for qc, keys, _ in problems]
    weights = []
    for ss in scores:
        mx = functools.reduce(jnp.maximum, [jnp.max(s, axis=-1, keepdims=True) for s in ss])
        es = [jnp.exp2(s - mx) for s in ss]
        den = functools.reduce(jnp.add, [jnp.sum(e, axis=-1, keepdims=True) for e in es])
        weights.append(([e.astype(BF16) for e in es], 1.0 / den))
    maps = [functools.reduce(jnp.add, [jnp.dot(e, vv, preferred_element_type=F32) for e, vv in zip(es, vals)]) * r
            for (es, r), (_, _, vals) in zip(weights, problems)]
    idx = 0
    for hh in range(q_ref.shape[0]):
        for r in range(tq // sub):
            o = maps[idx] - lam * maps[idx + 1]
            idx += 2
            o = o * lax.rsqrt(jnp.mean(o * o, axis=-1, keepdims=True) + RMS_EPS) * g_ref[...]
            o_ref[hh, 0, r * sub:(r + 1) * sub, :] = (o * (1.0 - lam_init)).astype(o_ref.dtype)


def _attention(q, k, v, ctx_kv, lam_params, subln_g, lam_init):
    (q_arr, q_row), (k_arr, k_row), (v_arr, v_row) = q, k, v
    _, bsz, t, _ = q_arr.shape
    tq = min(ATT_TQ, t)
    hp = N_HEADS_A if t <= ATT_SHORT_SEQ else 1
    const = lambda b, h, i: (0, 0)
    rows = lambda first: (lambda b, h, i: (first // hp + h, b, 0, 0))
    in_specs = [
        pl.BlockSpec((4, DH_A), const),
        pl.BlockSpec((1, DV_A), const),
        pl.BlockSpec((hp, 1, tq, DV_A), lambda b, h, i: (q_row // hp + h, b, i, 0)),
        pl.BlockSpec((hp, 1, t, DV_A), rows(k_row)),
        pl.BlockSpec((hp, 1, t, DV_A), rows(v_row)),
    ]
    args = [lam_params, subln_g, q_arr, k_arr, v_arr]
    if ctx_kv is not None:
        p = ctx_kv[0].shape[1]
        head_all = lambda b, h, i: (b, 0, h)
        in_specs += [pl.BlockSpec((1, p, hp * DV_A), head_all), pl.BlockSpec((1, p, hp * DV_A), head_all)]
        args += list(ctx_kv)
    return pl.pallas_call(
        functools.partial(_attn_kernel, lam_init=lam_init, has_ctx=ctx_kv is not None),
        grid=(bsz, N_HEADS_A // hp, t // tq),
        in_specs=in_specs,
        out_specs=pl.BlockSpec((hp, 1, tq, DV_A), lambda b, h, i: (h, b, i, 0)),
        out_shape=jax.ShapeDtypeStruct((N_HEADS_A, bsz, t, DV_A), BF16),
        compiler_params=_params(("parallel", "parallel", "arbitrary")),
        name="attention",
    )(*args)


def _chunk_scan(g, reverse):
    r = g.shape[0]
    pos = lax.broadcasted_iota(jnp.int32, (r, 1), 0) % HGRN_C
    b = g
    sh = 1
    while sh < HGRN_C:
        if reverse:
            b = b + jnp.where(pos < HGRN_C - sh, pltpu.roll(b, r - sh, 0), 0.0)
        else:
            b = b + jnp.where(pos >= sh, pltpu.roll(b, sh, 0), 0.0)
        sh *= 2
    return b


def _hgrn_kernel(*refs, seq_len, has_s0, want_state):
    refs = list(refs)
    hq_ref, hff_ref, hfb_ref, hi_ref, hg_ref, lbl_ref, ng_ref = refs[:7]
    refs = refs[7:]
    s0_ref = refs.pop(0) if has_s0 else None
    o_ref = refs.pop(0)
    sfin_ref = refs.pop(0) if want_state else None
    g_scr, kd_scr, q_scr, od_scr, st_scr, b_scr, ok_scr = refs
    hp = hq_ref.shape[0]
    c = HGRN_C
    rb = min(HGRN_ROWS, seq_len)
    nb = seq_len // rb
    cpb = rb // c
    z_refs = (hff_ref, hfb_ref)

    def lower_bound(hh, d):
        logits = lbl_ref[d, :, hh * DK_H:(hh + 1) * DK_H]
        mx = jnp.max(logits, axis=0, keepdims=True)
        e = jnp.exp(logits - mx)
        return e[0:1] / jnp.sum(e, axis=0, keepdims=True)

    lbs = [[lower_bound(hh, d) for d in range(2)] for hh in range(hp)]

    def gates(i, carry):
        rows = pl.ds(pl.multiple_of(i * rb, rb), rb)
        for d in range(2):
            worst = jnp.zeros((1, DK_H), F32)
            for hh in range(hp):
                if d == 0:
                    hq = hq_ref[hh, 0, rows, :]
                    q_scr[hh, rows, :] = hq * jax.nn.sigmoid(hq)
                z = z_refs[d][hh, 0, rows, :]
                lb = lbs[hh][d]
                e = jnp.exp(-jnp.abs(z))
                r = 1.0 / (1.0 + e)
                er = e * r
                pos = z >= 0.0
                g = jnp.log(lb + (1.0 - lb) * jnp.where(pos, r, er))
                g_scr[hh, d, rows, :] = g
                kd_scr[hh, d, rows, :] = (1.0 - lb) * jnp.where(pos, er, r)
                for j in range(cpb):
                    worst = jnp.maximum(worst, -jnp.sum(g[j * c:(j + 1) * c], axis=0, keepdims=True))
            ok_scr[d, i] = (jnp.max(worst) <= HGRN_SAFE_DECAY).astype(jnp.int32)
        return carry

    lax.fori_loop(0, nb, gates, 0)

    for hh in range(hp):
        for d in range(2):
            if has_s0:
                st_scr[hh, d] = s0_ref[0, d, hh].T
            else:
                st_scr[hh, d] = jnp.zeros((DV_H, DK_H), F32)

    row = lax.broadcasted_iota(jnp.int32, (c, 1), 0)
    col = lax.broadcasted_iota(jnp.int32, (1, c), 1)

    def fast_block(d, r0):
        reverse = d == 1
        rows = pl.ds(r0, rb)
        n = hp * cpb
        flat = lambda a: a.reshape(hp * rb, a.shape[-1])
        chunks = lambda a: a.reshape(n, c, a.shape[-1])
        vb = chunks(hi_ref[:, 0, rows, :].astype(BF16))
        b = _chunk_scan(flat(g_scr[:, d, rows, :]), reverse)
        qd = chunks((flat(q_scr[:, rows, :]) * jnp.exp(b)).astype(BF16))
        ku = chunks(flat(kd_scr[:, d, rows, :]) * jnp.exp(-b))
        b = chunks(b)
        e_end = jnp.exp(b[:, 0:1, :] if reverse else b[:, c - 1:c, :])
        causal = ((row <= col) if reverse else (row >= col))[None]
        a = jnp.einsum('nck,nsk->ncs', qd, ku.astype(BF16), preferred_element_type=F32)
        o = jnp.einsum('ncs,nsv->ncv', jnp.where(causal, a, 0.0).astype(BF16), vb, preferred_element_type=F32)
        u = jnp.einsum('nsv,nsk->nvk', vb, (ku * e_end).astype(BF16), preferred_element_type=F32)
        entering = [None] * n
        for hh in range(hp):
            st = st_scr[hh, d]
            for j in (range(cpb - 1, -1, -1) if reverse else range(cpb)):
                entering[hh * cpb + j] = st.astype(BF16)
                st = e_end[hh * cpb + j] * st + u[hh * cpb + j]
            st_scr[hh, d] = st
        o = o + jnp.einsum('nck,nvk->ncv', qd, jnp.stack(entering), preferred_element_type=F32)
        od_scr[:, d, rows, :] = o.reshape(hp, rb, DV_H)

    def exact_chunk(hh, d, r0):
        reverse = d == 1
        rows = pl.ds(r0, c)
        b = _chunk_scan(g_scr[hh, d, rows, :], reverse)
        b_scr[...] = b

        def one_row(t, carry):
            bt = b_scr[pl.ds(t, 1), :]
            qt = q_scr[hh, pl.ds(r0 + t, 1), :]
            seen = (row >= t) if reverse else (row <= t)
            w = jnp.exp(jnp.where(seen, bt - b_scr[...], -jnp.inf))
            p = jnp.sum(qt * kd_scr[hh, d, rows, :] * w, axis=-1, keepdims=True)
            od_scr[hh, d, pl.ds(r0 + t, 1), :] = jnp.sum(p * hi_ref[hh, 0, rows, :], axis=0, keepdims=True)
            return carry

        lax.fori_loop(0, c, one_row, 0)
        b_end = b[0:1] if reverse else b[c - 1:c]
        st = st_scr[hh, d]
        qd = (q_scr[hh, rows, :] * jnp.exp(b)).astype(BF16)
        od_scr[hh, d, rows, :] = od_scr[hh, d, rows, :] + _nt_dot(qd, st.astype(BF16))
        k_end = (kd_scr[hh, d, rows, :] * jnp.exp(b_end - b)).astype(BF16)
        st_scr[hh, d] = jnp.exp(b_end) * st + _tn_dot(hi_ref[hh, 0, rows, :].astype(BF16), k_end)

    def block(i, carry):
        starts = (pl.multiple_of(i * rb, rb), pl.multiple_of((nb - 1 - i) * rb, rb))
        mild = (ok_scr[0, i] + ok_scr[1, nb - 1 - i]) == 2

        @pl.when(mild)
        def _():
            for d in range(2):
                fast_block(d, starts[d])

        @pl.when(jnp.logical_not(mild))
        def _():
            for hh in range(hp):
                for d in range(2):
                    for j in (range(cpb - 1, -1, -1) if d == 1 else range(cpb)):
                        exact_chunk(hh, d, pl.multiple_of(starts[d] + j * c, c))

        return carry

    lax.fori_loop(0, nb, block, 0)

    def finish(i, carry):
        rows = pl.ds(pl.multiple_of(i * rb, rb), rb)
        for hh in range(hp):
            o = od_scr[hh, 0, rows, :] + od_scr[hh, 1, rows, :]
            o = o * lax.rsqrt(jnp.mean(o * o, axis=-1, keepdims=True) + RMS_EPS) * ng_ref[...]
            hg = hg_ref[hh, 0, rows, :]
            o_ref[hh, 0, rows, :] = (o * (hg * jax.nn.sigmoid(hg))).astype(o_ref.dtype)
        return carry

    lax.fori_loop(0, nb, finish, 0)
    if want_state:
        for hh in range(hp):
            for d in range(2):
                sfin_ref[0, d, hh] = st_scr[hh, d].T


def _hgrn(streams, lb_logits, norm_g, s0, want_state):
    _, bsz, t, _ = streams.shape
    hp = HGRN_SHORT_HEADS if t <= HGRN_ROWS else 1
    stream = lambda s: pl.BlockSpec((hp, 1, t, DK_H), lambda b, h: (s * N_HEADS_H // hp + h, b, 0, 0))
    in_specs = [stream(HGRN_BLOCKS.index(w)) for w in (W_HQ, W_HFF, W_HFB, W_HI, W_HG)]
    in_specs += [
        pl.BlockSpec((2, DEPTH + 1, hp * DK_H), lambda b, h: (0, 0, h)),
        pl.BlockSpec((1, DV_H), lambda b, h: (0, 0)),
    ]
    args = [streams] * 5 + [lb_logits, norm_g]
    state_spec = pl.BlockSpec((1, 2, hp, DK_H, DV_H), lambda b, h: (b, 0, h, 0, 0))
    if s0 is not None:
        in_specs.append(state_spec)
        args.append(s0)
    out_specs = [pl.BlockSpec((hp, 1, t, DV_H), lambda b, h: (h, b, 0, 0))]
    out_shape = [jax.ShapeDtypeStruct((N_HEADS_H, bsz, t, DV_H), BF16)]
    if want_state:
        out_specs.append(state_spec)
        out_shape.append(jax.ShapeDtypeStruct((bsz, 2, N_HEADS_H, DK_H, DV_H), F32))
    n_blocks = t // min(HGRN_ROWS, t)
    outs = pl.pallas_call(
        functools.partial(_hgrn_kernel, seq_len=t, has_s0=s0 is not None, want_state=want_state),
        grid=(bsz, N_HEADS_H // hp),
        in_specs=in_specs,
        out_specs=out_specs,
        out_shape=out_shape,
        scratch_shapes=[
            pltpu.VMEM((hp, 2, t, DK_H), F32),
            pltpu.VMEM((hp, 2, t, DK_H), F32),
            pltpu.VMEM((hp, t, DK_H), F32),
            pltpu.VMEM((hp, 2, t, DV_H), F32),
            pltpu.VMEM((hp, 2, DV_H, DK_H), F32),
            pltpu.VMEM((HGRN_C, DK_H), F32),
            pltpu.SMEM((2, n_blocks), jnp.int32),
        ],
        compiler_params=_params(("parallel", "parallel")),
        name="hgrn",
    )(*args)
    return outs if want_state else (outs[0], None)


def _merge_ffn2_kernel(oa_ref, oh_ref, sa_ref, sh_ref, x_ref, mod_ref, wa_hbm, wh_hbm, wo_hbm, win_hbm, wout_hbm,
                       lng_ref, lnb_ref, y_ref, wa_scr, wh_scr, wo_scr,
                       win_scr, wout_scr, wide_stage, square_stage, wide_sems, square_sems, act_scr):
    @pl.when(pl.program_id(0) == 0)
    def _():
        _load_weight(win_hbm, win_scr, wide_stage, wide_sems)
        for w_hbm, w_scr in ((wa_hbm, wa_scr), (wh_hbm, wh_scr), (wo_hbm, wo_scr), (wout_hbm, wout_scr)):
            _load_weight(w_hbm, w_scr, square_stage, square_sems)

    heads = lambda ref: jnp.concatenate([ref[hh] for hh in range(ref.shape[0])], axis=-1)
    ba = jnp.dot(heads(oa_ref), wa_scr[...], preferred_element_type=F32)
    bh = jnp.dot(heads(oh_ref), wh_scr[...], preferred_element_type=F32)
    merged = sa_ref[...].astype(F32) * ba + sh_ref[...].astype(F32) * bh
    mix = jnp.dot(merged.astype(BF16), wo_scr[...], preferred_element_type=F32)
    x2 = _layer_norm(ALPHA * x_ref[...] + mod_ref[0, 5:6, :] * mix, lng_ref[1:2, :], lnb_ref[1:2, :])
    y_ref[...] = _half_step(x2, mod_ref, win_scr, wout_scr, lng_ref, lnb_ref, act_scr, 2)


def _merge_ffn2(o_a, o_h, gates, x, first_tile, mod, cond_of_tile, w_a, w_h, w_o, w_in, w_out, ln_g, ln_b):
    m = gates.shape[0]
    row = lambda i: (i, 0)
    const = lambda i: (0, 0)
    per_head = lambda i: (0, i, 0)
    hbm = pl.BlockSpec(memory_space=pl.ANY)
    return pl.pallas_call(
        _merge_ffn2_kernel,
        grid=(m // FFN_TM,),
        in_specs=[
            pl.BlockSpec((N_HEADS_A, FFN_TM, DV_A), per_head),
            pl.BlockSpec((N_HEADS_H, FFN_TM, DV_H), per_head),
            pl.BlockSpec((FFN_TM, D_MODEL), lambda i: (i, 0)),
            pl.BlockSpec((FFN_TM, D_MODEL), lambda i: (i, 1)),
            pl.BlockSpec((FFN_TM, D_MODEL), lambda i: (first_tile + i, 0)),
            pl.BlockSpec((1, 9, D_MODEL), lambda i: (cond_of_tile(first_tile + i, FFN_TM), 0, 0)),
            hbm, hbm, hbm, hbm, hbm,
            pl.BlockSpec((3, D_MODEL), const),
            pl.BlockSpec((3, D_MODEL), const),
        ],
        out_specs=pl.BlockSpec((FFN_TM, D_MODEL), row),
        out_shape=jax.ShapeDtypeStruct((m, D_MODEL), F32),
        scratch_shapes=[pltpu.VMEM((W_A, D_MODEL), BF16), pltpu.VMEM((W_H, D_MODEL), BF16),
                        pltpu.VMEM((D_MODEL, D_MODEL), BF16)] + _ffn_weight_scratch()
                       + [pltpu.VMEM((FFN_TM, D_FF), BF16)],
        compiler_params=_params(("arbitrary",)),
        name="merge_ffn2",
    )(o_a, o_h, gates, gates, x, mod, w_a, w_h, w_o, w_in, w_out, ln_g, ln_b)


def _rope_tables(n_tok):
    rows = n_tok // GRID_W
    row = np.repeat(np.arange(rows, dtype=np.float32), GRID_W)
    col = np.tile(np.arange(GRID_W, dtype=np.float32), rows)
    half = DH_A // 2
    inv = (ROPE_BASE ** (-np.arange(0, half, 2, dtype=np.float32) / half)).astype(np.float32)
    ar = row[:, None] * inv
    ac = col[:, None] * inv
    ang = np.concatenate([ar, ar, ac, ac] * 2, axis=-1)
    cos, sin = np.cos(ang).astype(np.float32), np.sin(ang).astype(np.float32)
    scale = np.float32(Q_SCALE)
    return (jnp.asarray(np.stack([cos * scale, cos, np.ones_like(cos)])),
            jnp.asarray(np.stack([sin * scale, sin, np.zeros_like(sin)])))


def _mixer_and_ffn2(x1, h1, group, mod, cond_of_tile, wts, lam_init, bsz, ctx):
    m = MIX_TM
    t = m // bsz
    latent = ctx is not None
    per_head = lambda a: a.reshape(a.shape[0], bsz, t, a.shape[-1])
    w_mix = wts['w_mix_in']
    if latent:
        (qkv,) = _project(h1, group, w_mix, W_Q, 3, [('head', BF16)], rope_tables=_rope_tables(t), name="proj_qkv")
        k = v = None
        ctx_kv = (ctx[0], ctx[1])
    else:
        qkv, qkv32 = _project(h1, group, w_mix, W_Q, 3, [('head', BF16), ('stack', F32)], scale_first=True,
                              name="proj_qkv")
        k, v = qkv32[W_K], qkv32[W_V]
        ctx_kv = None
    (streams,) = _project(h1, group, w_mix, W_HQ, len(HGRN_BLOCKS), [('head', F32)], name="proj_hgrn")
    (gates,) = _project(h1, group, w_mix, W_GA0, len(GATE_BLOCKS), [('tok', BF16)], sigmoid=True,
                        name="proj_gates")
    qkv = per_head(qkv)
    q_kv = [(qkv, W_Q * N_HEADS_A), (qkv, W_K * N_HEADS_A), (qkv, W_V * N_HEADS_A)]
    o_a = _attention(*q_kv, ctx_kv, wts['lam_params'], wts['attn_subln_g'], lam_init)
    o_h, s_fin = _hgrn(per_head(streams), wts['lb_logits'], wts['hgrn_norm_g'],
                       ctx[2] if latent else None, not latent)
    y = _merge_ffn2(o_a.reshape(N_HEADS_A, m, DV_A), o_h.reshape(N_HEADS_H, m, DV_H), gates,
                    x1, group * (m // FFN_TM), mod, cond_of_tile,
                    wts['w_branch_a'], wts['w_branch_h'], wts['w_mix_out'],
                    wts['ffn2_w_in'], wts['ffn2_w_out'], wts['ln_g'], wts['ln_b'])
    return y, (k, v, s_fin)


def kernel(x_prompt, x_sample, cache_k, cache_v, state_hgrn, c, c_ctx, w_ada, b_ada, ffn1_w_in, ffn1_w_out,
           w_mix_in, lambda_q1, lambda_k1, lambda_q2, lambda_k2, attn_subln_g, hgrn_lb_logits, hgrn_norm_g,
           w_branch_a, w_branch_h, w_mix_out, ffn2_w_in, ffn2_w_out, ln_g, ln_b):
    batch, seq, _ = x_prompt.shape
    dec_batch, dec_seq, _ = x_sample.shape
    past = cache_k.shape[2]
    assert DEPTH == 1 and dec_batch + 1 <= MOD_ROWS
    lam_init = 0.8 - 0.6 * math.exp(-0.3 * 0)

    cond = jnp.concatenate([c_ctx[None, :], c, jnp.zeros((MOD_ROWS - 1 - dec_batch, D_MODEL), F32)], axis=0)
    mod = _modulation(cond, w_ada[0], b_ada[0])

    wts = {
        'w_mix_in': w_mix_in[0],
        'lam_params': jnp.concatenate([lambda_q1, lambda_k1, lambda_q2, lambda_k2], axis=0),
        'attn_subln_g': attn_subln_g, 'lb_logits': hgrn_lb_logits, 'hgrn_norm_g': hgrn_norm_g,
        'w_branch_a': w_branch_a[0], 'w_branch_h': w_branch_h[0], 'w_mix_out': w_mix_out[0],
        'ffn2_w_in': ffn2_w_in[0], 'ffn2_w_out': ffn2_w_out[0],
        'ln_g': ln_g[0], 'ln_b': ln_b[0],
    }
    n_ctx = batch * seq
    assert n_ctx == MIX_TM and dec_batch * dec_seq == MIX_TM

    def cond_of_tile(i, tm):
        first = i * tm
        return jnp.where(first < n_ctx, 0, 1 + (first - n_ctx) // dec_seq)

    x1, h1 = _ffn1(x_prompt.reshape(n_ctx, D_MODEL), x_sample.reshape(dec_batch * dec_seq, D_MODEL), mod,
                   cond_of_tile, ffn1_w_in[0], ffn1_w_out[0], ln_g[0], ln_b[0])
    y_p, (k_c, v_c, s_c) = _mixer_and_ffn2(x1, h1, 0, mod, cond_of_tile, wts, lam_init, batch, None)
    ctx = (cache_k[:, 0].reshape(dec_batch, past, QK_A), cache_v[:, 0].reshape(dec_batch, past, W_A),
           state_hgrn[:, 0])
    y_s, _ = _mixer_and_ffn2(x1, h1, 1, mod, cond_of_tile, wts, lam_init, dec_batch, ctx)

    return (y_p.reshape(batch, seq, D_MODEL),
            y_s.reshape(dec_batch, dec_seq, D_MODEL),
            k_c.reshape(batch, DEPTH, seq, N_HEADS_A, 2, DH_A),
            v_c.reshape(batch, DEPTH, seq, N_HEADS_A, DV_A),
            s_c.reshape(batch, DEPTH, 2, N_HEADS_H, DK_H, DV_H))
```

```python
import functools
import math

import numpy as np
import jax
import jax.numpy as jnp
from jax import lax
from jax.experimental import pallas as pl
from jax.experimental.pallas import tpu as pltpu

D_MODEL = 1024
DEPTH = 1
GRID_W = 64
N_HEADS_A = 4
DH_A = 64
DV_A = 2 * DH_A
QK_A = N_HEADS_A * 2 * DH_A
W_A = N_HEADS_A * DV_A
ROPE_BASE = 10000.0
N_HEADS_H = 4
DK_H = 128
DV_H = 128
QK_H = N_HEADS_H * DK_H
W_H = N_HEADS_H * DV_H
D_FF = 2816
MIX_IN = 2 * QK_A + W_A + 3 * QK_H + 2 * W_H + 2 * D_MODEL
ALPHA = (2 * DEPTH) ** 0.25
LN_EPS = 1e-5
RMS_EPS = 1e-6
Q_SCALE = DH_A ** -0.5 * math.log2(math.e)

F32 = jnp.float32
BF16 = jnp.bfloat16

LANES = 128
VMEM_LIMIT = 56 * 1024 * 1024

FFN_TM = 512
FFN_TF = 256
WIDE_CHUNK_ROWS = 128
SQUARE_CHUNK_ROWS = 256
MIX_TM = 4096
PROJ_TM = 4096
MIX_TN = 512
ATT_TQ = 1024
ATT_SUB = 512
ATT_SHORT_SEQ = 256
HGRN_C = 64
HGRN_ROWS = 512
HGRN_SHORT_HEADS = 4
HGRN_SAFE_DECAY = 75.0
MOD_ROWS = 8

W_Q, W_K, W_V, W_HQ, W_HFF, W_HFB, W_HI, W_HG, W_GA0, W_GA1, W_GH0, W_GH1 = range(12)
HGRN_BLOCKS = (W_HQ, W_HFF, W_HFB, W_HI, W_HG)


def _params(sem):
    return pltpu.CompilerParams(dimension_semantics=sem, vmem_limit_bytes=VMEM_LIMIT)


def _nt_dot(a, b):
    return lax.dot_general(a, b, (((1,), (1,)), ((), ())), preferred_element_type=F32)


def _tn_dot(a, b):
    return lax.dot_general(a, b, (((0,), (0,)), ((), ())), preferred_element_type=F32)


def _layer_norm(y, g, b):
    mu = jnp.mean(y, axis=-1, keepdims=True)
    yc = y - mu
    var = jnp.mean(yc * yc, axis=-1, keepdims=True)
    return yc * lax.rsqrt(var + LN_EPS) * g + b


def _mod_kernel(c_ref, w_ref, b_ref, o_ref):
    c = c_ref[...]
    a = (c * jax.nn.sigmoid(c)).astype(BF16)
    o_ref[...] = jnp.dot(a, w_ref[...].astype(BF16), preferred_element_type=F32) + b_ref[...]


def _modulation(cond, w_ada, b_ada):
    n_out = w_ada.shape[1]
    tn = D_MODEL
    out = pl.pallas_call(
        _mod_kernel,
        grid=(n_out // tn,),
        in_specs=[
            pl.BlockSpec((MOD_ROWS, D_MODEL), lambda n: (0, 0)),
            pl.BlockSpec((D_MODEL, tn), lambda n: (0, n)),
            pl.BlockSpec((1, tn), lambda n: (0, n)),
        ],
        out_specs=pl.BlockSpec((MOD_ROWS, tn), lambda n: (0, n)),
        out_shape=jax.ShapeDtypeStruct((MOD_ROWS, n_out), F32),
        compiler_params=_params(("arbitrary",)),
        name="modulation",
    )(cond, w_ada, b_ada.reshape(1, n_out))
    return out.reshape(MOD_ROWS, 9, D_MODEL)


def _load_weight(w_hbm, w_scr, stage, sems, src_col=0, dst_col=0):
    _, chunk, width = stage.shape
    n_chunks = w_hbm.shape[0] // chunk

    def copy(c):
        return pltpu.make_async_copy(w_hbm.at[pl.ds(c * chunk, chunk), pl.ds(src_col, width)],
                                     stage.at[c % 2], sems.at[c % 2])

    copy(0).start()
    for c in range(n_chunks):
        if c + 1 < n_chunks:
            copy(c + 1).start()
        copy(c).wait()
        w_scr[c * chunk:(c + 1) * chunk, dst_col:dst_col + width] = stage[c % 2].astype(BF16)


def _modulate(x, mod_ref, sub):
    return x * (1.0 + mod_ref[0, 3 * sub + 1:3 * sub + 2, :]) + mod_ref[0, 3 * sub:3 * sub + 1, :]


def _half_step(x, mod_ref, win_ref, wout_ref, lng_ref, lnb_ref, act_scr, sub):
    h = _modulate(x, mod_ref, sub).astype(BF16)
    for c in range(D_FF // FFN_TF):
        cols = slice(c * FFN_TF, (c + 1) * FFN_TF)
        up_cols = slice(D_FF + c * FFN_TF, D_FF + (c + 1) * FFN_TF)
        gate = jnp.dot(h, win_ref[:, cols], preferred_element_type=F32)
        up = jnp.dot(h, win_ref[:, up_cols], preferred_element_type=F32)
        act_scr[:, cols] = (gate * jax.nn.sigmoid(gate) * up).astype(BF16)
    f = jnp.dot(act_scr[...], wout_ref[...], preferred_element_type=F32)
    g = mod_ref[0, 3 * sub + 2:3 * sub + 3, :]
    return _layer_norm(ALPHA * x + 0.5 * g * f, lng_ref[sub:sub + 1, :], lnb_ref[sub:sub + 1, :])


def _ffn_weight_scratch():
    return [
        pltpu.VMEM((D_MODEL, 2 * D_FF), BF16),
        pltpu.VMEM((D_FF, D_MODEL), BF16),
        pltpu.VMEM((2, WIDE_CHUNK_ROWS, 2 * D_FF), F32),
        pltpu.VMEM((2, SQUARE_CHUNK_ROWS, D_MODEL), F32),
        pltpu.SemaphoreType.DMA((2,)),
        pltpu.SemaphoreType.DMA((2,)),
    ]


def _ffn1_kernel(xa_ref, xb_ref, mod_ref, win_hbm, wout_hbm, lng_ref, lnb_ref, xo_ref, ho_ref,
                 win_scr, wout_scr, wide_stage, square_stage, wide_sems, square_sems, act_scr, *, tiles_a):
    i = pl.program_id(0)

    @pl.when(i == 0)
    def _():
        _load_weight(win_hbm, win_scr, wide_stage, wide_sems)
        _load_weight(wout_hbm, wout_scr, square_stage, square_sems)

    x = jnp.where(i < tiles_a, xa_ref[...], xb_ref[...])
    xn = _half_step(x, mod_ref, win_scr, wout_scr, lng_ref, lnb_ref, act_scr, 0)
    xo_ref[...] = xn
    ho_ref[...] = _modulate(xn, mod_ref, 1).astype(BF16)


def _ffn1(xa, xb, mod, cond_of_tile, w_in, w_out, ln_g, ln_b):
    tiles_a, tiles_b = xa.shape[0] // FFN_TM, xb.shape[0] // FFN_TM
    m = xa.shape[0] + xb.shape[0]
    row = lambda i: (i, 0)
    const = lambda i: (0, 0)
    hbm = pl.BlockSpec(memory_space=pl.ANY)
    return pl.pallas_call(
        functools.partial(_ffn1_kernel, tiles_a=tiles_a),
        grid=(tiles_a + tiles_b,),
        in_specs=[
            pl.BlockSpec((FFN_TM, D_MODEL), lambda i: (jnp.minimum(i, tiles_a - 1), 0)),
            pl.BlockSpec((FFN_TM, D_MODEL), lambda i: (jnp.maximum(i - tiles_a, 0), 0)),
            pl.BlockSpec((1, 9, D_MODEL), lambda i: (cond_of_tile(i, FFN_TM), 0, 0)),
            hbm, hbm,
            pl.BlockSpec((3, D_MODEL), const),
            pl.BlockSpec((3, D_MODEL), const),
        ],
        out_specs=[pl.BlockSpec((FFN_TM, D_MODEL), row), pl.BlockSpec((FFN_TM, D_MODEL), row)],
        out_shape=[jax.ShapeDtypeStruct((m, D_MODEL), F32), jax.ShapeDtypeStruct((m, D_MODEL), BF16)],
        scratch_shapes=_ffn_weight_scratch() + [pltpu.VMEM((FFN_TM, D_FF), BF16)],
        compiler_params=_params(("arbitrary",)),
        name="ffn1",
    )(xa, xb, mod, w_in, w_out, ln_g, ln_b)


def _rope(x, cos, sin):
    lane = lax.broadcasted_iota(jnp.int32, (1, LANES), 1)
    first_half = (lane % 32) < 16
    outs = []
    for hb in range(QK_A // LANES):
        xb = x[:, hb * LANES:(hb + 1) * LANES]
        ahead = pltpu.roll(xb, LANES - 16, 1)
        behind = pltpu.roll(xb, 16, 1)
        rot = jnp.where(first_half, -ahead, behind)
        outs.append(xb * cos + rot * sin)
    return jnp.concatenate(outs, axis=-1)


def _proj_kernel(*refs, rope, scale_first, outs):
    if rope:
        h_ref, w_ref, cos_ref, sin_ref = refs[:4]
        refs = refs[4:]
    else:
        h_ref, w_ref = refs[:2]
        refs = refs[2:]
    out_refs = refs
    y = jnp.dot(h_ref[...], w_ref[...].astype(BF16), preferred_element_type=F32)
    if rope:
        t_rope = cos_ref.shape[1]
        y = jnp.concatenate([_rope(y[r * t_rope:(r + 1) * t_rope], cos_ref[0], sin_ref[0])
                             for r in range(y.shape[0] // t_rope)], axis=0)
    if scale_first:
        y = y * jnp.where(pl.program_id(1) == 0, Q_SCALE, 1.0)
    for ref, kind in zip(out_refs, outs):
        if kind == 'head':
            for hh in range(MIX_TN // LANES):
                ref[hh] = y[:, hh * LANES:(hh + 1) * LANES].astype(ref.dtype)
        else:
            ref[0] = y.astype(ref.dtype)


def _project(h, group, w, first_block, n_blocks, outs, rope_tables=None, scale_first=False, name="proj"):
    m = MIX_TM
    tm = PROJ_TM
    tiles = m // tm
    rope = rope_tables is not None
    heads = MIX_TN // LANES
    h_mode = dict(pipeline_mode=pl.Buffered(1)) if tiles == 1 else {}
    in_specs = [pl.BlockSpec((tm, D_MODEL), lambda i, n: (group * tiles + i, 0), **h_mode),
                pl.BlockSpec((D_MODEL, MIX_TN), lambda i, n: (0, first_block + n))]
    args = [h, w]
    if rope:
        t_rope = rope_tables[0].shape[1]
        rope_rows = min(t_rope, tm)
        n_rope_tiles = t_rope // rope_rows
        for t in rope_tables:
            in_specs.append(pl.BlockSpec((1, rope_rows, DV_A), lambda i, n: (n, i % n_rope_tiles, 0)))
            args.append(t)
    specs = {
        'head': (pl.BlockSpec((heads, tm, LANES), lambda i, n: (n, i, 0)), (n_blocks * heads, m, LANES)),
        'stack': (pl.BlockSpec((1, tm, MIX_TN), lambda i, n: (n, i, 0)), (n_blocks, m, MIX_TN)),
    }
    return pl.pallas_call(
        functools.partial(_proj_kernel, rope=rope, scale_first=scale_first,
                          outs=tuple(kind for kind, _ in outs)),
        grid=(tiles, n_blocks),
        in_specs=in_specs,
        out_specs=[specs[kind][0] for kind, _ in outs],
        out_shape=[jax.ShapeDtypeStruct(specs[kind][1], dt) for kind, dt in outs],
        compiler_params=_params(("arbitrary", "arbitrary")),
        name=name,
    )(*args)


def _attn_kernel(*refs, lam_init, has_ctx):
    if has_ctx:
        lam_ref, g_ref, q_ref, k_ref, v_ref, kc_ref, vc_ref, o_ref = refs
    else:
        lam_ref, g_ref, q_ref, k_ref, v_ref, o_ref = refs
    lp = lam_ref[...]
    lam = (jnp.exp(jnp.sum(lp[0:1] * lp[1:2], axis=-1, keepdims=True))
           - jnp.exp(jnp.sum(lp[2:3] * lp[3:4], axis=-1, keepdims=True)) + lam_init)

    lane = lax.broadcasted_iota(jnp.int32, (1, DV_A), 1)
    comp0 = lane < DH_A
    tq = q_ref.shape[2]
    sub = min(ATT_SUB, tq)
    problems = []
    for hh in range(q_ref.shape[0]):
        keys = [k_ref[hh, 0]]
        vals = [v_ref[hh, 0]]
        if has_ctx:
            keys.append(kc_ref[0, :, hh * DV_A:(hh + 1) * DV_A].astype(BF16))
            vals.append(vc_ref[0, :, hh * DV_A:(hh + 1) * DV_A].astype(BF16))
        for r in range(tq // sub):
            q = q_ref[hh, 0, r * sub:(r + 1) * sub, :]
            zero = jnp.zeros_like(q)
            for qc in (jnp.where(comp0, q, zero), jnp.where(comp0, zero, q)):
                problems.append((qc, keys, vals))
    scores = [[_nt_dot(qc, kk) for kk in keys] for qc, keys, _ in problems]
    weights = []
    for ss in scores:
        mx = functools.reduce(jnp.maximum, [jnp.max(s, axis=-1, keepdims=True) for s in ss])
        es = [jnp.exp2(s - mx) for s in ss]
        den = functools.reduce(jnp.add, [jnp.sum(e, axis=-1, keepdims=True) for e in es])
        weights.append(([e.astype(BF16) for e in es], 1.0 / den))
    maps = [functools.reduce(jnp.add, [jnp.dot(e, vv, preferred_element_type=F32) for e, vv in zip(es, vals)]) * r
            for (es, r), (_, _, vals) in zip(weights, problems)]
    idx = 0
    for hh in range(q_ref.shape[0]):
        for r in range(tq // sub):
            o = maps[idx] - lam * maps[idx + 1]
            idx += 2
            o = o * lax.rsqrt(jnp.mean(o * o, axis=-1, keepdims=True) + RMS_EPS) * g_ref[...]
            o_ref[hh, 0, r * sub:(r + 1) * sub, :] = (o * (1.0 - lam_init)).astype(o_ref.dtype)


def _attention(q, k, v, ctx_kv, lam_params, subln_g, lam_init):
    (q_arr, q_row), (k_arr, k_row), (v_arr, v_row) = q, k, v
    _, bsz, t, _ = q_arr.shape
    tq = min(ATT_TQ, t)
    hp = N_HEADS_A if t <= ATT_SHORT_SEQ else 1
    const = lambda b, h, i: (0, 0)
    rows = lambda first: (lambda b, h, i: (first // hp + h, b, 0, 0))
    in_specs = [
        pl.BlockSpec((4, DH_A), const),
        pl.BlockSpec((1, DV_A), const),
        pl.BlockSpec((hp, 1, tq, DV_A), lambda b, h, i: (q_row // hp + h, b, i, 0)),
        pl.BlockSpec((hp, 1, t, DV_A), rows(k_row)),
        pl.BlockSpec((hp, 1, t, DV_A), rows(v_row)),
    ]
    args = [lam_params, subln_g, q_arr, k_arr, v_arr]
    if ctx_kv is not None:
        p = ctx_kv[0].shape[1]
        head_all = lambda b, h, i: (b, 0, h)
        in_specs += [pl.BlockSpec((1, p, hp * DV_A), head_all), pl.BlockSpec((1, p, hp * DV_A), head_all)]
        args += list(ctx_kv)
    return pl.pallas_call(
        functools.partial(_attn_kernel, lam_init=lam_init, has_ctx=ctx_kv is not None),
        grid=(bsz, N_HEADS_A // hp, t // tq),
        in_specs=in_specs,
        out_specs=pl.BlockSpec((hp, 1, tq, DV_A), lambda b, h, i: (h, b, i, 0)),
        out_shape=jax.ShapeDtypeStruct((N_HEADS_A, bsz, t, DV_A), BF16),
        compiler_params=_params(("parallel", "parallel", "arbitrary")),
        name="attention",
    )(*args)


def _chunk_scan(g, reverse):
    r = g.shape[0]
    pos = lax.broadcasted_iota(jnp.int32, (r, 1), 0) % HGRN_C
    b = g
    sh = 1
    while sh < HGRN_C:
        if reverse:
            b = b + jnp.where(pos < HGRN_C - sh, pltpu.roll(b, r - sh, 0), 0.0)
        else:
            b = b + jnp.where(pos >= sh, pltpu.roll(b, sh, 0), 0.0)
        sh *= 2
    return b


def _hgrn_kernel(*refs, seq_len, has_s0, want_state):
    refs = list(refs)
    hq_ref, hff_ref, hfb_ref, hi_ref, hg_ref, lbl_ref, ng_ref = refs[:7]
    refs = refs[7:]
    s0_ref = refs.pop(0) if has_s0 else None
    o_ref = refs.pop(0)
    sfin_ref = refs.pop(0) if want_state else None
    g_scr, kd_scr, q_scr, od_scr, st_scr, b_scr, ok_scr = refs
    hp = hq_ref.shape[0]
    c = HGRN_C
    rb = min(HGRN_ROWS, seq_len)
    nb = seq_len // rb
    cpb = rb // c
    z_refs = (hff_ref, hfb_ref)

    def lower_bound(hh, d):
        logits = lbl_ref[d, :, hh * DK_H:(hh + 1) * DK_H]
        mx = jnp.max(logits, axis=0, keepdims=True)
        e = jnp.exp(logits - mx)
        return e[0:1] / jnp.sum(e, axis=0, keepdims=True)

    lbs = [[lower_bound(hh, d) for d in range(2)] for hh in range(hp)]

    def gates(i, carry):
        rows = pl.ds(pl.multiple_of(i * rb, rb), rb)
        for d in range(2):
            worst = jnp.zeros((1, DK_H), F32)
            for hh in range(hp):
                if d == 0:
                    hq = hq_ref[hh, 0, rows, :]
                    q_scr[hh, rows, :] = hq * jax.nn.sigmoid(hq)
                z = z_refs[d][hh, 0, rows, :]
                lb = lbs[hh][d]
                e = jnp.exp(-jnp.abs(z))
                r = 1.0 / (1.0 + e)
                er = e * r
                pos = z >= 0.0
                g = jnp.log(lb + (1.0 - lb) * jnp.where(pos, r, er))
                g_scr[hh, d, rows, :] = g
                kd_scr[hh, d, rows, :] = (1.0 - lb) * jnp.where(pos, er, r)
                for j in range(cpb):
                    worst = jnp.maximum(worst, -jnp.sum(g[j * c:(j + 1) * c], axis=0, keepdims=True))
            ok_scr[d, i] = (jnp.max(worst) <= HGRN_SAFE_DECAY).astype(jnp.int32)
        return carry

    lax.fori_loop(0, nb, gates, 0)

    for hh in range(hp):
        for d in range(2):
            if has_s0:
                st_scr[hh, d] = s0_ref[0, d, hh].T
            else:
                st_scr[hh, d] = jnp.zeros((DV_H, DK_H), F32)

    row = lax.broadcasted_iota(jnp.int32, (c, 1), 0)
    col = lax.broadcasted_iota(jnp.int32, (1, c), 1)

    def fast_block(d, r0):
        reverse = d == 1
        rows = pl.ds(r0, rb)
        n = hp * cpb
        flat = lambda a: a.reshape(hp * rb, a.shape[-1])
        chunks = lambda a: a.reshape(n, c, a.shape[-1])
        vb = chunks(hi_ref[:, 0, rows, :].astype(BF16))
        b = _chunk_scan(flat(g_scr[:, d, rows, :]), reverse)
        qd = chunks((flat(q_scr[:, rows, :]) * jnp.exp(b)).astype(BF16))
        ku = chunks(flat(kd_scr[:, d, rows, :]) * jnp.exp(-b))
        b = chunks(b)
        e_end = jnp.exp(b[:, 0:1, :] if reverse else b[:, c - 1:c, :])
        causal = ((row <= col) if reverse else (row >= col))[None]
        a = jnp.einsum('nck,nsk->ncs', qd, ku.astype(BF16), preferred_element_type=F32)
        o = jnp.einsum('ncs,nsv->ncv', jnp.where(causal, a, 0.0).astype(BF16), vb, preferred_element_type=F32)
        u = jnp.einsum('nsv,nsk->nvk', vb, (ku * e_end).astype(BF16), preferred_element_type=F32)
        entering = [None] * n
        for hh in range(hp):
            st = st_scr[hh, d]
            for j in (range(cpb - 1, -1, -1) if reverse else range(cpb)):
                entering[hh * cpb + j] = st.astype(BF16)
                st = e_end[hh * cpb + j] * st + u[hh * cpb + j]
            st_scr[hh, d] = st
        o = o + jnp.einsum('nck,nvk->ncv', qd, jnp.stack(entering), preferred_element_type=F32)
        od_scr[:, d, rows, :] = o.reshape(hp, rb, DV_H)

    def exact_chunk(hh, d, r0):
        reverse = d == 1
        rows = pl.ds(r0, c)
        b = _chunk_scan(g_scr[hh, d, rows, :], reverse)
        b_scr[...] = b

        def one_row(t, carry):
            bt = b_scr[pl.ds(t, 1), :]
            qt = q_scr[hh, pl.ds(r0 + t, 1), :]
            seen = (row >= t) if reverse else (row <= t)
            w = jnp.exp(jnp.where(seen, bt - b_scr[...], -jnp.inf))
            p = jnp.sum(qt * kd_scr[hh, d, rows, :] * w, axis=-1, keepdims=True)
            od_scr[hh, d, pl.ds(r0 + t, 1), :] = jnp.sum(p * hi_ref[hh, 0, rows, :], axis=0, keepdims=True)
            return carry

        lax.fori_loop(0, c, one_row, 0)
        b_end = b[0:1] if reverse else b[c - 1:c]
        st = st_scr[hh, d]
        qd = (q_scr[hh, rows, :] * jnp.exp(b)).astype(BF16)
        od_scr[hh, d, rows, :] = od_scr[hh, d, rows, :] + _nt_dot(qd, st.astype(BF16))
        k_end = (kd_scr[hh, d, rows, :] * jnp.exp(b_end - b)).astype(BF16)
        st_scr[hh, d] = jnp.exp(b_end) * st + _tn_dot(hi_ref[hh, 0, rows, :].astype(BF16), k_end)

    def block(i, carry):
        starts = (pl.multiple_of(i * rb, rb), pl.multiple_of((nb - 1 - i) * rb, rb))
        mild = (ok_scr[0, i] + ok_scr[1, nb - 1 - i]) == 2

        @pl.when(mild)
        def _():
            for d in range(2):
                fast_block(d, starts[d])

        @pl.when(jnp.logical_not(mild))
        def _():
            for hh in range(hp):
                for d in range(2):
                    for j in (range(cpb - 1, -1, -1) if d == 1 else range(cpb)):
                        exact_chunk(hh, d, pl.multiple_of(starts[d] + j * c, c))

        return carry

    lax.fori_loop(0, nb, block, 0)

    def finish(i, carry):
        rows = pl.ds(pl.multiple_of(i * rb, rb), rb)
        for hh in range(hp):
            o = od_scr[hh, 0, rows, :] + od_scr[hh, 1, rows, :]
            o = o * lax.rsqrt(jnp.mean(o * o, axis=-1, keepdims=True) + RMS_EPS) * ng_ref[...]
            hg = hg_ref[hh, 0, rows, :]
            o_ref[hh, 0, rows, :] = (o * (hg * jax.nn.sigmoid(hg))).astype(o_ref.dtype)
        return carry

    lax.fori_loop(0, nb, finish, 0)
    if want_state:
        for hh in range(hp):
            for d in range(2):
                sfin_ref[0, d, hh] = st_scr[hh, d].T


def _hgrn(streams, lb_logits, norm_g, s0, want_state):
    _, bsz, t, _ = streams.shape
    hp = HGRN_SHORT_HEADS if t <= HGRN_ROWS else 1
    stream = lambda s: pl.BlockSpec((hp, 1, t, DK_H), lambda b, h: (s * N_HEADS_H // hp + h, b, 0, 0))
    in_specs = [stream(HGRN_BLOCKS.index(w)) for w in (W_HQ, W_HFF, W_HFB, W_HI, W_HG)]
    in_specs += [
        pl.BlockSpec((2, DEPTH + 1, hp * DK_H), lambda b, h: (0, 0, h)),
        pl.BlockSpec((1, DV_H), lambda b, h: (0, 0)),
    ]
    args = [streams] * 5 + [lb_logits, norm_g]
    state_spec = pl.BlockSpec((1, 2, hp, DK_H, DV_H), lambda b, h: (b, 0, h, 0, 0))
    if s0 is not None:
        in_specs.append(state_spec)
        args.append(s0)
    out_specs = [pl.BlockSpec((hp, 1, t, DV_H), lambda b, h: (h, b, 0, 0))]
    out_shape = [jax.ShapeDtypeStruct((N_HEADS_H, bsz, t, DV_H), BF16)]
    if want_state:
        out_specs.append(state_spec)
        out_shape.append(jax.ShapeDtypeStruct((bsz, 2, N_HEADS_H, DK_H, DV_H), F32))
    n_blocks = t // min(HGRN_ROWS, t)
    outs = pl.pallas_call(
        functools.partial(_hgrn_kernel, seq_len=t, has_s0=s0 is not None, want_state=want_state),
        grid=(bsz, N_HEADS_H // hp),
        in_specs=in_specs,
        out_specs=out_specs,
        out_shape=out_shape,
        scratch_shapes=[
            pltpu.VMEM((hp, 2, t, DK_H), F32),
            pltpu.VMEM((hp, 2, t, DK_H), F32),
            pltpu.VMEM((hp, t, DK_H), F32),
            pltpu.VMEM((hp, 2, t, DV_H), F32),
            pltpu.VMEM((hp, 2, DV_H, DK_H), F32),
            pltpu.VMEM((HGRN_C, DK_H), F32),
            pltpu.SMEM((2, n_blocks), jnp.int32),
        ],
        compiler_params=_params(("parallel", "parallel")),
        name="hgrn",
    )(*args)
    return outs if want_state else (outs[0], None)


def _merge_ffn2_kernel(oa_ref, oh_ref, h_ref, x_ref, mod_ref, wmix_hbm, wa_hbm, wh_hbm, wo_hbm, win_hbm, wout_hbm,
                       lng_ref, lnb_ref, y_ref, wg_scr, wa_scr, wh_scr, wo_scr,
                       win_scr, wout_scr, wide_stage, square_stage, wide_sems, square_sems, act_scr):
    @pl.when(pl.program_id(0) == 0)
    def _():
        _load_weight(win_hbm, win_scr, wide_stage, wide_sems)
        for half in range(2):
            _load_weight(wmix_hbm, wg_scr, square_stage, square_sems,
                         src_col=W_GA0 * MIX_TN + half * D_MODEL, dst_col=half * D_MODEL)
        for w_hbm, w_scr in ((wa_hbm, wa_scr), (wh_hbm, wh_scr), (wo_hbm, wo_scr), (wout_hbm, wout_scr)):
            _load_weight(w_hbm, w_scr, square_stage, square_sems)

    heads = lambda ref: jnp.concatenate([ref[hh] for hh in range(ref.shape[0])], axis=-1)
    gates = jax.nn.sigmoid(jnp.dot(h_ref[...], wg_scr[...], preferred_element_type=F32))
    ba = jnp.dot(heads(oa_ref), wa_scr[...], preferred_element_type=F32)
    bh = jnp.dot(heads(oh_ref), wh_scr[...], preferred_element_type=F32)
    merged = gates[:, :D_MODEL] * ba + gates[:, D_MODEL:] * bh
    mix = jnp.dot(merged.astype(BF16), wo_scr[...], preferred_element_type=F32)
    x2 = _layer_norm(ALPHA * x_ref[...] + mod_ref[0, 5:6, :] * mix, lng_ref[1:2, :], lnb_ref[1:2, :])
    y_ref[...] = _half_step(x2, mod_ref, win_scr, wout_scr, lng_ref, lnb_ref, act_scr, 2)


def _merge_ffn2(o_a, o_h, h, x, first_tile, mod, cond_of_tile, w_mix, w_a, w_h, w_o, w_in, w_out, ln_g, ln_b):
    m = o_a.shape[1]
    row = lambda i: (i, 0)
    group_row = lambda i: (first_tile + i, 0)
    const = lambda i: (0, 0)
    per_head = lambda i: (0, i, 0)
    hbm = pl.BlockSpec(memory_space=pl.ANY)
    return pl.pallas_call(
        _merge_ffn2_kernel,
        grid=(m // FFN_TM,),
        in_specs=[
            pl.BlockSpec((N_HEADS_A, FFN_TM, DV_A), per_head),
            pl.BlockSpec((N_HEADS_H, FFN_TM, DV_H), per_head),
            pl.BlockSpec((FFN_TM, D_MODEL), group_row),
            pl.BlockSpec((FFN_TM, D_MODEL), group_row),
            pl.BlockSpec((1, 9, D_MODEL), lambda i: (cond_of_tile(first_tile + i, FFN_TM), 0, 0)),
            hbm, hbm, hbm, hbm, hbm, hbm,
            pl.BlockSpec((3, D_MODEL), const),
            pl.BlockSpec((3, D_MODEL), const),
        ],
        out_specs=pl.BlockSpec((FFN_TM, D_MODEL), row),
        out_shape=jax.ShapeDtypeStruct((m, D_MODEL), F32),
        scratch_shapes=[pltpu.VMEM((D_MODEL, 2 * D_MODEL), BF16), pltpu.VMEM((W_A, D_MODEL), BF16),
                        pltpu.VMEM((W_H, D_MODEL), BF16), pltpu.VMEM((D_MODEL, D_MODEL), BF16)]
                       + _ffn_weight_scratch() + [pltpu.VMEM((FFN_TM, D_FF), BF16)],
        compiler_params=_params(("arbitrary",)),
        name="merge_ffn2",
    )(o_a, o_h, h, x, mod, w_mix, w_a, w_h, w_o, w_in, w_out, ln_g, ln_b)


def _rope_tables(n_tok):
    rows = n_tok // GRID_W
    row = np.repeat(np.arange(rows, dtype=np.float32), GRID_W)
    col = np.tile(np.arange(GRID_W, dtype=np.float32), rows)
    half = DH_A // 2
    inv = (ROPE_BASE ** (-np.arange(0, half, 2, dtype=np.float32) / half)).astype(np.float32)
    ar = row[:, None] * inv
    ac = col[:, None] * inv
    ang = np.concatenate([ar, ar, ac, ac] * 2, axis=-1)
    cos, sin = np.cos(ang).astype(np.float32), np.sin(ang).astype(np.float32)
    scale = np.float32(Q_SCALE)
    return (jnp.asarray(np.stack([cos * scale, cos, np.ones_like(cos)])),
            jnp.asarray(np.stack([sin * scale, sin, np.zeros_like(sin)])))


def _mixer_and_ffn2(x1, h1, group, mod, cond_of_tile, wts, lam_init, bsz, ctx):
    m = MIX_TM
    t = m // bsz
    latent = ctx is not None
    per_head = lambda a: a.reshape(a.shape[0], bsz, t, a.shape[-1])
    w_mix = wts['w_mix_in']
    if latent:
        (qkv,) = _project(h1, group, w_mix, W_Q, 3, [('head', BF16)], rope_tables=_rope_tables(t), name="proj_qkv")
        k = v = None
        ctx_kv = (ctx[0], ctx[1])
    else:
        qkv, qkv32 = _project(h1, group, w_mix, W_Q, 3, [('head', BF16), ('stack', F32)], scale_first=True,
                              name="proj_qkv")
        k, v = qkv32[W_K], qkv32[W_V]
        ctx_kv = None
    (streams,) = _project(h1, group, w_mix, W_HQ, len(HGRN_BLOCKS), [('head', F32)], name="proj_hgrn")
    qkv = per_head(qkv)
    q_kv = [(qkv, W_Q * N_HEADS_A), (qkv, W_K * N_HEADS_A), (qkv, W_V * N_HEADS_A)]
    o_a = _attention(*q_kv, ctx_kv, wts['lam_params'], wts['attn_subln_g'], lam_init)
    o_h, s_fin = _hgrn(per_head(streams), wts['lb_logits'], wts['hgrn_norm_g'],
                       ctx[2] if latent else None, not latent)
    y = _merge_ffn2(o_a.reshape(N_HEADS_A, m, DV_A), o_h.reshape(N_HEADS_H, m, DV_H),
                    h1, x1, group * (m // FFN_TM), mod, cond_of_tile,
                    w_mix, wts['w_branch_a'], wts['w_branch_h'], wts['w_mix_out'],
                    wts['ffn2_w_in'], wts['ffn2_w_out'], wts['ln_g'], wts['ln_b'])
    return y, (k, v, s_fin)


def kernel(x_prompt, x_sample, cache_k, cache_v, state_hgrn, c, c_ctx, w_ada, b_ada, ffn1_w_in, ffn1_w_out,
           w_mix_in, lambda_q1, lambda_k1, lambda_q2, lambda_k2, attn_subln_g, hgrn_lb_logits, hgrn_norm_g,
           w_branch_a, w_branch_h, w_mix_out, ffn2_w_in, ffn2_w_out, ln_g, ln_b):
    batch, seq, _ = x_prompt.shape
    dec_batch, dec_seq, _ = x_sample.shape
    past = cache_k.shape[2]
    assert DEPTH == 1 and dec_batch + 1 <= MOD_ROWS
    lam_init = 0.8 - 0.6 * math.exp(-0.3 * 0)

    cond = jnp.concatenate([c_ctx[None, :], c, jnp.zeros((MOD_ROWS - 1 - dec_batch, D_MODEL), F32)], axis=0)
    mod = _modulation(cond, w_ada[0], b_ada[0])

    wts = {
        'w_mix_in': w_mix_in[0],
        'lam_params': jnp.concatenate([lambda_q1, lambda_k1, lambda_q2, lambda_k2], axis=0),
        'attn_subln_g': attn_subln_g, 'lb_logits': hgrn_lb_logits, 'hgrn_norm_g': hgrn_norm_g,
        'w_branch_a': w_branch_a[0], 'w_branch_h': w_branch_h[0], 'w_mix_out': w_mix_out[0],
        'ffn2_w_in': ffn2_w_in[0], 'ffn2_w_out': ffn2_w_out[0],
        'ln_g': ln_g[0], 'ln_b': ln_b[0],
    }
    n_ctx = batch * seq
    assert n_ctx == MIX_TM and dec_batch * dec_seq == MIX_TM

    def cond_of_tile(i, tm):
        first = i * tm
        return jnp.where(first < n_ctx, 0, 1 + (first - n_ctx) // dec_seq)

    x1, h1 = _ffn1(x_prompt.reshape(n_ctx, D_MODEL), x_sample.reshape(dec_batch * dec_seq, D_MODEL), mod,
                   cond_of_tile, ffn1_w_in[0], ffn1_w_out[0], ln_g[0], ln_b[0])
    y_p, (k_c, v_c, s_c) = _mixer_and_ffn2(x1, h1, 0, mod, cond_of_tile, wts, lam_init, batch, None)
    ctx = (cache_k[:, 0].reshape(dec_batch, past, QK_A), cache_v[:, 0].reshape(dec_batch, past, W_A),
           state_hgrn[:, 0])
    y_s, _ = _mixer_and_ffn2(x1, h1, 1, mod, cond_of_tile, wts, lam_init, dec_batch, ctx)

    return (y_p.reshape(batch, seq, D_MODEL),
            y_s.reshape(dec_batch, dec_seq, D_MODEL),
            k_c.reshape(batch, DEPTH, seq, N_HEADS_A, 2, DH_A),
            v_c.reshape(batch, DEPTH, seq, N_HEADS_A, DV_A),
            s_c.reshape(batch, DEPTH, 2, N_HEADS_H, DK_H, DV_H))
```

```python
import functools
import math

import numpy as np
import jax
import jax.numpy as jnp
from jax import lax
from jax.experimental import pallas as pl
from jax.experimental.pallas import tpu as pltpu

D_MODEL = 1024
DEPTH = 1
GRID_W = 64
N_HEADS_A = 4
DH_A = 64
DV_A = 2 * DH_A
QK_A = N_HEADS_A * 2 * DH_A
W_A = N_HEADS_A * DV_A
ROPE_BASE = 10000.0
N_HEADS_H = 4
DK_H = 128
DV_H = 128
QK_H = N_HEADS_H * DK_H
W_H = N_HEADS_H * DV_H
D_FF = 2816
MIX_IN = 2 * QK_A + W_A + 3 * QK_H + 2 * W_H + 2 * D_MODEL
ALPHA = (2 * DEPTH) ** 0.25
LN_EPS = 1e-5
RMS_EPS = 1e-6
Q_SCALE = DH_A ** -0.5 * math.log2(math.e)

F32 = jnp.float32
BF16 = jnp.bfloat16

LANES = 128
VMEM_LIMIT = 56 * 1024 * 1024

FFN_TM = 512
FFN_TF = 256
WIDE_CHUNK_ROWS = 128
SQUARE_CHUNK_ROWS = 256
MIX_TM = 4096
PROJ_TM = 4096
MIX_TN = 512
ATT_TQ = 1024
ATT_SUB = 512
ATT_SHORT_SEQ = 256
HGRN_C = 64
HGRN_ROWS = 512
HGRN_SHORT_HEADS = 4
HGRN_SAFE_DECAY = 75.0
MOD_ROWS = 8

W_Q, W_K, W_V, W_HQ, W_HFF, W_HFB, W_HI, W_HG, W_GA0, W_GA1, W_GH0, W_GH1 = range(12)
HGRN_BLOCKS = (W_HQ, W_HFF, W_HFB, W_HI, W_HG)


def _params(sem):
    return pltpu.CompilerParams(dimension_semantics=sem, vmem_limit_bytes=VMEM_LIMIT)


def _nt_dot(a, b):
    return lax.dot_general(a, b, (((1,), (1,)), ((), ())), preferred_element_type=F32)


def _tn_dot(a, b):
    return lax.dot_general(a, b, (((0,), (0,)), ((), ())), preferred_element_type=F32)


def _layer_norm(y, g, b):
    mu = jnp.mean(y, axis=-1, keepdims=True)
    yc = y - mu
    var = jnp.mean(yc * yc, axis=-1, keepdims=True)
    return yc * lax.rsqrt(var + LN_EPS) * g + b


def _mod_kernel(c_ref, w_ref, b_ref, o_ref):
    c = c_ref[...]
    a = (c * jax.nn.sigmoid(c)).astype(BF16)
    o_ref[...] = jnp.dot(a, w_ref[...].astype(BF16), preferred_element_type=F32) + b_ref[...]


def _modulation(cond, w_ada, b_ada):
    n_out = w_ada.shape[1]
    tn = D_MODEL
    out = pl.pallas_call(
        _mod_kernel,
        grid=(n_out // tn,),
        in_specs=[
            pl.BlockSpec((MOD_ROWS, D_MODEL), lambda n: (0, 0)),
            pl.BlockSpec((D_MODEL, tn), lambda n: (0, n)),
            pl.BlockSpec((1, tn), lambda n: (0, n)),
        ],
        out_specs=pl.BlockSpec((MOD_ROWS, tn), lambda n: (0, n)),
        out_shape=jax.ShapeDtypeStruct((MOD_ROWS, n_out), F32),
        compiler_params=_params(("arbitrary",)),
        name="modulation",
    )(cond, w_ada, b_ada.reshape(1, n_out))
    return out.reshape(MOD_ROWS, 9, D_MODEL)


def _load_weight(w_hbm, w_scr, stage, sems, src_col=0, dst_col=0):
    _, chunk, width = stage.shape
    n_chunks = w_hbm.shape[0] // chunk

    def copy(c):
        return pltpu.make_async_copy(w_hbm.at[pl.ds(c * chunk, chunk), pl.ds(src_col, width)],
                                     stage.at[c % 2], sems.at[c % 2])

    copy(0).start()
    for c in range(n_chunks):
        if c + 1 < n_chunks:
            copy(c + 1).start()
        copy(c).wait()
        w_scr[c * chunk:(c + 1) * chunk, dst_col:dst_col + width] = stage[c % 2].astype(BF16)


def _modulate(x, mod_ref, sub):
    return x * (1.0 + mod_ref[0, 3 * sub + 1:3 * sub + 2, :]) + mod_ref[0, 3 * sub:3 * sub + 1, :]


def _half_step(x, mod_ref, win_ref, wout_ref, lng_ref, lnb_ref, act_scr, sub):
    h = _modulate(x, mod_ref, sub).astype(BF16)
    for c in range(D_FF // FFN_TF):
        cols = slice(c * FFN_TF, (c + 1) * FFN_TF)
        up_cols = slice(D_FF + c * FFN_TF, D_FF + (c + 1) * FFN_TF)
        gate = jnp.dot(h, win_ref[:, cols], preferred_element_type=F32)
        up = jnp.dot(h, win_ref[:, up_cols], preferred_element_type=F32)
        act_scr[:, cols] = (gate * jax.nn.sigmoid(gate) * up).astype(BF16)
    f = jnp.dot(act_scr[...], wout_ref[...], preferred_element_type=F32)
    g = mod_ref[0, 3 * sub + 2:3 * sub + 3, :]
    return _layer_norm(ALPHA * x + 0.5 * g * f, lng_ref[sub:sub + 1, :], lnb_ref[sub:sub + 1, :])


def _ffn_weight_scratch():
    return [
        pltpu.VMEM((D_MODEL, 2 * D_FF), BF16),
        pltpu.VMEM((D_FF, D_MODEL), BF16),
        pltpu.VMEM((2, WIDE_CHUNK_ROWS, 2 * D_FF), F32),
        pltpu.VMEM((2, SQUARE_CHUNK_ROWS, D_MODEL), F32),
        pltpu.SemaphoreType.DMA((2,)),
        pltpu.SemaphoreType.DMA((2,)),
    ]


def _ffn1_kernel(xa_ref, xb_ref, mod_ref, win_hbm, wout_hbm, lng_ref, lnb_ref, xo_ref, ho_ref,
                 win_scr, wout_scr, wide_stage, square_stage, wide_sems, square_sems, act_scr, *, tiles_a):
    i = pl.program_id(0)

    @pl.when(i == 0)
    def _():
        _load_weight(win_hbm, win_scr, wide_stage, wide_sems)
        _load_weight(wout_hbm, wout_scr, square_stage, square_sems)

    x = jnp.where(i < tiles_a, xa_ref[...], xb_ref[...])
    xn = _half_step(x, mod_ref, win_scr, wout_scr, lng_ref, lnb_ref, act_scr, 0)
    xo_ref[...] = xn
    ho_ref[...] = _modulate(xn, mod_ref, 1).astype(BF16)


def _ffn1(xa, xb, mod, cond_of_tile, w_in, w_out, ln_g, ln_b):
    tiles_a, tiles_b = xa.shape[0] // FFN_TM, xb.shape[0] // FFN_TM
    m = xa.shape[0] + xb.shape[0]
    row = lambda i: (i, 0)
    const = lambda i: (0, 0)
    hbm = pl.BlockSpec(memory_space=pl.ANY)
    return pl.pallas_call(
        functools.partial(_ffn1_kernel, tiles_a=tiles_a),
        grid=(tiles_a + tiles_b,),
        in_specs=[
            pl.BlockSpec((FFN_TM, D_MODEL), lambda i: (jnp.minimum(i, tiles_a - 1), 0)),
            pl.BlockSpec((FFN_TM, D_MODEL), lambda i: (jnp.maximum(i - tiles_a, 0), 0)),
            pl.BlockSpec((1, 9, D_MODEL), lambda i: (cond_of_tile(i, FFN_TM), 0, 0)),
            hbm, hbm,
            pl.BlockSpec((3, D_MODEL), const),
            pl.BlockSpec((3, D_MODEL), const),
        ],
        out_specs=[pl.BlockSpec((FFN_TM, D_MODEL), row), pl.BlockSpec((FFN_TM, D_MODEL), row)],
        out_shape=[jax.ShapeDtypeStruct((m, D_MODEL), F32), jax.ShapeDtypeStruct((m, D_MODEL), BF16)],
        scratch_shapes=_ffn_weight_scratch() + [pltpu.VMEM((FFN_TM, D_FF), BF16)],
        compiler_params=_params(("arbitrary",)),
        name="ffn1",
    )(xa, xb, mod, w_in, w_out, ln_g, ln_b)


def _rope(x, cos, sin):
    lane = lax.broadcasted_iota(jnp.int32, (1, LANES), 1)
    first_half = (lane % 32) < 16
    outs = []
    for hb in range(QK_A // LANES):
        xb = x[:, hb * LANES:(hb + 1) * LANES]
        ahead = pltpu.roll(xb, LANES - 16, 1)
        behind = pltpu.roll(xb, 16, 1)
        rot = jnp.where(first_half, -ahead, behind)
        outs.append(xb * cos + rot * sin)
    return jnp.concatenate(outs, axis=-1)


def _proj_kernel(*refs, rope, scale_first, outs):
    if rope:
        h_ref, w_ref, cos_ref, sin_ref = refs[:4]
        refs = refs[4:]
    else:
        h_ref, w_ref = refs[:2]
        refs = refs[2:]
    out_refs = refs
    y = jnp.dot(h_ref[...], w_ref[...].astype(BF16), preferred_element_type=F32)
    if rope:
        t_rope = cos_ref.shape[1]
        y = jnp.concatenate([_rope(y[r * t_rope:(r + 1) * t_rope], cos_ref[0], sin_ref[0])
                             for r in range(y.shape[0] // t_rope)], axis=0)
    if scale_first:
        y = y * jnp.where(pl.program_id(1) == 0, Q_SCALE, 1.0)
    for ref, kind in zip(out_refs, outs):
        if kind == 'head':
            for hh in range(MIX_TN // LANES):
                ref[hh] = y[:, hh * LANES:(hh + 1) * LANES].astype(ref.dtype)
        else:
            ref[0] = y.astype(ref.dtype)


def _project(h, group, w, first_block, n_blocks, outs, rope_tables=None, scale_first=False, name="proj"):
    m = MIX_TM
    tm = PROJ_TM
    tiles = m // tm
    rope = rope_tables is not None
    heads = MIX_TN // LANES
    h_mode = dict(pipeline_mode=pl.Buffered(1)) if tiles == 1 else {}
    in_specs = [pl.BlockSpec((tm, D_MODEL), lambda i, n: (group * tiles + i, 0), **h_mode),
                pl.BlockSpec((D_MODEL, MIX_TN), lambda i, n: (0, first_block + n))]
    args = [h, w]
    if rope:
        t_rope = rope_tables[0].shape[1]
        rope_rows = min(t_rope, tm)
        n_rope_tiles = t_rope // rope_rows
        for t in rope_tables:
            in_specs.append(pl.BlockSpec((1, rope_rows, DV_A), lambda i, n: (n, i % n_rope_tiles, 0)))
            args.append(t)
    specs = {
        'head': (pl.BlockSpec((heads, tm, LANES), lambda i, n: (n, i, 0)), (n_blocks * heads, m, LANES)),
        'stack': (pl.BlockSpec((1, tm, MIX_TN), lambda i, n: (n, i, 0)), (n_blocks, m, MIX_TN)),
    }
    return pl.pallas_call(
        functools.partial(_proj_kernel, rope=rope, scale_first=scale_first,
                          outs=tuple(kind for kind, _ in outs)),
        grid=(tiles, n_blocks),
        in_specs=in_specs,
        out_specs=[specs[kind][0] for kind, _ in outs],
        out_shape=[jax.ShapeDtypeStruct(specs[kind][1], dt) for kind, dt in outs],
        compiler_params=_params(("arbitrary", "arbitrary")),
        name=name,
    )(*args)


def _attn_kernel(*refs, lam_init, has_ctx):
    if has_ctx:
        lam_ref, g_ref, q_ref, k_ref, v_ref, kc_ref, vc_ref, o_ref = refs
    else:
        lam_ref, g_ref, q_ref, k_ref, v_ref, o_ref = refs
    lp = lam_ref[...]
    lam = (jnp.exp(jnp.sum(lp[0:1] * lp[1:2], axis=-1, keepdims=True))
           - jnp.exp(jnp.sum(lp[2:3] * lp[3:4], axis=-1, keepdims=True)) + lam_init)

    lane = lax.broadcasted_iota(jnp.int32, (1, DV_A), 1)
    comp0 = lane < DH_A
    tq = q_ref.shape[2]
    sub = min(ATT_SUB, tq)
    problems = []
    for hh in range(q_ref.shape[0]):
        keys = [k_ref[hh, 0]]
        vals = [v_ref[hh, 0]]
        if has_ctx:
            keys.append(kc_ref[0, :, hh * DV_A:(hh + 1) * DV_A].astype(BF16))
            vals.append(vc_ref[0, :, hh * DV_A:(hh + 1) * DV_A].astype(BF16))
        for r in range(tq // sub):
            q = q_ref[hh, 0, r * sub:(r + 1) * sub, :]
            zero = jnp.zeros_like(q)
            for qc in (jnp.where(comp0, q, zero), jnp.where(comp0, zero, q)):
                problems.append((qc, keys, vals))
    scores = [[_nt_dot(qc, kk) for kk in keys] for qc, keys, _ in problems]
    weights = []
    for ss in scores:
        mx = functools.reduce(jnp.maximum, [jnp.max(s, axis=-1, keepdims=True) for s in ss])
        es = [jnp.exp2(s - mx) for s in ss]
        den = functools.reduce(jnp.add, [jnp.sum(e, axis=-1, keepdims=True) for e in es])
        weights.append(([e.astype(BF16) for e in es], 1.0 / den))
    maps = [functools.reduce(jnp.add, [jnp.dot(e, vv, preferred_element_type=F32) for e, vv in zip(es, vals)]) * r
            for (es, r), (_, _, vals) in zip(weights, problems)]
    idx = 0
    for hh in range(q_ref.shape[0]):
        for r in range(tq // sub):
            o = maps[idx] - lam * maps[idx + 1]
            idx += 2
            o = o * lax.rsqrt(jnp.mean(o * o, axis=-1, keepdims=True) + RMS_EPS) * g_ref[...]
            o_ref[hh, 0, r * sub:(r + 1) * sub, :] = (o * (1.0 - lam_init)).astype(o_ref.dtype)


def _attention(q, k, v, ctx_kv, lam_params, subln_g, lam_init):
    (q_arr, q_row), (k_arr, k_row), (v_arr, v_row) = q, k, v
    _, bsz, t, _ = q_arr.shape
    tq = min(ATT_TQ, t)
    hp = N_HEADS_A if t <= ATT_SHORT_SEQ else 1
    const = lambda b, h, i: (0, 0)
    rows = lambda first: (lambda b, h, i: (first // hp + h, b, 0, 0))
    in_specs = [
        pl.BlockSpec((4, DH_A), const),
        pl.BlockSpec((1, DV_A), const),
        pl.BlockSpec((hp, 1, tq, DV_A), lambda b, h, i: (q_row // hp + h, b, i, 0)),
        pl.BlockSpec((hp, 1, t, DV_A), rows(k_row)),
        pl.BlockSpec((hp, 1, t, DV_A), rows(v_row)),
    ]
    args = [lam_params, subln_g, q_arr, k_arr, v_arr]
    if ctx_kv is not None:
        p = ctx_kv[0].shape[1]
        head_all = lambda b, h, i: (b, 0, h)
        in_specs += [pl.BlockSpec((1, p, hp * DV_A), head_all), pl.BlockSpec((1, p, hp * DV_A), head_all)]
        args += list(ctx_kv)
    return pl.pallas_call(
        functools.partial(_attn_kernel, lam_init=lam_init, has_ctx=ctx_kv is not None),
        grid=(bsz, N_HEADS_A // hp, t // tq),
        in_specs=in_specs,
        out_specs=pl.BlockSpec((hp, 1, tq, DV_A), lambda b, h, i: (h, b, i, 0)),
        out_shape=jax.ShapeDtypeStruct((N_HEADS_A, bsz, t, DV_A), BF16),
        compiler_params=_params(("parallel", "parallel", "arbitrary")),
        name="attention",
    )(*args)


def _chunk_scan(g, reverse):
    r = g.shape[0]
    pos = lax.broadcasted_iota(jnp.int32, (r, 1), 0) % HGRN_C
    b = g
    sh = 1
    while sh < HGRN_C:
        if reverse:
            b = b + jnp.where(pos < HGRN_C - sh, pltpu.roll(b, r - sh, 0), 0.0)
        else:
            b = b + jnp.where(pos >= sh, pltpu.roll(b, sh, 0), 0.0)
        sh *= 2
    return b


def _hgrn_kernel(*refs, seq_len, has_s0, want_state):
    refs = list(refs)
    hq_ref, hff_ref, hfb_ref, hi_ref, hg_ref, lbl_ref, ng_ref = refs[:7]
    refs = refs[7:]
    s0_ref = refs.pop(0) if has_s0 else None
    o_ref = refs.pop(0)
    sfin_ref = refs.pop(0) if want_state else None
    g_scr, kd_scr, q_scr, od_scr, st_scr, b_scr, ok_scr = refs
    hp = hq_ref.shape[0]
    c = HGRN_C
    rb = min(HGRN_ROWS, seq_len)
    nb = seq_len // rb
    cpb = rb // c
    z_refs = (hff_ref, hfb_ref)

    def lower_bound(hh, d):
        logits = lbl_ref[d, :, hh * DK_H:(hh + 1) * DK_H]
        mx = jnp.max(logits, axis=0, keepdims=True)
        e = jnp.exp(logits - mx)
        return e[0:1] / jnp.sum(e, axis=0, keepdims=True)

    lbs = [[lower_bound(hh, d) for d in range(2)] for hh in range(hp)]

    def gates(i, carry):
        rows = pl.ds(pl.multiple_of(i * rb, rb), rb)
        for d in range(2):
            worst = jnp.zeros((1, DK_H), F32)
            for hh in range(hp):
                if d == 0:
                    hq = hq_ref[hh, 0, rows, :]
                    q_scr[hh, rows, :] = hq * jax.nn.sigmoid(hq)
                z = z_refs[d][hh, 0, rows, :]
                lb = lbs[hh][d]
                e = jnp.exp(-jnp.abs(z))
                r = 1.0 / (1.0 + e)
                er = e * r
                pos = z >= 0.0
                g = jnp.log(lb + (1.0 - lb) * jnp.where(pos, r, er))
                g_scr[hh, d, rows, :] = g
                kd_scr[hh, d, rows, :] = (1.0 - lb) * jnp.where(pos, er, r)
                for j in range(cpb):
                    worst = jnp.maximum(worst, -jnp.sum(g[j * c:(j + 1) * c], axis=0, keepdims=True))
            ok_scr[d, i] = (jnp.max(worst) <= HGRN_SAFE_DECAY).astype(jnp.int32)
        return carry

    lax.fori_loop(0, nb, gates, 0)

    for hh in range(hp):
        for d in range(2):
            if has_s0:
                st_scr[hh, d] = s0_ref[0, d, hh].T
            else:
                st_scr[hh, d] = jnp.zeros((DV_H, DK_H), F32)

    row = lax.broadcasted_iota(jnp.int32, (c, 1), 0)
    col = lax.broadcasted_iota(jnp.int32, (1, c), 1)

    def fast_block(d, r0):
        reverse = d == 1
        rows = pl.ds(r0, rb)
        n = hp * cpb
        flat = lambda a: a.reshape(hp * rb, a.shape[-1])
        chunks = lambda a: a.reshape(n, c, a.shape[-1])
        vb = chunks(hi_ref[:, 0, rows, :].astype(BF16))
        b = _chunk_scan(flat(g_scr[:, d, rows, :]), reverse)
        qd = chunks((flat(q_scr[:, rows, :]) * jnp.exp(b)).astype(BF16))
        ku = chunks(flat(kd_scr[:, d, rows, :]) * jnp.exp(-b))
        b = chunks(b)
        e_end = jnp.exp(b[:, 0:1, :] if reverse else b[:, c - 1:c, :])
        causal = ((row <= col) if reverse else (row >= col))[None]
        a = jnp.einsum('nck,nsk->ncs', qd, ku.astype(BF16), preferred_element_type=F32)
        o = jnp.einsum('ncs,nsv->ncv', jnp.where(causal, a, 0.0).astype(BF16), vb, preferred_element_type=F32)
        u = jnp.einsum('nsv,nsk->nvk', vb, (ku * e_end).astype(BF16), preferred_element_type=F32)
        entering = [None] * n
        for hh in range(hp):
            st = st_scr[hh, d]
            for j in (range(cpb - 1, -1, -1) if reverse else range(cpb)):
                entering[hh * cpb + j] = st.astype(BF16)
                st = e_end[hh * cpb + j] * st + u[hh * cpb + j]
            st_scr[hh, d] = st
        o = o + jnp.einsum('nck,nvk->ncv', qd, jnp.stack(entering), preferred_element_type=F32)
        od_scr[:, d, rows, :] = o.reshape(hp, rb, DV_H)

    def exact_chunk(hh, d, r0):
        reverse = d == 1
        rows = pl.ds(r0, c)
        b = _chunk_scan(g_scr[hh, d, rows, :], reverse)
        b_scr[...] = b

        def one_row(t, carry):
            bt = b_scr[pl.ds(t, 1), :]
            qt = q_scr[hh, pl.ds(r0 + t, 1), :]
            seen = (row >= t) if reverse else (row <= t)
            w = jnp.exp(jnp.where(seen, bt - b_scr[...], -jnp.inf))
            p = jnp.sum(qt * kd_scr[hh, d, rows, :] * w, axis=-1, keepdims=True)
            od_scr[hh, d, pl.ds(r0 + t, 1), :] = jnp.sum(p * hi_ref[hh, 0, rows, :], axis=0, keepdims=True)
            return carry

        lax.fori_loop(0, c, one_row, 0)
        b_end = b[0:1] if reverse else b[c - 1:c]
        st = st_scr[hh, d]
        qd = (q_scr[hh, rows, :] * jnp.exp(b)).astype(BF16)
        od_scr[hh, d, rows, :] = od_scr[hh, d, rows, :] + _nt_dot(qd, st.astype(BF16))
        k_end = (kd_scr[hh, d, rows, :] * jnp.exp(b_end - b)).astype(BF16)
        st_scr[hh, d] = jnp.exp(b_end) * st + _tn_dot(hi_ref[hh, 0, rows, :].astype(BF16), k_end)

    def block(i, carry):
        starts = (pl.multiple_of(i * rb, rb), pl.multiple_of((nb - 1 - i) * rb, rb))
        mild = (ok_scr[0, i] + ok_scr[1, nb - 1 - i]) == 2

        @pl.when(mild)
        def _():
            for d in range(2):
                fast_block(d, starts[d])

        @pl.when(jnp.logical_not(mild))
        def _():
            for hh in range(hp):
                for d in range(2):
                    for j in (range(cpb - 1, -1, -1) if d == 1 else range(cpb)):
                        exact_chunk(hh, d, pl.multiple_of(starts[d] + j * c, c))

        return carry

    lax.fori_loop(0, nb, block, 0)

    def finish(i, carry):
        rows = pl.ds(pl.multiple_of(i * rb, rb), rb)
        for hh in range(hp):
            o = od_scr[hh, 0, rows, :] + od_scr[hh, 1, rows, :]
            o = o * lax.rsqrt(jnp.mean(o * o, axis=-1, keepdims=True) + RMS_EPS) * ng_ref[...]
            hg = hg_ref[hh, 0, rows, :]
            o_ref[hh, 0, rows, :] = (o * (hg * jax.nn.sigmoid(hg))).astype(o_ref.dtype)
        return carry

    lax.fori_loop(0, nb, finish, 0)
    if want_state:
        for hh in range(hp):
            for d in range(2):
                sfin_ref[0, d, hh] = st_scr[hh, d].T


def _hgrn(streams, lb_logits, norm_g, s0, want_state):
    _, bsz, t, _ = streams.shape
    hp = HGRN_SHORT_HEADS if t <= HGRN_ROWS else 1
    stream = lambda s: pl.BlockSpec((hp, 1, t, DK_H), lambda b, h: (s * N_HEADS_H // hp + h, b, 0, 0))
    in_specs = [stream(HGRN_BLOCKS.index(w)) for w in (W_HQ, W_HFF, W_HFB, W_HI, W_HG)]
    in_specs += [
        pl.BlockSpec((2, DEPTH + 1, hp * DK_H), lambda b, h: (0, 0, h)),
        pl.BlockSpec((1, DV_H), lambda b, h: (0, 0)),
    ]
    args = [streams] * 5 + [lb_logits, norm_g]
    state_spec = pl.BlockSpec((1, 2, hp, DK_H, DV_H), lambda b, h: (b, 0, h, 0, 0))
    if s0 is not None:
        in_specs.append(state_spec)
        args.append(s0)
    out_specs = [pl.BlockSpec((hp, 1, t, DV_H), lambda b, h: (h, b, 0, 0))]
    out_shape = [jax.ShapeDtypeStruct((N_HEADS_H, bsz, t, DV_H), BF16)]
    if want_state:
        out_specs.append(state_spec)
        out_shape.append(jax.ShapeDtypeStruct((bsz, 2, N_HEADS_H, DK_H, DV_H), F32))
    n_blocks = t // min(HGRN_ROWS, t)
    outs = pl.pallas_call(
        functools.partial(_hgrn_kernel, seq_len=t, has_s0=s0 is not None, want_state=want_state),
        grid=(bsz, N_HEADS_H // hp),
        in_specs=in_specs,
        out_specs=out_specs,
        out_shape=out_shape,
        scratch_shapes=[
            pltpu.VMEM((hp, 2, t, DK_H), F32),
            pltpu.VMEM((hp, 2, t, DK_H), F32),
            pltpu.VMEM((hp, t, DK_H), F32),
            pltpu.VMEM((hp, 2, t, DV_H), F32),
            pltpu.VMEM((hp, 2, DV_H, DK_H), F32),
            pltpu.VMEM((HGRN_C, DK_H), F32),
            pltpu.SMEM((2, n_blocks), jnp.int32),
        ],
        compiler_params=_params(("parallel", "parallel")),
        name="hgrn",
    )(*args)
    return outs if want_state else (outs[0], None)


def _merge_ffn2_kernel(oa0_ref, oh0_ref, oa1_ref, oh1_ref, h_ref, x_ref, mod_ref,
                       wmix_hbm, wa_hbm, wh_hbm, wo_hbm, win_hbm, wout_hbm, lng_ref, lnb_ref, y0_ref, y1_ref,
                       wg_scr, wa_scr, wh_scr, wo_scr,
                       win_scr, wout_scr, wide_stage, square_stage, wide_sems, square_sems, act_scr, *, tiles_0):
    i = pl.program_id(0)
    first_group = i < tiles_0

    @pl.when(i == 0)
    def _():
        _load_weight(win_hbm, win_scr, wide_stage, wide_sems)
        for half in range(2):
            _load_weight(wmix_hbm, wg_scr, square_stage, square_sems,
                         src_col=W_GA0 * MIX_TN + half * D_MODEL, dst_col=half * D_MODEL)
        for w_hbm, w_scr in ((wa_hbm, wa_scr), (wh_hbm, wh_scr), (wo_hbm, wo_scr), (wout_hbm, wout_scr)):
            _load_weight(w_hbm, w_scr, square_stage, square_sems)

    def heads(ref0, ref1):
        pick = lambda hh: jnp.where(first_group, ref0[hh], ref1[hh])
        return jnp.concatenate([pick(hh) for hh in range(ref0.shape[0])], axis=-1)

    gates = jax.nn.sigmoid(jnp.dot(h_ref[...], wg_scr[...], preferred_element_type=F32))
    ba = jnp.dot(heads(oa0_ref, oa1_ref), wa_scr[...], preferred_element_type=F32)
    bh = jnp.dot(heads(oh0_ref, oh1_ref), wh_scr[...], preferred_element_type=F32)
    merged = gates[:, :D_MODEL] * ba + gates[:, D_MODEL:] * bh
    mix = jnp.dot(merged.astype(BF16), wo_scr[...], preferred_element_type=F32)
    x2 = _layer_norm(ALPHA * x_ref[...] + mod_ref[0, 5:6, :] * mix, lng_ref[1:2, :], lnb_ref[1:2, :])
    y = _half_step(x2, mod_ref, win_scr, wout_scr, lng_ref, lnb_ref, act_scr, 2)

    @pl.when(first_group)
    def _():
        y0_ref[...] = y

    @pl.when(jnp.logical_not(first_group))
    def _():
        y1_ref[...] = y


def _merge_ffn2(mixed, h, x, mod, cond_of_tile, w_mix, w_a, w_h, w_o, w_in, w_out, ln_g, ln_b):
    (oa0, oh0), (oa1, oh1) = mixed
    tiles_0, tiles_1 = oa0.shape[1] // FFN_TM, oa1.shape[1] // FFN_TM
    row = lambda i: (i, 0)
    const = lambda i: (0, 0)
    in_group_0 = lambda i: jnp.minimum(i, tiles_0 - 1)
    in_group_1 = lambda i: jnp.maximum(i - tiles_0, 0)
    hbm = pl.BlockSpec(memory_space=pl.ANY)
    return pl.pallas_call(
        functools.partial(_merge_ffn2_kernel, tiles_0=tiles_0),
        grid=(tiles_0 + tiles_1,),
        in_specs=[
            pl.BlockSpec((N_HEADS_A, FFN_TM, DV_A), lambda i: (0, in_group_0(i), 0)),
            pl.BlockSpec((N_HEADS_H, FFN_TM, DV_H), lambda i: (0, in_group_0(i), 0)),
            pl.BlockSpec((N_HEADS_A, FFN_TM, DV_A), lambda i: (0, in_group_1(i), 0)),
            pl.BlockSpec((N_HEADS_H, FFN_TM, DV_H), lambda i: (0, in_group_1(i), 0)),
            pl.BlockSpec((FFN_TM, D_MODEL), row),
            pl.BlockSpec((FFN_TM, D_MODEL), row),
            pl.BlockSpec((1, 9, D_MODEL), lambda i: (cond_of_tile(i, FFN_TM), 0, 0)),
            hbm, hbm, hbm, hbm, hbm, hbm,
            pl.BlockSpec((3, D_MODEL), const),
            pl.BlockSpec((3, D_MODEL), const),
        ],
        out_specs=[pl.BlockSpec((FFN_TM, D_MODEL), lambda i: (in_group_0(i), 0)),
                   pl.BlockSpec((FFN_TM, D_MODEL), lambda i: (in_group_1(i), 0))],
        out_shape=[jax.ShapeDtypeStruct((tiles_0 * FFN_TM, D_MODEL), F32),
                   jax.ShapeDtypeStruct((tiles_1 * FFN_TM, D_MODEL), F32)],
        scratch_shapes=[pltpu.VMEM((D_MODEL, 2 * D_MODEL), BF16), pltpu.VMEM((W_A, D_MODEL), BF16),
                        pltpu.VMEM((W_H, D_MODEL), BF16), pltpu.VMEM((D_MODEL, D_MODEL), BF16)]
                       + _ffn_weight_scratch() + [pltpu.VMEM((FFN_TM, D_FF), BF16)],
        compiler_params=_params(("arbitrary",)),
        name="merge_ffn2",
    )(oa0, oh0, oa1, oh1, h, x, mod, w_mix, w_a, w_h, w_o, w_in, w_out, ln_g, ln_b)


def _rope_tables(n_tok):
    rows = n_tok // GRID_W
    row = np.repeat(np.arange(rows, dtype=np.float32), GRID_W)
    col = np.tile(np.arange(GRID_W, dtype=np.float32), rows)
    half = DH_A // 2
    inv = (ROPE_BASE ** (-np.arange(0, half, 2, dtype=np.float32) / half)).astype(np.float32)
    ar = row[:, None] * inv
    ac = col[:, None] * inv
    ang = np.concatenate([ar, ar, ac, ac] * 2, axis=-1)
    cos, sin = np.cos(ang).astype(np.float32), np.sin(ang).astype(np.float32)
    scale = np.float32(Q_SCALE)
    return (jnp.asarray(np.stack([cos * scale, cos, np.ones_like(cos)])),
            jnp.asarray(np.stack([sin * scale, sin, np.zeros_like(sin)])))


def _mixer(h1, group, wts, lam_init, bsz, ctx):
    m = MIX_TM
    t = m // bsz
    latent = ctx is not None
    per_head = lambda a: a.reshape(a.shape[0], bsz, t, a.shape[-1])
    w_mix = wts['w_mix_in']
    if latent:
        (qkv,) = _project(h1, group, w_mix, W_Q, 3, [('head', BF16)], rope_tables=_rope_tables(t), name="proj_qkv")
        k = v = None
        ctx_kv = (ctx[0], ctx[1])
    else:
        qkv, qkv32 = _project(h1, group, w_mix, W_Q, 3, [('head', BF16), ('stack', F32)], scale_first=True,
                              name="proj_qkv")
        k, v = qkv32[W_K], qkv32[W_V]
        ctx_kv = None
    (streams,) = _project(h1, group, w_mix, W_HQ, len(HGRN_BLOCKS), [('head', F32)], name="proj_hgrn")
    qkv = per_head(qkv)
    q_kv = [(qkv, W_Q * N_HEADS_A), (qkv, W_K * N_HEADS_A), (qkv, W_V * N_HEADS_A)]
    o_a = _attention(*q_kv, ctx_kv, wts['lam_params'], wts['attn_subln_g'], lam_init)
    o_h, s_fin = _hgrn(per_head(streams), wts['lb_logits'], wts['hgrn_norm_g'],
                       ctx[2] if latent else None, not latent)
    return (o_a.reshape(N_HEADS_A, m, DV_A), o_h.reshape(N_HEADS_H, m, DV_H)), (k, v, s_fin)


def kernel(x_prompt, x_sample, cache_k, cache_v, state_hgrn, c, c_ctx, w_ada, b_ada, ffn1_w_in, ffn1_w_out,
           w_mix_in, lambda_q1, lambda_k1, lambda_q2, lambda_k2, attn_subln_g, hgrn_lb_logits, hgrn_norm_g,
           w_branch_a, w_branch_h, w_mix_out, ffn2_w_in, ffn2_w_out, ln_g, ln_b):
    batch, seq, _ = x_prompt.shape
    dec_batch, dec_seq, _ = x_sample.shape
    past = cache_k.shape[2]
    assert DEPTH == 1 and dec_batch + 1 <= MOD_ROWS
    lam_init = 0.8 - 0.6 * math.exp(-0.3 * 0)

    cond = jnp.concatenate([c_ctx[None, :], c, jnp.zeros((MOD_ROWS - 1 - dec_batch, D_MODEL), F32)], axis=0)
    mod = _modulation(cond, w_ada[0], b_ada[0])

    wts = {
        'w_mix_in': w_mix_in[0],
        'lam_params': jnp.concatenate([lambda_q1, lambda_k1, lambda_q2, lambda_k2], axis=0),
        'attn_subln_g': attn_subln_g, 'lb_logits': hgrn_lb_logits, 'hgrn_norm_g': hgrn_norm_g,
    }
    n_ctx = batch * seq
    assert n_ctx == MIX_TM and dec_batch * dec_seq == MIX_TM

    def cond_of_tile(i, tm):
        first = i * tm
        return jnp.where(first < n_ctx, 0, 1 + (first - n_ctx) // dec_seq)

    x1, h1 = _ffn1(x_prompt.reshape(n_ctx, D_MODEL), x_sample.reshape(dec_batch * dec_seq, D_MODEL), mod,
                   cond_of_tile, ffn1_w_in[0], ffn1_w_out[0], ln_g[0], ln_b[0])
    mixed_p, (k_c, v_c, s_c) = _mixer(h1, 0, wts, lam_init, batch, None)
    ctx = (cache_k[:, 0].reshape(dec_batch, past, QK_A), cache_v[:, 0].reshape(dec_batch, past, W_A),
           state_hgrn[:, 0])
    mixed_s, _ = _mixer(h1, 1, wts, lam_init, dec_batch, ctx)
    y_p, y_s = _merge_ffn2((mixed_p, mixed_s), h1, x1, mod, cond_of_tile, w_mix_in[0],
                           w_branch_a[0], w_branch_h[0], w_mix_out[0], ffn2_w_in[0], ffn2_w_out[0],
                           ln_g[0], ln_b[0])

    return (y_p.reshape(batch, seq, D_MODEL),
            y_s.reshape(dec_batch, dec_seq, D_MODEL),
            k_c.reshape(batch, DEPTH, seq, N_HEADS_A, 2, DH_A),
            v_c.reshape(batch, DEPTH, seq, N_HEADS_A, DV_A),
            s_c.reshape(batch, DEPTH, 2, N_HEADS_H, DK_H, DV_H))
```

```python
import functools
import math

import numpy as np
import jax
import jax.numpy as jnp
from jax import lax
from jax.experimental import pallas as pl
from jax.experimental.pallas import tpu as pltpu

D_MODEL = 1024
DEPTH = 1
GRID_W = 64
N_HEADS_A = 4
DH_A = 64
DV_A = 2 * DH_A
QK_A = N_HEADS_A * 2 * DH_A
W_A = N_HEADS_A * DV_A
ROPE_BASE = 10000.0
N_HEADS_H = 4
DK_H = 128
DV_H = 128
QK_H = N_HEADS_H * DK_H
W_H = N_HEADS_H * DV_H
D_FF = 2816
MIX_IN = 2 * QK_A + W_A + 3 * QK_H + 2 * W_H + 2 * D_MODEL
ALPHA = (2 * DEPTH) ** 0.25
LN_EPS = 1e-5
RMS_EPS = 1e-6
Q_SCALE = DH_A ** -0.5 * math.log2(math.e)

F32 = jnp.float32
BF16 = jnp.bfloat16

LANES = 128
VMEM_LIMIT = 56 * 1024 * 1024

FFN_TM = 512
FFN_TF = 256
WIDE_CHUNK_ROWS = 128
SQUARE_CHUNK_ROWS = 256
MIX_TM = 4096
PROJ_TM = 2048
MIX_TN = 512
ATT_TQ = 1024
ATT_SUB = 512
ATT_SHORT_SEQ = 256
HGRN_C = 64
HGRN_ROWS = 512
HGRN_SHORT_HEADS = 4
HGRN_SAFE_DECAY = 75.0
MOD_ROWS = 8

W_Q, W_K, W_V, W_HQ, W_HFF, W_HFB, W_HI, W_HG, W_GA0, W_GA1, W_GH0, W_GH1 = range(12)


def _params(sem):
    return pltpu.CompilerParams(dimension_semantics=sem, vmem_limit_bytes=VMEM_LIMIT)


def _nt_dot(a, b):
    return lax.dot_general(a, b, (((1,), (1,)), ((), ())), preferred_element_type=F32)


def _tn_dot(a, b):
    return lax.dot_general(a, b, (((0,), (0,)), ((), ())), preferred_element_type=F32)


def _layer_norm(y, g, b):
    mu = jnp.mean(y, axis=-1, keepdims=True)
    yc = y - mu
    var = jnp.mean(yc * yc, axis=-1, keepdims=True)
    return yc * lax.rsqrt(var + LN_EPS) * g + b


def _mod_kernel(c_ref, w_ref, b_ref, o_ref):
    c = c_ref[...]
    a = (c * jax.nn.sigmoid(c)).astype(BF16)
    o_ref[...] = jnp.dot(a, w_ref[...].astype(BF16), preferred_element_type=F32) + b_ref[...]


def _modulation(cond, w_ada, b_ada):
    n_out = w_ada.shape[1]
    tn = D_MODEL
    out = pl.pallas_call(
        _mod_kernel,
        grid=(n_out // tn,),
        in_specs=[
            pl.BlockSpec((MOD_ROWS, D_MODEL), lambda n: (0, 0)),
            pl.BlockSpec((D_MODEL, tn), lambda n: (0, n)),
            pl.BlockSpec((1, tn), lambda n: (0, n)),
        ],
        out_specs=pl.BlockSpec((MOD_ROWS, tn), lambda n: (0, n)),
        out_shape=jax.ShapeDtypeStruct((MOD_ROWS, n_out), F32),
        compiler_params=_params(("arbitrary",)),
        name="modulation",
    )(cond, w_ada, b_ada.reshape(1, n_out))
    return out.reshape(MOD_ROWS, 9, D_MODEL)


def _load_weight(w_hbm, w_scr, stage, sems, src_col=0, dst_col=0):
    _, chunk, width = stage.shape
    n_chunks = w_hbm.shape[0] // chunk

    def copy(c):
        return pltpu.make_async_copy(w_hbm.at[pl.ds(c * chunk, chunk), pl.ds(src_col, width)],
                                     stage.at[c % 2], sems.at[c % 2])

    copy(0).start()
    for c in range(n_chunks):
        if c + 1 < n_chunks:
            copy(c + 1).start()
        copy(c).wait()
        w_scr[c * chunk:(c + 1) * chunk, dst_col:dst_col + width] = stage[c % 2].astype(BF16)


def _modulate(x, mod_ref, sub):
    return x * (1.0 + mod_ref[0, 3 * sub + 1:3 * sub + 2, :]) + mod_ref[0, 3 * sub:3 * sub + 1, :]


def _half_step(x, mod_ref, win_ref, wout_ref, lng_ref, lnb_ref, act_scr, sub):
    h = _modulate(x, mod_ref, sub).astype(BF16)
    for c in range(D_FF // FFN_TF):
        cols = slice(c * FFN_TF, (c + 1) * FFN_TF)
        up_cols = slice(D_FF + c * FFN_TF, D_FF + (c + 1) * FFN_TF)
        gate = jnp.dot(h, win_ref[:, cols], preferred_element_type=F32)
        up = jnp.dot(h, win_ref[:, up_cols], preferred_element_type=F32)
        act_scr[:, cols] = (gate * jax.nn.sigmoid(gate) * up).astype(BF16)
    f = jnp.dot(act_scr[...], wout_ref[...], preferred_element_type=F32)
    g = mod_ref[0, 3 * sub + 2:3 * sub + 3, :]
    return _layer_norm(ALPHA * x + 0.5 * g * f, lng_ref[sub:sub + 1, :], lnb_ref[sub:sub + 1, :])


def _ffn_weight_scratch():
    return [
        pltpu.VMEM((D_MODEL, 2 * D_FF), BF16),
        pltpu.VMEM((D_FF, D_MODEL), BF16),
        pltpu.VMEM((2, WIDE_CHUNK_ROWS, 2 * D_FF), F32),
        pltpu.VMEM((2, SQUARE_CHUNK_ROWS, D_MODEL), F32),
        pltpu.SemaphoreType.DMA((2,)),
        pltpu.SemaphoreType.DMA((2,)),
    ]


def _ffn1_kernel(xa_ref, xb_ref, mod_ref, win_hbm, wout_hbm, lng_ref, lnb_ref, xo_ref, ho_ref,
                 win_scr, wout_scr, wide_stage, square_stage, wide_sems, square_sems, act_scr, *, tiles_a):
    i = pl.program_id(0)

    @pl.when(i == 0)
    def _():
        _load_weight(win_hbm, win_scr, wide_stage, wide_sems)
        _load_weight(wout_hbm, wout_scr, square_stage, square_sems)

    x = jnp.where(i < tiles_a, xa_ref[...], xb_ref[...])
    xn = _half_step(x, mod_ref, win_scr, wout_scr, lng_ref, lnb_ref, act_scr, 0)
    xo_ref[...] = xn
    ho_ref[...] = _modulate(xn, mod_ref, 1).astype(BF16)


def _ffn1(xa, xb, mod, cond_of_tile, w_in, w_out, ln_g, ln_b):
    tiles_a, tiles_b = xa.shape[0] // FFN_TM, xb.shape[0] // FFN_TM
    m = xa.shape[0] + xb.shape[0]
    row = lambda i: (i, 0)
    const = lambda i: (0, 0)
    hbm = pl.BlockSpec(memory_space=pl.ANY)
    return pl.pallas_call(
        functools.partial(_ffn1_kernel, tiles_a=tiles_a),
        grid=(tiles_a + tiles_b,),
        in_specs=[
            pl.BlockSpec((FFN_TM, D_MODEL), lambda i: (jnp.minimum(i, tiles_a - 1), 0)),
            pl.BlockSpec((FFN_TM, D_MODEL), lambda i: (jnp.maximum(i - tiles_a, 0), 0)),
            pl.BlockSpec((1, 9, D_MODEL), lambda i: (cond_of_tile(i, FFN_TM), 0, 0)),
            hbm, hbm,
            pl.BlockSpec((3, D_MODEL), const),
            pl.BlockSpec((3, D_MODEL), const),
        ],
        out_specs=[pl.BlockSpec((FFN_TM, D_MODEL), row), pl.BlockSpec((FFN_TM, D_MODEL), row)],
        out_shape=[jax.ShapeDtypeStruct((m, D_MODEL), F32), jax.ShapeDtypeStruct((m, D_MODEL), BF16)],
        scratch_shapes=_ffn_weight_scratch() + [pltpu.VMEM((FFN_TM, D_FF), BF16)],
        compiler_params=_params(("arbitrary",)),
        name="ffn1",
    )(xa, xb, mod, w_in, w_out, ln_g, ln_b)


def _rope(x, cos, sin):
    lane = lax.broadcasted_iota(jnp.int32, (1, LANES), 1)
    first_half = (lane % 32) < 16
    outs = []
    for hb in range(QK_A // LANES):
        xb = x[:, hb * LANES:(hb + 1) * LANES]
        ahead = pltpu.roll(xb, LANES - 16, 1)
        behind = pltpu.roll(xb, 16, 1)
        rot = jnp.where(first_half, -ahead, behind)
        outs.append(xb * cos + rot * sin)
    return jnp.concatenate(outs, axis=-1)


def _forget_gate(z, logits):
    mx = jnp.max(logits, axis=0, keepdims=True)
    el = jnp.exp(logits - mx)
    lb = el[0:1] / jnp.sum(el, axis=0, keepdims=True)
    e = jnp.exp(-jnp.abs(z))
    r = 1.0 / (1.0 + e)
    er = e * r
    pos = z >= 0.0
    log_f = jnp.log(lb + (1.0 - lb) * jnp.where(pos, r, er))
    return log_f, (1.0 - lb) * jnp.where(pos, er, r)


def _proj_kernel(*refs, rope, want_kv32):
    refs = list(refs)
    h_ref, w_ref, lbl_ref = refs[:3]
    refs = refs[3:]
    cos_ref, sin_ref = (refs.pop(0), refs.pop(0)) if rope else (None, None)
    qkv_ref = refs.pop(0)
    kv32_ref = refs.pop(0) if want_kv32 else None
    plain_ref, g_ref, kd_ref = refs
    n = pl.program_id(1)

    def project():
        return jnp.dot(h_ref[...], w_ref[...].astype(BF16), preferred_element_type=F32)

    def store_heads(ref, y):
        for hh in range(MIX_TN // LANES):
            ref[hh] = y[:, hh * LANES:(hh + 1) * LANES].astype(ref.dtype)

    @pl.when(n <= W_V)
    def _():
        y = project()
        if want_kv32:
            kv32_ref[0] = y
        if rope:
            t_rope = cos_ref.shape[1]
            y = jnp.concatenate([_rope(y[r * t_rope:(r + 1) * t_rope], cos_ref[0], sin_ref[0])
                                 for r in range(y.shape[0] // t_rope)], axis=0)
        else:
            y = y * jnp.where(n == W_Q, Q_SCALE, 1.0)
        store_heads(qkv_ref, y)

    @pl.when((n == W_HFF) | (n == W_HFB))
    def _():
        log_f, kd = _forget_gate(project(), lbl_ref[0])
        store_heads(g_ref, log_f)
        store_heads(kd_ref, kd)

    @pl.when((n == W_HQ) | (n >= W_HI))
    def _():
        store_heads(plain_ref, project())


def _project(h, group, w, lb_logits, rope_tables, want_kv32):
    m = MIX_TM
    tm = PROJ_TM
    tiles = m // tm
    rope = rope_tables is not None
    heads = MIX_TN // LANES
    n_blocks = W_HG + 1
    in_specs = [pl.BlockSpec((tm, D_MODEL), lambda i, n: (group * tiles + i, 0)),
                pl.BlockSpec((D_MODEL, MIX_TN), lambda i, n: (0, n)),
                pl.BlockSpec((1, DEPTH + 1, QK_H), lambda i, n: (jnp.clip(n - W_HFF, 0, 1), 0, 0))]
    args = [h, w, lb_logits]
    qkv_col = lambda n: jnp.minimum(n, W_V)
    if rope:
        t_rope = rope_tables[0].shape[1]
        rope_rows = min(t_rope, tm)
        n_rope_tiles = t_rope // rope_rows
        for t in rope_tables:
            in_specs.append(pl.BlockSpec((1, rope_rows, DV_A), lambda i, n: (qkv_col(n), i % n_rope_tiles, 0)))
            args.append(t)
    head_block = (heads, tm, LANES)
    plain_col = lambda n: jnp.where(n < W_HI, 0, n - W_HI + 1)
    gate_col = lambda n: jnp.clip(n - W_HFF, 0, 1)
    out_specs = [pl.BlockSpec(head_block, lambda i, n: (qkv_col(n), i, 0))]
    out_shape = [jax.ShapeDtypeStruct((3 * heads, m, LANES), BF16)]
    if want_kv32:
        out_specs.append(pl.BlockSpec((1, tm, MIX_TN), lambda i, n: (qkv_col(n), i, 0)))
        out_shape.append(jax.ShapeDtypeStruct((3, m, MIX_TN), F32))
    out_specs += [pl.BlockSpec(head_block, lambda i, n: (plain_col(n), i, 0)),
                  pl.BlockSpec(head_block, lambda i, n: (gate_col(n), i, 0)),
                  pl.BlockSpec(head_block, lambda i, n: (gate_col(n), i, 0))]
    out_shape += [jax.ShapeDtypeStruct((3 * heads, m, LANES), BF16),
                  jax.ShapeDtypeStruct((2 * heads, m, LANES), F32),
                  jax.ShapeDtypeStruct((2 * heads, m, LANES), BF16)]
    return pl.pallas_call(
        functools.partial(_proj_kernel, rope=rope, want_kv32=want_kv32),
        grid=(tiles, n_blocks),
        in_specs=in_specs,
        out_specs=out_specs,
        out_shape=out_shape,
        compiler_params=_params(("arbitrary", "arbitrary")),
        name="mix_proj",
    )(*args)


def _attn_kernel(*refs, lam_init, has_ctx):
    if has_ctx:
        lam_ref, g_ref, q_ref, k_ref, v_ref, kc_ref, vc_ref, o_ref = refs
    else:
        lam_ref, g_ref, q_ref, k_ref, v_ref, o_ref = refs
    lp = lam_ref[...]
    lam = (jnp.exp(jnp.sum(lp[0:1] * lp[1:2], axis=-1, keepdims=True))
           - jnp.exp(jnp.sum(lp[2:3] * lp[3:4], axis=-1, keepdims=True)) + lam_init)

    lane = lax.broadcasted_iota(jnp.int32, (1, DV_A), 1)
    comp0 = lane < DH_A
    tq = q_ref.shape[2]
    sub = min(ATT_SUB, tq)
    problems = []
    for hh in range(q_ref.shape[0]):
        keys = [k_ref[hh, 0]]
        vals = [v_ref[hh, 0]]
        if has_ctx:
            keys.append(kc_ref[0, :, hh * DV_A:(hh + 1) * DV_A].astype(BF16))
            vals.append(vc_ref[0, :, hh * DV_A:(hh + 1) * DV_A].astype(BF16))
        for r in range(tq // sub):
            q = q_ref[hh, 0, r * sub:(r + 1) * sub, :]
            zero = jnp.zeros_like(q)
            for qc in (jnp.where(comp0, q, zero), jnp.where(comp0, zero, q)):
                problems.append((qc, keys, vals))
    scores = [[_nt_dot(qc, kk) for kk in keys] for qc, keys, _ in problems]
    weights = []
    for ss in scores:
        mx = functools.reduce(jnp.maximum, [jnp.max(s, axis=-1, keepdims=True) for s in ss])
        es = [jnp.exp2(s - mx) for s in ss]
        den = functools.reduce(jnp.add, [jnp.sum(e, axis=-1, keepdims=True) for e in es])
        weights.append(([e.astype(BF16) for e in es], 1.0 / den))
    maps = [functools.reduce(jnp.add, [jnp.dot(e, vv, preferred_element_type=F32) for e, vv in zip(es, vals)]) * r
            for (es, r), (_, _, vals) in zip(weights, problems)]
    idx = 0
    for hh in range(q_ref.shape[0]):
        for r in range(tq // sub):
            o = maps[idx] - lam * maps[idx + 1]
            idx += 2
            o = o * lax.rsqrt(jnp.mean(o * o, axis=-1, keepdims=True) + RMS_EPS) * g_ref[...]
            o_ref[hh, 0, r * sub:(r + 1) * sub, :] = (o * (1.0 - lam_init)).astype(o_ref.dtype)


def _attention(q, k, v, ctx_kv, lam_params, subln_g, lam_init):
    (q_arr, q_row), (k_arr, k_row), (v_arr, v_row) = q, k, v
    _, bsz, t, _ = q_arr.shape
    tq = min(ATT_TQ, t)
    hp = N_HEADS_A if t <= ATT_SHORT_SEQ else 1
    const = lambda b, h, i: (0, 0)
    rows = lambda first: (lambda b, h, i: (first // hp + h, b, 0, 0))
    in_specs = [
        pl.BlockSpec((4, DH_A), const),
        pl.BlockSpec((1, DV_A), const),
        pl.BlockSpec((hp, 1, tq, DV_A), lambda b, h, i: (q_row // hp + h, b, i, 0)),
        pl.BlockSpec((hp, 1, t, DV_A), rows(k_row)),
        pl.BlockSpec((hp, 1, t, DV_A), rows(v_row)),
    ]
    args = [lam_params, subln_g, q_arr, k_arr, v_arr]
    if ctx_kv is not None:
        p = ctx_kv[0].shape[1]
        head_all = lambda b, h, i: (b, 0, h)
        in_specs += [pl.BlockSpec((1, p, hp * DV_A), head_all), pl.BlockSpec((1, p, hp * DV_A), head_all)]
        args += list(ctx_kv)
    return pl.pallas_call(
        functools.partial(_attn_kernel, lam_init=lam_init, has_ctx=ctx_kv is not None),
        grid=(bsz, N_HEADS_A // hp, t // tq),
        in_specs=in_specs,
        out_specs=pl.BlockSpec((hp, 1, tq, DV_A), lambda b, h, i: (h, b, i, 0)),
        out_shape=jax.ShapeDtypeStruct((N_HEADS_A, bsz, t, DV_A), BF16),
        compiler_params=_params(("parallel", "parallel", "arbitrary")),
        name="attention",
    )(*args)


def _chunk_scan(g, reverse):
    r = g.shape[0]
    pos = lax.broadcasted_iota(jnp.int32, (r, 1), 0) % HGRN_C
    b = g
    sh = 1
    while sh < HGRN_C:
        if reverse:
            b = b + jnp.where(pos < HGRN_C - sh, pltpu.roll(b, r - sh, 0), 0.0)
        else:
            b = b + jnp.where(pos >= sh, pltpu.roll(b, sh, 0), 0.0)
        sh *= 2
    return b


def _hgrn_kernel(*refs, seq_len, has_s0, want_state):
    refs = list(refs)
    hq_ref, hi_ref, hg_ref, gf_ref, gb_ref, kf_ref, kb_ref, ng_ref = refs[:8]
    refs = refs[8:]
    s0_ref = refs.pop(0) if has_s0 else None
    o_ref = refs.pop(0)
    sfin_ref = refs.pop(0) if want_state else None
    od_scr, st_scr, b_scr, q_scr, k_scr, ok_scr = refs
    hp = hq_ref.shape[0]
    c = HGRN_C
    rb = min(HGRN_ROWS, seq_len)
    nb = seq_len // rb
    cpb = rb // c
    g_refs = (gf_ref, gb_ref)
    k_refs = (kf_ref, kb_ref)

    def queries(hq):
        hq = hq.astype(F32)
        return hq * jax.nn.sigmoid(hq)

    def flags(i, carry):
        rows = pl.ds(pl.multiple_of(i * rb, rb), rb)
        for d in range(2):
            worst = jnp.zeros((1, DK_H), F32)
            for hh in range(hp):
                g = g_refs[d][hh, 0, rows, :]
                for j in range(cpb):
                    worst = jnp.maximum(worst, -jnp.sum(g[j * c:(j + 1) * c], axis=0, keepdims=True))
            ok_scr[d, i] = (jnp.max(worst) <= HGRN_SAFE_DECAY).astype(jnp.int32)
        return carry

    lax.fori_loop(0, nb, flags, 0)

    for hh in range(hp):
        for d in range(2):
            if has_s0:
                st_scr[hh, d] = s0_ref[0, d, hh].T
            else:
                st_scr[hh, d] = jnp.zeros((DV_H, DK_H), F32)

    row = lax.broadcasted_iota(jnp.int32, (c, 1), 0)
    col = lax.broadcasted_iota(jnp.int32, (1, c), 1)

    def fast_block(d, r0):
        reverse = d == 1
        rows = pl.ds(r0, rb)
        n = hp * cpb
        flat = lambda a: a.reshape(hp * rb, a.shape[-1])
        chunks = lambda a: a.reshape(n, c, a.shape[-1])
        vb = chunks(hi_ref[:, 0, rows, :])
        b = _chunk_scan(flat(g_refs[d][:, 0, rows, :]), reverse)
        qd = chunks((queries(flat(hq_ref[:, 0, rows, :])) * jnp.exp(b)).astype(BF16))
        ku = chunks(flat(k_refs[d][:, 0, rows, :]).astype(F32) * jnp.exp(-b))
        b = chunks(b)
        e_end = jnp.exp(b[:, 0:1, :] if reverse else b[:, c - 1:c, :])
        causal = ((row <= col) if reverse else (row >= col))[None]
        a = jnp.einsum('nck,nsk->ncs', qd, ku.astype(BF16), preferred_element_type=F32)
        o = jnp.einsum('ncs,nsv->ncv', jnp.where(causal, a, 0.0).astype(BF16), vb, preferred_element_type=F32)
        u = jnp.einsum('nsv,nsk->nvk', vb, (ku * e_end).astype(BF16), preferred_element_type=F32)
        entering = [None] * n
        for hh in range(hp):
            st = st_scr[hh, d]
            for j in (range(cpb - 1, -1, -1) if reverse else range(cpb)):
                entering[hh * cpb + j] = st.astype(BF16)
                st = e_end[hh * cpb + j] * st + u[hh * cpb + j]
            st_scr[hh, d] = st
        o = o + jnp.einsum('nck,nvk->ncv', qd, jnp.stack(entering), preferred_element_type=F32)
        od_scr[:, d, rows, :] = o.reshape(hp, rb, DV_H)

    def exact_chunk(hh, d, r0):
        reverse = d == 1
        rows = pl.ds(r0, c)
        b = _chunk_scan(g_refs[d][hh, 0, rows, :], reverse)
        b_scr[...] = b
        q_scr[...] = queries(hq_ref[hh, 0, rows, :])
        k_scr[...] = k_refs[d][hh, 0, rows, :].astype(F32)

        def one_row(t, carry):
            bt = b_scr[pl.ds(t, 1), :]
            qt = q_scr[pl.ds(t, 1), :]
            seen = (row >= t) if reverse else (row <= t)
            w = jnp.exp(jnp.where(seen, bt - b_scr[...], -jnp.inf))
            p = jnp.sum(qt * k_scr[...] * w, axis=-1, keepdims=True)
            od_scr[hh, d, pl.ds(r0 + t, 1), :] = jnp.sum(p * hi_ref[hh, 0, rows, :].astype(F32),
                                                         axis=0, keepdims=True)
            return carry

        lax.fori_loop(0, c, one_row, 0)
        b_end = b[0:1] if reverse else b[c - 1:c]
        st = st_scr[hh, d]
        qd = (q_scr[...] * jnp.exp(b)).astype(BF16)
        od_scr[hh, d, rows, :] = od_scr[hh, d, rows, :] + _nt_dot(qd, st.astype(BF16))
        k_end = (k_scr[...] * jnp.exp(b_end - b)).astype(BF16)
        st_scr[hh, d] = jnp.exp(b_end) * st + _tn_dot(hi_ref[hh, 0, rows, :], k_end)

    def block(i, carry):
        starts = (pl.multiple_of(i * rb, rb), pl.multiple_of((nb - 1 - i) * rb, rb))
        mild = (ok_scr[0, i] + ok_scr[1, nb - 1 - i]) == 2

        @pl.when(mild)
        def _():
            for d in range(2):
                fast_block(d, starts[d])

        @pl.when(jnp.logical_not(mild))
        def _():
            for hh in range(hp):
                for d in range(2):
                    for j in (range(cpb - 1, -1, -1) if d == 1 else range(cpb)):
                        exact_chunk(hh, d, pl.multiple_of(starts[d] + j * c, c))

        return carry

    lax.fori_loop(0, nb, block, 0)

    def finish(i, carry):
        rows = pl.ds(pl.multiple_of(i * rb, rb), rb)
        for hh in range(hp):
            o = od_scr[hh, 0, rows, :] + od_scr[hh, 1, rows, :]
            o = o * lax.rsqrt(jnp.mean(o * o, axis=-1, keepdims=True) + RMS_EPS) * ng_ref[...]
            hg = hg_ref[hh, 0, rows, :].astype(F32)
            o_ref[hh, 0, rows, :] = (o * (hg * jax.nn.sigmoid(hg))).astype(o_ref.dtype)
        return carry

    lax.fori_loop(0, nb, finish, 0)
    if want_state:
        for hh in range(hp):
            for d in range(2):
                sfin_ref[0, d, hh] = st_scr[hh, d].T


def _hgrn(plain, g, kd, norm_g, s0, want_state):
    _, bsz, t, _ = plain.shape
    hp = HGRN_SHORT_HEADS if t <= HGRN_ROWS else 1
    stream = lambda s: pl.BlockSpec((hp, 1, t, DK_H), lambda b, h: (s * N_HEADS_H // hp + h, b, 0, 0))
    in_specs = [stream(0), stream(1), stream(2), stream(0), stream(1), stream(0), stream(1),
                pl.BlockSpec((1, DV_H), lambda b, h: (0, 0))]
    args = [plain, plain, plain, g, g, kd, kd, norm_g]
    state_spec = pl.BlockSpec((1, 2, hp, DK_H, DV_H), lambda b, h: (b, 0, h, 0, 0))
    if s0 is not None:
        in_specs.append(state_spec)
        args.append(s0)
    out_specs = [pl.BlockSpec((hp, 1, t, DV_H), lambda b, h: (h, b, 0, 0))]
    out_shape = [jax.ShapeDtypeStruct((N_HEADS_H, bsz, t, DV_H), BF16)]
    if want_state:
        out_specs.append(state_spec)
        out_shape.append(jax.ShapeDtypeStruct((bsz, 2, N_HEADS_H, DK_H, DV_H), F32))
    n_blocks = t // min(HGRN_ROWS, t)
    chunk = pltpu.VMEM((HGRN_C, DK_H), F32)
    outs = pl.pallas_call(
        functools.partial(_hgrn_kernel, seq_len=t, has_s0=s0 is not None, want_state=want_state),
        grid=(bsz, N_HEADS_H // hp),
        in_specs=in_specs,
        out_specs=out_specs,
        out_shape=out_shape,
        scratch_shapes=[
            pltpu.VMEM((hp, 2, t, DV_H), F32),
            pltpu.VMEM((hp, 2, DV_H, DK_H), F32),
            chunk, chunk, chunk,
            pltpu.SMEM((2, n_blocks), jnp.int32),
        ],
        compiler_params=_params(("parallel", "parallel")),
        name="hgrn",
    )(*args)
    return outs if want_state else (outs[0], None)


def _merge_ffn2_kernel(oa0_ref, oh0_ref, oa1_ref, oh1_ref, h_ref, x_ref, mod_ref,
                       wmix_hbm, wa_hbm, wh_hbm, wo_hbm, win_hbm, wout_hbm, lng_ref, lnb_ref, y0_ref, y1_ref,
                       wg_scr, wa_scr, wh_scr, wo_scr,
                       win_scr, wout_scr, wide_stage, square_stage, wide_sems, square_sems, act_scr, *, tiles_0):
    i = pl.program_id(0)
    first_group = i < tiles_0

    @pl.when(i == 0)
    def _():
        _load_weight(win_hbm, win_scr, wide_stage, wide_sems)
        for half in range(2):
            _load_weight(wmix_hbm, wg_scr, square_stage, square_sems,
                         src_col=W_GA0 * MIX_TN + half * D_MODEL, dst_col=half * D_MODEL)
        for w_hbm, w_scr in ((wa_hbm, wa_scr), (wh_hbm, wh_scr), (wo_hbm, wo_scr), (wout_hbm, wout_scr)):
            _load_weight(w_hbm, w_scr, square_stage, square_sems)

    def heads(ref0, ref1):
        pick = lambda hh: jnp.where(first_group, ref0[hh], ref1[hh])
        return jnp.concatenate([pick(hh) for hh in range(ref0.shape[0])], axis=-1)

    gates = jax.nn.sigmoid(jnp.dot(h_ref[...], wg_scr[...], preferred_element_type=F32))
    ba = jnp.dot(heads(oa0_ref, oa1_ref), wa_scr[...], preferred_element_type=F32)
    bh = jnp.dot(heads(oh0_ref, oh1_ref), wh_scr[...], preferred_element_type=F32)
    merged = gates[:, :D_MODEL] * ba + gates[:, D_MODEL:] * bh
    mix = jnp.dot(merged.astype(BF16), wo_scr[...], preferred_element_type=F32)
    x2 = _layer_norm(ALPHA * x_ref[...] + mod_ref[0, 5:6, :] * mix, lng_ref[1:2, :], lnb_ref[1:2, :])
    y = _half_step(x2, mod_ref, win_scr, wout_scr, lng_ref, lnb_ref, act_scr, 2)

    @pl.when(first_group)
    def _():
        y0_ref[...] = y

    @pl.when(jnp.logical_not(first_group))
    def _():
        y1_ref[...] = y


def _merge_ffn2(mixed, h, x, mod, cond_of_tile, w_mix, w_a, w_h, w_o, w_in, w_out, ln_g, ln_b):
    (oa0, oh0), (oa1, oh1) = mixed
    tiles_0, tiles_1 = oa0.shape[1] // FFN_TM, oa1.shape[1] // FFN_TM
    row = lambda i: (i, 0)
    const = lambda i: (0, 0)
    in_group_0 = lambda i: jnp.minimum(i, tiles_0 - 1)
    in_group_1 = lambda i: jnp.maximum(i - tiles_0, 0)
    hbm = pl.BlockSpec(memory_space=pl.ANY)
    return pl.pallas_call(
        functools.partial(_merge_ffn2_kernel, tiles_0=tiles_0),
        grid=(tiles_0 + tiles_1,),
        in_specs=[
            pl.BlockSpec((N_HEADS_A, FFN_TM, DV_A), lambda i: (0, in_group_0(i), 0)),
            pl.BlockSpec((N_HEADS_H, FFN_TM, DV_H), lambda i: (0, in_group_0(i), 0)),
            pl.BlockSpec((N_HEADS_A, FFN_TM, DV_A), lambda i: (0, in_group_1(i), 0)),
            pl.BlockSpec((N_HEADS_H, FFN_TM, DV_H), lambda i: (0, in_group_1(i), 0)),
            pl.BlockSpec((FFN_TM, D_MODEL), row),
            pl.BlockSpec((FFN_TM, D_MODEL), row),
            pl.BlockSpec((1, 9, D_MODEL), lambda i: (cond_of_tile(i, FFN_TM), 0, 0)),
            hbm, hbm, hbm, hbm, hbm, hbm,
            pl.BlockSpec((3, D_MODEL), const),
            pl.BlockSpec((3, D_MODEL), const),
        ],
        out_specs=[pl.BlockSpec((FFN_TM, D_MODEL), lambda i: (in_group_0(i), 0)),
                   pl.BlockSpec((FFN_TM, D_MODEL), lambda i: (in_group_1(i), 0))],
        out_shape=[jax.ShapeDtypeStruct((tiles_0 * FFN_TM, D_MODEL), F32),
                   jax.ShapeDtypeStruct((tiles_1 * FFN_TM, D_MODEL), F32)],
        scratch_shapes=[pltpu.VMEM((D_MODEL, 2 * D_MODEL), BF16), pltpu.VMEM((W_A, D_MODEL), BF16),
                        pltpu.VMEM((W_H, D_MODEL), BF16), pltpu.VMEM((D_MODEL, D_MODEL), BF16)]
                       + _ffn_weight_scratch() + [pltpu.VMEM((FFN_TM, D_FF), BF16)],
        compiler_params=_params(("arbitrary",)),
        name="merge_ffn2",
    )(oa0, oh0, oa1, oh1, h, x, mod, w_mix, w_a, w_h, w_o, w_in, w_out, ln_g, ln_b)


def _rope_tables(n_tok):
    rows = n_tok // GRID_W
    row = np.repeat(np.arange(rows, dtype=np.float32), GRID_W)
    col = np.tile(np.arange(GRID_W, dtype=np.float32), rows)
    half = DH_A // 2
    inv = (ROPE_BASE ** (-np.arange(0, half, 2, dtype=np.float32) / half)).astype(np.float32)
    ar = row[:, None] * inv
    ac = col[:, None] * inv
    ang = np.concatenate([ar, ar, ac, ac] * 2, axis=-1)
    cos, sin = np.cos(ang).astype(np.float32), np.sin(ang).astype(np.float32)
    scale = np.float32(Q_SCALE)
    return (jnp.asarray(np.stack([cos * scale, cos, np.ones_like(cos)])),
            jnp.asarray(np.stack([sin * scale, sin, np.zeros_like(sin)])))


def _mixer(h1, group, wts, lam_init, bsz, ctx):
    m = MIX_TM
    t = m // bsz
    latent = ctx is not None
    per_head = lambda a: a.reshape(a.shape[0], bsz, t, a.shape[-1])
    if latent:
        qkv, plain, g, kd = _project(h1, group, wts['w_mix_in'], wts['lb_logits'], _rope_tables(t), False)
        k = v = None
        ctx_kv = (ctx[0], ctx[1])
    else:
        qkv, kv32, plain, g, kd = _project(h1, group, wts['w_mix_in'], wts['lb_logits'], None, True)
        k, v = kv32[W_K], kv32[W_V]
        ctx_kv = None
    qkv = per_head(qkv)
    q_kv = [(qkv, W_Q * N_HEADS_A), (qkv, W_K * N_HEADS_A), (qkv, W_V * N_HEADS_A)]
    o_a = _attention(*q_kv, ctx_kv, wts['lam_params'], wts['attn_subln_g'], lam_init)
    o_h, s_fin = _hgrn(per_head(plain), per_head(g), per_head(kd), wts['hgrn_norm_g'],
                       ctx[2] if latent else None, not latent)
    return (o_a.reshape(N_HEADS_A, m, DV_A), o_h.reshape(N_HEADS_H, m, DV_H)), (k, v, s_fin)


def kernel(x_prompt, x_sample, cache_k, cache_v, state_hgrn, c, c_ctx, w_ada, b_ada, ffn1_w_in, ffn1_w_out,
           w_mix_in, lambda_q1, lambda_k1, lambda_q2, lambda_k2, attn_subln_g, hgrn_lb_logits, hgrn_norm_g,
           w_branch_a, w_branch_h, w_mix_out, ffn2_w_in, ffn2_w_out, ln_g, ln_b):
    batch, seq, _ = x_prompt.shape
    dec_batch, dec_seq, _ = x_sample.shape
    past = cache_k.shape[2]
    assert DEPTH == 1 and dec_batch + 1 <= MOD_ROWS
    lam_init = 0.8 - 0.6 * math.exp(-0.3 * 0)

    cond = jnp.concatenate([c_ctx[None, :], c, jnp.zeros((MOD_ROWS - 1 - dec_batch, D_MODEL), F32)], axis=0)
    mod = _modulation(cond, w_ada[0], b_ada[0])

    wts = {
        'w_mix_in': w_mix_in[0],
        'lam_params': jnp.concatenate([lambda_q1, lambda_k1, lambda_q2, lambda_k2], axis=0),
        'attn_subln_g': attn_subln_g, 'lb_logits': hgrn_lb_logits, 'hgrn_norm_g': hgrn_norm_g,
    }
    n_ctx = batch * seq
    assert n_ctx == MIX_TM and dec_batch * dec_seq == MIX_TM

    def cond_of_tile(i, tm):
        first = i * tm
        return jnp.where(first < n_ctx, 0, 1 + (first - n_ctx) // dec_seq)

    x1, h1 = _ffn1(x_prompt.reshape(n_ctx, D_MODEL), x_sample.reshape(dec_batch * dec_seq, D_MODEL), mod,
                   cond_of_tile, ffn1_w_in[0], ffn1_w_out[0], ln_g[0], ln_b[0])
    mixed_p, (k_c, v_c, s_c) = _mixer(h1, 0, wts, lam_init, batch, None)
    ctx = (cache_k[:, 0].reshape(dec_batch, past, QK_A), cache_v[:, 0].reshape(dec_batch, past, W_A),
           state_hgrn[:, 0])
    mixed_s, _ = _mixer(h1, 1, wts, lam_init, dec_batch, ctx)
    y_p, y_s = _merge_ffn2((mixed_p, mixed_s), h1, x1, mod, cond_of_tile, w_mix_in[0],
                           w_branch_a[0], w_branch_h[0], w_mix_out[0], ffn2_w_in[0], ffn2_w_out[0],
                           ln_g[0], ln_b[0])

    return (y_p.reshape(batch, seq, D_MODEL),
            y_s.reshape(dec_batch, dec_seq, D_MODEL),
            k_c.reshape(batch, DEPTH, seq, N_HEADS_A, 2, DH_A),
            v_c.reshape(batch, DEPTH, seq, N_HEADS_A, DV_A),
            s_c.reshape(batch, DEPTH, 2, N_HEADS_H, DK_H, DV_H))
```

```python
import functools
import math

import numpy as np
import jax
import jax.numpy as jnp
from jax import lax
from jax.experimental import pallas as pl
from jax.experimental.pallas import tpu as pltpu

D_MODEL = 1024
DEPTH = 1
GRID_W = 64
N_HEADS_A = 4
DH_A = 64
DV_A = 2 * DH_A
QK_A = N_HEADS_A * 2 * DH_A
W_A = N_HEADS_A * DV_A
ROPE_BASE = 10000.0
N_HEADS_H = 4
DK_H = 128
DV_H = 128
QK_H = N_HEADS_H * DK_H
W_H = N_HEADS_H * DV_H
D_FF = 2816
MIX_IN = 2 * QK_A + W_A + 3 * QK_H + 2 * W_H + 2 * D_MODEL
ALPHA = (2 * DEPTH) ** 0.25
LN_EPS = 1e-5
RMS_EPS = 1e-6
Q_SCALE = DH_A ** -0.5 * math.log2(math.e)

F32 = jnp.float32
BF16 = jnp.bfloat16

LANES = 128
VMEM_LIMIT = 56 * 1024 * 1024

FFN_TM = 512
FFN_TF = 256
WIDE_CHUNK_ROWS = 128
SQUARE_CHUNK_ROWS = 256
MIX_TM = 4096
PROJ_TM = 2048
PROJ_SUB = 512
MIX_TN = 512
ATT_TQ = 1024
ATT_SUB = 512
ATT_SHORT_SEQ = 256
HGRN_C = 64
HGRN_ROWS = 512
HGRN_SHORT_HEADS = 4
HGRN_SAFE_DECAY = 75.0
MOD_ROWS = 8

W_Q, W_K, W_V, W_HQ, W_HFF, W_HFB, W_HI, W_HG, W_GA0, W_GA1, W_GH0, W_GH1 = range(12)


def _params(sem):
    return pltpu.CompilerParams(dimension_semantics=sem, vmem_limit_bytes=VMEM_LIMIT)


def _nt_dot(a, b):
    return lax.dot_general(a, b, (((1,), (1,)), ((), ())), preferred_element_type=F32)


def _tn_dot(a, b):
    return lax.dot_general(a, b, (((0,), (0,)), ((), ())), preferred_element_type=F32)


def _layer_norm(y, g, b):
    mu = jnp.mean(y, axis=-1, keepdims=True)
    yc = y - mu
    var = jnp.mean(yc * yc, axis=-1, keepdims=True)
    return yc * lax.rsqrt(var + LN_EPS) * g + b


def _mod_kernel(c_ref, w_ref, b_ref, o_ref):
    c = c_ref[...]
    a = (c * jax.nn.sigmoid(c)).astype(BF16)
    o_ref[...] = jnp.dot(a, w_ref[...].astype(BF16), preferred_element_type=F32) + b_ref[...]


def _modulation(cond, w_ada, b_ada):
    n_out = w_ada.shape[1]
    tn = D_MODEL
    out = pl.pallas_call(
        _mod_kernel,
        grid=(n_out // tn,),
        in_specs=[
            pl.BlockSpec((MOD_ROWS, D_MODEL), lambda n: (0, 0)),
            pl.BlockSpec((D_MODEL, tn), lambda n: (0, n)),
            pl.BlockSpec((1, tn), lambda n: (0, n)),
        ],
        out_specs=pl.BlockSpec((MOD_ROWS, tn), lambda n: (0, n)),
        out_shape=jax.ShapeDtypeStruct((MOD_ROWS, n_out), F32),
        compiler_params=_params(("arbitrary",)),
        name="modulation",
    )(cond, w_ada, b_ada.reshape(1, n_out))
    return out.reshape(MOD_ROWS, 9, D_MODEL)


def _load_weight(w_hbm, w_scr, stage, sems, src_col=0, dst_col=0):
    _, chunk, width = stage.shape
    n_chunks = w_hbm.shape[0] // chunk

    def copy(c):
        return pltpu.make_async_copy(w_hbm.at[pl.ds(c * chunk, chunk), pl.ds(src_col, width)],
                                     stage.at[c % 2], sems.at[c % 2])

    copy(0).start()
    for c in range(n_chunks):
        if c + 1 < n_chunks:
            copy(c + 1).start()
        copy(c).wait()
        w_scr[c * chunk:(c + 1) * chunk, dst_col:dst_col + width] = stage[c % 2].astype(BF16)


def _modulate(x, mod_ref, sub):
    return x * (1.0 + mod_ref[0, 3 * sub + 1:3 * sub + 2, :]) + mod_ref[0, 3 * sub:3 * sub + 1, :]


def _half_step(x, mod_ref, win_ref, wout_ref, lng_ref, lnb_ref, act_scr, sub):
    h = _modulate(x, mod_ref, sub).astype(BF16)
    for c in range(D_FF // FFN_TF):
        cols = slice(c * FFN_TF, (c + 1) * FFN_TF)
        up_cols = slice(D_FF + c * FFN_TF, D_FF + (c + 1) * FFN_TF)
        gate = jnp.dot(h, win_ref[:, cols], preferred_element_type=F32)
        up = jnp.dot(h, win_ref[:, up_cols], preferred_element_type=F32)
        act_scr[:, cols] = (gate * jax.nn.sigmoid(gate) * up).astype(BF16)
    f = jnp.dot(act_scr[...], wout_ref[...], preferred_element_type=F32)
    g = mod_ref[0, 3 * sub + 2:3 * sub + 3, :]
    return _layer_norm(ALPHA * x + 0.5 * g * f, lng_ref[sub:sub + 1, :], lnb_ref[sub:sub + 1, :])


def _ffn_weight_scratch():
    return [
        pltpu.VMEM((D_MODEL, 2 * D_FF), BF16),
        pltpu.VMEM((D_FF, D_MODEL), BF16),
        pltpu.VMEM((2, WIDE_CHUNK_ROWS, 2 * D_FF), F32),
        pltpu.VMEM((2, SQUARE_CHUNK_ROWS, D_MODEL), F32),
        pltpu.SemaphoreType.DMA((2,)),
        pltpu.SemaphoreType.DMA((2,)),
    ]


def _ffn1_kernel(xa_ref, xb_ref, mod_ref, win_hbm, wout_hbm, lng_ref, lnb_ref, xo_ref, ho_ref,
                 win_scr, wout_scr, wide_stage, square_stage, wide_sems, square_sems, act_scr, *, tiles_a):
    i = pl.program_id(0)

    @pl.when(i == 0)
    def _():
        _load_weight(win_hbm, win_scr, wide_stage, wide_sems)
        _load_weight(wout_hbm, wout_scr, square_stage, square_sems)

    x = jnp.where(i < tiles_a, xa_ref[...], xb_ref[...])
    xn = _half_step(x, mod_ref, win_scr, wout_scr, lng_ref, lnb_ref, act_scr, 0)
    xo_ref[...] = xn
    ho_ref[...] = _modulate(xn, mod_ref, 1).astype(BF16)


def _ffn1(xa, xb, mod, cond_of_tile, w_in, w_out, ln_g, ln_b):
    tiles_a, tiles_b = xa.shape[0] // FFN_TM, xb.shape[0] // FFN_TM
    m = xa.shape[0] + xb.shape[0]
    row = lambda i: (i, 0)
    const = lambda i: (0, 0)
    hbm = pl.BlockSpec(memory_space=pl.ANY)
    return pl.pallas_call(
        functools.partial(_ffn1_kernel, tiles_a=tiles_a),
        grid=(tiles_a + tiles_b,),
        in_specs=[
            pl.BlockSpec((FFN_TM, D_MODEL), lambda i: (jnp.minimum(i, tiles_a - 1), 0)),
            pl.BlockSpec((FFN_TM, D_MODEL), lambda i: (jnp.maximum(i - tiles_a, 0), 0)),
            pl.BlockSpec((1, 9, D_MODEL), lambda i: (cond_of_tile(i, FFN_TM), 0, 0)),
            hbm, hbm,
            pl.BlockSpec((3, D_MODEL), const),
            pl.BlockSpec((3, D_MODEL), const),
        ],
        out_specs=[pl.BlockSpec((FFN_TM, D_MODEL), row), pl.BlockSpec((FFN_TM, D_MODEL), row)],
        out_shape=[jax.ShapeDtypeStruct((m, D_MODEL), F32), jax.ShapeDtypeStruct((m, D_MODEL), BF16)],
        scratch_shapes=_ffn_weight_scratch() + [pltpu.VMEM((FFN_TM, D_FF), BF16)],
        compiler_params=_params(("arbitrary",)),
        name="ffn1",
    )(xa, xb, mod, w_in, w_out, ln_g, ln_b)


def _rope(x, cos, sin):
    lane = lax.broadcasted_iota(jnp.int32, (1, LANES), 1)
    first_half = (lane % 32) < 16
    outs = []
    for hb in range(QK_A // LANES):
        xb = x[:, hb * LANES:(hb + 1) * LANES]
        ahead = pltpu.roll(xb, LANES - 16, 1)
        behind = pltpu.roll(xb, 16, 1)
        rot = jnp.where(first_half, -ahead, behind)
        outs.append(xb * cos + rot * sin)
    return jnp.concatenate(outs, axis=-1)


def _forget_gate(z, logits):
    mx = jnp.max(logits, axis=0, keepdims=True)
    el = jnp.exp(logits - mx)
    lb = el[0:1] / jnp.sum(el, axis=0, keepdims=True)
    sig = 0.5 + 0.5 * jnp.tanh(0.5 * z)
    u = (1.0 - lb) * sig
    return jnp.log(lb + u), (1.0 - lb) - u


def _proj_kernel(*refs, rope, want_kv32):
    refs = list(refs)
    h_ref, w_ref, lbl_ref = refs[:3]
    refs = refs[3:]
    cos_ref, sin_ref = (refs.pop(0), refs.pop(0)) if rope else (None, None)
    qkv_ref = refs.pop(0)
    kv32_ref = refs.pop(0) if want_kv32 else None
    plain_ref, g_ref, kd_ref = refs
    n = pl.program_id(1)

    pieces = [slice(r, r + PROJ_SUB) for r in range(0, h_ref.shape[0], PROJ_SUB)]

    def project(rows, w):
        return jnp.dot(h_ref[rows, :], w, preferred_element_type=F32)

    def store_heads(ref, rows, y):
        for hh in range(MIX_TN // LANES):
            ref[hh, rows, :] = y[:, hh * LANES:(hh + 1) * LANES].astype(ref.dtype)

    @pl.when(n <= W_V)
    def _():
        w = w_ref[...].astype(BF16)
        for rows in pieces:
            y = project(rows, w)
            if want_kv32:
                kv32_ref[0, rows, :] = y
            if rope:
                first = rows.start % cos_ref.shape[1]
                table_rows = slice(first, first + PROJ_SUB)
                y = _rope(y, cos_ref[0, table_rows, :], sin_ref[0, table_rows, :])
            else:
                y = y * jnp.where(n == W_Q, Q_SCALE, 1.0)
            store_heads(qkv_ref, rows, y)

    @pl.when((n == W_HFF) | (n == W_HFB))
    def _():
        w = w_ref[...].astype(BF16)
        for rows in pieces:
            log_f, kd = _forget_gate(project(rows, w), lbl_ref[0])
            store_heads(g_ref, rows, log_f)
            store_heads(kd_ref, rows, kd)

    @pl.when((n == W_HQ) | (n >= W_HI))
    def _():
        w = w_ref[...].astype(BF16)
        for rows in pieces:
            store_heads(plain_ref, rows, project(rows, w))


def _project(h, group, w, lb_logits, rope_tables, want_kv32):
    m = MIX_TM
    tm = PROJ_TM
    tiles = m // tm
    rope = rope_tables is not None
    heads = MIX_TN // LANES
    n_blocks = W_HG + 1
    in_specs = [pl.BlockSpec((tm, D_MODEL), lambda i, n: (group * tiles + i, 0)),
                pl.BlockSpec((D_MODEL, MIX_TN), lambda i, n: (0, n)),
                pl.BlockSpec((1, DEPTH + 1, QK_H), lambda i, n: (jnp.clip(n - W_HFF, 0, 1), 0, 0))]
    args = [h, w, lb_logits]
    qkv_col = lambda n: jnp.minimum(n, W_V)
    if rope:
        t_rope = rope_tables[0].shape[1]
        rope_rows = min(t_rope, tm)
        n_rope_tiles = t_rope // rope_rows
        for t in rope_tables:
            in_specs.append(pl.BlockSpec((1, rope_rows, DV_A), lambda i, n: (qkv_col(n), i % n_rope_tiles, 0)))
            args.append(t)
    head_block = (heads, tm, LANES)
    plain_col = lambda n: jnp.where(n < W_HI, 0, n - W_HI + 1)
    gate_col = lambda n: jnp.clip(n - W_HFF, 0, 1)
    out_specs = [pl.BlockSpec(head_block, lambda i, n: (qkv_col(n), i, 0))]
    out_shape = [jax.ShapeDtypeStruct((3 * heads, m, LANES), BF16)]
    if want_kv32:
        out_specs.append(pl.BlockSpec((1, tm, MIX_TN), lambda i, n: (jnp.clip(n - W_K, 0, 1), i, 0)))
        out_shape.append(jax.ShapeDtypeStruct((2, m, MIX_TN), F32))
    out_specs += [pl.BlockSpec(head_block, lambda i, n: (plain_col(n), i, 0)),
                  pl.BlockSpec(head_block, lambda i, n: (gate_col(n), i, 0)),
                  pl.BlockSpec(head_block, lambda i, n: (gate_col(n), i, 0))]
    out_shape += [jax.ShapeDtypeStruct((3 * heads, m, LANES), BF16),
                  jax.ShapeDtypeStruct((2 * heads, m, LANES), F32),
                  jax.ShapeDtypeStruct((2 * heads, m, LANES), BF16)]
    return pl.pallas_call(
        functools.partial(_proj_kernel, rope=rope, want_kv32=want_kv32),
        grid=(tiles, n_blocks),
        in_specs=in_specs,
        out_specs=out_specs,
        out_shape=out_shape,
        compiler_params=_params(("arbitrary", "arbitrary")),
        name="mix_proj",
    )(*args)


def _attn_kernel(*refs, lam_init, has_ctx):
    if has_ctx:
        lam_ref, g_ref, q_ref, k_ref, v_ref, kc_ref, vc_ref, o_ref = refs
    else:
        lam_ref, g_ref, q_ref, k_ref, v_ref, o_ref = refs
    lp = lam_ref[...]
    lam = (jnp.exp(jnp.sum(lp[0:1] * lp[1:2], axis=-1, keepdims=True))
           - jnp.exp(jnp.sum(lp[2:3] * lp[3:4], axis=-1, keepdims=True)) + lam_init)

    lane = lax.broadcasted_iota(jnp.int32, (1, DV_A), 1)
    comp0 = lane < DH_A
    tq = q_ref.shape[2]
    sub = min(ATT_SUB, tq)
    problems = []
    for hh in range(q_ref.shape[0]):
        keys = [k_ref[hh, 0]]
        vals = [v_ref[hh, 0]]
        if has_ctx:
            keys.append(kc_ref[0, :, hh * DV_A:(hh + 1) * DV_A].astype(BF16))
            vals.append(vc_ref[0, :, hh * DV_A:(hh + 1) * DV_A].astype(BF16))
        for r in range(tq // sub):
            q = q_ref[hh, 0, r * sub:(r + 1) * sub, :]
            zero = jnp.zeros_like(q)
            for qc in (jnp.where(comp0, q, zero), jnp.where(comp0, zero, q)):
                problems.append((qc, keys, vals))
    scores = [[_nt_dot(qc, kk) for kk in keys] for qc, keys, _ in problems]
    weights = []
    for ss in scores:
        mx = functools.reduce(jnp.maximum, [jnp.max(s, axis=-1, keepdims=True) for s in ss])
        es = [jnp.exp2(s - mx) for s in ss]
        den = functools.reduce(jnp.add, [jnp.sum(e, axis=-1, keepdims=True) for e in es])
        weights.append(([e.astype(BF16) for e in es], 1.0 / den))
    maps = [functools.reduce(jnp.add, [jnp.dot(e, vv, preferred_element_type=F32) for e, vv in zip(es, vals)]) * r
            for (es, r), (_, _, vals) in zip(weights, problems)]
    idx = 0
    for hh in range(q_ref.shape[0]):
        for r in range(tq // sub):
            o = maps[idx] - lam * maps[idx + 1]
            idx += 2
            o = o * lax.rsqrt(jnp.mean(o * o, axis=-1, keepdims=True) + RMS_EPS) * g_ref[...]
            o_ref[hh, 0, r * sub:(r + 1) * sub, :] = (o * (1.0 - lam_init)).astype(o_ref.dtype)


def _attention(q, k, v, ctx_kv, lam_params, subln_g, lam_init):
    (q_arr, q_row), (k_arr, k_row), (v_arr, v_row) = q, k, v
    _, bsz, t, _ = q_arr.shape
    tq = min(ATT_TQ, t)
    hp = N_HEADS_A if t <= ATT_SHORT_SEQ else 1
    const = lambda b, h, i: (0, 0)
    rows = lambda first: (lambda b, h, i: (first // hp + h, b, 0, 0))
    in_specs = [
        pl.BlockSpec((4, DH_A), const),
        pl.BlockSpec((1, DV_A), const),
        pl.BlockSpec((hp, 1, tq, DV_A), lambda b, h, i: (q_row // hp + h, b, i, 0)),
        pl.BlockSpec((hp, 1, t, DV_A), rows(k_row)),
        pl.BlockSpec((hp, 1, t, DV_A), rows(v_row)),
    ]
    args = [lam_params, subln_g, q_arr, k_arr, v_arr]
    if ctx_kv is not None:
        p = ctx_kv[0].shape[1]
        head_all = lambda b, h, i: (b, 0, h)
        in_specs += [pl.BlockSpec((1, p, hp * DV_A), head_all), pl.BlockSpec((1, p, hp * DV_A), head_all)]
        args += list(ctx_kv)
    return pl.pallas_call(
        functools.partial(_attn_kernel, lam_init=lam_init, has_ctx=ctx_kv is not None),
        grid=(bsz, N_HEADS_A // hp, t // tq),
        in_specs=in_specs,
        out_specs=pl.BlockSpec((hp, 1, tq, DV_A), lambda b, h, i: (h, b, i, 0)),
        out_shape=jax.ShapeDtypeStruct((N_HEADS_A, bsz, t, DV_A), BF16),
        compiler_params=_params(("parallel", "parallel", "arbitrary")),
        name="attention",
    )(*args)


def _chunk_scan(g, reverse):
    r = g.shape[0]
    pos = lax.broadcasted_iota(jnp.int32, (r, 1), 0) % HGRN_C
    b = g
    sh = 1
    while sh < HGRN_C:
        if reverse:
            b = b + jnp.where(pos < HGRN_C - sh, pltpu.roll(b, r - sh, 0), 0.0)
        else:
            b = b + jnp.where(pos >= sh, pltpu.roll(b, sh, 0), 0.0)
        sh *= 2
    return b


def _hgrn_kernel(*refs, seq_len, has_s0, want_state):
    refs = list(refs)
    hq_ref, hi_ref, hg_ref, gf_ref, gb_ref, kf_ref, kb_ref, ng_ref = refs[:8]
    refs = refs[8:]
    s0_ref = refs.pop(0) if has_s0 else None
    o_ref = refs.pop(0)
    sfin_ref = refs.pop(0) if want_state else None
    od_scr, st_scr, b_scr, q_scr, k_scr, ok_scr = refs
    hp = hq_ref.shape[0]
    c = HGRN_C
    rb = min(HGRN_ROWS, seq_len)
    nb = seq_len // rb
    cpb = rb // c
    g_refs = (gf_ref, gb_ref)
    k_refs = (kf_ref, kb_ref)

    def queries(hq):
        hq = hq.astype(F32)
        return hq * jax.nn.sigmoid(hq)

    def flags(i, carry):
        rows = pl.ds(pl.multiple_of(i * rb, rb), rb)
        for d in range(2):
            worst = jnp.zeros((1, DK_H), F32)
            for hh in range(hp):
                g = g_refs[d][hh, 0, rows, :]
                for j in range(cpb):
                    worst = jnp.maximum(worst, -jnp.sum(g[j * c:(j + 1) * c], axis=0, keepdims=True))
            ok_scr[d, i] = (jnp.max(worst) <= HGRN_SAFE_DECAY).astype(jnp.int32)
        return carry

    lax.fori_loop(0, nb, flags, 0)

    for hh in range(hp):
        for d in range(2):
            if has_s0:
                st_scr[hh, d] = s0_ref[0, d, hh].T
            else:
                st_scr[hh, d] = jnp.zeros((DV_H, DK_H), F32)

    row = lax.broadcasted_iota(jnp.int32, (c, 1), 0)
    col = lax.broadcasted_iota(jnp.int32, (1, c), 1)

    def fast_block(d, r0):
        reverse = d == 1
        rows = pl.ds(r0, rb)
        n = hp * cpb
        flat = lambda a: a.reshape(hp * rb, a.shape[-1])
        chunks = lambda a: a.reshape(n, c, a.shape[-1])
        vb = chunks(hi_ref[:, 0, rows, :])
        b = _chunk_scan(flat(g_refs[d][:, 0, rows, :]), reverse)
        qd = chunks((queries(flat(hq_ref[:, 0, rows, :])) * jnp.exp(b)).astype(BF16))
        ku = chunks(flat(k_refs[d][:, 0, rows, :]).astype(F32) * jnp.exp(-b))
        b = chunks(b)
        e_end = jnp.exp(b[:, 0:1, :] if reverse else b[:, c - 1:c, :])
        causal = ((row <= col) if reverse else (row >= col))[None]
        a = jnp.einsum('nck,nsk->ncs', qd, ku.astype(BF16), preferred_element_type=F32)
        o = jnp.einsum('ncs,nsv->ncv', jnp.where(causal, a, 0.0).astype(BF16), vb, preferred_element_type=F32)
        u = jnp.einsum('nsv,nsk->nvk', vb, (ku * e_end).astype(BF16), preferred_element_type=F32)
        entering = [None] * n
        for hh in range(hp):
            st = st_scr[hh, d]
            for j in (range(cpb - 1, -1, -1) if reverse else range(cpb)):
                entering[hh * cpb + j] = st.astype(BF16)
                st = e_end[hh * cpb + j] * st + u[hh * cpb + j]
            st_scr[hh, d] = st
        o = o + jnp.einsum('nck,nvk->ncv', qd, jnp.stack(entering), preferred_element_type=F32)
        od_scr[:, d, rows, :] = o.reshape(hp, rb, DV_H)

    def exact_chunk(hh, d, r0):
        reverse = d == 1
        rows = pl.ds(r0, c)
        b = _chunk_scan(g_refs[d][hh, 0, rows, :], reverse)
        b_scr[...] = b
        q_scr[...] = queries(hq_ref[hh, 0, rows, :])
        k_scr[...] = k_refs[d][hh, 0, rows, :].astype(F32)

        def one_row(t, carry):
            bt = b_scr[pl.ds(t, 1), :]
            qt = q_scr[pl.ds(t, 1), :]
            seen = (row >= t) if reverse else (row <= t)
            w = jnp.exp(jnp.where(seen, bt - b_scr[...], -jnp.inf))
            p = jnp.sum(qt * k_scr[...] * w, axis=-1, keepdims=True)
            od_scr[hh, d, pl.ds(r0 + t, 1), :] = jnp.sum(p * hi_ref[hh, 0, rows, :].astype(F32),
                                                         axis=0, keepdims=True)
            return carry

        lax.fori_loop(0, c, one_row, 0)
        b_end = b[0:1] if reverse else b[c - 1:c]
        st = st_scr[hh, d]
        qd = (q_scr[...] * jnp.exp(b)).astype(BF16)
        od_scr[hh, d, rows, :] = od_scr[hh, d, rows, :] + _nt_dot(qd, st.astype(BF16))
        k_end = (k_scr[...] * jnp.exp(b_end - b)).astype(BF16)
        st_scr[hh, d] = jnp.exp(b_end) * st + _tn_dot(hi_ref[hh, 0, rows, :], k_end)

    def block(i, carry):
        starts = (pl.multiple_of(i * rb, rb), pl.multiple_of((nb - 1 - i) * rb, rb))
        mild = (ok_scr[0, i] + ok_scr[1, nb - 1 - i]) == 2

        @pl.when(mild)
        def _():
            for d in range(2):
                fast_block(d, starts[d])

        @pl.when(jnp.logical_not(mild))
        def _():
            for hh in range(hp):
                for d in range(2):
                    for j in (range(cpb - 1, -1, -1) if d == 1 else range(cpb)):
                        exact_chunk(hh, d, pl.multiple_of(starts[d] + j * c, c))

        return carry

    lax.fori_loop(0, nb, block, 0)

    def finish(i, carry):
        rows = pl.ds(pl.multiple_of(i * rb, rb), rb)
        for hh in range(hp):
            o = od_scr[hh, 0, rows, :] + od_scr[hh, 1, rows, :]
            o = o * lax.rsqrt(jnp.mean(o * o, axis=-1, keepdims=True) + RMS_EPS) * ng_ref[...]
            hg = hg_ref[hh, 0, rows, :].astype(F32)
            o_ref[hh, 0, rows, :] = (o * (hg * jax.nn.sigmoid(hg))).astype(o_ref.dtype)
        return carry

    lax.fori_loop(0, nb, finish, 0)
    if want_state:
        for hh in range(hp):
            for d in range(2):
                sfin_ref[0, d, hh] = st_scr[hh, d].T


def _hgrn(plain, g, kd, norm_g, s0, want_state):
    _, bsz, t, _ = plain.shape
    hp = HGRN_SHORT_HEADS if t <= HGRN_ROWS else 1
    stream = lambda s: pl.BlockSpec((hp, 1, t, DK_H), lambda b, h: (s * N_HEADS_H // hp + h, b, 0, 0))
    in_specs = [stream(0), stream(1), stream(2), stream(0), stream(1), stream(0), stream(1),
                pl.BlockSpec((1, DV_H), lambda b, h: (0, 0))]
    args = [plain, plain, plain, g, g, kd, kd, norm_g]
    state_spec = pl.BlockSpec((1, 2, hp, DK_H, DV_H), lambda b, h: (b, 0, h, 0, 0))
    if s0 is not None:
        in_specs.append(state_spec)
        args.append(s0)
    out_specs = [pl.BlockSpec((hp, 1, t, DV_H), lambda b, h: (h, b, 0, 0))]
    out_shape = [jax.ShapeDtypeStruct((N_HEADS_H, bsz, t, DV_H), BF16)]
    if want_state:
        out_specs.append(state_spec)
        out_shape.append(jax.ShapeDtypeStruct((bsz, 2, N_HEADS_H, DK_H, DV_H), F32))
    n_blocks = t // min(HGRN_ROWS, t)
    chunk = pltpu.VMEM((HGRN_C, DK_H), F32)
    outs = pl.pallas_call(
        functools.partial(_hgrn_kernel, seq_len=t, has_s0=s0 is not None, want_state=want_state),
        grid=(bsz, N_HEADS_H // hp),
        in_specs=in_specs,
        out_specs=out_specs,
        out_shape=out_shape,
        scratch_shapes=[
            pltpu.VMEM((hp, 2, t, DV_H), F32),
            pltpu.VMEM((hp, 2, DV_H, DK_H), F32),
            chunk, chunk, chunk,
            pltpu.SMEM((2, n_blocks), jnp.int32),
        ],
        compiler_params=_params(("parallel", "parallel")),
        name="hgrn",
    )(*args)
    return outs if want_state else (outs[0], None)


def _merge_ffn2_kernel(oa0_ref, oh0_ref, oa1_ref, oh1_ref, h_ref, x_ref, mod_ref,
                       wmix_hbm, wa_hbm, wh_hbm, wo_hbm, win_hbm, wout_hbm, lng_ref, lnb_ref, y0_ref, y1_ref,
                       wg_scr, wa_scr, wh_scr, wo_scr,
                       win_scr, wout_scr, wide_stage, square_stage, wide_sems, square_sems, act_scr, *, tiles_0):
    i = pl.program_id(0)
    first_group = i < tiles_0

    @pl.when(i == 0)
    def _():
        _load_weight(win_hbm, win_scr, wide_stage, wide_sems)
        for half in range(2):
            _load_weight(wmix_hbm, wg_scr, square_stage, square_sems,
                         src_col=W_GA0 * MIX_TN + half * D_MODEL, dst_col=half * D_MODEL)
        for w_hbm, w_scr in ((wa_hbm, wa_scr), (wh_hbm, wh_scr), (wo_hbm, wo_scr), (wout_hbm, wout_scr)):
            _load_weight(w_hbm, w_scr, square_stage, square_sems)

    def heads(ref0, ref1):
        pick = lambda hh: jnp.where(first_group, ref0[hh], ref1[hh])
        return jnp.concatenate([pick(hh) for hh in range(ref0.shape[0])], axis=-1)

    gates = jax.nn.sigmoid(jnp.dot(h_ref[...], wg_scr[...], preferred_element_type=F32))
    ba = jnp.dot(heads(oa0_ref, oa1_ref), wa_scr[...], preferred_element_type=F32)
    bh = jnp.dot(heads(oh0_ref, oh1_ref), wh_scr[...], preferred_element_type=F32)
    merged = gates[:, :D_MODEL] * ba + gates[:, D_MODEL:] * bh
    mix = jnp.dot(merged.astype(BF16), wo_scr[...], preferred_element_type=F32)
    x2 = _layer_norm(ALPHA * x_ref[...] + mod_ref[0, 5:6, :] * mix, lng_ref[1:2, :], lnb_ref[1:2, :])
    y = _half_step(x2, mod_ref, win_scr, wout_scr, lng_ref, lnb_ref, act_scr, 2)

    @pl.when(first_group)
    def _():
        y0_ref[...] = y

    @pl.when(jnp.logical_not(first_group))
    def _():
        y1_ref[...] = y


def _merge_ffn2(mixed, h, x, mod, cond_of_tile, w_mix, w_a, w_h, w_o, w_in, w_out, ln_g, ln_b):
    (oa0, oh0), (oa1, oh1) = mixed
    tiles_0, tiles_1 = oa0.shape[1] // FFN_TM, oa1.shape[1] // FFN_TM
    row = lambda i: (i, 0)
    const = lambda i: (0, 0)
    in_group_0 = lambda i: jnp.minimum(i, tiles_0 - 1)
    in_group_1 = lambda i: jnp.maximum(i - tiles_0, 0)
    hbm = pl.BlockSpec(memory_space=pl.ANY)
    return pl.pallas_call(
        functools.partial(_merge_ffn2_kernel, tiles_0=tiles_0),
        grid=(tiles_0 + tiles_1,),
        in_specs=[
            pl.BlockSpec((N_HEADS_A, FFN_TM, DV_A), lambda i: (0, in_group_0(i), 0)),
            pl.BlockSpec((N_HEADS_H, FFN_TM, DV_H), lambda i: (0, in_group_0(i), 0)),
            pl.BlockSpec((N_HEADS_A, FFN_TM, DV_A), lambda i: (0, in_group_1(i), 0)),
            pl.BlockSpec((N_HEADS_H, FFN_TM, DV_H), lambda i: (0, in_group_1(i), 0)),
            pl.BlockSpec((FFN_TM, D_MODEL), row),
            pl.BlockSpec((FFN_TM, D_MODEL), row),
            pl.BlockSpec((1, 9, D_MODEL), lambda i: (cond_of_tile(i, FFN_TM), 0, 0)),
            hbm, hbm, hbm, hbm, hbm, hbm,
            pl.BlockSpec((3, D_MODEL), const),
            pl.BlockSpec((3, D_MODEL), const),
        ],
        out_specs=[pl.BlockSpec((FFN_TM, D_MODEL), lambda i: (in_group_0(i), 0)),
                   pl.BlockSpec((FFN_TM, D_MODEL), lambda i: (in_group_1(i), 0))],
        out_shape=[jax.ShapeDtypeStruct((tiles_0 * FFN_TM, D_MODEL), F32),
                   jax.ShapeDtypeStruct((tiles_1 * FFN_TM, D_MODEL), F32)],
        scratch_shapes=[pltpu.VMEM((D_MODEL, 2 * D_MODEL), BF16), pltpu.VMEM((W_A, D_MODEL), BF16),
                        pltpu.VMEM((W_H, D_MODEL), BF16), pltpu.VMEM((D_MODEL, D_MODEL), BF16)]
                       + _ffn_weight_scratch() + [pltpu.VMEM((FFN_TM, D_FF), BF16)],
        compiler_params=_params(("arbitrary",)),
        name="merge_ffn2",
    )(oa0, oh0, oa1, oh1, h, x, mod, w_mix, w_a, w_h, w_o, w_in, w_out, ln_g, ln_b)


def _rope_tables(n_tok):
    rows = n_tok // GRID_W
    row = np.repeat(np.arange(rows, dtype=np.float32), GRID_W)
    col = np.tile(np.arange(GRID_W, dtype=np.float32), rows)
    half = DH_A // 2
    inv = (ROPE_BASE ** (-np.arange(0, half, 2, dtype=np.float32) / half)).astype(np.float32)
    ar = row[:, None] * inv
    ac = col[:, None] * inv
    ang = np.concatenate([ar, ar, ac, ac] * 2, axis=-1)
    cos, sin = np.cos(ang).astype(np.float32), np.sin(ang).astype(np.float32)
    scale = np.float32(Q_SCALE)
    return (jnp.asarray(np.stack([cos * scale, cos, np.ones_like(cos)])),
            jnp.asarray(np.stack([sin * scale, sin, np.zeros_like(sin)])))


def _mixer(h1, group, wts, lam_init, bsz, ctx):
    m = MIX_TM
    t = m // bsz
    latent = ctx is not None
    per_head = lambda a: a.reshape(a.shape[0], bsz, t, a.shape[-1])
    if latent:
        qkv, plain, g, kd = _project(h1, group, wts['w_mix_in'], wts['lb_logits'], _rope_tables(t), False)
        k = v = None
        ctx_kv = (ctx[0], ctx[1])
    else:
        qkv, kv32, plain, g, kd = _project(h1, group, wts['w_mix_in'], wts['lb_logits'], None, True)
        k, v = kv32[0], kv32[1]
        ctx_kv = None
    qkv = per_head(qkv)
    q_kv = [(qkv, W_Q * N_HEADS_A), (qkv, W_K * N_HEADS_A), (qkv, W_V * N_HEADS_A)]
    o_a = _attention(*q_kv, ctx_kv, wts['lam_params'], wts['attn_subln_g'], lam_init)
    o_h, s_fin = _hgrn(per_head(plain), per_head(g), per_head(kd), wts['hgrn_norm_g'],
                       ctx[2] if latent else None, not latent)
    return (o_a.reshape(N_HEADS_A, m, DV_A), o_h.reshape(N_HEADS_H, m, DV_H)), (k, v, s_fin)


def kernel(x_prompt, x_sample, cache_k, cache_v, state_hgrn, c, c_ctx, w_ada, b_ada, ffn1_w_in, ffn1_w_out,
           w_mix_in, lambda_q1, lambda_k1, lambda_q2, lambda_k2, attn_subln_g, hgrn_lb_logits, hgrn_norm_g,
           w_branch_a, w_branch_h, w_mix_out, ffn2_w_in, ffn2_w_out, ln_g, ln_b):
    batch, seq, _ = x_prompt.shape
    dec_batch, dec_seq, _ = x_sample.shape
    past = cache_k.shape[2]
    assert DEPTH == 1 and dec_batch + 1 <= MOD_ROWS
    lam_init = 0.8 - 0.6 * math.exp(-0.3 * 0)

    cond = jnp.concatenate([c_ctx[None, :], c, jnp.zeros((MOD_ROWS - 1 - dec_batch, D_MODEL), F32)], axis=0)
    mod = _modulation(cond, w_ada[0], b_ada[0])

    wts = {
        'w_mix_in': w_mix_in[0],
        'lam_params': jnp.concatenate([lambda_q1, lambda_k1, lambda_q2, lambda_k2], axis=0),
        'attn_subln_g': attn_subln_g, 'lb_logits': hgrn_lb_logits, 'hgrn_norm_g': hgrn_norm_g,
    }
    n_ctx = batch * seq
    assert n_ctx == MIX_TM and dec_batch * dec_seq == MIX_TM

    def cond_of_tile(i, tm):
        first = i * tm
        return jnp.where(first < n_ctx, 0, 1 + (first - n_ctx) // dec_seq)

    x1, h1 = _ffn1(x_prompt.reshape(n_ctx, D_MODEL), x_sample.reshape(dec_batch * dec_seq, D_MODEL), mod,
                   cond_of_tile, ffn1_w_in[0], ffn1_w_out[0], ln_g[0], ln_b[0])
    mixed_p, (k_c, v_c, s_c) = _mixer(h1, 0, wts, lam_init, batch, None)
    ctx = (cache_k[:, 0].reshape(dec_batch, past, QK_A), cache_v[:, 0].reshape(dec_batch, past, W_A),
           state_hgrn[:, 0])
    mixed_s, _ = _mixer(h1, 1, wts, lam_init, dec_batch, ctx)
    y_p, y_s = _merge_ffn2((mixed_p, mixed_s), h1, x1, mod, cond_of_tile, w_mix_in[0],
                           w_branch_a[0], w_branch_h[0], w_mix_out[0], ffn2_w_in[0], ffn2_w_out[0],
                           ln_g[0], ln_b[0])

    return (y_p.reshape(batch, seq, D_MODEL),
            y_s.reshape(dec_batch, dec_seq, D_MODEL),
            k_c.reshape(batch, DEPTH, seq, N_HEADS_A, 2, DH_A),
            v_c.reshape(batch, DEPTH, seq, N_HEADS_A, DV_A),
            s_c.reshape(batch, DEPTH, 2, N_HEADS_H, DK_H, DV_H))
```

```python
import functools
import math

import numpy as np
import jax
import jax.numpy as jnp
from jax import lax
from jax.experimental import pallas as pl
from jax.experimental.pallas import tpu as pltpu

D_MODEL = 1024
DEPTH = 1
GRID_W = 64
N_HEADS_A = 4
DH_A = 64
DV_A = 2 * DH_A
QK_A = N_HEADS_A * 2 * DH_A
W_A = N_HEADS_A * DV_A
ROPE_BASE = 10000.0
N_HEADS_H = 4
DK_H = 128
DV_H = 128
QK_H = N_HEADS_H * DK_H
W_H = N_HEADS_H * DV_H
D_FF = 2816
MIX_IN = 2 * QK_A + W_A + 3 * QK_H + 2 * W_H + 2 * D_MODEL
ALPHA = (2 * DEPTH) ** 0.25
LN_EPS = 1e-5
RMS_EPS = 1e-6
Q_SCALE = DH_A ** -0.5 * math.log2(math.e)

F32 = jnp.float32
BF16 = jnp.bfloat16

LANES = 128
VMEM_LIMIT = 56 * 1024 * 1024

FFN_TM = 512
FFN_TF = 256
WIDE_CHUNK_ROWS = 128
SQUARE_CHUNK_ROWS = 256
MIX_TM = 4096
PROJ_TM = 2048
PROJ_SUB = 512
MIX_TN = 512
ATT_TQ = 1024
ATT_SUB = 512
ATT_SHORT_SEQ = 256
HGRN_C = 64
HGRN_ROWS = 512
HGRN_SHORT_HEADS = 4
HGRN_SAFE_DECAY = 75.0
MOD_ROWS = 8

W_Q, W_K, W_V, W_HQ, W_HFF, W_HFB, W_HI, W_HG, W_GA0, W_GA1, W_GH0, W_GH1 = range(12)


def _params(sem):
    return pltpu.CompilerParams(dimension_semantics=sem, vmem_limit_bytes=VMEM_LIMIT)


def _nt_dot(a, b):
    return lax.dot_general(a, b, (((1,), (1,)), ((), ())), preferred_element_type=F32)


def _tn_dot(a, b):
    return lax.dot_general(a, b, (((0,), (0,)), ((), ())), preferred_element_type=F32)


def _layer_norm(y, g, b):
    mu = jnp.mean(y, axis=-1, keepdims=True)
    yc = y - mu
    var = jnp.mean(yc * yc, axis=-1, keepdims=True)
    return yc * lax.rsqrt(var + LN_EPS) * g + b


def _mod_kernel(c_ref, w_ref, b_ref, o_ref):
    c = c_ref[...]
    a = (c * jax.nn.sigmoid(c)).astype(BF16)
    o_ref[...] = jnp.dot(a, w_ref[...].astype(BF16), preferred_element_type=F32) + b_ref[...]


def _modulation(cond, w_ada, b_ada):
    n_out = w_ada.shape[1]
    tn = D_MODEL
    out = pl.pallas_call(
        _mod_kernel,
        grid=(n_out // tn,),
        in_specs=[
            pl.BlockSpec((MOD_ROWS, D_MODEL), lambda n: (0, 0)),
            pl.BlockSpec((D_MODEL, tn), lambda n: (0, n)),
            pl.BlockSpec((1, tn), lambda n: (0, n)),
        ],
        out_specs=pl.BlockSpec((MOD_ROWS, tn), lambda n: (0, n)),
        out_shape=jax.ShapeDtypeStruct((MOD_ROWS, n_out), F32),
        compiler_params=_params(("arbitrary",)),
        name="modulation",
    )(cond, w_ada, b_ada.reshape(1, n_out))
    return out.reshape(MOD_ROWS, 9, D_MODEL)


def _load_weight(w_hbm, w_scr, stage, sems, src_col=0, dst_col=0):
    _, chunk, width = stage.shape
    n_chunks = w_hbm.shape[0] // chunk

    def copy(c):
        return pltpu.make_async_copy(w_hbm.at[pl.ds(c * chunk, chunk), pl.ds(src_col, width)],
                                     stage.at[c % 2], sems.at[c % 2])

    copy(0).start()
    for c in range(n_chunks):
        if c + 1 < n_chunks:
            copy(c + 1).start()
        copy(c).wait()
        w_scr[c * chunk:(c + 1) * chunk, dst_col:dst_col + width] = stage[c % 2].astype(BF16)


def _modulate(x, mod_ref, sub):
    return x * (1.0 + mod_ref[0, 3 * sub + 1:3 * sub + 2, :]) + mod_ref[0, 3 * sub:3 * sub + 1, :]


def _half_step(x, mod_ref, win_ref, wout_ref, lng_ref, lnb_ref, act_scr, sub):
    h = _modulate(x, mod_ref, sub).astype(BF16)
    for c in range(D_FF // FFN_TF):
        cols = slice(c * FFN_TF, (c + 1) * FFN_TF)
        up_cols = slice(D_FF + c * FFN_TF, D_FF + (c + 1) * FFN_TF)
        gate = jnp.dot(h, win_ref[:, cols], preferred_element_type=F32)
        up = jnp.dot(h, win_ref[:, up_cols], preferred_element_type=F32)
        act_scr[:, cols] = (gate * jax.nn.sigmoid(gate) * up).astype(BF16)
    f = jnp.dot(act_scr[...], wout_ref[...], preferred_element_type=F32)
    g = mod_ref[0, 3 * sub + 2:3 * sub + 3, :]
    return _layer_norm(ALPHA * x + 0.5 * g * f, lng_ref[sub:sub + 1, :], lnb_ref[sub:sub + 1, :])


def _ffn_weight_scratch():
    return [
        pltpu.VMEM((D_MODEL, 2 * D_FF), BF16),
        pltpu.VMEM((D_FF, D_MODEL), BF16),
        pltpu.VMEM((2, WIDE_CHUNK_ROWS, 2 * D_FF), F32),
        pltpu.VMEM((2, SQUARE_CHUNK_ROWS, D_MODEL), F32),
        pltpu.SemaphoreType.DMA((2,)),
        pltpu.SemaphoreType.DMA((2,)),
    ]


def _ffn1_kernel(xa_ref, xb_ref, mod_ref, win_hbm, wout_hbm, lng_ref, lnb_ref, xo_ref, ho_ref,
                 win_scr, wout_scr, wide_stage, square_stage, wide_sems, square_sems, act_scr, *, tiles_a):
    i = pl.program_id(0)

    @pl.when(i == 0)
    def _():
        _load_weight(win_hbm, win_scr, wide_stage, wide_sems)
        _load_weight(wout_hbm, wout_scr, square_stage, square_sems)

    x = jnp.where(i < tiles_a, xa_ref[...], xb_ref[...])
    xn = _half_step(x, mod_ref, win_scr, wout_scr, lng_ref, lnb_ref, act_scr, 0)
    xo_ref[...] = xn
    ho_ref[...] = _modulate(xn, mod_ref, 1).astype(BF16)


def _ffn1(xa, xb, mod, cond_of_tile, w_in, w_out, ln_g, ln_b):
    tiles_a, tiles_b = xa.shape[0] // FFN_TM, xb.shape[0] // FFN_TM
    m = xa.shape[0] + xb.shape[0]
    row = lambda i: (i, 0)
    const = lambda i: (0, 0)
    hbm = pl.BlockSpec(memory_space=pl.ANY)
    return pl.pallas_call(
        functools.partial(_ffn1_kernel, tiles_a=tiles_a),
        grid=(tiles_a + tiles_b,),
        in_specs=[
            pl.BlockSpec((FFN_TM, D_MODEL), lambda i: (jnp.minimum(i, tiles_a - 1), 0)),
            pl.BlockSpec((FFN_TM, D_MODEL), lambda i: (jnp.maximum(i - tiles_a, 0), 0)),
            pl.BlockSpec((1, 9, D_MODEL), lambda i: (cond_of_tile(i, FFN_TM), 0, 0)),
            hbm, hbm,
            pl.BlockSpec((3, D_MODEL), const),
            pl.BlockSpec((3, D_MODEL), const),
        ],
        out_specs=[pl.BlockSpec((FFN_TM, D_MODEL), row), pl.BlockSpec((FFN_TM, D_MODEL), row)],
        out_shape=[jax.ShapeDtypeStruct((m, D_MODEL), F32), jax.ShapeDtypeStruct((m, D_MODEL), BF16)],
        scratch_shapes=_ffn_weight_scratch() + [pltpu.VMEM((FFN_TM, D_FF), BF16)],
        compiler_params=_params(("arbitrary",)),
        name="ffn1",
    )(xa, xb, mod, w_in, w_out, ln_g, ln_b)


def _rope(x, cos, sin):
    lane = lax.broadcasted_iota(jnp.int32, (1, LANES), 1)
    first_half = (lane % 32) < 16
    outs = []
    for hb in range(QK_A // LANES):
        xb = x[:, hb * LANES:(hb + 1) * LANES]
        ahead = pltpu.roll(xb, LANES - 16, 1)
        behind = pltpu.roll(xb, 16, 1)
        rot = jnp.where(first_half, -ahead, behind)
        outs.append(xb * cos + rot * sin)
    return jnp.concatenate(outs, axis=-1)


def _forget_gate(z, logits):
    mx = jnp.max(logits, axis=0, keepdims=True)
    el = jnp.exp(logits - mx)
    lb = el[0:1] / jnp.sum(el, axis=0, keepdims=True)
    sig = 0.5 + 0.5 * jnp.tanh(0.5 * z)
    u = (1.0 - lb) * sig
    return jnp.log(lb + u), (1.0 - lb) - u


def _proj_kernel(*refs, rope, want_kv32):
    refs = list(refs)
    h_ref, w_ref, lbl_ref = refs[:3]
    refs = refs[3:]
    cos_ref, sin_ref = (refs.pop(0), refs.pop(0)) if rope else (None, None)
    qkv_ref = refs.pop(0)
    kv32_ref = refs.pop(0) if want_kv32 else None
    plain_ref, g_ref, kd_ref = refs
    n = pl.program_id(1)

    pieces = [slice(r, r + PROJ_SUB) for r in range(0, h_ref.shape[0], PROJ_SUB)]

    def project(rows, w):
        return jnp.dot(h_ref[rows, :], w, preferred_element_type=F32)

    def store_heads(ref, rows, y):
        for hh in range(MIX_TN // LANES):
            ref[hh, rows, :] = y[:, hh * LANES:(hh + 1) * LANES].astype(ref.dtype)

    @pl.when(n <= W_V)
    def _():
        w = w_ref[...].astype(BF16)
        for rows in pieces:
            y = project(rows, w)
            if want_kv32:
                kv32_ref[0, rows, :] = y
            if rope:
                first = rows.start % cos_ref.shape[1]
                table_rows = slice(first, first + PROJ_SUB)
                y = _rope(y, cos_ref[0, table_rows, :], sin_ref[0, table_rows, :])
            else:
                y = y * jnp.where(n == W_Q, Q_SCALE, 1.0)
            store_heads(qkv_ref, rows, y)

    @pl.when((n == W_HFF) | (n == W_HFB))
    def _():
        w = w_ref[...].astype(BF16)
        for rows in pieces:
            log_f, kd = _forget_gate(project(rows, w), lbl_ref[0])
            store_heads(g_ref, rows, log_f)
            store_heads(kd_ref, rows, kd)

    @pl.when((n == W_HQ) | (n >= W_HI))
    def _():
        w = w_ref[...].astype(BF16)
        for rows in pieces:
            store_heads(plain_ref, rows, project(rows, w))


def _project(h, group, w, lb_logits, rope_tables, want_kv32):
    m = MIX_TM
    tm = PROJ_TM
    tiles = m // tm
    rope = rope_tables is not None
    heads = MIX_TN // LANES
    n_blocks = W_HG + 1
    in_specs = [pl.BlockSpec((tm, D_MODEL), lambda i, n: (group * tiles + i, 0)),
                pl.BlockSpec((D_MODEL, MIX_TN), lambda i, n: (0, n)),
                pl.BlockSpec((1, DEPTH + 1, QK_H), lambda i, n: (jnp.clip(n - W_HFF, 0, 1), 0, 0))]
    args = [h, w, lb_logits]
    qkv_col = lambda n: jnp.minimum(n, W_V)
    if rope:
        t_rope = rope_tables[0].shape[1]
        rope_rows = min(t_rope, tm)
        n_rope_tiles = t_rope // rope_rows
        for t in rope_tables:
            in_specs.append(pl.BlockSpec((1, rope_rows, DV_A), lambda i, n: (qkv_col(n), i % n_rope_tiles, 0)))
            args.append(t)
    head_block = (heads, tm, LANES)
    plain_col = lambda n: jnp.where(n < W_HI, 0, n - W_HI + 1)
    gate_col = lambda n: jnp.clip(n - W_HFF, 0, 1)
    out_specs = [pl.BlockSpec(head_block, lambda i, n: (qkv_col(n), i, 0))]
    out_shape = [jax.ShapeDtypeStruct((3 * heads, m, LANES), BF16)]
    if want_kv32:
        out_specs.append(pl.BlockSpec((1, tm, MIX_TN), lambda i, n: (jnp.clip(n - W_K, 0, 1), i, 0)))
        out_shape.append(jax.ShapeDtypeStruct((2, m, MIX_TN), F32))
    out_specs += [pl.BlockSpec(head_block, lambda i, n: (plain_col(n), i, 0)),
                  pl.BlockSpec(head_block, lambda i, n: (gate_col(n), i, 0)),
                  pl.BlockSpec(head_block, lambda i, n: (gate_col(n), i, 0))]
    out_shape += [jax.ShapeDtypeStruct((3 * heads, m, LANES), BF16),
                  jax.ShapeDtypeStruct((2 * heads, m, LANES), F32),
                  jax.ShapeDtypeStruct((2 * heads, m, LANES), BF16)]
    return pl.pallas_call(
        functools.partial(_proj_kernel, rope=rope, want_kv32=want_kv32),
        grid=(tiles, n_blocks),
        in_specs=in_specs,
        out_specs=out_specs,
        out_shape=out_shape,
        compiler_params=_params(("arbitrary", "arbitrary")),
        name="mix_proj",
    )(*args)


def _attn_kernel(*refs, lam_init, has_ctx):
    if has_ctx:
        lam_ref, g_ref, q_ref, k_ref, v_ref, kc_ref, vc_ref, o_ref = refs
    else:
        lam_ref, g_ref, q_ref, k_ref, v_ref, o_ref = refs
    lp = lam_ref[...]
    lam = (jnp.exp(jnp.sum(lp[0:1] * lp[1:2], axis=-1, keepdims=True))
           - jnp.exp(jnp.sum(lp[2:3] * lp[3:4], axis=-1, keepdims=True)) + lam_init)

    lane = lax.broadcasted_iota(jnp.int32, (1, DV_A), 1)
    comp0 = lane < DH_A
    tq = q_ref.shape[2]
    sub = min(ATT_SUB, tq)
    problems = []
    for hh in range(q_ref.shape[0]):
        keys = [k_ref[hh, 0]]
        vals = [v_ref[hh, 0]]
        if has_ctx:
            keys.append(kc_ref[0, :, hh * DV_A:(hh + 1) * DV_A].astype(BF16))
            vals.append(vc_ref[0, :, hh * DV_A:(hh + 1) * DV_A].astype(BF16))
        for r in range(tq // sub):
            q = q_ref[hh, 0, r * sub:(r + 1) * sub, :]
            zero = jnp.zeros_like(q)
            for qc in (jnp.where(comp0, q, zero), jnp.where(comp0, zero, q)):
                problems.append((qc, keys, vals))
    scores = [[_nt_dot(kk, qc) for kk in keys] for qc, keys, _ in problems]
    weights = []
    for ss in scores:
        mx = functools.reduce(jnp.maximum, [jnp.max(s, axis=0, keepdims=True) for s in ss])
        es = [jnp.exp2(s - mx) for s in ss]
        den = functools.reduce(jnp.add, [jnp.sum(e, axis=0, keepdims=True) for e in es])
        weights.append(([e.astype(BF16) for e in es], 1.0 / den))
    maps = [functools.reduce(jnp.add, [_tn_dot(vv, e) for e, vv in zip(es, vals)]) * r
            for (es, r), (_, _, vals) in zip(weights, problems)]
    idx = 0
    for hh in range(q_ref.shape[0]):
        for r in range(tq // sub):
            o = maps[idx] - lam * maps[idx + 1]
            idx += 2
            o = o * lax.rsqrt(jnp.mean(o * o, axis=0, keepdims=True) + RMS_EPS)
            o = o.T * g_ref[...]
            o_ref[hh, 0, r * sub:(r + 1) * sub, :] = (o * (1.0 - lam_init)).astype(o_ref.dtype)


def _attention(q, k, v, ctx_kv, lam_params, subln_g, lam_init):
    (q_arr, q_row), (k_arr, k_row), (v_arr, v_row) = q, k, v
    _, bsz, t, _ = q_arr.shape
    tq = min(ATT_TQ, t)
    hp = N_HEADS_A if t <= ATT_SHORT_SEQ else 1
    const = lambda b, h, i: (0, 0)
    rows = lambda first: (lambda b, h, i: (first // hp + h, b, 0, 0))
    in_specs = [
        pl.BlockSpec((4, DH_A), const),
        pl.BlockSpec((1, DV_A), const),
        pl.BlockSpec((hp, 1, tq, DV_A), lambda b, h, i: (q_row // hp + h, b, i, 0)),
        pl.BlockSpec((hp, 1, t, DV_A), rows(k_row)),
        pl.BlockSpec((hp, 1, t, DV_A), rows(v_row)),
    ]
    args = [lam_params, subln_g, q_arr, k_arr, v_arr]
    if ctx_kv is not None:
        p = ctx_kv[0].shape[1]
        head_all = lambda b, h, i: (b, 0, h)
        in_specs += [pl.BlockSpec((1, p, hp * DV_A), head_all), pl.BlockSpec((1, p, hp * DV_A), head_all)]
        args += list(ctx_kv)
    return pl.pallas_call(
        functools.partial(_attn_kernel, lam_init=lam_init, has_ctx=ctx_kv is not None),
        grid=(bsz, N_HEADS_A // hp, t // tq),
        in_specs=in_specs,
        out_specs=pl.BlockSpec((hp, 1, tq, DV_A), lambda b, h, i: (h, b, i, 0)),
        out_shape=jax.ShapeDtypeStruct((N_HEADS_A, bsz, t, DV_A), BF16),
        compiler_params=_params(("parallel", "parallel", "arbitrary")),
        name="attention",
    )(*args)


def _chunk_scan(g, reverse):
    r = g.shape[0]
    pos = lax.broadcasted_iota(jnp.int32, (r, 1), 0) % HGRN_C
    b = g
    sh = 1
    while sh < HGRN_C:
        if reverse:
            b = b + jnp.where(pos < HGRN_C - sh, pltpu.roll(b, r - sh, 0), 0.0)
        else:
            b = b + jnp.where(pos >= sh, pltpu.roll(b, sh, 0), 0.0)
        sh *= 2
    return b


def _hgrn_kernel(*refs, seq_len, has_s0, want_state):
    refs = list(refs)
    hq_ref, hi_ref, hg_ref, gf_ref, gb_ref, kf_ref, kb_ref, ng_ref = refs[:8]
    refs = refs[8:]
    s0_ref = refs.pop(0) if has_s0 else None
    o_ref = refs.pop(0)
    sfin_ref = refs.pop(0) if want_state else None
    od_scr, st_scr, b_scr, q_scr, k_scr, ok_scr = refs
    hp = hq_ref.shape[0]
    c = HGRN_C
    rb = min(HGRN_ROWS, seq_len)
    nb = seq_len // rb
    cpb = rb // c
    g_refs = (gf_ref, gb_ref)
    k_refs = (kf_ref, kb_ref)

    def queries(hq):
        hq = hq.astype(F32)
        return hq * jax.nn.sigmoid(hq)

    def flags(i, carry):
        rows = pl.ds(pl.multiple_of(i * rb, rb), rb)
        for d in range(2):
            worst = jnp.zeros((1, DK_H), F32)
            for hh in range(hp):
                g = g_refs[d][hh, 0, rows, :]
                for j in range(cpb):
                    worst = jnp.maximum(worst, -jnp.sum(g[j * c:(j + 1) * c], axis=0, keepdims=True))
            ok_scr[d, i] = (jnp.max(worst) <= HGRN_SAFE_DECAY).astype(jnp.int32)
        return carry

    lax.fori_loop(0, nb, flags, 0)

    for hh in range(hp):
        for d in range(2):
            if has_s0:
                st_scr[hh, d] = s0_ref[0, d, hh].T
            else:
                st_scr[hh, d] = jnp.zeros((DV_H, DK_H), F32)

    row = lax.broadcasted_iota(jnp.int32, (c, 1), 0)
    col = lax.broadcasted_iota(jnp.int32, (1, c), 1)

    def fast_block(d, r0):
        reverse = d == 1
        rows = pl.ds(r0, rb)
        n = hp * cpb
        flat = lambda a: a.reshape(hp * rb, a.shape[-1])
        chunks = lambda a: a.reshape(n, c, a.shape[-1])
        vb = chunks(hi_ref[:, 0, rows, :])
        b = _chunk_scan(flat(g_refs[d][:, 0, rows, :]), reverse)
        qd = chunks((queries(flat(hq_ref[:, 0, rows, :])) * jnp.exp(b)).astype(BF16))
        ku = chunks(flat(k_refs[d][:, 0, rows, :]).astype(F32) * jnp.exp(-b))
        b = chunks(b)
        e_end = jnp.exp(b[:, 0:1, :] if reverse else b[:, c - 1:c, :])
        causal = ((row <= col) if reverse else (row >= col))[None]
        a = jnp.einsum('nck,nsk->ncs', qd, ku.astype(BF16), preferred_element_type=F32)
        o = jnp.einsum('ncs,nsv->ncv', jnp.where(causal, a, 0.0).astype(BF16), vb, preferred_element_type=F32)
        u = jnp.einsum('nsv,nsk->nvk', vb, (ku * e_end).astype(BF16), preferred_element_type=F32)
        entering = [None] * n
        for hh in range(hp):
            st = st_scr[hh, d]
            for j in (range(cpb - 1, -1, -1) if reverse else range(cpb)):
                entering[hh * cpb + j] = st.astype(BF16)
                st = e_end[hh * cpb + j] * st + u[hh * cpb + j]
            st_scr[hh, d] = st
        o = o + jnp.einsum('nck,nvk->ncv', qd, jnp.stack(entering), preferred_element_type=F32)
        od_scr[:, d, rows, :] = o.reshape(hp, rb, DV_H)

    def exact_chunk(hh, d, r0):
        reverse = d == 1
        rows = pl.ds(r0, c)
        b = _chunk_scan(g_refs[d][hh, 0, rows, :], reverse)
        b_scr[...] = b
        q_scr[...] = queries(hq_ref[hh, 0, rows, :])
        k_scr[...] = k_refs[d][hh, 0, rows, :].astype(F32)

        def one_row(t, carry):
            bt = b_scr[pl.ds(t, 1), :]
            qt = q_scr[pl.ds(t, 1), :]
            seen = (row >= t) if reverse else (row <= t)
            w = jnp.exp(jnp.where(seen, bt - b_scr[...], -jnp.inf))
            p = jnp.sum(qt * k_scr[...] * w, axis=-1, keepdims=True)
            od_scr[hh, d, pl.ds(r0 + t, 1), :] = jnp.sum(p * hi_ref[hh, 0, rows, :].astype(F32),
                                                         axis=0, keepdims=True)
            return carry

        lax.fori_loop(0, c, one_row, 0)
        b_end = b[0:1] if reverse else b[c - 1:c]
        st = st_scr[hh, d]
        qd = (q_scr[...] * jnp.exp(b)).astype(BF16)
        od_scr[hh, d, rows, :] = od_scr[hh, d, rows, :] + _nt_dot(qd, st.astype(BF16))
        k_end = (k_scr[...] * jnp.exp(b_end - b)).astype(BF16)
        st_scr[hh, d] = jnp.exp(b_end) * st + _tn_dot(hi_ref[hh, 0, rows, :], k_end)

    def block(i, carry):
        starts = (pl.multiple_of(i * rb, rb), pl.multiple_of((nb - 1 - i) * rb, rb))
        mild = (ok_scr[0, i] + ok_scr[1, nb - 1 - i]) == 2

        @pl.when(mild)
        def _():
            for d in range(2):
                fast_block(d, starts[d])

        @pl.when(jnp.logical_not(mild))
        def _():
            for hh in range(hp):
                for d in range(2):
                    for j in (range(cpb - 1, -1, -1) if d == 1 else range(cpb)):
                        exact_chunk(hh, d, pl.multiple_of(starts[d] + j * c, c))

        return carry

    lax.fori_loop(0, nb, block, 0)

    def finish(i, carry):
        rows = pl.ds(pl.multiple_of(i * rb, rb), rb)
        for hh in range(hp):
            o = od_scr[hh, 0, rows, :] + od_scr[hh, 1, rows, :]
            o = o * lax.rsqrt(jnp.mean(o * o, axis=-1, keepdims=True) + RMS_EPS) * ng_ref[...]
            hg = hg_ref[hh, 0, rows, :].astype(F32)
            o_ref[hh, 0, rows, :] = (o * (hg * jax.nn.sigmoid(hg))).astype(o_ref.dtype)
        return carry

    lax.fori_loop(0, nb, finish, 0)
    if want_state:
        for hh in range(hp):
            for d in range(2):
                sfin_ref[0, d, hh] = st_scr[hh, d].T


def _hgrn(plain, g, kd, norm_g, s0, want_state):
    _, bsz, t, _ = plain.shape
    hp = HGRN_SHORT_HEADS if t <= HGRN_ROWS else 1
    stream = lambda s: pl.BlockSpec((hp, 1, t, DK_H), lambda b, h: (s * N_HEADS_H // hp + h, b, 0, 0))
    in_specs = [stream(0), stream(1), stream(2), stream(0), stream(1), stream(0), stream(1),
                pl.BlockSpec((1, DV_H), lambda b, h: (0, 0))]
    args = [plain, plain, plain, g, g, kd, kd, norm_g]
    state_spec = pl.BlockSpec((1, 2, hp, DK_H, DV_H), lambda b, h: (b, 0, h, 0, 0))
    if s0 is not None:
        in_specs.append(state_spec)
        args.append(s0)
    out_specs = [pl.BlockSpec((hp, 1, t, DV_H), lambda b, h: (h, b, 0, 0))]
    out_shape = [jax.ShapeDtypeStruct((N_HEADS_H, bsz, t, DV_H), BF16)]
    if want_state:
        out_specs.append(state_spec)
        out_shape.append(jax.ShapeDtypeStruct((bsz, 2, N_HEADS_H, DK_H, DV_H), F32))
    n_blocks = t // min(HGRN_ROWS, t)
    chunk = pltpu.VMEM((HGRN_C, DK_H), F32)
    outs = pl.pallas_call(
        functools.partial(_hgrn_kernel, seq_len=t, has_s0=s0 is not None, want_state=want_state),
        grid=(bsz, N_HEADS_H // hp),
        in_specs=in_specs,
        out_specs=out_specs,
        out_shape=out_shape,
        scratch_shapes=[
            pltpu.VMEM((hp, 2, t, DV_H), F32),
            pltpu.VMEM((hp, 2, DV_H, DK_H), F32),
            chunk, chunk, chunk,
            pltpu.SMEM((2, n_blocks), jnp.int32),
        ],
        compiler_params=_params(("parallel", "parallel")),
        name="hgrn",
    )(*args)
    return outs if want_state else (outs[0], None)


def _merge_ffn2_kernel(oa0_ref, oh0_ref, oa1_ref, oh1_ref, h_ref, x_ref, mod_ref,
                       wmix_hbm, wa_hbm, wh_hbm, wo_hbm, win_hbm, wout_hbm, lng_ref, lnb_ref, y0_ref, y1_ref,
                       wg_scr, wa_scr, wh_scr, wo_scr,
                       win_scr, wout_scr, wide_stage, square_stage, wide_sems, square_sems, act_scr, *, tiles_0):
    i = pl.program_id(0)
    first_group = i < tiles_0

    @pl.when(i == 0)
    def _():
        _load_weight(win_hbm, win_scr, wide_stage, wide_sems)
        for half in range(2):
            _load_weight(wmix_hbm, wg_scr, square_stage, square_sems,
                         src_col=W_GA0 * MIX_TN + half * D_MODEL, dst_col=half * D_MODEL)
        for w_hbm, w_scr in ((wa_hbm, wa_scr), (wh_hbm, wh_scr), (wo_hbm, wo_scr), (wout_hbm, wout_scr)):
            _load_weight(w_hbm, w_scr, square_stage, square_sems)

    def heads(ref0, ref1):
        pick = lambda hh: jnp.where(first_group, ref0[hh], ref1[hh])
        return jnp.concatenate([pick(hh) for hh in range(ref0.shape[0])], axis=-1)

    gates = jax.nn.sigmoid(jnp.dot(h_ref[...], wg_scr[...], preferred_element_type=F32))
    ba = jnp.dot(heads(oa0_ref, oa1_ref), wa_scr[...], preferred_element_type=F32)
    bh = jnp.dot(heads(oh0_ref, oh1_ref), wh_scr[...], preferred_element_type=F32)
    merged = gates[:, :D_MODEL] * ba + gates[:, D_MODEL:] * bh
    mix = jnp.dot(merged.astype(BF16), wo_scr[...], preferred_element_type=F32)
    x2 = _layer_norm(ALPHA * x_ref[...] + mod_ref[0, 5:6, :] * mix, lng_ref[1:2, :], lnb_ref[1:2, :])
    y = _half_step(x2, mod_ref, win_scr, wout_scr, lng_ref, lnb_ref, act_scr, 2)

    @pl.when(first_group)
    def _():
        y0_ref[...] = y

    @pl.when(jnp.logical_not(first_group))
    def _():
        y1_ref[...] = y


def _merge_ffn2(mixed, h, x, mod, cond_of_tile, w_mix, w_a, w_h, w_o, w_in, w_out, ln_g, ln_b):
    (oa0, oh0), (oa1, oh1) = mixed
    tiles_0, tiles_1 = oa0.shape[1] // FFN_TM, oa1.shape[1] // FFN_TM
    row = lambda i: (i, 0)
    const = lambda i: (0, 0)
    in_group_0 = lambda i: jnp.minimum(i, tiles_0 - 1)
    in_group_1 = lambda i: jnp.maximum(i - tiles_0, 0)
    hbm = pl.BlockSpec(memory_space=pl.ANY)
    return pl.pallas_call(
        functools.partial(_merge_ffn2_kernel, tiles_0=tiles_0),
        grid=(tiles_0 + tiles_1,),
        in_specs=[
            pl.BlockSpec((N_HEADS_A, FFN_TM, DV_A), lambda i: (0, in_group_0(i), 0)),
            pl.BlockSpec((N_HEADS_H, FFN_TM, DV_H), lambda i: (0, in_group_0(i), 0)),
            pl.BlockSpec((N_HEADS_A, FFN_TM, DV_A), lambda i: (0, in_group_1(i), 0)),
            pl.BlockSpec((N_HEADS_H, FFN_TM, DV_H), lambda i: (0, in_group_1(i), 0)),
            pl.BlockSpec((FFN_TM, D_MODEL), row),
            pl.BlockSpec((FFN_TM, D_MODEL), row),
            pl.BlockSpec((1, 9, D_MODEL), lambda i: (cond_of_tile(i, FFN_TM), 0, 0)),
            hbm, hbm, hbm, hbm, hbm, hbm,
            pl.BlockSpec((3, D_MODEL), const),
            pl.BlockSpec((3, D_MODEL), const),
        ],
        out_specs=[pl.BlockSpec((FFN_TM, D_MODEL), lambda i: (in_group_0(i), 0)),
                   pl.BlockSpec((FFN_TM, D_MODEL), lambda i: (in_group_1(i), 0))],
        out_shape=[jax.ShapeDtypeStruct((tiles_0 * FFN_TM, D_MODEL), F32),
                   jax.ShapeDtypeStruct((tiles_1 * FFN_TM, D_MODEL), F32)],
        scratch_shapes=[pltpu.VMEM((D_MODEL, 2 * D_MODEL), BF16), pltpu.VMEM((W_A, D_MODEL), BF16),
                        pltpu.VMEM((W_H, D_MODEL), BF16), pltpu.VMEM((D_MODEL, D_MODEL), BF16)]
                       + _ffn_weight_scratch() + [pltpu.VMEM((FFN_TM, D_FF), BF16)],
        compiler_params=_params(("arbitrary",)),
        name="merge_ffn2",
    )(oa0, oh0, oa1, oh1, h, x, mod, w_mix, w_a, w_h, w_o, w_in, w_out, ln_g, ln_b)


def _rope_tables(n_tok):
    rows = n_tok // GRID_W
    row = np.repeat(np.arange(rows, dtype=np.float32), GRID_W)
    col = np.tile(np.arange(GRID_W, dtype=np.float32), rows)
    half = DH_A // 2
    inv = (ROPE_BASE ** (-np.arange(0, half, 2, dtype=np.float32) / half)).astype(np.float32)
    ar = row[:, None] * inv
    ac = col[:, None] * inv
    ang = np.concatenate([ar, ar, ac, ac] * 2, axis=-1)
    cos, sin = np.cos(ang).astype(np.float32), np.sin(ang).astype(np.float32)
    scale = np.float32(Q_SCALE)
    return (jnp.asarray(np.stack([cos * scale, cos, np.ones_like(cos)])),
            jnp.asarray(np.stack([sin * scale, sin, np.zeros_like(sin)])))


def _mixer(h1, group, wts, lam_init, bsz, ctx):
    m = MIX_TM
    t = m // bsz
    latent = ctx is not None
    per_head = lambda a: a.reshape(a.shape[0], bsz, t, a.shape[-1])
    if latent:
        qkv, plain, g, kd = _project(h1, group, wts['w_mix_in'], wts['lb_logits'], _rope_tables(t), False)
        k = v = None
        ctx_kv = (ctx[0], ctx[1])
    else:
        qkv, kv32, plain, g, kd = _project(h1, group, wts['w_mix_in'], wts['lb_logits'], None, True)
        k, v = kv32[0], kv32[1]
        ctx_kv = None
    qkv = per_head(qkv)
    q_kv = [(qkv, W_Q * N_HEADS_A), (qkv, W_K * N_HEADS_A), (qkv, W_V * N_HEADS_A)]
    o_a = _attention(*q_kv, ctx_kv, wts['lam_params'], wts['attn_subln_g'], lam_init)
    o_h, s_fin = _hgrn(per_head(plain), per_head(g), per_head(kd), wts['hgrn_norm_g'],
                       ctx[2] if latent else None, not latent)
    return (o_a.reshape(N_HEADS_A, m, DV_A), o_h.reshape(N_HEADS_H, m, DV_H)), (k, v, s_fin)


def kernel(x_prompt, x_sample, cache_k, cache_v, state_hgrn, c, c_ctx, w_ada, b_ada, ffn1_w_in, ffn1_w_out,
           w_mix_in, lambda_q1, lambda_k1, lambda_q2, lambda_k2, attn_subln_g, hgrn_lb_logits, hgrn_norm_g,
           w_branch_a, w_branch_h, w_mix_out, ffn2_w_in, ffn2_w_out, ln_g, ln_b):
    batch, seq, _ = x_prompt.shape
    dec_batch, dec_seq, _ = x_sample.shape
    past = cache_k.shape[2]
    assert DEPTH == 1 and dec_batch + 1 <= MOD_ROWS
    lam_init = 0.8 - 0.6 * math.exp(-0.3 * 0)

    cond = jnp.concatenate([c_ctx[None, :], c, jnp.zeros((MOD_ROWS - 1 - dec_batch, D_MODEL), F32)], axis=0)
    mod = _modulation(cond, w_ada[0], b_ada[0])

    wts = {
        'w_mix_in': w_mix_in[0],
        'lam_params': jnp.concatenate([lambda_q1, lambda_k1, lambda_q2, lambda_k2], axis=0),
        'attn_subln_g': attn_subln_g, 'lb_logits': hgrn_lb_logits, 'hgrn_norm_g': hgrn_norm_g,
    }
    n_ctx = batch * seq
    assert n_ctx == MIX_TM and dec_batch * dec_seq == MIX_TM

    def cond_of_tile(i, tm):
        first = i * tm
        return jnp.where(first < n_ctx, 0, 1 + (first - n_ctx) // dec_seq)

    x1, h1 = _ffn1(x_prompt.reshape(n_ctx, D_MODEL), x_sample.reshape(dec_batch * dec_seq, D_MODEL), mod,
                   cond_of_tile, ffn1_w_in[0], ffn1_w_out[0], ln_g[0], ln_b[0])
    mixed_p, (k_c, v_c, s_c) = _mixer(h1, 0, wts, lam_init, batch, None)
    ctx = (cache_k[:, 0].reshape(dec_batch, past, QK_A), cache_v[:, 0].reshape(dec_batch, past, W_A),
           state_hgrn[:, 0])
    mixed_s, _ = _mixer(h1, 1, wts, lam_init, dec_batch, ctx)
    y_p, y_s = _merge_ffn2((mixed_p, mixed_s), h1, x1, mod, cond_of_tile, w_mix_in[0],
                           w_branch_a[0], w_branch_h[0], w_mix_out[0], ffn2_w_in[0], ffn2_w_out[0],
                           ln_g[0], ln_b[0])

    return (y_p.reshape(batch, seq, D_MODEL),
            y_s.reshape(dec_batch, dec_seq, D_MODEL),
            k_c.reshape(batch, DEPTH, seq, N_HEADS_A, 2, DH_A),
            v_c.reshape(batch, DEPTH, seq, N_HEADS_A, DV_A),
            s_c.reshape(batch, DEPTH, 2, N_HEADS_H, DK_H, DV_H))
```

```python
import functools
import math

import numpy as np
import jax
import jax.numpy as jnp
from jax import lax
from jax.experimental import pallas as pl
from jax.experimental.pallas import tpu as pltpu

D_MODEL = 1024
DEPTH = 1
GRID_W = 64
N_HEADS_A = 4
DH_A = 64
DV_A = 2 * DH_A
QK_A = N_HEADS_A * 2 * DH_A
W_A = N_HEADS_A * DV_A
ROPE_BASE = 10000.0
N_HEADS_H = 4
DK_H = 128
DV_H = 128
QK_H = N_HEADS_H * DK_H
W_H = N_HEADS_H * DV_H
D_FF = 2816
MIX_IN = 2 * QK_A + W_A + 3 * QK_H + 2 * W_H + 2 * D_MODEL
ALPHA = (2 * DEPTH) ** 0.25
LN_EPS = 1e-5
RMS_EPS = 1e-6
Q_SCALE = DH_A ** -0.5 * math.log2(math.e)

F32 = jnp.float32
BF16 = jnp.bfloat16

LANES = 128
VMEM_LIMIT = 56 * 1024 * 1024

FFN_TM = 512
FFN_TF = 256
WIDE_CHUNK_ROWS = 128
SQUARE_CHUNK_ROWS = 256
MIX_TM = 4096
PROJ_TM = 2048
PROJ_SUB = 512
MIX_TN = 512
ATT_TQ = 1024
ATT_SUB = 512
ATT_SHORT_SEQ = 256
HGRN_C = 64
HGRN_ROWS = 512
HGRN_SHORT_HEADS = 4
HGRN_SAFE_DECAY = 75.0
MOD_ROWS = 8

W_Q, W_K, W_V, W_HQ, W_HFF, W_HFB, W_HI, W_HG, W_GA0, W_GA1, W_GH0, W_GH1 = range(12)


def _params(sem):
    return pltpu.CompilerParams(dimension_semantics=sem, vmem_limit_bytes=VMEM_LIMIT)


def _nt_dot(a, b):
    return lax.dot_general(a, b, (((1,), (1,)), ((), ())), preferred_element_type=F32)


def _tn_dot(a, b):
    return lax.dot_general(a, b, (((0,), (0,)), ((), ())), preferred_element_type=F32)


def _layer_norm(y, g, b):
    mu = jnp.mean(y, axis=-1, keepdims=True)
    yc = y - mu
    var = jnp.mean(yc * yc, axis=-1, keepdims=True)
    return yc * lax.rsqrt(var + LN_EPS) * g + b


def _mod_kernel(c_ref, w_ref, b_ref, o_ref):
    c = c_ref[...]
    a = (c * jax.nn.sigmoid(c)).astype(BF16)
    o_ref[...] = jnp.dot(a, w_ref[...].astype(BF16), preferred_element_type=F32) + b_ref[...]


def _modulation(cond, w_ada, b_ada):
    n_out = w_ada.shape[1]
    tn = D_MODEL
    out = pl.pallas_call(
        _mod_kernel,
        grid=(n_out // tn,),
        in_specs=[
            pl.BlockSpec((MOD_ROWS, D_MODEL), lambda n: (0, 0)),
            pl.BlockSpec((D_MODEL, tn), lambda n: (0, n)),
            pl.BlockSpec((1, tn), lambda n: (0, n)),
        ],
        out_specs=pl.BlockSpec((MOD_ROWS, tn), lambda n: (0, n)),
        out_shape=jax.ShapeDtypeStruct((MOD_ROWS, n_out), F32),
        compiler_params=_params(("arbitrary",)),
        name="modulation",
    )(cond, w_ada, b_ada.reshape(1, n_out))
    return out.reshape(MOD_ROWS, 9, D_MODEL)


def _load_weight(w_hbm, w_scr, stage, sems, src_col=0, dst_col=0):
    _, chunk, width = stage.shape
    n_chunks = w_hbm.shape[0] // chunk

    def copy(c):
        return pltpu.make_async_copy(w_hbm.at[pl.ds(c * chunk, chunk), pl.ds(src_col, width)],
                                     stage.at[c % 2], sems.at[c % 2])

    copy(0).start()
    for c in range(n_chunks):
        if c + 1 < n_chunks:
            copy(c + 1).start()
        copy(c).wait()
        w_scr[c * chunk:(c + 1) * chunk, dst_col:dst_col + width] = stage[c % 2].astype(BF16)


def _modulate(x, mod_ref, sub):
    return x * (1.0 + mod_ref[0, 3 * sub + 1:3 * sub + 2, :]) + mod_ref[0, 3 * sub:3 * sub + 1, :]


def _half_step(x, mod_ref, win_ref, wout_ref, lng_ref, lnb_ref, act_scr, sub):
    h = _modulate(x, mod_ref, sub).astype(BF16)
    for c in range(D_FF // FFN_TF):
        cols = slice(c * FFN_TF, (c + 1) * FFN_TF)
        up_cols = slice(D_FF + c * FFN_TF, D_FF + (c + 1) * FFN_TF)
        gate = jnp.dot(h, win_ref[:, cols], preferred_element_type=F32)
        up = jnp.dot(h, win_ref[:, up_cols], preferred_element_type=F32)
        act_scr[:, cols] = (gate * jax.nn.sigmoid(gate) * up).astype(BF16)
    f = jnp.dot(act_scr[...], wout_ref[...], preferred_element_type=F32)
    g = mod_ref[0, 3 * sub + 2:3 * sub + 3, :]
    return _layer_norm(ALPHA * x + 0.5 * g * f, lng_ref[sub:sub + 1, :], lnb_ref[sub:sub + 1, :])


def _ffn_weight_scratch():
    return [
        pltpu.VMEM((D_MODEL, 2 * D_FF), BF16),
        pltpu.VMEM((D_FF, D_MODEL), BF16),
        pltpu.VMEM((2, WIDE_CHUNK_ROWS, 2 * D_FF), F32),
        pltpu.VMEM((2, SQUARE_CHUNK_ROWS, D_MODEL), F32),
        pltpu.SemaphoreType.DMA((2,)),
        pltpu.SemaphoreType.DMA((2,)),
    ]


def _ffn1_kernel(xa_ref, xb_ref, mod_ref, win_hbm, wout_hbm, lng_ref, lnb_ref, xo_ref, ho_ref,
                 win_scr, wout_scr, wide_stage, square_stage, wide_sems, square_sems, act_scr, *, tiles_a):
    i = pl.program_id(0)

    @pl.when(i == 0)
    def _():
        _load_weight(win_hbm, win_scr, wide_stage, wide_sems)
        _load_weight(wout_hbm, wout_scr, square_stage, square_sems)

    x = jnp.where(i < tiles_a, xa_ref[...], xb_ref[...])
    xn = _half_step(x, mod_ref, win_scr, wout_scr, lng_ref, lnb_ref, act_scr, 0)
    xo_ref[...] = xn
    ho_ref[...] = _modulate(xn, mod_ref, 1).astype(BF16)


def _ffn1(xa, xb, mod, cond_of_tile, w_in, w_out, ln_g, ln_b):
    tiles_a, tiles_b = xa.shape[0] // FFN_TM, xb.shape[0] // FFN_TM
    m = xa.shape[0] + xb.shape[0]
    row = lambda i: (i, 0)
    const = lambda i: (0, 0)
    hbm = pl.BlockSpec(memory_space=pl.ANY)
    return pl.pallas_call(
        functools.partial(_ffn1_kernel, tiles_a=tiles_a),
        grid=(tiles_a + tiles_b,),
        in_specs=[
            pl.BlockSpec((FFN_TM, D_MODEL), lambda i: (jnp.minimum(i, tiles_a - 1), 0)),
            pl.BlockSpec((FFN_TM, D_MODEL), lambda i: (jnp.maximum(i - tiles_a, 0), 0)),
            pl.BlockSpec((1, 9, D_MODEL), lambda i: (cond_of_tile(i, FFN_TM), 0, 0)),
            hbm, hbm,
            pl.BlockSpec((3, D_MODEL), const),
            pl.BlockSpec((3, D_MODEL), const),
        ],
        out_specs=[pl.BlockSpec((FFN_TM, D_MODEL), row), pl.BlockSpec((FFN_TM, D_MODEL), row)],
        out_shape=[jax.ShapeDtypeStruct((m, D_MODEL), F32), jax.ShapeDtypeStruct((m, D_MODEL), BF16)],
        scratch_shapes=_ffn_weight_scratch() + [pltpu.VMEM((FFN_TM, D_FF), BF16)],
        compiler_params=_params(("arbitrary",)),
        name="ffn1",
    )(xa, xb, mod, w_in, w_out, ln_g, ln_b)


def _rope(x, cos, sin):
    half, quarter = DH_A // 2, DH_A // 4
    lane = lax.broadcasted_iota(jnp.int32, (1, LANES), 1)
    first_quarter = (lane % half) < quarter
    outs = []
    for hb in range(QK_A // LANES):
        xb = x[:, hb * LANES:(hb + 1) * LANES]
        ahead = pltpu.roll(xb, LANES - quarter, 1)
        behind = pltpu.roll(xb, quarter, 1)
        rot = jnp.where(first_quarter, -ahead, behind)
        outs.append(xb * cos + rot * sin)
    return jnp.concatenate(outs, axis=-1)


def _forget_gate(z, logits):
    mx = jnp.max(logits, axis=0, keepdims=True)
    el = jnp.exp(logits - mx)
    lb = el[0:1] / jnp.sum(el, axis=0, keepdims=True)
    sig = 0.5 + 0.5 * jnp.tanh(0.5 * z)
    u = (1.0 - lb) * sig
    return jnp.log(lb + u), (1.0 - lb) - u


def _proj_kernel(*refs, rope, want_kv32):
    refs = list(refs)
    h_ref, w_ref, lbl_ref = refs[:3]
    refs = refs[3:]
    cos_ref, sin_ref = (refs.pop(0), refs.pop(0)) if rope else (None, None)
    qkv_ref = refs.pop(0)
    kv32_ref = refs.pop(0) if want_kv32 else None
    plain_ref, g_ref, kd_ref = refs
    n = pl.program_id(1)

    pieces = [slice(r, r + PROJ_SUB) for r in range(0, h_ref.shape[0], PROJ_SUB)]

    def project(rows, w):
        return jnp.dot(h_ref[rows, :], w, preferred_element_type=F32)

    def store_heads(ref, rows, y):
        for hh in range(MIX_TN // LANES):
            ref[hh, rows, :] = y[:, hh * LANES:(hh + 1) * LANES].astype(ref.dtype)

    @pl.when(n <= W_V)
    def _():
        w = w_ref[...].astype(BF16)
        for rows in pieces:
            y = project(rows, w)
            if want_kv32:
                kv32_ref[0, rows, :] = y
            if rope:
                first = rows.start % cos_ref.shape[1]
                table_rows = slice(first, first + PROJ_SUB)
                y = _rope(y, cos_ref[0, table_rows, :], sin_ref[0, table_rows, :])
            else:
                y = y * jnp.where(n == W_Q, Q_SCALE, 1.0)
            store_heads(qkv_ref, rows, y)

    @pl.when((n == W_HFF) | (n == W_HFB))
    def _():
        w = w_ref[...].astype(BF16)
        for rows in pieces:
            log_f, kd = _forget_gate(project(rows, w), lbl_ref[0])
            store_heads(g_ref, rows, log_f)
            store_heads(kd_ref, rows, kd)

    @pl.when((n == W_HQ) | (n >= W_HI))
    def _():
        w = w_ref[...].astype(BF16)
        for rows in pieces:
            store_heads(plain_ref, rows, project(rows, w))


def _project(h, group, w, lb_logits, rope_tables, want_kv32):
    m = MIX_TM
    tm = PROJ_TM
    tiles = m // tm
    rope = rope_tables is not None
    heads = MIX_TN // LANES
    n_blocks = W_HG + 1
    in_specs = [pl.BlockSpec((tm, D_MODEL), lambda i, n: (group * tiles + i, 0)),
                pl.BlockSpec((D_MODEL, MIX_TN), lambda i, n: (0, n)),
                pl.BlockSpec((1, DEPTH + 1, QK_H), lambda i, n: (jnp.clip(n - W_HFF, 0, 1), 0, 0))]
    args = [h, w, lb_logits]
    qkv_col = lambda n: jnp.minimum(n, W_V)
    if rope:
        t_rope = rope_tables[0].shape[1]
        rope_rows = min(t_rope, tm)
        n_rope_tiles = t_rope // rope_rows
        for t in rope_tables:
            in_specs.append(pl.BlockSpec((1, rope_rows, DV_A), lambda i, n: (qkv_col(n), i % n_rope_tiles, 0)))
            args.append(t)
    head_block = (heads, tm, LANES)
    plain_col = lambda n: jnp.where(n < W_HI, 0, n - W_HI + 1)
    gate_col = lambda n: jnp.clip(n - W_HFF, 0, 1)
    out_specs = [pl.BlockSpec(head_block, lambda i, n: (qkv_col(n), i, 0))]
    out_shape = [jax.ShapeDtypeStruct((3 * heads, m, LANES), BF16)]
    if want_kv32:
        out_specs.append(pl.BlockSpec((1, tm, MIX_TN), lambda i, n: (jnp.clip(n - W_K, 0, 1), i, 0)))
        out_shape.append(jax.ShapeDtypeStruct((2, m, MIX_TN), F32))
    out_specs += [pl.BlockSpec(head_block, lambda i, n: (plain_col(n), i, 0)),
                  pl.BlockSpec(head_block, lambda i, n: (gate_col(n), i, 0)),
                  pl.BlockSpec(head_block, lambda i, n: (gate_col(n), i, 0))]
    out_shape += [jax.ShapeDtypeStruct((3 * heads, m, LANES), BF16),
                  jax.ShapeDtypeStruct((2 * heads, m, LANES), F32),
                  jax.ShapeDtypeStruct((2 * heads, m, LANES), BF16)]
    return pl.pallas_call(
        functools.partial(_proj_kernel, rope=rope, want_kv32=want_kv32),
        grid=(tiles, n_blocks),
        in_specs=in_specs,
        out_specs=out_specs,
        out_shape=out_shape,
        compiler_params=_params(("arbitrary", "arbitrary")),
        name="mix_proj",
    )(*args)


def _attn_kernel(*refs, lam_init, has_ctx):
    if has_ctx:
        lam_ref, g_ref, q_ref, k_ref, v_ref, kc_ref, vc_ref, o_ref = refs
    else:
        lam_ref, g_ref, q_ref, k_ref, v_ref, o_ref = refs
    lp = lam_ref[...]
    lam = (jnp.exp(jnp.sum(lp[0:1] * lp[1:2], axis=-1, keepdims=True))
           - jnp.exp(jnp.sum(lp[2:3] * lp[3:4], axis=-1, keepdims=True)) + lam_init)

    lane = lax.broadcasted_iota(jnp.int32, (1, DV_A), 1)
    comp0 = lane < DH_A
    tq = q_ref.shape[2]
    sub = min(ATT_SUB, tq)
    problems = []
    for hh in range(q_ref.shape[0]):
        keys = [k_ref[hh, 0]]
        vals = [v_ref[hh, 0]]
        if has_ctx:
            keys.append(kc_ref[0, :, hh * DV_A:(hh + 1) * DV_A].astype(BF16))
            vals.append(vc_ref[0, :, hh * DV_A:(hh + 1) * DV_A].astype(BF16))
        for r in range(tq // sub):
            q = q_ref[hh, 0, r * sub:(r + 1) * sub, :]
            zero = jnp.zeros_like(q)
            for qc in (jnp.where(comp0, q, zero), jnp.where(comp0, zero, q)):
                problems.append((qc, keys, vals))
    scores = [[_nt_dot(kk, qc) for kk in keys] for qc, keys, _ in problems]
    weights = []
    for ss in scores:
        mx = functools.reduce(jnp.maximum, [jnp.max(s, axis=0, keepdims=True) for s in ss])
        es = [jnp.exp2(s - mx) for s in ss]
        den = functools.reduce(jnp.add, [jnp.sum(e, axis=0, keepdims=True) for e in es])
        weights.append(([e.astype(BF16) for e in es], 1.0 / den))
    maps = [functools.reduce(jnp.add, [_tn_dot(vv, e) for e, vv in zip(es, vals)]) * r
            for (es, r), (_, _, vals) in zip(weights, problems)]
    idx = 0
    for hh in range(q_ref.shape[0]):
        for r in range(tq // sub):
            o = maps[idx] - lam * maps[idx + 1]
            idx += 2
            o = o * lax.rsqrt(jnp.mean(o * o, axis=0, keepdims=True) + RMS_EPS)
            o = o.T * g_ref[...]
            o_ref[hh, 0, r * sub:(r + 1) * sub, :] = (o * (1.0 - lam_init)).astype(o_ref.dtype)


def _attention(q, k, v, ctx_kv, lam_params, subln_g, lam_init):
    (q_arr, q_row), (k_arr, k_row), (v_arr, v_row) = q, k, v
    _, bsz, t, _ = q_arr.shape
    tq = min(ATT_TQ, t)
    hp = N_HEADS_A if t <= ATT_SHORT_SEQ else 1
    const = lambda b, h, i: (0, 0)
    rows = lambda first: (lambda b, h, i: (first // hp + h, b, 0, 0))
    in_specs = [
        pl.BlockSpec((4, DH_A), const),
        pl.BlockSpec((1, DV_A), const),
        pl.BlockSpec((hp, 1, tq, DV_A), lambda b, h, i: (q_row // hp + h, b, i, 0)),
        pl.BlockSpec((hp, 1, t, DV_A), rows(k_row)),
        pl.BlockSpec((hp, 1, t, DV_A), rows(v_row)),
    ]
    args = [lam_params, subln_g, q_arr, k_arr, v_arr]
    if ctx_kv is not None:
        p = ctx_kv[0].shape[1]
        head_all = lambda b, h, i: (b, 0, h)
        in_specs += [pl.BlockSpec((1, p, hp * DV_A), head_all), pl.BlockSpec((1, p, hp * DV_A), head_all)]
        args += list(ctx_kv)
    return pl.pallas_call(
        functools.partial(_attn_kernel, lam_init=lam_init, has_ctx=ctx_kv is not None),
        grid=(bsz, N_HEADS_A // hp, t // tq),
        in_specs=in_specs,
        out_specs=pl.BlockSpec((hp, 1, tq, DV_A), lambda b, h, i: (h, b, i, 0)),
        out_shape=jax.ShapeDtypeStruct((N_HEADS_A, bsz, t, DV_A), BF16),
        compiler_params=_params(("parallel", "parallel", "arbitrary")),
        name="attention",
    )(*args)


def _chunk_scan(g, reverse):
    r = g.shape[0]
    pos = lax.broadcasted_iota(jnp.int32, (r, 1), 0) % HGRN_C
    b = g
    sh = 1
    while sh < HGRN_C:
        if reverse:
            b = b + jnp.where(pos < HGRN_C - sh, pltpu.roll(b, r - sh, 0), 0.0)
        else:
            b = b + jnp.where(pos >= sh, pltpu.roll(b, sh, 0), 0.0)
        sh *= 2
    return b


def _hgrn_kernel(*refs, seq_len, has_s0, want_state):
    refs = list(refs)
    hq_ref, hi_ref, hg_ref, gf_ref, gb_ref, kf_ref, kb_ref, ng_ref = refs[:8]
    refs = refs[8:]
    s0_ref = refs.pop(0) if has_s0 else None
    o_ref = refs.pop(0)
    sfin_ref = refs.pop(0) if want_state else None
    od_scr, st_scr, b_scr, q_scr, k_scr, ok_scr = refs
    hp = hq_ref.shape[0]
    c = HGRN_C
    rb = min(HGRN_ROWS, seq_len)
    nb = seq_len // rb
    cpb = rb // c
    g_refs = (gf_ref, gb_ref)
    k_refs = (kf_ref, kb_ref)

    def queries(hq):
        hq = hq.astype(F32)
        return hq * jax.nn.sigmoid(hq)

    def flags(i, carry):
        rows = pl.ds(pl.multiple_of(i * rb, rb), rb)
        for d in range(2):
            worst = jnp.zeros((1, DK_H), F32)
            for hh in range(hp):
                g = g_refs[d][hh, 0, rows, :]
                for j in range(cpb):
                    worst = jnp.maximum(worst, -jnp.sum(g[j * c:(j + 1) * c], axis=0, keepdims=True))
            ok_scr[d, i] = (jnp.max(worst) <= HGRN_SAFE_DECAY).astype(jnp.int32)
        return carry

    lax.fori_loop(0, nb, flags, 0)

    for hh in range(hp):
        for d in range(2):
            if has_s0:
                st_scr[hh, d] = s0_ref[0, d, hh].T
            else:
                st_scr[hh, d] = jnp.zeros((DV_H, DK_H), F32)

    row = lax.broadcasted_iota(jnp.int32, (c, 1), 0)
    col = lax.broadcasted_iota(jnp.int32, (1, c), 1)

    def fast_block(d, r0):
        reverse = d == 1
        rows = pl.ds(r0, rb)
        n = hp * cpb
        flat = lambda a: a.reshape(hp * rb, a.shape[-1])
        chunks = lambda a: a.reshape(n, c, a.shape[-1])
        vb = chunks(hi_ref[:, 0, rows, :])
        b = _chunk_scan(flat(g_refs[d][:, 0, rows, :]), reverse)
        qd = chunks((queries(flat(hq_ref[:, 0, rows, :])) * jnp.exp(b)).astype(BF16))
        ku = chunks(flat(k_refs[d][:, 0, rows, :]).astype(F32) * jnp.exp(-b))
        b = chunks(b)
        e_end = jnp.exp(b[:, 0:1, :] if reverse else b[:, c - 1:c, :])
        causal = ((row <= col) if reverse else (row >= col))[None]
        a = jnp.einsum('nck,nsk->ncs', qd, ku.astype(BF16), preferred_element_type=F32)
        o = jnp.einsum('ncs,nsv->ncv', jnp.where(causal, a, 0.0).astype(BF16), vb, preferred_element_type=F32)
        u = jnp.einsum('nsv,nsk->nvk', vb, (ku * e_end).astype(BF16), preferred_element_type=F32)
        entering = [None] * n
        for hh in range(hp):
            st = st_scr[hh, d]
            for j in (range(cpb - 1, -1, -1) if reverse else range(cpb)):
                entering[hh * cpb + j] = st.astype(BF16)
                st = e_end[hh * cpb + j] * st + u[hh * cpb + j]
            st_scr[hh, d] = st
        o = o + jnp.einsum('nck,nvk->ncv', qd, jnp.stack(entering), preferred_element_type=F32)
        od_scr[:, d, rows, :] = o.reshape(hp, rb, DV_H)

    def exact_chunk(hh, d, r0):
        reverse = d == 1
        rows = pl.ds(r0, c)
        b = _chunk_scan(g_refs[d][hh, 0, rows, :], reverse)
        b_scr[...] = b
        q_scr[...] = queries(hq_ref[hh, 0, rows, :])
        k_scr[...] = k_refs[d][hh, 0, rows, :].astype(F32)

        def one_row(t, carry):
            bt = b_scr[pl.ds(t, 1), :]
            qt = q_scr[pl.ds(t, 1), :]
            seen = (row >= t) if reverse else (row <= t)
            w = jnp.exp(jnp.where(seen, bt - b_scr[...], -jnp.inf))
            p = jnp.sum(qt * k_scr[...] * w, axis=-1, keepdims=True)
            od_scr[hh, d, pl.ds(r0 + t, 1), :] = jnp.sum(p * hi_ref[hh, 0, rows, :].astype(F32),
                                                         axis=0, keepdims=True)
            return carry

        lax.fori_loop(0, c, one_row, 0)
        b_end = b[0:1] if reverse else b[c - 1:c]
        st = st_scr[hh, d]
        qd = (q_scr[...] * jnp.exp(b)).astype(BF16)
        od_scr[hh, d, rows, :] = od_scr[hh, d, rows, :] + _nt_dot(qd, st.astype(BF16))
        k_end = (k_scr[...] * jnp.exp(b_end - b)).astype(BF16)
        st_scr[hh, d] = jnp.exp(b_end) * st + _tn_dot(hi_ref[hh, 0, rows, :], k_end)

    def block(i, carry):
        starts = (pl.multiple_of(i * rb, rb), pl.multiple_of((nb - 1 - i) * rb, rb))
        mild = (ok_scr[0, i] + ok_scr[1, nb - 1 - i]) == 2

        @pl.when(mild)
        def _():
            for d in range(2):
                fast_block(d, starts[d])

        @pl.when(jnp.logical_not(mild))
        def _():
            for hh in range(hp):
                for d in range(2):
                    for j in (range(cpb - 1, -1, -1) if d == 1 else range(cpb)):
                        exact_chunk(hh, d, pl.multiple_of(starts[d] + j * c, c))

        return carry

    lax.fori_loop(0, nb, block, 0)

    def finish(i, carry):
        rows = pl.ds(pl.multiple_of(i * rb, rb), rb)
        for hh in range(hp):
            o = od_scr[hh, 0, rows, :] + od_scr[hh, 1, rows, :]
            o = o * lax.rsqrt(jnp.mean(o * o, axis=-1, keepdims=True) + RMS_EPS) * ng_ref[...]
            hg = hg_ref[hh, 0, rows, :].astype(F32)
            o_ref[hh, 0, rows, :] = (o * (hg * jax.nn.sigmoid(hg))).astype(o_ref.dtype)
        return carry

    lax.fori_loop(0, nb, finish, 0)
    if want_state:
        for hh in range(hp):
            for d in range(2):
                sfin_ref[0, d, hh] = st_scr[hh, d].T


def _hgrn(plain, g, kd, norm_g, s0, want_state):
    _, bsz, t, _ = plain.shape
    hp = HGRN_SHORT_HEADS if t <= HGRN_ROWS else 1
    stream = lambda s: pl.BlockSpec((hp, 1, t, DK_H), lambda b, h: (s * N_HEADS_H // hp + h, b, 0, 0))
    in_specs = [stream(0), stream(1), stream(2), stream(0), stream(1), stream(0), stream(1),
                pl.BlockSpec((1, DV_H), lambda b, h: (0, 0))]
    args = [plain, plain, plain, g, g, kd, kd, norm_g]
    state_spec = pl.BlockSpec((1, 2, hp, DK_H, DV_H), lambda b, h: (b, 0, h, 0, 0))
    if s0 is not None:
        in_specs.append(state_spec)
        args.append(s0)
    out_specs = [pl.BlockSpec((hp, 1, t, DV_H), lambda b, h: (h, b, 0, 0))]
    out_shape = [jax.ShapeDtypeStruct((N_HEADS_H, bsz, t, DV_H), BF16)]
    if want_state:
        out_specs.append(state_spec)
        out_shape.append(jax.ShapeDtypeStruct((bsz, 2, N_HEADS_H, DK_H, DV_H), F32))
    n_blocks = t // min(HGRN_ROWS, t)
    chunk = pltpu.VMEM((HGRN_C, DK_H), F32)
    outs = pl.pallas_call(
        functools.partial(_hgrn_kernel, seq_len=t, has_s0=s0 is not None, want_state=want_state),
        grid=(bsz, N_HEADS_H // hp),
        in_specs=in_specs,
        out_specs=out_specs,
        out_shape=out_shape,
        scratch_shapes=[
            pltpu.VMEM((hp, 2, t, DV_H), F32),
            pltpu.VMEM((hp, 2, DV_H, DK_H), F32),
            chunk, chunk, chunk,
            pltpu.SMEM((2, n_blocks), jnp.int32),
        ],
        compiler_params=_params(("parallel", "parallel")),
        name="hgrn",
    )(*args)
    return outs if want_state else (outs[0], None)


def _merge_ffn2_kernel(oa0_ref, oh0_ref, oa1_ref, oh1_ref, h_ref, x_ref, mod_ref,
                       wmix_hbm, wa_hbm, wh_hbm, wo_hbm, win_hbm, wout_hbm, lng_ref, lnb_ref, y0_ref, y1_ref,
                       wg_scr, wa_scr, wh_scr, wo_scr,
                       win_scr, wout_scr, wide_stage, square_stage, wide_sems, square_sems, act_scr, *, tiles_0):
    i = pl.program_id(0)
    first_group = i < tiles_0

    @pl.when(i == 0)
    def _():
        _load_weight(win_hbm, win_scr, wide_stage, wide_sems)
        for half in range(2):
            _load_weight(wmix_hbm, wg_scr, square_stage, square_sems,
                         src_col=W_GA0 * MIX_TN + half * D_MODEL, dst_col=half * D_MODEL)
        for w_hbm, w_scr in ((wa_hbm, wa_scr), (wh_hbm, wh_scr), (wo_hbm, wo_scr), (wout_hbm, wout_scr)):
            _load_weight(w_hbm, w_scr, square_stage, square_sems)

    def heads(ref0, ref1):
        pick = lambda hh: jnp.where(first_group, ref0[hh], ref1[hh])
        return jnp.concatenate([pick(hh) for hh in range(ref0.shape[0])], axis=-1)

    gates = jax.nn.sigmoid(jnp.dot(h_ref[...], wg_scr[...], preferred_element_type=F32))
    ba = jnp.dot(heads(oa0_ref, oa1_ref), wa_scr[...], preferred_element_type=F32)
    bh = jnp.dot(heads(oh0_ref, oh1_ref), wh_scr[...], preferred_element_type=F32)
    merged = gates[:, :D_MODEL] * ba + gates[:, D_MODEL:] * bh
    mix = jnp.dot(merged.astype(BF16), wo_scr[...], preferred_element_type=F32)
    x2 = _layer_norm(ALPHA * x_ref[...] + mod_ref[0, 5:6, :] * mix, lng_ref[1:2, :], lnb_ref[1:2, :])
    y = _half_step(x2, mod_ref, win_scr, wout_scr, lng_ref, lnb_ref, act_scr, 2)

    @pl.when(first_group)
    def _():
        y0_ref[...] = y

    @pl.when(jnp.logical_not(first_group))
    def _():
        y1_ref[...] = y


def _merge_ffn2(mixed, h, x, mod, cond_of_tile, w_mix, w_a, w_h, w_o, w_in, w_out, ln_g, ln_b):
    (oa0, oh0), (oa1, oh1) = mixed
    tiles_0, tiles_1 = oa0.shape[1] // FFN_TM, oa1.shape[1] // FFN_TM
    row = lambda i: (i, 0)
    const = lambda i: (0, 0)
    in_group_0 = lambda i: jnp.minimum(i, tiles_0 - 1)
    in_group_1 = lambda i: jnp.maximum(i - tiles_0, 0)
    hbm = pl.BlockSpec(memory_space=pl.ANY)
    return pl.pallas_call(
        functools.partial(_merge_ffn2_kernel, tiles_0=tiles_0),
        grid=(tiles_0 + tiles_1,),
        in_specs=[
            pl.BlockSpec((N_HEADS_A, FFN_TM, DV_A), lambda i: (0, in_group_0(i), 0)),
            pl.BlockSpec((N_HEADS_H, FFN_TM, DV_H), lambda i: (0, in_group_0(i), 0)),
            pl.BlockSpec((N_HEADS_A, FFN_TM, DV_A), lambda i: (0, in_group_1(i), 0)),
            pl.BlockSpec((N_HEADS_H, FFN_TM, DV_H), lambda i: (0, in_group_1(i), 0)),
            pl.BlockSpec((FFN_TM, D_MODEL), row),
            pl.BlockSpec((FFN_TM, D_MODEL), row),
            pl.BlockSpec((1, 9, D_MODEL), lambda i: (cond_of_tile(i, FFN_TM), 0, 0)),
            hbm, hbm, hbm, hbm, hbm, hbm,
            pl.BlockSpec((3, D_MODEL), const),
            pl.BlockSpec((3, D_MODEL), const),
        ],
        out_specs=[pl.BlockSpec((FFN_TM, D_MODEL), lambda i: (in_group_0(i), 0)),
                   pl.BlockSpec((FFN_TM, D_MODEL), lambda i: (in_group_1(i), 0))],
        out_shape=[jax.ShapeDtypeStruct((tiles_0 * FFN_TM, D_MODEL), F32),
                   jax.ShapeDtypeStruct((tiles_1 * FFN_TM, D_MODEL), F32)],
        scratch_shapes=[pltpu.VMEM((D_MODEL, 2 * D_MODEL), BF16), pltpu.VMEM((W_A, D_MODEL), BF16),
                        pltpu.VMEM((W_H, D_MODEL), BF16), pltpu.VMEM((D_MODEL, D_MODEL), BF16)]
                       + _ffn_weight_scratch() + [pltpu.VMEM((FFN_TM, D_FF), BF16)],
        compiler_params=_params(("arbitrary",)),
        name="merge_ffn2",
    )(oa0, oh0, oa1, oh1, h, x, mod, w_mix, w_a, w_h, w_o, w_in, w_out, ln_g, ln_b)


def _rope_tables(n_tok):
    rows = n_tok // GRID_W
    row = np.repeat(np.arange(rows, dtype=np.float32), GRID_W)
    col = np.tile(np.arange(GRID_W, dtype=np.float32), rows)
    half = DH_A // 2
    inv = (ROPE_BASE ** (-np.arange(0, half, 2, dtype=np.float32) / half)).astype(np.float32)
    ar = row[:, None] * inv
    ac = col[:, None] * inv
    ang = np.concatenate([ar, ar, ac, ac] * 2, axis=-1)
    cos, sin = np.cos(ang).astype(np.float32), np.sin(ang).astype(np.float32)
    scale = np.float32(Q_SCALE)
    return (jnp.asarray(np.stack([cos * scale, cos, np.ones_like(cos)])),
            jnp.asarray(np.stack([sin * scale, sin, np.zeros_like(sin)])))


def _mixer(h1, group, wts, lam_init, bsz, ctx):
    m = MIX_TM
    t = m // bsz
    latent = ctx is not None
    per_head = lambda a: a.reshape(a.shape[0], bsz, t, a.shape[-1])
    if latent:
        qkv, plain, g, kd = _project(h1, group, wts['w_mix_in'], wts['lb_logits'], _rope_tables(t), False)
        k = v = None
        ctx_kv = (ctx[0], ctx[1])
    else:
        qkv, kv32, plain, g, kd = _project(h1, group, wts['w_mix_in'], wts['lb_logits'], None, True)
        k, v = kv32[0], kv32[1]
        ctx_kv = None
    qkv = per_head(qkv)
    q_kv = [(qkv, W_Q * N_HEADS_A), (qkv, W_K * N_HEADS_A), (qkv, W_V * N_HEADS_A)]
    o_a = _attention(*q_kv, ctx_kv, wts['lam_params'], wts['attn_subln_g'], lam_init)
    o_h, s_fin = _hgrn(per_head(plain), per_head(g), per_head(kd), wts['hgrn_norm_g'],
                       ctx[2] if latent else None, not latent)
    return (o_a.reshape(N_HEADS_A, m, DV_A), o_h.reshape(N_HEADS_H, m, DV_H)), (k, v, s_fin)


def kernel(x_prompt, x_sample, cache_k, cache_v, state_hgrn, c, c_ctx, w_ada, b_ada, ffn1_w_in, ffn1_w_out,
           w_mix_in, lambda_q1, lambda_k1, lambda_q2, lambda_k2, attn_subln_g, hgrn_lb_logits, hgrn_norm_g,
           w_branch_a, w_branch_h, w_mix_out, ffn2_w_in, ffn2_w_out, ln_g, ln_b):
    batch, seq, _ = x_prompt.shape
    dec_batch, dec_seq, _ = x_sample.shape
    past = cache_k.shape[2]
    assert DEPTH == 1 and dec_batch + 1 <= MOD_ROWS
    lam_init = 0.8 - 0.6 * math.exp(-0.3 * 0)

    cond = jnp.concatenate([c_ctx[None, :], c, jnp.zeros((MOD_ROWS - 1 - dec_batch, D_MODEL), F32)], axis=0)
    mod = _modulation(cond, w_ada[0], b_ada[0])

    wts = {
        'w_mix_in': w_mix_in[0],
        'lam_params': jnp.concatenate([lambda_q1, lambda_k1, lambda_q2, lambda_k2], axis=0),
        'attn_subln_g': attn_subln_g, 'lb_logits': hgrn_lb_logits, 'hgrn_norm_g': hgrn_norm_g,
    }
    n_ctx = batch * seq
    assert n_ctx == MIX_TM and dec_batch * dec_seq == MIX_TM

    def cond_of_tile(i, tm):
        first = i * tm
        return jnp.where(first < n_ctx, 0, 1 + (first - n_ctx) // dec_seq)

    x1, h1 = _ffn1(x_prompt.reshape(n_ctx, D_MODEL), x_sample.reshape(dec_batch * dec_seq, D_MODEL), mod,
                   cond_of_tile, ffn1_w_in[0], ffn1_w_out[0], ln_g[0], ln_b[0])
    mixed_p, (k_c, v_c, s_c) = _mixer(h1, 0, wts, lam_init, batch, None)
    ctx = (cache_k[:, 0].reshape(dec_batch, past, QK_A), cache_v[:, 0].reshape(dec_batch, past, W_A),
           state_hgrn[:, 0])
    mixed_s, _ = _mixer(h1, 1, wts, lam_init, dec_batch, ctx)
    y_p, y_s = _merge_ffn2((mixed_p, mixed_s), h1, x1, mod, cond_of_tile, w_mix_in[0],
                           w_branch_a[0], w_branch_h[0], w_mix_out[0], ffn2_w_in[0], ffn2_w_out[0],
                           ln_g[0], ln_b[0])

    return (y_p.reshape(batch, seq, D_MODEL),
            y_s.reshape(dec_batch, dec_seq, D_MODEL),
            k_c.reshape(batch, DEPTH, seq, N_HEADS_A, 2, DH_A),
            v_c.reshape(batch, DEPTH, seq, N_HEADS_A, DV_A),
            s_c.reshape(batch, DEPTH, 2, N_HEADS_H, DK_H, DV_H))
```

```python
import functools
import math

import numpy as np
import jax
import jax.numpy as jnp
from jax import lax
from jax.experimental import pallas as pl
from jax.experimental.pallas import tpu as pltpu

D_MODEL = 1024
DEPTH = 1
GRID_W = 64
N_HEADS_A = 4
DH_A = 64
DV_A = 2 * DH_A
QK_A = N_HEADS_A * 2 * DH_A
W_A = N_HEADS_A * DV_A
ROPE_BASE = 10000.0
N_HEADS_H = 4
DK_H = 128
DV_H = 128
QK_H = N_HEADS_H * DK_H
W_H = N_HEADS_H * DV_H
D_FF = 2816
MIX_IN = 2 * QK_A + W_A + 3 * QK_H + 2 * W_H + 2 * D_MODEL
ALPHA = (2 * DEPTH) ** 0.25
LN_EPS = 1e-5
RMS_EPS = 1e-6
Q_SCALE = DH_A ** -0.5 * math.log2(math.e)

F32 = jnp.float32
BF16 = jnp.bfloat16

LANES = 128
VMEM_LIMIT = 56 * 1024 * 1024

FFN_TM = 512
FFN_TF = 256
WIDE_CHUNK_ROWS = 128
SQUARE_CHUNK_ROWS = 256
MIX_TM = 4096
PROJ_TM = 2048
PROJ_SUB = 256
MIX_TN = 512
ATT_TQ = 1024
ATT_SUB = 512
ATT_SHORT_SEQ = 256
HGRN_C = 64
HGRN_ROWS = 1024
HGRN_SHORT_HEADS = 4
HGRN_SAFE_DECAY = 75.0
MOD_ROWS = 8

W_Q, W_K, W_V, W_HQ, W_HFF, W_HFB, W_HI, W_HG, W_GA0, W_GA1, W_GH0, W_GH1 = range(12)


def _params(sem):
    return pltpu.CompilerParams(dimension_semantics=sem, vmem_limit_bytes=VMEM_LIMIT)


def _nt_dot(a, b):
    return lax.dot_general(a, b, (((1,), (1,)), ((), ())), preferred_element_type=F32)


def _tn_dot(a, b):
    return lax.dot_general(a, b, (((0,), (0,)), ((), ())), preferred_element_type=F32)


def _layer_norm(y, g, b):
    mu = jnp.mean(y, axis=-1, keepdims=True)
    yc = y - mu
    var = jnp.mean(yc * yc, axis=-1, keepdims=True)
    return yc * lax.rsqrt(var + LN_EPS) * g + b


def _mod_kernel(c_ref, w_ref, b_ref, o_ref):
    c = c_ref[...]
    a = (c * jax.nn.sigmoid(c)).astype(BF16)
    o_ref[...] = jnp.dot(a, w_ref[...].astype(BF16), preferred_element_type=F32) + b_ref[...]


def _modulation(cond, w_ada, b_ada):
    n_out = w_ada.shape[1]
    tn = D_MODEL
    out = pl.pallas_call(
        _mod_kernel,
        grid=(n_out // tn,),
        in_specs=[
            pl.BlockSpec((MOD_ROWS, D_MODEL), lambda n: (0, 0)),
            pl.BlockSpec((D_MODEL, tn), lambda n: (0, n)),
            pl.BlockSpec((1, tn), lambda n: (0, n)),
        ],
        out_specs=pl.BlockSpec((MOD_ROWS, tn), lambda n: (0, n)),
        out_shape=jax.ShapeDtypeStruct((MOD_ROWS, n_out), F32),
        compiler_params=_params(("arbitrary",)),
        name="modulation",
    )(cond, w_ada, b_ada.reshape(1, n_out))
    return out.reshape(MOD_ROWS, 9, D_MODEL)


def _load_weight(w_hbm, w_scr, stage, sems, src_col=0, dst_col=0):
    _, chunk, width = stage.shape
    n_chunks = w_hbm.shape[0] // chunk

    def copy(c):
        return pltpu.make_async_copy(w_hbm.at[pl.ds(c * chunk, chunk), pl.ds(src_col, width)],
                                     stage.at[c % 2], sems.at[c % 2])

    copy(0).start()
    for c in range(n_chunks):
        if c + 1 < n_chunks:
            copy(c + 1).start()
        copy(c).wait()
        w_scr[c * chunk:(c + 1) * chunk, dst_col:dst_col + width] = stage[c % 2].astype(BF16)


def _modulate(x, mod_ref, sub):
    return x * (1.0 + mod_ref[0, 3 * sub + 1:3 * sub + 2, :]) + mod_ref[0, 3 * sub:3 * sub + 1, :]


def _half_step(x, mod_ref, win_ref, wout_ref, lng_ref, lnb_ref, act_scr, sub):
    h = _modulate(x, mod_ref, sub).astype(BF16)
    for c in range(D_FF // FFN_TF):
        cols = slice(c * FFN_TF, (c + 1) * FFN_TF)
        up_cols = slice(D_FF + c * FFN_TF, D_FF + (c + 1) * FFN_TF)
        gate = jnp.dot(h, win_ref[:, cols], preferred_element_type=F32)
        up = jnp.dot(h, win_ref[:, up_cols], preferred_element_type=F32)
        act_scr[:, cols] = (gate * jax.nn.sigmoid(gate) * up).astype(BF16)
    f = jnp.dot(act_scr[...], wout_ref[...], preferred_element_type=F32)
    g = mod_ref[0, 3 * sub + 2:3 * sub + 3, :]
    return _layer_norm(ALPHA * x + 0.5 * g * f, lng_ref[sub:sub + 1, :], lnb_ref[sub:sub + 1, :])


def _ffn_weight_scratch():
    return [
        pltpu.VMEM((D_MODEL, 2 * D_FF), BF16),
        pltpu.VMEM((D_FF, D_MODEL), BF16),
        pltpu.VMEM((2, WIDE_CHUNK_ROWS, 2 * D_FF), F32),
        pltpu.VMEM((2, SQUARE_CHUNK_ROWS, D_MODEL), F32),
        pltpu.SemaphoreType.DMA((2,)),
        pltpu.SemaphoreType.DMA((2,)),
    ]


def _ffn1_kernel(xa_ref, xb_ref, mod_ref, win_hbm, wout_hbm, lng_ref, lnb_ref, xo_ref, ho_ref,
                 win_scr, wout_scr, wide_stage, square_stage, wide_sems, square_sems, act_scr, *, tiles_a):
    i = pl.program_id(0)

    @pl.when(i == 0)
    def _():
        _load_weight(win_hbm, win_scr, wide_stage, wide_sems)
        _load_weight(wout_hbm, wout_scr, square_stage, square_sems)

    x = jnp.where(i < tiles_a, xa_ref[...], xb_ref[...])
    xn = _half_step(x, mod_ref, win_scr, wout_scr, lng_ref, lnb_ref, act_scr, 0)
    xo_ref[...] = xn
    ho_ref[...] = _modulate(xn, mod_ref, 1).astype(BF16)


def _ffn1(xa, xb, mod, cond_of_tile, w_in, w_out, ln_g, ln_b):
    tiles_a, tiles_b = xa.shape[0] // FFN_TM, xb.shape[0] // FFN_TM
    m = xa.shape[0] + xb.shape[0]
    row = lambda i: (i, 0)
    const = lambda i: (0, 0)
    hbm = pl.BlockSpec(memory_space=pl.ANY)
    return pl.pallas_call(
        functools.partial(_ffn1_kernel, tiles_a=tiles_a),
        grid=(tiles_a + tiles_b,),
        in_specs=[
            pl.BlockSpec((FFN_TM, D_MODEL), lambda i: (jnp.minimum(i, tiles_a - 1), 0)),
            pl.BlockSpec((FFN_TM, D_MODEL), lambda i: (jnp.maximum(i - tiles_a, 0), 0)),
            pl.BlockSpec((1, 9, D_MODEL), lambda i: (cond_of_tile(i, FFN_TM), 0, 0)),
            hbm, hbm,
            pl.BlockSpec((3, D_MODEL), const),
            pl.BlockSpec((3, D_MODEL), const),
        ],
        out_specs=[pl.BlockSpec((FFN_TM, D_MODEL), row), pl.BlockSpec((FFN_TM, D_MODEL), row)],
        out_shape=[jax.ShapeDtypeStruct((m, D_MODEL), F32), jax.ShapeDtypeStruct((m, D_MODEL), BF16)],
        scratch_shapes=_ffn_weight_scratch() + [pltpu.VMEM((FFN_TM, D_FF), BF16)],
        compiler_params=_params(("arbitrary",)),
        name="ffn1",
    )(xa, xb, mod, w_in, w_out, ln_g, ln_b)


def _rope(x, cos, sin):
    half, quarter = DH_A // 2, DH_A // 4
    lane = lax.broadcasted_iota(jnp.int32, (1, LANES), 1)
    first_quarter = (lane % half) < quarter
    outs = []
    for hb in range(QK_A // LANES):
        xb = x[:, hb * LANES:(hb + 1) * LANES]
        ahead = pltpu.roll(xb, LANES - quarter, 1)
        behind = pltpu.roll(xb, quarter, 1)
        rot = jnp.where(first_quarter, -ahead, behind)
        outs.append(xb * cos + rot * sin)
    return jnp.concatenate(outs, axis=-1)


def _forget_gate(z, logits):
    mx = jnp.max(logits, axis=0, keepdims=True)
    el = jnp.exp(logits - mx)
    lb = el[0:1] / jnp.sum(el, axis=0, keepdims=True)
    sig = 0.5 + 0.5 * jnp.tanh(0.5 * z)
    u = (1.0 - lb) * sig
    return jnp.log(lb + u), (1.0 - lb) - u


def _proj_kernel(*refs, rope, want_kv32):
    refs = list(refs)
    h_ref, w_ref, lbl_ref = refs[:3]
    refs = refs[3:]
    cos_ref, sin_ref = (refs.pop(0), refs.pop(0)) if rope else (None, None)
    qkv_ref = refs.pop(0)
    kv32_ref = refs.pop(0) if want_kv32 else None
    plain_ref, g_ref, kd_ref = refs
    n = pl.program_id(1)

    pieces = [slice(r, r + PROJ_SUB) for r in range(0, h_ref.shape[0], PROJ_SUB)]

    def project(rows, w):
        return jnp.dot(h_ref[rows, :], w, preferred_element_type=F32)

    def store_heads(ref, rows, y):
        for hh in range(MIX_TN // LANES):
            ref[hh, rows, :] = y[:, hh * LANES:(hh + 1) * LANES].astype(ref.dtype)

    @pl.when(n <= W_V)
    def _():
        w = w_ref[...].astype(BF16)
        for rows in pieces:
            y = project(rows, w)
            if want_kv32:
                kv32_ref[0, rows, :] = y
            if rope:
                first = rows.start % cos_ref.shape[1]
                table_rows = slice(first, first + PROJ_SUB)
                y = _rope(y, cos_ref[0, table_rows, :], sin_ref[0, table_rows, :])
            else:
                y = y * jnp.where(n == W_Q, Q_SCALE, 1.0)
            store_heads(qkv_ref, rows, y)

    @pl.when((n == W_HFF) | (n == W_HFB))
    def _():
        w = w_ref[...].astype(BF16)
        for rows in pieces:
            log_f, kd = _forget_gate(project(rows, w), lbl_ref[0])
            store_heads(g_ref, rows, log_f)
            store_heads(kd_ref, rows, kd)

    @pl.when((n == W_HQ) | (n >= W_HI))
    def _():
        w = w_ref[...].astype(BF16)
        for rows in pieces:
            store_heads(plain_ref, rows, project(rows, w))


def _project(h, group, w, lb_logits, rope_tables, want_kv32):
    m = MIX_TM
    tm = PROJ_TM
    tiles = m // tm
    rope = rope_tables is not None
    heads = MIX_TN // LANES
    n_blocks = W_HG + 1
    in_specs = [pl.BlockSpec((tm, D_MODEL), lambda i, n: (group * tiles + i, 0)),
                pl.BlockSpec((D_MODEL, MIX_TN), lambda i, n: (0, n)),
                pl.BlockSpec((1, DEPTH + 1, QK_H), lambda i, n: (jnp.clip(n - W_HFF, 0, 1), 0, 0))]
    args = [h, w, lb_logits]
    qkv_col = lambda n: jnp.minimum(n, W_V)
    if rope:
        t_rope = rope_tables[0].shape[1]
        rope_rows = min(t_rope, tm)
        n_rope_tiles = t_rope // rope_rows
        for t in rope_tables:
            in_specs.append(pl.BlockSpec((1, rope_rows, DV_A), lambda i, n: (qkv_col(n), i % n_rope_tiles, 0)))
            args.append(t)
    head_block = (heads, tm, LANES)
    plain_col = lambda n: jnp.where(n < W_HI, 0, n - W_HI + 1)
    gate_col = lambda n: jnp.clip(n - W_HFF, 0, 1)
    out_specs = [pl.BlockSpec(head_block, lambda i, n: (qkv_col(n), i, 0))]
    out_shape = [jax.ShapeDtypeStruct((3 * heads, m, LANES), BF16)]
    if want_kv32:
        out_specs.append(pl.BlockSpec((1, tm, MIX_TN), lambda i, n: (jnp.clip(n - W_K, 0, 1), i, 0)))
        out_shape.append(jax.ShapeDtypeStruct((2, m, MIX_TN), F32))
    out_specs += [pl.BlockSpec(head_block, lambda i, n: (plain_col(n), i, 0)),
                  pl.BlockSpec(head_block, lambda i, n: (gate_col(n), i, 0)),
                  pl.BlockSpec(head_block, lambda i, n: (gate_col(n), i, 0))]
    out_shape += [jax.ShapeDtypeStruct((3 * heads, m, LANES), BF16),
                  jax.ShapeDtypeStruct((2 * heads, m, LANES), F32),
                  jax.ShapeDtypeStruct((2 * heads, m, LANES), BF16)]
    return pl.pallas_call(
        functools.partial(_proj_kernel, rope=rope, want_kv32=want_kv32),
        grid=(tiles, n_blocks),
        in_specs=in_specs,
        out_specs=out_specs,
        out_shape=out_shape,
        compiler_params=_params(("arbitrary", "arbitrary")),
        name="mix_proj",
    )(*args)


def _attn_kernel(*refs, lam_init, has_ctx):
    if has_ctx:
        lam_ref, g_ref, q_ref, k_ref, v_ref, kc_ref, vc_ref, o_ref = refs
    else:
        lam_ref, g_ref, q_ref, k_ref, v_ref, o_ref = refs
    lp = lam_ref[...]
    lam = (jnp.exp(jnp.sum(lp[0:1] * lp[1:2], axis=-1, keepdims=True))
           - jnp.exp(jnp.sum(lp[2:3] * lp[3:4], axis=-1, keepdims=True)) + lam_init)

    lane = lax.broadcasted_iota(jnp.int32, (1, DV_A), 1)
    comp0 = lane < DH_A
    tq = q_ref.shape[2]
    sub = min(ATT_SUB, tq)
    problems = []
    for hh in range(q_ref.shape[0]):
        keys = [k_ref[hh, 0]]
        vals = [v_ref[hh, 0]]
        if has_ctx:
            keys.append(kc_ref[0, :, hh * DV_A:(hh + 1) * DV_A].astype(BF16))
            vals.append(vc_ref[0, :, hh * DV_A:(hh + 1) * DV_A].astype(BF16))
        for r in range(tq // sub):
            q = q_ref[hh, 0, r * sub:(r + 1) * sub, :]
            zero = jnp.zeros_like(q)
            for qc in (jnp.where(comp0, q, zero), jnp.where(comp0, zero, q)):
                problems.append((qc, keys, vals))
    scores = [[_nt_dot(kk, qc) for kk in keys] for qc, keys, _ in problems]
    weights = []
    for ss in scores:
        mx = functools.reduce(jnp.maximum, [jnp.max(s, axis=0, keepdims=True) for s in ss])
        es = [jnp.exp2(s - mx) for s in ss]
        den = functools.reduce(jnp.add, [jnp.sum(e, axis=0, keepdims=True) for e in es])
        weights.append(([e.astype(BF16) for e in es], 1.0 / den))
    maps = [functools.reduce(jnp.add, [_tn_dot(vv, e) for e, vv in zip(es, vals)]) * r
            for (es, r), (_, _, vals) in zip(weights, problems)]
    idx = 0
    for hh in range(q_ref.shape[0]):
        for r in range(tq // sub):
            o = maps[idx] - lam * maps[idx + 1]
            idx += 2
            o = o * lax.rsqrt(jnp.mean(o * o, axis=0, keepdims=True) + RMS_EPS)
            o = o.T * g_ref[...]
            o_ref[hh, 0, r * sub:(r + 1) * sub, :] = (o * (1.0 - lam_init)).astype(o_ref.dtype)


def _attention(q, k, v, ctx_kv, lam_params, subln_g, lam_init):
    (q_arr, q_row), (k_arr, k_row), (v_arr, v_row) = q, k, v
    _, bsz, t, _ = q_arr.shape
    tq = min(ATT_TQ, t)
    hp = N_HEADS_A if t <= ATT_SHORT_SEQ else 1
    const = lambda b, h, i: (0, 0)
    rows = lambda first: (lambda b, h, i: (first // hp + h, b, 0, 0))
    in_specs = [
        pl.BlockSpec((4, DH_A), const),
        pl.BlockSpec((1, DV_A), const),
        pl.BlockSpec((hp, 1, tq, DV_A), lambda b, h, i: (q_row // hp + h, b, i, 0)),
        pl.BlockSpec((hp, 1, t, DV_A), rows(k_row)),
        pl.BlockSpec((hp, 1, t, DV_A), rows(v_row)),
    ]
    args = [lam_params, subln_g, q_arr, k_arr, v_arr]
    if ctx_kv is not None:
        p = ctx_kv[0].shape[1]
        head_all = lambda b, h, i: (b, 0, h)
        in_specs += [pl.BlockSpec((1, p, hp * DV_A), head_all), pl.BlockSpec((1, p, hp * DV_A), head_all)]
        args += list(ctx_kv)
    return pl.pallas_call(
        functools.partial(_attn_kernel, lam_init=lam_init, has_ctx=ctx_kv is not None),
        grid=(bsz, N_HEADS_A // hp, t // tq),
        in_specs=in_specs,
        out_specs=pl.BlockSpec((hp, 1, tq, DV_A), lambda b, h, i: (h, b, i, 0)),
        out_shape=jax.ShapeDtypeStruct((N_HEADS_A, bsz, t, DV_A), BF16),
        compiler_params=_params(("parallel", "parallel", "arbitrary")),
        name="attention",
    )(*args)


def _chunk_scan(g, reverse):
    r = g.shape[0]
    pos = lax.broadcasted_iota(jnp.int32, (r, 1), 0) % HGRN_C
    b = g
    sh = 1
    while sh < HGRN_C:
        if reverse:
            b = b + jnp.where(pos < HGRN_C - sh, pltpu.roll(b, r - sh, 0), 0.0)
        else:
            b = b + jnp.where(pos >= sh, pltpu.roll(b, sh, 0), 0.0)
        sh *= 2
    return b


def _hgrn_kernel(*refs, seq_len, has_s0, want_state):
    refs = list(refs)
    hq_ref, hi_ref, hg_ref, gf_ref, gb_ref, kf_ref, kb_ref, ng_ref = refs[:8]
    refs = refs[8:]
    s0_ref = refs.pop(0) if has_s0 else None
    o_ref = refs.pop(0)
    sfin_ref = refs.pop(0) if want_state else None
    od_scr, st_scr, b_scr, q_scr, k_scr, ok_scr = refs
    hp = hq_ref.shape[0]
    c = HGRN_C
    rb = min(HGRN_ROWS, seq_len)
    nb = seq_len // rb
    cpb = rb // c
    g_refs = (gf_ref, gb_ref)
    k_refs = (kf_ref, kb_ref)

    def queries(hq):
        hq = hq.astype(F32)
        return hq * jax.nn.sigmoid(hq)

    def flags(i, carry):
        rows = pl.ds(pl.multiple_of(i * rb, rb), rb)
        for d in range(2):
            worst = jnp.zeros((1, DK_H), F32)
            for hh in range(hp):
                g = g_refs[d][hh, 0, rows, :]
                for j in range(cpb):
                    worst = jnp.maximum(worst, -jnp.sum(g[j * c:(j + 1) * c], axis=0, keepdims=True))
            ok_scr[d, i] = (jnp.max(worst) <= HGRN_SAFE_DECAY).astype(jnp.int32)
        return carry

    lax.fori_loop(0, nb, flags, 0)

    for hh in range(hp):
        for d in range(2):
            if has_s0:
                st_scr[hh, d] = s0_ref[0, d, hh].T
            else:
                st_scr[hh, d] = jnp.zeros((DV_H, DK_H), F32)

    row = lax.broadcasted_iota(jnp.int32, (c, 1), 0)
    col = lax.broadcasted_iota(jnp.int32, (1, c), 1)

    def fast_block(d, r0):
        reverse = d == 1
        rows = pl.ds(r0, rb)
        n = hp * cpb
        flat = lambda a: a.reshape(hp * rb, a.shape[-1])
        chunks = lambda a: a.reshape(n, c, a.shape[-1])
        vb = chunks(hi_ref[:, 0, rows, :])
        b = _chunk_scan(flat(g_refs[d][:, 0, rows, :]), reverse)
        qd = chunks((queries(flat(hq_ref[:, 0, rows, :])) * jnp.exp(b)).astype(BF16))
        ku = chunks(flat(k_refs[d][:, 0, rows, :]).astype(F32) * jnp.exp(-b))
        b = chunks(b)
        e_end = jnp.exp(b[:, 0:1, :] if reverse else b[:, c - 1:c, :])
        causal = ((row <= col) if reverse else (row >= col))[None]
        a = jnp.einsum('nck,nsk->ncs', qd, ku.astype(BF16), preferred_element_type=F32)
        o = jnp.einsum('ncs,nsv->ncv', jnp.where(causal, a, 0.0).astype(BF16), vb, preferred_element_type=F32)
        u = jnp.einsum('nsv,nsk->nvk', vb, (ku * e_end).astype(BF16), preferred_element_type=F32)
        entering = [None] * n
        for hh in range(hp):
            st = st_scr[hh, d]
            for j in (range(cpb - 1, -1, -1) if reverse else range(cpb)):
                entering[hh * cpb + j] = st.astype(BF16)
                st = e_end[hh * cpb + j] * st + u[hh * cpb + j]
            st_scr[hh, d] = st
        o = o + jnp.einsum('nck,nvk->ncv', qd, jnp.stack(entering), preferred_element_type=F32)
        od_scr[:, d, rows, :] = o.reshape(hp, rb, DV_H)

    def exact_chunk(hh, d, r0):
        reverse = d == 1
        rows = pl.ds(r0, c)
        b = _chunk_scan(g_refs[d][hh, 0, rows, :], reverse)
        b_scr[...] = b
        q_scr[...] = queries(hq_ref[hh, 0, rows, :])
        k_scr[...] = k_refs[d][hh, 0, rows, :].astype(F32)

        def one_row(t, carry):
            bt = b_scr[pl.ds(t, 1), :]
            qt = q_scr[pl.ds(t, 1), :]
            seen = (row >= t) if reverse else (row <= t)
            w = jnp.exp(jnp.where(seen, bt - b_scr[...], -jnp.inf))
            p = jnp.sum(qt * k_scr[...] * w, axis=-1, keepdims=True)
            od_scr[hh, d, pl.ds(r0 + t, 1), :] = jnp.sum(p * hi_ref[hh, 0, rows, :].astype(F32),
                                                         axis=0, keepdims=True)
            return carry

        lax.fori_loop(0, c, one_row, 0)
        b_end = b[0:1] if reverse else b[c - 1:c]
        st = st_scr[hh, d]
        qd = (q_scr[...] * jnp.exp(b)).astype(BF16)
        od_scr[hh, d, rows, :] = od_scr[hh, d, rows, :] + _nt_dot(qd, st.astype(BF16))
        k_end = (k_scr[...] * jnp.exp(b_end - b)).astype(BF16)
        st_scr[hh, d] = jnp.exp(b_end) * st + _tn_dot(hi_ref[hh, 0, rows, :], k_end)

    def block(i, carry):
        starts = (pl.multiple_of(i * rb, rb), pl.multiple_of((nb - 1 - i) * rb, rb))
        mild = (ok_scr[0, i] + ok_scr[1, nb - 1 - i]) == 2

        @pl.when(mild)
        def _():
            for d in range(2):
                fast_block(d, starts[d])

        @pl.when(jnp.logical_not(mild))
        def _():
            for hh in range(hp):
                for d in range(2):
                    for j in (range(cpb - 1, -1, -1) if d == 1 else range(cpb)):
                        exact_chunk(hh, d, pl.multiple_of(starts[d] + j * c, c))

        return carry

    lax.fori_loop(0, nb, block, 0)

    def finish(i, carry):
        rows = pl.ds(pl.multiple_of(i * rb, rb), rb)
        for hh in range(hp):
            o = od_scr[hh, 0, rows, :] + od_scr[hh, 1, rows, :]
            o = o * lax.rsqrt(jnp.mean(o * o, axis=-1, keepdims=True) + RMS_EPS) * ng_ref[...]
            hg = hg_ref[hh, 0, rows, :].astype(F32)
            o_ref[hh, 0, rows, :] = (o * (hg * jax.nn.sigmoid(hg))).astype(o_ref.dtype)
        return carry

    lax.fori_loop(0, nb, finish, 0)
    if want_state:
        for hh in range(hp):
            for d in range(2):
                sfin_ref[0, d, hh] = st_scr[hh, d].T


def _hgrn(plain, g, kd, norm_g, s0, want_state):
    _, bsz, t, _ = plain.shape
    hp = HGRN_SHORT_HEADS if t <= HGRN_ROWS else 1
    stream = lambda s: pl.BlockSpec((hp, 1, t, DK_H), lambda b, h: (s * N_HEADS_H // hp + h, b, 0, 0))
    in_specs = [stream(0), stream(1), stream(2), stream(0), stream(1), stream(0), stream(1),
                pl.BlockSpec((1, DV_H), lambda b, h: (0, 0))]
    args = [plain, plain, plain, g, g, kd, kd, norm_g]
    state_spec = pl.BlockSpec((1, 2, hp, DK_H, DV_H), lambda b, h: (b, 0, h, 0, 0))
    if s0 is not None:
        in_specs.append(state_spec)
        args.append(s0)
    out_specs = [pl.BlockSpec((hp, 1, t, DV_H), lambda b, h: (h, b, 0, 0))]
    out_shape = [jax.ShapeDtypeStruct((N_HEADS_H, bsz, t, DV_H), BF16)]
    if want_state:
        out_specs.append(state_spec)
        out_shape.append(jax.ShapeDtypeStruct((bsz, 2, N_HEADS_H, DK_H, DV_H), F32))
    n_blocks = t // min(HGRN_ROWS, t)
    chunk = pltpu.VMEM((HGRN_C, DK_H), F32)
    outs = pl.pallas_call(
        functools.partial(_hgrn_kernel, seq_len=t, has_s0=s0 is not None, want_state=want_state),
        grid=(bsz, N_HEADS_H // hp),
        in_specs=in_specs,
        out_specs=out_specs,
        out_shape=out_shape,
        scratch_shapes=[
            pltpu.VMEM((hp, 2, t, DV_H), F32),
            pltpu.VMEM((hp, 2, DV_H, DK_H), F32),
            chunk, chunk, chunk,
            pltpu.SMEM((2, n_blocks), jnp.int32),
        ],
        compiler_params=_params(("parallel", "parallel")),
        name="hgrn",
    )(*args)
    return outs if want_state else (outs[0], None)


def _merge_ffn2_kernel(oa0_ref, oh0_ref, oa1_ref, oh1_ref, h_ref, x_ref, mod_ref,
                       wmix_hbm, wa_hbm, wh_hbm, wo_hbm, win_hbm, wout_hbm, lng_ref, lnb_ref, y0_ref, y1_ref,
                       wg_scr, wa_scr, wh_scr, wo_scr,
                       win_scr, wout_scr, wide_stage, square_stage, wide_sems, square_sems, act_scr, *, tiles_0):
    i = pl.program_id(0)
    first_group = i < tiles_0

    @pl.when(i == 0)
    def _():
        _load_weight(win_hbm, win_scr, wide_stage, wide_sems)
        for half in range(2):
            _load_weight(wmix_hbm, wg_scr, square_stage, square_sems,
                         src_col=W_GA0 * MIX_TN + half * D_MODEL, dst_col=half * D_MODEL)
        for w_hbm, w_scr in ((wa_hbm, wa_scr), (wh_hbm, wh_scr), (wo_hbm, wo_scr), (wout_hbm, wout_scr)):
            _load_weight(w_hbm, w_scr, square_stage, square_sems)

    def heads(ref0, ref1):
        pick = lambda hh: jnp.where(first_group, ref0[hh], ref1[hh])
        return jnp.concatenate([pick(hh) for hh in range(ref0.shape[0])], axis=-1)

    gates = jax.nn.sigmoid(jnp.dot(h_ref[...], wg_scr[...], preferred_element_type=F32))
    ba = jnp.dot(heads(oa0_ref, oa1_ref), wa_scr[...], preferred_element_type=F32)
    bh = jnp.dot(heads(oh0_ref, oh1_ref), wh_scr[...], preferred_element_type=F32)
    merged = gates[:, :D_MODEL] * ba + gates[:, D_MODEL:] * bh
    mix = jnp.dot(merged.astype(BF16), wo_scr[...], preferred_element_type=F32)
    x2 = _layer_norm(ALPHA * x_ref[...] + mod_ref[0, 5:6, :] * mix, lng_ref[1:2, :], lnb_ref[1:2, :])
    y = _half_step(x2, mod_ref, win_scr, wout_scr, lng_ref, lnb_ref, act_scr, 2)

    @pl.when(first_group)
    def _():
        y0_ref[...] = y

    @pl.when(jnp.logical_not(first_group))
    def _():
        y1_ref[...] = y


def _merge_ffn2(mixed, h, x, mod, cond_of_tile, w_mix, w_a, w_h, w_o, w_in, w_out, ln_g, ln_b):
    (oa0, oh0), (oa1, oh1) = mixed
    tiles_0, tiles_1 = oa0.shape[1] // FFN_TM, oa1.shape[1] // FFN_TM
    row = lambda i: (i, 0)
    const = lambda i: (0, 0)
    in_group_0 = lambda i: jnp.minimum(i, tiles_0 - 1)
    in_group_1 = lambda i: jnp.maximum(i - tiles_0, 0)
    hbm = pl.BlockSpec(memory_space=pl.ANY)
    return pl.pallas_call(
        functools.partial(_merge_ffn2_kernel, tiles_0=tiles_0),
        grid=(tiles_0 + tiles_1,),
        in_specs=[
            pl.BlockSpec((N_HEADS_A, FFN_TM, DV_A), lambda i: (0, in_group_0(i), 0)),
            pl.BlockSpec((N_HEADS_H, FFN_TM, DV_H), lambda i: (0, in_group_0(i), 0)),
            pl.BlockSpec((N_HEADS_A, FFN_TM, DV_A), lambda i: (0, in_group_1(i), 0)),
            pl.BlockSpec((N_HEADS_H, FFN_TM, DV_H), lambda i: (0, in_group_1(i), 0)),
            pl.BlockSpec((FFN_TM, D_MODEL), row),
            pl.BlockSpec((FFN_TM, D_MODEL), row),
            pl.BlockSpec((1, 9, D_MODEL), lambda i: (cond_of_tile(i, FFN_TM), 0, 0)),
            hbm, hbm, hbm, hbm, hbm, hbm,
            pl.BlockSpec((3, D_MODEL), const),
            pl.BlockSpec((3, D_MODEL), const),
        ],
        out_specs=[pl.BlockSpec((FFN_TM, D_MODEL), lambda i: (in_group_0(i), 0)),
                   pl.BlockSpec((FFN_TM, D_MODEL), lambda i: (in_group_1(i), 0))],
        out_shape=[jax.ShapeDtypeStruct((tiles_0 * FFN_TM, D_MODEL), F32),
                   jax.ShapeDtypeStruct((tiles_1 * FFN_TM, D_MODEL), F32)],
        scratch_shapes=[pltpu.VMEM((D_MODEL, 2 * D_MODEL), BF16), pltpu.VMEM((W_A, D_MODEL), BF16),
                        pltpu.VMEM((W_H, D_MODEL), BF16), pltpu.VMEM((D_MODEL, D_MODEL), BF16)]
                       + _ffn_weight_scratch() + [pltpu.VMEM((FFN_TM, D_FF), BF16)],
        compiler_params=_params(("arbitrary",)),
        name="merge_ffn2",
    )(oa0, oh0, oa1, oh1, h, x, mod, w_mix, w_a, w_h, w_o, w_in, w_out, ln_g, ln_b)


def _rope_tables(n_tok):
    rows = n_tok // GRID_W
    row = np.repeat(np.arange(rows, dtype=np.float32), GRID_W)
    col = np.tile(np.arange(GRID_W, dtype=np.float32), rows)
    half = DH_A // 2
    inv = (ROPE_BASE ** (-np.arange(0, half, 2, dtype=np.float32) / half)).astype(np.float32)
    ar = row[:, None] * inv
    ac = col[:, None] * inv
    ang = np.concatenate([ar, ar, ac, ac] * 2, axis=-1)
    cos, sin = np.cos(ang).astype(np.float32), np.sin(ang).astype(np.float32)
    scale = np.float32(Q_SCALE)
    return (jnp.asarray(np.stack([cos * scale, cos, np.ones_like(cos)])),
            jnp.asarray(np.stack([sin * scale, sin, np.zeros_like(sin)])))


def _mixer(h1, group, wts, lam_init, bsz, ctx):
    m = MIX_TM
    t = m // bsz
    latent = ctx is not None
    per_head = lambda a: a.reshape(a.shape[0], bsz, t, a.shape[-1])
    if latent:
        qkv, plain, g, kd = _project(h1, group, wts['w_mix_in'], wts['lb_logits'], _rope_tables(t), False)
        k = v = None
        ctx_kv = (ctx[0], ctx[1])
    else:
        qkv, kv32, plain, g, kd = _project(h1, group, wts['w_mix_in'], wts['lb_logits'], None, True)
        k, v = kv32[0], kv32[1]
        ctx_kv = None
    qkv = per_head(qkv)
    q_kv = [(qkv, W_Q * N_HEADS_A), (qkv, W_K * N_HEADS_A), (qkv, W_V * N_HEADS_A)]
    o_a = _attention(*q_kv, ctx_kv, wts['lam_params'], wts['attn_subln_g'], lam_init)
    o_h, s_fin = _hgrn(per_head(plain), per_head(g), per_head(kd), wts['hgrn_norm_g'],
                       ctx[2] if latent else None, not latent)
    return (o_a.reshape(N_HEADS_A, m, DV_A), o_h.reshape(N_HEADS_H, m, DV_H)), (k, v, s_fin)


def kernel(x_prompt, x_sample, cache_k, cache_v, state_hgrn, c, c_ctx, w_ada, b_ada, ffn1_w_in, ffn1_w_out,
           w_mix_in, lambda_q1, lambda_k1, lambda_q2, lambda_k2, attn_subln_g, hgrn_lb_logits, hgrn_norm_g,
           w_branch_a, w_branch_h, w_mix_out, ffn2_w_in, ffn2_w_out, ln_g, ln_b):
    batch, seq, _ = x_prompt.shape
    dec_batch, dec_seq, _ = x_sample.shape
    past = cache_k.shape[2]
    assert DEPTH == 1 and dec_batch + 1 <= MOD_ROWS
    lam_init = 0.8 - 0.6 * math.exp(-0.3 * 0)

    cond = jnp.concatenate([c_ctx[None, :], c, jnp.zeros((MOD_ROWS - 1 - dec_batch, D_MODEL), F32)], axis=0)
    mod = _modulation(cond, w_ada[0], b_ada[0])

    wts = {
        'w_mix_in': w_mix_in[0],
        'lam_params': jnp.concatenate([lambda_q1, lambda_k1, lambda_q2, lambda_k2], axis=0),
        'attn_subln_g': attn_subln_g, 'lb_logits': hgrn_lb_logits, 'hgrn_norm_g': hgrn_norm_g,
    }
    n_ctx = batch * seq
    assert n_ctx == MIX_TM and dec_batch * dec_seq == MIX_TM

    def cond_of_tile(i, tm):
        first = i * tm
        return jnp.where(first < n_ctx, 0, 1 + (first - n_ctx) // dec_seq)

    x1, h1 = _ffn1(x_prompt.reshape(n_ctx, D_MODEL), x_sample.reshape(dec_batch * dec_seq, D_MODEL), mod,
                   cond_of_tile, ffn1_w_in[0], ffn1_w_out[0], ln_g[0], ln_b[0])
    mixed_p, (k_c, v_c, s_c) = _mixer(h1, 0, wts, lam_init, batch, None)
    ctx = (cache_k[:, 0].reshape(dec_batch, past, QK_A), cache_v[:, 0].reshape(dec_batch, past, W_A),
           state_hgrn[:, 0])
    mixed_s, _ = _mixer(h1, 1, wts, lam_init, dec_batch, ctx)
    y_p, y_s = _merge_ffn2((mixed_p, mixed_s), h1, x1, mod, cond_of_tile, w_mix_in[0],
                           w_branch_a[0], w_branch_h[0], w_mix_out[0], ffn2_w_in[0], ffn2_w_out[0],
                           ln_g[0], ln_b[0])

    return (y_p.reshape(batch, seq, D_MODEL),
            y_s.reshape(dec_batch, dec_seq, D_MODEL),
            k_c.reshape(batch, DEPTH, seq, N_HEADS_A, 2, DH_A),
            v_c.reshape(batch, DEPTH, seq, N_HEADS_A, DV_A),
            s_c.reshape(batch, DEPTH, 2, N_HEADS_H, DK_H, DV_H))
```

```python
import functools
import math

import numpy as np
import jax
import jax.numpy as jnp
from jax import lax
from jax.experimental import pallas as pl
from jax.experimental.pallas import tpu as pltpu

D_MODEL = 1024
DEPTH = 1
GRID_W = 64
N_HEADS_A = 4
DH_A = 64
DV_A = 2 * DH_A
QK_A = N_HEADS_A * 2 * DH_A
W_A = N_HEADS_A * DV_A
ROPE_BASE = 10000.0
N_HEADS_H = 4
DK_H = 128
DV_H = 128
QK_H = N_HEADS_H * DK_H
W_H = N_HEADS_H * DV_H
D_FF = 2816
MIX_IN = 2 * QK_A + W_A + 3 * QK_H + 2 * W_H + 2 * D_MODEL
ALPHA = (2 * DEPTH) ** 0.25
LN_EPS = 1e-5
RMS_EPS = 1e-6
Q_SCALE = DH_A ** -0.5 * math.log2(math.e)

F32 = jnp.float32
BF16 = jnp.bfloat16

LANES = 128
VMEM_LIMIT = 56 * 1024 * 1024

FFN_TM = 512
FFN_TF = 256
WIDE_CHUNK_ROWS = 128
SQUARE_CHUNK_ROWS = 256
MIX_TM = 4096
PROJ_TM = 2048
PROJ_SUB = 256
MIX_TN = 512
ATT_TQ = 1024
ATT_SUB = 512
ATT_SHORT_SEQ = 256
HGRN_C = 64
HGRN_ROWS = 1024
HGRN_SHORT_HEADS = 4
HGRN_SAFE_DECAY = 75.0
MOD_ROWS = 8
MOD_TN = 2304

W_Q, W_K, W_V, W_HQ, W_HFF, W_HFB, W_HI, W_HG, W_GA0, W_GA1, W_GH0, W_GH1 = range(12)


def _params(sem):
    return pltpu.CompilerParams(dimension_semantics=sem, vmem_limit_bytes=VMEM_LIMIT)


def _nt_dot(a, b):
    return lax.dot_general(a, b, (((1,), (1,)), ((), ())), preferred_element_type=F32)


def _tn_dot(a, b):
    return lax.dot_general(a, b, (((0,), (0,)), ((), ())), preferred_element_type=F32)


def _layer_norm(y, g, b):
    mu = jnp.mean(y, axis=-1, keepdims=True)
    yc = y - mu
    var = jnp.mean(yc * yc, axis=-1, keepdims=True)
    return yc * lax.rsqrt(var + LN_EPS) * g + b


def _mod_kernel(c_ref, w_ref, b_ref, o_ref):
    c = c_ref[...]
    a = (c * jax.nn.sigmoid(c)).astype(BF16)
    o_ref[...] = jnp.dot(a, w_ref[...].astype(BF16), preferred_element_type=F32) + b_ref[...]


def _modulation(cond, w_ada, b_ada):
    n_out = w_ada.shape[1]
    tn = MOD_TN
    out = pl.pallas_call(
        _mod_kernel,
        grid=(n_out // tn,),
        in_specs=[
            pl.BlockSpec((MOD_ROWS, D_MODEL), lambda n: (0, 0)),
            pl.BlockSpec((D_MODEL, tn), lambda n: (0, n)),
            pl.BlockSpec((1, tn), lambda n: (0, n)),
        ],
        out_specs=pl.BlockSpec((MOD_ROWS, tn), lambda n: (0, n)),
        out_shape=jax.ShapeDtypeStruct((MOD_ROWS, n_out), F32),
        compiler_params=_params(("arbitrary",)),
        name="modulation",
    )(cond, w_ada, b_ada.reshape(1, n_out))
    return out.reshape(MOD_ROWS, 9, D_MODEL)


def _load_weight(w_hbm, w_scr, stage, sems, src_col=0, dst_col=0):
    _, chunk, width = stage.shape
    n_chunks = w_hbm.shape[0] // chunk

    def copy(c):
        return pltpu.make_async_copy(w_hbm.at[pl.ds(c * chunk, chunk), pl.ds(src_col, width)],
                                     stage.at[c % 2], sems.at[c % 2])

    copy(0).start()
    for c in range(n_chunks):
        if c + 1 < n_chunks:
            copy(c + 1).start()
        copy(c).wait()
        w_scr[c * chunk:(c + 1) * chunk, dst_col:dst_col + width] = stage[c % 2].astype(BF16)


def _modulate(x, mod_ref, sub):
    return x * (1.0 + mod_ref[0, 3 * sub + 1:3 * sub + 2, :]) + mod_ref[0, 3 * sub:3 * sub + 1, :]


def _half_step(x, mod_ref, win_ref, wout_ref, lng_ref, lnb_ref, act_scr, sub):
    h = _modulate(x, mod_ref, sub).astype(BF16)
    for c in range(D_FF // FFN_TF):
        cols = slice(c * FFN_TF, (c + 1) * FFN_TF)
        up_cols = slice(D_FF + c * FFN_TF, D_FF + (c + 1) * FFN_TF)
        gate = jnp.dot(h, win_ref[:, cols], preferred_element_type=F32)
        up = jnp.dot(h, win_ref[:, up_cols], preferred_element_type=F32)
        act_scr[:, cols] = (gate * jax.nn.sigmoid(gate) * up).astype(BF16)
    f = jnp.dot(act_scr[...], wout_ref[...], preferred_element_type=F32)
    g = mod_ref[0, 3 * sub + 2:3 * sub + 3, :]
    return _layer_norm(ALPHA * x + 0.5 * g * f, lng_ref[sub:sub + 1, :], lnb_ref[sub:sub + 1, :])


def _ffn_weight_scratch():
    return [
        pltpu.VMEM((D_MODEL, 2 * D_FF), BF16),
        pltpu.VMEM((D_FF, D_MODEL), BF16),
        pltpu.VMEM((2, WIDE_CHUNK_ROWS, 2 * D_FF), F32),
        pltpu.VMEM((2, SQUARE_CHUNK_ROWS, D_MODEL), F32),
        pltpu.SemaphoreType.DMA((2,)),
        pltpu.SemaphoreType.DMA((2,)),
    ]


def _ffn1_kernel(xa_ref, xb_ref, mod_ref, win_hbm, wout_hbm, lng_ref, lnb_ref, xo_ref, ho_ref,
                 win_scr, wout_scr, wide_stage, square_stage, wide_sems, square_sems, act_scr, *, tiles_a):
    i = pl.program_id(0)

    @pl.when(i == 0)
    def _():
        _load_weight(win_hbm, win_scr, wide_stage, wide_sems)
        _load_weight(wout_hbm, wout_scr, square_stage, square_sems)

    x = jnp.where(i < tiles_a, xa_ref[...], xb_ref[...])
    xn = _half_step(x, mod_ref, win_scr, wout_scr, lng_ref, lnb_ref, act_scr, 0)
    xo_ref[...] = xn
    ho_ref[...] = _modulate(xn, mod_ref, 1).astype(BF16)


def _ffn1(xa, xb, mod, cond_of_tile, w_in, w_out, ln_g, ln_b):
    tiles_a, tiles_b = xa.shape[0] // FFN_TM, xb.shape[0] // FFN_TM
    m = xa.shape[0] + xb.shape[0]
    row = lambda i: (i, 0)
    const = lambda i: (0, 0)
    hbm = pl.BlockSpec(memory_space=pl.ANY)
    return pl.pallas_call(
        functools.partial(_ffn1_kernel, tiles_a=tiles_a),
        grid=(tiles_a + tiles_b,),
        in_specs=[
            pl.BlockSpec((FFN_TM, D_MODEL), lambda i: (jnp.minimum(i, tiles_a - 1), 0)),
            pl.BlockSpec((FFN_TM, D_MODEL), lambda i: (jnp.maximum(i - tiles_a, 0), 0)),
            pl.BlockSpec((1, 9, D_MODEL), lambda i: (cond_of_tile(i, FFN_TM), 0, 0)),
            hbm, hbm,
            pl.BlockSpec((3, D_MODEL), const),
            pl.BlockSpec((3, D_MODEL), const),
        ],
        out_specs=[pl.BlockSpec((FFN_TM, D_MODEL), row), pl.BlockSpec((FFN_TM, D_MODEL), row)],
        out_shape=[jax.ShapeDtypeStruct((m, D_MODEL), F32), jax.ShapeDtypeStruct((m, D_MODEL), BF16)],
        scratch_shapes=_ffn_weight_scratch() + [pltpu.VMEM((FFN_TM, D_FF), BF16)],
        compiler_params=_params(("arbitrary",)),
        name="ffn1",
    )(xa, xb, mod, w_in, w_out, ln_g, ln_b)


def _rope(x, cos, sin):
    half, quarter = DH_A // 2, DH_A // 4
    lane = lax.broadcasted_iota(jnp.int32, (1, LANES), 1)
    first_quarter = (lane % half) < quarter
    outs = []
    for hb in range(QK_A // LANES):
        xb = x[:, hb * LANES:(hb + 1) * LANES]
        ahead = pltpu.roll(xb, LANES - quarter, 1)
        behind = pltpu.roll(xb, quarter, 1)
        rot = jnp.where(first_quarter, -ahead, behind)
        outs.append(xb * cos + rot * sin)
    return jnp.concatenate(outs, axis=-1)


def _forget_gate(z, logits):
    mx = jnp.max(logits, axis=0, keepdims=True)
    el = jnp.exp(logits - mx)
    lb = el[0:1] / jnp.sum(el, axis=0, keepdims=True)
    sig = 0.5 + 0.5 * jnp.tanh(0.5 * z)
    u = (1.0 - lb) * sig
    return jnp.log(lb + u), (1.0 - lb) - u


def _proj_kernel(*refs, rope, want_kv32):
    refs = list(refs)
    h_ref, w_ref, lbl_ref = refs[:3]
    refs = refs[3:]
    cos_ref, sin_ref = (refs.pop(0), refs.pop(0)) if rope else (None, None)
    qkv_ref = refs.pop(0)
    kv32_ref = refs.pop(0) if want_kv32 else None
    plain_ref, g_ref, kd_ref = refs
    n = pl.program_id(1)

    pieces = [slice(r, r + PROJ_SUB) for r in range(0, h_ref.shape[0], PROJ_SUB)]

    def project(rows, w):
        return jnp.dot(h_ref[rows, :], w, preferred_element_type=F32)

    def store_heads(ref, rows, y):
        for hh in range(MIX_TN // LANES):
            ref[hh, rows, :] = y[:, hh * LANES:(hh + 1) * LANES].astype(ref.dtype)

    @pl.when(n <= W_V)
    def _():
        w = w_ref[...].astype(BF16)
        for rows in pieces:
            y = project(rows, w)
            if want_kv32:
                kv32_ref[0, rows, :] = y
            if rope:
                first = rows.start % cos_ref.shape[1]
                table_rows = slice(first, first + PROJ_SUB)
                y = _rope(y, cos_ref[0, table_rows, :], sin_ref[0, table_rows, :])
            else:
                y = y * jnp.where(n == W_Q, Q_SCALE, 1.0)
            store_heads(qkv_ref, rows, y)

    @pl.when((n == W_HFF) | (n == W_HFB))
    def _():
        w = w_ref[...].astype(BF16)
        for rows in pieces:
            log_f, kd = _forget_gate(project(rows, w), lbl_ref[0])
            store_heads(g_ref, rows, log_f)
            store_heads(kd_ref, rows, kd)

    @pl.when((n == W_HQ) | (n >= W_HI))
    def _():
        w = w_ref[...].astype(BF16)
        for rows in pieces:
            store_heads(plain_ref, rows, project(rows, w))


def _project(h, group, w, lb_logits, rope_tables, want_kv32):
    m = MIX_TM
    tm = PROJ_TM
    tiles = m // tm
    rope = rope_tables is not None
    heads = MIX_TN // LANES
    n_blocks = W_HG + 1
    in_specs = [pl.BlockSpec((tm, D_MODEL), lambda i, n: (group * tiles + i, 0)),
                pl.BlockSpec((D_MODEL, MIX_TN), lambda i, n: (0, n)),
                pl.BlockSpec((1, DEPTH + 1, QK_H), lambda i, n: (jnp.clip(n - W_HFF, 0, 1), 0, 0))]
    args = [h, w, lb_logits]
    qkv_col = lambda n: jnp.minimum(n, W_V)
    if rope:
        t_rope = rope_tables[0].shape[1]
        rope_rows = min(t_rope, tm)
        n_rope_tiles = t_rope // rope_rows
        for t in rope_tables:
            in_specs.append(pl.BlockSpec((1, rope_rows, DV_A), lambda i, n: (qkv_col(n), i % n_rope_tiles, 0)))
            args.append(t)
    head_block = (heads, tm, LANES)
    plain_col = lambda n: jnp.where(n < W_HI, 0, n - W_HI + 1)
    gate_col = lambda n: jnp.clip(n - W_HFF, 0, 1)
    out_specs = [pl.BlockSpec(head_block, lambda i, n: (qkv_col(n), i, 0))]
    out_shape = [jax.ShapeDtypeStruct((3 * heads, m, LANES), BF16)]
    if want_kv32:
        out_specs.append(pl.BlockSpec((1, tm, MIX_TN), lambda i, n: (jnp.clip(n - W_K, 0, 1), i, 0)))
        out_shape.append(jax.ShapeDtypeStruct((2, m, MIX_TN), F32))
    out_specs += [pl.BlockSpec(head_block, lambda i, n: (plain_col(n), i, 0)),
                  pl.BlockSpec(head_block, lambda i, n: (gate_col(n), i, 0)),
                  pl.BlockSpec(head_block, lambda i, n: (gate_col(n), i, 0))]
    out_shape += [jax.ShapeDtypeStruct((3 * heads, m, LANES), BF16),
                  jax.ShapeDtypeStruct((2 * heads, m, LANES), F32),
                  jax.ShapeDtypeStruct((2 * heads, m, LANES), BF16)]
    return pl.pallas_call(
        functools.partial(_proj_kernel, rope=rope, want_kv32=want_kv32),
        grid=(tiles, n_blocks),
        in_specs=in_specs,
        out_specs=out_specs,
        out_shape=out_shape,
        compiler_params=_params(("arbitrary", "arbitrary")),
        name="mix_proj",
    )(*args)


def _attn_kernel(*refs, lam_init, has_ctx):
    if has_ctx:
        lam_ref, g_ref, q_ref, k_ref, v_ref, kc_ref, vc_ref, o_ref = refs
    else:
        lam_ref, g_ref, q_ref, k_ref, v_ref, o_ref = refs
    lp = lam_ref[...]
    lam = (jnp.exp(jnp.sum(lp[0:1] * lp[1:2], axis=-1, keepdims=True))
           - jnp.exp(jnp.sum(lp[2:3] * lp[3:4], axis=-1, keepdims=True)) + lam_init)

    lane = lax.broadcasted_iota(jnp.int32, (1, DV_A), 1)
    comp0 = lane < DH_A
    tq = q_ref.shape[2]
    sub = min(ATT_SUB, tq)
    problems = []
    for hh in range(q_ref.shape[0]):
        keys = [k_ref[hh, 0]]
        vals = [v_ref[hh, 0]]
        if has_ctx:
            keys.append(kc_ref[0, :, hh * DV_A:(hh + 1) * DV_A].astype(BF16))
            vals.append(vc_ref[0, :, hh * DV_A:(hh + 1) * DV_A].astype(BF16))
        for r in range(tq // sub):
            q = q_ref[hh, 0, r * sub:(r + 1) * sub, :]
            zero = jnp.zeros_like(q)
            for qc in (jnp.where(comp0, q, zero), jnp.where(comp0, zero, q)):
                problems.append((qc, keys, vals))
    scores = [[_nt_dot(kk, qc) for kk in keys] for qc, keys, _ in problems]
    weights = []
    for ss in scores:
        mx = functools.reduce(jnp.maximum, [jnp.max(s, axis=0, keepdims=True) for s in ss])
        es = [jnp.exp2(s - mx) for s in ss]
        den = functools.reduce(jnp.add, [jnp.sum(e, axis=0, keepdims=True) for e in es])
        weights.append(([e.astype(BF16) for e in es], 1.0 / den))
    maps = [functools.reduce(jnp.add, [_tn_dot(vv, e) for e, vv in zip(es, vals)]) * r
            for (es, r), (_, _, vals) in zip(weights, problems)]
    idx = 0
    for hh in range(q_ref.shape[0]):
        for r in range(tq // sub):
            o = maps[idx] - lam * maps[idx + 1]
            idx += 2
            o = o * lax.rsqrt(jnp.mean(o * o, axis=0, keepdims=True) + RMS_EPS)
            o = o.T * g_ref[...]
            o_ref[hh, 0, r * sub:(r + 1) * sub, :] = (o * (1.0 - lam_init)).astype(o_ref.dtype)


def _attention(q, k, v, ctx_kv, lam_params, subln_g, lam_init):
    (q_arr, q_row), (k_arr, k_row), (v_arr, v_row) = q, k, v
    _, bsz, t, _ = q_arr.shape
    tq = min(ATT_TQ, t)
    hp = N_HEADS_A if t <= ATT_SHORT_SEQ else 1
    const = lambda b, h, i: (0, 0)
    rows = lambda first: (lambda b, h, i: (first // hp + h, b, 0, 0))
    in_specs = [
        pl.BlockSpec((4, DH_A), const),
        pl.BlockSpec((1, DV_A), const),
        pl.BlockSpec((hp, 1, tq, DV_A), lambda b, h, i: (q_row // hp + h, b, i, 0)),
        pl.BlockSpec((hp, 1, t, DV_A), rows(k_row)),
        pl.BlockSpec((hp, 1, t, DV_A), rows(v_row)),
    ]
    args = [lam_params, subln_g, q_arr, k_arr, v_arr]
    if ctx_kv is not None:
        p = ctx_kv[0].shape[1]
        head_all = lambda b, h, i: (b, 0, h)
        in_specs += [pl.BlockSpec((1, p, hp * DV_A), head_all), pl.BlockSpec((1, p, hp * DV_A), head_all)]
        args += list(ctx_kv)
    return pl.pallas_call(
        functools.partial(_attn_kernel, lam_init=lam_init, has_ctx=ctx_kv is not None),
        grid=(bsz, N_HEADS_A // hp, t // tq),
        in_specs=in_specs,
        out_specs=pl.BlockSpec((hp, 1, tq, DV_A), lambda b, h, i: (h, b, i, 0)),
        out_shape=jax.ShapeDtypeStruct((N_HEADS_A, bsz, t, DV_A), BF16),
        compiler_params=_params(("parallel", "parallel", "arbitrary")),
        name="attention",
    )(*args)


def _chunk_scan(g, reverse):
    r = g.shape[0]
    pos = lax.broadcasted_iota(jnp.int32, (r, 1), 0) % HGRN_C
    b = g
    sh = 1
    while sh < HGRN_C:
        if reverse:
            b = b + jnp.where(pos < HGRN_C - sh, pltpu.roll(b, r - sh, 0), 0.0)
        else:
            b = b + jnp.where(pos >= sh, pltpu.roll(b, sh, 0), 0.0)
        sh *= 2
    return b


def _hgrn_kernel(*refs, seq_len, has_s0, want_state):
    refs = list(refs)
    hq_ref, hi_ref, hg_ref, gf_ref, gb_ref, kf_ref, kb_ref, ng_ref = refs[:8]
    refs = refs[8:]
    s0_ref = refs.pop(0) if has_s0 else None
    o_ref = refs.pop(0)
    sfin_ref = refs.pop(0) if want_state else None
    od_scr, st_scr, b_scr, q_scr, k_scr, ok_scr = refs
    hp = hq_ref.shape[0]
    c = HGRN_C
    rb = min(HGRN_ROWS, seq_len)
    nb = seq_len // rb
    cpb = rb // c
    g_refs = (gf_ref, gb_ref)
    k_refs = (kf_ref, kb_ref)

    def queries(hq):
        hq = hq.astype(F32)
        return hq * jax.nn.sigmoid(hq)

    def flags(i, carry):
        rows = pl.ds(pl.multiple_of(i * rb, rb), rb)
        for d in range(2):
            worst = jnp.zeros((1, DK_H), F32)
            for hh in range(hp):
                g = g_refs[d][hh, 0, rows, :]
                for j in range(cpb):
                    worst = jnp.maximum(worst, -jnp.sum(g[j * c:(j + 1) * c], axis=0, keepdims=True))
            ok_scr[d, i] = (jnp.max(worst) <= HGRN_SAFE_DECAY).astype(jnp.int32)
        return carry

    lax.fori_loop(0, nb, flags, 0)

    for hh in range(hp):
        for d in range(2):
            if has_s0:
                st_scr[hh, d] = s0_ref[0, d, hh].T
            else:
                st_scr[hh, d] = jnp.zeros((DV_H, DK_H), F32)

    row = lax.broadcasted_iota(jnp.int32, (c, 1), 0)
    col = lax.broadcasted_iota(jnp.int32, (1, c), 1)

    def fast_block(d, r0):
        reverse = d == 1
        rows = pl.ds(r0, rb)
        n = hp * cpb
        flat = lambda a: a.reshape(hp * rb, a.shape[-1])
        chunks = lambda a: a.reshape(n, c, a.shape[-1])
        vb = chunks(hi_ref[:, 0, rows, :])
        b = _chunk_scan(flat(g_refs[d][:, 0, rows, :]), reverse)
        qd = chunks((queries(flat(hq_ref[:, 0, rows, :])) * jnp.exp(b)).astype(BF16))
        ku = chunks(flat(k_refs[d][:, 0, rows, :]).astype(F32) * jnp.exp(-b))
        b = chunks(b)
        e_end = jnp.exp(b[:, 0:1, :] if reverse else b[:, c - 1:c, :])
        causal = ((row <= col) if reverse else (row >= col))[None]
        a = jnp.einsum('nck,nsk->ncs', qd, ku.astype(BF16), preferred_element_type=F32)
        o = jnp.einsum('ncs,nsv->ncv', jnp.where(causal, a, 0.0).astype(BF16), vb, preferred_element_type=F32)
        u = jnp.einsum('nsv,nsk->nvk', vb, (ku * e_end).astype(BF16), preferred_element_type=F32)
        entering = [None] * n
        for hh in range(hp):
            st = st_scr[hh, d]
            for j in (range(cpb - 1, -1, -1) if reverse else range(cpb)):
                entering[hh * cpb + j] = st.astype(BF16)
                st = e_end[hh * cpb + j] * st + u[hh * cpb + j]
            st_scr[hh, d] = st
        o = o + jnp.einsum('nck,nvk->ncv', qd, jnp.stack(entering), preferred_element_type=F32)
        od_scr[:, d, rows, :] = o.reshape(hp, rb, DV_H)

    def exact_chunk(hh, d, r0):
        reverse = d == 1
        rows = pl.ds(r0, c)
        b = _chunk_scan(g_refs[d][hh, 0, rows, :], reverse)
        b_scr[...] = b
        q_scr[...] = queries(hq_ref[hh, 0, rows, :])
        k_scr[...] = k_refs[d][hh, 0, rows, :].astype(F32)

        def one_row(t, carry):
            bt = b_scr[pl.ds(t, 1), :]
            qt = q_scr[pl.ds(t, 1), :]
            seen = (row >= t) if reverse else (row <= t)
            w = jnp.exp(jnp.where(seen, bt - b_scr[...], -jnp.inf))
            p = jnp.sum(qt * k_scr[...] * w, axis=-1, keepdims=True)
            od_scr[hh, d, pl.ds(r0 + t, 1), :] = jnp.sum(p * hi_ref[hh, 0, rows, :].astype(F32),
                                                         axis=0, keepdims=True)
            return carry

        lax.fori_loop(0, c, one_row, 0)
        b_end = b[0:1] if reverse else b[c - 1:c]
        st = st_scr[hh, d]
        qd = (q_scr[...] * jnp.exp(b)).astype(BF16)
        od_scr[hh, d, rows, :] = od_scr[hh, d, rows, :] + _nt_dot(qd, st.astype(BF16))
        k_end = (k_scr[...] * jnp.exp(b_end - b)).astype(BF16)
        st_scr[hh, d] = jnp.exp(b_end) * st + _tn_dot(hi_ref[hh, 0, rows, :], k_end)

    def block(i, carry):
        starts = (pl.multiple_of(i * rb, rb), pl.multiple_of((nb - 1 - i) * rb, rb))
        mild = (ok_scr[0, i] + ok_scr[1, nb - 1 - i]) == 2

        @pl.when(mild)
        def _():
            for d in range(2):
                fast_block(d, starts[d])

        @pl.when(jnp.logical_not(mild))
        def _():
            for hh in range(hp):
                for d in range(2):
                    for j in (range(cpb - 1, -1, -1) if d == 1 else range(cpb)):
                        exact_chunk(hh, d, pl.multiple_of(starts[d] + j * c, c))

        return carry

    lax.fori_loop(0, nb, block, 0)

    def finish(i, carry):
        rows = pl.ds(pl.multiple_of(i * rb, rb), rb)
        for hh in range(hp):
            o = od_scr[hh, 0, rows, :] + od_scr[hh, 1, rows, :]
            o = o * lax.rsqrt(jnp.mean(o * o, axis=-1, keepdims=True) + RMS_EPS) * ng_ref[...]
            hg = hg_ref[hh, 0, rows, :].astype(F32)
            o_ref[hh, 0, rows, :] = (o * (hg * jax.nn.sigmoid(hg))).astype(o_ref.dtype)
        return carry

    lax.fori_loop(0, nb, finish, 0)
    if want_state:
        for hh in range(hp):
            for d in range(2):
                sfin_ref[0, d, hh] = st_scr[hh, d].T


def _hgrn(plain, g, kd, norm_g, s0, want_state):
    _, bsz, t, _ = plain.shape
    hp = HGRN_SHORT_HEADS if t <= HGRN_ROWS else 1
    stream = lambda s: pl.BlockSpec((hp, 1, t, DK_H), lambda b, h: (s * N_HEADS_H // hp + h, b, 0, 0))
    in_specs = [stream(0), stream(1), stream(2), stream(0), stream(1), stream(0), stream(1),
                pl.BlockSpec((1, DV_H), lambda b, h: (0, 0))]
    args = [plain, plain, plain, g, g, kd, kd, norm_g]
    state_spec = pl.BlockSpec((1, 2, hp, DK_H, DV_H), lambda b, h: (b, 0, h, 0, 0))
    if s0 is not None:
        in_specs.append(state_spec)
        args.append(s0)
    out_specs = [pl.BlockSpec((hp, 1, t, DV_H), lambda b, h: (h, b, 0, 0))]
    out_shape = [jax.ShapeDtypeStruct((N_HEADS_H, bsz, t, DV_H), BF16)]
    if want_state:
        out_specs.append(state_spec)
        out_shape.append(jax.ShapeDtypeStruct((bsz, 2, N_HEADS_H, DK_H, DV_H), F32))
    n_blocks = t // min(HGRN_ROWS, t)
    chunk = pltpu.VMEM((HGRN_C, DK_H), F32)
    outs = pl.pallas_call(
        functools.partial(_hgrn_kernel, seq_len=t, has_s0=s0 is not None, want_state=want_state),
        grid=(bsz, N_HEADS_H // hp),
        in_specs=in_specs,
        out_specs=out_specs,
        out_shape=out_shape,
        scratch_shapes=[
            pltpu.VMEM((hp, 2, t, DV_H), F32),
            pltpu.VMEM((hp, 2, DV_H, DK_H), F32),
            chunk, chunk, chunk,
            pltpu.SMEM((2, n_blocks), jnp.int32),
        ],
        compiler_params=_params(("parallel", "parallel")),
        name="hgrn",
    )(*args)
    return outs if want_state else (outs[0], None)


def _merge_ffn2_kernel(oa0_ref, oh0_ref, oa1_ref, oh1_ref, h_ref, x_ref, mod_ref,
                       wmix_hbm, wa_hbm, wh_hbm, wo_hbm, win_hbm, wout_hbm, lng_ref, lnb_ref, y0_ref, y1_ref,
                       wg_scr, wa_scr, wh_scr, wo_scr,
                       win_scr, wout_scr, wide_stage, square_stage, wide_sems, square_sems, act_scr, *, tiles_0):
    i = pl.program_id(0)
    first_group = i < tiles_0

    @pl.when(i == 0)
    def _():
        _load_weight(win_hbm, win_scr, wide_stage, wide_sems)
        for half in range(2):
            _load_weight(wmix_hbm, wg_scr, square_stage, square_sems,
                         src_col=W_GA0 * MIX_TN + half * D_MODEL, dst_col=half * D_MODEL)
        for w_hbm, w_scr in ((wa_hbm, wa_scr), (wh_hbm, wh_scr), (wo_hbm, wo_scr), (wout_hbm, wout_scr)):
            _load_weight(w_hbm, w_scr, square_stage, square_sems)

    def heads(ref0, ref1):
        pick = lambda hh: jnp.where(first_group, ref0[hh], ref1[hh])
        return jnp.concatenate([pick(hh) for hh in range(ref0.shape[0])], axis=-1)

    gates = jax.nn.sigmoid(jnp.dot(h_ref[...], wg_scr[...], preferred_element_type=F32))
    ba = jnp.dot(heads(oa0_ref, oa1_ref), wa_scr[...], preferred_element_type=F32)
    bh = jnp.dot(heads(oh0_ref, oh1_ref), wh_scr[...], preferred_element_type=F32)
    merged = gates[:, :D_MODEL] * ba + gates[:, D_MODEL:] * bh
    mix = jnp.dot(merged.astype(BF16), wo_scr[...], preferred_element_type=F32)
    x2 = _layer_norm(ALPHA * x_ref[...] + mod_ref[0, 5:6, :] * mix, lng_ref[1:2, :], lnb_ref[1:2, :])
    y = _half_step(x2, mod_ref, win_scr, wout_scr, lng_ref, lnb_ref, act_scr, 2)

    @pl.when(first_group)
    def _():
        y0_ref[...] = y

    @pl.when(jnp.logical_not(first_group))
    def _():
        y1_ref[...] = y


def _merge_ffn2(mixed, h, x, mod, cond_of_tile, w_mix, w_a, w_h, w_o, w_in, w_out, ln_g, ln_b):
    (oa0, oh0), (oa1, oh1) = mixed
    tiles_0, tiles_1 = oa0.shape[1] // FFN_TM, oa1.shape[1] // FFN_TM
    row = lambda i: (i, 0)
    const = lambda i: (0, 0)
    in_group_0 = lambda i: jnp.minimum(i, tiles_0 - 1)
    in_group_1 = lambda i: jnp.maximum(i - tiles_0, 0)
    hbm = pl.BlockSpec(memory_space=pl.ANY)
    return pl.pallas_call(
        functools.partial(_merge_ffn2_kernel, tiles_0=tiles_0),
        grid=(tiles_0 + tiles_1,),
        in_specs=[
            pl.BlockSpec((N_HEADS_A, FFN_TM, DV_A), lambda i: (0, in_group_0(i), 0)),
            pl.BlockSpec((N_HEADS_H, FFN_TM, DV_H), lambda i: (0, in_group_0(i), 0)),
            pl.BlockSpec((N_HEADS_A, FFN_TM, DV_A), lambda i: (0, in_group_1(i), 0)),
            pl.BlockSpec((N_HEADS_H, FFN_TM, DV_H), lambda i: (0, in_group_1(i), 0)),
            pl.BlockSpec((FFN_TM, D_MODEL), row),
            pl.BlockSpec((FFN_TM, D_MODEL), row),
            pl.BlockSpec((1, 9, D_MODEL), lambda i: (cond_of_tile(i, FFN_TM), 0, 0)),
            hbm, hbm, hbm, hbm, hbm, hbm,
            pl.BlockSpec((3, D_MODEL), const),
            pl.BlockSpec((3, D_MODEL), const),
        ],
        out_specs=[pl.BlockSpec((FFN_TM, D_MODEL), lambda i: (in_group_0(i), 0)),
                   pl.BlockSpec((FFN_TM, D_MODEL), lambda i: (in_group_1(i), 0))],
        out_shape=[jax.ShapeDtypeStruct((tiles_0 * FFN_TM, D_MODEL), F32),
                   jax.ShapeDtypeStruct((tiles_1 * FFN_TM, D_MODEL), F32)],
        scratch_shapes=[pltpu.VMEM((D_MODEL, 2 * D_MODEL), BF16), pltpu.VMEM((W_A, D_MODEL), BF16),
                        pltpu.VMEM((W_H, D_MODEL), BF16), pltpu.VMEM((D_MODEL, D_MODEL), BF16)]
                       + _ffn_weight_scratch() + [pltpu.VMEM((FFN_TM, D_FF), BF16)],
        compiler_params=_params(("arbitrary",)),
        name="merge_ffn2",
    )(oa0, oh0, oa1, oh1, h, x, mod, w_mix, w_a, w_h, w_o, w_in, w_out, ln_g, ln_b)


def _rope_tables(n_tok):
    rows = n_tok // GRID_W
    row = np.repeat(np.arange(rows, dtype=np.float32), GRID_W)
    col = np.tile(np.arange(GRID_W, dtype=np.float32), rows)
    half = DH_A // 2
    inv = (ROPE_BASE ** (-np.arange(0, half, 2, dtype=np.float32) / half)).astype(np.float32)
    ar = row[:, None] * inv
    ac = col[:, None] * inv
    ang = np.concatenate([ar, ar, ac, ac] * 2, axis=-1)
    cos, sin = np.cos(ang).astype(np.float32), np.sin(ang).astype(np.float32)
    scale = np.float32(Q_SCALE)
    return (jnp.asarray(np.stack([cos * scale, cos, np.ones_like(cos)])),
            jnp.asarray(np.stack([sin * scale, sin, np.zeros_like(sin)])))


def _mixer(h1, group, wts, lam_init, bsz, ctx):
    m = MIX_TM
    t = m // bsz
    latent = ctx is not None
    per_head = lambda a: a.reshape(a.shape[0], bsz, t, a.shape[-1])
    if latent:
        qkv, plain, g, kd = _project(h1, group, wts['w_mix_in'], wts['lb_logits'], _rope_tables(t), False)
        k = v = None
        ctx_kv = (ctx[0], ctx[1])
    else:
        qkv, kv32, plain, g, kd = _project(h1, group, wts['w_mix_in'], wts['lb_logits'], None, True)
        k, v = kv32[0], kv32[1]
        ctx_kv = None
    qkv = per_head(qkv)
    q_kv = [(qkv, W_Q * N_HEADS_A), (qkv, W_K * N_HEADS_A), (qkv, W_V * N_HEADS_A)]
    o_a = _attention(*q_kv, ctx_kv, wts['lam_params'], wts['attn_subln_g'], lam_init)
    o_h, s_fin = _hgrn(per_head(plain), per_head(g), per_head(kd), wts['hgrn_norm_g'],
                       ctx[2] if latent else None, not latent)
    return (o_a.reshape(N_HEADS_A, m, DV_A), o_h.reshape(N_HEADS_H, m, DV_H)), (k, v, s_fin)


def kernel(x_prompt, x_sample, cache_k, cache_v, state_hgrn, c, c_ctx, w_ada, b_ada, ffn1_w_in, ffn1_w_out,
           w_mix_in, lambda_q1, lambda_k1, lambda_q2, lambda_k2, attn_subln_g, hgrn_lb_logits, hgrn_norm_g,
           w_branch_a, w_branch_h, w_mix_out, ffn2_w_in, ffn2_w_out, ln_g, ln_b):
    batch, seq, _ = x_prompt.shape
    dec_batch, dec_seq, _ = x_sample.shape
    past = cache_k.shape[2]
    assert DEPTH == 1 and dec_batch + 1 <= MOD_ROWS
    lam_init = 0.8 - 0.6 * math.exp(-0.3 * 0)

    cond = jnp.concatenate([c_ctx[None, :], c, jnp.zeros((MOD_ROWS - 1 - dec_batch, D_MODEL), F32)], axis=0)
    mod = _modulation(cond, w_ada[0], b_ada[0])

    wts = {
        'w_mix_in': w_mix_in[0],
        'lam_params': jnp.concatenate([lambda_q1, lambda_k1, lambda_q2, lambda_k2], axis=0),
        'attn_subln_g': attn_subln_g, 'lb_logits': hgrn_lb_logits, 'hgrn_norm_g': hgrn_norm_g,
    }
    n_ctx = batch * seq
    assert n_ctx == MIX_TM and dec_batch * dec_seq == MIX_TM

    def cond_of_tile(i, tm):
        first = i * tm
        return jnp.where(first < n_ctx, 0, 1 + (first - n_ctx) // dec_seq)

    x1, h1 = _ffn1(x_prompt.reshape(n_ctx, D_MODEL), x_sample.reshape(dec_batch * dec_seq, D_MODEL), mod,
                   cond_of_tile, ffn1_w_in[0], ffn1_w_out[0], ln_g[0], ln_b[0])
    mixed_p, (k_c, v_c, s_c) = _mixer(h1, 0, wts, lam_init, batch, None)
    ctx = (cache_k[:, 0].reshape(dec_batch, past, QK_A), cache_v[:, 0].reshape(dec_batch, past, W_A),
           state_hgrn[:, 0])
    mixed_s, _ = _mixer(h1, 1, wts, lam_init, dec_batch, ctx)
    y_p, y_s = _merge_ffn2((mixed_p, mixed_s), h1, x1, mod, cond_of_tile, w_mix_in[0],
                           w_branch_a[0], w_branch_h[0], w_mix_out[0], ffn2_w_in[0], ffn2_w_out[0],
                           ln_g[0], ln_b[0])

    return (y_p.reshape(batch, seq, D_MODEL),
            y_s.reshape(dec_batch, dec_seq, D_MODEL),
            k_c.reshape(batch, DEPTH, seq, N_HEADS_A, 2, DH_A),
            v_c.reshape(batch, DEPTH, seq, N_HEADS_A, DV_A),
            s_c.reshape(batch, DEPTH, 2, N_HEADS_H, DK_H, DV_H))
```

```python
import functools
import math

import numpy as np
import jax
import jax.numpy as jnp
from jax import lax
from jax.experimental import pallas as pl
from jax.experimental.pallas import tpu as pltpu

D_MODEL = 1024
DEPTH = 1
GRID_W = 64
N_HEADS_A = 4
DH_A = 64
DV_A = 2 * DH_A
QK_A = N_HEADS_A * 2 * DH_A
W_A = N_HEADS_A * DV_A
ROPE_BASE = 10000.0
N_HEADS_H = 4
DK_H = 128
DV_H = 128
QK_H = N_HEADS_H * DK_H
W_H = N_HEADS_H * DV_H
D_FF = 2816
MIX_IN = 2 * QK_A + W_A + 3 * QK_H + 2 * W_H + 2 * D_MODEL
ALPHA = (2 * DEPTH) ** 0.25
LN_EPS = 1e-5
RMS_EPS = 1e-6
Q_SCALE = DH_A ** -0.5 * math.log2(math.e)

F32 = jnp.float32
BF16 = jnp.bfloat16

LANES = 128
VMEM_LIMIT = 56 * 1024 * 1024

FFN_TM = 512
FFN_TF = 256
WIDE_CHUNK_ROWS = 128
SQUARE_CHUNK_ROWS = 256
MIX_TM = 4096
PROJ_TM = 2048
PROJ_SUB = 256
MIX_TN = 512
ATT_TQ = 1024
ATT_SUB = 512
ATT_SHORT_SEQ = 256
HGRN_C = 64
HGRN_ROWS = 1024
HGRN_SHORT_HEADS = 4
HGRN_SAFE_DECAY = 75.0
MOD_ROWS = 8
MOD_TN = 2304

W_Q, W_K, W_V, W_HQ, W_HFF, W_HFB, W_HI, W_HG, W_GA0, W_GA1, W_GH0, W_GH1 = range(12)


def _params(sem):
    return pltpu.CompilerParams(dimension_semantics=sem, vmem_limit_bytes=VMEM_LIMIT)


def _nt_dot(a, b):
    return lax.dot_general(a, b, (((1,), (1,)), ((), ())), preferred_element_type=F32)


def _tn_dot(a, b):
    return lax.dot_general(a, b, (((0,), (0,)), ((), ())), preferred_element_type=F32)


def _layer_norm(y, g, b):
    mu = jnp.mean(y, axis=-1, keepdims=True)
    yc = y - mu
    var = jnp.mean(yc * yc, axis=-1, keepdims=True)
    return yc * lax.rsqrt(var + LN_EPS) * g + b


def _mod_kernel(c_ref, w_ref, b_ref, o_ref):
    c = c_ref[...]
    a = (c * jax.nn.sigmoid(c)).astype(BF16)
    o_ref[...] = jnp.dot(a, w_ref[...].astype(BF16), preferred_element_type=F32) + b_ref[...]


def _modulation(cond, w_ada, b_ada):
    n_out = w_ada.shape[1]
    tn = MOD_TN
    out = pl.pallas_call(
        _mod_kernel,
        grid=(n_out // tn,),
        in_specs=[
            pl.BlockSpec((MOD_ROWS, D_MODEL), lambda n: (0, 0)),
            pl.BlockSpec((D_MODEL, tn), lambda n: (0, n)),
            pl.BlockSpec((1, tn), lambda n: (0, n)),
        ],
        out_specs=pl.BlockSpec((MOD_ROWS, tn), lambda n: (0, n)),
        out_shape=jax.ShapeDtypeStruct((MOD_ROWS, n_out), F32),
        compiler_params=_params(("arbitrary",)),
        name="modulation",
    )(cond, w_ada, b_ada.reshape(1, n_out))
    return out.reshape(MOD_ROWS, 9, D_MODEL)


def _load_weight(w_hbm, w_scr, stage, sems, src_col=0, dst_col=0):
    _, chunk, width = stage.shape
    n_chunks = w_hbm.shape[0] // chunk

    def copy(c):
        return pltpu.make_async_copy(w_hbm.at[pl.ds(c * chunk, chunk), pl.ds(src_col, width)],
                                     stage.at[c % 2], sems.at[c % 2])

    copy(0).start()
    for c in range(n_chunks):
        if c + 1 < n_chunks:
            copy(c + 1).start()
        copy(c).wait()
        w_scr[c * chunk:(c + 1) * chunk, dst_col:dst_col + width] = stage[c % 2].astype(BF16)


def _modulate(x, mod_ref, sub):
    return x * (1.0 + mod_ref[0, 3 * sub + 1:3 * sub + 2, :]) + mod_ref[0, 3 * sub:3 * sub + 1, :]


def _half_step(x, mod_ref, win_ref, wout_ref, lng_ref, lnb_ref, act_scr, sub):
    h = _modulate(x, mod_ref, sub).astype(BF16)
    for c in range(D_FF // FFN_TF):
        cols = slice(c * FFN_TF, (c + 1) * FFN_TF)
        up_cols = slice(D_FF + c * FFN_TF, D_FF + (c + 1) * FFN_TF)
        gate = jnp.dot(h, win_ref[:, cols], preferred_element_type=F32)
        up = jnp.dot(h, win_ref[:, up_cols], preferred_element_type=F32)
        act_scr[:, cols] = (gate * jax.nn.sigmoid(gate) * up).astype(BF16)
    f = jnp.dot(act_scr[...], wout_ref[...], preferred_element_type=F32)
    g = mod_ref[0, 3 * sub + 2:3 * sub + 3, :]
    return _layer_norm(ALPHA * x + 0.5 * g * f, lng_ref[sub:sub + 1, :], lnb_ref[sub:sub + 1, :])


def _ffn_weight_scratch():
    return [
        pltpu.VMEM((D_MODEL, 2 * D_FF), BF16),
        pltpu.VMEM((D_FF, D_MODEL), BF16),
        pltpu.VMEM((2, WIDE_CHUNK_ROWS, 2 * D_FF), F32),
        pltpu.VMEM((2, SQUARE_CHUNK_ROWS, D_MODEL), F32),
        pltpu.SemaphoreType.DMA((2,)),
        pltpu.SemaphoreType.DMA((2,)),
    ]


def _ffn1_kernel(xa_ref, xb_ref, mod_ref, win_hbm, wout_hbm, lng_ref, lnb_ref, xo_ref, ho_ref,
                 win_scr, wout_scr, wide_stage, square_stage, wide_sems, square_sems, act_scr, *, tiles_a):
    i = pl.program_id(0)

    @pl.when(i == 0)
    def _():
        _load_weight(win_hbm, win_scr, wide_stage, wide_sems)
        _load_weight(wout_hbm, wout_scr, square_stage, square_sems)

    x = jnp.where(i < tiles_a, xa_ref[...], xb_ref[...])
    xn = _half_step(x, mod_ref, win_scr, wout_scr, lng_ref, lnb_ref, act_scr, 0)
    xo_ref[...] = xn
    ho_ref[...] = _modulate(xn, mod_ref, 1).astype(BF16)


def _ffn1(xa, xb, mod, cond_of_tile, w_in, w_out, ln_g, ln_b):
    tiles_a, tiles_b = xa.shape[0] // FFN_TM, xb.shape[0] // FFN_TM
    m = xa.shape[0] + xb.shape[0]
    row = lambda i: (i, 0)
    const = lambda i: (0, 0)
    hbm = pl.BlockSpec(memory_space=pl.ANY)
    return pl.pallas_call(
        functools.partial(_ffn1_kernel, tiles_a=tiles_a),
        grid=(tiles_a + tiles_b,),
        in_specs=[
            pl.BlockSpec((FFN_TM, D_MODEL), lambda i: (jnp.minimum(i, tiles_a - 1), 0)),
            pl.BlockSpec((FFN_TM, D_MODEL), lambda i: (jnp.maximum(i - tiles_a, 0), 0)),
            pl.BlockSpec((1, 9, D_MODEL), lambda i: (cond_of_tile(i, FFN_TM), 0, 0)),
            hbm, hbm,
            pl.BlockSpec((3, D_MODEL), const),
            pl.BlockSpec((3, D_MODEL), const),
        ],
        out_specs=[pl.BlockSpec((FFN_TM, D_MODEL), row), pl.BlockSpec((FFN_TM, D_MODEL), row)],
        out_shape=[jax.ShapeDtypeStruct((m, D_MODEL), F32), jax.ShapeDtypeStruct((m, D_MODEL), BF16)],
        scratch_shapes=_ffn_weight_scratch() + [pltpu.VMEM((FFN_TM, D_FF), BF16)],
        compiler_params=_params(("arbitrary",)),
        name="ffn1",
    )(xa, xb, mod, w_in, w_out, ln_g, ln_b)


def _rope(x, cos, sin):
    half, quarter = DH_A // 2, DH_A // 4
    lane = lax.broadcasted_iota(jnp.int32, (1, LANES), 1)
    first_quarter = (lane % half) < quarter
    outs = []
    for hb in range(QK_A // LANES):
        xb = x[:, hb * LANES:(hb + 1) * LANES]
        ahead = pltpu.roll(xb, LANES - quarter, 1)
        behind = pltpu.roll(xb, quarter, 1)
        rot = jnp.where(first_quarter, -ahead, behind)
        outs.append(xb * cos + rot * sin)
    return jnp.concatenate(outs, axis=-1)


def _forget_gate(z, logits):
    mx = jnp.max(logits, axis=0, keepdims=True)
    el = jnp.exp(logits - mx)
    lb = el[0:1] / jnp.sum(el, axis=0, keepdims=True)
    sig = 0.5 + 0.5 * jnp.tanh(0.5 * z)
    u = (1.0 - lb) * sig
    f = jnp.maximum(lb + u, jnp.finfo(jnp.float32).tiny)
    return jnp.log(f), (1.0 - lb) - u


def _proj_kernel(*refs, rope, want_kv32):
    refs = list(refs)
    h_ref, w_ref, lbl_ref = refs[:3]
    refs = refs[3:]
    cos_ref, sin_ref = (refs.pop(0), refs.pop(0)) if rope else (None, None)
    qkv_ref = refs.pop(0)
    kv32_ref = refs.pop(0) if want_kv32 else None
    plain_ref, g_ref, kd_ref = refs
    n = pl.program_id(1)

    pieces = [slice(r, r + PROJ_SUB) for r in range(0, h_ref.shape[0], PROJ_SUB)]

    def project(rows, w):
        return jnp.dot(h_ref[rows, :], w, preferred_element_type=F32)

    def store_heads(ref, rows, y):
        for hh in range(MIX_TN // LANES):
            ref[hh, rows, :] = y[:, hh * LANES:(hh + 1) * LANES].astype(ref.dtype)

    @pl.when(n <= W_V)
    def _():
        w = w_ref[...].astype(BF16)
        for rows in pieces:
            y = project(rows, w)
            if want_kv32:
                kv32_ref[0, rows, :] = y
            if rope:
                first = rows.start % cos_ref.shape[1]
                table_rows = slice(first, first + PROJ_SUB)
                y = _rope(y, cos_ref[0, table_rows, :], sin_ref[0, table_rows, :])
            else:
                y = y * jnp.where(n == W_Q, Q_SCALE, 1.0)
            store_heads(qkv_ref, rows, y)

    @pl.when((n == W_HFF) | (n == W_HFB))
    def _():
        w = w_ref[...].astype(BF16)
        for rows in pieces:
            log_f, kd = _forget_gate(project(rows, w), lbl_ref[0])
            store_heads(g_ref, rows, log_f)
            store_heads(kd_ref, rows, kd)

    @pl.when((n == W_HQ) | (n >= W_HI))
    def _():
        w = w_ref[...].astype(BF16)
        for rows in pieces:
            store_heads(plain_ref, rows, project(rows, w))


def _project(h, group, w, lb_logits, rope_tables, want_kv32):
    m = MIX_TM
    tm = PROJ_TM
    tiles = m // tm
    rope = rope_tables is not None
    heads = MIX_TN // LANES
    n_blocks = W_HG + 1
    in_specs = [pl.BlockSpec((tm, D_MODEL), lambda i, n: (group * tiles + i, 0)),
                pl.BlockSpec((D_MODEL, MIX_TN), lambda i, n: (0, n)),
                pl.BlockSpec((1, DEPTH + 1, QK_H), lambda i, n: (jnp.clip(n - W_HFF, 0, 1), 0, 0))]
    args = [h, w, lb_logits]
    qkv_col = lambda n: jnp.minimum(n, W_V)
    if rope:
        t_rope = rope_tables[0].shape[1]
        rope_rows = min(t_rope, tm)
        n_rope_tiles = t_rope // rope_rows
        for t in rope_tables:
            in_specs.append(pl.BlockSpec((1, rope_rows, DV_A), lambda i, n: (qkv_col(n), i % n_rope_tiles, 0)))
            args.append(t)
    head_block = (heads, tm, LANES)
    plain_col = lambda n: jnp.where(n < W_HI, 0, n - W_HI + 1)
    gate_col = lambda n: jnp.clip(n - W_HFF, 0, 1)
    out_specs = [pl.BlockSpec(head_block, lambda i, n: (qkv_col(n), i, 0))]
    out_shape = [jax.ShapeDtypeStruct((3 * heads, m, LANES), BF16)]
    if want_kv32:
        out_specs.append(pl.BlockSpec((1, tm, MIX_TN), lambda i, n: (jnp.clip(n - W_K, 0, 1), i, 0)))
        out_shape.append(jax.ShapeDtypeStruct((2, m, MIX_TN), F32))
    out_specs += [pl.BlockSpec(head_block, lambda i, n: (plain_col(n), i, 0)),
                  pl.BlockSpec(head_block, lambda i, n: (gate_col(n), i, 0)),
                  pl.BlockSpec(head_block, lambda i, n: (gate_col(n), i, 0))]
    out_shape += [jax.ShapeDtypeStruct((3 * heads, m, LANES), BF16),
                  jax.ShapeDtypeStruct((2 * heads, m, LANES), F32),
                  jax.ShapeDtypeStruct((2 * heads, m, LANES), BF16)]
    return pl.pallas_call(
        functools.partial(_proj_kernel, rope=rope, want_kv32=want_kv32),
        grid=(tiles, n_blocks),
        in_specs=in_specs,
        out_specs=out_specs,
        out_shape=out_shape,
        compiler_params=_params(("arbitrary", "arbitrary")),
        name="mix_proj",
    )(*args)


def _attn_kernel(*refs, lam_init, has_ctx):
    if has_ctx:
        lam_ref, g_ref, q_ref, k_ref, v_ref, kc_ref, vc_ref, o_ref = refs
    else:
        lam_ref, g_ref, q_ref, k_ref, v_ref, o_ref = refs
    lp = lam_ref[...]
    lam = (jnp.exp(jnp.sum(lp[0:1] * lp[1:2], axis=-1, keepdims=True))
           - jnp.exp(jnp.sum(lp[2:3] * lp[3:4], axis=-1, keepdims=True)) + lam_init)

    lane = lax.broadcasted_iota(jnp.int32, (1, DV_A), 1)
    comp0 = lane < DH_A
    tq = q_ref.shape[2]
    sub = min(ATT_SUB, tq)
    problems = []
    for hh in range(q_ref.shape[0]):
        keys = [k_ref[hh, 0]]
        vals = [v_ref[hh, 0]]
        if has_ctx:
            keys.append(kc_ref[0, :, hh * DV_A:(hh + 1) * DV_A].astype(BF16))
            vals.append(vc_ref[0, :, hh * DV_A:(hh + 1) * DV_A].astype(BF16))
        for r in range(tq // sub):
            q = q_ref[hh, 0, r * sub:(r + 1) * sub, :]
            zero = jnp.zeros_like(q)
            for qc in (jnp.where(comp0, q, zero), jnp.where(comp0, zero, q)):
                problems.append((qc, keys, vals))
    scores = [[_nt_dot(kk, qc) for kk in keys] for qc, keys, _ in problems]
    weights = []
    for ss in scores:
        mx = functools.reduce(jnp.maximum, [jnp.max(s, axis=0, keepdims=True) for s in ss])
        es = [jnp.exp2(s - mx) for s in ss]
        den = functools.reduce(jnp.add, [jnp.sum(e, axis=0, keepdims=True) for e in es])
        weights.append(([e.astype(BF16) for e in es], 1.0 / den))
    maps = [functools.reduce(jnp.add, [_tn_dot(vv, e) for e, vv in zip(es, vals)]) * r
            for (es, r), (_, _, vals) in zip(weights, problems)]
    idx = 0
    for hh in range(q_ref.shape[0]):
        for r in range(tq // sub):
            o = maps[idx] - lam * maps[idx + 1]
            idx += 2
            o = o * lax.rsqrt(jnp.mean(o * o, axis=0, keepdims=True) + RMS_EPS)
            o = o.T * g_ref[...]
            o_ref[hh, 0, r * sub:(r + 1) * sub, :] = (o * (1.0 - lam_init)).astype(o_ref.dtype)


def _attention(q, k, v, ctx_kv, lam_params, subln_g, lam_init):
    (q_arr, q_row), (k_arr, k_row), (v_arr, v_row) = q, k, v
    _, bsz, t, _ = q_arr.shape
    tq = min(ATT_TQ, t)
    hp = N_HEADS_A if t <= ATT_SHORT_SEQ else 1
    const = lambda b, h, i: (0, 0)
    rows = lambda first: (lambda b, h, i: (first // hp + h, b, 0, 0))
    in_specs = [
        pl.BlockSpec((4, DH_A), const),
        pl.BlockSpec((1, DV_A), const),
        pl.BlockSpec((hp, 1, tq, DV_A), lambda b, h, i: (q_row // hp + h, b, i, 0)),
        pl.BlockSpec((hp, 1, t, DV_A), rows(k_row)),
        pl.BlockSpec((hp, 1, t, DV_A), rows(v_row)),
    ]
    args = [lam_params, subln_g, q_arr, k_arr, v_arr]
    if ctx_kv is not None:
        p = ctx_kv[0].shape[1]
        head_all = lambda b, h, i: (b, 0, h)
        in_specs += [pl.BlockSpec((1, p, hp * DV_A), head_all), pl.BlockSpec((1, p, hp * DV_A), head_all)]
        args += list(ctx_kv)
    return pl.pallas_call(
        functools.partial(_attn_kernel, lam_init=lam_init, has_ctx=ctx_kv is not None),
        grid=(bsz, N_HEADS_A // hp, t // tq),
        in_specs=in_specs,
        out_specs=pl.BlockSpec((hp, 1, tq, DV_A), lambda b, h, i: (h, b, i, 0)),
        out_shape=jax.ShapeDtypeStruct((N_HEADS_A, bsz, t, DV_A), BF16),
        compiler_params=_params(("parallel", "parallel", "arbitrary")),
        name="attention",
    )(*args)


def _chunk_scan(g, reverse):
    r = g.shape[0]
    pos = lax.broadcasted_iota(jnp.int32, (r, 1), 0) % HGRN_C
    b = g
    sh = 1
    while sh < HGRN_C:
        if reverse:
            b = b + jnp.where(pos < HGRN_C - sh, pltpu.roll(b, r - sh, 0), 0.0)
        else:
            b = b + jnp.where(pos >= sh, pltpu.roll(b, sh, 0), 0.0)
        sh *= 2
    return b


def _hgrn_kernel(*refs, seq_len, has_s0, want_state):
    refs = list(refs)
    hq_ref, hi_ref, hg_ref, gf_ref, gb_ref, kf_ref, kb_ref, ng_ref = refs[:8]
    refs = refs[8:]
    s0_ref = refs.pop(0) if has_s0 else None
    o_ref = refs.pop(0)
    sfin_ref = refs.pop(0) if want_state else None
    od_scr, st_scr, b_scr, q_scr, k_scr, ok_scr = refs
    hp = hq_ref.shape[0]
    c = HGRN_C
    rb = min(HGRN_ROWS, seq_len)
    nb = seq_len // rb
    cpb = rb // c
    g_refs = (gf_ref, gb_ref)
    k_refs = (kf_ref, kb_ref)

    def queries(hq):
        hq = hq.astype(F32)
        return hq * jax.nn.sigmoid(hq)

    def flags(i, carry):
        rows = pl.ds(pl.multiple_of(i * rb, rb), rb)
        for d in range(2):
            worst = jnp.zeros((1, DK_H), F32)
            for hh in range(hp):
                g = g_refs[d][hh, 0, rows, :]
                for j in range(cpb):
                    worst = jnp.maximum(worst, -jnp.sum(g[j * c:(j + 1) * c], axis=0, keepdims=True))
            ok_scr[d, i] = (jnp.max(worst) <= HGRN_SAFE_DECAY).astype(jnp.int32)
        return carry

    lax.fori_loop(0, nb, flags, 0)

    for hh in range(hp):
        for d in range(2):
            if has_s0:
                st_scr[hh, d] = s0_ref[0, d, hh].T
            else:
                st_scr[hh, d] = jnp.zeros((DV_H, DK_H), F32)

    row = lax.broadcasted_iota(jnp.int32, (c, 1), 0)
    col = lax.broadcasted_iota(jnp.int32, (1, c), 1)

    def fast_block(d, r0):
        reverse = d == 1
        rows = pl.ds(r0, rb)
        n = hp * cpb
        flat = lambda a: a.reshape(hp * rb, a.shape[-1])
        chunks = lambda a: a.reshape(n, c, a.shape[-1])
        vb = chunks(hi_ref[:, 0, rows, :])
        b = _chunk_scan(flat(g_refs[d][:, 0, rows, :]), reverse)
        qd = chunks((queries(flat(hq_ref[:, 0, rows, :])) * jnp.exp(b)).astype(BF16))
        ku = chunks(flat(k_refs[d][:, 0, rows, :]).astype(F32) * jnp.exp(-b))
        b = chunks(b)
        e_end = jnp.exp(b[:, 0:1, :] if reverse else b[:, c - 1:c, :])
        causal = ((row <= col) if reverse else (row >= col))[None]
        a = jnp.einsum('nck,nsk->ncs', qd, ku.astype(BF16), preferred_element_type=F32)
        o = jnp.einsum('ncs,nsv->ncv', jnp.where(causal, a, 0.0).astype(BF16), vb, preferred_element_type=F32)
        u = jnp.einsum('nsv,nsk->nvk', vb, (ku * e_end).astype(BF16), preferred_element_type=F32)
        entering = [None] * n
        for hh in range(hp):
            st = st_scr[hh, d]
            for j in (range(cpb - 1, -1, -1) if reverse else range(cpb)):
                entering[hh * cpb + j] = st.astype(BF16)
                st = e_end[hh * cpb + j] * st + u[hh * cpb + j]
            st_scr[hh, d] = st
        o = o + jnp.einsum('nck,nvk->ncv', qd, jnp.stack(entering), preferred_element_type=F32)
        od_scr[:, d, rows, :] = o.reshape(hp, rb, DV_H)

    def exact_chunk(hh, d, r0):
        reverse = d == 1
        rows = pl.ds(r0, c)
        b = _chunk_scan(g_refs[d][hh, 0, rows, :], reverse)
        b_scr[...] = b
        q_scr[...] = queries(hq_ref[hh, 0, rows, :])
        k_scr[...] = k_refs[d][hh, 0, rows, :].astype(F32)

        def one_row(t, carry):
            bt = b_scr[pl.ds(t, 1), :]
            qt = q_scr[pl.ds(t, 1), :]
            seen = (row >= t) if reverse else (row <= t)
            w = jnp.exp(jnp.where(seen, bt - b_scr[...], -jnp.inf))
            p = jnp.sum(qt * k_scr[...] * w, axis=-1, keepdims=True)
            od_scr[hh, d, pl.ds(r0 + t, 1), :] = jnp.sum(p * hi_ref[hh, 0, rows, :].astype(F32),
                                                         axis=0, keepdims=True)
            return carry

        lax.fori_loop(0, c, one_row, 0)
        b_end = b[0:1] if reverse else b[c - 1:c]
        st = st_scr[hh, d]
        qd = (q_scr[...] * jnp.exp(b)).astype(BF16)
        od_scr[hh, d, rows, :] = od_scr[hh, d, rows, :] + _nt_dot(qd, st.astype(BF16))
        k_end = (k_scr[...] * jnp.exp(b_end - b)).astype(BF16)
        st_scr[hh, d] = jnp.exp(b_end) * st + _tn_dot(hi_ref[hh, 0, rows, :], k_end)

    def block(i, carry):
        starts = (pl.multiple_of(i * rb, rb), pl.multiple_of((nb - 1 - i) * rb, rb))
        mild = (ok_scr[0, i] + ok_scr[1, nb - 1 - i]) == 2

        @pl.when(mild)
        def _():
            for d in range(2):
                fast_block(d, starts[d])

        @pl.when(jnp.logical_not(mild))
        def _():
            for hh in range(hp):
                for d in range(2):
                    for j in (range(cpb - 1, -1, -1) if d == 1 else range(cpb)):
                        exact_chunk(hh, d, pl.multiple_of(starts[d] + j * c, c))

        return carry

    lax.fori_loop(0, nb, block, 0)

    def finish(i, carry):
        rows = pl.ds(pl.multiple_of(i * rb, rb), rb)
        for hh in range(hp):
            o = od_scr[hh, 0, rows, :] + od_scr[hh, 1, rows, :]
            o = o * lax.rsqrt(jnp.mean(o * o, axis=-1, keepdims=True) + RMS_EPS) * ng_ref[...]
            hg = hg_ref[hh, 0, rows, :].astype(F32)
            o_ref[hh, 0, rows, :] = (o * (hg * jax.nn.sigmoid(hg))).astype(o_ref.dtype)
        return carry

    lax.fori_loop(0, nb, finish, 0)
    if want_state:
        for hh in range(hp):
            for d in range(2):
                sfin_ref[0, d, hh] = st_scr[hh, d].T


def _hgrn(plain, g, kd, norm_g, s0, want_state):
    _, bsz, t, _ = plain.shape
    hp = HGRN_SHORT_HEADS if t <= HGRN_ROWS else 1
    stream = lambda s: pl.BlockSpec((hp, 1, t, DK_H), lambda b, h: (s * N_HEADS_H // hp + h, b, 0, 0))
    in_specs = [stream(0), stream(1), stream(2), stream(0), stream(1), stream(0), stream(1),
                pl.BlockSpec((1, DV_H), lambda b, h: (0, 0))]
    args = [plain, plain, plain, g, g, kd, kd, norm_g]
    state_spec = pl.BlockSpec((1, 2, hp, DK_H, DV_H), lambda b, h: (b, 0, h, 0, 0))
    if s0 is not None:
        in_specs.append(state_spec)
        args.append(s0)
    out_specs = [pl.BlockSpec((hp, 1, t, DV_H), lambda b, h: (h, b, 0, 0))]
    out_shape = [jax.ShapeDtypeStruct((N_HEADS_H, bsz, t, DV_H), BF16)]
    if want_state:
        out_specs.append(state_spec)
        out_shape.append(jax.ShapeDtypeStruct((bsz, 2, N_HEADS_H, DK_H, DV_H), F32))
    n_blocks = t // min(HGRN_ROWS, t)
    chunk = pltpu.VMEM((HGRN_C, DK_H), F32)
    outs = pl.pallas_call(
        functools.partial(_hgrn_kernel, seq_len=t, has_s0=s0 is not None, want_state=want_state),
        grid=(bsz, N_HEADS_H // hp),
        in_specs=in_specs,
        out_specs=out_specs,
        out_shape=out_shape,
        scratch_shapes=[
            pltpu.VMEM((hp, 2, t, DV_H), F32),
            pltpu.VMEM((hp, 2, DV_H, DK_H), F32),
            chunk, chunk, chunk,
            pltpu.SMEM((2, n_blocks), jnp.int32),
        ],
        compiler_params=_params(("parallel", "parallel")),
        name="hgrn",
    )(*args)
    return outs if want_state else (outs[0], None)


def _merge_ffn2_kernel(oa0_ref, oh0_ref, oa1_ref, oh1_ref, h_ref, x_ref, mod_ref,
                       wmix_hbm, wa_hbm, wh_hbm, wo_hbm, win_hbm, wout_hbm, lng_ref, lnb_ref, y0_ref, y1_ref,
                       wg_scr, wa_scr, wh_scr, wo_scr,
                       win_scr, wout_scr, wide_stage, square_stage, wide_sems, square_sems, act_scr, *, tiles_0):
    i = pl.program_id(0)
    first_group = i < tiles_0

    @pl.when(i == 0)
    def _():
        _load_weight(win_hbm, win_scr, wide_stage, wide_sems)
        for half in range(2):
            _load_weight(wmix_hbm, wg_scr, square_stage, square_sems,
                         src_col=W_GA0 * MIX_TN + half * D_MODEL, dst_col=half * D_MODEL)
        for w_hbm, w_scr in ((wa_hbm, wa_scr), (wh_hbm, wh_scr), (wo_hbm, wo_scr), (wout_hbm, wout_scr)):
            _load_weight(w_hbm, w_scr, square_stage, square_sems)

    def heads(ref0, ref1):
        pick = lambda hh: jnp.where(first_group, ref0[hh], ref1[hh])
        return jnp.concatenate([pick(hh) for hh in range(ref0.shape[0])], axis=-1)

    gates = jax.nn.sigmoid(jnp.dot(h_ref[...], wg_scr[...], preferred_element_type=F32))
    ba = jnp.dot(heads(oa0_ref, oa1_ref), wa_scr[...], preferred_element_type=F32)
    bh = jnp.dot(heads(oh0_ref, oh1_ref), wh_scr[...], preferred_element_type=F32)
    merged = gates[:, :D_MODEL] * ba + gates[:, D_MODEL:] * bh
    mix = jnp.dot(merged.astype(BF16), wo_scr[...], preferred_element_type=F32)
    x2 = _layer_norm(ALPHA * x_ref[...] + mod_ref[0, 5:6, :] * mix, lng_ref[1:2, :], lnb_ref[1:2, :])
    y = _half_step(x2, mod_ref, win_scr, wout_scr, lng_ref, lnb_ref, act_scr, 2)

    @pl.when(first_group)
    def _():
        y0_ref[...] = y

    @pl.when(jnp.logical_not(first_group))
    def _():
        y1_ref[...] = y


def _merge_ffn2(mixed, h, x, mod, cond_of_tile, w_mix, w_a, w_h, w_o, w_in, w_out, ln_g, ln_b):
    (oa0, oh0), (oa1, oh1) = mixed
    tiles_0, tiles_1 = oa0.shape[1] // FFN_TM, oa1.shape[1] // FFN_TM
    row = lambda i: (i, 0)
    const = lambda i: (0, 0)
    in_group_0 = lambda i: jnp.minimum(i, tiles_0 - 1)
    in_group_1 = lambda i: jnp.maximum(i - tiles_0, 0)
    hbm = pl.BlockSpec(memory_space=pl.ANY)
    return pl.pallas_call(
        functools.partial(_merge_ffn2_kernel, tiles_0=tiles_0),
        grid=(tiles_0 + tiles_1,),
        in_specs=[
            pl.BlockSpec((N_HEADS_A, FFN_TM, DV_A), lambda i: (0, in_group_0(i), 0)),
            pl.BlockSpec((N_HEADS_H, FFN_TM, DV_H), lambda i: (0, in_group_0(i), 0)),
            pl.BlockSpec((N_HEADS_A, FFN_TM, DV_A), lambda i: (0, in_group_1(i), 0)),
            pl.BlockSpec((N_HEADS_H, FFN_TM, DV_H), lambda i: (0, in_group_1(i), 0)),
            pl.BlockSpec((FFN_TM, D_MODEL), row),
            pl.BlockSpec((FFN_TM, D_MODEL), row),
            pl.BlockSpec((1, 9, D_MODEL), lambda i: (cond_of_tile(i, FFN_TM), 0, 0)),
            hbm, hbm, hbm, hbm, hbm, hbm,
            pl.BlockSpec((3, D_MODEL), const),
            pl.BlockSpec((3, D_MODEL), const),
        ],
        out_specs=[pl.BlockSpec((FFN_TM, D_MODEL), lambda i: (in_group_0(i), 0)),
                   pl.BlockSpec((FFN_TM, D_MODEL), lambda i: (in_group_1(i), 0))],
        out_shape=[jax.ShapeDtypeStruct((tiles_0 * FFN_TM, D_MODEL), F32),
                   jax.ShapeDtypeStruct((tiles_1 * FFN_TM, D_MODEL), F32)],
        scratch_shapes=[pltpu.VMEM((D_MODEL, 2 * D_MODEL), BF16), pltpu.VMEM((W_A, D_MODEL), BF16),
                        pltpu.VMEM((W_H, D_MODEL), BF16), pltpu.VMEM((D_MODEL, D_MODEL), BF16)]
                       + _ffn_weight_scratch() + [pltpu.VMEM((FFN_TM, D_FF), BF16)],
        compiler_params=_params(("arbitrary",)),
        name="merge_ffn2",
    )(oa0, oh0, oa1, oh1, h, x, mod, w_mix, w_a, w_h, w_o, w_in, w_out, ln_g, ln_b)


def _rope_tables(n_tok):
    rows = n_tok // GRID_W
    row = np.repeat(np.arange(rows, dtype=np.float32), GRID_W)
    col = np.tile(np.arange(GRID_W, dtype=np.float32), rows)
    half = DH_A // 2
    inv = (ROPE_BASE ** (-np.arange(0, half, 2, dtype=np.float32) / half)).astype(np.float32)
    ar = row[:, None] * inv
    ac = col[:, None] * inv
    ang = np.concatenate([ar, ar, ac, ac] * 2, axis=-1)
    cos, sin = np.cos(ang).astype(np.float32), np.sin(ang).astype(np.float32)
    scale = np.float32(Q_SCALE)
    return (jnp.asarray(np.stack([cos * scale, cos, np.ones_like(cos)])),
            jnp.asarray(np.stack([sin * scale, sin, np.zeros_like(sin)])))


def _mixer(h1, group, wts, lam_init, bsz, ctx):
    m = MIX_TM
    t = m // bsz
    latent = ctx is not None
    per_head = lambda a: a.reshape(a.shape[0], bsz, t, a.shape[-1])
    if latent:
        qkv, plain, g, kd = _project(h1, group, wts['w_mix_in'], wts['lb_logits'], _rope_tables(t), False)
        k = v = None
        ctx_kv = (ctx[0], ctx[1])
    else:
        qkv, kv32, plain, g, kd = _project(h1, group, wts['w_mix_in'], wts['lb_logits'], None, True)
        k, v = kv32[0], kv32[1]
        ctx_kv = None
    qkv = per_head(qkv)
    q_kv = [(qkv, W_Q * N_HEADS_A), (qkv, W_K * N_HEADS_A), (qkv, W_V * N_HEADS_A)]
    o_a = _attention(*q_kv, ctx_kv, wts['lam_params'], wts['attn_subln_g'], lam_init)
    o_h, s_fin = _hgrn(per_head(plain), per_head(g), per_head(kd), wts['hgrn_norm_g'],
                       ctx[2] if latent else None, not latent)
    return (o_a.reshape(N_HEADS_A, m, DV_A), o_h.reshape(N_HEADS_H, m, DV_H)), (k, v, s_fin)


def kernel(x_prompt, x_sample, cache_k, cache_v, state_hgrn, c, c_ctx, w_ada, b_ada, ffn1_w_in, ffn1_w_out,
           w_mix_in, lambda_q1, lambda_k1, lambda_q2, lambda_k2, attn_subln_g, hgrn_lb_logits, hgrn_norm_g,
           w_branch_a, w_branch_h, w_mix_out, ffn2_w_in, ffn2_w_out, ln_g, ln_b):
    batch, seq, _ = x_prompt.shape
    dec_batch, dec_seq, _ = x_sample.shape
    past = cache_k.shape[2]
    assert DEPTH == 1 and dec_batch + 1 <= MOD_ROWS
    lam_init = 0.8 - 0.6 * math.exp(-0.3 * 0)

    cond = jnp.concatenate([c_ctx[None, :], c, jnp.zeros((MOD_ROWS - 1 - dec_batch, D_MODEL), F32)], axis=0)
    mod = _modulation(cond, w_ada[0], b_ada[0])

    wts = {
        'w_mix_in': w_mix_in[0],
        'lam_params': jnp.concatenate([lambda_q1, lambda_k1, lambda_q2, lambda_k2], axis=0),
        'attn_subln_g': attn_subln_g, 'lb_logits': hgrn_lb_logits, 'hgrn_norm_g': hgrn_norm_g,
    }
    n_ctx = batch * seq
    assert n_ctx == MIX_TM and dec_batch * dec_seq == MIX_TM

    def cond_of_tile(i, tm):
        first = i * tm
        return jnp.where(first < n_ctx, 0, 1 + (first - n_ctx) // dec_seq)

    x1, h1 = _ffn1(x_prompt.reshape(n_ctx, D_MODEL), x_sample.reshape(dec_batch * dec_seq, D_MODEL), mod,
                   cond_of_tile, ffn1_w_in[0], ffn1_w_out[0], ln_g[0], ln_b[0])
    mixed_p, (k_c, v_c, s_c) = _mixer(h1, 0, wts, lam_init, batch, None)
    ctx = (cache_k[:, 0].reshape(dec_batch, past, QK_A), cache_v[:, 0].reshape(dec_batch, past, W_A),
           state_hgrn[:, 0])
    mixed_s, _ = _mixer(h1, 1, wts, lam_init, dec_batch, ctx)
    y_p, y_s = _merge_ffn2((mixed_p, mixed_s), h1, x1, mod, cond_of_tile, w_mix_in[0],
                           w_branch_a[0], w_branch_h[0], w_mix_out[0], ffn2_w_in[0], ffn2_w_out[0],
                           ln_g[0], ln_b[0])

    return (y_p.reshape(batch, seq, D_MODEL),
            y_s.reshape(dec_batch, dec_seq, D_MODEL),
            k_c.reshape(batch, DEPTH, seq, N_HEADS_A, 2, DH_A),
            v_c.reshape(batch, DEPTH, seq, N_HEADS_A, DV_A),
            s_c.reshape(batch, DEPTH, 2, N_HEADS_H, DK_H, DV_H))
```

```python
import functools
import math

import numpy as np
import jax
import jax.numpy as jnp
from jax import lax
from jax.experimental import pallas as pl
from jax.experimental.pallas import tpu as pltpu

D_MODEL = 1024
DEPTH = 1
GRID_W = 64
N_HEADS_A = 4
DH_A = 64
DV_A = 2 * DH_A
QK_A = N_HEADS_A * 2 * DH_A
W_A = N_HEADS_A * DV_A
ROPE_BASE = 10000.0
N_HEADS_H = 4
DK_H = 128
DV_H = 128
QK_H = N_HEADS_H * DK_H
W_H = N_HEADS_H * DV_H
D_FF = 2816
MIX_IN = 2 * QK_A + W_A + 3 * QK_H + 2 * W_H + 2 * D_MODEL
ALPHA = (2 * DEPTH) ** 0.25
LN_EPS = 1e-5
RMS_EPS = 1e-6
Q_SCALE = DH_A ** -0.5 * math.log2(math.e)

F32 = jnp.float32
BF16 = jnp.bfloat16

LANES = 128
VMEM_LIMIT = 56 * 1024 * 1024

FFN_TM = 512
FFN_TF = 256
WIDE_CHUNK_ROWS = 128
SQUARE_CHUNK_ROWS = 256
MIX_TM = 4096
PROJ_TM = 2048
PROJ_SUB = 256
MIX_TN = 512
ATT_TQ = 1024
ATT_SUB = 512
ATT_SHORT_SEQ = 256
HGRN_C = 64
HGRN_ROWS = 1024
HGRN_SHORT_HEADS = 4
HGRN_SAFE_DECAY = 75.0
MOD_ROWS = 8
MOD_TN = 2304

W_Q, W_K, W_V, W_HQ, W_HFF, W_HFB, W_HI, W_HG, W_GA0, W_GA1, W_GH0, W_GH1 = range(12)


def _params(sem):
    return pltpu.CompilerParams(dimension_semantics=sem, vmem_limit_bytes=VMEM_LIMIT)


def _nt_dot(a, b):
    return lax.dot_general(a, b, (((1,), (1,)), ((), ())), preferred_element_type=F32)


def _tn_dot(a, b):
    return lax.dot_general(a, b, (((0,), (0,)), ((), ())), preferred_element_type=F32)


def _layer_norm(y, g, b):
    mu = jnp.mean(y, axis=-1, keepdims=True)
    yc = y - mu
    var = jnp.mean(yc * yc, axis=-1, keepdims=True)
    return yc * lax.rsqrt(var + LN_EPS) * g + b


def _mod_kernel(c_ref, w_ref, b_ref, o_ref):
    c = c_ref[...]
    a = (c * jax.nn.sigmoid(c)).astype(BF16)
    o_ref[...] = jnp.dot(a, w_ref[...].astype(BF16), preferred_element_type=F32) + b_ref[...]


def _modulation(cond, w_ada, b_ada):
    n_out = w_ada.shape[1]
    tn = MOD_TN
    out = pl.pallas_call(
        _mod_kernel,
        grid=(n_out // tn,),
        in_specs=[
            pl.BlockSpec((MOD_ROWS, D_MODEL), lambda n: (0, 0)),
            pl.BlockSpec((D_MODEL, tn), lambda n: (0, n)),
            pl.BlockSpec((1, tn), lambda n: (0, n)),
        ],
        out_specs=pl.BlockSpec((MOD_ROWS, tn), lambda n: (0, n)),
        out_shape=jax.ShapeDtypeStruct((MOD_ROWS, n_out), F32),
        compiler_params=_params(("arbitrary",)),
        name="modulation",
    )(cond, w_ada, b_ada.reshape(1, n_out))
    return out.reshape(MOD_ROWS, 9, D_MODEL)


def _load_weight(w_hbm, w_scr, stage, sems, src_col=0, dst_col=0):
    _, chunk, width = stage.shape
    n_chunks = w_hbm.shape[0] // chunk

    def copy(c):
        return pltpu.make_async_copy(w_hbm.at[pl.ds(c * chunk, chunk), pl.ds(src_col, width)],
                                     stage.at[c % 2], sems.at[c % 2])

    copy(0).start()
    for c in range(n_chunks):
        if c + 1 < n_chunks:
            copy(c + 1).start()
        copy(c).wait()
        w_scr[c * chunk:(c + 1) * chunk, dst_col:dst_col + width] = stage[c % 2].astype(BF16)


def _modulate(x, mod_ref, sub):
    return x * (1.0 + mod_ref[0, 3 * sub + 1:3 * sub + 2, :]) + mod_ref[0, 3 * sub:3 * sub + 1, :]


def _half_step(x, mod_ref, win_ref, wout_ref, lng_ref, lnb_ref, act_scr, sub):
    h = _modulate(x, mod_ref, sub).astype(BF16)
    for c in range(D_FF // FFN_TF):
        cols = slice(c * FFN_TF, (c + 1) * FFN_TF)
        up_cols = slice(D_FF + c * FFN_TF, D_FF + (c + 1) * FFN_TF)
        gate = jnp.dot(h, win_ref[:, cols], preferred_element_type=F32)
        up = jnp.dot(h, win_ref[:, up_cols], preferred_element_type=F32)
        act_scr[:, cols] = (gate * jax.nn.sigmoid(gate) * up).astype(BF16)
    f = jnp.dot(act_scr[...], wout_ref[...], preferred_element_type=F32)
    g = mod_ref[0, 3 * sub + 2:3 * sub + 3, :]
    return _layer_norm(ALPHA * x + 0.5 * g * f, lng_ref[sub:sub + 1, :], lnb_ref[sub:sub + 1, :])


def _ffn_weight_scratch():
    return [
        pltpu.VMEM((D_MODEL, 2 * D_FF), BF16),
        pltpu.VMEM((D_FF, D_MODEL), BF16),
        pltpu.VMEM((2, WIDE_CHUNK_ROWS, 2 * D_FF), F32),
        pltpu.VMEM((2, SQUARE_CHUNK_ROWS, D_MODEL), F32),
        pltpu.SemaphoreType.DMA((2,)),
        pltpu.SemaphoreType.DMA((2,)),
    ]


def _ffn1_kernel(xa_ref, xb_ref, mod_ref, win_hbm, wout_hbm, lng_ref, lnb_ref, xo_ref, ho_ref,
                 win_scr, wout_scr, wide_stage, square_stage, wide_sems, square_sems, act_scr, *, tiles_a):
    i = pl.program_id(0)

    @pl.when(i == 0)
    def _():
        _load_weight(win_hbm, win_scr, wide_stage, wide_sems)
        _load_weight(wout_hbm, wout_scr, square_stage, square_sems)

    x = jnp.where(i < tiles_a, xa_ref[...], xb_ref[...])
    xn = _half_step(x, mod_ref, win_scr, wout_scr, lng_ref, lnb_ref, act_scr, 0)
    xo_ref[...] = xn
    ho_ref[...] = _modulate(xn, mod_ref, 1).astype(BF16)


def _ffn1(xa, xb, mod, cond_of_tile, w_in, w_out, ln_g, ln_b):
    tiles_a, tiles_b = xa.shape[0] // FFN_TM, xb.shape[0] // FFN_TM
    m = xa.shape[0] + xb.shape[0]
    row = lambda i: (i, 0)
    const = lambda i: (0, 0)
    hbm = pl.BlockSpec(memory_space=pl.ANY)
    return pl.pallas_call(
        functools.partial(_ffn1_kernel, tiles_a=tiles_a),
        grid=(tiles_a + tiles_b,),
        in_specs=[
            pl.BlockSpec((FFN_TM, D_MODEL), lambda i: (jnp.minimum(i, tiles_a - 1), 0)),
            pl.BlockSpec((FFN_TM, D_MODEL), lambda i: (jnp.maximum(i - tiles_a, 0), 0)),
            pl.BlockSpec((1, 9, D_MODEL), lambda i: (cond_of_tile(i, FFN_TM), 0, 0)),
            hbm, hbm,
            pl.BlockSpec((3, D_MODEL), const),
            pl.BlockSpec((3, D_MODEL), const),
        ],
        out_specs=[pl.BlockSpec((FFN_TM, D_MODEL), row), pl.BlockSpec((FFN_TM, D_MODEL), row)],
        out_shape=[jax.ShapeDtypeStruct((m, D_MODEL), F32), jax.ShapeDtypeStruct((m, D_MODEL), BF16)],
        scratch_shapes=_ffn_weight_scratch() + [pltpu.VMEM((FFN_TM, D_FF), BF16)],
        compiler_params=_params(("arbitrary",)),
        name="ffn1",
    )(xa, xb, mod, w_in, w_out, ln_g, ln_b)


def _rope(x, cos, sin):
    half, quarter = DH_A // 2, DH_A // 4
    lane = lax.broadcasted_iota(jnp.int32, (1, LANES), 1)
    first_quarter = (lane % half) < quarter
    outs = []
    for hb in range(QK_A // LANES):
        xb = x[:, hb * LANES:(hb + 1) * LANES]
        ahead = pltpu.roll(xb, LANES - quarter, 1)
        behind = pltpu.roll(xb, quarter, 1)
        rot = jnp.where(first_quarter, -ahead, behind)
        outs.append(xb * cos + rot * sin)
    return jnp.concatenate(outs, axis=-1)


def _forget_gate(z, logits):
    mx = jnp.max(logits, axis=0, keepdims=True)
    el = jnp.exp(logits - mx)
    lb = el[0:1] / jnp.sum(el, axis=0, keepdims=True)
    sig = 0.5 + 0.5 * jnp.tanh(0.5 * z)
    u = (1.0 - lb) * sig
    f = lb + u
    return jnp.where(f > 0.0, jnp.log(f), z), (1.0 - lb) - u


def _proj_kernel(*refs, rope, want_kv32):
    refs = list(refs)
    h_ref, w_ref, lbl_ref = refs[:3]
    refs = refs[3:]
    cos_ref, sin_ref = (refs.pop(0), refs.pop(0)) if rope else (None, None)
    qkv_ref = refs.pop(0)
    kv32_ref = refs.pop(0) if want_kv32 else None
    plain_ref, g_ref, kd_ref = refs
    n = pl.program_id(1)

    pieces = [slice(r, r + PROJ_SUB) for r in range(0, h_ref.shape[0], PROJ_SUB)]

    def project(rows, w):
        return jnp.dot(h_ref[rows, :], w, preferred_element_type=F32)

    def store_heads(ref, rows, y):
        for hh in range(MIX_TN // LANES):
            ref[hh, rows, :] = y[:, hh * LANES:(hh + 1) * LANES].astype(ref.dtype)

    @pl.when(n <= W_V)
    def _():
        w = w_ref[...].astype(BF16)
        for rows in pieces:
            y = project(rows, w)
            if want_kv32:
                kv32_ref[0, rows, :] = y
            if rope:
                first = rows.start % cos_ref.shape[1]
                table_rows = slice(first, first + PROJ_SUB)
                y = _rope(y, cos_ref[0, table_rows, :], sin_ref[0, table_rows, :])
            else:
                y = y * jnp.where(n == W_Q, Q_SCALE, 1.0)
            store_heads(qkv_ref, rows, y)

    @pl.when((n == W_HFF) | (n == W_HFB))
    def _():
        w = w_ref[...].astype(BF16)
        for rows in pieces:
            log_f, kd = _forget_gate(project(rows, w), lbl_ref[0])
            store_heads(g_ref, rows, log_f)
            store_heads(kd_ref, rows, kd)

    @pl.when((n == W_HQ) | (n >= W_HI))
    def _():
        w = w_ref[...].astype(BF16)
        for rows in pieces:
            store_heads(plain_ref, rows, project(rows, w))


def _project(h, group, w, lb_logits, rope_tables, want_kv32):
    m = MIX_TM
    tm = PROJ_TM
    tiles = m // tm
    rope = rope_tables is not None
    heads = MIX_TN // LANES
    n_blocks = W_HG + 1
    in_specs = [pl.BlockSpec((tm, D_MODEL), lambda i, n: (group * tiles + i, 0)),
                pl.BlockSpec((D_MODEL, MIX_TN), lambda i, n: (0, n)),
                pl.BlockSpec((1, DEPTH + 1, QK_H), lambda i, n: (jnp.clip(n - W_HFF, 0, 1), 0, 0))]
    args = [h, w, lb_logits]
    qkv_col = lambda n: jnp.minimum(n, W_V)
    if rope:
        t_rope = rope_tables[0].shape[1]
        rope_rows = min(t_rope, tm)
        n_rope_tiles = t_rope // rope_rows
        for t in rope_tables:
            in_specs.append(pl.BlockSpec((1, rope_rows, DV_A), lambda i, n: (qkv_col(n), i % n_rope_tiles, 0)))
            args.append(t)
    head_block = (heads, tm, LANES)
    plain_col = lambda n: jnp.where(n < W_HI, 0, n - W_HI + 1)
    gate_col = lambda n: jnp.clip(n - W_HFF, 0, 1)
    out_specs = [pl.BlockSpec(head_block, lambda i, n: (qkv_col(n), i, 0))]
    out_shape = [jax.ShapeDtypeStruct((3 * heads, m, LANES), BF16)]
    if want_kv32:
        out_specs.append(pl.BlockSpec((1, tm, MIX_TN), lambda i, n: (jnp.clip(n - W_K, 0, 1), i, 0)))
        out_shape.append(jax.ShapeDtypeStruct((2, m, MIX_TN), F32))
    out_specs += [pl.BlockSpec(head_block, lambda i, n: (plain_col(n), i, 0)),
                  pl.BlockSpec(head_block, lambda i, n: (gate_col(n), i, 0)),
                  pl.BlockSpec(head_block, lambda i, n: (gate_col(n), i, 0))]
    out_shape += [jax.ShapeDtypeStruct((3 * heads, m, LANES), BF16),
                  jax.ShapeDtypeStruct((2 * heads, m, LANES), F32),
                  jax.ShapeDtypeStruct((2 * heads, m, LANES), BF16)]
    return pl.pallas_call(
        functools.partial(_proj_kernel, rope=rope, want_kv32=want_kv32),
        grid=(tiles, n_blocks),
        in_specs=in_specs,
        out_specs=out_specs,
        out_shape=out_shape,
        compiler_params=_params(("arbitrary", "arbitrary")),
        name="mix_proj",
    )(*args)


def _attn_kernel(*refs, lam_init, has_ctx):
    if has_ctx:
        lam_ref, g_ref, q_ref, k_ref, v_ref, kc_ref, vc_ref, o_ref = refs
    else:
        lam_ref, g_ref, q_ref, k_ref, v_ref, o_ref = refs
    lp = lam_ref[...]
    lam = (jnp.exp(jnp.sum(lp[0:1] * lp[1:2], axis=-1, keepdims=True))
           - jnp.exp(jnp.sum(lp[2:3] * lp[3:4], axis=-1, keepdims=True)) + lam_init)

    lane = lax.broadcasted_iota(jnp.int32, (1, DV_A), 1)
    comp0 = lane < DH_A
    tq = q_ref.shape[2]
    sub = min(ATT_SUB, tq)
    problems = []
    for hh in range(q_ref.shape[0]):
        keys = [k_ref[hh, 0]]
        vals = [v_ref[hh, 0]]
        if has_ctx:
            keys.append(kc_ref[0, :, hh * DV_A:(hh + 1) * DV_A].astype(BF16))
            vals.append(vc_ref[0, :, hh * DV_A:(hh + 1) * DV_A].astype(BF16))
        for r in range(tq // sub):
            q = q_ref[hh, 0, r * sub:(r + 1) * sub, :]
            zero = jnp.zeros_like(q)
            for qc in (jnp.where(comp0, q, zero), jnp.where(comp0, zero, q)):
                problems.append((qc, keys, vals))
    scores = [[_nt_dot(kk, qc) for kk in keys] for qc, keys, _ in problems]
    weights = []
    for ss in scores:
        mx = functools.reduce(jnp.maximum, [jnp.max(s, axis=0, keepdims=True) for s in ss])
        es = [jnp.exp2(s - mx) for s in ss]
        den = functools.reduce(jnp.add, [jnp.sum(e, axis=0, keepdims=True) for e in es])
        weights.append(([e.astype(BF16) for e in es], 1.0 / den))
    maps = [functools.reduce(jnp.add, [_tn_dot(vv, e) for e, vv in zip(es, vals)]) * r
            for (es, r), (_, _, vals) in zip(weights, problems)]
    idx = 0
    for hh in range(q_ref.shape[0]):
        for r in range(tq // sub):
            o = maps[idx] - lam * maps[idx + 1]
            idx += 2
            o = o * lax.rsqrt(jnp.mean(o * o, axis=0, keepdims=True) + RMS_EPS)
            o = o.T * g_ref[...]
            o_ref[hh, 0, r * sub:(r + 1) * sub, :] = (o * (1.0 - lam_init)).astype(o_ref.dtype)


def _attention(q, k, v, ctx_kv, lam_params, subln_g, lam_init):
    (q_arr, q_row), (k_arr, k_row), (v_arr, v_row) = q, k, v
    _, bsz, t, _ = q_arr.shape
    tq = min(ATT_TQ, t)
    hp = N_HEADS_A if t <= ATT_SHORT_SEQ else 1
    const = lambda b, h, i: (0, 0)
    rows = lambda first: (lambda b, h, i: (first // hp + h, b, 0, 0))
    in_specs = [
        pl.BlockSpec((4, DH_A), const),
        pl.BlockSpec((1, DV_A), const),
        pl.BlockSpec((hp, 1, tq, DV_A), lambda b, h, i: (q_row // hp + h, b, i, 0)),
        pl.BlockSpec((hp, 1, t, DV_A), rows(k_row)),
        pl.BlockSpec((hp, 1, t, DV_A), rows(v_row)),
    ]
    args = [lam_params, subln_g, q_arr, k_arr, v_arr]
    if ctx_kv is not None:
        p = ctx_kv[0].shape[1]
        head_all = lambda b, h, i: (b, 0, h)
        in_specs += [pl.BlockSpec((1, p, hp * DV_A), head_all), pl.BlockSpec((1, p, hp * DV_A), head_all)]
        args += list(ctx_kv)
    return pl.pallas_call(
        functools.partial(_attn_kernel, lam_init=lam_init, has_ctx=ctx_kv is not None),
        grid=(bsz, N_HEADS_A // hp, t // tq),
        in_specs=in_specs,
        out_specs=pl.BlockSpec((hp, 1, tq, DV_A), lambda b, h, i: (h, b, i, 0)),
        out_shape=jax.ShapeDtypeStruct((N_HEADS_A, bsz, t, DV_A), BF16),
        compiler_params=_params(("parallel", "parallel", "arbitrary")),
        name="attention",
    )(*args)


def _chunk_scan(g, reverse):
    r = g.shape[0]
    pos = lax.broadcasted_iota(jnp.int32, (r, 1), 0) % HGRN_C
    b = g
    sh = 1
    while sh < HGRN_C:
        if reverse:
            b = b + jnp.where(pos < HGRN_C - sh, pltpu.roll(b, r - sh, 0), 0.0)
        else:
            b = b + jnp.where(pos >= sh, pltpu.roll(b, sh, 0), 0.0)
        sh *= 2
    return b


def _hgrn_kernel(*refs, seq_len, has_s0, want_state):
    refs = list(refs)
    hq_ref, hi_ref, hg_ref, gf_ref, gb_ref, kf_ref, kb_ref, ng_ref = refs[:8]
    refs = refs[8:]
    s0_ref = refs.pop(0) if has_s0 else None
    o_ref = refs.pop(0)
    sfin_ref = refs.pop(0) if want_state else None
    od_scr, st_scr, b_scr, q_scr, k_scr, ok_scr = refs
    hp = hq_ref.shape[0]
    c = HGRN_C
    rb = min(HGRN_ROWS, seq_len)
    nb = seq_len // rb
    cpb = rb // c
    g_refs = (gf_ref, gb_ref)
    k_refs = (kf_ref, kb_ref)

    def queries(hq):
        hq = hq.astype(F32)
        return hq * jax.nn.sigmoid(hq)

    def flags(i, carry):
        rows = pl.ds(pl.multiple_of(i * rb, rb), rb)
        for d in range(2):
            worst = jnp.zeros((1, DK_H), F32)
            for hh in range(hp):
                g = g_refs[d][hh, 0, rows, :]
                for j in range(cpb):
                    worst = jnp.maximum(worst, -jnp.sum(g[j * c:(j + 1) * c], axis=0, keepdims=True))
            ok_scr[d, i] = (jnp.max(worst) <= HGRN_SAFE_DECAY).astype(jnp.int32)
        return carry

    lax.fori_loop(0, nb, flags, 0)

    for hh in range(hp):
        for d in range(2):
            if has_s0:
                st_scr[hh, d] = s0_ref[0, d, hh].T
            else:
                st_scr[hh, d] = jnp.zeros((DV_H, DK_H), F32)

    row = lax.broadcasted_iota(jnp.int32, (c, 1), 0)
    col = lax.broadcasted_iota(jnp.int32, (1, c), 1)

    def fast_block(d, r0):
        reverse = d == 1
        rows = pl.ds(r0, rb)
        n = hp * cpb
        flat = lambda a: a.reshape(hp * rb, a.shape[-1])
        chunks = lambda a: a.reshape(n, c, a.shape[-1])
        vb = chunks(hi_ref[:, 0, rows, :])
        b = _chunk_scan(flat(g_refs[d][:, 0, rows, :]), reverse)
        qd = chunks((queries(flat(hq_ref[:, 0, rows, :])) * jnp.exp(b)).astype(BF16))
        ku = chunks(flat(k_refs[d][:, 0, rows, :]).astype(F32) * jnp.exp(-b))
        b = chunks(b)
        e_end = jnp.exp(b[:, 0:1, :] if reverse else b[:, c - 1:c, :])
        causal = ((row <= col) if reverse else (row >= col))[None]
        a = jnp.einsum('nck,nsk->ncs', qd, ku.astype(BF16), preferred_element_type=F32)
        o = jnp.einsum('ncs,nsv->ncv', jnp.where(causal, a, 0.0).astype(BF16), vb, preferred_element_type=F32)
        u = jnp.einsum('nsv,nsk->nvk', vb, (ku * e_end).astype(BF16), preferred_element_type=F32)
        entering = [None] * n
        for hh in range(hp):
            st = st_scr[hh, d]
            for j in (range(cpb - 1, -1, -1) if reverse else range(cpb)):
                entering[hh * cpb + j] = st.astype(BF16)
                st = e_end[hh * cpb + j] * st + u[hh * cpb + j]
            st_scr[hh, d] = st
        o = o + jnp.einsum('nck,nvk->ncv', qd, jnp.stack(entering), preferred_element_type=F32)
        od_scr[:, d, rows, :] = o.reshape(hp, rb, DV_H)

    def exact_chunk(hh, d, r0):
        reverse = d == 1
        rows = pl.ds(r0, c)
        b = _chunk_scan(g_refs[d][hh, 0, rows, :], reverse)
        b_scr[...] = b
        q_scr[...] = queries(hq_ref[hh, 0, rows, :])
        k_scr[...] = k_refs[d][hh, 0, rows, :].astype(F32)

        def one_row(t, carry):
            bt = b_scr[pl.ds(t, 1), :]
            qt = q_scr[pl.ds(t, 1), :]
            seen = (row >= t) if reverse else (row <= t)
            w = jnp.exp(jnp.where(seen, bt - b_scr[...], -jnp.inf))
            p = jnp.sum(qt * k_scr[...] * w, axis=-1, keepdims=True)
            od_scr[hh, d, pl.ds(r0 + t, 1), :] = jnp.sum(p * hi_ref[hh, 0, rows, :].astype(F32),
                                                         axis=0, keepdims=True)
            return carry

        lax.fori_loop(0, c, one_row, 0)
        b_end = b[0:1] if reverse else b[c - 1:c]
        st = st_scr[hh, d]
        qd = (q_scr[...] * jnp.exp(b)).astype(BF16)
        od_scr[hh, d, rows, :] = od_scr[hh, d, rows, :] + _nt_dot(qd, st.astype(BF16))
        k_end = (k_scr[...] * jnp.exp(b_end - b)).astype(BF16)
        st_scr[hh, d] = jnp.exp(b_end) * st + _tn_dot(hi_ref[hh, 0, rows, :], k_end)

    def block(i, carry):
        starts = (pl.multiple_of(i * rb, rb), pl.multiple_of((nb - 1 - i) * rb, rb))
        mild = (ok_scr[0, i] + ok_scr[1, nb - 1 - i]) == 2

        @pl.when(mild)
        def _():
            for d in range(2):
                fast_block(d, starts[d])

        @pl.when(jnp.logical_not(mild))
        def _():
            for hh in range(hp):
                for d in range(2):
                    for j in (range(cpb - 1, -1, -1) if d == 1 else range(cpb)):
                        exact_chunk(hh, d, pl.multiple_of(starts[d] + j * c, c))

        return carry

    lax.fori_loop(0, nb, block, 0)

    def finish(i, carry):
        rows = pl.ds(pl.multiple_of(i * rb, rb), rb)
        for hh in range(hp):
            o = od_scr[hh, 0, rows, :] + od_scr[hh, 1, rows, :]
            o = o * lax.rsqrt(jnp.mean(o * o, axis=-1, keepdims=True) + RMS_EPS) * ng_ref[...]
            hg = hg_ref[hh, 0, rows, :].astype(F32)
            o_ref[hh, 0, rows, :] = (o * (hg * jax.nn.sigmoid(hg))).astype(o_ref.dtype)
        return carry

    lax.fori_loop(0, nb, finish, 0)
    if want_state:
        for hh in range(hp):
            for d in range(2):
                sfin_ref[0, d, hh] = st_scr[hh, d].T


def _hgrn(plain, g, kd, norm_g, s0, want_state):
    _, bsz, t, _ = plain.shape
    hp = HGRN_SHORT_HEADS if t <= HGRN_ROWS else 1
    stream = lambda s: pl.BlockSpec((hp, 1, t, DK_H), lambda b, h: (s * N_HEADS_H // hp + h, b, 0, 0))
    in_specs = [stream(0), stream(1), stream(2), stream(0), stream(1), stream(0), stream(1),
                pl.BlockSpec((1, DV_H), lambda b, h: (0, 0))]
    args = [plain, plain, plain, g, g, kd, kd, norm_g]
    state_spec = pl.BlockSpec((1, 2, hp, DK_H, DV_H), lambda b, h: (b, 0, h, 0, 0))
    if s0 is not None:
        in_specs.append(state_spec)
        args.append(s0)
    out_specs = [pl.BlockSpec((hp, 1, t, DV_H), lambda b, h: (h, b, 0, 0))]
    out_shape = [jax.ShapeDtypeStruct((N_HEADS_H, bsz, t, DV_H), BF16)]
    if want_state:
        out_specs.append(state_spec)
        out_shape.append(jax.ShapeDtypeStruct((bsz, 2, N_HEADS_H, DK_H, DV_H), F32))
    n_blocks = t // min(HGRN_ROWS, t)
    chunk = pltpu.VMEM((HGRN_C, DK_H), F32)
    outs = pl.pallas_call(
        functools.partial(_hgrn_kernel, seq_len=t, has_s0=s0 is not None, want_state=want_state),
        grid=(bsz, N_HEADS_H // hp),
        in_specs=in_specs,
        out_specs=out_specs,
        out_shape=out_shape,
        scratch_shapes=[
            pltpu.VMEM((hp, 2, t, DV_H), F32),
            pltpu.VMEM((hp, 2, DV_H, DK_H), F32),
            chunk, chunk, chunk,
            pltpu.SMEM((2, n_blocks), jnp.int32),
        ],
        compiler_params=_params(("parallel", "parallel")),
        name="hgrn",
    )(*args)
    return outs if want_state else (outs[0], None)


def _merge_ffn2_kernel(oa0_ref, oh0_ref, oa1_ref, oh1_ref, h_ref, x_ref, mod_ref,
                       wmix_hbm, wa_hbm, wh_hbm, wo_hbm, win_hbm, wout_hbm, lng_ref, lnb_ref, y0_ref, y1_ref,
                       wg_scr, wa_scr, wh_scr, wo_scr,
                       win_scr, wout_scr, wide_stage, square_stage, wide_sems, square_sems, act_scr, *, tiles_0):
    i = pl.program_id(0)
    first_group = i < tiles_0

    @pl.when(i == 0)
    def _():
        _load_weight(win_hbm, win_scr, wide_stage, wide_sems)
        for half in range(2):
            _load_weight(wmix_hbm, wg_scr, square_stage, square_sems,
                         src_col=W_GA0 * MIX_TN + half * D_MODEL, dst_col=half * D_MODEL)
        for w_hbm, w_scr in ((wa_hbm, wa_scr), (wh_hbm, wh_scr), (wo_hbm, wo_scr), (wout_hbm, wout_scr)):
            _load_weight(w_hbm, w_scr, square_stage, square_sems)

    def heads(ref0, ref1):
        pick = lambda hh: jnp.where(first_group, ref0[hh], ref1[hh])
        return jnp.concatenate([pick(hh) for hh in range(ref0.shape[0])], axis=-1)

    gates = jax.nn.sigmoid(jnp.dot(h_ref[...], wg_scr[...], preferred_element_type=F32))
    ba = jnp.dot(heads(oa0_ref, oa1_ref), wa_scr[...], preferred_element_type=F32)
    bh = jnp.dot(heads(oh0_ref, oh1_ref), wh_scr[...], preferred_element_type=F32)
    merged = gates[:, :D_MODEL] * ba + gates[:, D_MODEL:] * bh
    mix = jnp.dot(merged.astype(BF16), wo_scr[...], preferred_element_type=F32)
    x2 = _layer_norm(ALPHA * x_ref[...] + mod_ref[0, 5:6, :] * mix, lng_ref[1:2, :], lnb_ref[1:2, :])
    y = _half_step(x2, mod_ref, win_scr, wout_scr, lng_ref, lnb_ref, act_scr, 2)

    @pl.when(first_group)
    def _():
        y0_ref[...] = y

    @pl.when(jnp.logical_not(first_group))
    def _():
        y1_ref[...] = y


def _merge_ffn2(mixed, h, x, mod, cond_of_tile, w_mix, w_a, w_h, w_o, w_in, w_out, ln_g, ln_b):
    (oa0, oh0), (oa1, oh1) = mixed
    tiles_0, tiles_1 = oa0.shape[1] // FFN_TM, oa1.shape[1] // FFN_TM
    row = lambda i: (i, 0)
    const = lambda i: (0, 0)
    in_group_0 = lambda i: jnp.minimum(i, tiles_0 - 1)
    in_group_1 = lambda i: jnp.maximum(i - tiles_0, 0)
    hbm = pl.BlockSpec(memory_space=pl.ANY)
    return pl.pallas_call(
        functools.partial(_merge_ffn2_kernel, tiles_0=tiles_0),
        grid=(tiles_0 + tiles_1,),
        in_specs=[
            pl.BlockSpec((N_HEADS_A, FFN_TM, DV_A), lambda i: (0, in_group_0(i), 0)),
            pl.BlockSpec((N_HEADS_H, FFN_TM, DV_H), lambda i: (0, in_group_0(i), 0)),
            pl.BlockSpec((N_HEADS_A, FFN_TM, DV_A), lambda i: (0, in_group_1(i), 0)),
            pl.BlockSpec((N_HEADS_H, FFN_TM, DV_H), lambda i: (0, in_group_1(i), 0)),
            pl.BlockSpec((FFN_TM, D_MODEL), row),
            pl.BlockSpec((FFN_TM, D_MODEL), row),
            pl.BlockSpec((1, 9, D_MODEL), lambda i: (cond_of_tile(i, FFN_TM), 0, 0)),
            hbm, hbm, hbm, hbm, hbm, hbm,
            pl.BlockSpec((3, D_MODEL), const),
            pl.BlockSpec((3, D_MODEL), const),
        ],
        out_specs=[pl.BlockSpec((FFN_TM, D_MODEL), lambda i: (in_group_0(i), 0)),
                   pl.BlockSpec((FFN_TM, D_MODEL), lambda i: (in_group_1(i), 0))],
        out_shape=[jax.ShapeDtypeStruct((tiles_0 * FFN_TM, D_MODEL), F32),
                   jax.ShapeDtypeStruct((tiles_1 * FFN_TM, D_MODEL), F32)],
        scratch_shapes=[pltpu.VMEM((D_MODEL, 2 * D_MODEL), BF16), pltpu.VMEM((W_A, D_MODEL), BF16),
                        pltpu.VMEM((W_H, D_MODEL), BF16), pltpu.VMEM((D_MODEL, D_MODEL), BF16)]
                       + _ffn_weight_scratch() + [pltpu.VMEM((FFN_TM, D_FF), BF16)],
        compiler_params=_params(("arbitrary",)),
        name="merge_ffn2",
    )(oa0, oh0, oa1, oh1, h, x, mod, w_mix, w_a, w_h, w_o, w_in, w_out, ln_g, ln_b)


def _rope_tables(n_tok):
    rows = n_tok // GRID_W
    row = np.repeat(np.arange(rows, dtype=np.float32), GRID_W)
    col = np.tile(np.arange(GRID_W, dtype=np.float32), rows)
    half = DH_A // 2
    inv = (ROPE_BASE ** (-np.arange(0, half, 2, dtype=np.float32) / half)).astype(np.float32)
    ar = row[:, None] * inv
    ac = col[:, None] * inv
    ang = np.concatenate([ar, ar, ac, ac] * 2, axis=-1)
    cos, sin = np.cos(ang).astype(np.float32), np.sin(ang).astype(np.float32)
    scale = np.float32(Q_SCALE)
    return (jnp.asarray(np.stack([cos * scale, cos, np.ones_like(cos)])),
            jnp.asarray(np.stack([sin * scale, sin, np.zeros_like(sin)])))


def _mixer(h1, group, wts, lam_init, bsz, ctx):
    m = MIX_TM
    t = m // bsz
    latent = ctx is not None
    per_head = lambda a: a.reshape(a.shape[0], bsz, t, a.shape[-1])
    if latent:
        qkv, plain, g, kd = _project(h1, group, wts['w_mix_in'], wts['lb_logits'], _rope_tables(t), False)
        k = v = None
        ctx_kv = (ctx[0], ctx[1])
    else:
        qkv, kv32, plain, g, kd = _project(h1, group, wts['w_mix_in'], wts['lb_logits'], None, True)
        k, v = kv32[0], kv32[1]
        ctx_kv = None
    qkv = per_head(qkv)
    q_kv = [(qkv, W_Q * N_HEADS_A), (qkv, W_K * N_HEADS_A), (qkv, W_V * N_HEADS_A)]
    o_a = _attention(*q_kv, ctx_kv, wts['lam_params'], wts['attn_subln_g'], lam_init)
    o_h, s_fin = _hgrn(per_head(plain), per_head(g), per_head(kd), wts['hgrn_norm_g'],
                       ctx[2] if latent else None, not latent)
    return (o_a.reshape(N_HEADS_A, m, DV_A), o_h.reshape(N_HEADS_H, m, DV_H)), (k, v, s_fin)


def kernel(x_prompt, x_sample, cache_k, cache_v, state_hgrn, c, c_ctx, w_ada, b_ada, ffn1_w_in, ffn1_w_out,
           w_mix_in, lambda_q1, lambda_k1, lambda_q2, lambda_k2, attn_subln_g, hgrn_lb_logits, hgrn_norm_g,
           w_branch_a, w_branch_h, w_mix_out, ffn2_w_in, ffn2_w_out, ln_g, ln_b):
    batch, seq, _ = x_prompt.shape
    dec_batch, dec_seq, _ = x_sample.shape
    past = cache_k.shape[2]
    assert DEPTH == 1 and dec_batch + 1 <= MOD_ROWS
    lam_init = 0.8 - 0.6 * math.exp(-0.3 * 0)

    cond = jnp.concatenate([c_ctx[None, :], c, jnp.zeros((MOD_ROWS - 1 - dec_batch, D_MODEL), F32)], axis=0)
    mod = _modulation(cond, w_ada[0], b_ada[0])

    wts = {
        'w_mix_in': w_mix_in[0],
        'lam_params': jnp.concatenate([lambda_q1, lambda_k1, lambda_q2, lambda_k2], axis=0),
        'attn_subln_g': attn_subln_g, 'lb_logits': hgrn_lb_logits, 'hgrn_norm_g': hgrn_norm_g,
    }
    n_ctx = batch * seq
    assert n_ctx == MIX_TM and dec_batch * dec_seq == MIX_TM

    def cond_of_tile(i, tm):
        first = i * tm
        return jnp.where(first < n_ctx, 0, 1 + (first - n_ctx) // dec_seq)

    x1, h1 = _ffn1(x_prompt.reshape(n_ctx, D_MODEL), x_sample.reshape(dec_batch * dec_seq, D_MODEL), mod,
                   cond_of_tile, ffn1_w_in[0], ffn1_w_out[0], ln_g[0], ln_b[0])
    mixed_p, (k_c, v_c, s_c) = _mixer(h1, 0, wts, lam_init, batch, None)
    ctx = (cache_k[:, 0].reshape(dec_batch, past, QK_A), cache_v[:, 0].reshape(dec_batch, past, W_A),
           state_hgrn[:, 0])
    mixed_s, _ = _mixer(h1, 1, wts, lam_init, dec_batch, ctx)
    y_p, y_s = _merge_ffn2((mixed_p, mixed_s), h1, x1, mod, cond_of_tile, w_mix_in[0],
                           w_branch_a[0], w_branch_h[0], w_mix_out[0], ffn2_w_in[0], ffn2_w_out[0],
                           ln_g[0], ln_b[0])

    return (y_p.reshape(batch, seq, D_MODEL),
            y_s.reshape(dec_batch, dec_seq, D_MODEL),
            k_c.reshape(batch, DEPTH, seq, N_HEADS_A, 2, DH_A),
            v_c.reshape(batch, DEPTH, seq, N_HEADS_A, DV_A),
            s_c.reshape(batch, DEPTH, 2, N_HEADS_H, DK_H, DV_H))
```

```python
import functools
import math

import numpy as np
import jax
import jax.numpy as jnp
from jax import lax
from jax.experimental import pallas as pl
from jax.experimental.pallas import tpu as pltpu

D_MODEL = 1024
DEPTH = 1
GRID_W = 64
N_HEADS_A = 4
DH_A = 64
DV_A = 2 * DH_A
QK_A = N_HEADS_A * 2 * DH_A
W_A = N_HEADS_A * DV_A
ROPE_BASE = 10000.0
N_HEADS_H = 4
DK_H = 128
DV_H = 128
QK_H = N_HEADS_H * DK_H
W_H = N_HEADS_H * DV_H
D_FF = 2816
MIX_IN = 2 * QK_A + W_A + 3 * QK_H + 2 * W_H + 2 * D_MODEL
ALPHA = (2 * DEPTH) ** 0.25
LN_EPS = 1e-5
RMS_EPS = 1e-6
Q_SCALE = DH_A ** -0.5 * math.log2(math.e)

F32 = jnp.float32
BF16 = jnp.bfloat16

LANES = 128
VMEM_LIMIT = 56 * 1024 * 1024

FFN_TM = 512
FFN_TF = 256
WIDE_CHUNK_ROWS = 128
SQUARE_CHUNK_ROWS = 256
MIX_TM = 4096
PROJ_TM = 2048
PROJ_SUB = 256
MIX_TN = 512
ATT_TQ = 1024
ATT_SUB = 512
ATT_SHORT_SEQ = 256
HGRN_C = 64
HGRN_ROWS = 1024
HGRN_SHORT_HEADS = 4
HGRN_SAFE_DECAY = 75.0
MOD_ROWS = 8
MOD_TN = 2304

W_Q, W_K, W_V, W_HQ, W_HFF, W_HFB, W_HI, W_HG, W_GA0, W_GA1, W_GH0, W_GH1 = range(12)


def _params(sem):
    return pltpu.CompilerParams(dimension_semantics=sem, vmem_limit_bytes=VMEM_LIMIT)


def _nt_dot(a, b):
    return lax.dot_general(a, b, (((1,), (1,)), ((), ())), preferred_element_type=F32)


def _tn_dot(a, b):
    return lax.dot_general(a, b, (((0,), (0,)), ((), ())), preferred_element_type=F32)


def _layer_norm(y, g, b):
    mu = jnp.mean(y, axis=-1, keepdims=True)
    yc = y - mu
    var = jnp.mean(yc * yc, axis=-1, keepdims=True)
    return yc * lax.rsqrt(var + LN_EPS) * g + b


def _mod_kernel(c_ref, w_ref, b_ref, o_ref):
    c = c_ref[...]
    a = (c * jax.nn.sigmoid(c)).astype(BF16)
    o_ref[...] = jnp.dot(a, w_ref[...].astype(BF16), preferred_element_type=F32) + b_ref[...]


def _modulation(cond, w_ada, b_ada):
    n_out = w_ada.shape[1]
    tn = MOD_TN
    out = pl.pallas_call(
        _mod_kernel,
        grid=(n_out // tn,),
        in_specs=[
            pl.BlockSpec((MOD_ROWS, D_MODEL), lambda n: (0, 0)),
            pl.BlockSpec((D_MODEL, tn), lambda n: (0, n)),
            pl.BlockSpec((1, tn), lambda n: (0, n)),
        ],
        out_specs=pl.BlockSpec((MOD_ROWS, tn), lambda n: (0, n)),
        out_shape=jax.ShapeDtypeStruct((MOD_ROWS, n_out), F32),
        compiler_params=_params(("arbitrary",)),
        name="modulation",
    )(cond, w_ada, b_ada.reshape(1, n_out))
    return out.reshape(MOD_ROWS, 9, D_MODEL)


def _load_weight(w_hbm, w_scr, stage, sems, src_col=0, dst_col=0):
    _, chunk, width = stage.shape
    n_chunks = w_hbm.shape[0] // chunk

    def copy(c):
        return pltpu.make_async_copy(w_hbm.at[pl.ds(c * chunk, chunk), pl.ds(src_col, width)],
                                     stage.at[c % 2], sems.at[c % 2])

    copy(0).start()
    for c in range(n_chunks):
        if c + 1 < n_chunks:
            copy(c + 1).start()
        copy(c).wait()
        w_scr[c * chunk:(c + 1) * chunk, dst_col:dst_col + width] = stage[c % 2].astype(BF16)


def _modulate(x, mod_ref, sub):
    return x * (1.0 + mod_ref[0, 3 * sub + 1:3 * sub + 2, :]) + mod_ref[0, 3 * sub:3 * sub + 1, :]


def _half_step(x, mod_ref, win_ref, wout_ref, lng_ref, lnb_ref, act_scr, sub):
    h = _modulate(x, mod_ref, sub).astype(BF16)
    for c in range(D_FF // FFN_TF):
        cols = slice(c * FFN_TF, (c + 1) * FFN_TF)
        up_cols = slice(D_FF + c * FFN_TF, D_FF + (c + 1) * FFN_TF)
        gate = jnp.dot(h, win_ref[:, cols], preferred_element_type=F32)
        up = jnp.dot(h, win_ref[:, up_cols], preferred_element_type=F32)
        act_scr[:, cols] = (gate * jax.nn.sigmoid(gate) * up).astype(BF16)
    f = jnp.dot(act_scr[...], wout_ref[...], preferred_element_type=F32)
    g = mod_ref[0, 3 * sub + 2:3 * sub + 3, :]
    return _layer_norm(ALPHA * x + 0.5 * g * f, lng_ref[sub:sub + 1, :], lnb_ref[sub:sub + 1, :])


def _ffn_weight_scratch():
    return [
        pltpu.VMEM((D_MODEL, 2 * D_FF), BF16),
        pltpu.VMEM((D_FF, D_MODEL), BF16),
        pltpu.VMEM((2, WIDE_CHUNK_ROWS, 2 * D_FF), F32),
        pltpu.VMEM((2, SQUARE_CHUNK_ROWS, D_MODEL), F32),
        pltpu.SemaphoreType.DMA((2,)),
        pltpu.SemaphoreType.DMA((2,)),
    ]


def _ffn1_kernel(xa_ref, xb_ref, mod_ref, win_hbm, wout_hbm, lng_ref, lnb_ref, xo_ref, ho_ref,
                 win_scr, wout_scr, wide_stage, square_stage, wide_sems, square_sems, act_scr, *, tiles_a):
    i = pl.program_id(0)

    @pl.when(i == 0)
    def _():
        _load_weight(win_hbm, win_scr, wide_stage, wide_sems)
        _load_weight(wout_hbm, wout_scr, square_stage, square_sems)

    x = jnp.where(i < tiles_a, xa_ref[...], xb_ref[...])
    xn = _half_step(x, mod_ref, win_scr, wout_scr, lng_ref, lnb_ref, act_scr, 0)
    xo_ref[...] = xn
    ho_ref[...] = _modulate(xn, mod_ref, 1).astype(BF16)


def _ffn1(xa, xb, mod, cond_of_tile, w_in, w_out, ln_g, ln_b):
    tiles_a, tiles_b = xa.shape[0] // FFN_TM, xb.shape[0] // FFN_TM
    m = xa.shape[0] + xb.shape[0]
    row = lambda i: (i, 0)
    const = lambda i: (0, 0)
    hbm = pl.BlockSpec(memory_space=pl.ANY)
    return pl.pallas_call(
        functools.partial(_ffn1_kernel, tiles_a=tiles_a),
        grid=(tiles_a + tiles_b,),
        in_specs=[
            pl.BlockSpec((FFN_TM, D_MODEL), lambda i: (jnp.minimum(i, tiles_a - 1), 0)),
            pl.BlockSpec((FFN_TM, D_MODEL), lambda i: (jnp.maximum(i - tiles_a, 0), 0)),
            pl.BlockSpec((1, 9, D_MODEL), lambda i: (cond_of_tile(i, FFN_TM), 0, 0)),
            hbm, hbm,
            pl.BlockSpec((3, D_MODEL), const),
            pl.BlockSpec((3, D_MODEL), const),
        ],
        out_specs=[pl.BlockSpec((FFN_TM, D_MODEL), row), pl.BlockSpec((FFN_TM, D_MODEL), row)],
        out_shape=[jax.ShapeDtypeStruct((m, D_MODEL), F32), jax.ShapeDtypeStruct((m, D_MODEL), BF16)],
        scratch_shapes=_ffn_weight_scratch() + [pltpu.VMEM((FFN_TM, D_FF), BF16)],
        compiler_params=_params(("arbitrary",)),
        name="ffn1",
    )(xa, xb, mod, w_in, w_out, ln_g, ln_b)


def _rope(x, cos, sin):
    half, quarter = DH_A // 2, DH_A // 4
    lane = lax.broadcasted_iota(jnp.int32, (1, LANES), 1)
    first_quarter = (lane % half) < quarter
    outs = []
    for hb in range(QK_A // LANES):
        xb = x[:, hb * LANES:(hb + 1) * LANES]
        ahead = pltpu.roll(xb, LANES - quarter, 1)
        behind = pltpu.roll(xb, quarter, 1)
        rot = jnp.where(first_quarter, -ahead, behind)
        outs.append(xb * cos + rot * sin)
    return jnp.concatenate(outs, axis=-1)


def _forget_gate(z, logits):
    mx = jnp.max(logits, axis=0, keepdims=True)
    el = jnp.exp(logits - mx)
    lb = el[0:1] / jnp.sum(el, axis=0, keepdims=True)
    sig = 0.5 + 0.5 * jnp.tanh(0.5 * z)
    u = (1.0 - lb) * sig
    f = lb + u
    return jnp.where(f > 0.0, jnp.log(f), z), (1.0 - lb) - u


def _proj_kernel(*refs, rope, want_kv32):
    refs = list(refs)
    h_ref, w_ref, lbl_ref = refs[:3]
    refs = refs[3:]
    cos_ref, sin_ref = (refs.pop(0), refs.pop(0)) if rope else (None, None)
    qkv_ref = refs.pop(0)
    kt_ref, v32_ref = (refs.pop(0), refs.pop(0)) if want_kv32 else (None, None)
    plain_ref, g_ref, kd_ref = refs
    n = pl.program_id(1)

    pieces = [slice(r, r + PROJ_SUB) for r in range(0, h_ref.shape[0], PROJ_SUB)]

    def project(rows, w):
        return jnp.dot(h_ref[rows, :], w, preferred_element_type=F32)

    def store_heads(ref, rows, y):
        for hh in range(MIX_TN // LANES):
            ref[hh, rows, :] = y[:, hh * LANES:(hh + 1) * LANES].astype(ref.dtype)

    def qkv_step(block):
        w = w_ref[...].astype(BF16)
        for p, rows in enumerate(pieces):
            y = project(rows, w)
            if block == W_K:
                for hh in range(MIX_TN // LANES):
                    kt_ref[p, hh] = y[:, hh * LANES:(hh + 1) * LANES].T
            if block == W_V:
                v32_ref[rows, :] = y
            if rope:
                first = rows.start % cos_ref.shape[1]
                table_rows = slice(first, first + PROJ_SUB)
                y = _rope(y, cos_ref[0, table_rows, :], sin_ref[0, table_rows, :])
            elif block is None:
                y = y * jnp.where(n == W_Q, Q_SCALE, 1.0)
            elif block == W_Q:
                y = y * Q_SCALE
            store_heads(qkv_ref, rows, y)

    if want_kv32:
        for block in (W_Q, W_K, W_V):
            pl.when(n == block)(functools.partial(qkv_step, block))
    else:
        pl.when(n <= W_V)(functools.partial(qkv_step, None))

    @pl.when((n == W_HFF) | (n == W_HFB))
    def _():
        w = w_ref[...].astype(BF16)
        for rows in pieces:
            log_f, kd = _forget_gate(project(rows, w), lbl_ref[0])
            store_heads(g_ref, rows, log_f)
            store_heads(kd_ref, rows, kd)

    @pl.when((n == W_HQ) | (n >= W_HI))
    def _():
        w = w_ref[...].astype(BF16)
        for rows in pieces:
            store_heads(plain_ref, rows, project(rows, w))


def _project(h, group, w, lb_logits, rope_tables, want_kv32):
    m = MIX_TM
    tm = PROJ_TM
    tiles = m // tm
    rope = rope_tables is not None
    heads = MIX_TN // LANES
    n_blocks = W_HG + 1
    in_specs = [pl.BlockSpec((tm, D_MODEL), lambda i, n: (group * tiles + i, 0)),
                pl.BlockSpec((D_MODEL, MIX_TN), lambda i, n: (0, n)),
                pl.BlockSpec((1, DEPTH + 1, QK_H), lambda i, n: (jnp.clip(n - W_HFF, 0, 1), 0, 0))]
    args = [h, w, lb_logits]
    qkv_col = lambda n: jnp.minimum(n, W_V)
    if rope:
        t_rope = rope_tables[0].shape[1]
        rope_rows = min(t_rope, tm)
        n_rope_tiles = t_rope // rope_rows
        for t in rope_tables:
            in_specs.append(pl.BlockSpec((1, rope_rows, DV_A), lambda i, n: (qkv_col(n), i % n_rope_tiles, 0)))
            args.append(t)
    head_block = (heads, tm, LANES)
    plain_col = lambda n: jnp.where(n < W_HI, 0, n - W_HI + 1)
    gate_col = lambda n: jnp.clip(n - W_HFF, 0, 1)
    out_specs = [pl.BlockSpec(head_block, lambda i, n: (qkv_col(n), i, 0))]
    out_shape = [jax.ShapeDtypeStruct((3 * heads, m, LANES), BF16)]
    if want_kv32:
        requests = m // PROJ_SUB
        out_specs += [pl.BlockSpec((tm // PROJ_SUB, heads, LANES, PROJ_SUB), lambda i, n: (i, 0, 0, 0)),
                      pl.BlockSpec((tm, MIX_TN), lambda i, n: (i, 0))]
        out_shape += [jax.ShapeDtypeStruct((requests, heads, LANES, PROJ_SUB), F32),
                      jax.ShapeDtypeStruct((m, MIX_TN), F32)]
    out_specs += [pl.BlockSpec(head_block, lambda i, n: (plain_col(n), i, 0)),
                  pl.BlockSpec(head_block, lambda i, n: (gate_col(n), i, 0)),
                  pl.BlockSpec(head_block, lambda i, n: (gate_col(n), i, 0))]
    out_shape += [jax.ShapeDtypeStruct((3 * heads, m, LANES), BF16),
                  jax.ShapeDtypeStruct((2 * heads, m, LANES), F32),
                  jax.ShapeDtypeStruct((2 * heads, m, LANES), BF16)]
    return pl.pallas_call(
        functools.partial(_proj_kernel, rope=rope, want_kv32=want_kv32),
        grid=(tiles, n_blocks),
        in_specs=in_specs,
        out_specs=out_specs,
        out_shape=out_shape,
        compiler_params=_params(("arbitrary", "arbitrary")),
        name="mix_proj",
    )(*args)


def _attn_kernel(*refs, lam_init, has_ctx):
    if has_ctx:
        lam_ref, g_ref, q_ref, k_ref, v_ref, kc_ref, vc_ref, o_ref = refs
    else:
        lam_ref, g_ref, q_ref, k_ref, v_ref, o_ref = refs
    lp = lam_ref[...]
    lam = (jnp.exp(jnp.sum(lp[0:1] * lp[1:2], axis=-1, keepdims=True))
           - jnp.exp(jnp.sum(lp[2:3] * lp[3:4], axis=-1, keepdims=True)) + lam_init)

    lane = lax.broadcasted_iota(jnp.int32, (1, DV_A), 1)
    comp0 = lane < DH_A
    tq = q_ref.shape[2]
    sub = min(ATT_SUB, tq)
    problems = []
    for hh in range(q_ref.shape[0]):
        keys = [k_ref[hh, 0]]
        vals = [v_ref[hh, 0]]
        if has_ctx:
            keys.append(kc_ref[0, :, hh * DV_A:(hh + 1) * DV_A].astype(BF16))
            vals.append(vc_ref[0, :, hh * DV_A:(hh + 1) * DV_A].astype(BF16))
        for r in range(tq // sub):
            q = q_ref[hh, 0, r * sub:(r + 1) * sub, :]
            zero = jnp.zeros_like(q)
            for qc in (jnp.where(comp0, q, zero), jnp.where(comp0, zero, q)):
                problems.append((qc, keys, vals))
    scores = [[_nt_dot(kk, qc) for kk in keys] for qc, keys, _ in problems]
    weights = []
    for ss in scores:
        mx = functools.reduce(jnp.maximum, [jnp.max(s, axis=0, keepdims=True) for s in ss])
        es = [jnp.exp2(s - mx) for s in ss]
        den = functools.reduce(jnp.add, [jnp.sum(e, axis=0, keepdims=True) for e in es])
        weights.append(([e.astype(BF16) for e in es], 1.0 / den))
    maps = [functools.reduce(jnp.add, [_tn_dot(vv, e) for e, vv in zip(es, vals)]) * r
            for (es, r), (_, _, vals) in zip(weights, problems)]
    idx = 0
    for hh in range(q_ref.shape[0]):
        for r in range(tq // sub):
            o = maps[idx] - lam * maps[idx + 1]
            idx += 2
            o = o * lax.rsqrt(jnp.mean(o * o, axis=0, keepdims=True) + RMS_EPS)
            o = o.T * g_ref[...]
            o_ref[hh, 0, r * sub:(r + 1) * sub, :] = (o * (1.0 - lam_init)).astype(o_ref.dtype)


def _attention(q, k, v, ctx_kv, lam_params, subln_g, lam_init):
    (q_arr, q_row), (k_arr, k_row), (v_arr, v_row) = q, k, v
    _, bsz, t, _ = q_arr.shape
    tq = min(ATT_TQ, t)
    hp = N_HEADS_A if t <= ATT_SHORT_SEQ else 1
    const = lambda b, h, i: (0, 0)
    rows = lambda first: (lambda b, h, i: (first // hp + h, b, 0, 0))
    in_specs = [
        pl.BlockSpec((4, DH_A), const),
        pl.BlockSpec((1, DV_A), const),
        pl.BlockSpec((hp, 1, tq, DV_A), lambda b, h, i: (q_row // hp + h, b, i, 0)),
        pl.BlockSpec((hp, 1, t, DV_A), rows(k_row)),
        pl.BlockSpec((hp, 1, t, DV_A), rows(v_row)),
    ]
    args = [lam_params, subln_g, q_arr, k_arr, v_arr]
    if ctx_kv is not None:
        p = ctx_kv[0].shape[1]
        head_all = lambda b, h, i: (b, 0, h)
        in_specs += [pl.BlockSpec((1, p, hp * DV_A), head_all), pl.BlockSpec((1, p, hp * DV_A), head_all)]
        args += list(ctx_kv)
    return pl.pallas_call(
        functools.partial(_attn_kernel, lam_init=lam_init, has_ctx=ctx_kv is not None),
        grid=(bsz, N_HEADS_A // hp, t // tq),
        in_specs=in_specs,
        out_specs=pl.BlockSpec((hp, 1, tq, DV_A), lambda b, h, i: (h, b, i, 0)),
        out_shape=jax.ShapeDtypeStruct((N_HEADS_A, bsz, t, DV_A), BF16),
        compiler_params=_params(("parallel", "parallel", "arbitrary")),
        name="attention",
    )(*args)


def _chunk_scan(g, reverse):
    r = g.shape[0]
    pos = lax.broadcasted_iota(jnp.int32, (r, 1), 0) % HGRN_C
    b = g
    sh = 1
    while sh < HGRN_C:
        if reverse:
            b = b + jnp.where(pos < HGRN_C - sh, pltpu.roll(b, r - sh, 0), 0.0)
        else:
            b = b + jnp.where(pos >= sh, pltpu.roll(b, sh, 0), 0.0)
        sh *= 2
    return b


def _hgrn_kernel(*refs, seq_len, has_s0, want_state):
    refs = list(refs)
    hq_ref, hi_ref, hg_ref, gf_ref, gb_ref, kf_ref, kb_ref, ng_ref = refs[:8]
    refs = refs[8:]
    s0_ref = refs.pop(0) if has_s0 else None
    o_ref = refs.pop(0)
    sfin_ref = refs.pop(0) if want_state else None
    od_scr, st_scr, b_scr, q_scr, k_scr, ok_scr = refs
    hp = hq_ref.shape[0]
    c = HGRN_C
    rb = min(HGRN_ROWS, seq_len)
    nb = seq_len // rb
    cpb = rb // c
    g_refs = (gf_ref, gb_ref)
    k_refs = (kf_ref, kb_ref)

    def queries(hq):
        hq = hq.astype(F32)
        return hq * jax.nn.sigmoid(hq)

    def flags(i, carry):
        rows = pl.ds(pl.multiple_of(i * rb, rb), rb)
        for d in range(2):
            worst = jnp.zeros((1, DK_H), F32)
            for hh in range(hp):
                g = g_refs[d][hh, 0, rows, :]
                for j in range(cpb):
                    worst = jnp.maximum(worst, -jnp.sum(g[j * c:(j + 1) * c], axis=0, keepdims=True))
            ok_scr[d, i] = (jnp.max(worst) <= HGRN_SAFE_DECAY).astype(jnp.int32)
        return carry

    lax.fori_loop(0, nb, flags, 0)

    for hh in range(hp):
        for d in range(2):
            if has_s0:
                st_scr[hh, d] = s0_ref[0, d, hh].T
            else:
                st_scr[hh, d] = jnp.zeros((DV_H, DK_H), F32)

    row = lax.broadcasted_iota(jnp.int32, (c, 1), 0)
    col = lax.broadcasted_iota(jnp.int32, (1, c), 1)

    def fast_block(d, r0):
        reverse = d == 1
        rows = pl.ds(r0, rb)
        n = hp * cpb
        flat = lambda a: a.reshape(hp * rb, a.shape[-1])
        chunks = lambda a: a.reshape(n, c, a.shape[-1])
        vb = chunks(hi_ref[:, 0, rows, :])
        b = _chunk_scan(flat(g_refs[d][:, 0, rows, :]), reverse)
        qd = chunks((queries(flat(hq_ref[:, 0, rows, :])) * jnp.exp(b)).astype(BF16))
        ku = chunks(flat(k_refs[d][:, 0, rows, :]).astype(F32) * jnp.exp(-b))
        b = chunks(b)
        e_end = jnp.exp(b[:, 0:1, :] if reverse else b[:, c - 1:c, :])
        causal = ((row <= col) if reverse else (row >= col))[None]
        a = jnp.einsum('nck,nsk->ncs', qd, ku.astype(BF16), preferred_element_type=F32)
        o = jnp.einsum('ncs,nsv->ncv', jnp.where(causal, a, 0.0).astype(BF16), vb, preferred_element_type=F32)
        u = jnp.einsum('nsv,nsk->nvk', vb, (ku * e_end).astype(BF16), preferred_element_type=F32)
        entering = [None] * n
        for hh in range(hp):
            st = st_scr[hh, d]
            for j in (range(cpb - 1, -1, -1) if reverse else range(cpb)):
                entering[hh * cpb + j] = st.astype(BF16)
                st = e_end[hh * cpb + j] * st + u[hh * cpb + j]
            st_scr[hh, d] = st
        o = o + jnp.einsum('nck,nvk->ncv', qd, jnp.stack(entering), preferred_element_type=F32)
        od_scr[:, d, rows, :] = o.reshape(hp, rb, DV_H)

    def exact_chunk(hh, d, r0):
        reverse = d == 1
        rows = pl.ds(r0, c)
        b = _chunk_scan(g_refs[d][hh, 0, rows, :], reverse)
        b_scr[...] = b
        q_scr[...] = queries(hq_ref[hh, 0, rows, :])
        k_scr[...] = k_refs[d][hh, 0, rows, :].astype(F32)

        def one_row(t, carry):
            bt = b_scr[pl.ds(t, 1), :]
            qt = q_scr[pl.ds(t, 1), :]
            seen = (row >= t) if reverse else (row <= t)
            w = jnp.exp(jnp.where(seen, bt - b_scr[...], -jnp.inf))
            p = jnp.sum(qt * k_scr[...] * w, axis=-1, keepdims=True)
            od_scr[hh, d, pl.ds(r0 + t, 1), :] = jnp.sum(p * hi_ref[hh, 0, rows, :].astype(F32),
                                                         axis=0, keepdims=True)
            return carry

        lax.fori_loop(0, c, one_row, 0)
        b_end = b[0:1] if reverse else b[c - 1:c]
        st = st_scr[hh, d]
        qd = (q_scr[...] * jnp.exp(b)).astype(BF16)
        od_scr[hh, d, rows, :] = od_scr[hh, d, rows, :] + _nt_dot(qd, st.astype(BF16))
        k_end = (k_scr[...] * jnp.exp(b_end - b)).astype(BF16)
        st_scr[hh, d] = jnp.exp(b_end) * st + _tn_dot(hi_ref[hh, 0, rows, :], k_end)

    def block(i, carry):
        starts = (pl.multiple_of(i * rb, rb), pl.multiple_of((nb - 1 - i) * rb, rb))
        mild = (ok_scr[0, i] + ok_scr[1, nb - 1 - i]) == 2

        @pl.when(mild)
        def _():
            for d in range(2):
                fast_block(d, starts[d])

        @pl.when(jnp.logical_not(mild))
        def _():
            for hh in range(hp):
                for d in range(2):
                    for j in (range(cpb - 1, -1, -1) if d == 1 else range(cpb)):
                        exact_chunk(hh, d, pl.multiple_of(starts[d] + j * c, c))

        return carry

    lax.fori_loop(0, nb, block, 0)

    def finish(i, carry):
        rows = pl.ds(pl.multiple_of(i * rb, rb), rb)
        for hh in range(hp):
            o = od_scr[hh, 0, rows, :] + od_scr[hh, 1, rows, :]
            o = o * lax.rsqrt(jnp.mean(o * o, axis=-1, keepdims=True) + RMS_EPS) * ng_ref[...]
            hg = hg_ref[hh, 0, rows, :].astype(F32)
            o_ref[hh, 0, rows, :] = (o * (hg * jax.nn.sigmoid(hg))).astype(o_ref.dtype)
        return carry

    lax.fori_loop(0, nb, finish, 0)
    if want_state:
        for hh in range(hp):
            for d in range(2):
                sfin_ref[0, d, hh] = st_scr[hh, d].T


def _hgrn(plain, g, kd, norm_g, s0, want_state):
    _, bsz, t, _ = plain.shape
    hp = HGRN_SHORT_HEADS if t <= HGRN_ROWS else 1
    stream = lambda s: pl.BlockSpec((hp, 1, t, DK_H), lambda b, h: (s * N_HEADS_H // hp + h, b, 0, 0))
    in_specs = [stream(0), stream(1), stream(2), stream(0), stream(1), stream(0), stream(1),
                pl.BlockSpec((1, DV_H), lambda b, h: (0, 0))]
    args = [plain, plain, plain, g, g, kd, kd, norm_g]
    state_spec = pl.BlockSpec((1, 2, hp, DK_H, DV_H), lambda b, h: (b, 0, h, 0, 0))
    if s0 is not None:
        in_specs.append(state_spec)
        args.append(s0)
    out_specs = [pl.BlockSpec((hp, 1, t, DV_H), lambda b, h: (h, b, 0, 0))]
    out_shape = [jax.ShapeDtypeStruct((N_HEADS_H, bsz, t, DV_H), BF16)]
    if want_state:
        out_specs.append(state_spec)
        out_shape.append(jax.ShapeDtypeStruct((bsz, 2, N_HEADS_H, DK_H, DV_H), F32))
    n_blocks = t // min(HGRN_ROWS, t)
    chunk = pltpu.VMEM((HGRN_C, DK_H), F32)
    outs = pl.pallas_call(
        functools.partial(_hgrn_kernel, seq_len=t, has_s0=s0 is not None, want_state=want_state),
        grid=(bsz, N_HEADS_H // hp),
        in_specs=in_specs,
        out_specs=out_specs,
        out_shape=out_shape,
        scratch_shapes=[
            pltpu.VMEM((hp, 2, t, DV_H), F32),
            pltpu.VMEM((hp, 2, DV_H, DK_H), F32),
            chunk, chunk, chunk,
            pltpu.SMEM((2, n_blocks), jnp.int32),
        ],
        compiler_params=_params(("parallel", "parallel")),
        name="hgrn",
    )(*args)
    return outs if want_state else (outs[0], None)


def _merge_ffn2_kernel(oa0_ref, oh0_ref, oa1_ref, oh1_ref, h_ref, x_ref, mod_ref,
                       wmix_hbm, wa_hbm, wh_hbm, wo_hbm, win_hbm, wout_hbm, lng_ref, lnb_ref, y0_ref, y1_ref,
                       wg_scr, wa_scr, wh_scr, wo_scr,
                       win_scr, wout_scr, wide_stage, square_stage, wide_sems, square_sems, act_scr, *, tiles_0):
    i = pl.program_id(0)
    first_group = i < tiles_0

    @pl.when(i == 0)
    def _():
        _load_weight(win_hbm, win_scr, wide_stage, wide_sems)
        for half in range(2):
            _load_weight(wmix_hbm, wg_scr, square_stage, square_sems,
                         src_col=W_GA0 * MIX_TN + half * D_MODEL, dst_col=half * D_MODEL)
        for w_hbm, w_scr in ((wa_hbm, wa_scr), (wh_hbm, wh_scr), (wo_hbm, wo_scr), (wout_hbm, wout_scr)):
            _load_weight(w_hbm, w_scr, square_stage, square_sems)

    def heads(ref0, ref1):
        pick = lambda hh: jnp.where(first_group, ref0[hh], ref1[hh])
        return jnp.concatenate([pick(hh) for hh in range(ref0.shape[0])], axis=-1)

    gates = jax.nn.sigmoid(jnp.dot(h_ref[...], wg_scr[...], preferred_element_type=F32))
    ba = jnp.dot(heads(oa0_ref, oa1_ref), wa_scr[...], preferred_element_type=F32)
    bh = jnp.dot(heads(oh0_ref, oh1_ref), wh_scr[...], preferred_element_type=F32)
    merged = gates[:, :D_MODEL] * ba + gates[:, D_MODEL:] * bh
    mix = jnp.dot(merged.astype(BF16), wo_scr[...], preferred_element_type=F32)
    x2 = _layer_norm(ALPHA * x_ref[...] + mod_ref[0, 5:6, :] * mix, lng_ref[1:2, :], lnb_ref[1:2, :])
    y = _half_step(x2, mod_ref, win_scr, wout_scr, lng_ref, lnb_ref, act_scr, 2)

    @pl.when(first_group)
    def _():
        y0_ref[...] = y

    @pl.when(jnp.logical_not(first_group))
    def _():
        y1_ref[...] = y


def _merge_ffn2(mixed, h, x, mod, cond_of_tile, w_mix, w_a, w_h, w_o, w_in, w_out, ln_g, ln_b):
    (oa0, oh0), (oa1, oh1) = mixed
    tiles_0, tiles_1 = oa0.shape[1] // FFN_TM, oa1.shape[1] // FFN_TM
    row = lambda i: (i, 0)
    const = lambda i: (0, 0)
    in_group_0 = lambda i: jnp.minimum(i, tiles_0 - 1)
    in_group_1 = lambda i: jnp.maximum(i - tiles_0, 0)
    hbm = pl.BlockSpec(memory_space=pl.ANY)
    return pl.pallas_call(
        functools.partial(_merge_ffn2_kernel, tiles_0=tiles_0),
        grid=(tiles_0 + tiles_1,),
        in_specs=[
            pl.BlockSpec((N_HEADS_A, FFN_TM, DV_A), lambda i: (0, in_group_0(i), 0)),
            pl.BlockSpec((N_HEADS_H, FFN_TM, DV_H), lambda i: (0, in_group_0(i), 0)),
            pl.BlockSpec((N_HEADS_A, FFN_TM, DV_A), lambda i: (0, in_group_1(i), 0)),
            pl.BlockSpec((N_HEADS_H, FFN_TM, DV_H), lambda i: (0, in_group_1(i), 0)),
            pl.BlockSpec((FFN_TM, D_MODEL), row),
            pl.BlockSpec((FFN_TM, D_MODEL), row),
            pl.BlockSpec((1, 9, D_MODEL), lambda i: (cond_of_tile(i, FFN_TM), 0, 0)),
            hbm, hbm, hbm, hbm, hbm, hbm,
            pl.BlockSpec((3, D_MODEL), const),
            pl.BlockSpec((3, D_MODEL), const),
        ],
        out_specs=[pl.BlockSpec((FFN_TM, D_MODEL), lambda i: (in_group_0(i), 0)),
                   pl.BlockSpec((FFN_TM, D_MODEL), lambda i: (in_group_1(i), 0))],
        out_shape=[jax.ShapeDtypeStruct((tiles_0 * FFN_TM, D_MODEL), F32),
                   jax.ShapeDtypeStruct((tiles_1 * FFN_TM, D_MODEL), F32)],
        scratch_shapes=[pltpu.VMEM((D_MODEL, 2 * D_MODEL), BF16), pltpu.VMEM((W_A, D_MODEL), BF16),
                        pltpu.VMEM((W_H, D_MODEL), BF16), pltpu.VMEM((D_MODEL, D_MODEL), BF16)]
                       + _ffn_weight_scratch() + [pltpu.VMEM((FFN_TM, D_FF), BF16)],
        compiler_params=_params(("arbitrary",)),
        name="merge_ffn2",
    )(oa0, oh0, oa1, oh1, h, x, mod, w_mix, w_a, w_h, w_o, w_in, w_out, ln_g, ln_b)


def _rope_tables(n_tok):
    rows = n_tok // GRID_W
    row = np.repeat(np.arange(rows, dtype=np.float32), GRID_W)
    col = np.tile(np.arange(GRID_W, dtype=np.float32), rows)
    half = DH_A // 2
    inv = (ROPE_BASE ** (-np.arange(0, half, 2, dtype=np.float32) / half)).astype(np.float32)
    ar = row[:, None] * inv
    ac = col[:, None] * inv
    ang = np.concatenate([ar, ar, ac, ac] * 2, axis=-1)
    cos, sin = np.cos(ang).astype(np.float32), np.sin(ang).astype(np.float32)
    scale = np.float32(Q_SCALE)
    return (jnp.asarray(np.stack([cos * scale, cos, np.ones_like(cos)])),
            jnp.asarray(np.stack([sin * scale, sin, np.zeros_like(sin)])))


def _mixer(h1, group, wts, lam_init, bsz, ctx):
    m = MIX_TM
    t = m // bsz
    latent = ctx is not None
    per_head = lambda a: a.reshape(a.shape[0], bsz, t, a.shape[-1])
    if latent:
        qkv, plain, g, kd = _project(h1, group, wts['w_mix_in'], wts['lb_logits'], _rope_tables(t), False)
        k = v = None
        ctx_kv = (ctx[0], ctx[1])
    else:
        assert t == PROJ_SUB
        qkv, k, v, plain, g, kd = _project(h1, group, wts['w_mix_in'], wts['lb_logits'], None, True)
        ctx_kv = None
    qkv = per_head(qkv)
    q_kv = [(qkv, W_Q * N_HEADS_A), (qkv, W_K * N_HEADS_A), (qkv, W_V * N_HEADS_A)]
    o_a = _attention(*q_kv, ctx_kv, wts['lam_params'], wts['attn_subln_g'], lam_init)
    o_h, s_fin = _hgrn(per_head(plain), per_head(g), per_head(kd), wts['hgrn_norm_g'],
                       ctx[2] if latent else None, not latent)
    return (o_a.reshape(N_HEADS_A, m, DV_A), o_h.reshape(N_HEADS_H, m, DV_H)), (k, v, s_fin)


def kernel(x_prompt, x_sample, cache_k, cache_v, state_hgrn, c, c_ctx, w_ada, b_ada, ffn1_w_in, ffn1_w_out,
           w_mix_in, lambda_q1, lambda_k1, lambda_q2, lambda_k2, attn_subln_g, hgrn_lb_logits, hgrn_norm_g,
           w_branch_a, w_branch_h, w_mix_out, ffn2_w_in, ffn2_w_out, ln_g, ln_b):
    batch, seq, _ = x_prompt.shape
    dec_batch, dec_seq, _ = x_sample.shape
    past = cache_k.shape[2]
    assert DEPTH == 1 and dec_batch + 1 <= MOD_ROWS
    lam_init = 0.8 - 0.6 * math.exp(-0.3 * 0)

    cond = jnp.concatenate([c_ctx[None, :], c, jnp.zeros((MOD_ROWS - 1 - dec_batch, D_MODEL), F32)], axis=0)
    mod = _modulation(cond, w_ada[0], b_ada[0])

    wts = {
        'w_mix_in': w_mix_in[0],
        'lam_params': jnp.concatenate([lambda_q1, lambda_k1, lambda_q2, lambda_k2], axis=0),
        'attn_subln_g': attn_subln_g, 'lb_logits': hgrn_lb_logits, 'hgrn_norm_g': hgrn_norm_g,
    }
    n_ctx = batch * seq
    assert n_ctx == MIX_TM and dec_batch * dec_seq == MIX_TM

    def cond_of_tile(i, tm):
        first = i * tm
        return jnp.where(first < n_ctx, 0, 1 + (first - n_ctx) // dec_seq)

    x1, h1 = _ffn1(x_prompt.reshape(n_ctx, D_MODEL), x_sample.reshape(dec_batch * dec_seq, D_MODEL), mod,
                   cond_of_tile, ffn1_w_in[0], ffn1_w_out[0], ln_g[0], ln_b[0])
    mixed_p, (k_c, v_c, s_c) = _mixer(h1, 0, wts, lam_init, batch, None)
    ctx = (cache_k[:, 0].reshape(dec_batch, past, QK_A), cache_v[:, 0].reshape(dec_batch, past, W_A),
           state_hgrn[:, 0])
    mixed_s, _ = _mixer(h1, 1, wts, lam_init, dec_batch, ctx)
    y_p, y_s = _merge_ffn2((mixed_p, mixed_s), h1, x1, mod, cond_of_tile, w_mix_in[0],
                           w_branch_a[0], w_branch_h[0], w_mix_out[0], ffn2_w_in[0], ffn2_w_out[0],
                           ln_g[0], ln_b[0])

    return (y_p.reshape(batch, seq, D_MODEL),
            y_s.reshape(dec_batch, dec_seq, D_MODEL),
            k_c.reshape(batch, DEPTH, N_HEADS_A, 2, DH_A, seq).transpose(0, 1, 5, 2, 3, 4),
            v_c.reshape(batch, DEPTH, seq, N_HEADS_A, DV_A),
            s_c.reshape(batch, DEPTH, 2, N_HEADS_H, DK_H, DV_H))
```

```python
import functools
import math

import numpy as np
import jax
import jax.numpy as jnp
from jax import lax
from jax.experimental import pallas as pl
from jax.experimental.pallas import tpu as pltpu

D_MODEL = 1024
DEPTH = 1
GRID_W = 64
N_HEADS_A = 4
DH_A = 64
DV_A = 2 * DH_A
QK_A = N_HEADS_A * 2 * DH_A
W_A = N_HEADS_A * DV_A
ROPE_BASE = 10000.0
N_HEADS_H = 4
DK_H = 128
DV_H = 128
QK_H = N_HEADS_H * DK_H
W_H = N_HEADS_H * DV_H
D_FF = 2816
MIX_IN = 2 * QK_A + W_A + 3 * QK_H + 2 * W_H + 2 * D_MODEL
ALPHA = (2 * DEPTH) ** 0.25
LN_EPS = 1e-5
RMS_EPS = 1e-6
Q_SCALE = DH_A ** -0.5 * math.log2(math.e)

F32 = jnp.float32
BF16 = jnp.bfloat16

LANES = 128
VMEM_LIMIT = 56 * 1024 * 1024

FFN_TM = 512
FFN_TF = 256
WIDE_CHUNK_ROWS = 128
SQUARE_CHUNK_ROWS = 256
MIX_TM = 4096
PROJ_TM = 2048
PROJ_SUB = 256
MIX_TN = 512
ATT_TQ = 1024
ATT_SUB = 512
ATT_SHORT_SEQ = 256
HGRN_C = 64
HGRN_ROWS = 1024
HGRN_SHORT_HEADS = 4
HGRN_SAFE_DECAY = 75.0
MOD_ROWS = 8
MOD_TN = 2304

W_Q, W_K, W_V, W_HQ, W_HFF, W_HFB, W_HI, W_HG, W_GA0, W_GA1, W_GH0, W_GH1 = range(12)


def _params(sem):
    return pltpu.CompilerParams(dimension_semantics=sem, vmem_limit_bytes=VMEM_LIMIT)


def _nt_dot(a, b):
    return lax.dot_general(a, b, (((1,), (1,)), ((), ())), preferred_element_type=F32)


def _tn_dot(a, b):
    return lax.dot_general(a, b, (((0,), (0,)), ((), ())), preferred_element_type=F32)


def _layer_norm(y, g, b):
    mu = jnp.mean(y, axis=-1, keepdims=True)
    yc = y - mu
    var = jnp.mean(yc * yc, axis=-1, keepdims=True)
    return yc * lax.rsqrt(var + LN_EPS) * g + b


def _mod_kernel(c_ref, w_ref, b_ref, o_ref):
    c = c_ref[...]
    a = (c * jax.nn.sigmoid(c)).astype(BF16)
    o_ref[...] = jnp.dot(a, w_ref[...].astype(BF16), preferred_element_type=F32) + b_ref[...]


def _modulation(cond, w_ada, b_ada):
    n_out = w_ada.shape[1]
    tn = MOD_TN
    out = pl.pallas_call(
        _mod_kernel,
        grid=(n_out // tn,),
        in_specs=[
            pl.BlockSpec((MOD_ROWS, D_MODEL), lambda n: (0, 0)),
            pl.BlockSpec((D_MODEL, tn), lambda n: (0, n)),
            pl.BlockSpec((1, tn), lambda n: (0, n)),
        ],
        out_specs=pl.BlockSpec((MOD_ROWS, tn), lambda n: (0, n)),
        out_shape=jax.ShapeDtypeStruct((MOD_ROWS, n_out), F32),
        compiler_params=_params(("arbitrary",)),
        name="modulation",
    )(cond, w_ada, b_ada.reshape(1, n_out))
    return out.reshape(MOD_ROWS, 9, D_MODEL)


def _load_weight(w_hbm, w_scr, stage, sems, src_col=0, dst_col=0):
    _, chunk, width = stage.shape
    n_chunks = w_hbm.shape[0] // chunk

    def copy(c):
        return pltpu.make_async_copy(w_hbm.at[pl.ds(c * chunk, chunk), pl.ds(src_col, width)],
                                     stage.at[c % 2], sems.at[c % 2])

    copy(0).start()
    for c in range(n_chunks):
        if c + 1 < n_chunks:
            copy(c + 1).start()
        copy(c).wait()
        w_scr[c * chunk:(c + 1) * chunk, dst_col:dst_col + width] = stage[c % 2].astype(BF16)


def _modulate(x, mod_ref, sub):
    return x * (1.0 + mod_ref[0, 3 * sub + 1:3 * sub + 2, :]) + mod_ref[0, 3 * sub:3 * sub + 1, :]


def _half_step(x, mod_ref, win_ref, wout_ref, lng_ref, lnb_ref, act_scr, sub):
    h = _modulate(x, mod_ref, sub).astype(BF16)
    for c in range(D_FF // FFN_TF):
        cols = slice(c * FFN_TF, (c + 1) * FFN_TF)
        up_cols = slice(D_FF + c * FFN_TF, D_FF + (c + 1) * FFN_TF)
        gate = jnp.dot(h, win_ref[:, cols], preferred_element_type=F32)
        up = jnp.dot(h, win_ref[:, up_cols], preferred_element_type=F32)
        act_scr[:, cols] = (gate * jax.nn.sigmoid(gate) * up).astype(BF16)
    f = jnp.dot(act_scr[...], wout_ref[...], preferred_element_type=F32)
    g = mod_ref[0, 3 * sub + 2:3 * sub + 3, :]
    return _layer_norm(ALPHA * x + 0.5 * g * f, lng_ref[sub:sub + 1, :], lnb_ref[sub:sub + 1, :])


def _ffn_weight_scratch():
    return [
        pltpu.VMEM((D_MODEL, 2 * D_FF), BF16),
        pltpu.VMEM((D_FF, D_MODEL), BF16),
        pltpu.VMEM((2, WIDE_CHUNK_ROWS, 2 * D_FF), F32),
        pltpu.VMEM((2, SQUARE_CHUNK_ROWS, D_MODEL), F32),
        pltpu.SemaphoreType.DMA((2,)),
        pltpu.SemaphoreType.DMA((2,)),
    ]


def _ffn1_kernel(xa_ref, xb_ref, mod_ref, win_hbm, wout_hbm, lng_ref, lnb_ref, xo_ref, ho_ref,
                 win_scr, wout_scr, wide_stage, square_stage, wide_sems, square_sems, act_scr, *, tiles_a):
    i = pl.program_id(0)

    @pl.when(i == 0)
    def _():
        _load_weight(win_hbm, win_scr, wide_stage, wide_sems)
        _load_weight(wout_hbm, wout_scr, square_stage, square_sems)

    x = jnp.where(i < tiles_a, xa_ref[...], xb_ref[...])
    xn = _half_step(x, mod_ref, win_scr, wout_scr, lng_ref, lnb_ref, act_scr, 0)
    xo_ref[...] = xn
    ho_ref[...] = _modulate(xn, mod_ref, 1).astype(BF16)


def _ffn1(xa, xb, mod, cond_of_tile, w_in, w_out, ln_g, ln_b):
    tiles_a, tiles_b = xa.shape[0] // FFN_TM, xb.shape[0] // FFN_TM
    m = xa.shape[0] + xb.shape[0]
    row = lambda i: (i, 0)
    const = lambda i: (0, 0)
    hbm = pl.BlockSpec(memory_space=pl.ANY)
    return pl.pallas_call(
        functools.partial(_ffn1_kernel, tiles_a=tiles_a),
        grid=(tiles_a + tiles_b,),
        in_specs=[
            pl.BlockSpec((FFN_TM, D_MODEL), lambda i: (jnp.minimum(i, tiles_a - 1), 0)),
            pl.BlockSpec((FFN_TM, D_MODEL), lambda i: (jnp.maximum(i - tiles_a, 0), 0)),
            pl.BlockSpec((1, 9, D_MODEL), lambda i: (cond_of_tile(i, FFN_TM), 0, 0)),
            hbm, hbm,
            pl.BlockSpec((3, D_MODEL), const),
            pl.BlockSpec((3, D_MODEL), const),
        ],
        out_specs=[pl.BlockSpec((FFN_TM, D_MODEL), row), pl.BlockSpec((FFN_TM, D_MODEL), row)],
        out_shape=[jax.ShapeDtypeStruct((m, D_MODEL), F32), jax.ShapeDtypeStruct((m, D_MODEL), BF16)],
        scratch_shapes=_ffn_weight_scratch() + [pltpu.VMEM((FFN_TM, D_FF), BF16)],
        compiler_params=_params(("arbitrary",)),
        name="ffn1",
    )(xa, xb, mod, w_in, w_out, ln_g, ln_b)


def _rope(x, cos, sin):
    half, quarter = DH_A // 2, DH_A // 4
    lane = lax.broadcasted_iota(jnp.int32, (1, LANES), 1)
    first_quarter = (lane % half) < quarter
    outs = []
    for hb in range(QK_A // LANES):
        xb = x[:, hb * LANES:(hb + 1) * LANES]
        ahead = pltpu.roll(xb, LANES - quarter, 1)
        behind = pltpu.roll(xb, quarter, 1)
        rot = jnp.where(first_quarter, -ahead, behind)
        outs.append(xb * cos + rot * sin)
    return jnp.concatenate(outs, axis=-1)


def _forget_gate(z, logits):
    mx = jnp.max(logits, axis=0, keepdims=True)
    el = jnp.exp(logits - mx)
    lb = el[0:1] / jnp.sum(el, axis=0, keepdims=True)
    sig = 0.5 + 0.5 * jnp.tanh(0.5 * z)
    u = (1.0 - lb) * sig
    f = lb + u
    return jnp.where(f > 0.0, jnp.log(f), z), (1.0 - lb) - u


def _proj_kernel(*refs, rope, want_kv32):
    refs = list(refs)
    h_ref, w_ref, lbl_ref = refs[:3]
    refs = refs[3:]
    cos_ref, sin_ref = (refs.pop(0), refs.pop(0)) if rope else (None, None)
    qkv_ref = refs.pop(0)
    kt_ref, v32_ref = (refs.pop(0), refs.pop(0)) if want_kv32 else (None, None)
    plain_ref, g_ref, kd_ref = refs
    n = pl.program_id(1)

    pieces = [slice(r, r + PROJ_SUB) for r in range(0, h_ref.shape[0], PROJ_SUB)]

    def project(rows, w):
        return jnp.dot(h_ref[rows, :], w, preferred_element_type=F32)

    def store_heads(ref, rows, y):
        for hh in range(MIX_TN // LANES):
            ref[hh, rows, :] = y[:, hh * LANES:(hh + 1) * LANES].astype(ref.dtype)

    def qkv_step(block):
        w = w_ref[...].astype(BF16)
        for p, rows in enumerate(pieces):
            y = project(rows, w)
            if block == W_K:
                for hh in range(MIX_TN // LANES):
                    kt_ref[p, hh] = y[:, hh * LANES:(hh + 1) * LANES].T
            if block == W_V:
                v32_ref[rows] = y.reshape(PROJ_SUB, MIX_TN // LANES, LANES)
            if rope:
                first = rows.start % cos_ref.shape[1]
                table_rows = slice(first, first + PROJ_SUB)
                y = _rope(y, cos_ref[0, table_rows, :], sin_ref[0, table_rows, :])
            elif block is None:
                y = y * jnp.where(n == W_Q, Q_SCALE, 1.0)
            elif block == W_Q:
                y = y * Q_SCALE
            store_heads(qkv_ref, rows, y)

    if want_kv32:
        for block in (W_Q, W_K, W_V):
            pl.when(n == block)(functools.partial(qkv_step, block))
    else:
        pl.when(n <= W_V)(functools.partial(qkv_step, None))

    @pl.when((n == W_HFF) | (n == W_HFB))
    def _():
        w = w_ref[...].astype(BF16)
        for rows in pieces:
            log_f, kd = _forget_gate(project(rows, w), lbl_ref[0])
            store_heads(g_ref, rows, log_f)
            store_heads(kd_ref, rows, kd)

    @pl.when((n == W_HQ) | (n >= W_HI))
    def _():
        w = w_ref[...].astype(BF16)
        for rows in pieces:
            store_heads(plain_ref, rows, project(rows, w))


def _project(h, group, w, lb_logits, rope_tables, want_kv32):
    m = MIX_TM
    tm = PROJ_TM
    tiles = m // tm
    rope = rope_tables is not None
    heads = MIX_TN // LANES
    n_blocks = W_HG + 1
    in_specs = [pl.BlockSpec((tm, D_MODEL), lambda i, n: (group * tiles + i, 0)),
                pl.BlockSpec((D_MODEL, MIX_TN), lambda i, n: (0, n)),
                pl.BlockSpec((1, DEPTH + 1, QK_H), lambda i, n: (jnp.clip(n - W_HFF, 0, 1), 0, 0))]
    args = [h, w, lb_logits]
    qkv_col = lambda n: jnp.minimum(n, W_V)
    if rope:
        t_rope = rope_tables[0].shape[1]
        rope_rows = min(t_rope, tm)
        n_rope_tiles = t_rope // rope_rows
        for t in rope_tables:
            in_specs.append(pl.BlockSpec((1, rope_rows, DV_A), lambda i, n: (qkv_col(n), i % n_rope_tiles, 0)))
            args.append(t)
    head_block = (heads, tm, LANES)
    plain_col = lambda n: jnp.where(n < W_HI, 0, n - W_HI + 1)
    gate_col = lambda n: jnp.clip(n - W_HFF, 0, 1)
    out_specs = [pl.BlockSpec(head_block, lambda i, n: (qkv_col(n), i, 0))]
    out_shape = [jax.ShapeDtypeStruct((3 * heads, m, LANES), BF16)]
    if want_kv32:
        requests = m // PROJ_SUB
        out_specs += [pl.BlockSpec((tm // PROJ_SUB, heads, LANES, PROJ_SUB), lambda i, n: (i, 0, 0, 0)),
                      pl.BlockSpec((tm, heads, LANES), lambda i, n: (i, 0, 0))]
        out_shape += [jax.ShapeDtypeStruct((requests, heads, LANES, PROJ_SUB), F32),
                      jax.ShapeDtypeStruct((m, heads, LANES), F32)]
    out_specs += [pl.BlockSpec(head_block, lambda i, n: (plain_col(n), i, 0)),
                  pl.BlockSpec(head_block, lambda i, n: (gate_col(n), i, 0)),
                  pl.BlockSpec(head_block, lambda i, n: (gate_col(n), i, 0))]
    out_shape += [jax.ShapeDtypeStruct((3 * heads, m, LANES), BF16),
                  jax.ShapeDtypeStruct((2 * heads, m, LANES), F32),
                  jax.ShapeDtypeStruct((2 * heads, m, LANES), BF16)]
    return pl.pallas_call(
        functools.partial(_proj_kernel, rope=rope, want_kv32=want_kv32),
        grid=(tiles, n_blocks),
        in_specs=in_specs,
        out_specs=out_specs,
        out_shape=out_shape,
        compiler_params=_params(("arbitrary", "arbitrary")),
        name="mix_proj",
    )(*args)


def _attn_kernel(*refs, lam_init, has_ctx):
    if has_ctx:
        lam_ref, g_ref, q_ref, k_ref, v_ref, kc_ref, vc_ref, o_ref = refs
    else:
        lam_ref, g_ref, q_ref, k_ref, v_ref, o_ref = refs
    lp = lam_ref[...]
    lam = (jnp.exp(jnp.sum(lp[0:1] * lp[1:2], axis=-1, keepdims=True))
           - jnp.exp(jnp.sum(lp[2:3] * lp[3:4], axis=-1, keepdims=True)) + lam_init)

    lane = lax.broadcasted_iota(jnp.int32, (1, DV_A), 1)
    comp0 = lane < DH_A
    tq = q_ref.shape[2]
    sub = min(ATT_SUB, tq)
    problems = []
    for hh in range(q_ref.shape[0]):
        keys = [k_ref[hh, 0]]
        vals = [v_ref[hh, 0]]
        if has_ctx:
            keys.append(kc_ref[0, :, hh * DV_A:(hh + 1) * DV_A].astype(BF16))
            vals.append(vc_ref[0, :, hh * DV_A:(hh + 1) * DV_A].astype(BF16))
        for r in range(tq // sub):
            q = q_ref[hh, 0, r * sub:(r + 1) * sub, :]
            zero = jnp.zeros_like(q)
            for qc in (jnp.where(comp0, q, zero), jnp.where(comp0, zero, q)):
                problems.append((qc, keys, vals))
    scores = [[_nt_dot(kk, qc) for kk in keys] for qc, keys, _ in problems]
    weights = []
    for ss in scores:
        mx = functools.reduce(jnp.maximum, [jnp.max(s, axis=0, keepdims=True) for s in ss])
        es = [jnp.exp2(s - mx) for s in ss]
        den = functools.reduce(jnp.add, [jnp.sum(e, axis=0, keepdims=True) for e in es])
        weights.append(([e.astype(BF16) for e in es], 1.0 / den))
    maps = [functools.reduce(jnp.add, [_tn_dot(vv, e) for e, vv in zip(es, vals)]) * r
            for (es, r), (_, _, vals) in zip(weights, problems)]
    idx = 0
    for hh in range(q_ref.shape[0]):
        for r in range(tq // sub):
            o = maps[idx] - lam * maps[idx + 1]
            idx += 2
            o = o * lax.rsqrt(jnp.mean(o * o, axis=0, keepdims=True) + RMS_EPS)
            o = o.T * g_ref[...]
            o_ref[hh, 0, r * sub:(r + 1) * sub, :] = (o * (1.0 - lam_init)).astype(o_ref.dtype)


def _attention(q, k, v, ctx_kv, lam_params, subln_g, lam_init):
    (q_arr, q_row), (k_arr, k_row), (v_arr, v_row) = q, k, v
    _, bsz, t, _ = q_arr.shape
    tq = min(ATT_TQ, t)
    hp = N_HEADS_A if t <= ATT_SHORT_SEQ else 1
    const = lambda b, h, i: (0, 0)
    rows = lambda first: (lambda b, h, i: (first // hp + h, b, 0, 0))
    in_specs = [
        pl.BlockSpec((4, DH_A), const),
        pl.BlockSpec((1, DV_A), const),
        pl.BlockSpec((hp, 1, tq, DV_A), lambda b, h, i: (q_row // hp + h, b, i, 0)),
        pl.BlockSpec((hp, 1, t, DV_A), rows(k_row)),
        pl.BlockSpec((hp, 1, t, DV_A), rows(v_row)),
    ]
    args = [lam_params, subln_g, q_arr, k_arr, v_arr]
    if ctx_kv is not None:
        p = ctx_kv[0].shape[1]
        head_all = lambda b, h, i: (b, 0, h)
        in_specs += [pl.BlockSpec((1, p, hp * DV_A), head_all), pl.BlockSpec((1, p, hp * DV_A), head_all)]
        args += list(ctx_kv)
    return pl.pallas_call(
        functools.partial(_attn_kernel, lam_init=lam_init, has_ctx=ctx_kv is not None),
        grid=(bsz, N_HEADS_A // hp, t // tq),
        in_specs=in_specs,
        out_specs=pl.BlockSpec((hp, 1, tq, DV_A), lambda b, h, i: (h, b, i, 0)),
        out_shape=jax.ShapeDtypeStruct((N_HEADS_A, bsz, t, DV_A), BF16),
        compiler_params=_params(("parallel", "parallel", "arbitrary")),
        name="attention",
    )(*args)


def _chunk_scan(g, reverse):
    r = g.shape[0]
    pos = lax.broadcasted_iota(jnp.int32, (r, 1), 0) % HGRN_C
    b = g
    sh = 1
    while sh < HGRN_C:
        if reverse:
            b = b + jnp.where(pos < HGRN_C - sh, pltpu.roll(b, r - sh, 0), 0.0)
        else:
            b = b + jnp.where(pos >= sh, pltpu.roll(b, sh, 0), 0.0)
        sh *= 2
    return b


def _hgrn_kernel(*refs, seq_len, has_s0, want_state):
    refs = list(refs)
    hq_ref, hi_ref, hg_ref, gf_ref, gb_ref, kf_ref, kb_ref, ng_ref = refs[:8]
    refs = refs[8:]
    s0_ref = refs.pop(0) if has_s0 else None
    o_ref = refs.pop(0)
    sfin_ref = refs.pop(0) if want_state else None
    od_scr, st_scr, b_scr, q_scr, k_scr, ok_scr = refs
    hp = hq_ref.shape[0]
    c = HGRN_C
    rb = min(HGRN_ROWS, seq_len)
    nb = seq_len // rb
    cpb = rb // c
    g_refs = (gf_ref, gb_ref)
    k_refs = (kf_ref, kb_ref)

    def queries(hq):
        hq = hq.astype(F32)
        return hq * jax.nn.sigmoid(hq)

    def flags(i, carry):
        rows = pl.ds(pl.multiple_of(i * rb, rb), rb)
        for d in range(2):
            worst = jnp.zeros((1, DK_H), F32)
            for hh in range(hp):
                g = g_refs[d][hh, 0, rows, :]
                for j in range(cpb):
                    worst = jnp.maximum(worst, -jnp.sum(g[j * c:(j + 1) * c], axis=0, keepdims=True))
            ok_scr[d, i] = (jnp.max(worst) <= HGRN_SAFE_DECAY).astype(jnp.int32)
        return carry

    lax.fori_loop(0, nb, flags, 0)

    for hh in range(hp):
        for d in range(2):
            if has_s0:
                st_scr[hh, d] = s0_ref[0, d, hh].T
            else:
                st_scr[hh, d] = jnp.zeros((DV_H, DK_H), F32)

    row = lax.broadcasted_iota(jnp.int32, (c, 1), 0)
    col = lax.broadcasted_iota(jnp.int32, (1, c), 1)

    def fast_block(d, r0):
        reverse = d == 1
        rows = pl.ds(r0, rb)
        n = hp * cpb
        flat = lambda a: a.reshape(hp * rb, a.shape[-1])
        chunks = lambda a: a.reshape(n, c, a.shape[-1])
        vb = chunks(hi_ref[:, 0, rows, :])
        b = _chunk_scan(flat(g_refs[d][:, 0, rows, :]), reverse)
        qd = chunks((queries(flat(hq_ref[:, 0, rows, :])) * jnp.exp(b)).astype(BF16))
        ku = chunks(flat(k_refs[d][:, 0, rows, :]).astype(F32) * jnp.exp(-b))
        b = chunks(b)
        e_end = jnp.exp(b[:, 0:1, :] if reverse else b[:, c - 1:c, :])
        causal = ((row <= col) if reverse else (row >= col))[None]
        a = jnp.einsum('nck,nsk->ncs', qd, ku.astype(BF16), preferred_element_type=F32)
        o = jnp.einsum('ncs,nsv->ncv', jnp.where(causal, a, 0.0).astype(BF16), vb, preferred_element_type=F32)
        u = jnp.einsum('nsv,nsk->nvk', vb, (ku * e_end).astype(BF16), preferred_element_type=F32)
        entering = [None] * n
        for hh in range(hp):
            st = st_scr[hh, d]
            for j in (range(cpb - 1, -1, -1) if reverse else range(cpb)):
                entering[hh * cpb + j] = st.astype(BF16)
                st = e_end[hh * cpb + j] * st + u[hh * cpb + j]
            st_scr[hh, d] = st
        o = o + jnp.einsum('nck,nvk->ncv', qd, jnp.stack(entering), preferred_element_type=F32)
        od_scr[:, d, rows, :] = o.reshape(hp, rb, DV_H)

    def exact_chunk(hh, d, r0):
        reverse = d == 1
        rows = pl.ds(r0, c)
        b = _chunk_scan(g_refs[d][hh, 0, rows, :], reverse)
        b_scr[...] = b
        q_scr[...] = queries(hq_ref[hh, 0, rows, :])
        k_scr[...] = k_refs[d][hh, 0, rows, :].astype(F32)

        def one_row(t, carry):
            bt = b_scr[pl.ds(t, 1), :]
            qt = q_scr[pl.ds(t, 1), :]
            seen = (row >= t) if reverse else (row <= t)
            w = jnp.exp(jnp.where(seen, bt - b_scr[...], -jnp.inf))
            p = jnp.sum(qt * k_scr[...] * w, axis=-1, keepdims=True)
            od_scr[hh, d, pl.ds(r0 + t, 1), :] = jnp.sum(p * hi_ref[hh, 0, rows, :].astype(F32),
                                                         axis=0, keepdims=True)
            return carry

        lax.fori_loop(0, c, one_row, 0)
        b_end = b[0:1] if reverse else b[c - 1:c]
        st = st_scr[hh, d]
        qd = (q_scr[...] * jnp.exp(b)).astype(BF16)
        od_scr[hh, d, rows, :] = od_scr[hh, d, rows, :] + _nt_dot(qd, st.astype(BF16))
        k_end = (k_scr[...] * jnp.exp(b_end - b)).astype(BF16)
        st_scr[hh, d] = jnp.exp(b_end) * st + _tn_dot(hi_ref[hh, 0, rows, :], k_end)

    def block(i, carry):
        starts = (pl.multiple_of(i * rb, rb), pl.multiple_of((nb - 1 - i) * rb, rb))
        mild = (ok_scr[0, i] + ok_scr[1, nb - 1 - i]) == 2

        @pl.when(mild)
        def _():
            for d in range(2):
                fast_block(d, starts[d])

        @pl.when(jnp.logical_not(mild))
        def _():
            for hh in range(hp):
                for d in range(2):
                    for j in (range(cpb - 1, -1, -1) if d == 1 else range(cpb)):
                        exact_chunk(hh, d, pl.multiple_of(starts[d] + j * c, c))

        return carry

    lax.fori_loop(0, nb, block, 0)

    def finish(i, carry):
        rows = pl.ds(pl.multiple_of(i * rb, rb), rb)
        for hh in range(hp):
            o = od_scr[hh, 0, rows, :] + od_scr[hh, 1, rows, :]
            o = o * lax.rsqrt(jnp.mean(o * o, axis=-1, keepdims=True) + RMS_EPS) * ng_ref[...]
            hg = hg_ref[hh, 0, rows, :].astype(F32)
            o_ref[hh, 0, rows, :] = (o * (hg * jax.nn.sigmoid(hg))).astype(o_ref.dtype)
        return carry

    lax.fori_loop(0, nb, finish, 0)
    if want_state:
        for hh in range(hp):
            for d in range(2):
                sfin_ref[0, d, hh] = st_scr[hh, d].T


def _hgrn(plain, g, kd, norm_g, s0, want_state):
    _, bsz, t, _ = plain.shape
    hp = HGRN_SHORT_HEADS if t <= HGRN_ROWS else 1
    stream = lambda s: pl.BlockSpec((hp, 1, t, DK_H), lambda b, h: (s * N_HEADS_H // hp + h, b, 0, 0))
    in_specs = [stream(0), stream(1), stream(2), stream(0), stream(1), stream(0), stream(1),
                pl.BlockSpec((1, DV_H), lambda b, h: (0, 0))]
    args = [plain, plain, plain, g, g, kd, kd, norm_g]
    state_spec = pl.BlockSpec((1, 2, hp, DK_H, DV_H), lambda b, h: (b, 0, h, 0, 0))
    if s0 is not None:
        in_specs.append(state_spec)
        args.append(s0)
    out_specs = [pl.BlockSpec((hp, 1, t, DV_H), lambda b, h: (h, b, 0, 0))]
    out_shape = [jax.ShapeDtypeStruct((N_HEADS_H, bsz, t, DV_H), BF16)]
    if want_state:
        out_specs.append(state_spec)
        out_shape.append(jax.ShapeDtypeStruct((bsz, 2, N_HEADS_H, DK_H, DV_H), F32))
    n_blocks = t // min(HGRN_ROWS, t)
    chunk = pltpu.VMEM((HGRN_C, DK_H), F32)
    outs = pl.pallas_call(
        functools.partial(_hgrn_kernel, seq_len=t, has_s0=s0 is not None, want_state=want_state),
        grid=(bsz, N_HEADS_H // hp),
        in_specs=in_specs,
        out_specs=out_specs,
        out_shape=out_shape,
        scratch_shapes=[
            pltpu.VMEM((hp, 2, t, DV_H), F32),
            pltpu.VMEM((hp, 2, DV_H, DK_H), F32),
            chunk, chunk, chunk,
            pltpu.SMEM((2, n_blocks), jnp.int32),
        ],
        compiler_params=_params(("parallel", "parallel")),
        name="hgrn",
    )(*args)
    return outs if want_state else (outs[0], None)


def _merge_ffn2_kernel(oa0_ref, oh0_ref, oa1_ref, oh1_ref, h_ref, x_ref, mod_ref,
                       wmix_hbm, wa_hbm, wh_hbm, wo_hbm, win_hbm, wout_hbm, lng_ref, lnb_ref, y0_ref, y1_ref,
                       wg_scr, wa_scr, wh_scr, wo_scr,
                       win_scr, wout_scr, wide_stage, square_stage, wide_sems, square_sems, act_scr, *, tiles_0):
    i = pl.program_id(0)
    first_group = i < tiles_0

    @pl.when(i == 0)
    def _():
        _load_weight(win_hbm, win_scr, wide_stage, wide_sems)
        for half in range(2):
            _load_weight(wmix_hbm, wg_scr, square_stage, square_sems,
                         src_col=W_GA0 * MIX_TN + half * D_MODEL, dst_col=half * D_MODEL)
        for w_hbm, w_scr in ((wa_hbm, wa_scr), (wh_hbm, wh_scr), (wo_hbm, wo_scr), (wout_hbm, wout_scr)):
            _load_weight(w_hbm, w_scr, square_stage, square_sems)

    def heads(ref0, ref1):
        pick = lambda hh: jnp.where(first_group, ref0[hh], ref1[hh])
        return jnp.concatenate([pick(hh) for hh in range(ref0.shape[0])], axis=-1)

    gates = jax.nn.sigmoid(jnp.dot(h_ref[...], wg_scr[...], preferred_element_type=F32))
    ba = jnp.dot(heads(oa0_ref, oa1_ref), wa_scr[...], preferred_element_type=F32)
    bh = jnp.dot(heads(oh0_ref, oh1_ref), wh_scr[...], preferred_element_type=F32)
    merged = gates[:, :D_MODEL] * ba + gates[:, D_MODEL:] * bh
    mix = jnp.dot(merged.astype(BF16), wo_scr[...], preferred_element_type=F32)
    x2 = _layer_norm(ALPHA * x_ref[...] + mod_ref[0, 5:6, :] * mix, lng_ref[1:2, :], lnb_ref[1:2, :])
    y = _half_step(x2, mod_ref, win_scr, wout_scr, lng_ref, lnb_ref, act_scr, 2)

    @pl.when(first_group)
    def _():
        y0_ref[...] = y

    @pl.when(jnp.logical_not(first_group))
    def _():
        y1_ref[...] = y


def _merge_ffn2(mixed, h, x, mod, cond_of_tile, w_mix, w_a, w_h, w_o, w_in, w_out, ln_g, ln_b):
    (oa0, oh0), (oa1, oh1) = mixed
    tiles_0, tiles_1 = oa0.shape[1] // FFN_TM, oa1.shape[1] // FFN_TM
    row = lambda i: (i, 0)
    const = lambda i: (0, 0)
    in_group_0 = lambda i: jnp.minimum(i, tiles_0 - 1)
    in_group_1 = lambda i: jnp.maximum(i - tiles_0, 0)
    hbm = pl.BlockSpec(memory_space=pl.ANY)
    return pl.pallas_call(
        functools.partial(_merge_ffn2_kernel, tiles_0=tiles_0),
        grid=(tiles_0 + tiles_1,),
        in_specs=[
            pl.BlockSpec((N_HEADS_A, FFN_TM, DV_A), lambda i: (0, in_group_0(i), 0)),
            pl.BlockSpec((N_HEADS_H, FFN_TM, DV_H), lambda i: (0, in_group_0(i), 0)),
            pl.BlockSpec((N_HEADS_A, FFN_TM, DV_A), lambda i: (0, in_group_1(i), 0)),
            pl.BlockSpec((N_HEADS_H, FFN_TM, DV_H), lambda i: (0, in_group_1(i), 0)),
            pl.BlockSpec((FFN_TM, D_MODEL), row),
            pl.BlockSpec((FFN_TM, D_MODEL), row),
            pl.BlockSpec((1, 9, D_MODEL), lambda i: (cond_of_tile(i, FFN_TM), 0, 0)),
            hbm, hbm, hbm, hbm, hbm, hbm,
            pl.BlockSpec((3, D_MODEL), const),
            pl.BlockSpec((3, D_MODEL), const),
        ],
        out_specs=[pl.BlockSpec((FFN_TM, D_MODEL), lambda i: (in_group_0(i), 0)),
                   pl.BlockSpec((FFN_TM, D_MODEL), lambda i: (in_group_1(i), 0))],
        out_shape=[jax.ShapeDtypeStruct((tiles_0 * FFN_TM, D_MODEL), F32),
                   jax.ShapeDtypeStruct((tiles_1 * FFN_TM, D_MODEL), F32)],
        scratch_shapes=[pltpu.VMEM((D_MODEL, 2 * D_MODEL), BF16), pltpu.VMEM((W_A, D_MODEL), BF16),
                        pltpu.VMEM((W_H, D_MODEL), BF16), pltpu.VMEM((D_MODEL, D_MODEL), BF16)]
                       + _ffn_weight_scratch() + [pltpu.VMEM((FFN_TM, D_FF), BF16)],
        compiler_params=_params(("arbitrary",)),
        name="merge_ffn2",
    )(oa0, oh0, oa1, oh1, h, x, mod, w_mix, w_a, w_h, w_o, w_in, w_out, ln_g, ln_b)


def _rope_tables(n_tok):
    rows = n_tok // GRID_W
    row = np.repeat(np.arange(rows, dtype=np.float32), GRID_W)
    col = np.tile(np.arange(GRID_W, dtype=np.float32), rows)
    half = DH_A // 2
    inv = (ROPE_BASE ** (-np.arange(0, half, 2, dtype=np.float32) / half)).astype(np.float32)
    ar = row[:, None] * inv
    ac = col[:, None] * inv
    ang = np.concatenate([ar, ar, ac, ac] * 2, axis=-1)
    cos, sin = np.cos(ang).astype(np.float32), np.sin(ang).astype(np.float32)
    scale = np.float32(Q_SCALE)
    return (jnp.asarray(np.stack([cos * scale, cos, np.ones_like(cos)])),
            jnp.asarray(np.stack([sin * scale, sin, np.zeros_like(sin)])))


def _mixer(h1, group, wts, lam_init, bsz, ctx):
    m = MIX_TM
    t = m // bsz
    latent = ctx is not None
    per_head = lambda a: a.reshape(a.shape[0], bsz, t, a.shape[-1])
    if latent:
        qkv, plain, g, kd = _project(h1, group, wts['w_mix_in'], wts['lb_logits'], _rope_tables(t), False)
        k = v = None
        ctx_kv = (ctx[0], ctx[1])
    else:
        assert t == PROJ_SUB
        qkv, k, v, plain, g, kd = _project(h1, group, wts['w_mix_in'], wts['lb_logits'], None, True)
        ctx_kv = None
    qkv = per_head(qkv)
    q_kv = [(qkv, W_Q * N_HEADS_A), (qkv, W_K * N_HEADS_A), (qkv, W_V * N_HEADS_A)]
    o_a = _attention(*q_kv, ctx_kv, wts['lam_params'], wts['attn_subln_g'], lam_init)
    o_h, s_fin = _hgrn(per_head(plain), per_head(g), per_head(kd), wts['hgrn_norm_g'],
                       ctx[2] if latent else None, not latent)
    return (o_a.reshape(N_HEADS_A, m, DV_A), o_h.reshape(N_HEADS_H, m, DV_H)), (k, v, s_fin)


def kernel(x_prompt, x_sample, cache_k, cache_v, state_hgrn, c, c_ctx, w_ada, b_ada, ffn1_w_in, ffn1_w_out,
           w_mix_in, lambda_q1, lambda_k1, lambda_q2, lambda_k2, attn_subln_g, hgrn_lb_logits, hgrn_norm_g,
           w_branch_a, w_branch_h, w_mix_out, ffn2_w_in, ffn2_w_out, ln_g, ln_b):
    batch, seq, _ = x_prompt.shape
    dec_batch, dec_seq, _ = x_sample.shape
    past = cache_k.shape[2]
    assert DEPTH == 1 and dec_batch + 1 <= MOD_ROWS
    lam_init = 0.8 - 0.6 * math.exp(-0.3 * 0)

    cond = jnp.concatenate([c_ctx[None, :], c, jnp.zeros((MOD_ROWS - 1 - dec_batch, D_MODEL), F32)], axis=0)
    mod = _modulation(cond, w_ada[0], b_ada[0])

    wts = {
        'w_mix_in': w_mix_in[0],
        'lam_params': jnp.concatenate([lambda_q1, lambda_k1, lambda_q2, lambda_k2], axis=0),
        'attn_subln_g': attn_subln_g, 'lb_logits': hgrn_lb_logits, 'hgrn_norm_g': hgrn_norm_g,
    }
    n_ctx = batch * seq
    assert n_ctx == MIX_TM and dec_batch * dec_seq == MIX_TM

    def cond_of_tile(i, tm):
        first = i * tm
        return jnp.where(first < n_ctx, 0, 1 + (first - n_ctx) // dec_seq)

    x1, h1 = _ffn1(x_prompt.reshape(n_ctx, D_MODEL), x_sample.reshape(dec_batch * dec_seq, D_MODEL), mod,
                   cond_of_tile, ffn1_w_in[0], ffn1_w_out[0], ln_g[0], ln_b[0])
    mixed_p, (k_c, v_c, s_c) = _mixer(h1, 0, wts, lam_init, batch, None)
    ctx = (cache_k[:, 0].reshape(dec_batch, past, QK_A), cache_v[:, 0].reshape(dec_batch, past, W_A),
           state_hgrn[:, 0])
    mixed_s, _ = _mixer(h1, 1, wts, lam_init, dec_batch, ctx)
    y_p, y_s = _merge_ffn2((mixed_p, mixed_s), h1, x1, mod, cond_of_tile, w_mix_in[0],
                           w_branch_a[0], w_branch_h[0], w_mix_out[0], ffn2_w_in[0], ffn2_w_out[0],
                           ln_g[0], ln_b[0])

    return (y_p.reshape(batch, seq, D_MODEL),
            y_s.reshape(dec_batch, dec_seq, D_MODEL),
            k_c.reshape(batch, DEPTH, N_HEADS_A, 2, DH_A, seq).transpose(0, 1, 5, 2, 3, 4),
            v_c.reshape(batch, DEPTH, seq, N_HEADS_A, DV_A),
            s_c.reshape(batch, DEPTH, 2, N_HEADS_H, DK_H, DV_H))
```

```python
import functools
import math

import numpy as np
import jax
import jax.numpy as jnp
from jax import lax
from jax.experimental import pallas as pl
from jax.experimental.pallas import tpu as pltpu

D_MODEL = 1024
DEPTH = 1
GRID_W = 64
N_HEADS_A = 4
DH_A = 64
DV_A = 2 * DH_A
QK_A = N_HEADS_A * 2 * DH_A
W_A = N_HEADS_A * DV_A
ROPE_BASE = 10000.0
N_HEADS_H = 4
DK_H = 128
DV_H = 128
QK_H = N_HEADS_H * DK_H
W_H = N_HEADS_H * DV_H
D_FF = 2816
MIX_IN = 2 * QK_A + W_A + 3 * QK_H + 2 * W_H + 2 * D_MODEL
ALPHA = (2 * DEPTH) ** 0.25
LN_EPS = 1e-5
RMS_EPS = 1e-6
Q_SCALE = DH_A ** -0.5 * math.log2(math.e)

F32 = jnp.float32
BF16 = jnp.bfloat16

LANES = 128
VMEM_LIMIT = 56 * 1024 * 1024

FFN_TM = 512
FFN_TF = 256
WIDE_CHUNK_ROWS = 128
SQUARE_CHUNK_ROWS = 256
MIX_TM = 4096
PROJ_TM = 2048
PROJ_SUB = 256
MIX_TN = 512
ATT_TQ = 1024
ATT_SUB = 512
ATT_SHORT_SEQ = 256
HGRN_C = 64
HGRN_ROWS = 1024
HGRN_SHORT_HEADS = 4
HGRN_SAFE_DECAY = 75.0
MOD_ROWS = 8
MOD_TN = 2304

W_Q, W_K, W_V, W_HQ, W_HFF, W_HFB, W_HI, W_HG, W_GA0, W_GA1, W_GH0, W_GH1 = range(12)


def _params(sem):
    return pltpu.CompilerParams(dimension_semantics=sem, vmem_limit_bytes=VMEM_LIMIT)


def _nt_dot(a, b):
    return lax.dot_general(a, b, (((1,), (1,)), ((), ())), preferred_element_type=F32)


def _tn_dot(a, b):
    return lax.dot_general(a, b, (((0,), (0,)), ((), ())), preferred_element_type=F32)


def _layer_norm(y, g, b):
    mu = jnp.mean(y, axis=-1, keepdims=True)
    yc = y - mu
    var = jnp.mean(yc * yc, axis=-1, keepdims=True)
    return yc * lax.rsqrt(var + LN_EPS) * g + b


def _mod_kernel(c_ref, w_ref, b_ref, o_ref):
    c = c_ref[...]
    a = (c * jax.nn.sigmoid(c)).astype(BF16)
    o_ref[...] = jnp.dot(a, w_ref[...].astype(BF16), preferred_element_type=F32) + b_ref[...]


def _modulation(cond, w_ada, b_ada):
    n_out = w_ada.shape[1]
    tn = MOD_TN
    out = pl.pallas_call(
        _mod_kernel,
        grid=(n_out // tn,),
        in_specs=[
            pl.BlockSpec((MOD_ROWS, D_MODEL), lambda n: (0, 0)),
            pl.BlockSpec((D_MODEL, tn), lambda n: (0, n)),
            pl.BlockSpec((1, tn), lambda n: (0, n)),
        ],
        out_specs=pl.BlockSpec((MOD_ROWS, tn), lambda n: (0, n)),
        out_shape=jax.ShapeDtypeStruct((MOD_ROWS, n_out), F32),
        compiler_params=_params(("arbitrary",)),
        name="modulation",
    )(cond, w_ada, b_ada.reshape(1, n_out))
    return out.reshape(MOD_ROWS, 9, D_MODEL)


def _load_weight(w_hbm, w_scr, stage, sems, src_col=0, dst_col=0):
    _, chunk, width = stage.shape
    n_chunks = w_hbm.shape[0] // chunk

    def copy(c):
        return pltpu.make_async_copy(w_hbm.at[pl.ds(c * chunk, chunk), pl.ds(src_col, width)],
                                     stage.at[c % 2], sems.at[c % 2])

    copy(0).start()
    for c in range(n_chunks):
        if c + 1 < n_chunks:
            copy(c + 1).start()
        copy(c).wait()
        w_scr[c * chunk:(c + 1) * chunk, dst_col:dst_col + width] = stage[c % 2].astype(BF16)


def _modulate(x, mod_ref, sub):
    return x * (1.0 + mod_ref[0, 3 * sub + 1:3 * sub + 2, :]) + mod_ref[0, 3 * sub:3 * sub + 1, :]


def _half_step(x, mod_ref, win_ref, wout_ref, lng_ref, lnb_ref, act_scr, sub):
    h = _modulate(x, mod_ref, sub).astype(BF16)
    for c in range(D_FF // FFN_TF):
        cols = slice(c * FFN_TF, (c + 1) * FFN_TF)
        up_cols = slice(D_FF + c * FFN_TF, D_FF + (c + 1) * FFN_TF)
        gate = jnp.dot(h, win_ref[:, cols], preferred_element_type=F32)
        up = jnp.dot(h, win_ref[:, up_cols], preferred_element_type=F32)
        act_scr[:, cols] = (gate * jax.nn.sigmoid(gate) * up).astype(BF16)
    f = jnp.dot(act_scr[...], wout_ref[...], preferred_element_type=F32)
    g = mod_ref[0, 3 * sub + 2:3 * sub + 3, :]
    return _layer_norm(ALPHA * x + 0.5 * g * f, lng_ref[0, sub:sub + 1, :], lnb_ref[0, sub:sub + 1, :])


def _ffn_weight_scratch():
    return [
        pltpu.VMEM((D_MODEL, 2 * D_FF), BF16),
        pltpu.VMEM((D_FF, D_MODEL), BF16),
        pltpu.VMEM((2, WIDE_CHUNK_ROWS, 2 * D_FF), F32),
        pltpu.VMEM((2, SQUARE_CHUNK_ROWS, D_MODEL), F32),
        pltpu.SemaphoreType.DMA((2,)),
        pltpu.SemaphoreType.DMA((2,)),
    ]


def _ffn1_kernel(xa_ref, xb_ref, mod_ref, win_hbm, wout_hbm, lng_ref, lnb_ref, xo_ref, ho_ref,
                 win_scr, wout_scr, wide_stage, square_stage, wide_sems, square_sems, act_scr, *, tiles_a):
    i = pl.program_id(0)

    @pl.when(i == 0)
    def _():
        _load_weight(win_hbm, win_scr, wide_stage, wide_sems)
        _load_weight(wout_hbm, wout_scr, square_stage, square_sems)

    x = jnp.where(i < tiles_a, xa_ref[...], xb_ref[...])
    xn = _half_step(x, mod_ref, win_scr, wout_scr, lng_ref, lnb_ref, act_scr, 0)
    xo_ref[...] = xn
    ho_ref[...] = _modulate(xn, mod_ref, 1).astype(BF16)


def _ffn1(xa, xb, mod, cond_of_tile, w_in, w_out, ln_g, ln_b):
    tiles_a, tiles_b = xa.shape[0] // FFN_TM, xb.shape[0] // FFN_TM
    m = xa.shape[0] + xb.shape[0]
    row = lambda i: (i, 0)
    const = lambda i: (0, 0)
    hbm = pl.BlockSpec(memory_space=pl.ANY)
    return pl.pallas_call(
        functools.partial(_ffn1_kernel, tiles_a=tiles_a),
        grid=(tiles_a + tiles_b,),
        in_specs=[
            pl.BlockSpec((FFN_TM, D_MODEL), lambda i: (jnp.minimum(i, tiles_a - 1), 0)),
            pl.BlockSpec((FFN_TM, D_MODEL), lambda i: (jnp.maximum(i - tiles_a, 0), 0)),
            pl.BlockSpec((1, 9, D_MODEL), lambda i: (cond_of_tile(i, FFN_TM), 0, 0)),
            hbm, hbm,
            pl.BlockSpec((DEPTH, 3, D_MODEL), lambda i: (0, 0, 0)),
            pl.BlockSpec((DEPTH, 3, D_MODEL), lambda i: (0, 0, 0)),
        ],
        out_specs=[pl.BlockSpec((FFN_TM, D_MODEL), row), pl.BlockSpec((FFN_TM, D_MODEL), row)],
        out_shape=[jax.ShapeDtypeStruct((m, D_MODEL), F32), jax.ShapeDtypeStruct((m, D_MODEL), BF16)],
        scratch_shapes=_ffn_weight_scratch() + [pltpu.VMEM((FFN_TM, D_FF), BF16)],
        compiler_params=_params(("arbitrary",)),
        name="ffn1",
    )(xa, xb, mod, w_in, w_out, ln_g, ln_b)


def _rope(x, cos, sin):
    half, quarter = DH_A // 2, DH_A // 4
    lane = lax.broadcasted_iota(jnp.int32, (1, LANES), 1)
    first_quarter = (lane % half) < quarter
    outs = []
    for hb in range(QK_A // LANES):
        xb = x[:, hb * LANES:(hb + 1) * LANES]
        ahead = pltpu.roll(xb, LANES - quarter, 1)
        behind = pltpu.roll(xb, quarter, 1)
        rot = jnp.where(first_quarter, -ahead, behind)
        outs.append(xb * cos + rot * sin)
    return jnp.concatenate(outs, axis=-1)


def _forget_gate(z, logits):
    mx = jnp.max(logits, axis=0, keepdims=True)
    el = jnp.exp(logits - mx)
    lb = el[0:1] / jnp.sum(el, axis=0, keepdims=True)
    sig = 0.5 + 0.5 * jnp.tanh(0.5 * z)
    u = (1.0 - lb) * sig
    f = lb + u
    return jnp.where(f > 0.0, jnp.log(f), z), (1.0 - lb) - u


def _proj_kernel(*refs, rope, want_kv32):
    refs = list(refs)
    h_ref, w_ref, lbl_ref = refs[:3]
    refs = refs[3:]
    cos_ref, sin_ref = (refs.pop(0), refs.pop(0)) if rope else (None, None)
    qkv_ref = refs.pop(0)
    kt_ref, v32_ref = (refs.pop(0), refs.pop(0)) if want_kv32 else (None, None)
    plain_ref, g_ref, kd_ref = refs
    n = pl.program_id(1)

    pieces = [slice(r, r + PROJ_SUB) for r in range(0, h_ref.shape[0], PROJ_SUB)]

    def project(rows, w):
        return jnp.dot(h_ref[rows, :], w, preferred_element_type=F32)

    def store_heads(ref, rows, y):
        for hh in range(MIX_TN // LANES):
            ref[hh, rows, :] = y[:, hh * LANES:(hh + 1) * LANES].astype(ref.dtype)

    def qkv_step(block):
        w = w_ref[...].astype(BF16)
        for p, rows in enumerate(pieces):
            y = project(rows, w)
            if block == W_K:
                for hh in range(MIX_TN // LANES):
                    kt_ref[p, hh] = y[:, hh * LANES:(hh + 1) * LANES].T
            if block == W_V:
                v32_ref[rows] = y.reshape(PROJ_SUB, MIX_TN // LANES, LANES)
            if rope:
                first = rows.start % cos_ref.shape[1]
                table_rows = slice(first, first + PROJ_SUB)
                y = _rope(y, cos_ref[0, table_rows, :], sin_ref[0, table_rows, :])
            elif block is None:
                y = y * jnp.where(n == W_Q, Q_SCALE, 1.0)
            elif block == W_Q:
                y = y * Q_SCALE
            store_heads(qkv_ref, rows, y)

    if want_kv32:
        for block in (W_Q, W_K, W_V):
            pl.when(n == block)(functools.partial(qkv_step, block))
    else:
        pl.when(n <= W_V)(functools.partial(qkv_step, None))

    @pl.when((n == W_HFF) | (n == W_HFB))
    def _():
        w = w_ref[...].astype(BF16)
        for rows in pieces:
            log_f, kd = _forget_gate(project(rows, w), lbl_ref[0])
            store_heads(g_ref, rows, log_f)
            store_heads(kd_ref, rows, kd)

    @pl.when((n == W_HQ) | (n >= W_HI))
    def _():
        w = w_ref[...].astype(BF16)
        for rows in pieces:
            store_heads(plain_ref, rows, project(rows, w))


def _project(h, group, w, lb_logits, rope_tables, want_kv32):
    m = MIX_TM
    tm = PROJ_TM
    tiles = m // tm
    rope = rope_tables is not None
    heads = MIX_TN // LANES
    n_blocks = W_HG + 1
    in_specs = [pl.BlockSpec((tm, D_MODEL), lambda i, n: (group * tiles + i, 0)),
                pl.BlockSpec((D_MODEL, MIX_TN), lambda i, n: (0, n)),
                pl.BlockSpec((1, DEPTH + 1, QK_H), lambda i, n: (jnp.clip(n - W_HFF, 0, 1), 0, 0))]
    args = [h, w, lb_logits]
    qkv_col = lambda n: jnp.minimum(n, W_V)
    if rope:
        t_rope = rope_tables[0].shape[1]
        rope_rows = min(t_rope, tm)
        n_rope_tiles = t_rope // rope_rows
        for t in rope_tables:
            in_specs.append(pl.BlockSpec((1, rope_rows, DV_A), lambda i, n: (qkv_col(n), i % n_rope_tiles, 0)))
            args.append(t)
    head_block = (heads, tm, LANES)
    plain_col = lambda n: jnp.where(n < W_HI, 0, n - W_HI + 1)
    gate_col = lambda n: jnp.clip(n - W_HFF, 0, 1)
    out_specs = [pl.BlockSpec(head_block, lambda i, n: (qkv_col(n), i, 0))]
    out_shape = [jax.ShapeDtypeStruct((3 * heads, m, LANES), BF16)]
    if want_kv32:
        requests = m // PROJ_SUB
        out_specs += [pl.BlockSpec((tm // PROJ_SUB, heads, LANES, PROJ_SUB), lambda i, n: (i, 0, 0, 0)),
                      pl.BlockSpec((tm, heads, LANES), lambda i, n: (i, 0, 0))]
        out_shape += [jax.ShapeDtypeStruct((requests, heads, LANES, PROJ_SUB), F32),
                      jax.ShapeDtypeStruct((m, heads, LANES), F32)]
    out_specs += [pl.BlockSpec(head_block, lambda i, n: (plain_col(n), i, 0)),
                  pl.BlockSpec(head_block, lambda i, n: (gate_col(n), i, 0)),
                  pl.BlockSpec(head_block, lambda i, n: (gate_col(n), i, 0))]
    out_shape += [jax.ShapeDtypeStruct((3 * heads, m, LANES), BF16),
                  jax.ShapeDtypeStruct((2 * heads, m, LANES), F32),
                  jax.ShapeDtypeStruct((2 * heads, m, LANES), BF16)]
    return pl.pallas_call(
        functools.partial(_proj_kernel, rope=rope, want_kv32=want_kv32),
        grid=(tiles, n_blocks),
        in_specs=in_specs,
        out_specs=out_specs,
        out_shape=out_shape,
        compiler_params=_params(("arbitrary", "arbitrary")),
        name="mix_proj",
    )(*args)


def _attn_kernel(*refs, lam_init, has_ctx):
    if has_ctx:
        lam_ref, g_ref, q_ref, k_ref, v_ref, kc_ref, vc_ref, o_ref = refs
    else:
        lam_ref, g_ref, q_ref, k_ref, v_ref, o_ref = refs
    lp = lam_ref[...]
    lam = (jnp.exp(jnp.sum(lp[0:1] * lp[1:2], axis=-1, keepdims=True))
           - jnp.exp(jnp.sum(lp[2:3] * lp[3:4], axis=-1, keepdims=True)) + lam_init)

    lane = lax.broadcasted_iota(jnp.int32, (1, DV_A), 1)
    comp0 = lane < DH_A
    tq = q_ref.shape[2]
    sub = min(ATT_SUB, tq)
    problems = []
    for hh in range(q_ref.shape[0]):
        keys = [k_ref[hh, 0]]
        vals = [v_ref[hh, 0]]
        if has_ctx:
            keys.append(kc_ref[0, :, hh * DV_A:(hh + 1) * DV_A].astype(BF16))
            vals.append(vc_ref[0, :, hh * DV_A:(hh + 1) * DV_A].astype(BF16))
        for r in range(tq // sub):
            q = q_ref[hh, 0, r * sub:(r + 1) * sub, :]
            zero = jnp.zeros_like(q)
            for qc in (jnp.where(comp0, q, zero), jnp.where(comp0, zero, q)):
                problems.append((qc, keys, vals))
    scores = [[_nt_dot(kk, qc) for kk in keys] for qc, keys, _ in problems]
    weights = []
    for ss in scores:
        mx = functools.reduce(jnp.maximum, [jnp.max(s, axis=0, keepdims=True) for s in ss])
        es = [jnp.exp2(s - mx) for s in ss]
        den = functools.reduce(jnp.add, [jnp.sum(e, axis=0, keepdims=True) for e in es])
        weights.append(([e.astype(BF16) for e in es], 1.0 / den))
    maps = [functools.reduce(jnp.add, [_tn_dot(vv, e) for e, vv in zip(es, vals)]) * r
            for (es, r), (_, _, vals) in zip(weights, problems)]
    idx = 0
    for hh in range(q_ref.shape[0]):
        for r in range(tq // sub):
            o = maps[idx] - lam * maps[idx + 1]
            idx += 2
            o = o * lax.rsqrt(jnp.mean(o * o, axis=0, keepdims=True) + RMS_EPS)
            o = o.T * g_ref[...]
            o_ref[hh, 0, r * sub:(r + 1) * sub, :] = (o * (1.0 - lam_init)).astype(o_ref.dtype)


def _attention(q, k, v, ctx_kv, lam_params, subln_g, lam_init):
    (q_arr, q_row), (k_arr, k_row), (v_arr, v_row) = q, k, v
    _, bsz, t, _ = q_arr.shape
    tq = min(ATT_TQ, t)
    hp = N_HEADS_A if t <= ATT_SHORT_SEQ else 1
    const = lambda b, h, i: (0, 0)
    rows = lambda first: (lambda b, h, i: (first // hp + h, b, 0, 0))
    in_specs = [
        pl.BlockSpec((4, DH_A), const),
        pl.BlockSpec((1, DV_A), const),
        pl.BlockSpec((hp, 1, tq, DV_A), lambda b, h, i: (q_row // hp + h, b, i, 0)),
        pl.BlockSpec((hp, 1, t, DV_A), rows(k_row)),
        pl.BlockSpec((hp, 1, t, DV_A), rows(v_row)),
    ]
    args = [lam_params, subln_g, q_arr, k_arr, v_arr]
    if ctx_kv is not None:
        p = ctx_kv[0].shape[1]
        head_all = lambda b, h, i: (b, 0, h)
        in_specs += [pl.BlockSpec((1, p, hp * DV_A), head_all), pl.BlockSpec((1, p, hp * DV_A), head_all)]
        args += list(ctx_kv)
    return pl.pallas_call(
        functools.partial(_attn_kernel, lam_init=lam_init, has_ctx=ctx_kv is not None),
        grid=(bsz, N_HEADS_A // hp, t // tq),
        in_specs=in_specs,
        out_specs=pl.BlockSpec((hp, 1, tq, DV_A), lambda b, h, i: (h, b, i, 0)),
        out_shape=jax.ShapeDtypeStruct((N_HEADS_A, bsz, t, DV_A), BF16),
        compiler_params=_params(("parallel", "parallel", "arbitrary")),
        name="attention",
    )(*args)


def _chunk_scan(g, reverse):
    r = g.shape[0]
    pos = lax.broadcasted_iota(jnp.int32, (r, 1), 0) % HGRN_C
    b = g
    sh = 1
    while sh < HGRN_C:
        if reverse:
            b = b + jnp.where(pos < HGRN_C - sh, pltpu.roll(b, r - sh, 0), 0.0)
        else:
            b = b + jnp.where(pos >= sh, pltpu.roll(b, sh, 0), 0.0)
        sh *= 2
    return b


def _hgrn_kernel(*refs, seq_len, has_s0, want_state):
    refs = list(refs)
    hq_ref, hi_ref, hg_ref, gf_ref, gb_ref, kf_ref, kb_ref, ng_ref = refs[:8]
    refs = refs[8:]
    s0_ref = refs.pop(0) if has_s0 else None
    o_ref = refs.pop(0)
    sfin_ref = refs.pop(0) if want_state else None
    od_scr, st_scr, b_scr, q_scr, k_scr, ok_scr = refs
    hp = hq_ref.shape[0]
    c = HGRN_C
    rb = min(HGRN_ROWS, seq_len)
    nb = seq_len // rb
    cpb = rb // c
    g_refs = (gf_ref, gb_ref)
    k_refs = (kf_ref, kb_ref)

    def queries(hq):
        hq = hq.astype(F32)
        return hq * jax.nn.sigmoid(hq)

    def flags(i, carry):
        rows = pl.ds(pl.multiple_of(i * rb, rb), rb)
        for d in range(2):
            worst = jnp.zeros((1, DK_H), F32)
            for hh in range(hp):
                g = g_refs[d][hh, 0, rows, :]
                for j in range(cpb):
                    worst = jnp.maximum(worst, -jnp.sum(g[j * c:(j + 1) * c], axis=0, keepdims=True))
            ok_scr[d, i] = (jnp.max(worst) <= HGRN_SAFE_DECAY).astype(jnp.int32)
        return carry

    lax.fori_loop(0, nb, flags, 0)

    for hh in range(hp):
        for d in range(2):
            if has_s0:
                st_scr[hh, d] = s0_ref[0, d, hh].T
            else:
                st_scr[hh, d] = jnp.zeros((DV_H, DK_H), F32)

    row = lax.broadcasted_iota(jnp.int32, (c, 1), 0)
    col = lax.broadcasted_iota(jnp.int32, (1, c), 1)

    def fast_block(d, r0):
        reverse = d == 1
        rows = pl.ds(r0, rb)
        n = hp * cpb
        flat = lambda a: a.reshape(hp * rb, a.shape[-1])
        chunks = lambda a: a.reshape(n, c, a.shape[-1])
        vb = chunks(hi_ref[:, 0, rows, :])
        b = _chunk_scan(flat(g_refs[d][:, 0, rows, :]), reverse)
        qd = chunks((queries(flat(hq_ref[:, 0, rows, :])) * jnp.exp(b)).astype(BF16))
        ku = chunks(flat(k_refs[d][:, 0, rows, :]).astype(F32) * jnp.exp(-b))
        b = chunks(b)
        e_end = jnp.exp(b[:, 0:1, :] if reverse else b[:, c - 1:c, :])
        causal = ((row <= col) if reverse else (row >= col))[None]
        a = jnp.einsum('nck,nsk->ncs', qd, ku.astype(BF16), preferred_element_type=F32)
        o = jnp.einsum('ncs,nsv->ncv', jnp.where(causal, a, 0.0).astype(BF16), vb, preferred_element_type=F32)
        u = jnp.einsum('nsv,nsk->nvk', vb, (ku * e_end).astype(BF16), preferred_element_type=F32)
        entering = [None] * n
        for hh in range(hp):
            st = st_scr[hh, d]
            for j in (range(cpb - 1, -1, -1) if reverse else range(cpb)):
                entering[hh * cpb + j] = st.astype(BF16)
                st = e_end[hh * cpb + j] * st + u[hh * cpb + j]
            st_scr[hh, d] = st
        o = o + jnp.einsum('nck,nvk->ncv', qd, jnp.stack(entering), preferred_element_type=F32)
        od_scr[:, d, rows, :] = o.reshape(hp, rb, DV_H)

    def exact_chunk(hh, d, r0):
        reverse = d == 1
        rows = pl.ds(r0, c)
        b = _chunk_scan(g_refs[d][hh, 0, rows, :], reverse)
        b_scr[...] = b
        q_scr[...] = queries(hq_ref[hh, 0, rows, :])
        k_scr[...] = k_refs[d][hh, 0, rows, :].astype(F32)

        def one_row(t, carry):
            bt = b_scr[pl.ds(t, 1), :]
            qt = q_scr[pl.ds(t, 1), :]
            seen = (row >= t) if reverse else (row <= t)
            w = jnp.exp(jnp.where(seen, bt - b_scr[...], -jnp.inf))
            p = jnp.sum(qt * k_scr[...] * w, axis=-1, keepdims=True)
            od_scr[hh, d, pl.ds(r0 + t, 1), :] = jnp.sum(p * hi_ref[hh, 0, rows, :].astype(F32),
                                                         axis=0, keepdims=True)
            return carry

        lax.fori_loop(0, c, one_row, 0)
        b_end = b[0:1] if reverse else b[c - 1:c]
        st = st_scr[hh, d]
        qd = (q_scr[...] * jnp.exp(b)).astype(BF16)
        od_scr[hh, d, rows, :] = od_scr[hh, d, rows, :] + _nt_dot(qd, st.astype(BF16))
        k_end = (k_scr[...] * jnp.exp(b_end - b)).astype(BF16)
        st_scr[hh, d] = jnp.exp(b_end) * st + _tn_dot(hi_ref[hh, 0, rows, :], k_end)

    def block(i, carry):
        starts = (pl.multiple_of(i * rb, rb), pl.multiple_of((nb - 1 - i) * rb, rb))
        mild = (ok_scr[0, i] + ok_scr[1, nb - 1 - i]) == 2

        @pl.when(mild)
        def _():
            for d in range(2):
                fast_block(d, starts[d])

        @pl.when(jnp.logical_not(mild))
        def _():
            for hh in range(hp):
                for d in range(2):
                    for j in (range(cpb - 1, -1, -1) if d == 1 else range(cpb)):
                        exact_chunk(hh, d, pl.multiple_of(starts[d] + j * c, c))

        return carry

    lax.fori_loop(0, nb, block, 0)

    def finish(i, carry):
        rows = pl.ds(pl.multiple_of(i * rb, rb), rb)
        for hh in range(hp):
            o = od_scr[hh, 0, rows, :] + od_scr[hh, 1, rows, :]
            o = o * lax.rsqrt(jnp.mean(o * o, axis=-1, keepdims=True) + RMS_EPS) * ng_ref[...]
            hg = hg_ref[hh, 0, rows, :].astype(F32)
            o_ref[hh, 0, rows, :] = (o * (hg * jax.nn.sigmoid(hg))).astype(o_ref.dtype)
        return carry

    lax.fori_loop(0, nb, finish, 0)
    if want_state:
        for hh in range(hp):
            for d in range(2):
                sfin_ref[0, d, hh] = st_scr[hh, d].T


def _hgrn(plain, g, kd, norm_g, s0, want_state):
    _, bsz, t, _ = plain.shape
    hp = HGRN_SHORT_HEADS if t <= HGRN_ROWS else 1
    stream = lambda s: pl.BlockSpec((hp, 1, t, DK_H), lambda b, h: (s * N_HEADS_H // hp + h, b, 0, 0))
    in_specs = [stream(0), stream(1), stream(2), stream(0), stream(1), stream(0), stream(1),
                pl.BlockSpec((1, DV_H), lambda b, h: (0, 0))]
    args = [plain, plain, plain, g, g, kd, kd, norm_g]
    state_spec = pl.BlockSpec((1, 2, hp, DK_H, DV_H), lambda b, h: (b, 0, h, 0, 0))
    if s0 is not None:
        in_specs.append(state_spec)
        args.append(s0)
    out_specs = [pl.BlockSpec((hp, 1, t, DV_H), lambda b, h: (h, b, 0, 0))]
    out_shape = [jax.ShapeDtypeStruct((N_HEADS_H, bsz, t, DV_H), BF16)]
    if want_state:
        out_specs.append(state_spec)
        out_shape.append(jax.ShapeDtypeStruct((bsz, 2, N_HEADS_H, DK_H, DV_H), F32))
    n_blocks = t // min(HGRN_ROWS, t)
    chunk = pltpu.VMEM((HGRN_C, DK_H), F32)
    outs = pl.pallas_call(
        functools.partial(_hgrn_kernel, seq_len=t, has_s0=s0 is not None, want_state=want_state),
        grid=(bsz, N_HEADS_H // hp),
        in_specs=in_specs,
        out_specs=out_specs,
        out_shape=out_shape,
        scratch_shapes=[
            pltpu.VMEM((hp, 2, t, DV_H), F32),
            pltpu.VMEM((hp, 2, DV_H, DK_H), F32),
            chunk, chunk, chunk,
            pltpu.SMEM((2, n_blocks), jnp.int32),
        ],
        compiler_params=_params(("parallel", "parallel")),
        name="hgrn",
    )(*args)
    return outs if want_state else (outs[0], None)


def _merge_ffn2_kernel(oa0_ref, oh0_ref, oa1_ref, oh1_ref, h_ref, x_ref, mod_ref,
                       wmix_hbm, wa_hbm, wh_hbm, wo_hbm, win_hbm, wout_hbm, lng_ref, lnb_ref, y0_ref, y1_ref,
                       wg_scr, wa_scr, wh_scr, wo_scr,
                       win_scr, wout_scr, wide_stage, square_stage, wide_sems, square_sems, act_scr, *, tiles_0):
    i = pl.program_id(0)
    first_group = i < tiles_0

    @pl.when(i == 0)
    def _():
        _load_weight(win_hbm, win_scr, wide_stage, wide_sems)
        for half in range(2):
            _load_weight(wmix_hbm, wg_scr, square_stage, square_sems,
                         src_col=W_GA0 * MIX_TN + half * D_MODEL, dst_col=half * D_MODEL)
        for w_hbm, w_scr in ((wa_hbm, wa_scr), (wh_hbm, wh_scr), (wo_hbm, wo_scr), (wout_hbm, wout_scr)):
            _load_weight(w_hbm, w_scr, square_stage, square_sems)

    def heads(ref0, ref1):
        pick = lambda hh: jnp.where(first_group, ref0[hh], ref1[hh])
        return jnp.concatenate([pick(hh) for hh in range(ref0.shape[0])], axis=-1)

    gates = jax.nn.sigmoid(jnp.dot(h_ref[...], wg_scr[...], preferred_element_type=F32))
    ba = jnp.dot(heads(oa0_ref, oa1_ref), wa_scr[...], preferred_element_type=F32)
    bh = jnp.dot(heads(oh0_ref, oh1_ref), wh_scr[...], preferred_element_type=F32)
    merged = gates[:, :D_MODEL] * ba + gates[:, D_MODEL:] * bh
    mix = jnp.dot(merged.astype(BF16), wo_scr[...], preferred_element_type=F32)
    x2 = _layer_norm(ALPHA * x_ref[...] + mod_ref[0, 5:6, :] * mix, lng_ref[0, 1:2, :], lnb_ref[0, 1:2, :])
    y = _half_step(x2, mod_ref, win_scr, wout_scr, lng_ref, lnb_ref, act_scr, 2)

    @pl.when(first_group)
    def _():
        y0_ref[...] = y

    @pl.when(jnp.logical_not(first_group))
    def _():
        y1_ref[...] = y


def _merge_ffn2(mixed, h, x, mod, cond_of_tile, w_mix, w_a, w_h, w_o, w_in, w_out, ln_g, ln_b):
    (oa0, oh0), (oa1, oh1) = mixed
    tiles_0, tiles_1 = oa0.shape[1] // FFN_TM, oa1.shape[1] // FFN_TM
    row = lambda i: (i, 0)
    const = lambda i: (0, 0)
    in_group_0 = lambda i: jnp.minimum(i, tiles_0 - 1)
    in_group_1 = lambda i: jnp.maximum(i - tiles_0, 0)
    hbm = pl.BlockSpec(memory_space=pl.ANY)
    return pl.pallas_call(
        functools.partial(_merge_ffn2_kernel, tiles_0=tiles_0),
        grid=(tiles_0 + tiles_1,),
        in_specs=[
            pl.BlockSpec((N_HEADS_A, FFN_TM, DV_A), lambda i: (0, in_group_0(i), 0)),
            pl.BlockSpec((N_HEADS_H, FFN_TM, DV_H), lambda i: (0, in_group_0(i), 0)),
            pl.BlockSpec((N_HEADS_A, FFN_TM, DV_A), lambda i: (0, in_group_1(i), 0)),
            pl.BlockSpec((N_HEADS_H, FFN_TM, DV_H), lambda i: (0, in_group_1(i), 0)),
            pl.BlockSpec((FFN_TM, D_MODEL), row),
            pl.BlockSpec((FFN_TM, D_MODEL), row),
            pl.BlockSpec((1, 9, D_MODEL), lambda i: (cond_of_tile(i, FFN_TM), 0, 0)),
            hbm, hbm, hbm, hbm, hbm, hbm,
            pl.BlockSpec((DEPTH, 3, D_MODEL), lambda i: (0, 0, 0)),
            pl.BlockSpec((DEPTH, 3, D_MODEL), lambda i: (0, 0, 0)),
        ],
        out_specs=[pl.BlockSpec((FFN_TM, D_MODEL), lambda i: (in_group_0(i), 0)),
                   pl.BlockSpec((FFN_TM, D_MODEL), lambda i: (in_group_1(i), 0))],
        out_shape=[jax.ShapeDtypeStruct((tiles_0 * FFN_TM, D_MODEL), F32),
                   jax.ShapeDtypeStruct((tiles_1 * FFN_TM, D_MODEL), F32)],
        scratch_shapes=[pltpu.VMEM((D_MODEL, 2 * D_MODEL), BF16), pltpu.VMEM((W_A, D_MODEL), BF16),
                        pltpu.VMEM((W_H, D_MODEL), BF16), pltpu.VMEM((D_MODEL, D_MODEL), BF16)]
                       + _ffn_weight_scratch() + [pltpu.VMEM((FFN_TM, D_FF), BF16)],
        compiler_params=_params(("arbitrary",)),
        name="merge_ffn2",
    )(oa0, oh0, oa1, oh1, h, x, mod, w_mix, w_a, w_h, w_o, w_in, w_out, ln_g, ln_b)


def _rope_tables(n_tok):
    rows = n_tok // GRID_W
    row = np.repeat(np.arange(rows, dtype=np.float32), GRID_W)
    col = np.tile(np.arange(GRID_W, dtype=np.float32), rows)
    half = DH_A // 2
    inv = (ROPE_BASE ** (-np.arange(0, half, 2, dtype=np.float32) / half)).astype(np.float32)
    ar = row[:, None] * inv
    ac = col[:, None] * inv
    ang = np.concatenate([ar, ar, ac, ac] * 2, axis=-1)
    cos, sin = np.cos(ang).astype(np.float32), np.sin(ang).astype(np.float32)
    scale = np.float32(Q_SCALE)
    return (jnp.asarray(np.stack([cos * scale, cos, np.ones_like(cos)])),
            jnp.asarray(np.stack([sin * scale, sin, np.zeros_like(sin)])))


def _mixer(h1, group, wts, lam_init, bsz, ctx):
    m = MIX_TM
    t = m // bsz
    latent = ctx is not None
    per_head = lambda a: a.reshape(a.shape[0], bsz, t, a.shape[-1])
    if latent:
        qkv, plain, g, kd = _project(h1, group, wts['w_mix_in'], wts['lb_logits'], _rope_tables(t), False)
        k = v = None
        ctx_kv = (ctx[0], ctx[1])
    else:
        assert t == PROJ_SUB
        qkv, k, v, plain, g, kd = _project(h1, group, wts['w_mix_in'], wts['lb_logits'], None, True)
        ctx_kv = None
    qkv = per_head(qkv)
    q_kv = [(qkv, W_Q * N_HEADS_A), (qkv, W_K * N_HEADS_A), (qkv, W_V * N_HEADS_A)]
    o_a = _attention(*q_kv, ctx_kv, wts['lam_params'], wts['attn_subln_g'], lam_init)
    o_h, s_fin = _hgrn(per_head(plain), per_head(g), per_head(kd), wts['hgrn_norm_g'],
                       ctx[2] if latent else None, not latent)
    return (o_a.reshape(N_HEADS_A, m, DV_A), o_h.reshape(N_HEADS_H, m, DV_H)), (k, v, s_fin)


def kernel(x_prompt, x_sample, cache_k, cache_v, state_hgrn, c, c_ctx, w_ada, b_ada, ffn1_w_in, ffn1_w_out,
           w_mix_in, lambda_q1, lambda_k1, lambda_q2, lambda_k2, attn_subln_g, hgrn_lb_logits, hgrn_norm_g,
           w_branch_a, w_branch_h, w_mix_out, ffn2_w_in, ffn2_w_out, ln_g, ln_b):
    batch, seq, _ = x_prompt.shape
    dec_batch, dec_seq, _ = x_sample.shape
    past = cache_k.shape[2]
    assert DEPTH == 1 and dec_batch + 1 <= MOD_ROWS
    lam_init = 0.8 - 0.6 * math.exp(-0.3 * 0)

    cond = jnp.concatenate([c_ctx[None, :], c, jnp.zeros((MOD_ROWS - 1 - dec_batch, D_MODEL), F32)], axis=0)
    mod = _modulation(cond, w_ada[0], b_ada[0])

    wts = {
        'w_mix_in': w_mix_in[0],
        'lam_params': jnp.concatenate([lambda_q1, lambda_k1, lambda_q2, lambda_k2], axis=0),
        'attn_subln_g': attn_subln_g, 'lb_logits': hgrn_lb_logits, 'hgrn_norm_g': hgrn_norm_g,
    }
    n_ctx = batch * seq
    assert n_ctx == MIX_TM and dec_batch * dec_seq == MIX_TM

    def cond_of_tile(i, tm):
        first = i * tm
        return jnp.where(first < n_ctx, 0, 1 + (first - n_ctx) // dec_seq)

    x1, h1 = _ffn1(x_prompt.reshape(n_ctx, D_MODEL), x_sample.reshape(dec_batch * dec_seq, D_MODEL), mod,
                   cond_of_tile, ffn1_w_in[0], ffn1_w_out[0], ln_g, ln_b)
    mixed_p, (k_c, v_c, s_c) = _mixer(h1, 0, wts, lam_init, batch, None)
    ctx = (cache_k[:, 0].reshape(dec_batch, past, QK_A), cache_v[:, 0].reshape(dec_batch, past, W_A),
           state_hgrn[:, 0])
    mixed_s, _ = _mixer(h1, 1, wts, lam_init, dec_batch, ctx)
    y_p, y_s = _merge_ffn2((mixed_p, mixed_s), h1, x1, mod, cond_of_tile, w_mix_in[0],
                           w_branch_a[0], w_branch_h[0], w_mix_out[0], ffn2_w_in[0], ffn2_w_out[0],
                           ln_g, ln_b)

    return (y_p.reshape(batch, seq, D_MODEL),
            y_s.reshape(dec_batch, dec_seq, D_MODEL),
            k_c.reshape(batch, DEPTH, N_HEADS_A, 2, DH_A, seq).transpose(0, 1, 5, 2, 3, 4),
            v_c.reshape(batch, DEPTH, seq, N_HEADS_A, DV_A),
            s_c.reshape(batch, DEPTH, 2, N_HEADS_H, DK_H, DV_H))
```

```python
import functools
import math

import numpy as np
import jax
import jax.numpy as jnp
from jax import lax
from jax.experimental import pallas as pl
from jax.experimental.pallas import tpu as pltpu

D_MODEL = 1024
DEPTH = 1
GRID_W = 64
N_HEADS_A = 4
DH_A = 64
DV_A = 2 * DH_A
QK_A = N_HEADS_A * 2 * DH_A
W_A = N_HEADS_A * DV_A
ROPE_BASE = 10000.0
N_HEADS_H = 4
DK_H = 128
DV_H = 128
QK_H = N_HEADS_H * DK_H
W_H = N_HEADS_H * DV_H
D_FF = 2816
MIX_IN = 2 * QK_A + W_A + 3 * QK_H + 2 * W_H + 2 * D_MODEL
ALPHA = (2 * DEPTH) ** 0.25
LN_EPS = 1e-5
RMS_EPS = 1e-6
Q_SCALE = DH_A ** -0.5 * math.log2(math.e)

F32 = jnp.float32
BF16 = jnp.bfloat16

LANES = 128
VMEM_LIMIT = 56 * 1024 * 1024

FFN_TM = 512
FFN_TF = 256
WIDE_CHUNK_ROWS = 128
SQUARE_CHUNK_ROWS = 256
MIX_TM = 4096
PROJ_TM = 2048
PROJ_SUB = 256
MIX_TN = 512
ATT_TQ = 2048
ATT_SUB = 512
ATT_SHORT_SEQ = 256
HGRN_C = 64
HGRN_ROWS = 1024
HGRN_SHORT_HEADS = 4
HGRN_SAFE_DECAY = 75.0
MOD_ROWS = 8
MOD_TN = 2304

W_Q, W_K, W_V, W_HQ, W_HFF, W_HFB, W_HI, W_HG, W_GA0, W_GA1, W_GH0, W_GH1 = range(12)


def _params(sem):
    return pltpu.CompilerParams(dimension_semantics=sem, vmem_limit_bytes=VMEM_LIMIT)


def _nt_dot(a, b):
    return lax.dot_general(a, b, (((1,), (1,)), ((), ())), preferred_element_type=F32)


def _tn_dot(a, b):
    return lax.dot_general(a, b, (((0,), (0,)), ((), ())), preferred_element_type=F32)


def _layer_norm(y, g, b):
    mu = jnp.mean(y, axis=-1, keepdims=True)
    yc = y - mu
    var = jnp.mean(yc * yc, axis=-1, keepdims=True)
    return yc * lax.rsqrt(var + LN_EPS) * g + b


def _mod_kernel(c_ref, w_ref, b_ref, o_ref):
    c = c_ref[...]
    a = (c * jax.nn.sigmoid(c)).astype(BF16)
    o_ref[...] = jnp.dot(a, w_ref[...].astype(BF16), preferred_element_type=F32) + b_ref[...]


def _modulation(cond, w_ada, b_ada):
    n_out = w_ada.shape[1]
    tn = MOD_TN
    out = pl.pallas_call(
        _mod_kernel,
        grid=(n_out // tn,),
        in_specs=[
            pl.BlockSpec((MOD_ROWS, D_MODEL), lambda n: (0, 0)),
            pl.BlockSpec((D_MODEL, tn), lambda n: (0, n)),
            pl.BlockSpec((1, tn), lambda n: (0, n)),
        ],
        out_specs=pl.BlockSpec((MOD_ROWS, tn), lambda n: (0, n)),
        out_shape=jax.ShapeDtypeStruct((MOD_ROWS, n_out), F32),
        compiler_params=_params(("arbitrary",)),
        name="modulation",
    )(cond, w_ada, b_ada.reshape(1, n_out))
    return out.reshape(MOD_ROWS, 9, D_MODEL)


def _load_weight(w_hbm, w_scr, stage, sems, src_col=0, dst_col=0):
    _, chunk, width = stage.shape
    n_chunks = w_hbm.shape[0] // chunk

    def copy(c):
        return pltpu.make_async_copy(w_hbm.at[pl.ds(c * chunk, chunk), pl.ds(src_col, width)],
                                     stage.at[c % 2], sems.at[c % 2])

    copy(0).start()
    for c in range(n_chunks):
        if c + 1 < n_chunks:
            copy(c + 1).start()
        copy(c).wait()
        w_scr[c * chunk:(c + 1) * chunk, dst_col:dst_col + width] = stage[c % 2].astype(BF16)


def _modulate(x, mod_ref, sub):
    return x * (1.0 + mod_ref[0, 3 * sub + 1:3 * sub + 2, :]) + mod_ref[0, 3 * sub:3 * sub + 1, :]


def _half_step(x, mod_ref, win_ref, wout_ref, lng_ref, lnb_ref, act_scr, sub):
    h = _modulate(x, mod_ref, sub).astype(BF16)
    for c in range(D_FF // FFN_TF):
        cols = slice(c * FFN_TF, (c + 1) * FFN_TF)
        up_cols = slice(D_FF + c * FFN_TF, D_FF + (c + 1) * FFN_TF)
        gate = jnp.dot(h, win_ref[:, cols], preferred_element_type=F32)
        up = jnp.dot(h, win_ref[:, up_cols], preferred_element_type=F32)
        act_scr[:, cols] = (gate * jax.nn.sigmoid(gate) * up).astype(BF16)
    f = jnp.dot(act_scr[...], wout_ref[...], preferred_element_type=F32)
    g = mod_ref[0, 3 * sub + 2:3 * sub + 3, :]
    return _layer_norm(ALPHA * x + 0.5 * g * f, lng_ref[sub:sub + 1, :], lnb_ref[sub:sub + 1, :])


def _ffn_weight_scratch():
    return [
        pltpu.VMEM((D_MODEL, 2 * D_FF), BF16),
        pltpu.VMEM((D_FF, D_MODEL), BF16),
        pltpu.VMEM((2, WIDE_CHUNK_ROWS, 2 * D_FF), F32),
        pltpu.VMEM((2, SQUARE_CHUNK_ROWS, D_MODEL), F32),
        pltpu.SemaphoreType.DMA((2,)),
        pltpu.SemaphoreType.DMA((2,)),
    ]


def _ffn1_kernel(xa_ref, xb_ref, mod_ref, win_hbm, wout_hbm, lng_ref, lnb_ref, xo_ref, ho_ref,
                 win_scr, wout_scr, wide_stage, square_stage, wide_sems, square_sems, act_scr, *, tiles_a):
    i = pl.program_id(0)

    @pl.when(i == 0)
    def _():
        _load_weight(win_hbm, win_scr, wide_stage, wide_sems)
        _load_weight(wout_hbm, wout_scr, square_stage, square_sems)

    x = jnp.where(i < tiles_a, xa_ref[...], xb_ref[...])
    xn = _half_step(x, mod_ref, win_scr, wout_scr, lng_ref, lnb_ref, act_scr, 0)
    xo_ref[...] = xn
    ho_ref[...] = _modulate(xn, mod_ref, 1).astype(BF16)


def _ffn1(xa, xb, mod, cond_of_tile, w_in, w_out, ln_g, ln_b):
    tiles_a, tiles_b = xa.shape[0] // FFN_TM, xb.shape[0] // FFN_TM
    m = xa.shape[0] + xb.shape[0]
    row = lambda i: (i, 0)
    const = lambda i: (0, 0)
    hbm = pl.BlockSpec(memory_space=pl.ANY)
    return pl.pallas_call(
        functools.partial(_ffn1_kernel, tiles_a=tiles_a),
        grid=(tiles_a + tiles_b,),
        in_specs=[
            pl.BlockSpec((FFN_TM, D_MODEL), lambda i: (jnp.minimum(i, tiles_a - 1), 0)),
            pl.BlockSpec((FFN_TM, D_MODEL), lambda i: (jnp.maximum(i - tiles_a, 0), 0)),
            pl.BlockSpec((1, 9, D_MODEL), lambda i: (cond_of_tile(i, FFN_TM), 0, 0)),
            hbm, hbm,
            pl.BlockSpec((3, D_MODEL), const),
            pl.BlockSpec((3, D_MODEL), const),
        ],
        out_specs=[pl.BlockSpec((FFN_TM, D_MODEL), row), pl.BlockSpec((FFN_TM, D_MODEL), row)],
        out_shape=[jax.ShapeDtypeStruct((m, D_MODEL), F32), jax.ShapeDtypeStruct((m, D_MODEL), BF16)],
        scratch_shapes=_ffn_weight_scratch() + [pltpu.VMEM((FFN_TM, D_FF), BF16)],
        compiler_params=_params(("arbitrary",)),
        name="ffn1",
    )(xa, xb, mod, w_in, w_out, ln_g, ln_b)


def _rope(x, cos, sin):
    half, quarter = DH_A // 2, DH_A // 4
    lane = lax.broadcasted_iota(jnp.int32, (1, LANES), 1)
    first_quarter = (lane % half) < quarter
    outs = []
    for hb in range(QK_A // LANES):
        xb = x[:, hb * LANES:(hb + 1) * LANES]
        ahead = pltpu.roll(xb, LANES - quarter, 1)
        behind = pltpu.roll(xb, quarter, 1)
        rot = jnp.where(first_quarter, -ahead, behind)
        outs.append(xb * cos + rot * sin)
    return jnp.concatenate(outs, axis=-1)


def _forget_gate(z, logits):
    mx = jnp.max(logits, axis=0, keepdims=True)
    el = jnp.exp(logits - mx)
    lb = el[0:1] / jnp.sum(el, axis=0, keepdims=True)
    sig = 0.5 + 0.5 * jnp.tanh(0.5 * z)
    u = (1.0 - lb) * sig
    f = lb + u
    return jnp.where(f > 0.0, jnp.log(f), z), (1.0 - lb) - u


def _proj_kernel(*refs, rope, want_kv32):
    refs = list(refs)
    h_ref, w_ref, lbl_ref = refs[:3]
    refs = refs[3:]
    cos_ref, sin_ref = (refs.pop(0), refs.pop(0)) if rope else (None, None)
    qkv_ref = refs.pop(0)
    kt_ref, v32_ref = (refs.pop(0), refs.pop(0)) if want_kv32 else (None, None)
    plain_ref, g_ref, kd_ref = refs
    n = pl.program_id(1)

    pieces = [slice(r, r + PROJ_SUB) for r in range(0, h_ref.shape[0], PROJ_SUB)]

    def project(rows, w):
        return jnp.dot(h_ref[rows, :], w, preferred_element_type=F32)

    def store_heads(ref, rows, y):
        for hh in range(MIX_TN // LANES):
            ref[hh, rows, :] = y[:, hh * LANES:(hh + 1) * LANES].astype(ref.dtype)

    def qkv_step(block):
        w = w_ref[...].astype(BF16)
        for p, rows in enumerate(pieces):
            y = project(rows, w)
            if block == W_K:
                for hh in range(MIX_TN // LANES):
                    kt_ref[p, hh] = y[:, hh * LANES:(hh + 1) * LANES].T
            if block == W_V:
                v32_ref[rows] = y.reshape(PROJ_SUB, MIX_TN // LANES, LANES)
            if rope:
                first = rows.start % cos_ref.shape[1]
                table_rows = slice(first, first + PROJ_SUB)
                y = _rope(y, cos_ref[0, table_rows, :], sin_ref[0, table_rows, :])
            elif block is None:
                y = y * jnp.where(n == W_Q, Q_SCALE, 1.0)
            elif block == W_Q:
                y = y * Q_SCALE
            store_heads(qkv_ref, rows, y)

    if want_kv32:
        for block in (W_Q, W_K, W_V):
            pl.when(n == block)(functools.partial(qkv_step, block))
    else:
        pl.when(n <= W_V)(functools.partial(qkv_step, None))

    @pl.when((n == W_HFF) | (n == W_HFB))
    def _():
        w = w_ref[...].astype(BF16)
        for rows in pieces:
            log_f, kd = _forget_gate(project(rows, w), lbl_ref[0])
            store_heads(g_ref, rows, log_f)
            store_heads(kd_ref, rows, kd)

    @pl.when((n == W_HQ) | (n >= W_HI))
    def _():
        w = w_ref[...].astype(BF16)
        for rows in pieces:
            store_heads(plain_ref, rows, project(rows, w))


def _project(h, group, w, lb_logits, rope_tables, want_kv32):
    m = MIX_TM
    tm = PROJ_TM
    tiles = m // tm
    rope = rope_tables is not None
    heads = MIX_TN // LANES
    n_blocks = W_HG + 1
    in_specs = [pl.BlockSpec((tm, D_MODEL), lambda i, n: (group * tiles + i, 0)),
                pl.BlockSpec((D_MODEL, MIX_TN), lambda i, n: (0, n)),
                pl.BlockSpec((1, DEPTH + 1, QK_H), lambda i, n: (jnp.clip(n - W_HFF, 0, 1), 0, 0))]
    args = [h, w, lb_logits]
    qkv_col = lambda n: jnp.minimum(n, W_V)
    if rope:
        t_rope = rope_tables[0].shape[1]
        rope_rows = min(t_rope, tm)
        n_rope_tiles = t_rope // rope_rows
        for t in rope_tables:
            in_specs.append(pl.BlockSpec((1, rope_rows, DV_A), lambda i, n: (qkv_col(n), i % n_rope_tiles, 0)))
            args.append(t)
    head_block = (heads, tm, LANES)
    plain_col = lambda n: jnp.where(n < W_HI, 0, n - W_HI + 1)
    gate_col = lambda n: jnp.clip(n - W_HFF, 0, 1)
    out_specs = [pl.BlockSpec(head_block, lambda i, n: (qkv_col(n), i, 0))]
    out_shape = [jax.ShapeDtypeStruct((3 * heads, m, LANES), BF16)]
    if want_kv32:
        requests = m // PROJ_SUB
        out_specs += [pl.BlockSpec((tm // PROJ_SUB, heads, LANES, PROJ_SUB), lambda i, n: (i, 0, 0, 0)),
                      pl.BlockSpec((tm, heads, LANES), lambda i, n: (i, 0, 0))]
        out_shape += [jax.ShapeDtypeStruct((requests, heads, LANES, PROJ_SUB), F32),
                      jax.ShapeDtypeStruct((m, heads, LANES), F32)]
    out_specs += [pl.BlockSpec(head_block, lambda i, n: (plain_col(n), i, 0)),
                  pl.BlockSpec(head_block, lambda i, n: (gate_col(n), i, 0)),
                  pl.BlockSpec(head_block, lambda i, n: (gate_col(n), i, 0))]
    out_shape += [jax.ShapeDtypeStruct((3 * heads, m, LANES), BF16),
                  jax.ShapeDtypeStruct((2 * heads, m, LANES), F32),
                  jax.ShapeDtypeStruct((2 * heads, m, LANES), BF16)]
    return pl.pallas_call(
        functools.partial(_proj_kernel, rope=rope, want_kv32=want_kv32),
        grid=(tiles, n_blocks),
        in_specs=in_specs,
        out_specs=out_specs,
        out_shape=out_shape,
        compiler_params=_params(("arbitrary", "arbitrary")),
        name="mix_proj",
    )(*args)


def _attn_kernel(*refs, lam_init, has_ctx):
    if has_ctx:
        lam_ref, g_ref, q_ref, k_ref, v_ref, kc_ref, vc_ref, o_ref = refs
    else:
        lam_ref, g_ref, q_ref, k_ref, v_ref, o_ref = refs
    lp = lam_ref[...]
    lam = (jnp.exp(jnp.sum(lp[0:1] * lp[1:2], axis=-1, keepdims=True))
           - jnp.exp(jnp.sum(lp[2:3] * lp[3:4], axis=-1, keepdims=True)) + lam_init)

    lane = lax.broadcasted_iota(jnp.int32, (1, DV_A), 1)
    comp0 = lane < DH_A
    tq = q_ref.shape[2]
    sub = min(ATT_SUB, tq)
    problems = []
    for hh in range(q_ref.shape[0]):
        keys = [k_ref[hh, 0]]
        vals = [v_ref[hh, 0]]
        if has_ctx:
            keys.append(kc_ref[0, :, hh * DV_A:(hh + 1) * DV_A].astype(BF16))
            vals.append(vc_ref[0, :, hh * DV_A:(hh + 1) * DV_A].astype(BF16))
        for r in range(tq // sub):
            q = q_ref[hh, 0, r * sub:(r + 1) * sub, :]
            zero = jnp.zeros_like(q)
            for qc in (jnp.where(comp0, q, zero), jnp.where(comp0, zero, q)):
                problems.append((qc, keys, vals))
    scores = [[_nt_dot(kk, qc) for kk in keys] for qc, keys, _ in problems]
    weights = []
    for ss in scores:
        mx = functools.reduce(jnp.maximum, [jnp.max(s, axis=0, keepdims=True) for s in ss])
        es = [jnp.exp2(s - mx) for s in ss]
        den = functools.reduce(jnp.add, [jnp.sum(e, axis=0, keepdims=True) for e in es])
        weights.append(([e.astype(BF16) for e in es], 1.0 / den))
    maps = [functools.reduce(jnp.add, [_tn_dot(vv, e) for e, vv in zip(es, vals)]) * r
            for (es, r), (_, _, vals) in zip(weights, problems)]
    idx = 0
    for hh in range(q_ref.shape[0]):
        for r in range(tq // sub):
            o = maps[idx] - lam * maps[idx + 1]
            idx += 2
            o = o * lax.rsqrt(jnp.mean(o * o, axis=0, keepdims=True) + RMS_EPS)
            o = o.T * g_ref[...]
            o_ref[hh, 0, r * sub:(r + 1) * sub, :] = (o * (1.0 - lam_init)).astype(o_ref.dtype)


def _attention(q, k, v, ctx_kv, lam_params, subln_g, lam_init):
    (q_arr, q_row), (k_arr, k_row), (v_arr, v_row) = q, k, v
    _, bsz, t, _ = q_arr.shape
    tq = min(ATT_TQ, t)
    hp = N_HEADS_A if t <= ATT_SHORT_SEQ else 1
    const = lambda b, h, i: (0, 0)
    rows = lambda first: (lambda b, h, i: (first // hp + h, b, 0, 0))
    in_specs = [
        pl.BlockSpec((4, DH_A), const),
        pl.BlockSpec((1, DV_A), const),
        pl.BlockSpec((hp, 1, tq, DV_A), lambda b, h, i: (q_row // hp + h, b, i, 0)),
        pl.BlockSpec((hp, 1, t, DV_A), rows(k_row)),
        pl.BlockSpec((hp, 1, t, DV_A), rows(v_row)),
    ]
    args = [lam_params, subln_g, q_arr, k_arr, v_arr]
    if ctx_kv is not None:
        p = ctx_kv[0].shape[1]
        head_all = lambda b, h, i: (b, 0, h)
        in_specs += [pl.BlockSpec((1, p, hp * DV_A), head_all), pl.BlockSpec((1, p, hp * DV_A), head_all)]
        args += list(ctx_kv)
    return pl.pallas_call(
        functools.partial(_attn_kernel, lam_init=lam_init, has_ctx=ctx_kv is not None),
        grid=(bsz, N_HEADS_A // hp, t // tq),
        in_specs=in_specs,
        out_specs=pl.BlockSpec((hp, 1, tq, DV_A), lambda b, h, i: (h, b, i, 0)),
        out_shape=jax.ShapeDtypeStruct((N_HEADS_A, bsz, t, DV_A), BF16),
        compiler_params=_params(("parallel", "parallel", "arbitrary")),
        name="attention",
    )(*args)


def _chunk_scan(g, reverse):
    r = g.shape[0]
    pos = lax.broadcasted_iota(jnp.int32, (r, 1), 0) % HGRN_C
    b = g
    sh = 1
    while sh < HGRN_C:
        if reverse:
            b = b + jnp.where(pos < HGRN_C - sh, pltpu.roll(b, r - sh, 0), 0.0)
        else:
            b = b + jnp.where(pos >= sh, pltpu.roll(b, sh, 0), 0.0)
        sh *= 2
    return b


def _hgrn_kernel(*refs, seq_len, has_s0, want_state):
    refs = list(refs)
    hq_ref, hi_ref, hg_ref, gf_ref, gb_ref, kf_ref, kb_ref, ng_ref = refs[:8]
    refs = refs[8:]
    s0_ref = refs.pop(0) if has_s0 else None
    o_ref = refs.pop(0)
    sfin_ref = refs.pop(0) if want_state else None
    od_scr, st_scr, b_scr, q_scr, k_scr, ok_scr = refs
    hp = hq_ref.shape[0]
    c = HGRN_C
    rb = min(HGRN_ROWS, seq_len)
    nb = seq_len // rb
    cpb = rb // c
    g_refs = (gf_ref, gb_ref)
    k_refs = (kf_ref, kb_ref)

    def queries(hq):
        hq = hq.astype(F32)
        return hq * jax.nn.sigmoid(hq)

    def flags(i, carry):
        rows = pl.ds(pl.multiple_of(i * rb, rb), rb)
        for d in range(2):
            worst = jnp.zeros((1, DK_H), F32)
            for hh in range(hp):
                g = g_refs[d][hh, 0, rows, :]
                for j in range(cpb):
                    worst = jnp.maximum(worst, -jnp.sum(g[j * c:(j + 1) * c], axis=0, keepdims=True))
            ok_scr[d, i] = (jnp.max(worst) <= HGRN_SAFE_DECAY).astype(jnp.int32)
        return carry

    lax.fori_loop(0, nb, flags, 0)

    for hh in range(hp):
        for d in range(2):
            if has_s0:
                st_scr[hh, d] = s0_ref[0, d, hh].T
            else:
                st_scr[hh, d] = jnp.zeros((DV_H, DK_H), F32)

    row = lax.broadcasted_iota(jnp.int32, (c, 1), 0)
    col = lax.broadcasted_iota(jnp.int32, (1, c), 1)

    def fast_block(d, r0):
        reverse = d == 1
        rows = pl.ds(r0, rb)
        n = hp * cpb
        flat = lambda a: a.reshape(hp * rb, a.shape[-1])
        chunks = lambda a: a.reshape(n, c, a.shape[-1])
        vb = chunks(hi_ref[:, 0, rows, :])
        b = _chunk_scan(flat(g_refs[d][:, 0, rows, :]), reverse)
        qd = chunks((queries(flat(hq_ref[:, 0, rows, :])) * jnp.exp(b)).astype(BF16))
        ku = chunks(flat(k_refs[d][:, 0, rows, :]).astype(F32) * jnp.exp(-b))
        b = chunks(b)
        e_end = jnp.exp(b[:, 0:1, :] if reverse else b[:, c - 1:c, :])
        causal = ((row <= col) if reverse else (row >= col))[None]
        a = jnp.einsum('nck,nsk->ncs', qd, ku.astype(BF16), preferred_element_type=F32)
        o = jnp.einsum('ncs,nsv->ncv', jnp.where(causal, a, 0.0).astype(BF16), vb, preferred_element_type=F32)
        u = jnp.einsum('nsv,nsk->nvk', vb, (ku * e_end).astype(BF16), preferred_element_type=F32)
        entering = [None] * n
        for hh in range(hp):
            st = st_scr[hh, d]
            for j in (range(cpb - 1, -1, -1) if reverse else range(cpb)):
                entering[hh * cpb + j] = st.astype(BF16)
                st = e_end[hh * cpb + j] * st + u[hh * cpb + j]
            st_scr[hh, d] = st
        o = o + jnp.einsum('nck,nvk->ncv', qd, jnp.stack(entering), preferred_element_type=F32)
        od_scr[:, d, rows, :] = o.reshape(hp, rb, DV_H)

    def exact_chunk(hh, d, r0):
        reverse = d == 1
        rows = pl.ds(r0, c)
        b = _chunk_scan(g_refs[d][hh, 0, rows, :], reverse)
        b_scr[...] = b
        q_scr[...] = queries(hq_ref[hh, 0, rows, :])
        k_scr[...] = k_refs[d][hh, 0, rows, :].astype(F32)

        def one_row(t, carry):
            bt = b_scr[pl.ds(t, 1), :]
            qt = q_scr[pl.ds(t, 1), :]
            seen = (row >= t) if reverse else (row <= t)
            w = jnp.exp(jnp.where(seen, bt - b_scr[...], -jnp.inf))
            p = jnp.sum(qt * k_scr[...] * w, axis=-1, keepdims=True)
            od_scr[hh, d, pl.ds(r0 + t, 1), :] = jnp.sum(p * hi_ref[hh, 0, rows, :].astype(F32),
                                                         axis=0, keepdims=True)
            return carry

        lax.fori_loop(0, c, one_row, 0)
        b_end = b[0:1] if reverse else b[c - 1:c]
        st = st_scr[hh, d]
        qd = (q_scr[...] * jnp.exp(b)).astype(BF16)
        od_scr[hh, d, rows, :] = od_scr[hh, d, rows, :] + _nt_dot(qd, st.astype(BF16))
        k_end = (k_scr[...] * jnp.exp(b_end - b)).astype(BF16)
        st_scr[hh, d] = jnp.exp(b_end) * st + _tn_dot(hi_ref[hh, 0, rows, :], k_end)

    def block(i, carry):
        starts = (pl.multiple_of(i * rb, rb), pl.multiple_of((nb - 1 - i) * rb, rb))
        mild = (ok_scr[0, i] + ok_scr[1, nb - 1 - i]) == 2

        @pl.when(mild)
        def _():
            for d in range(2):
                fast_block(d, starts[d])

        @pl.when(jnp.logical_not(mild))
        def _():
            for hh in range(hp):
                for d in range(2):
                    for j in (range(cpb - 1, -1, -1) if d == 1 else range(cpb)):
                        exact_chunk(hh, d, pl.multiple_of(starts[d] + j * c, c))

        return carry

    lax.fori_loop(0, nb, block, 0)

    def finish(i, carry):
        rows = pl.ds(pl.multiple_of(i * rb, rb), rb)
        for hh in range(hp):
            o = od_scr[hh, 0, rows, :] + od_scr[hh, 1, rows, :]
            o = o * lax.rsqrt(jnp.mean(o * o, axis=-1, keepdims=True) + RMS_EPS) * ng_ref[...]
            hg = hg_ref[hh, 0, rows, :].astype(F32)
            o_ref[hh, 0, rows, :] = (o * (hg * jax.nn.sigmoid(hg))).astype(o_ref.dtype)
        return carry

    lax.fori_loop(0, nb, finish, 0)
    if want_state:
        for hh in range(hp):
            for d in range(2):
                sfin_ref[0, d, hh] = st_scr[hh, d].T


def _hgrn(plain, g, kd, norm_g, s0, want_state):
    _, bsz, t, _ = plain.shape
    hp = HGRN_SHORT_HEADS if t <= HGRN_ROWS else 1
    stream = lambda s: pl.BlockSpec((hp, 1, t, DK_H), lambda b, h: (s * N_HEADS_H // hp + h, b, 0, 0))
    in_specs = [stream(0), stream(1), stream(2), stream(0), stream(1), stream(0), stream(1),
                pl.BlockSpec((1, DV_H), lambda b, h: (0, 0))]
    args = [plain, plain, plain, g, g, kd, kd, norm_g]
    state_spec = pl.BlockSpec((1, 2, hp, DK_H, DV_H), lambda b, h: (b, 0, h, 0, 0))
    if s0 is not None:
        in_specs.append(state_spec)
        args.append(s0)
    out_specs = [pl.BlockSpec((hp, 1, t, DV_H), lambda b, h: (h, b, 0, 0))]
    out_shape = [jax.ShapeDtypeStruct((N_HEADS_H, bsz, t, DV_H), BF16)]
    if want_state:
        out_specs.append(state_spec)
        out_shape.append(jax.ShapeDtypeStruct((bsz, 2, N_HEADS_H, DK_H, DV_H), F32))
    n_blocks = t // min(HGRN_ROWS, t)
    chunk = pltpu.VMEM((HGRN_C, DK_H), F32)
    outs = pl.pallas_call(
        functools.partial(_hgrn_kernel, seq_len=t, has_s0=s0 is not None, want_state=want_state),
        grid=(bsz, N_HEADS_H // hp),
        in_specs=in_specs,
        out_specs=out_specs,
        out_shape=out_shape,
        scratch_shapes=[
            pltpu.VMEM((hp, 2, t, DV_H), F32),
            pltpu.VMEM((hp, 2, DV_H, DK_H), F32),
            chunk, chunk, chunk,
            pltpu.SMEM((2, n_blocks), jnp.int32),
        ],
        compiler_params=_params(("parallel", "parallel")),
        name="hgrn",
    )(*args)
    return outs if want_state else (outs[0], None)


def _merge_ffn2_kernel(oa0_ref, oh0_ref, oa1_ref, oh1_ref, h_ref, x_ref, mod_ref,
                       wmix_hbm, wa_hbm, wh_hbm, wo_hbm, win_hbm, wout_hbm, lng_ref, lnb_ref, y0_ref, y1_ref,
                       wg_scr, wa_scr, wh_scr, wo_scr,
                       win_scr, wout_scr, wide_stage, square_stage, wide_sems, square_sems, act_scr, *, tiles_0):
    i = pl.program_id(0)
    first_group = i < tiles_0

    @pl.when(i == 0)
    def _():
        _load_weight(win_hbm, win_scr, wide_stage, wide_sems)
        for half in range(2):
            _load_weight(wmix_hbm, wg_scr, square_stage, square_sems,
                         src_col=W_GA0 * MIX_TN + half * D_MODEL, dst_col=half * D_MODEL)
        for w_hbm, w_scr in ((wa_hbm, wa_scr), (wh_hbm, wh_scr), (wo_hbm, wo_scr), (wout_hbm, wout_scr)):
            _load_weight(w_hbm, w_scr, square_stage, square_sems)

    def heads(ref0, ref1):
        pick = lambda hh: jnp.where(first_group, ref0[hh], ref1[hh])
        return jnp.concatenate([pick(hh) for hh in range(ref0.shape[0])], axis=-1)

    gates = jax.nn.sigmoid(jnp.dot(h_ref[...], wg_scr[...], preferred_element_type=F32))
    ba = jnp.dot(heads(oa0_ref, oa1_ref), wa_scr[...], preferred_element_type=F32)
    bh = jnp.dot(heads(oh0_ref, oh1_ref), wh_scr[...], preferred_element_type=F32)
    merged = gates[:, :D_MODEL] * ba + gates[:, D_MODEL:] * bh
    mix = jnp.dot(merged.astype(BF16), wo_scr[...], preferred_element_type=F32)
    x2 = _layer_norm(ALPHA * x_ref[...] + mod_ref[0, 5:6, :] * mix, lng_ref[1:2, :], lnb_ref[1:2, :])
    y = _half_step(x2, mod_ref, win_scr, wout_scr, lng_ref, lnb_ref, act_scr, 2)

    @pl.when(first_group)
    def _():
        y0_ref[...] = y

    @pl.when(jnp.logical_not(first_group))
    def _():
        y1_ref[...] = y


def _merge_ffn2(mixed, h, x, mod, cond_of_tile, w_mix, w_a, w_h, w_o, w_in, w_out, ln_g, ln_b):
    (oa0, oh0), (oa1, oh1) = mixed
    tiles_0, tiles_1 = oa0.shape[1] // FFN_TM, oa1.shape[1] // FFN_TM
    row = lambda i: (i, 0)
    const = lambda i: (0, 0)
    in_group_0 = lambda i: jnp.minimum(i, tiles_0 - 1)
    in_group_1 = lambda i: jnp.maximum(i - tiles_0, 0)
    hbm = pl.BlockSpec(memory_space=pl.ANY)
    return pl.pallas_call(
        functools.partial(_merge_ffn2_kernel, tiles_0=tiles_0),
        grid=(tiles_0 + tiles_1,),
        in_specs=[
            pl.BlockSpec((N_HEADS_A, FFN_TM, DV_A), lambda i: (0, in_group_0(i), 0)),
            pl.BlockSpec((N_HEADS_H, FFN_TM, DV_H), lambda i: (0, in_group_0(i), 0)),
            pl.BlockSpec((N_HEADS_A, FFN_TM, DV_A), lambda i: (0, in_group_1(i), 0)),
            pl.BlockSpec((N_HEADS_H, FFN_TM, DV_H), lambda i: (0, in_group_1(i), 0)),
            pl.BlockSpec((FFN_TM, D_MODEL), row),
            pl.BlockSpec((FFN_TM, D_MODEL), row),
            pl.BlockSpec((1, 9, D_MODEL), lambda i: (cond_of_tile(i, FFN_TM), 0, 0)),
            hbm, hbm, hbm, hbm, hbm, hbm,
            pl.BlockSpec((3, D_MODEL), const),
            pl.BlockSpec((3, D_MODEL), const),
        ],
        out_specs=[pl.BlockSpec((FFN_TM, D_MODEL), lambda i: (in_group_0(i), 0)),
                   pl.BlockSpec((FFN_TM, D_MODEL), lambda i: (in_group_1(i), 0))],
        out_shape=[jax.ShapeDtypeStruct((tiles_0 * FFN_TM, D_MODEL), F32),
                   jax.ShapeDtypeStruct((tiles_1 * FFN_TM, D_MODEL), F32)],
        scratch_shapes=[pltpu.VMEM((D_MODEL, 2 * D_MODEL), BF16), pltpu.VMEM((W_A, D_MODEL), BF16),
                        pltpu.VMEM((W_H, D_MODEL), BF16), pltpu.VMEM((D_MODEL, D_MODEL), BF16)]
                       + _ffn_weight_scratch() + [pltpu.VMEM((FFN_TM, D_FF), BF16)],
        compiler_params=_params(("arbitrary",)),
        name="merge_ffn2",
    )(oa0, oh0, oa1, oh1, h, x, mod, w_mix, w_a, w_h, w_o, w_in, w_out, ln_g, ln_b)


def _rope_tables(n_tok):
    rows = n_tok // GRID_W
    row = np.repeat(np.arange(rows, dtype=np.float32), GRID_W)
    col = np.tile(np.arange(GRID_W, dtype=np.float32), rows)
    half = DH_A // 2
    inv = (ROPE_BASE ** (-np.arange(0, half, 2, dtype=np.float32) / half)).astype(np.float32)
    ar = row[:, None] * inv
    ac = col[:, None] * inv
    ang = np.concatenate([ar, ar, ac, ac] * 2, axis=-1)
    cos, sin = np.cos(ang).astype(np.float32), np.sin(ang).astype(np.float32)
    scale = np.float32(Q_SCALE)
    return (jnp.asarray(np.stack([cos * scale, cos, np.ones_like(cos)])),
            jnp.asarray(np.stack([sin * scale, sin, np.zeros_like(sin)])))


def _mixer(h1, group, wts, lam_init, bsz, ctx):
    m = MIX_TM
    t = m // bsz
    latent = ctx is not None
    per_head = lambda a: a.reshape(a.shape[0], bsz, t, a.shape[-1])
    if latent:
        qkv, plain, g, kd = _project(h1, group, wts['w_mix_in'], wts['lb_logits'], _rope_tables(t), False)
        k = v = None
        ctx_kv = (ctx[0], ctx[1])
    else:
        assert t == PROJ_SUB
        qkv, k, v, plain, g, kd = _project(h1, group, wts['w_mix_in'], wts['lb_logits'], None, True)
        ctx_kv = None
    qkv = per_head(qkv)
    q_kv = [(qkv, W_Q * N_HEADS_A), (qkv, W_K * N_HEADS_A), (qkv, W_V * N_HEADS_A)]
    o_a = _attention(*q_kv, ctx_kv, wts['lam_params'], wts['attn_subln_g'], lam_init)
    o_h, s_fin = _hgrn(per_head(plain), per_head(g), per_head(kd), wts['hgrn_norm_g'],
                       ctx[2] if latent else None, not latent)
    return (o_a.reshape(N_HEADS_A, m, DV_A), o_h.reshape(N_HEADS_H, m, DV_H)), (k, v, s_fin)


def kernel(x_prompt, x_sample, cache_k, cache_v, state_hgrn, c, c_ctx, w_ada, b_ada, ffn1_w_in, ffn1_w_out,
           w_mix_in, lambda_q1, lambda_k1, lambda_q2, lambda_k2, attn_subln_g, hgrn_lb_logits, hgrn_norm_g,
           w_branch_a, w_branch_h, w_mix_out, ffn2_w_in, ffn2_w_out, ln_g, ln_b):
    batch, seq, _ = x_prompt.shape
    dec_batch, dec_seq, _ = x_sample.shape
    past = cache_k.shape[2]
    assert DEPTH == 1 and dec_batch + 1 <= MOD_ROWS
    lam_init = 0.8 - 0.6 * math.exp(-0.3 * 0)

    cond = jnp.concatenate([c_ctx[None, :], c, jnp.zeros((MOD_ROWS - 1 - dec_batch, D_MODEL), F32)], axis=0)
    mod = _modulation(cond, w_ada[0], b_ada[0])

    wts = {
        'w_mix_in': w_mix_in[0],
        'lam_params': jnp.concatenate([lambda_q1, lambda_k1, lambda_q2, lambda_k2], axis=0),
        'attn_subln_g': attn_subln_g, 'lb_logits': hgrn_lb_logits, 'hgrn_norm_g': hgrn_norm_g,
    }
    n_ctx = batch * seq
    assert n_ctx == MIX_TM and dec_batch * dec_seq == MIX_TM

    def cond_of_tile(i, tm):
        first = i * tm
        return jnp.where(first < n_ctx, 0, 1 + (first - n_ctx) // dec_seq)

    x1, h1 = _ffn1(x_prompt.reshape(n_ctx, D_MODEL), x_sample.reshape(dec_batch * dec_seq, D_MODEL), mod,
                   cond_of_tile, ffn1_w_in[0], ffn1_w_out[0], ln_g[0], ln_b[0])
    mixed_p, (k_c, v_c, s_c) = _mixer(h1, 0, wts, lam_init, batch, None)
    ctx = (cache_k[:, 0].reshape(dec_batch, past, QK_A), cache_v[:, 0].reshape(dec_batch, past, W_A),
           state_hgrn[:, 0])
    mixed_s, _ = _mixer(h1, 1, wts, lam_init, dec_batch, ctx)
    y_p, y_s = _merge_ffn2((mixed_p, mixed_s), h1, x1, mod, cond_of_tile, w_mix_in[0],
                           w_branch_a[0], w_branch_h[0], w_mix_out[0], ffn2_w_in[0], ffn2_w_out[0],
                           ln_g[0], ln_b[0])

    return (y_p.reshape(batch, seq, D_MODEL),
            y_s.reshape(dec_batch, dec_seq, D_MODEL),
            k_c.reshape(batch, DEPTH, N_HEADS_A, 2, DH_A, seq).transpose(0, 1, 5, 2, 3, 4),
            v_c.reshape(batch, DEPTH, seq, N_HEADS_A, DV_A),
            s_c.reshape(batch, DEPTH, 2, N_HEADS_H, DK_H, DV_H))
```

```python
import functools
import math

import numpy as np
import jax
import jax.numpy as jnp
from jax import lax
from jax.experimental import pallas as pl
from jax.experimental.pallas import tpu as pltpu

D_MODEL = 1024
DEPTH = 1
GRID_W = 64
N_HEADS_A = 4
DH_A = 64
DV_A = 2 * DH_A
QK_A = N_HEADS_A * 2 * DH_A
W_A = N_HEADS_A * DV_A
ROPE_BASE = 10000.0
N_HEADS_H = 4
DK_H = 128
DV_H = 128
QK_H = N_HEADS_H * DK_H
W_H = N_HEADS_H * DV_H
D_FF = 2816
MIX_IN = 2 * QK_A + W_A + 3 * QK_H + 2 * W_H + 2 * D_MODEL
ALPHA = (2 * DEPTH) ** 0.25
LN_EPS = 1e-5
RMS_EPS = 1e-6
Q_SCALE = DH_A ** -0.5 * math.log2(math.e)

F32 = jnp.float32
BF16 = jnp.bfloat16

LANES = 128
VMEM_LIMIT = 56 * 1024 * 1024

FFN_TM = 512
FFN_TF = 256
WIDE_CHUNK_ROWS = 128
SQUARE_CHUNK_ROWS = 256
MIX_TM = 4096
PROJ_TM = 2048
PROJ_SUB = 256
MIX_TN = 512
ATT_TQ = 2048
ATT_SUB = 512
ATT_SHORT_SEQ = 256
HGRN_C = 64
HGRN_ROWS = 1024
HGRN_SHORT_HEADS = 4
HGRN_LONG_HEADS = 2
HGRN_SAFE_DECAY = 75.0
MOD_ROWS = 8
MOD_TN = 2304

W_Q, W_K, W_V, W_HQ, W_HFF, W_HFB, W_HI, W_HG, W_GA0, W_GA1, W_GH0, W_GH1 = range(12)


def _params(sem):
    return pltpu.CompilerParams(dimension_semantics=sem, vmem_limit_bytes=VMEM_LIMIT)


def _nt_dot(a, b):
    return lax.dot_general(a, b, (((1,), (1,)), ((), ())), preferred_element_type=F32)


def _tn_dot(a, b):
    return lax.dot_general(a, b, (((0,), (0,)), ((), ())), preferred_element_type=F32)


def _layer_norm(y, g, b):
    mu = jnp.mean(y, axis=-1, keepdims=True)
    yc = y - mu
    var = jnp.mean(yc * yc, axis=-1, keepdims=True)
    return yc * lax.rsqrt(var + LN_EPS) * g + b


def _mod_kernel(c_ref, w_ref, b_ref, o_ref):
    c = c_ref[...]
    a = (c * jax.nn.sigmoid(c)).astype(BF16)
    o_ref[...] = jnp.dot(a, w_ref[...].astype(BF16), preferred_element_type=F32) + b_ref[...]


def _modulation(cond, w_ada, b_ada):
    n_out = w_ada.shape[1]
    tn = MOD_TN
    out = pl.pallas_call(
        _mod_kernel,
        grid=(n_out // tn,),
        in_specs=[
            pl.BlockSpec((MOD_ROWS, D_MODEL), lambda n: (0, 0)),
            pl.BlockSpec((D_MODEL, tn), lambda n: (0, n)),
            pl.BlockSpec((1, tn), lambda n: (0, n)),
        ],
        out_specs=pl.BlockSpec((MOD_ROWS, tn), lambda n: (0, n)),
        out_shape=jax.ShapeDtypeStruct((MOD_ROWS, n_out), F32),
        compiler_params=_params(("arbitrary",)),
        name="modulation",
    )(cond, w_ada, b_ada.reshape(1, n_out))
    return out.reshape(MOD_ROWS, 9, D_MODEL)


def _load_weight(w_hbm, w_scr, stage, sems, src_col=0, dst_col=0):
    _, chunk, width = stage.shape
    n_chunks = w_hbm.shape[0] // chunk

    def copy(c):
        return pltpu.make_async_copy(w_hbm.at[pl.ds(c * chunk, chunk), pl.ds(src_col, width)],
                                     stage.at[c % 2], sems.at[c % 2])

    copy(0).start()
    for c in range(n_chunks):
        if c + 1 < n_chunks:
            copy(c + 1).start()
        copy(c).wait()
        w_scr[c * chunk:(c + 1) * chunk, dst_col:dst_col + width] = stage[c % 2].astype(BF16)


def _modulate(x, mod_ref, sub):
    return x * (1.0 + mod_ref[0, 3 * sub + 1:3 * sub + 2, :]) + mod_ref[0, 3 * sub:3 * sub + 1, :]


def _half_step(x, mod_ref, win_ref, wout_ref, lng_ref, lnb_ref, act_scr, sub):
    h = _modulate(x, mod_ref, sub).astype(BF16)
    for c in range(D_FF // FFN_TF):
        cols = slice(c * FFN_TF, (c + 1) * FFN_TF)
        up_cols = slice(D_FF + c * FFN_TF, D_FF + (c + 1) * FFN_TF)
        gate = jnp.dot(h, win_ref[:, cols], preferred_element_type=F32)
        up = jnp.dot(h, win_ref[:, up_cols], preferred_element_type=F32)
        act_scr[:, cols] = (gate * jax.nn.sigmoid(gate) * up).astype(BF16)
    f = jnp.dot(act_scr[...], wout_ref[...], preferred_element_type=F32)
    g = mod_ref[0, 3 * sub + 2:3 * sub + 3, :]
    return _layer_norm(ALPHA * x + 0.5 * g * f, lng_ref[sub:sub + 1, :], lnb_ref[sub:sub + 1, :])


def _ffn_weight_scratch():
    return [
        pltpu.VMEM((D_MODEL, 2 * D_FF), BF16),
        pltpu.VMEM((D_FF, D_MODEL), BF16),
        pltpu.VMEM((2, WIDE_CHUNK_ROWS, 2 * D_FF), F32),
        pltpu.VMEM((2, SQUARE_CHUNK_ROWS, D_MODEL), F32),
        pltpu.SemaphoreType.DMA((2,)),
        pltpu.SemaphoreType.DMA((2,)),
    ]


def _ffn1_kernel(xa_ref, xb_ref, mod_ref, win_hbm, wout_hbm, lng_ref, lnb_ref, xo_ref, ho_ref,
                 win_scr, wout_scr, wide_stage, square_stage, wide_sems, square_sems, act_scr, *, tiles_a):
    i = pl.program_id(0)

    @pl.when(i == 0)
    def _():
        _load_weight(win_hbm, win_scr, wide_stage, wide_sems)
        _load_weight(wout_hbm, wout_scr, square_stage, square_sems)

    x = jnp.where(i < tiles_a, xa_ref[...], xb_ref[...])
    xn = _half_step(x, mod_ref, win_scr, wout_scr, lng_ref, lnb_ref, act_scr, 0)
    xo_ref[...] = xn
    ho_ref[...] = _modulate(xn, mod_ref, 1).astype(BF16)


def _ffn1(xa, xb, mod, cond_of_tile, w_in, w_out, ln_g, ln_b):
    tiles_a, tiles_b = xa.shape[0] // FFN_TM, xb.shape[0] // FFN_TM
    m = xa.shape[0] + xb.shape[0]
    row = lambda i: (i, 0)
    const = lambda i: (0, 0)
    hbm = pl.BlockSpec(memory_space=pl.ANY)
    return pl.pallas_call(
        functools.partial(_ffn1_kernel, tiles_a=tiles_a),
        grid=(tiles_a + tiles_b,),
        in_specs=[
            pl.BlockSpec((FFN_TM, D_MODEL), lambda i: (jnp.minimum(i, tiles_a - 1), 0)),
            pl.BlockSpec((FFN_TM, D_MODEL), lambda i: (jnp.maximum(i - tiles_a, 0), 0)),
            pl.BlockSpec((1, 9, D_MODEL), lambda i: (cond_of_tile(i, FFN_TM), 0, 0)),
            hbm, hbm,
            pl.BlockSpec((3, D_MODEL), const),
            pl.BlockSpec((3, D_MODEL), const),
        ],
        out_specs=[pl.BlockSpec((FFN_TM, D_MODEL), row), pl.BlockSpec((FFN_TM, D_MODEL), row)],
        out_shape=[jax.ShapeDtypeStruct((m, D_MODEL), F32), jax.ShapeDtypeStruct((m, D_MODEL), BF16)],
        scratch_shapes=_ffn_weight_scratch() + [pltpu.VMEM((FFN_TM, D_FF), BF16)],
        compiler_params=_params(("arbitrary",)),
        name="ffn1",
    )(xa, xb, mod, w_in, w_out, ln_g, ln_b)


def _rope(x, cos, sin):
    half, quarter = DH_A // 2, DH_A // 4
    lane = lax.broadcasted_iota(jnp.int32, (1, LANES), 1)
    first_quarter = (lane % half) < quarter
    outs = []
    for hb in range(QK_A // LANES):
        xb = x[:, hb * LANES:(hb + 1) * LANES]
        ahead = pltpu.roll(xb, LANES - quarter, 1)
        behind = pltpu.roll(xb, quarter, 1)
        rot = jnp.where(first_quarter, -ahead, behind)
        outs.append(xb * cos + rot * sin)
    return jnp.concatenate(outs, axis=-1)


def _forget_gate(z, logits):
    mx = jnp.max(logits, axis=0, keepdims=True)
    el = jnp.exp(logits - mx)
    lb = el[0:1] / jnp.sum(el, axis=0, keepdims=True)
    sig = 0.5 + 0.5 * jnp.tanh(0.5 * z)
    u = (1.0 - lb) * sig
    f = lb + u
    return jnp.where(f > 0.0, jnp.log(f), z), (1.0 - lb) - u


def _proj_kernel(*refs, rope, want_kv32):
    refs = list(refs)
    h_ref, w_ref, lbl_ref = refs[:3]
    refs = refs[3:]
    cos_ref, sin_ref = (refs.pop(0), refs.pop(0)) if rope else (None, None)
    qkv_ref = refs.pop(0)
    kt_ref, v32_ref = (refs.pop(0), refs.pop(0)) if want_kv32 else (None, None)
    plain_ref, g_ref, kd_ref = refs
    n = pl.program_id(1)

    pieces = [slice(r, r + PROJ_SUB) for r in range(0, h_ref.shape[0], PROJ_SUB)]

    def project(rows, w):
        return jnp.dot(h_ref[rows, :], w, preferred_element_type=F32)

    def store_heads(ref, rows, y):
        for hh in range(MIX_TN // LANES):
            ref[hh, rows, :] = y[:, hh * LANES:(hh + 1) * LANES].astype(ref.dtype)

    def qkv_step(block):
        w = w_ref[...].astype(BF16)
        for p, rows in enumerate(pieces):
            y = project(rows, w)
            if block == W_K:
                for hh in range(MIX_TN // LANES):
                    kt_ref[p, hh] = y[:, hh * LANES:(hh + 1) * LANES].T
            if block == W_V:
                v32_ref[rows] = y.reshape(PROJ_SUB, MIX_TN // LANES, LANES)
            if rope:
                first = rows.start % cos_ref.shape[1]
                table_rows = slice(first, first + PROJ_SUB)
                y = _rope(y, cos_ref[0, table_rows, :], sin_ref[0, table_rows, :])
            elif block is None:
                y = y * jnp.where(n == W_Q, Q_SCALE, 1.0)
            elif block == W_Q:
                y = y * Q_SCALE
            store_heads(qkv_ref, rows, y)

    if want_kv32:
        for block in (W_Q, W_K, W_V):
            pl.when(n == block)(functools.partial(qkv_step, block))
    else:
        pl.when(n <= W_V)(functools.partial(qkv_step, None))

    @pl.when((n == W_HFF) | (n == W_HFB))
    def _():
        w = w_ref[...].astype(BF16)
        for rows in pieces:
            log_f, kd = _forget_gate(project(rows, w), lbl_ref[0])
            store_heads(g_ref, rows, log_f)
            store_heads(kd_ref, rows, kd)

    @pl.when((n == W_HQ) | (n >= W_HI))
    def _():
        w = w_ref[...].astype(BF16)
        for rows in pieces:
            store_heads(plain_ref, rows, project(rows, w))


def _project(h, group, w, lb_logits, rope_tables, want_kv32):
    m = MIX_TM
    tm = PROJ_TM
    tiles = m // tm
    rope = rope_tables is not None
    heads = MIX_TN // LANES
    n_blocks = W_HG + 1
    in_specs = [pl.BlockSpec((tm, D_MODEL), lambda i, n: (group * tiles + i, 0)),
                pl.BlockSpec((D_MODEL, MIX_TN), lambda i, n: (0, n)),
                pl.BlockSpec((1, DEPTH + 1, QK_H), lambda i, n: (jnp.clip(n - W_HFF, 0, 1), 0, 0))]
    args = [h, w, lb_logits]
    qkv_col = lambda n: jnp.minimum(n, W_V)
    if rope:
        t_rope = rope_tables[0].shape[1]
        rope_rows = min(t_rope, tm)
        n_rope_tiles = t_rope // rope_rows
        for t in rope_tables:
            in_specs.append(pl.BlockSpec((1, rope_rows, DV_A), lambda i, n: (qkv_col(n), i % n_rope_tiles, 0)))
            args.append(t)
    head_block = (heads, tm, LANES)
    plain_col = lambda n: jnp.where(n < W_HI, 0, n - W_HI + 1)
    gate_col = lambda n: jnp.clip(n - W_HFF, 0, 1)
    out_specs = [pl.BlockSpec(head_block, lambda i, n: (qkv_col(n), i, 0))]
    out_shape = [jax.ShapeDtypeStruct((3 * heads, m, LANES), BF16)]
    if want_kv32:
        requests = m // PROJ_SUB
        out_specs += [pl.BlockSpec((tm // PROJ_SUB, heads, LANES, PROJ_SUB), lambda i, n: (i, 0, 0, 0)),
                      pl.BlockSpec((tm, heads, LANES), lambda i, n: (i, 0, 0))]
        out_shape += [jax.ShapeDtypeStruct((requests, heads, LANES, PROJ_SUB), F32),
                      jax.ShapeDtypeStruct((m, heads, LANES), F32)]
    out_specs += [pl.BlockSpec(head_block, lambda i, n: (plain_col(n), i, 0)),
                  pl.BlockSpec(head_block, lambda i, n: (gate_col(n), i, 0)),
                  pl.BlockSpec(head_block, lambda i, n: (gate_col(n), i, 0))]
    out_shape += [jax.ShapeDtypeStruct((3 * heads, m, LANES), BF16),
                  jax.ShapeDtypeStruct((2 * heads, m, LANES), F32),
                  jax.ShapeDtypeStruct((2 * heads, m, LANES), BF16)]
    return pl.pallas_call(
        functools.partial(_proj_kernel, rope=rope, want_kv32=want_kv32),
        grid=(tiles, n_blocks),
        in_specs=in_specs,
        out_specs=out_specs,
        out_shape=out_shape,
        compiler_params=_params(("arbitrary", "arbitrary")),
        name="mix_proj",
    )(*args)


def _attn_kernel(*refs, lam_init, has_ctx):
    if has_ctx:
        lam_ref, g_ref, q_ref, k_ref, v_ref, kc_ref, vc_ref, o_ref = refs
    else:
        lam_ref, g_ref, q_ref, k_ref, v_ref, o_ref = refs
    lp = lam_ref[...]
    lam = (jnp.exp(jnp.sum(lp[0:1] * lp[1:2], axis=-1, keepdims=True))
           - jnp.exp(jnp.sum(lp[2:3] * lp[3:4], axis=-1, keepdims=True)) + lam_init)

    lane = lax.broadcasted_iota(jnp.int32, (1, DV_A), 1)
    comp0 = lane < DH_A
    tq = q_ref.shape[2]
    sub = min(ATT_SUB, tq)
    problems = []
    for hh in range(q_ref.shape[0]):
        keys = [k_ref[hh, 0]]
        vals = [v_ref[hh, 0]]
        if has_ctx:
            keys.append(kc_ref[0, :, hh * DV_A:(hh + 1) * DV_A].astype(BF16))
            vals.append(vc_ref[0, :, hh * DV_A:(hh + 1) * DV_A].astype(BF16))
        for r in range(tq // sub):
            q = q_ref[hh, 0, r * sub:(r + 1) * sub, :]
            zero = jnp.zeros_like(q)
            for qc in (jnp.where(comp0, q, zero), jnp.where(comp0, zero, q)):
                problems.append((qc, keys, vals))
    scores = [[_nt_dot(kk, qc) for kk in keys] for qc, keys, _ in problems]
    weights = []
    for ss in scores:
        mx = functools.reduce(jnp.maximum, [jnp.max(s, axis=0, keepdims=True) for s in ss])
        es = [jnp.exp2(s - mx) for s in ss]
        den = functools.reduce(jnp.add, [jnp.sum(e, axis=0, keepdims=True) for e in es])
        weights.append(([e.astype(BF16) for e in es], 1.0 / den))
    maps = [functools.reduce(jnp.add, [_tn_dot(vv, e) for e, vv in zip(es, vals)]) * r
            for (es, r), (_, _, vals) in zip(weights, problems)]
    idx = 0
    for hh in range(q_ref.shape[0]):
        for r in range(tq // sub):
            o = maps[idx] - lam * maps[idx + 1]
            idx += 2
            o = o * lax.rsqrt(jnp.mean(o * o, axis=0, keepdims=True) + RMS_EPS)
            o = o.T * g_ref[...]
            o_ref[hh, 0, r * sub:(r + 1) * sub, :] = (o * (1.0 - lam_init)).astype(o_ref.dtype)


def _attention(q, k, v, ctx_kv, lam_params, subln_g, lam_init):
    (q_arr, q_row), (k_arr, k_row), (v_arr, v_row) = q, k, v
    _, bsz, t, _ = q_arr.shape
    tq = min(ATT_TQ, t)
    hp = N_HEADS_A if t <= ATT_SHORT_SEQ else 1
    const = lambda b, h, i: (0, 0)
    rows = lambda first: (lambda b, h, i: (first // hp + h, b, 0, 0))
    in_specs = [
        pl.BlockSpec((4, DH_A), const),
        pl.BlockSpec((1, DV_A), const),
        pl.BlockSpec((hp, 1, tq, DV_A), lambda b, h, i: (q_row // hp + h, b, i, 0)),
        pl.BlockSpec((hp, 1, t, DV_A), rows(k_row)),
        pl.BlockSpec((hp, 1, t, DV_A), rows(v_row)),
    ]
    args = [lam_params, subln_g, q_arr, k_arr, v_arr]
    if ctx_kv is not None:
        p = ctx_kv[0].shape[1]
        head_all = lambda b, h, i: (b, 0, h)
        in_specs += [pl.BlockSpec((1, p, hp * DV_A), head_all), pl.BlockSpec((1, p, hp * DV_A), head_all)]
        args += list(ctx_kv)
    return pl.pallas_call(
        functools.partial(_attn_kernel, lam_init=lam_init, has_ctx=ctx_kv is not None),
        grid=(bsz, N_HEADS_A // hp, t // tq),
        in_specs=in_specs,
        out_specs=pl.BlockSpec((hp, 1, tq, DV_A), lambda b, h, i: (h, b, i, 0)),
        out_shape=jax.ShapeDtypeStruct((N_HEADS_A, bsz, t, DV_A), BF16),
        compiler_params=_params(("parallel", "parallel", "arbitrary")),
        name="attention",
    )(*args)


def _chunk_scan(g, reverse):
    r = g.shape[0]
    pos = lax.broadcasted_iota(jnp.int32, (r, 1), 0) % HGRN_C
    b = g
    sh = 1
    while sh < HGRN_C:
        if reverse:
            b = b + jnp.where(pos < HGRN_C - sh, pltpu.roll(b, r - sh, 0), 0.0)
        else:
            b = b + jnp.where(pos >= sh, pltpu.roll(b, sh, 0), 0.0)
        sh *= 2
    return b


def _hgrn_kernel(*refs, seq_len, has_s0, want_state):
    refs = list(refs)
    hq_ref, hi_ref, hg_ref, gf_ref, gb_ref, kf_ref, kb_ref, ng_ref = refs[:8]
    refs = refs[8:]
    s0_ref = refs.pop(0) if has_s0 else None
    o_ref = refs.pop(0)
    sfin_ref = refs.pop(0) if want_state else None
    od_scr, st_scr, b_scr, q_scr, k_scr, ok_scr = refs
    hp = hq_ref.shape[0]
    c = HGRN_C
    rb = min(HGRN_ROWS, seq_len)
    nb = seq_len // rb
    cpb = rb // c
    g_refs = (gf_ref, gb_ref)
    k_refs = (kf_ref, kb_ref)

    def queries(hq):
        hq = hq.astype(F32)
        return hq * jax.nn.sigmoid(hq)

    def flags(i, carry):
        rows = pl.ds(pl.multiple_of(i * rb, rb), rb)
        for d in range(2):
            worst = jnp.zeros((1, DK_H), F32)
            for hh in range(hp):
                g = g_refs[d][hh, 0, rows, :]
                for j in range(cpb):
                    worst = jnp.maximum(worst, -jnp.sum(g[j * c:(j + 1) * c], axis=0, keepdims=True))
            ok_scr[d, i] = (jnp.max(worst) <= HGRN_SAFE_DECAY).astype(jnp.int32)
        return carry

    lax.fori_loop(0, nb, flags, 0)

    for hh in range(hp):
        for d in range(2):
            if has_s0:
                st_scr[hh, d] = s0_ref[0, d, hh].T
            else:
                st_scr[hh, d] = jnp.zeros((DV_H, DK_H), F32)

    row = lax.broadcasted_iota(jnp.int32, (c, 1), 0)
    col = lax.broadcasted_iota(jnp.int32, (1, c), 1)

    def fast_block(d, r0):
        reverse = d == 1
        rows = pl.ds(r0, rb)
        n = hp * cpb
        flat = lambda a: a.reshape(hp * rb, a.shape[-1])
        chunks = lambda a: a.reshape(n, c, a.shape[-1])
        vb = chunks(hi_ref[:, 0, rows, :])
        b = _chunk_scan(flat(g_refs[d][:, 0, rows, :]), reverse)
        qd = chunks((queries(flat(hq_ref[:, 0, rows, :])) * jnp.exp(b)).astype(BF16))
        ku = chunks(flat(k_refs[d][:, 0, rows, :]).astype(F32) * jnp.exp(-b))
        b = chunks(b)
        e_end = jnp.exp(b[:, 0:1, :] if reverse else b[:, c - 1:c, :])
        causal = ((row <= col) if reverse else (row >= col))[None]
        a = jnp.einsum('nck,nsk->ncs', qd, ku.astype(BF16), preferred_element_type=F32)
        o = jnp.einsum('ncs,nsv->ncv', jnp.where(causal, a, 0.0).astype(BF16), vb, preferred_element_type=F32)
        u = jnp.einsum('nsv,nsk->nvk', vb, (ku * e_end).astype(BF16), preferred_element_type=F32)
        entering = [None] * n
        for hh in range(hp):
            st = st_scr[hh, d]
            for j in (range(cpb - 1, -1, -1) if reverse else range(cpb)):
                entering[hh * cpb + j] = st.astype(BF16)
                st = e_end[hh * cpb + j] * st + u[hh * cpb + j]
            st_scr[hh, d] = st
        o = o + jnp.einsum('nck,nvk->ncv', qd, jnp.stack(entering), preferred_element_type=F32)
        od_scr[:, d, rows, :] = o.reshape(hp, rb, DV_H)

    def exact_chunk(hh, d, r0):
        reverse = d == 1
        rows = pl.ds(r0, c)
        b = _chunk_scan(g_refs[d][hh, 0, rows, :], reverse)
        b_scr[...] = b
        q_scr[...] = queries(hq_ref[hh, 0, rows, :])
        k_scr[...] = k_refs[d][hh, 0, rows, :].astype(F32)

        def one_row(t, carry):
            bt = b_scr[pl.ds(t, 1), :]
            qt = q_scr[pl.ds(t, 1), :]
            seen = (row >= t) if reverse else (row <= t)
            w = jnp.exp(jnp.where(seen, bt - b_scr[...], -jnp.inf))
            p = jnp.sum(qt * k_scr[...] * w, axis=-1, keepdims=True)
            od_scr[hh, d, pl.ds(r0 + t, 1), :] = jnp.sum(p * hi_ref[hh, 0, rows, :].astype(F32),
                                                         axis=0, keepdims=True)
            return carry

        lax.fori_loop(0, c, one_row, 0)
        b_end = b[0:1] if reverse else b[c - 1:c]
        st = st_scr[hh, d]
        qd = (q_scr[...] * jnp.exp(b)).astype(BF16)
        od_scr[hh, d, rows, :] = od_scr[hh, d, rows, :] + _nt_dot(qd, st.astype(BF16))
        k_end = (k_scr[...] * jnp.exp(b_end - b)).astype(BF16)
        st_scr[hh, d] = jnp.exp(b_end) * st + _tn_dot(hi_ref[hh, 0, rows, :], k_end)

    def block(i, carry):
        starts = (pl.multiple_of(i * rb, rb), pl.multiple_of((nb - 1 - i) * rb, rb))
        mild = (ok_scr[0, i] + ok_scr[1, nb - 1 - i]) == 2

        @pl.when(mild)
        def _():
            for d in range(2):
                fast_block(d, starts[d])

        @pl.when(jnp.logical_not(mild))
        def _():
            for hh in range(hp):
                for d in range(2):
                    for j in (range(cpb - 1, -1, -1) if d == 1 else range(cpb)):
                        exact_chunk(hh, d, pl.multiple_of(starts[d] + j * c, c))

        return carry

    lax.fori_loop(0, nb, block, 0)

    def finish(i, carry):
        rows = pl.ds(pl.multiple_of(i * rb, rb), rb)
        for hh in range(hp):
            o = od_scr[hh, 0, rows, :] + od_scr[hh, 1, rows, :]
            o = o * lax.rsqrt(jnp.mean(o * o, axis=-1, keepdims=True) + RMS_EPS) * ng_ref[...]
            hg = hg_ref[hh, 0, rows, :].astype(F32)
            o_ref[hh, 0, rows, :] = (o * (hg * jax.nn.sigmoid(hg))).astype(o_ref.dtype)
        return carry

    lax.fori_loop(0, nb, finish, 0)
    if want_state:
        for hh in range(hp):
            for d in range(2):
                sfin_ref[0, d, hh] = st_scr[hh, d].T


def _hgrn(plain, g, kd, norm_g, s0, want_state):
    _, bsz, t, _ = plain.shape
    hp = HGRN_SHORT_HEADS if t <= HGRN_ROWS else HGRN_LONG_HEADS
    stream = lambda s: pl.BlockSpec((hp, 1, t, DK_H), lambda b, h: (s * N_HEADS_H // hp + h, b, 0, 0))
    in_specs = [stream(0), stream(1), stream(2), stream(0), stream(1), stream(0), stream(1),
                pl.BlockSpec((1, DV_H), lambda b, h: (0, 0))]
    args = [plain, plain, plain, g, g, kd, kd, norm_g]
    state_spec = pl.BlockSpec((1, 2, hp, DK_H, DV_H), lambda b, h: (b, 0, h, 0, 0))
    if s0 is not None:
        in_specs.append(state_spec)
        args.append(s0)
    out_specs = [pl.BlockSpec((hp, 1, t, DV_H), lambda b, h: (h, b, 0, 0))]
    out_shape = [jax.ShapeDtypeStruct((N_HEADS_H, bsz, t, DV_H), BF16)]
    if want_state:
        out_specs.append(state_spec)
        out_shape.append(jax.ShapeDtypeStruct((bsz, 2, N_HEADS_H, DK_H, DV_H), F32))
    n_blocks = t // min(HGRN_ROWS, t)
    chunk = pltpu.VMEM((HGRN_C, DK_H), F32)
    outs = pl.pallas_call(
        functools.partial(_hgrn_kernel, seq_len=t, has_s0=s0 is not None, want_state=want_state),
        grid=(bsz, N_HEADS_H // hp),
        in_specs=in_specs,
        out_specs=out_specs,
        out_shape=out_shape,
        scratch_shapes=[
            pltpu.VMEM((hp, 2, t, DV_H), F32),
            pltpu.VMEM((hp, 2, DV_H, DK_H), F32),
            chunk, chunk, chunk,
            pltpu.SMEM((2, n_blocks), jnp.int32),
        ],
        compiler_params=_params(("parallel", "parallel")),
        name="hgrn",
    )(*args)
    return outs if want_state else (outs[0], None)


def _merge_ffn2_kernel(oa0_ref, oh0_ref, oa1_ref, oh1_ref, h_ref, x_ref, mod_ref,
                       wmix_hbm, wa_hbm, wh_hbm, wo_hbm, win_hbm, wout_hbm, lng_ref, lnb_ref, y0_ref, y1_ref,
                       wg_scr, wa_scr, wh_scr, wo_scr,
                       win_scr, wout_scr, wide_stage, square_stage, wide_sems, square_sems, act_scr, *, tiles_0):
    i = pl.program_id(0)
    first_group = i < tiles_0

    @pl.when(i == 0)
    def _():
        _load_weight(win_hbm, win_scr, wide_stage, wide_sems)
        for half in range(2):
            _load_weight(wmix_hbm, wg_scr, square_stage, square_sems,
                         src_col=W_GA0 * MIX_TN + half * D_MODEL, dst_col=half * D_MODEL)
        for w_hbm, w_scr in ((wa_hbm, wa_scr), (wh_hbm, wh_scr), (wo_hbm, wo_scr), (wout_hbm, wout_scr)):
            _load_weight(w_hbm, w_scr, square_stage, square_sems)

    def heads(ref0, ref1):
        pick = lambda hh: jnp.where(first_group, ref0[hh], ref1[hh])
        return jnp.concatenate([pick(hh) for hh in range(ref0.shape[0])], axis=-1)

    gates = jax.nn.sigmoid(jnp.dot(h_ref[...], wg_scr[...], preferred_element_type=F32))
    ba = jnp.dot(heads(oa0_ref, oa1_ref), wa_scr[...], preferred_element_type=F32)
    bh = jnp.dot(heads(oh0_ref, oh1_ref), wh_scr[...], preferred_element_type=F32)
    merged = gates[:, :D_MODEL] * ba + gates[:, D_MODEL:] * bh
    mix = jnp.dot(merged.astype(BF16), wo_scr[...], preferred_element_type=F32)
    x2 = _layer_norm(ALPHA * x_ref[...] + mod_ref[0, 5:6, :] * mix, lng_ref[1:2, :], lnb_ref[1:2, :])
    y = _half_step(x2, mod_ref, win_scr, wout_scr, lng_ref, lnb_ref, act_scr, 2)

    @pl.when(first_group)
    def _():
        y0_ref[...] = y

    @pl.when(jnp.logical_not(first_group))
    def _():
        y1_ref[...] = y


def _merge_ffn2(mixed, h, x, mod, cond_of_tile, w_mix, w_a, w_h, w_o, w_in, w_out, ln_g, ln_b):
    (oa0, oh0), (oa1, oh1) = mixed
    tiles_0, tiles_1 = oa0.shape[1] // FFN_TM, oa1.shape[1] // FFN_TM
    row = lambda i: (i, 0)
    const = lambda i: (0, 0)
    in_group_0 = lambda i: jnp.minimum(i, tiles_0 - 1)
    in_group_1 = lambda i: jnp.maximum(i - tiles_0, 0)
    hbm = pl.BlockSpec(memory_space=pl.ANY)
    return pl.pallas_call(
        functools.partial(_merge_ffn2_kernel, tiles_0=tiles_0),
        grid=(tiles_0 + tiles_1,),
        in_specs=[
            pl.BlockSpec((N_HEADS_A, FFN_TM, DV_A), lambda i: (0, in_group_0(i), 0)),
            pl.BlockSpec((N_HEADS_H, FFN_TM, DV_H), lambda i: (0, in_group_0(i), 0)),
            pl.BlockSpec((N_HEADS_A, FFN_TM, DV_A), lambda i: (0, in_group_1(i), 0)),
            pl.BlockSpec((N_HEADS_H, FFN_TM, DV_H), lambda i: (0, in_group_1(i), 0)),
            pl.BlockSpec((FFN_TM, D_MODEL), row),
            pl.BlockSpec((FFN_TM, D_MODEL), row),
            pl.BlockSpec((1, 9, D_MODEL), lambda i: (cond_of_tile(i, FFN_TM), 0, 0)),
            hbm, hbm, hbm, hbm, hbm, hbm,
            pl.BlockSpec((3, D_MODEL), const),
            pl.BlockSpec((3, D_MODEL), const),
        ],
        out_specs=[pl.BlockSpec((FFN_TM, D_MODEL), lambda i: (in_group_0(i), 0)),
                   pl.BlockSpec((FFN_TM, D_MODEL), lambda i: (in_group_1(i), 0))],
        out_shape=[jax.ShapeDtypeStruct((tiles_0 * FFN_TM, D_MODEL), F32),
                   jax.ShapeDtypeStruct((tiles_1 * FFN_TM, D_MODEL), F32)],
        scratch_shapes=[pltpu.VMEM((D_MODEL, 2 * D_MODEL), BF16), pltpu.VMEM((W_A, D_MODEL), BF16),
                        pltpu.VMEM((W_H, D_MODEL), BF16), pltpu.VMEM((D_MODEL, D_MODEL), BF16)]
                       + _ffn_weight_scratch() + [pltpu.VMEM((FFN_TM, D_FF), BF16)],
        compiler_params=_params(("arbitrary",)),
        name="merge_ffn2",
    )(oa0, oh0, oa1, oh1, h, x, mod, w_mix, w_a, w_h, w_o, w_in, w_out, ln_g, ln_b)


def _rope_tables(n_tok):
    rows = n_tok // GRID_W
    row = np.repeat(np.arange(rows, dtype=np.float32), GRID_W)
    col = np.tile(np.arange(GRID_W, dtype=np.float32), rows)
    half = DH_A // 2
    inv = (ROPE_BASE ** (-np.arange(0, half, 2, dtype=np.float32) / half)).astype(np.float32)
    ar = row[:, None] * inv
    ac = col[:, None] * inv
    ang = np.concatenate([ar, ar, ac, ac] * 2, axis=-1)
    cos, sin = np.cos(ang).astype(np.float32), np.sin(ang).astype(np.float32)
    scale = np.float32(Q_SCALE)
    return (jnp.asarray(np.stack([cos * scale, cos, np.ones_like(cos)])),
            jnp.asarray(np.stack([sin * scale, sin, np.zeros_like(sin)])))


def _mixer(h1, group, wts, lam_init, bsz, ctx):
    m = MIX_TM
    t = m // bsz
    latent = ctx is not None
    per_head = lambda a: a.reshape(a.shape[0], bsz, t, a.shape[-1])
    if latent:
        qkv, plain, g, kd = _project(h1, group, wts['w_mix_in'], wts['lb_logits'], _rope_tables(t), False)
        k = v = None
        ctx_kv = (ctx[0], ctx[1])
    else:
        assert t == PROJ_SUB
        qkv, k, v, plain, g, kd = _project(h1, group, wts['w_mix_in'], wts['lb_logits'], None, True)
        ctx_kv = None
    qkv = per_head(qkv)
    q_kv = [(qkv, W_Q * N_HEADS_A), (qkv, W_K * N_HEADS_A), (qkv, W_V * N_HEADS_A)]
    o_a = _attention(*q_kv, ctx_kv, wts['lam_params'], wts['attn_subln_g'], lam_init)
    o_h, s_fin = _hgrn(per_head(plain), per_head(g), per_head(kd), wts['hgrn_norm_g'],
                       ctx[2] if latent else None, not latent)
    return (o_a.reshape(N_HEADS_A, m, DV_A), o_h.reshape(N_HEADS_H, m, DV_H)), (k, v, s_fin)


def kernel(x_prompt, x_sample, cache_k, cache_v, state_hgrn, c, c_ctx, w_ada, b_ada, ffn1_w_in, ffn1_w_out,
           w_mix_in, lambda_q1, lambda_k1, lambda_q2, lambda_k2, attn_subln_g, hgrn_lb_logits, hgrn_norm_g,
           w_branch_a, w_branch_h, w_mix_out, ffn2_w_in, ffn2_w_out, ln_g, ln_b):
    batch, seq, _ = x_prompt.shape
    dec_batch, dec_seq, _ = x_sample.shape
    past = cache_k.shape[2]
    assert DEPTH == 1 and dec_batch + 1 <= MOD_ROWS
    lam_init = 0.8 - 0.6 * math.exp(-0.3 * 0)

    cond = jnp.concatenate([c_ctx[None, :], c, jnp.zeros((MOD_ROWS - 1 - dec_batch, D_MODEL), F32)], axis=0)
    mod = _modulation(cond, w_ada[0], b_ada[0])

    wts = {
        'w_mix_in': w_mix_in[0],
        'lam_params': jnp.concatenate([lambda_q1, lambda_k1, lambda_q2, lambda_k2], axis=0),
        'attn_subln_g': attn_subln_g, 'lb_logits': hgrn_lb_logits, 'hgrn_norm_g': hgrn_norm_g,
    }
    n_ctx = batch * seq
    assert n_ctx == MIX_TM and dec_batch * dec_seq == MIX_TM

    def cond_of_tile(i, tm):
        first = i * tm
        return jnp.where(first < n_ctx, 0, 1 + (first - n_ctx) // dec_seq)

    x1, h1 = _ffn1(x_prompt.reshape(n_ctx, D_MODEL), x_sample.reshape(dec_batch * dec_seq, D_MODEL), mod,
                   cond_of_tile, ffn1_w_in[0], ffn1_w_out[0], ln_g[0], ln_b[0])
    mixed_p, (k_c, v_c, s_c) = _mixer(h1, 0, wts, lam_init, batch, None)
    ctx = (cache_k[:, 0].reshape(dec_batch, past, QK_A), cache_v[:, 0].reshape(dec_batch, past, W_A),
           state_hgrn[:, 0])
    mixed_s, _ = _mixer(h1, 1, wts, lam_init, dec_batch, ctx)
    y_p, y_s = _merge_ffn2((mixed_p, mixed_s), h1, x1, mod, cond_of_tile, w_mix_in[0],
                           w_branch_a[0], w_branch_h[0], w_mix_out[0], ffn2_w_in[0], ffn2_w_out[0],
                           ln_g[0], ln_b[0])

    return (y_p.reshape(batch, seq, D_MODEL),
            y_s.reshape(dec_batch, dec_seq, D_MODEL),
            k_c.reshape(batch, DEPTH, N_HEADS_A, 2, DH_A, seq).transpose(0, 1, 5, 2, 3, 4),
            v_c.reshape(batch, DEPTH, seq, N_HEADS_A, DV_A),
            s_c.reshape(batch, DEPTH, 2, N_HEADS_H, DK_H, DV_H))
```

```python
import functools
import math

import numpy as np
import jax
import jax.numpy as jnp
from jax import lax
from jax.experimental import pallas as pl
from jax.experimental.pallas import tpu as pltpu

D_MODEL = 1024
DEPTH = 1
GRID_W = 64
N_HEADS_A = 4
DH_A = 64
DV_A = 2 * DH_A
QK_A = N_HEADS_A * 2 * DH_A
W_A = N_HEADS_A * DV_A
ROPE_BASE = 10000.0
N_HEADS_H = 4
DK_H = 128
DV_H = 128
QK_H = N_HEADS_H * DK_H
W_H = N_HEADS_H * DV_H
D_FF = 2816
MIX_IN = 2 * QK_A + W_A + 3 * QK_H + 2 * W_H + 2 * D_MODEL
ALPHA = (2 * DEPTH) ** 0.25
LN_EPS = 1e-5
RMS_EPS = 1e-6
Q_SCALE = DH_A ** -0.5 * math.log2(math.e)

F32 = jnp.float32
BF16 = jnp.bfloat16

LANES = 128
VMEM_LIMIT = 56 * 1024 * 1024

FFN_TM = 512
FFN_TF = 256
WIDE_CHUNK_ROWS = 128
SQUARE_CHUNK_ROWS = 256
MIX_TM = 4096
PROJ_TM = 2048
PROJ_SUB = 256
MIX_TN = 512
ATT_TQ = 2048
ATT_SUB = 512
ATT_SHORT_SEQ = 256
HGRN_C = 64
HGRN_ROWS = 1024
HGRN_SHORT_HEADS = 4
HGRN_LONG_HEADS = 2
HGRN_SAFE_DECAY = 75.0
MOD_ROWS = 8
MOD_TN = 2304
MOD_SPLIT = 4

W_Q, W_K, W_V, W_HQ, W_HFF, W_HFB, W_HI, W_HG, W_GA0, W_GA1, W_GH0, W_GH1 = range(12)


def _params(sem):
    return pltpu.CompilerParams(dimension_semantics=sem, vmem_limit_bytes=VMEM_LIMIT)


def _nt_dot(a, b):
    return lax.dot_general(a, b, (((1,), (1,)), ((), ())), preferred_element_type=F32)


def _tn_dot(a, b):
    return lax.dot_general(a, b, (((0,), (0,)), ((), ())), preferred_element_type=F32)


def _layer_norm(y, g, b):
    mu = jnp.mean(y, axis=-1, keepdims=True)
    yc = y - mu
    var = jnp.mean(yc * yc, axis=-1, keepdims=True)
    return yc * lax.rsqrt(var + LN_EPS) * g + b


def _mod_kernel(c_ref, *refs):
    w_refs, b_ref, o_ref = refs[:MOD_SPLIT], refs[MOD_SPLIT], refs[MOD_SPLIT + 1]
    c = c_ref[...]
    a = (c * jax.nn.sigmoid(c)).astype(BF16)
    band = D_MODEL // MOD_SPLIT
    acc = b_ref[...]
    for s, w_ref in enumerate(w_refs):
        acc = acc + jnp.dot(a[:, s * band:(s + 1) * band], w_ref[...].astype(BF16), preferred_element_type=F32)
    o_ref[...] = acc


def _modulation(cond, w_ada, b_ada):
    n_out = w_ada.shape[1]
    tn = MOD_TN
    out = pl.pallas_call(
        _mod_kernel,
        grid=(n_out // tn,),
        in_specs=[pl.BlockSpec((MOD_ROWS, D_MODEL), lambda n: (0, 0))]
        + [pl.BlockSpec((D_MODEL // MOD_SPLIT, tn), functools.partial(lambda n, s: (s, n), s=s))
           for s in range(MOD_SPLIT)]
        + [pl.BlockSpec((1, tn), lambda n: (0, n))],
        out_specs=pl.BlockSpec((MOD_ROWS, tn), lambda n: (0, n)),
        out_shape=jax.ShapeDtypeStruct((MOD_ROWS, n_out), F32),
        compiler_params=_params(("arbitrary",)),
        name="modulation",
    )(cond, *([w_ada] * MOD_SPLIT), b_ada.reshape(1, n_out))
    return out.reshape(MOD_ROWS, 9, D_MODEL)


def _load_weight(w_hbm, w_scr, stage, sems, src_col=0, dst_col=0):
    _, chunk, width = stage.shape
    n_chunks = w_hbm.shape[0] // chunk

    def copy(c):
        return pltpu.make_async_copy(w_hbm.at[pl.ds(c * chunk, chunk), pl.ds(src_col, width)],
                                     stage.at[c % 2], sems.at[c % 2])

    copy(0).start()
    for c in range(n_chunks):
        if c + 1 < n_chunks:
            copy(c + 1).start()
        copy(c).wait()
        w_scr[c * chunk:(c + 1) * chunk, dst_col:dst_col + width] = stage[c % 2].astype(BF16)


def _modulate(x, mod_ref, sub):
    return x * (1.0 + mod_ref[0, 3 * sub + 1:3 * sub + 2, :]) + mod_ref[0, 3 * sub:3 * sub + 1, :]


def _half_step(x, mod_ref, win_ref, wout_ref, lng_ref, lnb_ref, act_scr, sub):
    h = _modulate(x, mod_ref, sub).astype(BF16)
    for c in range(D_FF // FFN_TF):
        cols = slice(c * FFN_TF, (c + 1) * FFN_TF)
        up_cols = slice(D_FF + c * FFN_TF, D_FF + (c + 1) * FFN_TF)
        gate = jnp.dot(h, win_ref[:, cols], preferred_element_type=F32)
        up = jnp.dot(h, win_ref[:, up_cols], preferred_element_type=F32)
        act_scr[:, cols] = (gate * jax.nn.sigmoid(gate) * up).astype(BF16)
    f = jnp.dot(act_scr[...], wout_ref[...], preferred_element_type=F32)
    g = mod_ref[0, 3 * sub + 2:3 * sub + 3, :]
    return _layer_norm(ALPHA * x + 0.5 * g * f, lng_ref[sub:sub + 1, :], lnb_ref[sub:sub + 1, :])


def _ffn_weight_scratch():
    return [
        pltpu.VMEM((D_MODEL, 2 * D_FF), BF16),
        pltpu.VMEM((D_FF, D_MODEL), BF16),
        pltpu.VMEM((2, WIDE_CHUNK_ROWS, 2 * D_FF), F32),
        pltpu.VMEM((2, SQUARE_CHUNK_ROWS, D_MODEL), F32),
        pltpu.SemaphoreType.DMA((2,)),
        pltpu.SemaphoreType.DMA((2,)),
    ]


def _ffn1_kernel(xa_ref, xb_ref, mod_ref, win_hbm, wout_hbm, lng_ref, lnb_ref, xo_ref, ho_ref,
                 win_scr, wout_scr, wide_stage, square_stage, wide_sems, square_sems, act_scr, *, tiles_a):
    i = pl.program_id(0)

    @pl.when(i == 0)
    def _():
        _load_weight(win_hbm, win_scr, wide_stage, wide_sems)
        _load_weight(wout_hbm, wout_scr, square_stage, square_sems)

    x = jnp.where(i < tiles_a, xa_ref[...], xb_ref[...])
    xn = _half_step(x, mod_ref, win_scr, wout_scr, lng_ref, lnb_ref, act_scr, 0)
    xo_ref[...] = xn
    ho_ref[...] = _modulate(xn, mod_ref, 1).astype(BF16)


def _ffn1(xa, xb, mod, cond_of_tile, w_in, w_out, ln_g, ln_b):
    tiles_a, tiles_b = xa.shape[0] // FFN_TM, xb.shape[0] // FFN_TM
    m = xa.shape[0] + xb.shape[0]
    row = lambda i: (i, 0)
    const = lambda i: (0, 0)
    hbm = pl.BlockSpec(memory_space=pl.ANY)
    return pl.pallas_call(
        functools.partial(_ffn1_kernel, tiles_a=tiles_a),
        grid=(tiles_a + tiles_b,),
        in_specs=[
            pl.BlockSpec((FFN_TM, D_MODEL), lambda i: (jnp.minimum(i, tiles_a - 1), 0)),
            pl.BlockSpec((FFN_TM, D_MODEL), lambda i: (jnp.maximum(i - tiles_a, 0), 0)),
            pl.BlockSpec((1, 9, D_MODEL), lambda i: (cond_of_tile(i, FFN_TM), 0, 0)),
            hbm, hbm,
            pl.BlockSpec((3, D_MODEL), const),
            pl.BlockSpec((3, D_MODEL), const),
        ],
        out_specs=[pl.BlockSpec((FFN_TM, D_MODEL), row), pl.BlockSpec((FFN_TM, D_MODEL), row)],
        out_shape=[jax.ShapeDtypeStruct((m, D_MODEL), F32), jax.ShapeDtypeStruct((m, D_MODEL), BF16)],
        scratch_shapes=_ffn_weight_scratch() + [pltpu.VMEM((FFN_TM, D_FF), BF16)],
        compiler_params=_params(("arbitrary",)),
        name="ffn1",
    )(xa, xb, mod, w_in, w_out, ln_g, ln_b)


def _rope(x, cos, sin):
    half, quarter = DH_A // 2, DH_A // 4
    lane = lax.broadcasted_iota(jnp.int32, (1, LANES), 1)
    first_quarter = (lane % half) < quarter
    outs = []
    for hb in range(QK_A // LANES):
        xb = x[:, hb * LANES:(hb + 1) * LANES]
        ahead = pltpu.roll(xb, LANES - quarter, 1)
        behind = pltpu.roll(xb, quarter, 1)
        rot = jnp.where(first_quarter, -ahead, behind)
        outs.append(xb * cos + rot * sin)
    return jnp.concatenate(outs, axis=-1)


def _forget_gate(z, logits):
    mx = jnp.max(logits, axis=0, keepdims=True)
    el = jnp.exp(logits - mx)
    lb = el[0:1] / jnp.sum(el, axis=0, keepdims=True)
    sig = 0.5 + 0.5 * jnp.tanh(0.5 * z)
    u = (1.0 - lb) * sig
    f = lb + u
    return jnp.where(f > 0.0, jnp.log(f), z), (1.0 - lb) - u


def _proj_kernel(*refs, rope, want_kv32):
    refs = list(refs)
    h_ref, w_ref, lbl_ref = refs[:3]
    refs = refs[3:]
    cos_ref, sin_ref = (refs.pop(0), refs.pop(0)) if rope else (None, None)
    qkv_ref = refs.pop(0)
    kt_ref, v32_ref = (refs.pop(0), refs.pop(0)) if want_kv32 else (None, None)
    plain_ref, g_ref, kd_ref = refs
    n = pl.program_id(1)

    pieces = [slice(r, r + PROJ_SUB) for r in range(0, h_ref.shape[0], PROJ_SUB)]

    def project(rows, w):
        return jnp.dot(h_ref[rows, :], w, preferred_element_type=F32)

    def store_heads(ref, rows, y):
        for hh in range(MIX_TN // LANES):
            ref[hh, rows, :] = y[:, hh * LANES:(hh + 1) * LANES].astype(ref.dtype)

    def qkv_step(block):
        w = w_ref[...].astype(BF16)
        for p, rows in enumerate(pieces):
            y = project(rows, w)
            if block == W_K:
                for hh in range(MIX_TN // LANES):
                    kt_ref[p, hh] = y[:, hh * LANES:(hh + 1) * LANES].T
            if block == W_V:
                v32_ref[rows] = y.reshape(PROJ_SUB, MIX_TN // LANES, LANES)
            if rope:
                first = rows.start % cos_ref.shape[1]
                table_rows = slice(first, first + PROJ_SUB)
                y = _rope(y, cos_ref[0, table_rows, :], sin_ref[0, table_rows, :])
            elif block is None:
                y = y * jnp.where(n == W_Q, Q_SCALE, 1.0)
            elif block == W_Q:
                y = y * Q_SCALE
            store_heads(qkv_ref, rows, y)

    if want_kv32:
        for block in (W_Q, W_K, W_V):
            pl.when(n == block)(functools.partial(qkv_step, block))
    else:
        pl.when(n <= W_V)(functools.partial(qkv_step, None))

    @pl.when((n == W_HFF) | (n == W_HFB))
    def _():
        w = w_ref[...].astype(BF16)
        for rows in pieces:
            log_f, kd = _forget_gate(project(rows, w), lbl_ref[0])
            store_heads(g_ref, rows, log_f)
            store_heads(kd_ref, rows, kd)

    @pl.when((n == W_HQ) | (n >= W_HI))
    def _():
        w = w_ref[...].astype(BF16)
        for rows in pieces:
            store_heads(plain_ref, rows, project(rows, w))


def _project(h, group, w, lb_logits, rope_tables, want_kv32):
    m = MIX_TM
    tm = PROJ_TM
    tiles = m // tm
    rope = rope_tables is not None
    heads = MIX_TN // LANES
    n_blocks = W_HG + 1
    in_specs = [pl.BlockSpec((tm, D_MODEL), lambda i, n: (group * tiles + i, 0)),
                pl.BlockSpec((D_MODEL, MIX_TN), lambda i, n: (0, n)),
                pl.BlockSpec((1, DEPTH + 1, QK_H), lambda i, n: (jnp.clip(n - W_HFF, 0, 1), 0, 0))]
    args = [h, w, lb_logits]
    qkv_col = lambda n: jnp.minimum(n, W_V)
    if rope:
        t_rope = rope_tables[0].shape[1]
        rope_rows = min(t_rope, tm)
        n_rope_tiles = t_rope // rope_rows
        for t in rope_tables:
            in_specs.append(pl.BlockSpec((1, rope_rows, DV_A), lambda i, n: (qkv_col(n), i % n_rope_tiles, 0)))
            args.append(t)
    head_block = (heads, tm, LANES)
    plain_col = lambda n: jnp.where(n < W_HI, 0, n - W_HI + 1)
    gate_col = lambda n: jnp.clip(n - W_HFF, 0, 1)
    out_specs = [pl.BlockSpec(head_block, lambda i, n: (qkv_col(n), i, 0))]
    out_shape = [jax.ShapeDtypeStruct((3 * heads, m, LANES), BF16)]
    if want_kv32:
        requests = m // PROJ_SUB
        out_specs += [pl.BlockSpec((tm // PROJ_SUB, heads, LANES, PROJ_SUB), lambda i, n: (i, 0, 0, 0)),
                      pl.BlockSpec((tm, heads, LANES), lambda i, n: (i, 0, 0))]
        out_shape += [jax.ShapeDtypeStruct((requests, heads, LANES, PROJ_SUB), F32),
                      jax.ShapeDtypeStruct((m, heads, LANES), F32)]
    out_specs += [pl.BlockSpec(head_block, lambda i, n: (plain_col(n), i, 0)),
                  pl.BlockSpec(head_block, lambda i, n: (gate_col(n), i, 0)),
                  pl.BlockSpec(head_block, lambda i, n: (gate_col(n), i, 0))]
    out_shape += [jax.ShapeDtypeStruct((3 * heads, m, LANES), BF16),
                  jax.ShapeDtypeStruct((2 * heads, m, LANES), F32),
                  jax.ShapeDtypeStruct((2 * heads, m, LANES), BF16)]
    return pl.pallas_call(
        functools.partial(_proj_kernel, rope=rope, want_kv32=want_kv32),
        grid=(tiles, n_blocks),
        in_specs=in_specs,
        out_specs=out_specs,
        out_shape=out_shape,
        compiler_params=_params(("arbitrary", "arbitrary")),
        name="mix_proj",
    )(*args)


def _attn_kernel(*refs, lam_init, has_ctx):
    if has_ctx:
        lam_ref, g_ref, q_ref, k_ref, v_ref, kc_ref, vc_ref, o_ref = refs
    else:
        lam_ref, g_ref, q_ref, k_ref, v_ref, o_ref = refs
    lp = lam_ref[...]
    lam = (jnp.exp(jnp.sum(lp[0:1] * lp[1:2], axis=-1, keepdims=True))
           - jnp.exp(jnp.sum(lp[2:3] * lp[3:4], axis=-1, keepdims=True)) + lam_init)

    lane = lax.broadcasted_iota(jnp.int32, (1, DV_A), 1)
    comp0 = lane < DH_A
    tq = q_ref.shape[2]
    sub = min(ATT_SUB, tq)
    problems = []
    for hh in range(q_ref.shape[0]):
        keys = [k_ref[hh, 0]]
        vals = [v_ref[hh, 0]]
        if has_ctx:
            keys.append(kc_ref[0, :, hh * DV_A:(hh + 1) * DV_A].astype(BF16))
            vals.append(vc_ref[0, :, hh * DV_A:(hh + 1) * DV_A].astype(BF16))
        for r in range(tq // sub):
            q = q_ref[hh, 0, r * sub:(r + 1) * sub, :]
            zero = jnp.zeros_like(q)
            for qc in (jnp.where(comp0, q, zero), jnp.where(comp0, zero, q)):
                problems.append((qc, keys, vals))
    scores = [[_nt_dot(kk, qc) for kk in keys] for qc, keys, _ in problems]
    weights = []
    for ss in scores:
        mx = functools.reduce(jnp.maximum, [jnp.max(s, axis=0, keepdims=True) for s in ss])
        es = [jnp.exp2(s - mx) for s in ss]
        den = functools.reduce(jnp.add, [jnp.sum(e, axis=0, keepdims=True) for e in es])
        weights.append(([e.astype(BF16) for e in es], 1.0 / den))
    maps = [functools.reduce(jnp.add, [_tn_dot(vv, e) for e, vv in zip(es, vals)]) * r
            for (es, r), (_, _, vals) in zip(weights, problems)]
    idx = 0
    for hh in range(q_ref.shape[0]):
        for r in range(tq // sub):
            o = maps[idx] - lam * maps[idx + 1]
            idx += 2
            o = o * lax.rsqrt(jnp.mean(o * o, axis=0, keepdims=True) + RMS_EPS)
            o = o.T * g_ref[...]
            o_ref[hh, 0, r * sub:(r + 1) * sub, :] = (o * (1.0 - lam_init)).astype(o_ref.dtype)


def _attention(q, k, v, ctx_kv, lam_params, subln_g, lam_init):
    (q_arr, q_row), (k_arr, k_row), (v_arr, v_row) = q, k, v
    _, bsz, t, _ = q_arr.shape
    tq = min(ATT_TQ, t)
    hp = N_HEADS_A if t <= ATT_SHORT_SEQ else 1
    const = lambda b, h, i: (0, 0)
    rows = lambda first: (lambda b, h, i: (first // hp + h, b, 0, 0))
    in_specs = [
        pl.BlockSpec((4, DH_A), const),
        pl.BlockSpec((1, DV_A), const),
        pl.BlockSpec((hp, 1, tq, DV_A), lambda b, h, i: (q_row // hp + h, b, i, 0)),
        pl.BlockSpec((hp, 1, t, DV_A), rows(k_row)),
        pl.BlockSpec((hp, 1, t, DV_A), rows(v_row)),
    ]
    args = [lam_params, subln_g, q_arr, k_arr, v_arr]
    if ctx_kv is not None:
        p = ctx_kv[0].shape[1]
        head_all = lambda b, h, i: (b, 0, h)
        in_specs += [pl.BlockSpec((1, p, hp * DV_A), head_all), pl.BlockSpec((1, p, hp * DV_A), head_all)]
        args += list(ctx_kv)
    return pl.pallas_call(
        functools.partial(_attn_kernel, lam_init=lam_init, has_ctx=ctx_kv is not None),
        grid=(bsz, N_HEADS_A // hp, t // tq),
        in_specs=in_specs,
        out_specs=pl.BlockSpec((hp, 1, tq, DV_A), lambda b, h, i: (h, b, i, 0)),
        out_shape=jax.ShapeDtypeStruct((N_HEADS_A, bsz, t, DV_A), BF16),
        compiler_params=_params(("parallel", "parallel", "arbitrary")),
        name="attention",
    )(*args)


def _chunk_scan(g, reverse):
    r = g.shape[0]
    pos = lax.broadcasted_iota(jnp.int32, (r, 1), 0) % HGRN_C
    b = g
    sh = 1
    while sh < HGRN_C:
        if reverse:
            b = b + jnp.where(pos < HGRN_C - sh, pltpu.roll(b, r - sh, 0), 0.0)
        else:
            b = b + jnp.where(pos >= sh, pltpu.roll(b, sh, 0), 0.0)
        sh *= 2
    return b


def _hgrn_kernel(*refs, seq_len, has_s0, want_state):
    refs = list(refs)
    hq_ref, hi_ref, hg_ref, gf_ref, gb_ref, kf_ref, kb_ref, ng_ref = refs[:8]
    refs = refs[8:]
    s0_ref = refs.pop(0) if has_s0 else None
    o_ref = refs.pop(0)
    sfin_ref = refs.pop(0) if want_state else None
    od_scr, st_scr, b_scr, q_scr, k_scr, ok_scr = refs
    hp = hq_ref.shape[0]
    c = HGRN_C
    rb = min(HGRN_ROWS, seq_len)
    nb = seq_len // rb
    cpb = rb // c
    g_refs = (gf_ref, gb_ref)
    k_refs = (kf_ref, kb_ref)

    def queries(hq):
        hq = hq.astype(F32)
        return hq * jax.nn.sigmoid(hq)

    def flags(i, carry):
        rows = pl.ds(pl.multiple_of(i * rb, rb), rb)
        for d in range(2):
            worst = jnp.zeros((1, DK_H), F32)
            for hh in range(hp):
                g = g_refs[d][hh, 0, rows, :]
                for j in range(cpb):
                    worst = jnp.maximum(worst, -jnp.sum(g[j * c:(j + 1) * c], axis=0, keepdims=True))
            ok_scr[d, i] = (jnp.max(worst) <= HGRN_SAFE_DECAY).astype(jnp.int32)
        return carry

    lax.fori_loop(0, nb, flags, 0)

    for hh in range(hp):
        for d in range(2):
            if has_s0:
                st_scr[hh, d] = s0_ref[0, d, hh].T
            else:
                st_scr[hh, d] = jnp.zeros((DV_H, DK_H), F32)

    row = lax.broadcasted_iota(jnp.int32, (c, 1), 0)
    col = lax.broadcasted_iota(jnp.int32, (1, c), 1)

    def fast_block(d, r0):
        reverse = d == 1
        rows = pl.ds(r0, rb)
        n = hp * cpb
        flat = lambda a: a.reshape(hp * rb, a.shape[-1])
        chunks = lambda a: a.reshape(n, c, a.shape[-1])
        vb = chunks(hi_ref[:, 0, rows, :])
        b = _chunk_scan(flat(g_refs[d][:, 0, rows, :]), reverse)
        qd = chunks((queries(flat(hq_ref[:, 0, rows, :])) * jnp.exp(b)).astype(BF16))
        ku = chunks(flat(k_refs[d][:, 0, rows, :]).astype(F32) * jnp.exp(-b))
        b = chunks(b)
        e_end = jnp.exp(b[:, 0:1, :] if reverse else b[:, c - 1:c, :])
        causal = ((row <= col) if reverse else (row >= col))[None]
        a = jnp.einsum('nck,nsk->ncs', qd, ku.astype(BF16), preferred_element_type=F32)
        o = jnp.einsum('ncs,nsv->ncv', jnp.where(causal, a, 0.0).astype(BF16), vb, preferred_element_type=F32)
        u = jnp.einsum('nsv,nsk->nvk', vb, (ku * e_end).astype(BF16), preferred_element_type=F32)
        entering = [None] * n
        for hh in range(hp):
            st = st_scr[hh, d]
            for j in (range(cpb - 1, -1, -1) if reverse else range(cpb)):
                entering[hh * cpb + j] = st.astype(BF16)
                st = e_end[hh * cpb + j] * st + u[hh * cpb + j]
            st_scr[hh, d] = st
        o = o + jnp.einsum('nck,nvk->ncv', qd, jnp.stack(entering), preferred_element_type=F32)
        od_scr[:, d, rows, :] = o.reshape(hp, rb, DV_H)

    def exact_chunk(hh, d, r0):
        reverse = d == 1
        rows = pl.ds(r0, c)
        b = _chunk_scan(g_refs[d][hh, 0, rows, :], reverse)
        b_scr[...] = b
        q_scr[...] = queries(hq_ref[hh, 0, rows, :])
        k_scr[...] = k_refs[d][hh, 0, rows, :].astype(F32)

        def one_row(t, carry):
            bt = b_scr[pl.ds(t, 1), :]
            qt = q_scr[pl.ds(t, 1), :]
            seen = (row >= t) if reverse else (row <= t)
            w = jnp.exp(jnp.where(seen, bt - b_scr[...], -jnp.inf))
            p = jnp.sum(qt * k_scr[...] * w, axis=-1, keepdims=True)
            od_scr[hh, d, pl.ds(r0 + t, 1), :] = jnp.sum(p * hi_ref[hh, 0, rows, :].astype(F32),
                                                         axis=0, keepdims=True)
            return carry

        lax.fori_loop(0, c, one_row, 0)
        b_end = b[0:1] if reverse else b[c - 1:c]
        st = st_scr[hh, d]
        qd = (q_scr[...] * jnp.exp(b)).astype(BF16)
        od_scr[hh, d, rows, :] = od_scr[hh, d, rows, :] + _nt_dot(qd, st.astype(BF16))
        k_end = (k_scr[...] * jnp.exp(b_end - b)).astype(BF16)
        st_scr[hh, d] = jnp.exp(b_end) * st + _tn_dot(hi_ref[hh, 0, rows, :], k_end)

    def block(i, carry):
        starts = (pl.multiple_of(i * rb, rb), pl.multiple_of((nb - 1 - i) * rb, rb))
        mild = (ok_scr[0, i] + ok_scr[1, nb - 1 - i]) == 2

        @pl.when(mild)
        def _():
            for d in range(2):
                fast_block(d, starts[d])

        @pl.when(jnp.logical_not(mild))
        def _():
            for hh in range(hp):
                for d in range(2):
                    for j in (range(cpb - 1, -1, -1) if d == 1 else range(cpb)):
                        exact_chunk(hh, d, pl.multiple_of(starts[d] + j * c, c))

        return carry

    lax.fori_loop(0, nb, block, 0)

    def finish(i, carry):
        rows = pl.ds(pl.multiple_of(i * rb, rb), rb)
        for hh in range(hp):
            o = od_scr[hh, 0, rows, :] + od_scr[hh, 1, rows, :]
            o = o * lax.rsqrt(jnp.mean(o * o, axis=-1, keepdims=True) + RMS_EPS) * ng_ref[...]
            hg = hg_ref[hh, 0, rows, :].astype(F32)
            o_ref[hh, 0, rows, :] = (o * (hg * jax.nn.sigmoid(hg))).astype(o_ref.dtype)
        return carry

    lax.fori_loop(0, nb, finish, 0)
    if want_state:
        for hh in range(hp):
            for d in range(2):
                sfin_ref[0, d, hh] = st_scr[hh, d].T


def _hgrn(plain, g, kd, norm_g, s0, want_state):
    _, bsz, t, _ = plain.shape
    hp = HGRN_SHORT_HEADS if t <= HGRN_ROWS else HGRN_LONG_HEADS
    stream = lambda s: pl.BlockSpec((hp, 1, t, DK_H), lambda b, h: (s * N_HEADS_H // hp + h, b, 0, 0))
    in_specs = [stream(0), stream(1), stream(2), stream(0), stream(1), stream(0), stream(1),
                pl.BlockSpec((1, DV_H), lambda b, h: (0, 0))]
    args = [plain, plain, plain, g, g, kd, kd, norm_g]
    state_spec = pl.BlockSpec((1, 2, hp, DK_H, DV_H), lambda b, h: (b, 0, h, 0, 0))
    if s0 is not None:
        in_specs.append(state_spec)
        args.append(s0)
    out_specs = [pl.BlockSpec((hp, 1, t, DV_H), lambda b, h: (h, b, 0, 0))]
    out_shape = [jax.ShapeDtypeStruct((N_HEADS_H, bsz, t, DV_H), BF16)]
    if want_state:
        out_specs.append(state_spec)
        out_shape.append(jax.ShapeDtypeStruct((bsz, 2, N_HEADS_H, DK_H, DV_H), F32))
    n_blocks = t // min(HGRN_ROWS, t)
    chunk = pltpu.VMEM((HGRN_C, DK_H), F32)
    outs = pl.pallas_call(
        functools.partial(_hgrn_kernel, seq_len=t, has_s0=s0 is not None, want_state=want_state),
        grid=(bsz, N_HEADS_H // hp),
        in_specs=in_specs,
        out_specs=out_specs,
        out_shape=out_shape,
        scratch_shapes=[
            pltpu.VMEM((hp, 2, t, DV_H), F32),
            pltpu.VMEM((hp, 2, DV_H, DK_H), F32),
            chunk, chunk, chunk,
            pltpu.SMEM((2, n_blocks), jnp.int32),
        ],
        compiler_params=_params(("parallel", "parallel")),
        name="hgrn",
    )(*args)
    return outs if want_state else (outs[0], None)


def _merge_ffn2_kernel(oa0_ref, oh0_ref, oa1_ref, oh1_ref, h_ref, x_ref, mod_ref,
                       wmix_hbm, wa_hbm, wh_hbm, wo_hbm, win_hbm, wout_hbm, lng_ref, lnb_ref, y0_ref, y1_ref,
                       wg_scr, wa_scr, wh_scr, wo_scr,
                       win_scr, wout_scr, wide_stage, square_stage, wide_sems, square_sems, act_scr, *, tiles_0):
    i = pl.program_id(0)
    first_group = i < tiles_0

    @pl.when(i == 0)
    def _():
        _load_weight(win_hbm, win_scr, wide_stage, wide_sems)
        for half in range(2):
            _load_weight(wmix_hbm, wg_scr, square_stage, square_sems,
                         src_col=W_GA0 * MIX_TN + half * D_MODEL, dst_col=half * D_MODEL)
        for w_hbm, w_scr in ((wa_hbm, wa_scr), (wh_hbm, wh_scr), (wo_hbm, wo_scr), (wout_hbm, wout_scr)):
            _load_weight(w_hbm, w_scr, square_stage, square_sems)

    def heads(ref0, ref1):
        pick = lambda hh: jnp.where(first_group, ref0[hh], ref1[hh])
        return jnp.concatenate([pick(hh) for hh in range(ref0.shape[0])], axis=-1)

    gates = jax.nn.sigmoid(jnp.dot(h_ref[...], wg_scr[...], preferred_element_type=F32))
    ba = jnp.dot(heads(oa0_ref, oa1_ref), wa_scr[...], preferred_element_type=F32)
    bh = jnp.dot(heads(oh0_ref, oh1_ref), wh_scr[...], preferred_element_type=F32)
    merged = gates[:, :D_MODEL] * ba + gates[:, D_MODEL:] * bh
    mix = jnp.dot(merged.astype(BF16), wo_scr[...], preferred_element_type=F32)
    x2 = _layer_norm(ALPHA * x_ref[...] + mod_ref[0, 5:6, :] * mix, lng_ref[1:2, :], lnb_ref[1:2, :])
    y = _half_step(x2, mod_ref, win_scr, wout_scr, lng_ref, lnb_ref, act_scr, 2)

    @pl.when(first_group)
    def _():
        y0_ref[...] = y

    @pl.when(jnp.logical_not(first_group))
    def _():
        y1_ref[...] = y


def _merge_ffn2(mixed, h, x, mod, cond_of_tile, w_mix, w_a, w_h, w_o, w_in, w_out, ln_g, ln_b):
    (oa0, oh0), (oa1, oh1) = mixed
    tiles_0, tiles_1 = oa0.shape[1] // FFN_TM, oa1.shape[1] // FFN_TM
    row = lambda i: (i, 0)
    const = lambda i: (0, 0)
    in_group_0 = lambda i: jnp.minimum(i, tiles_0 - 1)
    in_group_1 = lambda i: jnp.maximum(i - tiles_0, 0)
    hbm = pl.BlockSpec(memory_space=pl.ANY)
    return pl.pallas_call(
        functools.partial(_merge_ffn2_kernel, tiles_0=tiles_0),
        grid=(tiles_0 + tiles_1,),
        in_specs=[
            pl.BlockSpec((N_HEADS_A, FFN_TM, DV_A), lambda i: (0, in_group_0(i), 0)),
            pl.BlockSpec((N_HEADS_H, FFN_TM, DV_H), lambda i: (0, in_group_0(i), 0)),
            pl.BlockSpec((N_HEADS_A, FFN_TM, DV_A), lambda i: (0, in_group_1(i), 0)),
            pl.BlockSpec((N_HEADS_H, FFN_TM, DV_H), lambda i: (0, in_group_1(i), 0)),
            pl.BlockSpec((FFN_TM, D_MODEL), row),
            pl.BlockSpec((FFN_TM, D_MODEL), row),
            pl.BlockSpec((1, 9, D_MODEL), lambda i: (cond_of_tile(i, FFN_TM), 0, 0)),
            hbm, hbm, hbm, hbm, hbm, hbm,
            pl.BlockSpec((3, D_MODEL), const),
            pl.BlockSpec((3, D_MODEL), const),
        ],
        out_specs=[pl.BlockSpec((FFN_TM, D_MODEL), lambda i: (in_group_0(i), 0)),
                   pl.BlockSpec((FFN_TM, D_MODEL), lambda i: (in_group_1(i), 0))],
        out_shape=[jax.ShapeDtypeStruct((tiles_0 * FFN_TM, D_MODEL), F32),
                   jax.ShapeDtypeStruct((tiles_1 * FFN_TM, D_MODEL), F32)],
        scratch_shapes=[pltpu.VMEM((D_MODEL, 2 * D_MODEL), BF16), pltpu.VMEM((W_A, D_MODEL), BF16),
                        pltpu.VMEM((W_H, D_MODEL), BF16), pltpu.VMEM((D_MODEL, D_MODEL), BF16)]
                       + _ffn_weight_scratch() + [pltpu.VMEM((FFN_TM, D_FF), BF16)],
        compiler_params=_params(("arbitrary",)),
        name="merge_ffn2",
    )(oa0, oh0, oa1, oh1, h, x, mod, w_mix, w_a, w_h, w_o, w_in, w_out, ln_g, ln_b)


def _rope_tables(n_tok):
    rows = n_tok // GRID_W
    row = np.repeat(np.arange(rows, dtype=np.float32), GRID_W)
    col = np.tile(np.arange(GRID_W, dtype=np.float32), rows)
    half = DH_A // 2
    inv = (ROPE_BASE ** (-np.arange(0, half, 2, dtype=np.float32) / half)).astype(np.float32)
    ar = row[:, None] * inv
    ac = col[:, None] * inv
    ang = np.concatenate([ar, ar, ac, ac] * 2, axis=-1)
    cos, sin = np.cos(ang).astype(np.float32), np.sin(ang).astype(np.float32)
    scale = np.float32(Q_SCALE)
    return (jnp.asarray(np.stack([cos * scale, cos, np.ones_like(cos)])),
            jnp.asarray(np.stack([sin * scale, sin, np.zeros_like(sin)])))


def _mixer(h1, group, wts, lam_init, bsz, ctx):
    m = MIX_TM
    t = m // bsz
    latent = ctx is not None
    per_head = lambda a: a.reshape(a.shape[0], bsz, t, a.shape[-1])
    if latent:
        qkv, plain, g, kd = _project(h1, group, wts['w_mix_in'], wts['lb_logits'], _rope_tables(t), False)
        k = v = None
        ctx_kv = (ctx[0], ctx[1])
    else:
        assert t == PROJ_SUB
        qkv, k, v, plain, g, kd = _project(h1, group, wts['w_mix_in'], wts['lb_logits'], None, True)
        ctx_kv = None
    qkv = per_head(qkv)
    q_kv = [(qkv, W_Q * N_HEADS_A), (qkv, W_K * N_HEADS_A), (qkv, W_V * N_HEADS_A)]
    o_a = _attention(*q_kv, ctx_kv, wts['lam_params'], wts['attn_subln_g'], lam_init)
    o_h, s_fin = _hgrn(per_head(plain), per_head(g), per_head(kd), wts['hgrn_norm_g'],
                       ctx[2] if latent else None, not latent)
    return (o_a.reshape(N_HEADS_A, m, DV_A), o_h.reshape(N_HEADS_H, m, DV_H)), (k, v, s_fin)


def kernel(x_prompt, x_sample, cache_k, cache_v, state_hgrn, c, c_ctx, w_ada, b_ada, ffn1_w_in, ffn1_w_out,
           w_mix_in, lambda_q1, lambda_k1, lambda_q2, lambda_k2, attn_subln_g, hgrn_lb_logits, hgrn_norm_g,
           w_branch_a, w_branch_h, w_mix_out, ffn2_w_in, ffn2_w_out, ln_g, ln_b):
    batch, seq, _ = x_prompt.shape
    dec_batch, dec_seq, _ = x_sample.shape
    past = cache_k.shape[2]
    assert DEPTH == 1 and dec_batch + 1 <= MOD_ROWS
    lam_init = 0.8 - 0.6 * math.exp(-0.3 * 0)

    cond = jnp.concatenate([c_ctx[None, :], c, jnp.zeros((MOD_ROWS - 1 - dec_batch, D_MODEL), F32)], axis=0)
    mod = _modulation(cond, w_ada[0], b_ada[0])

    wts = {
        'w_mix_in': w_mix_in[0],
        'lam_params': jnp.concatenate([lambda_q1, lambda_k1, lambda_q2, lambda_k2], axis=0),
        'attn_subln_g': attn_subln_g, 'lb_logits': hgrn_lb_logits, 'hgrn_norm_g': hgrn_norm_g,
    }
    n_ctx = batch * seq
    assert n_ctx == MIX_TM and dec_batch * dec_seq == MIX_TM

    def cond_of_tile(i, tm):
        first = i * tm
        return jnp.where(first < n_ctx, 0, 1 + (first - n_ctx) // dec_seq)

    x1, h1 = _ffn1(x_prompt.reshape(n_ctx, D_MODEL), x_sample.reshape(dec_batch * dec_seq, D_MODEL), mod,
                   cond_of_tile, ffn1_w_in[0], ffn1_w_out[0], ln_g[0], ln_b[0])
    mixed_p, (k_c, v_c, s_c) = _mixer(h1, 0, wts, lam_init, batch, None)
    ctx = (cache_k[:, 0].reshape(dec_batch, past, QK_A), cache_v[:, 0].reshape(dec_batch, past, W_A),
           state_hgrn[:, 0])
    mixed_s, _ = _mixer(h1, 1, wts, lam_init, dec_batch, ctx)
    y_p, y_s = _merge_ffn2((mixed_p, mixed_s), h1, x1, mod, cond_of_tile, w_mix_in[0],
                           w_branch_a[0], w_branch_h[0], w_mix_out[0], ffn2_w_in[0], ffn2_w_out[0],
                           ln_g[0], ln_b[0])

    return (y_p.reshape(batch, seq, D_MODEL),
            y_s.reshape(dec_batch, dec_seq, D_MODEL),
            k_c.reshape(batch, DEPTH, N_HEADS_A, 2, DH_A, seq).transpose(0, 1, 5, 2, 3, 4),
            v_c.reshape(batch, DEPTH, seq, N_HEADS_A, DV_A),
            s_c.reshape(batch, DEPTH, 2, N_HEADS_H, DK_H, DV_H))
```

```python
import functools
import math

import numpy as np
import jax
import jax.numpy as jnp
from jax import lax
from jax.experimental import pallas as pl
from jax.experimental.pallas import tpu as pltpu

D_MODEL = 1024
DEPTH = 1
GRID_W = 64
N_HEADS_A = 4
DH_A = 64
DV_A = 2 * DH_A
QK_A = N_HEADS_A * 2 * DH_A
W_A = N_HEADS_A * DV_A
ROPE_BASE = 10000.0
N_HEADS_H = 4
DK_H = 128
DV_H = 128
QK_H = N_HEADS_H * DK_H
W_H = N_HEADS_H * DV_H
D_FF = 2816
MIX_IN = 2 * QK_A + W_A + 3 * QK_H + 2 * W_H + 2 * D_MODEL
ALPHA = (2 * DEPTH) ** 0.25
LN_EPS = 1e-5
RMS_EPS = 1e-6
Q_SCALE = DH_A ** -0.5 * math.log2(math.e)

F32 = jnp.float32
BF16 = jnp.bfloat16

LANES = 128
VMEM_LIMIT = 56 * 1024 * 1024

FFN_TM = 512
FFN_TF = 256
WIDE_CHUNK_ROWS = 128
SQUARE_CHUNK_ROWS = 256
MIX_TM = 4096
PROJ_TM = 2048
PROJ_SUB = 256
MIX_TN = 512
ATT_TQ = 2048
ATT_SUB = 512
ATT_SHORT_SEQ = 256
HGRN_C = 64
HGRN_ROWS = 1024
HGRN_SHORT_HEADS = 4
HGRN_LONG_HEADS = 2
HGRN_SAFE_DECAY = 75.0
MOD_ROWS = 8
MOD_TN = 2304

W_Q, W_K, W_V, W_HQ, W_HFF, W_HFB, W_HI, W_HG, W_GA0, W_GA1, W_GH0, W_GH1 = range(12)


def _params(sem):
    return pltpu.CompilerParams(dimension_semantics=sem, vmem_limit_bytes=VMEM_LIMIT)


def _nt_dot(a, b):
    return lax.dot_general(a, b, (((1,), (1,)), ((), ())), preferred_element_type=F32)


def _tn_dot(a, b):
    return lax.dot_general(a, b, (((0,), (0,)), ((), ())), preferred_element_type=F32)


def _layer_norm(y, g, b):
    mu = jnp.mean(y, axis=-1, keepdims=True)
    yc = y - mu
    var = jnp.mean(yc * yc, axis=-1, keepdims=True)
    return yc * lax.rsqrt(var + LN_EPS) * g + b


def _mod_kernel(c_ref, w_ref, b_ref, o_ref):
    c = c_ref[...]
    a = (c * jax.nn.sigmoid(c)).astype(BF16)
    o_ref[...] = jnp.dot(a, w_ref[...].astype(BF16), preferred_element_type=F32) + b_ref[...]


def _modulation(cond, w_ada, b_ada):
    n_out = w_ada.shape[1]
    tn = MOD_TN
    out = pl.pallas_call(
        _mod_kernel,
        grid=(n_out // tn,),
        in_specs=[
            pl.BlockSpec((MOD_ROWS, D_MODEL), lambda n: (0, 0)),
            pl.BlockSpec((D_MODEL, tn), lambda n: (0, n)),
            pl.BlockSpec((1, tn), lambda n: (0, n)),
        ],
        out_specs=pl.BlockSpec((MOD_ROWS, tn), lambda n: (0, n)),
        out_shape=jax.ShapeDtypeStruct((MOD_ROWS, n_out), F32),
        compiler_params=_params(("arbitrary",)),
        name="modulation",
    )(cond, w_ada, b_ada.reshape(1, n_out))
    return out.reshape(MOD_ROWS, 9, D_MODEL)


def _load_weight(w_hbm, w_scr, stage, sems, src_col=0, dst_col=0):
    _, chunk, width = stage.shape
    n_chunks = w_hbm.shape[0] // chunk

    def copy(c):
        return pltpu.make_async_copy(w_hbm.at[pl.ds(c * chunk, chunk), pl.ds(src_col, width)],
                                     stage.at[c % 2], sems.at[c % 2])

    copy(0).start()
    for c in range(n_chunks):
        if c + 1 < n_chunks:
            copy(c + 1).start()
        copy(c).wait()
        w_scr[c * chunk:(c + 1) * chunk, dst_col:dst_col + width] = stage[c % 2].astype(BF16)


def _modulate(x, mod_ref, sub):
    return x * (1.0 + mod_ref[0, 3 * sub + 1:3 * sub + 2, :]) + mod_ref[0, 3 * sub:3 * sub + 1, :]


def _half_step(x, mod_ref, win_ref, wout_ref, lng_ref, lnb_ref, act_scr, sub):
    h = _modulate(x, mod_ref, sub).astype(BF16)
    for c in range(D_FF // FFN_TF):
        cols = slice(c * FFN_TF, (c + 1) * FFN_TF)
        up_cols = slice(D_FF + c * FFN_TF, D_FF + (c + 1) * FFN_TF)
        gate = jnp.dot(h, win_ref[:, cols], preferred_element_type=F32)
        up = jnp.dot(h, win_ref[:, up_cols], preferred_element_type=F32)
        act_scr[:, cols] = (gate * jax.nn.sigmoid(gate) * up).astype(BF16)
    f = jnp.dot(act_scr[...], wout_ref[...], preferred_element_type=F32)
    g = mod_ref[0, 3 * sub + 2:3 * sub + 3, :]
    return _layer_norm(ALPHA * x + 0.5 * g * f, lng_ref[sub:sub + 1, :], lnb_ref[sub:sub + 1, :])


def _ffn_weight_scratch():
    return [
        pltpu.VMEM((D_MODEL, 2 * D_FF), BF16),
        pltpu.VMEM((D_FF, D_MODEL), BF16),
        pltpu.VMEM((2, WIDE_CHUNK_ROWS, 2 * D_FF), F32),
        pltpu.VMEM((2, SQUARE_CHUNK_ROWS, D_MODEL), F32),
        pltpu.SemaphoreType.DMA((2,)),
        pltpu.SemaphoreType.DMA((2,)),
    ]


def _ffn1_kernel(xa_ref, xb_ref, mod_ref, win_hbm, wout_hbm, lng_ref, lnb_ref, xo_ref, ho_ref,
                 win_scr, wout_scr, wide_stage, square_stage, wide_sems, square_sems, act_scr, *, tiles_a):
    i = pl.program_id(0)

    @pl.when(i == 0)
    def _():
        _load_weight(win_hbm, win_scr, wide_stage, wide_sems)
        _load_weight(wout_hbm, wout_scr, square_stage, square_sems)

    x = jnp.where(i < tiles_a, xa_ref[...], xb_ref[...])
    xn = _half_step(x, mod_ref, win_scr, wout_scr, lng_ref, lnb_ref, act_scr, 0)
    xo_ref[...] = xn
    ho_ref[...] = _modulate(xn, mod_ref, 1).astype(BF16)


def _ffn1(xa, xb, mod, cond_of_tile, w_in, w_out, ln_g, ln_b):
    tiles_a, tiles_b = xa.shape[0] // FFN_TM, xb.shape[0] // FFN_TM
    m = xa.shape[0] + xb.shape[0]
    row = lambda i: (i, 0)
    const = lambda i: (0, 0)
    hbm = pl.BlockSpec(memory_space=pl.ANY)
    return pl.pallas_call(
        functools.partial(_ffn1_kernel, tiles_a=tiles_a),
        grid=(tiles_a + tiles_b,),
        in_specs=[
            pl.BlockSpec((FFN_TM, D_MODEL), lambda i: (jnp.minimum(i, tiles_a - 1), 0)),
            pl.BlockSpec((FFN_TM, D_MODEL), lambda i: (jnp.maximum(i - tiles_a, 0), 0)),
            pl.BlockSpec((1, 9, D_MODEL), lambda i: (cond_of_tile(i, FFN_TM), 0, 0)),
            hbm, hbm,
            pl.BlockSpec((3, D_MODEL), const),
            pl.BlockSpec((3, D_MODEL), const),
        ],
        out_specs=[pl.BlockSpec((FFN_TM, D_MODEL), row), pl.BlockSpec((FFN_TM, D_MODEL), row)],
        out_shape=[jax.ShapeDtypeStruct((m, D_MODEL), F32), jax.ShapeDtypeStruct((m, D_MODEL), BF16)],
        scratch_shapes=_ffn_weight_scratch() + [pltpu.VMEM((FFN_TM, D_FF), BF16)],
        compiler_params=_params(("arbitrary",)),
        name="ffn1",
    )(xa, xb, mod, w_in, w_out, ln_g, ln_b)


def _rope(x, cos, sin):
    half, quarter = DH_A // 2, DH_A // 4
    lane = lax.broadcasted_iota(jnp.int32, (1, LANES), 1)
    first_quarter = (lane % half) < quarter
    outs = []
    for hb in range(QK_A // LANES):
        xb = x[:, hb * LANES:(hb + 1) * LANES]
        ahead = pltpu.roll(xb, LANES - quarter, 1)
        behind = pltpu.roll(xb, quarter, 1)
        rot = jnp.where(first_quarter, -ahead, behind)
        outs.append(xb * cos + rot * sin)
    return jnp.concatenate(outs, axis=-1)


def _forget_gate(z, logits):
    mx = jnp.max(logits, axis=0, keepdims=True)
    el = jnp.exp(logits - mx)
    lb = el[0:1] / jnp.sum(el, axis=0, keepdims=True)
    sig = 0.5 + 0.5 * jnp.tanh(0.5 * z)
    u = (1.0 - lb) * sig
    f = lb + u
    return jnp.where(f > 0.0, jnp.log(f), z), (1.0 - lb) - u


def _proj_kernel(*refs, rope, want_kv32):
    refs = list(refs)
    h_ref, w_ref, lbl_ref = refs[:3]
    refs = refs[3:]
    cos_ref, sin_ref = (refs.pop(0), refs.pop(0)) if rope else (None, None)
    qkv_ref = refs.pop(0)
    kt_ref, v32_ref = (refs.pop(0), refs.pop(0)) if want_kv32 else (None, None)
    plain_ref, g_ref = refs
    n = pl.program_id(1)

    pieces = [slice(r, r + PROJ_SUB) for r in range(0, h_ref.shape[0], PROJ_SUB)]

    def project(rows, w):
        return jnp.dot(h_ref[rows, :], w, preferred_element_type=F32)

    def store_heads(ref, rows, y):
        for hh in range(MIX_TN // LANES):
            ref[hh, rows, :] = y[:, hh * LANES:(hh + 1) * LANES].astype(ref.dtype)

    def qkv_step(block):
        w = w_ref[...].astype(BF16)
        for p, rows in enumerate(pieces):
            y = project(rows, w)
            if block == W_K:
                for hh in range(MIX_TN // LANES):
                    kt_ref[p, hh] = y[:, hh * LANES:(hh + 1) * LANES].T
            if block == W_V:
                v32_ref[rows] = y.reshape(PROJ_SUB, MIX_TN // LANES, LANES)
            if rope:
                first = rows.start % cos_ref.shape[1]
                table_rows = slice(first, first + PROJ_SUB)
                y = _rope(y, cos_ref[0, table_rows, :], sin_ref[0, table_rows, :])
            elif block is None:
                y = y * jnp.where(n == W_Q, Q_SCALE, 1.0)
            elif block == W_Q:
                y = y * Q_SCALE
            store_heads(qkv_ref, rows, y)

    if want_kv32:
        for block in (W_Q, W_K, W_V):
            pl.when(n == block)(functools.partial(qkv_step, block))
    else:
        pl.when(n <= W_V)(functools.partial(qkv_step, None))

    @pl.when((n == W_HFF) | (n == W_HFB))
    def _():
        w = w_ref[...].astype(BF16)
        for rows in pieces:
            log_f, _ = _forget_gate(project(rows, w), lbl_ref[0])
            store_heads(g_ref, rows, log_f)

    @pl.when((n == W_HQ) | (n >= W_HI))
    def _():
        w = w_ref[...].astype(BF16)
        for rows in pieces:
            store_heads(plain_ref, rows, project(rows, w))


def _project(h, group, w, lb_logits, rope_tables, want_kv32):
    m = MIX_TM
    tm = PROJ_TM
    tiles = m // tm
    rope = rope_tables is not None
    heads = MIX_TN // LANES
    n_blocks = W_HG + 1
    in_specs = [pl.BlockSpec((tm, D_MODEL), lambda i, n: (group * tiles + i, 0)),
                pl.BlockSpec((D_MODEL, MIX_TN), lambda i, n: (0, n)),
                pl.BlockSpec((1, DEPTH + 1, QK_H), lambda i, n: (jnp.clip(n - W_HFF, 0, 1), 0, 0))]
    args = [h, w, lb_logits]
    qkv_col = lambda n: jnp.minimum(n, W_V)
    if rope:
        t_rope = rope_tables[0].shape[1]
        rope_rows = min(t_rope, tm)
        n_rope_tiles = t_rope // rope_rows
        for t in rope_tables:
            in_specs.append(pl.BlockSpec((1, rope_rows, DV_A), lambda i, n: (qkv_col(n), i % n_rope_tiles, 0)))
            args.append(t)
    head_block = (heads, tm, LANES)
    plain_col = lambda n: jnp.where(n < W_HI, 0, n - W_HI + 1)
    gate_col = lambda n: jnp.clip(n - W_HFF, 0, 1)
    out_specs = [pl.BlockSpec(head_block, lambda i, n: (qkv_col(n), i, 0))]
    out_shape = [jax.ShapeDtypeStruct((3 * heads, m, LANES), BF16)]
    if want_kv32:
        requests = m // PROJ_SUB
        out_specs += [pl.BlockSpec((tm // PROJ_SUB, heads, LANES, PROJ_SUB), lambda i, n: (i, 0, 0, 0)),
                      pl.BlockSpec((tm, heads, LANES), lambda i, n: (i, 0, 0))]
        out_shape += [jax.ShapeDtypeStruct((requests, heads, LANES, PROJ_SUB), F32),
                      jax.ShapeDtypeStruct((m, heads, LANES), F32)]
    out_specs += [pl.BlockSpec(head_block, lambda i, n: (plain_col(n), i, 0)),
                  pl.BlockSpec(head_block, lambda i, n: (gate_col(n), i, 0))]
    out_shape += [jax.ShapeDtypeStruct((3 * heads, m, LANES), BF16),
                  jax.ShapeDtypeStruct((2 * heads, m, LANES), F32)]
    return pl.pallas_call(
        functools.partial(_proj_kernel, rope=rope, want_kv32=want_kv32),
        grid=(tiles, n_blocks),
        in_specs=in_specs,
        out_specs=out_specs,
        out_shape=out_shape,
        compiler_params=_params(("arbitrary", "arbitrary")),
        name="mix_proj",
    )(*args)


def _attn_kernel(*refs, lam_init, has_ctx):
    if has_ctx:
        lam_ref, g_ref, q_ref, k_ref, v_ref, kc_ref, vc_ref, o_ref = refs
    else:
        lam_ref, g_ref, q_ref, k_ref, v_ref, o_ref = refs
    lp = lam_ref[...]
    lam = (jnp.exp(jnp.sum(lp[0:1] * lp[1:2], axis=-1, keepdims=True))
           - jnp.exp(jnp.sum(lp[2:3] * lp[3:4], axis=-1, keepdims=True)) + lam_init)

    lane = lax.broadcasted_iota(jnp.int32, (1, DV_A), 1)
    comp0 = lane < DH_A
    tq = q_ref.shape[2]
    sub = min(ATT_SUB, tq)
    problems = []
    for hh in range(q_ref.shape[0]):
        keys = [k_ref[hh, 0]]
        vals = [v_ref[hh, 0]]
        if has_ctx:
            keys.append(kc_ref[0, :, hh * DV_A:(hh + 1) * DV_A].astype(BF16))
            vals.append(vc_ref[0, :, hh * DV_A:(hh + 1) * DV_A].astype(BF16))
        for r in range(tq // sub):
            q = q_ref[hh, 0, r * sub:(r + 1) * sub, :]
            zero = jnp.zeros_like(q)
            for qc in (jnp.where(comp0, q, zero), jnp.where(comp0, zero, q)):
                problems.append((qc, keys, vals))
    scores = [[_nt_dot(kk, qc) for kk in keys] for qc, keys, _ in problems]
    weights = []
    for ss in scores:
        mx = functools.reduce(jnp.maximum, [jnp.max(s, axis=0, keepdims=True) for s in ss])
        es = [jnp.exp2(s - mx) for s in ss]
        den = functools.reduce(jnp.add, [jnp.sum(e, axis=0, keepdims=True) for e in es])
        weights.append(([e.astype(BF16) for e in es], 1.0 / den))
    maps = [functools.reduce(jnp.add, [_tn_dot(vv, e) for e, vv in zip(es, vals)]) * r
            for (es, r), (_, _, vals) in zip(weights, problems)]
    idx = 0
    for hh in range(q_ref.shape[0]):
        for r in range(tq // sub):
            o = maps[idx] - lam * maps[idx + 1]
            idx += 2
            o = o * lax.rsqrt(jnp.mean(o * o, axis=0, keepdims=True) + RMS_EPS)
            o = o.T * g_ref[...]
            o_ref[hh, 0, r * sub:(r + 1) * sub, :] = (o * (1.0 - lam_init)).astype(o_ref.dtype)


def _attention(q, k, v, ctx_kv, lam_params, subln_g, lam_init):
    (q_arr, q_row), (k_arr, k_row), (v_arr, v_row) = q, k, v
    _, bsz, t, _ = q_arr.shape
    tq = min(ATT_TQ, t)
    hp = N_HEADS_A if t <= ATT_SHORT_SEQ else 1
    const = lambda b, h, i: (0, 0)
    rows = lambda first: (lambda b, h, i: (first // hp + h, b, 0, 0))
    in_specs = [
        pl.BlockSpec((4, DH_A), const),
        pl.BlockSpec((1, DV_A), const),
        pl.BlockSpec((hp, 1, tq, DV_A), lambda b, h, i: (q_row // hp + h, b, i, 0)),
        pl.BlockSpec((hp, 1, t, DV_A), rows(k_row)),
        pl.BlockSpec((hp, 1, t, DV_A), rows(v_row)),
    ]
    args = [lam_params, subln_g, q_arr, k_arr, v_arr]
    if ctx_kv is not None:
        p = ctx_kv[0].shape[1]
        head_all = lambda b, h, i: (b, 0, h)
        in_specs += [pl.BlockSpec((1, p, hp * DV_A), head_all), pl.BlockSpec((1, p, hp * DV_A), head_all)]
        args += list(ctx_kv)
    return pl.pallas_call(
        functools.partial(_attn_kernel, lam_init=lam_init, has_ctx=ctx_kv is not None),
        grid=(bsz, N_HEADS_A // hp, t // tq),
        in_specs=in_specs,
        out_specs=pl.BlockSpec((hp, 1, tq, DV_A), lambda b, h, i: (h, b, i, 0)),
        out_shape=jax.ShapeDtypeStruct((N_HEADS_A, bsz, t, DV_A), BF16),
        compiler_params=_params(("parallel", "parallel", "arbitrary")),
        name="attention",
    )(*args)


def _chunk_scan(g, reverse):
    r = g.shape[0]
    pos = lax.broadcasted_iota(jnp.int32, (r, 1), 0) % HGRN_C
    b = g
    sh = 1
    while sh < HGRN_C:
        if reverse:
            b = b + jnp.where(pos < HGRN_C - sh, pltpu.roll(b, r - sh, 0), 0.0)
        else:
            b = b + jnp.where(pos >= sh, pltpu.roll(b, sh, 0), 0.0)
        sh *= 2
    return b


def _hgrn_kernel(*refs, seq_len, has_s0, want_state):
    refs = list(refs)
    hq_ref, hi_ref, hg_ref, gf_ref, gb_ref, ng_ref = refs[:6]
    refs = refs[6:]
    s0_ref = refs.pop(0) if has_s0 else None
    o_ref = refs.pop(0)
    sfin_ref = refs.pop(0) if want_state else None
    od_scr, st_scr, b_scr, q_scr, k_scr, ok_scr = refs
    hp = hq_ref.shape[0]
    c = HGRN_C
    rb = min(HGRN_ROWS, seq_len)
    nb = seq_len // rb
    cpb = rb // c
    g_refs = (gf_ref, gb_ref)

    def queries(hq):
        hq = hq.astype(F32)
        return hq * jax.nn.sigmoid(hq)

    def flags(i, carry):
        rows = pl.ds(pl.multiple_of(i * rb, rb), rb)
        for d in range(2):
            worst = jnp.zeros((1, DK_H), F32)
            for hh in range(hp):
                g = g_refs[d][hh, 0, rows, :]
                for j in range(cpb):
                    worst = jnp.maximum(worst, -jnp.sum(g[j * c:(j + 1) * c], axis=0, keepdims=True))
            ok_scr[d, i] = (jnp.max(worst) <= HGRN_SAFE_DECAY).astype(jnp.int32)
        return carry

    lax.fori_loop(0, nb, flags, 0)

    for hh in range(hp):
        for d in range(2):
            if has_s0:
                st_scr[hh, d] = s0_ref[0, d, hh].T
            else:
                st_scr[hh, d] = jnp.zeros((DV_H, DK_H), F32)

    row = lax.broadcasted_iota(jnp.int32, (c, 1), 0)
    col = lax.broadcasted_iota(jnp.int32, (1, c), 1)

    def fast_block(d, r0):
        reverse = d == 1
        rows = pl.ds(r0, rb)
        n = hp * cpb
        flat = lambda a: a.reshape(hp * rb, a.shape[-1])
        chunks = lambda a: a.reshape(n, c, a.shape[-1])
        vb = chunks(hi_ref[:, 0, rows, :])
        g = flat(g_refs[d][:, 0, rows, :])
        b = _chunk_scan(g, reverse)
        qd = chunks((queries(flat(hq_ref[:, 0, rows, :])) * jnp.exp(b)).astype(BF16))
        ku = chunks((1.0 - jnp.exp(g)) * jnp.exp(-b))
        b = chunks(b)
        e_end = jnp.exp(b[:, 0:1, :] if reverse else b[:, c - 1:c, :])
        causal = ((row <= col) if reverse else (row >= col))[None]
        a = jnp.einsum('nck,nsk->ncs', qd, ku.astype(BF16), preferred_element_type=F32)
        o = jnp.einsum('ncs,nsv->ncv', jnp.where(causal, a, 0.0).astype(BF16), vb, preferred_element_type=F32)
        u = jnp.einsum('nsv,nsk->nvk', vb, (ku * e_end).astype(BF16), preferred_element_type=F32)
        entering = [None] * n
        for hh in range(hp):
            st = st_scr[hh, d]
            for j in (range(cpb - 1, -1, -1) if reverse else range(cpb)):
                entering[hh * cpb + j] = st.astype(BF16)
                st = e_end[hh * cpb + j] * st + u[hh * cpb + j]
            st_scr[hh, d] = st
        o = o + jnp.einsum('nck,nvk->ncv', qd, jnp.stack(entering), preferred_element_type=F32)
        od_scr[:, d, rows, :] = o.reshape(hp, rb, DV_H)

    def exact_chunk(hh, d, r0):
        reverse = d == 1
        rows = pl.ds(r0, c)
        b = _chunk_scan(g_refs[d][hh, 0, rows, :], reverse)
        b_scr[...] = b
        q_scr[...] = queries(hq_ref[hh, 0, rows, :])
        k_scr[...] = 1.0 - jnp.exp(g_refs[d][hh, 0, rows, :])

        def one_row(t, carry):
            bt = b_scr[pl.ds(t, 1), :]
            qt = q_scr[pl.ds(t, 1), :]
            seen = (row >= t) if reverse else (row <= t)
            w = jnp.exp(jnp.where(seen, bt - b_scr[...], -jnp.inf))
            p = jnp.sum(qt * k_scr[...] * w, axis=-1, keepdims=True)
            od_scr[hh, d, pl.ds(r0 + t, 1), :] = jnp.sum(p * hi_ref[hh, 0, rows, :].astype(F32),
                                                         axis=0, keepdims=True)
            return carry

        lax.fori_loop(0, c, one_row, 0)
        b_end = b[0:1] if reverse else b[c - 1:c]
        st = st_scr[hh, d]
        qd = (q_scr[...] * jnp.exp(b)).astype(BF16)
        od_scr[hh, d, rows, :] = od_scr[hh, d, rows, :] + _nt_dot(qd, st.astype(BF16))
        k_end = (k_scr[...] * jnp.exp(b_end - b)).astype(BF16)
        st_scr[hh, d] = jnp.exp(b_end) * st + _tn_dot(hi_ref[hh, 0, rows, :], k_end)

    def block(i, carry):
        starts = (pl.multiple_of(i * rb, rb), pl.multiple_of((nb - 1 - i) * rb, rb))
        mild = (ok_scr[0, i] + ok_scr[1, nb - 1 - i]) == 2

        @pl.when(mild)
        def _():
            for d in range(2):
                fast_block(d, starts[d])

        @pl.when(jnp.logical_not(mild))
        def _():
            for hh in range(hp):
                for d in range(2):
                    for j in (range(cpb - 1, -1, -1) if d == 1 else range(cpb)):
                        exact_chunk(hh, d, pl.multiple_of(starts[d] + j * c, c))

        return carry

    lax.fori_loop(0, nb, block, 0)

    def finish(i, carry):
        rows = pl.ds(pl.multiple_of(i * rb, rb), rb)
        for hh in range(hp):
            o = od_scr[hh, 0, rows, :] + od_scr[hh, 1, rows, :]
            o = o * lax.rsqrt(jnp.mean(o * o, axis=-1, keepdims=True) + RMS_EPS) * ng_ref[...]
            hg = hg_ref[hh, 0, rows, :].astype(F32)
            o_ref[hh, 0, rows, :] = (o * (hg * jax.nn.sigmoid(hg))).astype(o_ref.dtype)
        return carry

    lax.fori_loop(0, nb, finish, 0)
    if want_state:
        for hh in range(hp):
            for d in range(2):
                sfin_ref[0, d, hh] = st_scr[hh, d].T


def _hgrn(plain, g, norm_g, s0, want_state):
    _, bsz, t, _ = plain.shape
    hp = HGRN_SHORT_HEADS if t <= HGRN_ROWS else HGRN_LONG_HEADS
    stream = lambda s: pl.BlockSpec((hp, 1, t, DK_H), lambda b, h: (s * N_HEADS_H // hp + h, b, 0, 0))
    in_specs = [stream(0), stream(1), stream(2), stream(0), stream(1),
                pl.BlockSpec((1, DV_H), lambda b, h: (0, 0))]
    args = [plain, plain, plain, g, g, norm_g]
    state_spec = pl.BlockSpec((1, 2, hp, DK_H, DV_H), lambda b, h: (b, 0, h, 0, 0))
    if s0 is not None:
        in_specs.append(state_spec)
        args.append(s0)
    out_specs = [pl.BlockSpec((hp, 1, t, DV_H), lambda b, h: (h, b, 0, 0))]
    out_shape = [jax.ShapeDtypeStruct((N_HEADS_H, bsz, t, DV_H), BF16)]
    if want_state:
        out_specs.append(state_spec)
        out_shape.append(jax.ShapeDtypeStruct((bsz, 2, N_HEADS_H, DK_H, DV_H), F32))
    n_blocks = t // min(HGRN_ROWS, t)
    chunk = pltpu.VMEM((HGRN_C, DK_H), F32)
    outs = pl.pallas_call(
        functools.partial(_hgrn_kernel, seq_len=t, has_s0=s0 is not None, want_state=want_state),
        grid=(bsz, N_HEADS_H // hp),
        in_specs=in_specs,
        out_specs=out_specs,
        out_shape=out_shape,
        scratch_shapes=[
            pltpu.VMEM((hp, 2, t, DV_H), F32),
            pltpu.VMEM((hp, 2, DV_H, DK_H), F32),
            chunk, chunk, chunk,
            pltpu.SMEM((2, n_blocks), jnp.int32),
        ],
        compiler_params=_params(("parallel", "parallel")),
        name="hgrn",
    )(*args)
    return outs if want_state else (outs[0], None)


def _merge_ffn2_kernel(oa0_ref, oh0_ref, oa1_ref, oh1_ref, h_ref, x_ref, mod_ref,
                       wmix_hbm, wa_hbm, wh_hbm, wo_hbm, win_hbm, wout_hbm, lng_ref, lnb_ref, y0_ref, y1_ref,
                       wg_scr, wa_scr, wh_scr, wo_scr,
                       win_scr, wout_scr, wide_stage, square_stage, wide_sems, square_sems, act_scr, *, tiles_0):
    i = pl.program_id(0)
    first_group = i < tiles_0

    @pl.when(i == 0)
    def _():
        _load_weight(win_hbm, win_scr, wide_stage, wide_sems)
        for half in range(2):
            _load_weight(wmix_hbm, wg_scr, square_stage, square_sems,
                         src_col=W_GA0 * MIX_TN + half * D_MODEL, dst_col=half * D_MODEL)
        for w_hbm, w_scr in ((wa_hbm, wa_scr), (wh_hbm, wh_scr), (wo_hbm, wo_scr), (wout_hbm, wout_scr)):
            _load_weight(w_hbm, w_scr, square_stage, square_sems)

    def heads(ref0, ref1):
        pick = lambda hh: jnp.where(first_group, ref0[hh], ref1[hh])
        return jnp.concatenate([pick(hh) for hh in range(ref0.shape[0])], axis=-1)

    gates = jax.nn.sigmoid(jnp.dot(h_ref[...], wg_scr[...], preferred_element_type=F32))
    ba = jnp.dot(heads(oa0_ref, oa1_ref), wa_scr[...], preferred_element_type=F32)
    bh = jnp.dot(heads(oh0_ref, oh1_ref), wh_scr[...], preferred_element_type=F32)
    merged = gates[:, :D_MODEL] * ba + gates[:, D_MODEL:] * bh
    mix = jnp.dot(merged.astype(BF16), wo_scr[...], preferred_element_type=F32)
    x2 = _layer_norm(ALPHA * x_ref[...] + mod_ref[0, 5:6, :] * mix, lng_ref[1:2, :], lnb_ref[1:2, :])
    y = _half_step(x2, mod_ref, win_scr, wout_scr, lng_ref, lnb_ref, act_scr, 2)

    @pl.when(first_group)
    def _():
        y0_ref[...] = y

    @pl.when(jnp.logical_not(first_group))
    def _():
        y1_ref[...] = y


def _merge_ffn2(mixed, h, x, mod, cond_of_tile, w_mix, w_a, w_h, w_o, w_in, w_out, ln_g, ln_b):
    (oa0, oh0), (oa1, oh1) = mixed
    tiles_0, tiles_1 = oa0.shape[1] // FFN_TM, oa1.shape[1] // FFN_TM
    row = lambda i: (i, 0)
    const = lambda i: (0, 0)
    in_group_0 = lambda i: jnp.minimum(i, tiles_0 - 1)
    in_group_1 = lambda i: jnp.maximum(i - tiles_0, 0)
    hbm = pl.BlockSpec(memory_space=pl.ANY)
    return pl.pallas_call(
        functools.partial(_merge_ffn2_kernel, tiles_0=tiles_0),
        grid=(tiles_0 + tiles_1,),
        in_specs=[
            pl.BlockSpec((N_HEADS_A, FFN_TM, DV_A), lambda i: (0, in_group_0(i), 0)),
            pl.BlockSpec((N_HEADS_H, FFN_TM, DV_H), lambda i: (0, in_group_0(i), 0)),
            pl.BlockSpec((N_HEADS_A, FFN_TM, DV_A), lambda i: (0, in_group_1(i), 0)),
            pl.BlockSpec((N_HEADS_H, FFN_TM, DV_H), lambda i: (0, in_group_1(i), 0)),
            pl.BlockSpec((FFN_TM, D_MODEL), row),
            pl.BlockSpec((FFN_TM, D_MODEL), row),
            pl.BlockSpec((1, 9, D_MODEL), lambda i: (cond_of_tile(i, FFN_TM), 0, 0)),
            hbm, hbm, hbm, hbm, hbm, hbm,
            pl.BlockSpec((3, D_MODEL), const),
            pl.BlockSpec((3, D_MODEL), const),
        ],
        out_specs=[pl.BlockSpec((FFN_TM, D_MODEL), lambda i: (in_group_0(i), 0)),
                   pl.BlockSpec((FFN_TM, D_MODEL), lambda i: (in_group_1(i), 0))],
        out_shape=[jax.ShapeDtypeStruct((tiles_0 * FFN_TM, D_MODEL), F32),
                   jax.ShapeDtypeStruct((tiles_1 * FFN_TM, D_MODEL), F32)],
        scratch_shapes=[pltpu.VMEM((D_MODEL, 2 * D_MODEL), BF16), pltpu.VMEM((W_A, D_MODEL), BF16),
                        pltpu.VMEM((W_H, D_MODEL), BF16), pltpu.VMEM((D_MODEL, D_MODEL), BF16)]
                       + _ffn_weight_scratch() + [pltpu.VMEM((FFN_TM, D_FF), BF16)],
        compiler_params=_params(("arbitrary",)),
        name="merge_ffn2",
    )(oa0, oh0, oa1, oh1, h, x, mod, w_mix, w_a, w_h, w_o, w_in, w_out, ln_g, ln_b)


def _rope_tables(n_tok):
    rows = n_tok // GRID_W
    row = np.repeat(np.arange(rows, dtype=np.float32), GRID_W)
    col = np.tile(np.arange(GRID_W, dtype=np.float32), rows)
    half = DH_A // 2
    inv = (ROPE_BASE ** (-np.arange(0, half, 2, dtype=np.float32) / half)).astype(np.float32)
    ar = row[:, None] * inv
    ac = col[:, None] * inv
    ang = np.concatenate([ar, ar, ac, ac] * 2, axis=-1)
    cos, sin = np.cos(ang).astype(np.float32), np.sin(ang).astype(np.float32)
    scale = np.float32(Q_SCALE)
    return (jnp.asarray(np.stack([cos * scale, cos, np.ones_like(cos)])),
            jnp.asarray(np.stack([sin * scale, sin, np.zeros_like(sin)])))


def _mixer(h1, group, wts, lam_init, bsz, ctx):
    m = MIX_TM
    t = m // bsz
    latent = ctx is not None
    per_head = lambda a: a.reshape(a.shape[0], bsz, t, a.shape[-1])
    if latent:
        qkv, plain, g = _project(h1, group, wts['w_mix_in'], wts['lb_logits'], _rope_tables(t), False)
        k = v = None
        ctx_kv = (ctx[0], ctx[1])
    else:
        assert t == PROJ_SUB
        qkv, k, v, plain, g = _project(h1, group, wts['w_mix_in'], wts['lb_logits'], None, True)
        ctx_kv = None
    qkv = per_head(qkv)
    q_kv = [(qkv, W_Q * N_HEADS_A), (qkv, W_K * N_HEADS_A), (qkv, W_V * N_HEADS_A)]
    o_a = _attention(*q_kv, ctx_kv, wts['lam_params'], wts['attn_subln_g'], lam_init)
    o_h, s_fin = _hgrn(per_head(plain), per_head(g), wts['hgrn_norm_g'],
                       ctx[2] if latent else None, not latent)
    return (o_a.reshape(N_HEADS_A, m, DV_A), o_h.reshape(N_HEADS_H, m, DV_H)), (k, v, s_fin)


def kernel(x_prompt, x_sample, cache_k, cache_v, state_hgrn, c, c_ctx, w_ada, b_ada, ffn1_w_in, ffn1_w_out,
           w_mix_in, lambda_q1, lambda_k1, lambda_q2, lambda_k2, attn_subln_g, hgrn_lb_logits, hgrn_norm_g,
           w_branch_a, w_branch_h, w_mix_out, ffn2_w_in, ffn2_w_out, ln_g, ln_b):
    batch, seq, _ = x_prompt.shape
    dec_batch, dec_seq, _ = x_sample.shape
    past = cache_k.shape[2]
    assert DEPTH == 1 and dec_batch + 1 <= MOD_ROWS
    lam_init = 0.8 - 0.6 * math.exp(-0.3 * 0)

    cond = jnp.concatenate([c_ctx[None, :], c, jnp.zeros((MOD_ROWS - 1 - dec_batch, D_MODEL), F32)], axis=0)
    mod = _modulation(cond, w_ada[0], b_ada[0])

    wts = {
        'w_mix_in': w_mix_in[0],
        'lam_params': jnp.concatenate([lambda_q1, lambda_k1, lambda_q2, lambda_k2], axis=0),
        'attn_subln_g': attn_subln_g, 'lb_logits': hgrn_lb_logits, 'hgrn_norm_g': hgrn_norm_g,
    }
    n_ctx = batch * seq
    assert n_ctx == MIX_TM and dec_batch * dec_seq == MIX_TM

    def cond_of_tile(i, tm):
        first = i * tm
        return jnp.where(first < n_ctx, 0, 1 + (first - n_ctx) // dec_seq)

    x1, h1 = _ffn1(x_prompt.reshape(n_ctx, D_MODEL), x_sample.reshape(dec_batch * dec_seq, D_MODEL), mod,
                   cond_of_tile, ffn1_w_in[0], ffn1_w_out[0], ln_g[0], ln_b[0])
    mixed_p, (k_c, v_c, s_c) = _mixer(h1, 0, wts, lam_init, batch, None)
    ctx = (cache_k[:, 0].reshape(dec_batch, past, QK_A), cache_v[:, 0].reshape(dec_batch, past, W_A),
           state_hgrn[:, 0])
    mixed_s, _ = _mixer(h1, 1, wts, lam_init, dec_batch, ctx)
    y_p, y_s = _merge_ffn2((mixed_p, mixed_s), h1, x1, mod, cond_of_tile, w_mix_in[0],
                           w_branch_a[0], w_branch_h[0], w_mix_out[0], ffn2_w_in[0], ffn2_w_out[0],
                           ln_g[0], ln_b[0])

    return (y_p.reshape(batch, seq, D_MODEL),
            y_s.reshape(dec_batch, dec_seq, D_MODEL),
            k_c.reshape(batch, DEPTH, N_HEADS_A, 2, DH_A, seq).transpose(0, 1, 5, 2, 3, 4),
            v_c.reshape(batch, DEPTH, seq, N_HEADS_A, DV_A),
            s_c.reshape(batch, DEPTH, 2, N_HEADS_H, DK_H, DV_H))
```
